```python
import jax, jax.numpy as jnp
from jax import lax
import numpy as np

D_MODEL = 1024
BATCH = 8
SEQ = 16384
DEPTH = 1

HEAD_DIM = 128
DIL_PAIRS = ((128, 1), (512, 4), (2048, 16))
A_HEADS_PER_GROUP = 2
A_HEADS = A_HEADS_PER_GROUP * len(DIL_PAIRS)
B_Q_HEADS = 4
B_KV_HEADS = 2
B_WINDOW = 128
M_HEADS = 4
N_MEM = 256
D_FF = 2816
ROPE_THETA = 10000.0
BLOCK = 128
EPS = 1e-6
NEG_INF = -1e30

A_WIDTH = A_HEADS * HEAD_DIM
A_OUT = A_HEADS_PER_GROUP * HEAD_DIM
B_WIDTH = B_Q_HEADS * HEAD_DIM
B_KV_WIDTH = B_KV_HEADS * HEAD_DIM
M_WIDTH = M_HEADS * HEAD_DIM
D_IN = 3 * A_WIDTH + B_WIDTH + 2 * B_KV_WIDTH + M_WIDTH
IN_SPLITS = tuple(np.cumsum([A_WIDTH, A_WIDTH, A_WIDTH, B_WIDTH, B_KV_WIDTH, B_KV_WIDTH]).tolist())

kernel_name = "hybrid_dilated_swa_sink_memory_macaron"


def rms_norm(x, g):
    xf = x.astype(jnp.float32)
    y = xf * lax.rsqrt(jnp.mean(xf * xf, axis=-1, keepdims=True) + EPS)
    return (y * g.astype(jnp.float32)).astype(x.dtype)


def swiglu(x, w_in, w_out):
    gate, up = jnp.split(x @ w_in, 2, axis=-1)
    return (jax.nn.silu(gate) * up) @ w_out


def rope(x, pos):
    half = HEAD_DIM // 2
    inv = ROPE_THETA ** (-jnp.arange(half, dtype=jnp.float32) / half)
    ang = pos.astype(jnp.float32)[:, None] * inv[None, :]
    cos = jnp.cos(ang)[None, :, None, :]
    sin = jnp.sin(ang)[None, :, None, :]
    x1 = x[..., :half].astype(jnp.float32)
    x2 = x[..., half:].astype(jnp.float32)
    return jnp.concatenate([x1 * cos - x2 * sin, x2 * cos + x1 * sin], axis=-1).astype(x.dtype)


def banded_attention(q, k, v, max_dist, sink=None):
    b, L, hq, d = q.shape
    hkv = k.shape[2]
    grp = hq // hkv
    blk = min(BLOCK, L)
    nb = -(-L // blk)
    pad = nb * blk - L
    if pad:
        cfg = ((0, 0), (0, pad), (0, 0), (0, 0))
        q, k, v = jnp.pad(q, cfg), jnp.pad(k, cfg), jnp.pad(v, cfg)
    qb = q.reshape(b, nb, blk, hkv, grp, d)
    kb = k.reshape(b, nb, blk, hkv, d)
    vb = v.reshape(b, nb, blk, hkv, d)
    shift = ((0, 0), (1, 0), (0, 0), (0, 0), (0, 0))
    kk = jnp.concatenate([jnp.pad(kb, shift)[:, :-1], kb], axis=2)
    vv = jnp.concatenate([jnp.pad(vb, shift)[:, :-1], vb], axis=2)
    s = jnp.einsum("bnqhgd,bnkhd->bnhgqk", qb, kk).astype(jnp.float32) * (d ** -0.5)
    qpos = jnp.arange(blk)[:, None] + blk
    kpos = jnp.arange(2 * blk)[None, :]
    dist = qpos - kpos
    band = (dist >= 0) & (dist <= max_dist)
    has_prev = (jnp.arange(nb) > 0)[:, None, None] | (kpos >= blk)[None]
    mask = band[None] & has_prev
    s = jnp.where(mask[None, :, None, None], s, NEG_INF)
    m = jnp.max(s, axis=-1)
    if sink is not None:
        sk = sink.astype(jnp.float32).reshape(1, 1, hkv, grp, 1)
        m = jnp.maximum(m, sk)
    p = jnp.exp(s - m[..., None])
    den = jnp.sum(p, axis=-1)
    tot = den + jnp.exp(sk - m) if sink is not None else den
    o = jnp.einsum("bnhgqk,bnkhd->bnqhgd", p.astype(v.dtype), vv).astype(jnp.float32)
    o = o / jnp.moveaxis(tot, -1, 2)[..., None]
    o = o.astype(q.dtype).reshape(b, nb * blk, hq, d)[:, :L]
    lse = jnp.moveaxis(m + jnp.log(den), -1, 2).reshape(b, nb * blk, hq)[:, :L]
    return o, lse


def dilated_group(q, k, v, window, dilation):
    b, s, h, d = q.shape
    L = s // dilation

    def to_sub(t):
        return t.reshape(b, L, dilation, h, d).transpose(0, 2, 1, 3, 4).reshape(b * dilation, L, h, d)

    o, lse = banded_attention(to_sub(q), to_sub(k), to_sub(v), window // dilation)
    o = o.reshape(b, dilation, L, h, d).transpose(0, 2, 1, 3, 4).reshape(b, s, h, d)
    lse = lse.reshape(b, dilation, L, h).transpose(0, 2, 1, 3).reshape(b, s, h)
    return o, lse


def memory_attention(q, mem_n, w_mem_kv):
    b, n, _ = mem_n.shape
    mk, mv = jnp.split(mem_n @ w_mem_kv, 2, axis=-1)
    mk = mk.reshape(b, n, M_HEADS, HEAD_DIM)
    mv = mv.reshape(b, n, M_HEADS, HEAD_DIM)
    s = jnp.einsum("bshd,bmhd->bhsm", q, mk).astype(jnp.float32) * (HEAD_DIM ** -0.5)
    p = jax.nn.softmax(s, axis=-1)
    return jnp.einsum("bhsm,bmhd->bshd", p.astype(mv.dtype), mv)


def _fwd_setup_inputs(seed: int = 0) -> dict:
    key = jax.random.key(seed)
    ks = jax.random.split(key, 24)
    f = jnp.float32

    def w(k, shape, fan_in):
        return jax.random.normal(k, (DEPTH,) + shape, f) * fan_in ** -0.5

    def gain(k):
        return 1.0 + 0.1 * jax.random.normal(k, (DEPTH, D_MODEL), f)

    return {
        "x": jax.random.normal(ks[0], (BATCH, SEQ, D_MODEL), f),
        "mem": jax.random.normal(ks[1], (BATCH, N_MEM, D_MODEL), f),
        "ffn1_norm_pre": gain(ks[2]),
        "ffn1_w_in": w(ks[3], (D_MODEL, 2 * D_FF), D_MODEL),
        "ffn1_w_out": w(ks[4], (D_FF, D_MODEL), D_FF),
        "ffn1_norm_post": gain(ks[5]),
        "mix_norm_pre": gain(ks[6]),
        "w_in": w(ks[7], (D_MODEL, D_IN), D_MODEL),
        "sinks": 0.5 * jax.random.normal(ks[8], (DEPTH, B_Q_HEADS), f),
        "mem_norm": gain(ks[9]),
        "w_mem_kv": w(ks[10], (D_MODEL, 2 * M_WIDTH), D_MODEL),
        "w_gate": w(ks[11], (D_MODEL, 3 * D_MODEL), D_MODEL),
        "b_gate": 0.01 * jax.random.normal(ks[12], (DEPTH, 3 * D_MODEL), f),
        "w_o_a": w(ks[13], (A_OUT, D_MODEL), A_OUT),
        "w_o_b": w(ks[14], (B_WIDTH, D_MODEL), B_WIDTH),
        "w_o_m": w(ks[15], (M_WIDTH, D_MODEL), M_WIDTH),
        "w_out": w(ks[16], (D_MODEL, D_MODEL), D_MODEL),
        "mix_norm_post": gain(ks[17]),
        "ffn2_norm_pre": gain(ks[18]),
        "ffn2_w_in": w(ks[19], (D_MODEL, 2 * D_FF), D_MODEL),
        "ffn2_w_out": w(ks[20], (D_FF, D_MODEL), D_FF),
        "ffn2_norm_post": gain(ks[21]),
    }


def _fwd_reference(x, mem, ffn1_norm_pre, ffn1_w_in, ffn1_w_out, ffn1_norm_post, mix_norm_pre,
              w_in, sinks, mem_norm, w_mem_kv, w_gate, b_gate, w_o_a, w_o_b, w_o_m, w_out,
              mix_norm_post, ffn2_norm_pre, ffn2_w_in, ffn2_w_out, ffn2_norm_post):
    b, s, _ = x.shape
    pos = jnp.arange(s)
    h = x
    for l in range(DEPTH):
        f1 = swiglu(rms_norm(h, ffn1_norm_pre[l]), ffn1_w_in[l], ffn1_w_out[l])
        h = h + 0.5 * rms_norm(f1, ffn1_norm_post[l])

        u = rms_norm(h, mix_norm_pre[l])
        aq, ak, av, bq, bk, bv, mq = jnp.split(u @ w_in[l], IN_SPLITS, axis=-1)

        aq = rope(aq.reshape(b, s, A_HEADS, HEAD_DIM), pos).reshape(b, s, len(DIL_PAIRS), A_HEADS_PER_GROUP, HEAD_DIM)
        ak = rope(ak.reshape(b, s, A_HEADS, HEAD_DIM), pos).reshape(b, s, len(DIL_PAIRS), A_HEADS_PER_GROUP, HEAD_DIM)
        av = av.reshape(b, s, len(DIL_PAIRS), A_HEADS_PER_GROUP, HEAD_DIM)
        outs, lses = [], []
        for g, (window, dilation) in enumerate(DIL_PAIRS):
            o_g, lse_g = dilated_group(aq[:, :, g], ak[:, :, g], av[:, :, g], window, dilation)
            outs.append(o_g)
            lses.append(lse_g)
        wts = jax.nn.softmax(jnp.stack(lses, axis=0), axis=0)
        o_a = jnp.sum(wts[..., None] * jnp.stack(outs, axis=0).astype(jnp.float32), axis=0)
        o_a = o_a.astype(x.dtype).reshape(b, s, A_OUT)

        bq = rope(bq.reshape(b, s, B_Q_HEADS, HEAD_DIM), pos)
        bk = rope(bk.reshape(b, s, B_KV_HEADS, HEAD_DIM), pos)
        bv = bv.reshape(b, s, B_KV_HEADS, HEAD_DIM)
        o_b, _ = banded_attention(bq, bk, bv, B_WINDOW - 1, sink=sinks[l])
        o_b = o_b.reshape(b, s, B_WIDTH)

        o_m = memory_attention(mq.reshape(b, s, M_HEADS, HEAD_DIM), rms_norm(mem, mem_norm[l]), w_mem_kv[l])
        o_m = o_m.reshape(b, s, M_WIDTH)

        g_a, g_b, g_m = jnp.split(jax.nn.sigmoid(u @ w_gate[l] + b_gate[l]), 3, axis=-1)
        merged = g_a * (o_a @ w_o_a[l]) + g_b * (o_b @ w_o_b[l]) + g_m * (o_m @ w_o_m[l])
        h = h + rms_norm(merged @ w_out[l], mix_norm_post[l])

        f2 = swiglu(rms_norm(h, ffn2_norm_pre[l]), ffn2_w_in[l], ffn2_w_out[l])
        h = h + 0.5 * rms_norm(f2, ffn2_norm_post[l])
    return h


import jax as _jax
import jax.numpy as _jnp

TWIN_FORMAT = 'train_step'
FWD_PARAMS = ['x', 'mem', 'ffn1_norm_pre', 'ffn1_w_in', 'ffn1_w_out', 'ffn1_norm_post', 'mix_norm_pre', 'w_in', 'sinks', 'mem_norm', 'w_mem_kv', 'w_gate', 'b_gate', 'w_o_a', 'w_o_b', 'w_o_m', 'w_out', 'mix_norm_post', 'ffn2_norm_pre', 'ffn2_w_in', 'ffn2_w_out', 'ffn2_norm_post']
TWIN_WEIGHTS = ['ffn1_norm_pre', 'ffn1_w_in', 'ffn1_w_out', 'ffn1_norm_post', 'mix_norm_pre', 'w_in', 'sinks', 'mem_norm', 'w_mem_kv', 'w_gate', 'b_gate', 'w_o_a', 'w_o_b', 'w_o_m', 'w_out', 'mix_norm_post', 'ffn2_norm_pre', 'ffn2_w_in', 'ffn2_w_out', 'ffn2_norm_post']
TWIN_DIFF_INPUT = 'x'
TWIN_INPUTS = ['x', 'mem', 'ffn1_norm_pre', 'ffn1_w_in', 'ffn1_w_out', 'ffn1_norm_post', 'mix_norm_pre', 'w_in', 'sinks', 'mem_norm', 'w_mem_kv', 'w_gate', 'b_gate', 'w_o_a', 'w_o_b', 'w_o_m', 'w_out', 'mix_norm_post', 'ffn2_norm_pre', 'ffn2_w_in', 'ffn2_w_out', 'ffn2_norm_post', 'loss_target', 'm_ffn1_norm_pre', 'm_ffn1_w_in', 'm_ffn1_w_out', 'm_ffn1_norm_post', 'm_mix_norm_pre', 'm_w_in', 'm_sinks', 'm_mem_norm', 'm_w_mem_kv', 'm_w_gate', 'm_b_gate', 'm_w_o_a', 'm_w_o_b', 'm_w_o_m', 'm_w_out', 'm_mix_norm_post', 'm_ffn2_norm_pre', 'm_ffn2_w_in', 'm_ffn2_w_out', 'm_ffn2_norm_post', 'v_ffn1_norm_pre', 'v_ffn1_w_in', 'v_ffn1_w_out', 'v_ffn1_norm_post', 'v_mix_norm_pre', 'v_w_in', 'v_sinks', 'v_mem_norm', 'v_w_mem_kv', 'v_w_gate', 'v_b_gate', 'v_w_o_a', 'v_w_o_b', 'v_w_o_m', 'v_w_out', 'v_mix_norm_post', 'v_ffn2_norm_pre', 'v_ffn2_w_in', 'v_ffn2_w_out', 'v_ffn2_norm_post']
TWIN_OUTPUTS = ['loss', 'grad_x', 'grad_ffn1_norm_pre', 'grad_ffn1_w_in', 'grad_ffn1_w_out', 'grad_ffn1_norm_post', 'grad_mix_norm_pre', 'grad_w_in', 'grad_sinks', 'grad_mem_norm', 'grad_w_mem_kv', 'grad_w_gate', 'grad_b_gate', 'grad_w_o_a', 'grad_w_o_b', 'grad_w_o_m', 'grad_w_out', 'grad_mix_norm_post', 'grad_ffn2_norm_pre', 'grad_ffn2_w_in', 'grad_ffn2_w_out', 'grad_ffn2_norm_post', 'delta_ffn1_norm_pre', 'delta_ffn1_w_in', 'delta_ffn1_w_out', 'delta_ffn1_norm_post', 'delta_mix_norm_pre', 'delta_w_in', 'delta_sinks', 'delta_mem_norm', 'delta_w_mem_kv', 'delta_w_gate', 'delta_b_gate', 'delta_w_o_a', 'delta_w_o_b', 'delta_w_o_m', 'delta_w_out', 'delta_mix_norm_post', 'delta_ffn2_norm_pre', 'delta_ffn2_w_in', 'delta_ffn2_w_out', 'delta_ffn2_norm_post', 'new_m_ffn1_norm_pre', 'new_m_ffn1_w_in', 'new_m_ffn1_w_out', 'new_m_ffn1_norm_post', 'new_m_mix_norm_pre', 'new_m_w_in', 'new_m_sinks', 'new_m_mem_norm', 'new_m_w_mem_kv', 'new_m_w_gate', 'new_m_b_gate', 'new_m_w_o_a', 'new_m_w_o_b', 'new_m_w_o_m', 'new_m_w_out', 'new_m_mix_norm_post', 'new_m_ffn2_norm_pre', 'new_m_ffn2_w_in', 'new_m_ffn2_w_out', 'new_m_ffn2_norm_post', 'new_v_ffn1_norm_pre', 'new_v_ffn1_w_in', 'new_v_ffn1_w_out', 'new_v_ffn1_norm_post', 'new_v_mix_norm_pre', 'new_v_w_in', 'new_v_sinks', 'new_v_mem_norm', 'new_v_w_mem_kv', 'new_v_w_gate', 'new_v_b_gate', 'new_v_w_o_a', 'new_v_w_o_b', 'new_v_w_o_m', 'new_v_w_out', 'new_v_mix_norm_post', 'new_v_ffn2_norm_pre', 'new_v_ffn2_w_in', 'new_v_ffn2_w_out', 'new_v_ffn2_norm_post']
TWIN_LEAF_KINDS = {'loss': 'loss', 'grad_x': 'grad_x', 'grad_ffn1_norm_pre': 'grad_w', 'grad_ffn1_w_in': 'grad_w', 'grad_ffn1_w_out': 'grad_w', 'grad_ffn1_norm_post': 'grad_w', 'grad_mix_norm_pre': 'grad_w', 'grad_w_in': 'grad_w', 'grad_sinks': 'grad_w', 'grad_mem_norm': 'grad_w', 'grad_w_mem_kv': 'grad_w', 'grad_w_gate': 'grad_w', 'grad_b_gate': 'grad_w', 'grad_w_o_a': 'grad_w', 'grad_w_o_b': 'grad_w', 'grad_w_o_m': 'grad_w', 'grad_w_out': 'grad_w', 'grad_mix_norm_post': 'grad_w', 'grad_ffn2_norm_pre': 'grad_w', 'grad_ffn2_w_in': 'grad_w', 'grad_ffn2_w_out': 'grad_w', 'grad_ffn2_norm_post': 'grad_w', 'delta_ffn1_norm_pre': 'delta_w', 'delta_ffn1_w_in': 'delta_w', 'delta_ffn1_w_out': 'delta_w', 'delta_ffn1_norm_post': 'delta_w', 'delta_mix_norm_pre': 'delta_w', 'delta_w_in': 'delta_w', 'delta_sinks': 'delta_w', 'delta_mem_norm': 'delta_w', 'delta_w_mem_kv': 'delta_w', 'delta_w_gate': 'delta_w', 'delta_b_gate': 'delta_w', 'delta_w_o_a': 'delta_w', 'delta_w_o_b': 'delta_w', 'delta_w_o_m': 'delta_w', 'delta_w_out': 'delta_w', 'delta_mix_norm_post': 'delta_w', 'delta_ffn2_norm_pre': 'delta_w', 'delta_ffn2_w_in': 'delta_w', 'delta_ffn2_w_out': 'delta_w', 'delta_ffn2_norm_post': 'delta_w', 'new_m_ffn1_norm_pre': 'new_m', 'new_m_ffn1_w_in': 'new_m', 'new_m_ffn1_w_out': 'new_m', 'new_m_ffn1_norm_post': 'new_m', 'new_m_mix_norm_pre': 'new_m', 'new_m_w_in': 'new_m', 'new_m_sinks': 'new_m', 'new_m_mem_norm': 'new_m', 'new_m_w_mem_kv': 'new_m', 'new_m_w_gate': 'new_m', 'new_m_b_gate': 'new_m', 'new_m_w_o_a': 'new_m', 'new_m_w_o_b': 'new_m', 'new_m_w_o_m': 'new_m', 'new_m_w_out': 'new_m', 'new_m_mix_norm_post': 'new_m', 'new_m_ffn2_norm_pre': 'new_m', 'new_m_ffn2_w_in': 'new_m', 'new_m_ffn2_w_out': 'new_m', 'new_m_ffn2_norm_post': 'new_m', 'new_v_ffn1_norm_pre': 'new_v', 'new_v_ffn1_w_in': 'new_v', 'new_v_ffn1_w_out': 'new_v', 'new_v_ffn1_norm_post': 'new_v', 'new_v_mix_norm_pre': 'new_v', 'new_v_w_in': 'new_v', 'new_v_sinks': 'new_v', 'new_v_mem_norm': 'new_v', 'new_v_w_mem_kv': 'new_v', 'new_v_w_gate': 'new_v', 'new_v_b_gate': 'new_v', 'new_v_w_o_a': 'new_v', 'new_v_w_o_b': 'new_v', 'new_v_w_o_m': 'new_v', 'new_v_w_out': 'new_v', 'new_v_mix_norm_post': 'new_v', 'new_v_ffn2_norm_pre': 'new_v', 'new_v_ffn2_w_in': 'new_v', 'new_v_ffn2_w_out': 'new_v', 'new_v_ffn2_norm_post': 'new_v'}


def _forward(args):
    return _fwd_reference(*[args[k] for k in FWD_PARAMS])


def _output_shape():
    def fwd():
        inp = _fwd_setup_inputs(0)
        return _fwd_reference(*[inp[k] for k in FWD_PARAMS])
    out = _jax.eval_shape(fwd)
    return out.shape, out.dtype

N_MICROBATCH = 1
ADAM_LR = 0.001
ADAM_B1 = 0.9
ADAM_B2 = 0.999
ADAM_EPS = 1e-08
ADAM_WD = 0.01
ADAM_STEP = 10
PER_EXAMPLE_BATCH_AXIS = {'x': 0, 'mem': 0, 'loss_target': 0}
SHARED_INPUTS = []
_WEIGHT_DTYPES = {'ffn1_norm_pre': _jnp.float32, 'ffn1_w_in': _jnp.float32, 'ffn1_w_out': _jnp.float32, 'ffn1_norm_post': _jnp.float32, 'mix_norm_pre': _jnp.float32, 'w_in': _jnp.float32, 'sinks': _jnp.float32, 'mem_norm': _jnp.float32, 'w_mem_kv': _jnp.float32, 'w_gate': _jnp.float32, 'b_gate': _jnp.float32, 'w_o_a': _jnp.float32, 'w_o_b': _jnp.float32, 'w_o_m': _jnp.float32, 'w_out': _jnp.float32, 'mix_norm_post': _jnp.float32, 'ffn2_norm_pre': _jnp.float32, 'ffn2_w_in': _jnp.float32, 'ffn2_w_out': _jnp.float32, 'ffn2_norm_post': _jnp.float32}
MOMENT_SCALE = {'ffn1_norm_pre': 9.679613e-01, 'ffn1_w_in': 3.961043e-01, 'ffn1_w_out': 6.952151e-01, 'ffn1_norm_post': 3.189098e+01, 'mix_norm_pre': 1.098462e+00, 'w_in': 6.009647e-01, 'sinks': 4.888912e-01, 'mem_norm': 1.504502e+00, 'w_mem_kv': 1.164367e+00, 'w_gate': 1.562084e-01, 'b_gate': 2.414556e-01, 'w_o_a': 4.341807e-01, 'w_o_b': 6.867754e-01, 'w_o_m': 1.000194e+00, 'w_out': 1.191114e+00, 'mix_norm_post': 1.277173e+02, 'ffn2_norm_pre': 9.733858e-01, 'ffn2_w_in': 3.667037e-01, 'ffn2_w_out': 8.045515e-01, 'ffn2_norm_post': 3.192818e+01}


def _to_microbatches(a, axis):
    t = _jnp.moveaxis(a, axis, 0)
    t = t.reshape((N_MICROBATCH, t.shape[0] // N_MICROBATCH) + t.shape[1:])
    return _jnp.moveaxis(t, 1, axis + 1)


def setup_inputs(seed: int = 0) -> dict:
    inp = _fwd_setup_inputs(seed)
    key = _jax.random.fold_in(_jax.random.key(seed), 7919)
    shape, _ = _output_shape()
    out = dict(inp)
    out["loss_target"] = _jax.random.normal(_jax.random.fold_in(key, 0), shape, _jnp.float32)
    for i, name in enumerate(TWIN_WEIGHTS):
        w = inp[name].astype(_jnp.float32)
        if MOMENT_SCALE is None:
            s = _jnp.sqrt(_jnp.mean(_jnp.square(w)) + 1e-30)
        else:
            s = MOMENT_SCALE[name]
        km, kv = _jax.random.split(_jax.random.fold_in(key, i + 1))
        out[name] = w
        out["m_" + name] = s * _jax.random.normal(km, w.shape, _jnp.float32)
        out["v_" + name] = (s * s) * _jax.random.uniform(kv, w.shape, _jnp.float32, 0.5, 1.5)
    if N_MICROBATCH > 1:
        for name, axis in PER_EXAMPLE_BATCH_AXIS.items():
            out[name] = _to_microbatches(out[name], axis)
    return {'x': out['x'], 'mem': out['mem'], 'ffn1_norm_pre': out['ffn1_norm_pre'], 'ffn1_w_in': out['ffn1_w_in'], 'ffn1_w_out': out['ffn1_w_out'], 'ffn1_norm_post': out['ffn1_norm_post'], 'mix_norm_pre': out['mix_norm_pre'], 'w_in': out['w_in'], 'sinks': out['sinks'], 'mem_norm': out['mem_norm'], 'w_mem_kv': out['w_mem_kv'], 'w_gate': out['w_gate'], 'b_gate': out['b_gate'], 'w_o_a': out['w_o_a'], 'w_o_b': out['w_o_b'], 'w_o_m': out['w_o_m'], 'w_out': out['w_out'], 'mix_norm_post': out['mix_norm_post'], 'ffn2_norm_pre': out['ffn2_norm_pre'], 'ffn2_w_in': out['ffn2_w_in'], 'ffn2_w_out': out['ffn2_w_out'], 'ffn2_norm_post': out['ffn2_norm_post'], 'loss_target': out['loss_target'], 'm_ffn1_norm_pre': out['m_ffn1_norm_pre'], 'm_ffn1_w_in': out['m_ffn1_w_in'], 'm_ffn1_w_out': out['m_ffn1_w_out'], 'm_ffn1_norm_post': out['m_ffn1_norm_post'], 'm_mix_norm_pre': out['m_mix_norm_pre'], 'm_w_in': out['m_w_in'], 'm_sinks': out['m_sinks'], 'm_mem_norm': out['m_mem_norm'], 'm_w_mem_kv': out['m_w_mem_kv'], 'm_w_gate': out['m_w_gate'], 'm_b_gate': out['m_b_gate'], 'm_w_o_a': out['m_w_o_a'], 'm_w_o_b': out['m_w_o_b'], 'm_w_o_m': out['m_w_o_m'], 'm_w_out': out['m_w_out'], 'm_mix_norm_post': out['m_mix_norm_post'], 'm_ffn2_norm_pre': out['m_ffn2_norm_pre'], 'm_ffn2_w_in': out['m_ffn2_w_in'], 'm_ffn2_w_out': out['m_ffn2_w_out'], 'm_ffn2_norm_post': out['m_ffn2_norm_post'], 'v_ffn1_norm_pre': out['v_ffn1_norm_pre'], 'v_ffn1_w_in': out['v_ffn1_w_in'], 'v_ffn1_w_out': out['v_ffn1_w_out'], 'v_ffn1_norm_post': out['v_ffn1_norm_post'], 'v_mix_norm_pre': out['v_mix_norm_pre'], 'v_w_in': out['v_w_in'], 'v_sinks': out['v_sinks'], 'v_mem_norm': out['v_mem_norm'], 'v_w_mem_kv': out['v_w_mem_kv'], 'v_w_gate': out['v_w_gate'], 'v_b_gate': out['v_b_gate'], 'v_w_o_a': out['v_w_o_a'], 'v_w_o_b': out['v_w_o_b'], 'v_w_o_m': out['v_w_o_m'], 'v_w_out': out['v_w_out'], 'v_mix_norm_post': out['v_mix_norm_post'], 'v_ffn2_norm_pre': out['v_ffn2_norm_pre'], 'v_ffn2_w_in': out['v_ffn2_w_in'], 'v_ffn2_w_out': out['v_ffn2_w_out'], 'v_ffn2_norm_post': out['v_ffn2_norm_post']}


def _loss(weights, diff, rest, loss_target):
    with _jax.named_scope("forward"):
        args = {**rest, TWIN_DIFF_INPUT: diff, **{k: w.astype(_WEIGHT_DTYPES[k]) for k, w in weights.items()}}
        y = _forward(args)
    with _jax.named_scope("loss_head"):
        err = _jnp.square(y.astype(_jnp.float32) - loss_target)
        return 0.5 * _jnp.sum(_jnp.mean(err, axis=-1)) if err.ndim else 0.5 * err


def _adamw(w, g, m, v):
    m = ADAM_B1 * m + (1.0 - ADAM_B1) * g
    v = ADAM_B2 * v + (1.0 - ADAM_B2) * _jnp.square(g)
    m_hat = m / (1.0 - ADAM_B1 ** ADAM_STEP)
    v_hat = v / (1.0 - ADAM_B2 ** ADAM_STEP)
    delta = -ADAM_LR * (m_hat / (_jnp.sqrt(v_hat) + ADAM_EPS) + ADAM_WD * w)
    return delta, m, v


def reference(x, mem, ffn1_norm_pre, ffn1_w_in, ffn1_w_out, ffn1_norm_post, mix_norm_pre, w_in, sinks, mem_norm, w_mem_kv, w_gate, b_gate, w_o_a, w_o_b, w_o_m, w_out, mix_norm_post, ffn2_norm_pre, ffn2_w_in, ffn2_w_out, ffn2_norm_post, loss_target, m_ffn1_norm_pre, m_ffn1_w_in, m_ffn1_w_out, m_ffn1_norm_post, m_mix_norm_pre, m_w_in, m_sinks, m_mem_norm, m_w_mem_kv, m_w_gate, m_b_gate, m_w_o_a, m_w_o_b, m_w_o_m, m_w_out, m_mix_norm_post, m_ffn2_norm_pre, m_ffn2_w_in, m_ffn2_w_out, m_ffn2_norm_post, v_ffn1_norm_pre, v_ffn1_w_in, v_ffn1_w_out, v_ffn1_norm_post, v_mix_norm_pre, v_w_in, v_sinks, v_mem_norm, v_w_mem_kv, v_w_gate, v_b_gate, v_w_o_a, v_w_o_b, v_w_o_m, v_w_out, v_mix_norm_post, v_ffn2_norm_pre, v_ffn2_w_in, v_ffn2_w_out, v_ffn2_norm_post):
    given = dict(x=x, mem=mem, ffn1_norm_pre=ffn1_norm_pre, ffn1_w_in=ffn1_w_in, ffn1_w_out=ffn1_w_out, ffn1_norm_post=ffn1_norm_post, mix_norm_pre=mix_norm_pre, w_in=w_in, sinks=sinks, mem_norm=mem_norm, w_mem_kv=w_mem_kv, w_gate=w_gate, b_gate=b_gate, w_o_a=w_o_a, w_o_b=w_o_b, w_o_m=w_o_m, w_out=w_out, mix_norm_post=mix_norm_post, ffn2_norm_pre=ffn2_norm_pre, ffn2_w_in=ffn2_w_in, ffn2_w_out=ffn2_w_out, ffn2_norm_post=ffn2_norm_post, loss_target=loss_target, m_ffn1_norm_pre=m_ffn1_norm_pre, m_ffn1_w_in=m_ffn1_w_in, m_ffn1_w_out=m_ffn1_w_out, m_ffn1_norm_post=m_ffn1_norm_post, m_mix_norm_pre=m_mix_norm_pre, m_w_in=m_w_in, m_sinks=m_sinks, m_mem_norm=m_mem_norm, m_w_mem_kv=m_w_mem_kv, m_w_gate=m_w_gate, m_b_gate=m_b_gate, m_w_o_a=m_w_o_a, m_w_o_b=m_w_o_b, m_w_o_m=m_w_o_m, m_w_out=m_w_out, m_mix_norm_post=m_mix_norm_post, m_ffn2_norm_pre=m_ffn2_norm_pre, m_ffn2_w_in=m_ffn2_w_in, m_ffn2_w_out=m_ffn2_w_out, m_ffn2_norm_post=m_ffn2_norm_post, v_ffn1_norm_pre=v_ffn1_norm_pre, v_ffn1_w_in=v_ffn1_w_in, v_ffn1_w_out=v_ffn1_w_out, v_ffn1_norm_post=v_ffn1_norm_post, v_mix_norm_pre=v_mix_norm_pre, v_w_in=v_w_in, v_sinks=v_sinks, v_mem_norm=v_mem_norm, v_w_mem_kv=v_w_mem_kv, v_w_gate=v_w_gate, v_b_gate=v_b_gate, v_w_o_a=v_w_o_a, v_w_o_b=v_w_o_b, v_w_o_m=v_w_o_m, v_w_out=v_w_out, v_mix_norm_post=v_mix_norm_post, v_ffn2_norm_pre=v_ffn2_norm_pre, v_ffn2_w_in=v_ffn2_w_in, v_ffn2_w_out=v_ffn2_w_out, v_ffn2_norm_post=v_ffn2_norm_post)
    weights = {n: given[n] for n in TWIN_WEIGHTS}
    shared = {n: given[n] for n in SHARED_INPUTS}
    per_example = {n: given[n] for n in ['x', 'mem']}
    grad_fn = _jax.value_and_grad(_loss, argnums=(0, 1))

    def one_microbatch(ex, loss_target):
        ex = dict(ex)
        diff = ex.pop(TWIN_DIFF_INPUT)
        return grad_fn(weights, diff, {**shared, **ex}, loss_target)

    if N_MICROBATCH == 1:
        loss, (grad_w, grad_x) = one_microbatch(per_example, given["loss_target"])
    else:
        def body(carry, xs):
            loss_sum, grad_sum = carry
            l_k, (gw_k, gx_k) = one_microbatch(xs[0], xs[1])
            with _jax.named_scope("update"):
                return (loss_sum + l_k, _jax.tree.map(_jnp.add, grad_sum, gw_k)), gx_k

        init = (_jnp.zeros((), _jnp.float32), _jax.tree.map(_jnp.zeros_like, weights))
        (loss, grad_w), grad_x = _jax.lax.scan(body, init, (per_example, given["loss_target"]))
    with _jax.named_scope("update"):
        delta_w, new_m, new_v = {}, {}, {}
        for n in TWIN_WEIGHTS:
            delta_w[n], new_m[n], new_v[n] = _adamw(weights[n], grad_w[n], given["m_" + n], given["v_" + n])
    return (loss, grad_x, *[grad_w[n] for n in TWIN_WEIGHTS], *[delta_w[n] for n in TWIN_WEIGHTS],
            *[new_m[n] for n in TWIN_WEIGHTS], *[new_v[n] for n in TWIN_WEIGHTS])
```

```python
import functools

import jax
import jax.numpy as jnp
from jax import lax
from jax.experimental import pallas as pl
from jax.experimental.pallas import tpu as pltpu

F32 = jnp.float32
BF16 = jnp.bfloat16

D_MODEL = 1024
D_FF = 2816
HEAD = 128
N_CHIPS = 4
N_DEV = 8
EPS = 1e-6
NEG_INF = -1e30
ROPE_THETA = 10000.0
ATT_SCALE = HEAD ** -0.5

ADAM_LR = 0.001
ADAM_B1 = 0.9
ADAM_B2 = 0.999
ADAM_EPS = 1e-08
ADAM_WD = 0.01
ADAM_STEP = 10

VMEM_LIMIT = 52 * 2 ** 20
MESH = pl.DeviceIdType.MESH

QKV_W = 3840
AQ, AK, AV, BQ, BK, BV, MQ = 0, 6, 12, 18, 22, 24, 26
DIL = ((128, 1), (512, 4), (2048, 16))

TM = 512
FF_T = D_FF // 2


def _params(*sem):
    return pltpu.CompilerParams(dimension_semantics=sem, vmem_limit_bytes=VMEM_LIMIT)


def _dot(a, b):
    return jnp.dot(a, b, preferred_element_type=F32)


def _dot_nt(a, b):
    return lax.dot_general(a, b, (((1,), (1,)), ((), ())), preferred_element_type=F32)


def _dot_tn(a, b):
    return lax.dot_general(a, b, (((0,), (0,)), ((), ())), preferred_element_type=F32)


def _rstd(x):
    return lax.rsqrt(jnp.mean(x * x, axis=-1, keepdims=True) + EPS)


def _ffn_perm(k):
    return (k % 2) * 2 + k // 2


def ffn_in(h, g, w_sm, name):
    T, D = h.shape

    def body(h_ref, g_ref, wg_ref, wu_ref, xn_ref, gu_ref, a_ref):
        @pl.when(pl.program_id(1) == 0)
        def _():
            x = h_ref[...]
            xn_ref[...] = (x * _rstd(x) * g_ref[...]).astype(BF16)

        xn = xn_ref[...]
        gate = _dot(xn, wg_ref[...])
        up = _dot(xn, wu_ref[...])
        gu_ref[:, :FF_T] = gate.astype(BF16)
        gu_ref[:, FF_T:] = up.astype(BF16)
        a_ref[...] = (gate * jax.nn.sigmoid(gate) * up).astype(BF16)

    return pl.pallas_call(
        body, name=name,
        grid=(T // TM, 2),
        in_specs=[pl.BlockSpec((TM, D), lambda i, j: (i, 0)),
                  pl.BlockSpec((1, D), lambda i, j: (0, 0)),
                  pl.BlockSpec((None, D, FF_T), lambda i, j: (j, 0, 0)),
                  pl.BlockSpec((None, D, FF_T), lambda i, j: (j + 2, 0, 0))],
        out_specs=[pl.BlockSpec((TM, D), lambda i, j: (i, 0)),
                   pl.BlockSpec((TM, 2 * FF_T), lambda i, j: (i, j)),
                   pl.BlockSpec((TM, FF_T), lambda i, j: (i, j))],
        out_shape=[jax.ShapeDtypeStruct((T, D), BF16),
                   jax.ShapeDtypeStruct((T, 2 * D_FF), BF16),
                   jax.ShapeDtypeStruct((T, D_FF), BF16)],
        compiler_params=_params("parallel", "arbitrary"),
    )(h, g, w_sm, w_sm)


def mm_norm_res(a, w, h_in, g, coef, name, target=None):
    T, K = a.shape
    D = w.shape[1]
    final = target is not None

    def body(*refs):
        if final:
            a_ref, w_ref, h_ref, g_ref, t_ref, f_ref, o_ref, l_ref = refs
        else:
            a_ref, w_ref, h_ref, g_ref, f_ref, o_ref = refs
        f = _dot(a_ref[...], w_ref[...])
        f_ref[...] = f
        y = h_ref[...] + coef * (f * _rstd(f) * g_ref[...])
        if final:
            err = y - t_ref[...]
            o_ref[...] = err * (1.0 / D)

            @pl.when(pl.program_id(0) == 0)
            def _():
                l_ref[...] = jnp.zeros_like(l_ref)

            l_ref[...] += jnp.sum(err * err)
        else:
            o_ref[...] = y

    row = pl.BlockSpec((TM, D), lambda i: (i, 0))
    in_specs = [pl.BlockSpec((TM, K), lambda i: (i, 0)),
                pl.BlockSpec((K, D), lambda i: (0, 0)),
                row, pl.BlockSpec((1, D), lambda i: (0, 0))]
    out_specs = [row, row]
    out_shape = [jax.ShapeDtypeStruct((T, D), F32), jax.ShapeDtypeStruct((T, D), F32)]
    args = [a, w, h_in, g]
    if final:
        in_specs.append(row)
        args.append(target)
        out_specs.append(pl.BlockSpec((8, 128), lambda i: (0, 0)))
        out_shape.append(jax.ShapeDtypeStruct((8, 128), F32))
    return pl.pallas_call(
        body, name=name, grid=(T // TM,), in_specs=in_specs, out_specs=out_specs, out_shape=out_shape,
        compiler_params=_params("arbitrary"),
    )(*args)


def _rope(x, cos, sin_signed):
    return x * cos + pltpu.roll(x, HEAD // 2, axis=1) * sin_signed


def _unrope(x, cos, sin_signed):
    return x * cos - pltpu.roll(x, HEAD // 2, axis=1) * sin_signed


def mix_in(h, g, w, cos, sin_signed, name):
    T, D = h.shape
    tn = 768

    def body(h_ref, g_ref, w_ref, c_ref, s_ref, u_ref, o_ref):
        j = pl.program_id(1)

        @pl.when(j == 0)
        def _():
            x = h_ref[...]
            u_ref[...] = (x * _rstd(x) * g_ref[...]).astype(BF16)

        acc = _dot(u_ref[...], w_ref[...])
        rotary = (j == 0) | (j == 1) | (j == 3)

        @pl.when(rotary)
        def _():
            c, s = c_ref[...], s_ref[...]
            for hd in range(tn // HEAD):
                cols = slice(hd * HEAD, (hd + 1) * HEAD)
                o_ref[:, cols] = _rope(acc[:, cols], c, s).astype(BF16)

        @pl.when(jnp.logical_not(rotary))
        def _():
            o_ref[...] = acc.astype(BF16)

    return pl.pallas_call(
        body, name=name,
        grid=(T // TM, QKV_W // tn),
        in_specs=[pl.BlockSpec((TM, D), lambda i, j: (i, 0)),
                  pl.BlockSpec((1, D), lambda i, j: (0, 0)),
                  pl.BlockSpec((D, tn), lambda i, j: (0, j)),
                  pl.BlockSpec((TM, HEAD), lambda i, j: (i, 0)),
                  pl.BlockSpec((TM, HEAD), lambda i, j: (i, 0))],
        out_specs=[pl.BlockSpec((TM, D), lambda i, j: (i, 0)),
                   pl.BlockSpec((TM, tn), lambda i, j: (i, j))],
        out_shape=[jax.ShapeDtypeStruct((T, D), BF16), jax.ShapeDtypeStruct((T, QKV_W), BF16)],
        compiler_params=_params("parallel", "arbitrary"),
    )(h, g, w, cos, sin_signed)


def gate_proj(u, w_sm, b, name):
    T, D = u.shape
    tn = w_sm.shape[2]

    def body(u_ref, w_ref, b_ref, o_ref):
        o_ref[...] = jax.nn.sigmoid(_dot(u_ref[...], w_ref[...]) + b_ref[...]).astype(BF16)

    return pl.pallas_call(
        body, name=name,
        grid=(T // TM, N_CHIPS),
        in_specs=[pl.BlockSpec((TM, D), lambda i, j: (i, 0)),
                  pl.BlockSpec((None, D, tn), lambda i, j: (j, 0, 0)),
                  pl.BlockSpec((1, tn), lambda i, j: (0, j))],
        out_specs=pl.BlockSpec((TM, tn), lambda i, j: (i, j)),
        out_shape=jax.ShapeDtypeStruct((T, N_CHIPS * tn), BF16),
        compiler_params=_params("parallel", "arbitrary"),
    )(u, w_sm, b)


def gate_merge(gt, o_a, o_b, o_m, w_a, w_b, w_m, name):
    T = gt.shape[0]
    D = D_MODEL

    def body(gt_ref, oa_ref, ob_ref, om_ref, wa_ref, wb_ref, wm_ref, out_ref):
        acc = gt_ref[:, :D].astype(F32) * _dot(oa_ref[...], wa_ref[...])
        acc += gt_ref[:, D:2 * D].astype(F32) * _dot(ob_ref[...], wb_ref[...])
        acc += gt_ref[:, 2 * D:].astype(F32) * _dot(om_ref[...], wm_ref[...])
        out_ref[...] = acc.astype(BF16)

    def rows(width):
        return pl.BlockSpec((TM, width), lambda i: (i, 0))

    def whole(arr):
        return pl.BlockSpec(arr.shape, lambda i: (0, 0))

    return pl.pallas_call(
        body, name=name, grid=(T // TM,),
        in_specs=[rows(3 * D), rows(o_a.shape[1]), rows(o_b.shape[1]), rows(o_m.shape[1]),
                  whole(w_a), whole(w_b), whole(w_m)],
        out_specs=rows(D),
        out_shape=jax.ShapeDtypeStruct((T, D), BF16),
        compiler_params=_params("parallel"),
    )(gt, o_a, o_b, o_m, w_a, w_b, w_m)


def _band_mask(max_dist, first_has_prev):
    row = lax.broadcasted_iota(jnp.int32, (HEAD, 2 * HEAD), 0)
    col = lax.broadcasted_iota(jnp.int32, (HEAD, 2 * HEAD), 1)
    dist = row + HEAD - col
    band = (dist >= 0) & (dist <= max_dist)
    return band, band & (col >= jnp.where(first_has_prev, 0, HEAD))


def band_fwd(qkv, sinks, *, r, q_off, k_off, v_off, hkv, grp, max_dist, out_dtype, name):
    T, W = qkv.shape
    R, WB = T // r, W // HEAD
    RB = min(512, R)
    nsub, nib = RB // HEAD, R // RB
    hq = hkv * grp
    view = qkv.reshape(R, r * W)

    def body(sink_ref, q_ref, kc_ref, kp_ref, vc_ref, vp_ref, o_ref, l_ref, kbuf, vbuf):
        kvh, ib = pl.program_id(1), pl.program_id(2)
        kbuf[:HEAD] = kp_ref[...]
        kbuf[HEAD:] = kc_ref[...]
        vbuf[:HEAD] = vp_ref[...]
        vbuf[HEAD:] = vc_ref[...]
        band, band_first = _band_mask(max_dist, ib > 0)
        for j in range(nsub):
            rows = slice(j * HEAD, (j + 1) * HEAD)
            kcat = kbuf[j * HEAD:(j + 2) * HEAD]
            vcat = vbuf[j * HEAD:(j + 2) * HEAD]
            mask = band_first if j == 0 else band
            for gq in range(grp):
                cols = slice(gq * HEAD, (gq + 1) * HEAD)
                s = jnp.where(mask, _dot_nt(q_ref[rows, cols], kcat) * ATT_SCALE, NEG_INF)
                sk = sink_ref[kvh * grp + gq]
                m = jnp.maximum(jnp.max(s, axis=-1, keepdims=True), sk)
                p = jnp.exp(s - m)
                tot = jnp.sum(p, axis=-1, keepdims=True) + jnp.exp(sk - m)
                o = _dot(p.astype(BF16), vcat) / tot
                o_ref[rows, cols] = o.astype(out_dtype)
                l_ref[rows, cols] = jnp.broadcast_to(m + jnp.log(tot), (HEAD, HEAD))

    def cur(off, width):
        return pl.BlockSpec((RB, width * HEAD), lambda c, h, i: (i, (c * WB + off) // width + h))

    def prev(off):
        return pl.BlockSpec((HEAD, HEAD), lambda c, h, i: (jnp.maximum(i * nsub - 1, 0), c * WB + off + h))

    out_spec = pl.BlockSpec((RB, grp * HEAD), lambda c, h, i: (i, c * hkv + h))
    o, lse = pl.pallas_call(
        body, name=name, grid=(r, hkv, nib),
        in_specs=[pl.BlockSpec(memory_space=pltpu.SMEM),
                  cur(q_off, grp), cur(k_off, 1), prev(k_off), cur(v_off, 1), prev(v_off)],
        out_specs=[out_spec, out_spec],
        out_shape=[jax.ShapeDtypeStruct((R, r * hq * HEAD), out_dtype),
                   jax.ShapeDtypeStruct((R, r * hq * HEAD), F32)],
        scratch_shapes=[pltpu.VMEM((RB + HEAD, HEAD), BF16), pltpu.VMEM((RB + HEAD, HEAD), BF16)],
        compiler_params=_params("parallel", "parallel", "arbitrary"),
    )(sinks, view, view, view, view, view)
    return o.reshape(T, hq * HEAD), lse.reshape(T, hq * HEAD)


def band_bwd(qkv, do, o, lse, cos, sin_signed, sinks, *, r, q_off, k_off, v_off, hkv, grp, max_dist, name):
    T, W = qkv.shape
    R, WB = T // r, W // HEAD
    RB = min(512, R)
    nsub, nib = RB // HEAD, R // RB
    nblk = R // HEAD
    hq = hkv * grp
    with_sink = sinks is not None
    view = qkv.reshape(R, r * W)
    do_v, o_v, l_v = (t.reshape(R, r * hq * HEAD) for t in (do, o, lse))
    cos_v, sin_v = cos.reshape(R, r * HEAD), sin_signed.reshape(R, r * HEAD)

    def body(*refs):
        if with_sink:
            sink_ref, refs = refs[0], refs[1:]
        (q_ref, qn_ref, kc_ref, kp_ref, vc_ref, vp_ref, do_ref, don_ref, o_ref, on_ref, l_ref, ln_ref,
         c_ref, s_ref) = refs[:14]
        dq_ref, dk_ref, dv_ref = refs[14:17]
        ds_ref = refs[17] if with_sink else None
        kbuf, vbuf, dkacc, dvacc = refs[-4:]
        kvh, ib = pl.program_id(1), pl.program_id(2)
        kbuf[:HEAD] = kp_ref[...]
        kbuf[HEAD:] = kc_ref[...]
        vbuf[:HEAD] = vp_ref[...]
        vbuf[HEAD:] = vc_ref[...]
        dkacc[...] = jnp.zeros_like(dkacc)
        dvacc[...] = jnp.zeros_like(dvacc)
        band, band_first = _band_mask(max_dist, ib > 0)
        if with_sink:
            @pl.when(ib == 0)
            def _():
                ds_ref[...] = jnp.zeros_like(ds_ref)

        def grads(q, dout, out, logz, keys, vals, mask):
            delta = jnp.sum(dout.astype(F32) * out.astype(F32), axis=-1, keepdims=True)
            s = jnp.where(mask, _dot_nt(q, keys) * ATT_SCALE, NEG_INF)
            p = jnp.exp(s - logz[:, :1])
            ds = (p * (_dot_nt(dout, vals) - delta) * ATT_SCALE).astype(BF16)
            return p.astype(BF16), ds, delta

        for j in range(nsub):
            rows = slice(j * HEAD, (j + 1) * HEAD)
            both = slice(j * HEAD, (j + 2) * HEAD)
            kcat, vcat = kbuf[both], vbuf[both]
            mask = band_first if j == 0 else band
            for gq in range(grp):
                cols = slice(gq * HEAD, (gq + 1) * HEAD)
                q, dout = q_ref[rows, cols], do_ref[rows, cols]
                p, ds, delta = grads(q, dout, o_ref[rows, cols], l_ref[rows, cols], kcat, vcat, mask)
                dq_ref[rows, cols] = _unrope(_dot(ds, kcat), c_ref[rows], s_ref[rows]).astype(BF16)
                dkacc[both] += _dot_tn(ds, q)
                dvacc[both] += _dot_tn(p, dout)
                if with_sink:
                    p_sink = jnp.exp(sink_ref[kvh * grp + gq] - l_ref[rows, cols][:, :1])
                    ds_ref[gq * 8:(gq + 1) * 8] += jnp.sum(p_sink * delta)

        row = lax.broadcasted_iota(jnp.int32, (HEAD, HEAD), 0)
        col = lax.broadcasted_iota(jnp.int32, (HEAD, HEAD), 1)
        reach = col >= row + jnp.where(ib < nib - 1, HEAD - max_dist, 2 * HEAD)
        last = slice(RB, RB + HEAD)
        for gq in range(grp):
            cols = slice(gq * HEAD, (gq + 1) * HEAD)
            q, dout = qn_ref[:, cols], don_ref[:, cols]
            p, ds, _ = grads(q, dout, on_ref[:, cols], ln_ref[:, cols], kbuf[last], vbuf[last], reach)
            dkacc[last] += _dot_tn(ds, q)
            dvacc[last] += _dot_tn(p, dout)

        dk_ref[...] = _unrope(dkacc[HEAD:], c_ref[...], s_ref[...]).astype(BF16)
        dv_ref[...] = dvacc[HEAD:].astype(BF16)

    def cur(off, width):
        return pl.BlockSpec((RB, width * HEAD), lambda c, h, i: (i, (c * WB + off) // width + h))

    def prev(off):
        return pl.BlockSpec((HEAD, HEAD), lambda c, h, i: (jnp.maximum(i * nsub - 1, 0), c * WB + off + h))

    def nxt_row(i):
        return jnp.minimum((i + 1) * nsub, nblk - 1)

    q_next = pl.BlockSpec((HEAD, grp * HEAD), lambda c, h, i: (nxt_row(i), (c * WB + q_off) // grp + h))
    head_cur = pl.BlockSpec((RB, grp * HEAD), lambda c, h, i: (i, c * hkv + h))
    head_next = pl.BlockSpec((HEAD, grp * HEAD), lambda c, h, i: (nxt_row(i), c * hkv + h))
    table = pl.BlockSpec((RB, HEAD), lambda c, h, i: (i, c))
    kv_out = pl.BlockSpec((RB, HEAD), lambda c, h, i: (i, c * hkv + h))

    in_specs = [cur(q_off, grp), q_next, cur(k_off, 1), prev(k_off), cur(v_off, 1), prev(v_off),
                head_cur, head_next, head_cur, head_next, head_cur, head_next, table, table]
    args = [view, view, view, view, view, view, do_v, do_v, o_v, o_v, l_v, l_v, cos_v, sin_v]
    out_specs = [head_cur, kv_out, kv_out]
    out_shape = [jax.ShapeDtypeStruct((R, r * hq * HEAD), BF16),
                 jax.ShapeDtypeStruct((R, r * hkv * HEAD), BF16),
                 jax.ShapeDtypeStruct((R, r * hkv * HEAD), BF16)]
    if with_sink:
        in_specs.insert(0, pl.BlockSpec(memory_space=pltpu.SMEM))
        args.insert(0, sinks)
        out_specs.append(pl.BlockSpec((None, grp * 8, HEAD), lambda c, h, i: (h, 0, 0)))
        out_shape.append(jax.ShapeDtypeStruct((hkv, grp * 8, HEAD), F32))
    res = pl.pallas_call(
        body, name=name, grid=(r, hkv, nib), in_specs=in_specs, out_specs=out_specs, out_shape=out_shape,
        scratch_shapes=[pltpu.VMEM((RB + HEAD, HEAD), BF16), pltpu.VMEM((RB + HEAD, HEAD), BF16),
                        pltpu.VMEM((RB + HEAD, HEAD), F32), pltpu.VMEM((RB + HEAD, HEAD), F32)],
        compiler_params=_params("parallel", "parallel", "arbitrary"),
    )(*args)
    dq, dk, dv = (t.reshape(T, -1) for t in res[:3])
    return (dq, dk, dv, res[3]) if with_sink else (dq, dk, dv)


def merge_groups(outs, lses, name):
    T, Wd = outs[0].shape
    tm = 1024

    def body(o0, o1, o2, l0, l1, l2, out_ref, lt_ref):
        a, b, c = l0[...], l1[...], l2[...]
        m = jnp.maximum(jnp.maximum(a, b), c)
        wa, wb, wc = jnp.exp(a - m), jnp.exp(b - m), jnp.exp(c - m)
        z = wa + wb + wc
        out_ref[...] = ((wa * o0[...] + wb * o1[...] + wc * o2[...]) / z).astype(BF16)
        lt_ref[...] = m + jnp.log(z)

    spec = pl.BlockSpec((tm, Wd), lambda i: (i, 0))
    return pl.pallas_call(
        body, name=name, grid=(T // tm,), in_specs=[spec] * 6, out_specs=[spec, spec],
        out_shape=[jax.ShapeDtypeStruct((T, Wd), BF16), jax.ShapeDtypeStruct((T, Wd), F32)],
        compiler_params=_params("parallel"),
    )(*outs, *lses)


M_HEADS = 4


def mem_kv(mem, g, w, name):
    n, D = mem.shape

    def body(m_ref, g_ref, w_ref, mn_ref, kv_ref):
        x = m_ref[...]
        mn = (x * _rstd(x) * g_ref[...]).astype(BF16)
        mn_ref[...] = mn
        kv_ref[...] = _dot(mn, w_ref[...]).astype(BF16)

    return pl.pallas_call(
        body, name=name,
        out_shape=[jax.ShapeDtypeStruct((n, D), BF16), jax.ShapeDtypeStruct((n, w.shape[1]), BF16)],
        compiler_params=pltpu.CompilerParams(vmem_limit_bytes=VMEM_LIMIT),
    )(mem, g, w)


def mem_fwd(qkv, mkv, name):
    T = qkv.shape[0]
    n = mkv.shape[0]
    RB = 1024

    def body(q_ref, k_ref, v_ref, o_ref, l_ref):
        s = _dot_nt(q_ref[...], k_ref[...]) * ATT_SCALE
        m = jnp.max(s, axis=-1, keepdims=True)
        p = jnp.exp(s - m)
        den = jnp.sum(p, axis=-1, keepdims=True)
        o_ref[...] = (_dot(p.astype(BF16), v_ref[...]) / den).astype(BF16)
        l_ref[...] = jnp.broadcast_to(m + jnp.log(den), (RB, HEAD))

    out = pl.BlockSpec((RB, HEAD), lambda h, i: (i, h))
    return pl.pallas_call(
        body, name=name, grid=(M_HEADS, T // RB),
        in_specs=[pl.BlockSpec((RB, HEAD), lambda h, i: (i, MQ + h)),
                  pl.BlockSpec((n, HEAD), lambda h, i: (0, h)),
                  pl.BlockSpec((n, HEAD), lambda h, i: (0, M_HEADS + h))],
        out_specs=[out, out],
        out_shape=[jax.ShapeDtypeStruct((T, M_HEADS * HEAD), BF16), jax.ShapeDtypeStruct((T, M_HEADS * HEAD), F32)],
        compiler_params=_params("parallel", "parallel"),
    )(qkv, mkv, mkv)


def mem_bwd(qkv, mkv, do, o, lse, name):
    T = qkv.shape[0]
    n = mkv.shape[0]
    RB = 1024

    def body(q_ref, k_ref, v_ref, do_ref, o_ref, l_ref, dq_ref, dk_ref, dv_ref):
        @pl.when(pl.program_id(1) == 0)
        def _():
            dk_ref[...] = jnp.zeros_like(dk_ref)
            dv_ref[...] = jnp.zeros_like(dv_ref)

        q, dout = q_ref[...], do_ref[...]
        delta = jnp.sum(dout.astype(F32) * o_ref[...].astype(F32), axis=-1, keepdims=True)
        p = jnp.exp(_dot_nt(q, k_ref[...]) * ATT_SCALE - l_ref[...][:, :1])
        ds = (p * (_dot_nt(dout, v_ref[...]) - delta) * ATT_SCALE).astype(BF16)
        dq_ref[...] = _dot(ds, k_ref[...]).astype(BF16)
        dk_ref[...] += _dot_tn(ds, q)
        dv_ref[...] += _dot_tn(p.astype(BF16), dout)

    tok = pl.BlockSpec((RB, HEAD), lambda h, i: (i, h))
    slot = pl.BlockSpec((n, HEAD), lambda h, i: (0, h))
    return pl.pallas_call(
        body, name=name, grid=(M_HEADS, T // RB),
        in_specs=[pl.BlockSpec((RB, HEAD), lambda h, i: (i, MQ + h)),
                  slot, pl.BlockSpec((n, HEAD), lambda h, i: (0, M_HEADS + h)), tok, tok, tok],
        out_specs=[tok, slot, slot],
        out_shape=[jax.ShapeDtypeStruct((T, M_HEADS * HEAD), BF16),
                   jax.ShapeDtypeStruct((n, M_HEADS * HEAD), F32),
                   jax.ShapeDtypeStruct((n, M_HEADS * HEAD), F32)],
        compiler_params=_params("parallel", "arbitrary"),
    )(qkv, mkv, mkv, do, o, lse)


def mem_kv_bwd(mem, g, mem_n, w, dmkv, name):
    n, D = mem.shape

    def body(m_ref, g_ref, mn_ref, w_ref, d_ref, dw_ref, dg_ref):
        d = d_ref[...].astype(BF16)
        dw_ref[...] = _dot_tn(mn_ref[...], d)
        x = m_ref[...]
        dg_ref[...] = jnp.sum(_dot_nt(d, w_ref[...]) * (x * _rstd(x)), axis=0, keepdims=True)

    return pl.pallas_call(
        body, name=name,
        out_shape=[jax.ShapeDtypeStruct(w.shape, F32), jax.ShapeDtypeStruct((1, D), F32)],
        compiler_params=pltpu.CompilerParams(vmem_limit_bytes=VMEM_LIMIT),
    )(mem, g, mem_n, w, dmkv)


def _rms_bwd(dn, f, g):
    r = _rstd(f)
    fhat = f * r
    dfhat = dn * g
    df = r * (dfhat - fhat * jnp.mean(dfhat * fhat, axis=-1, keepdims=True))
    return df, jnp.sum(dn * fhat, axis=0, keepdims=True)


def ffn_out_bwd(dh, f, g, w_out, gu, coef, name):
    T, D = dh.shape

    def body(dh_ref, f_ref, g_ref, w_ref, gu_ref, df_ref, dgu_ref, a_ref, dg_ref):
        i, j = pl.program_id(0), pl.program_id(1)

        @pl.when(j == 0)
        def _():
            df, dg = _rms_bwd(coef * dh_ref[...], f_ref[...], g_ref[...])
            df_ref[...] = df.astype(BF16)

            @pl.when(i == 0)
            def _():
                dg_ref[...] = jnp.zeros_like(dg_ref)

            dg_ref[...] += dg

        da = _dot_nt(df_ref[...], w_ref[...])
        gate = gu_ref[:, :FF_T].astype(F32)
        up = gu_ref[:, FF_T:].astype(F32)
        sig = jax.nn.sigmoid(gate)
        silu = gate * sig
        dgu_ref[:, :FF_T] = (da * up * sig * (1.0 + gate * (1.0 - sig))).astype(BF16)
        dgu_ref[:, FF_T:] = (da * silu).astype(BF16)
        a_ref[...] = (silu * up).astype(BF16)

    row = pl.BlockSpec((TM, D), lambda i, j: (i, 0))
    wide = pl.BlockSpec((TM, 2 * FF_T), lambda i, j: (i, j))
    return pl.pallas_call(
        body, name=name, grid=(T // TM, 2),
        in_specs=[row, row, pl.BlockSpec((1, D), lambda i, j: (0, 0)),
                  pl.BlockSpec((FF_T, D), lambda i, j: (j, 0)), wide],
        out_specs=[row, wide, pl.BlockSpec((TM, FF_T), lambda i, j: (i, j)),
                   pl.BlockSpec((1, D), lambda i, j: (0, 0))],
        out_shape=[jax.ShapeDtypeStruct((T, D), BF16), jax.ShapeDtypeStruct((T, 2 * D_FF), BF16),
                   jax.ShapeDtypeStruct((T, D_FF), BF16), jax.ShapeDtypeStruct((1, D), F32)],
        compiler_params=_params("arbitrary", "arbitrary"),
    )(dh, f, g, w_out, gu)


def mix_out_bwd(dh, f, g, w_out, name):
    T, D = dh.shape

    def body(dh_ref, f_ref, g_ref, w_ref, df_ref, dm_ref, dg_ref):
        df, dg = _rms_bwd(dh_ref[...], f_ref[...], g_ref[...])
        df = df.astype(BF16)
        df_ref[...] = df

        @pl.when(pl.program_id(0) == 0)
        def _():
            dg_ref[...] = jnp.zeros_like(dg_ref)

        dg_ref[...] += dg
        dm_ref[...] = _dot_nt(df, w_ref[...]).astype(BF16)

    row = pl.BlockSpec((TM, D), lambda i: (i, 0))
    vec = pl.BlockSpec((1, D), lambda i: (0, 0))
    return pl.pallas_call(
        body, name=name, grid=(T // TM,),
        in_specs=[row, row, vec, pl.BlockSpec((D, D), lambda i: (0, 0))],
        out_specs=[row, row, vec],
        out_shape=[jax.ShapeDtypeStruct((T, D), BF16), jax.ShapeDtypeStruct((T, D), BF16),
                   jax.ShapeDtypeStruct((1, D), F32)],
        compiler_params=_params("arbitrary"),
    )(dh, f, g, w_out)


def mm_nt_norm_bwd(pieces, h_in, dh_out, g, name):
    T, D = h_in.shape
    counts = [a.shape[1] // tk for a, _, tk, _ in pieces]
    starts = [sum(counts[:p]) for p in range(len(pieces))]
    nk = sum(counts)

    def body(*refs):
        ab = refs[:2 * len(pieces)]
        h_ref, dh_ref, g_ref, o_ref, dg_ref, acc = refs[2 * len(pieces):]
        i, k = pl.program_id(0), pl.program_id(1)

        @pl.when(k == 0)
        def _():
            acc[...] = jnp.zeros_like(acc)

        for p in range(len(pieces)):
            @pl.when((k >= starts[p]) & (k < starts[p] + counts[p]))
            def _(p=p):
                acc[...] += _dot_nt(ab[2 * p][...], ab[2 * p + 1][...])

        @pl.when(k == nk - 1)
        def _():
            h = h_ref[...]
            r = _rstd(h)
            xhat = h * r
            dxn = acc[...]
            dxhat = dxn * g_ref[...]
            o_ref[...] = dh_ref[...] + r * (dxhat - xhat * jnp.mean(dxhat * xhat, axis=-1, keepdims=True))

            @pl.when(i == 0)
            def _():
                dg_ref[...] = jnp.zeros_like(dg_ref)

            dg_ref[...] += jnp.sum(dxn * xhat, axis=0, keepdims=True)

    in_specs, args = [], []
    for p, (a, w, tk, perm) in enumerate(pieces):
        def tile(k, p=p):
            return jnp.clip(k - starts[p], 0, counts[p] - 1)

        in_specs.append(pl.BlockSpec((TM, tk), lambda i, k, tile=tile: (i, tile(k))))
        if w.ndim == 3:
            in_specs.append(pl.BlockSpec((None, D, tk), lambda i, k, tile=tile, perm=perm: (perm(tile(k)), 0, 0)))
        else:
            in_specs.append(pl.BlockSpec((D, tk), lambda i, k, tile=tile: (0, tile(k))))
        args += [a, w]
    row = pl.BlockSpec((TM, D), lambda i, k: (i, 0))
    vec = pl.BlockSpec((1, D), lambda i, k: (0, 0))
    return pl.pallas_call(
        body, name=name, grid=(T // TM, nk),
        in_specs=in_specs + [row, row, vec],
        out_specs=[row, vec],
        out_shape=[jax.ShapeDtypeStruct((T, D), F32), jax.ShapeDtypeStruct((1, D), F32)],
        scratch_shapes=[pltpu.VMEM((TM, D), F32)],
        compiler_params=_params("arbitrary", "arbitrary"),
    )(*args, h_in, dh_out, g)


def gate_merge_bwd(dm, gt, o_a, o_b, o_m, w_a, w_b, w_m, name):
    T = dm.shape[0]
    D = D_MODEL
    branch = ((o_a, w_a), (o_b, w_b), (o_m, w_m))

    def body(dm_ref, gt_ref, oa_ref, ob_ref, om_ref, wa_ref, wb_ref, wm_ref,
             dgt_ref, dpa_ref, dpb_ref, dpm_ref, doa_ref, dob_ref, dom_ref, db_ref):
        @pl.when(pl.program_id(0) == 0)
        def _():
            db_ref[...] = jnp.zeros_like(db_ref)

        dmf = dm_ref[...].astype(F32)
        for x, (o_ref, w_ref, dp_ref, do_ref) in enumerate(((oa_ref, wa_ref, dpa_ref, doa_ref),
                                                           (ob_ref, wb_ref, dpb_ref, dob_ref),
                                                           (om_ref, wm_ref, dpm_ref, dom_ref))):
            cols = slice(x * D, (x + 1) * D)
            gx = gt_ref[:, cols].astype(F32)
            w = w_ref[...]
            dpre = dmf * _dot(o_ref[...], w) * gx * (1.0 - gx)
            dgt_ref[:, cols] = dpre.astype(BF16)
            db_ref[:, cols] += jnp.sum(dpre, axis=0, keepdims=True)
            dp = (dmf * gx).astype(BF16)
            dp_ref[...] = dp
            do_ref[...] = _dot_nt(dp, w).astype(BF16)

    def rows(width):
        return pl.BlockSpec((TM, width), lambda i: (i, 0))

    def whole(arr):
        return pl.BlockSpec(arr.shape, lambda i: (0, 0))

    widths = [o.shape[1] for o, _ in branch]
    return pl.pallas_call(
        body, name=name, grid=(T // TM,),
        in_specs=[rows(D), rows(3 * D)] + [rows(k) for k in widths] + [whole(w) for _, w in branch],
        out_specs=[rows(3 * D), rows(D), rows(D), rows(D)] + [rows(k) for k in widths]
                  + [pl.BlockSpec((1, 3 * D), lambda i: (0, 0))],
        out_shape=[jax.ShapeDtypeStruct((T, 3 * D), BF16)] + [jax.ShapeDtypeStruct((T, D), BF16)] * 3
                  + [jax.ShapeDtypeStruct((T, k), BF16) for k in widths]
                  + [jax.ShapeDtypeStruct((1, 3 * D), F32)],
        compiler_params=_params("arbitrary"),
    )(dm, gt, o_a, o_b, o_m, w_a, w_b, w_m)


def mm_tn(x, dy, tm, tn, name, shard_major=False, perm=None):
    T, M = x.shape
    N = dy.shape[1]
    tk = min(1024, T)
    perm = perm or (lambda j: j)

    def body(x_ref, dy_ref, o_ref):
        @pl.when(pl.program_id(2) == 0)
        def _():
            o_ref[...] = jnp.zeros_like(o_ref)

        o_ref[...] += _dot_tn(x_ref[...], dy_ref[...])

    if shard_major:
        out_spec = pl.BlockSpec((None, tm, tn), lambda i, j, k: (perm(j), i, 0))
        out_shape = jax.ShapeDtypeStruct((N // tn, M, tn), F32)
    else:
        out_spec = pl.BlockSpec((tm, tn), lambda i, j, k: (i, j))
        out_shape = jax.ShapeDtypeStruct((M, N), F32)
    return pl.pallas_call(
        body, name=name, grid=(M // tm, N // tn, T // tk),
        in_specs=[pl.BlockSpec((tk, tm), lambda i, j, k: (k, i)),
                  pl.BlockSpec((tk, tn), lambda i, j, k: (k, j))],
        out_specs=out_spec, out_shape=out_shape,
        compiler_params=_params("parallel", "parallel", "arbitrary"),
    )(x, dy)


def rope_tables(T):
    half = HEAD // 2
    inv = ROPE_THETA ** (-jnp.arange(half, dtype=F32) / half)
    ang = jnp.arange(T).astype(F32)[:, None] * inv[None, :]
    cos, sin = jnp.cos(ang), jnp.sin(ang)
    return jnp.concatenate([cos, cos], axis=1), jnp.concatenate([-sin, sin], axis=1)


def layer_step(x, mem, target, gains, sinks, b_gate, w):
    T = x.shape[0]
    cos, sin_signed = rope_tables(T)
    no_sink = jnp.full((2,), NEG_INF, F32)

    xn1, gu1, a1 = ffn_in(x, gains["ffn1_norm_pre"], w["ffn1_w_in"], "ffn1_in")
    f1, h1 = mm_norm_res(a1, w["ffn1_w_out"], x, gains["ffn1_norm_post"], 0.5, "ffn1_out")
    u, qkv = mix_in(h1, gains["mix_norm_pre"], w["w_in"], cos, sin_signed, "mix_in")
    gt = gate_proj(u, w["w_gate"], b_gate, "gate_proj")
    outs, lses = [], []
    for gidx, (window, dil) in enumerate(DIL):
        o_g, l_g = band_fwd(qkv, no_sink, r=dil, q_off=AQ + 2 * gidx, k_off=AK + 2 * gidx, v_off=AV + 2 * gidx,
                            hkv=2, grp=1, max_dist=window // dil, out_dtype=F32, name=f"attn_a{gidx}_fwd")
        outs.append(o_g)
        lses.append(l_g)
    o_a, l_a = merge_groups(outs, lses, "attn_a_merge")
    o_b, l_b = band_fwd(qkv, sinks, r=1, q_off=BQ, k_off=BK, v_off=BV, hkv=2, grp=2, max_dist=HEAD - 1,
                        out_dtype=BF16, name="attn_b_fwd")
    mem_n, mkv = mem_kv(mem, gains["mem_norm"], w["w_mem_kv"], "mem_kv")
    o_m, l_m = mem_fwd(qkv, mkv, "attn_m_fwd")
    merged = gate_merge(gt, o_a, o_b, o_m, w["w_o_a"], w["w_o_b"], w["w_o_m"], "gate_merge")
    mo, h2 = mm_norm_res(merged, w["w_out"], h1, gains["mix_norm_post"], 1.0, "mix_out")
    xn2, gu2, a2 = ffn_in(h2, gains["ffn2_norm_pre"], w["ffn2_w_in"], "ffn2_in")
    f2, dy, sq = mm_norm_res(a2, w["ffn2_w_out"], h2, gains["ffn2_norm_post"], 0.5, "ffn2_out", target=target)
    del a1, a2

    grads = {}

    def ffn_bwd(tag, dh_out, f, gu, xn, h_in):
        df, dgu, a, grads[f"{tag}_norm_post"] = ffn_out_bwd(
            dh_out, f, gains[f"{tag}_norm_post"], w[f"{tag}_w_out"], gu, 0.5, f"{tag}_out_bwd")
        grads[f"{tag}_w_out"] = mm_tn(a, df, FF_T, D_MODEL, f"{tag}_w_out_grad")
        grads[f"{tag}_w_in"] = mm_tn(xn, dgu, D_MODEL, FF_T, f"{tag}_w_in_grad", shard_major=True, perm=_ffn_perm)
        dh_in, grads[f"{tag}_norm_pre"] = mm_nt_norm_bwd(
            [(dgu, w[f"{tag}_w_in"], FF_T, _ffn_perm)], h_in, dh_out, gains[f"{tag}_norm_pre"], f"{tag}_in_bwd")
        return dh_in

    dh2 = ffn_bwd("ffn2", dy, f2, gu2, xn2, h2)

    dmo, dmerged, grads["mix_norm_post"] = mix_out_bwd(dh2, mo, gains["mix_norm_post"], w["w_out"], "mix_out_bwd")
    grads["w_out"] = mm_tn(merged, dmo, D_MODEL, D_MODEL, "w_out_grad")
    dgt, dpa, dpb, dpm, do_a, do_b, do_m, grads["b_gate"] = gate_merge_bwd(
        dmerged, gt, o_a, o_b, o_m, w["w_o_a"], w["w_o_b"], w["w_o_m"], "gate_merge_bwd")
    grads["w_o_a"] = mm_tn(o_a, dpa, o_a.shape[1], D_MODEL, "w_o_a_grad")
    grads["w_o_b"] = mm_tn(o_b, dpb, o_b.shape[1], D_MODEL, "w_o_b_grad")
    grads["w_o_m"] = mm_tn(o_m, dpm, o_m.shape[1], D_MODEL, "w_o_m_grad")

    dq_a, dk_a, dv_a = [], [], []
    for gidx, (window, dil) in enumerate(DIL):
        dq, dk, dv = band_bwd(qkv, do_a, o_a, l_a, cos, sin_signed, None, r=dil, q_off=AQ + 2 * gidx,
                              k_off=AK + 2 * gidx, v_off=AV + 2 * gidx, hkv=2, grp=1, max_dist=window // dil,
                              name=f"attn_a{gidx}_bwd")
        dq_a.append(dq)
        dk_a.append(dk)
        dv_a.append(dv)
    dq_b, dk_b, dv_b, dsink = band_bwd(qkv, do_b, o_b, l_b, cos, sin_signed, sinks, r=1, q_off=BQ, k_off=BK,
                                       v_off=BV, hkv=2, grp=2, max_dist=HEAD - 1, name="attn_b_bwd")
    grads["sinks"] = -dsink[:, ::8, 0].reshape(1, 4)
    dq_m, dmk, dmv = mem_bwd(qkv, mkv, do_m, o_m, l_m, "attn_m_bwd")
    grads["w_mem_kv"], grads["mem_norm"] = mem_kv_bwd(
        mem, gains["mem_norm"], mem_n, w["w_mem_kv"], jnp.concatenate([dmk, dmv], axis=1), "mem_kv_bwd")
    dqkv = jnp.concatenate(dq_a + dk_a + dv_a + [dq_b, dk_b, dv_b, dq_m], axis=1)

    grads["w_in"] = mm_tn(u, dqkv, D_MODEL, 1280, "w_in_grad")
    grads["w_gate"] = mm_tn(u, dgt, D_MODEL, 768, "w_gate_grad", shard_major=True)
    dh1, grads["mix_norm_pre"] = mm_nt_norm_bwd(
        [(dqkv, w["w_in"], 768, None), (dgt, w["w_gate"], 768, lambda k: k)],
        h1, dh2, gains["mix_norm_pre"], "mix_in_bwd")

    dx = ffn_bwd("ffn1", dh1, f1, gu1, xn1, x)
    return sq, dx, grads


def _place():
    return lax.axis_index("x"), lax.axis_index("y"), lax.axis_index("c")


def _other_chips(x, y):
    return [(1 - x, y), (x, 1 - y), (1 - x, 1 - y)]


def _hbm(n):
    return [pl.BlockSpec(memory_space=pltpu.HBM)] * n


def chip_all_gather(shards, name):
    n = len(shards)

    def body(*refs):
        ins, outs = refs[:n], refs[n:2 * n]
        send_sems, recv_sems, local_sems = refs[2 * n:]
        x, y, c = _place()
        me = 2 * x + y
        chips = _other_chips(x, y)

        def copy(i, j, slot):
            px, py = chips[j]
            return pltpu.make_async_remote_copy(
                src_ref=ins[i], dst_ref=outs[i].at[slot], send_sem=send_sems.at[3 * i + j],
                recv_sem=recv_sems.at[3 * i + j], device_id=(px, py, c), device_id_type=MESH)

        local = [pltpu.make_async_copy(ins[i], outs[i].at[me], local_sems.at[i]) for i in range(n)]
        for i in range(n):
            local[i].start()
            for j in range(3):
                copy(i, j, me).start()
        for i in range(n):
            for j, (px, py) in enumerate(chips):
                copy(i, j, 2 * px + py).wait()
            local[i].wait()

    return pl.pallas_call(
        body, name=name, in_specs=_hbm(n), out_specs=_hbm(n),
        out_shape=[jax.ShapeDtypeStruct((N_CHIPS,) + s.shape, s.dtype) for s in shards],
        scratch_shapes=[pltpu.SemaphoreType.DMA((3 * n,)), pltpu.SemaphoreType.DMA((3 * n,)),
                        pltpu.SemaphoreType.DMA((n,))],
    )(*shards)


def grad_exchange(grads_sm, small, name):
    n = len(grads_sm)
    flips = [(fx, fy, fc) for fx in (0, 1) for fy in (0, 1) for fc in (0, 1)][1:]

    def body(*refs):
        ins, small_in = refs[:n], refs[n]
        outs, small_out = refs[n + 1:2 * n + 1], refs[2 * n + 1]
        send_sems, recv_sems, local_sems = refs[2 * n + 2:]
        x, y, c = _place()
        me = 2 * x + y
        chips = _other_chips(x, y)

        def copy(i, j, src_slot, dst_slot):
            px, py = chips[j]
            return pltpu.make_async_remote_copy(
                src_ref=ins[i].at[src_slot], dst_ref=outs[i].at[dst_slot], send_sem=send_sems.at[3 * i + j],
                recv_sem=recv_sems.at[3 * i + j], device_id=(px, py, c), device_id_type=MESH)

        def small_copy(k, slot):
            fx, fy, fc = flips[k]
            return pltpu.make_async_remote_copy(
                src_ref=small_in, dst_ref=small_out.at[slot], send_sem=send_sems.at[3 * n + k],
                recv_sem=recv_sems.at[3 * n + k],
                device_id=(x ^ fx, y ^ fy, c ^ fc), device_id_type=MESH)

        def dev(k):
            fx, fy, fc = flips[k]
            return 4 * (x ^ fx) + 2 * (y ^ fy) + (c ^ fc)

        local = [pltpu.make_async_copy(ins[i].at[me], outs[i].at[me], local_sems.at[i]) for i in range(n)]
        local.append(pltpu.make_async_copy(small_in, small_out.at[4 * x + 2 * y + c], local_sems.at[n]))
        for k in range(len(flips)):
            small_copy(k, 4 * x + 2 * y + c).start()
        for i in range(n):
            local[i].start()
            for j, (px, py) in enumerate(chips):
                copy(i, j, 2 * px + py, me).start()
        local[n].start()
        for k in range(len(flips)):
            small_copy(k, dev(k)).wait()
        for i in range(n):
            for j, (px, py) in enumerate(chips):
                copy(i, j, 2 * px + py, 2 * px + py).wait()
            local[i].wait()
        local[n].wait()

    nsem = 3 * n + len(flips)
    return pl.pallas_call(
        body, name=name, in_specs=_hbm(n + 1), out_specs=_hbm(n + 1),
        out_shape=[jax.ShapeDtypeStruct(g.shape, g.dtype) for g in grads_sm]
                  + [jax.ShapeDtypeStruct((N_DEV,) + small.shape, small.dtype)],
        scratch_shapes=[pltpu.SemaphoreType.DMA((nsem,)), pltpu.SemaphoreType.DMA((nsem,)),
                        pltpu.SemaphoreType.DMA((n + 1,))],
    )(*grads_sm, small)


def sibling_exchange(parts, name):
    n = len(parts)

    def body(*refs):
        ins, outs = refs[:n], refs[n:2 * n]
        send_sems, recv_sems = refs[2 * n:]
        x, y, c = _place()
        copies = [pltpu.make_async_remote_copy(
            src_ref=ins[i], dst_ref=outs[i], send_sem=send_sems.at[i], recv_sem=recv_sems.at[i],
            device_id=(x, y, 1 - c), device_id_type=MESH) for i in range(n)]
        for cp in copies:
            cp.start()
        for cp in copies:
            cp.wait()

    return pl.pallas_call(
        body, name=name, in_specs=_hbm(n), out_specs=_hbm(n),
        out_shape=[jax.ShapeDtypeStruct(p.shape, p.dtype) for p in parts],
        scratch_shapes=[pltpu.SemaphoreType.DMA((n,)), pltpu.SemaphoreType.DMA((n,))],
    )(*parts)


def _row_tile(rows):
    for t in (256, 176, 128, 64, 32, 16, 8):
        if rows % t == 0:
            return t
    return rows


def chip_partial_sum(me, own_sm, recv, name):
    _, rows, cols = own_sm.shape
    tr = _row_tile(rows)

    def body(me_ref, own_ref, r0, r1, r2, r3, o_ref):
        acc = jnp.zeros((tr, cols), F32)
        for s, r_ref in enumerate((r0, r1, r2, r3)):
            acc = acc + jnp.where(me_ref[0] == s, own_ref[...], r_ref[...].astype(F32))
        o_ref[...] = acc

    def slot(s):
        return pl.BlockSpec((None, tr, cols), lambda i, me_ref, s=s: (s, i, 0))

    return pl.pallas_call(
        body, name=name,
        grid_spec=pltpu.PrefetchScalarGridSpec(
            num_scalar_prefetch=1, grid=(rows // tr,),
            in_specs=[pl.BlockSpec((None, tr, cols), lambda i, me_ref: (me_ref[0], i, 0))] + [slot(s) for s in range(4)],
            out_specs=pl.BlockSpec((tr, cols), lambda i, me_ref: (i, 0))),
        out_shape=jax.ShapeDtypeStruct((rows, cols), F32),
        compiler_params=_params("parallel"),
    )(me, own_sm, recv, recv, recv, recv)


def _adamw(w, g, m, v):
    m = ADAM_B1 * m + (1.0 - ADAM_B1) * g
    v = ADAM_B2 * v + (1.0 - ADAM_B2) * (g * g)
    m_hat = m / (1.0 - ADAM_B1 ** ADAM_STEP)
    v_hat = v / (1.0 - ADAM_B2 ** ADAM_STEP)
    delta = -ADAM_LR * (m_hat / (jnp.sqrt(v_hat) + ADAM_EPS) + ADAM_WD * w)
    return delta, m, v


def adamw_pair(part, sib, w, m, v, name):
    rows, cols = w.shape
    tr = _row_tile(rows)

    def body(p_ref, s_ref, w_ref, m_ref, v_ref, g_ref, d_ref, nm_ref, nv_ref):
        g = p_ref[...] + s_ref[...]
        g_ref[...] = g
        d_ref[...], nm_ref[...], nv_ref[...] = _adamw(w_ref[...], g, m_ref[...], v_ref[...])

    spec = pl.BlockSpec((tr, cols), lambda i: (i, 0))
    return pl.pallas_call(
        body, name=name, grid=(rows // tr,), in_specs=[spec] * 5, out_specs=[spec] * 4,
        out_shape=[jax.ShapeDtypeStruct((rows, cols), F32)] * 4,
        compiler_params=_params("parallel"),
    )(part, sib, w, m, v)


def adamw_small(g_all, w, m, v, name):
    def body(ga_ref, w_ref, m_ref, v_ref, g_ref, d_ref, nm_ref, nv_ref):
        g = ga_ref[0]
        for k in range(1, N_DEV):
            g = g + ga_ref[k]
        g_ref[...] = g
        d_ref[...], nm_ref[...], nv_ref[...] = _adamw(w_ref[...], g, m_ref[...], v_ref[...])

    return pl.pallas_call(
        body, name=name, out_shape=[jax.ShapeDtypeStruct(w.shape, F32)] * 4,
    )(g_all, w, m, v)


WEIGHTS = ("ffn1_norm_pre", "ffn1_w_in", "ffn1_w_out", "ffn1_norm_post", "mix_norm_pre", "w_in", "sinks",
           "mem_norm", "w_mem_kv", "w_gate", "b_gate", "w_o_a", "w_o_b", "w_o_m", "w_out", "mix_norm_post",
           "ffn2_norm_pre", "ffn2_w_in", "ffn2_w_out", "ffn2_norm_post")
BIG = ("ffn1_w_in", "ffn1_w_out", "w_in", "w_mem_kv", "w_gate", "w_o_a", "w_o_b", "w_o_m", "w_out",
       "ffn2_w_in", "ffn2_w_out")
COLUMN_SHARDED = ("ffn1_w_in", "ffn2_w_in", "w_in", "w_gate", "w_o_a", "w_o_b", "w_o_m")
KEPT_SHARD_MAJOR = ("ffn1_w_in", "ffn2_w_in", "w_gate")
GAINS = ("ffn1_norm_pre", "ffn1_norm_post", "mix_norm_pre", "mem_norm", "mix_norm_post", "ffn2_norm_pre",
         "ffn2_norm_post")
SMALL_ROWS = 16


def _pack_small(t):
    sinks = jnp.pad(t["sinks"], ((0, 0), (0, D_MODEL - t["sinks"].shape[1])))
    rows = [t[k] for k in GAINS] + [t["b_gate"].reshape(3, D_MODEL), sinks]
    packed = jnp.concatenate(rows, axis=0)
    return jnp.pad(packed, ((0, SMALL_ROWS - packed.shape[0]), (0, 0)))


def _unpack_small(p):
    out = {k: p[i:i + 1] for i, k in enumerate(GAINS)}
    out["b_gate"] = p[7:10].reshape(1, 3 * D_MODEL)
    out["sinks"] = p[10:11, :4]
    return out


def kernel(x, mem, ffn1_norm_pre, ffn1_w_in, ffn1_w_out, ffn1_norm_post, mix_norm_pre, w_in, sinks, mem_norm, w_mem_kv, w_gate, b_gate, w_o_a, w_o_b, w_o_m, w_out, mix_norm_post, ffn2_norm_pre, ffn2_w_in, ffn2_w_out, ffn2_norm_post, loss_target, m_ffn1_norm_pre, m_ffn1_w_in, m_ffn1_w_out, m_ffn1_norm_post, m_mix_norm_pre, m_w_in, m_sinks, m_mem_norm, m_w_mem_kv, m_w_gate, m_b_gate, m_w_o_a, m_w_o_b, m_w_o_m, m_w_out, m_mix_norm_post, m_ffn2_norm_pre, m_ffn2_w_in, m_ffn2_w_out, m_ffn2_norm_post, v_ffn1_norm_pre, v_ffn1_w_in, v_ffn1_w_out, v_ffn1_norm_post, v_mix_norm_pre, v_w_in, v_sinks, v_mem_norm, v_w_mem_kv, v_w_gate, v_b_gate, v_w_o_a, v_w_o_b, v_w_o_m, v_w_out, v_mix_norm_post, v_ffn2_norm_pre, v_ffn2_w_in, v_ffn2_w_out, v_ffn2_norm_post):
    given = dict(locals())
    wt = {k: given[k] for k in WEIGHTS}
    mom = {k: given["m_" + k] for k in WEIGHTS}
    var = {k: given["v_" + k] for k in WEIGHTS}
    me = (2 * lax.axis_index("x") + lax.axis_index("y")).astype(jnp.int32).reshape(1)

    gathered = chip_all_gather([wt[k][0].astype(BF16) for k in BIG], "weight_gather")
    full = {}
    for k, g in zip(BIG, gathered):
        if k in KEPT_SHARD_MAJOR:
            full[k] = g
        elif k in COLUMN_SHARDED:
            full[k] = jnp.swapaxes(g, 0, 1).reshape(g.shape[1], N_CHIPS * g.shape[2])
        else:
            full[k] = g.reshape(N_CHIPS * g.shape[1], g.shape[2])

    gains = {k: wt[k] for k in GAINS}
    sq, dx, grads = layer_step(x[0], mem[0], loss_target[0], gains, sinks[0], b_gate, full)
    loss = lax.psum(0.5 * sq[0, 0] / D_MODEL, ("x", "y", "c"))

    grads_sm = []
    for k in BIG:
        g = grads[k]
        if k in KEPT_SHARD_MAJOR:
            pass
        elif k in COLUMN_SHARDED:
            g = jnp.swapaxes(g.reshape(g.shape[0], N_CHIPS, g.shape[1] // N_CHIPS), 0, 1)
        else:
            g = g.reshape(N_CHIPS, g.shape[0] // N_CHIPS, g.shape[1])
        grads_sm.append(g)
    small = _pack_small(grads)
    *received, small_all = grad_exchange([g.astype(BF16) for g in grads_sm], small, "grad_exchange")
    parts = [chip_partial_sum(me, g, r, f"{k}_chip_sum") for k, g, r in zip(BIG, grads_sm, received)]
    sibs = sibling_exchange(parts, "sibling_exchange")

    res = {}
    for k, p, s in zip(BIG, parts, sibs):
        res[k] = [t[None] for t in adamw_pair(p, s, wt[k][0], mom[k][0], var[k][0], f"{k}_adamw")]
    packed = adamw_small(small_all, _pack_small(wt), _pack_small(mom), _pack_small(var), "small_adamw")
    for idx, p in enumerate(packed):
        for k, t in _unpack_small(p).items():
            res.setdefault(k, [None] * 4)[idx] = t

    return (loss, dx[None], *[res[k][0] for k in WEIGHTS], *[res[k][1] for k in WEIGHTS],
            *[res[k][2] for k in WEIGHTS], *[res[k][3] for k in WEIGHTS])
```

```python
import functools

import jax
import jax.numpy as jnp
from jax import lax
from jax.experimental import pallas as pl
from jax.experimental.pallas import tpu as pltpu

F32 = jnp.float32
BF16 = jnp.bfloat16

D_MODEL = 1024
D_FF = 2816
HEAD = 128
N_CHIPS = 4
N_DEV = 8
EPS = 1e-6
NEG_INF = -1e30
ROPE_THETA = 10000.0
ATT_SCALE = HEAD ** -0.5

ADAM_LR = 0.001
ADAM_B1 = 0.9
ADAM_B2 = 0.999
ADAM_EPS = 1e-08
ADAM_WD = 0.01
ADAM_STEP = 10

VMEM_LIMIT = 52 * 2 ** 20
MESH = pl.DeviceIdType.MESH

QKV_W = 3840
AQ, AK, AV, BQ, BK, BV, MQ = 0, 6, 12, 18, 22, 24, 26
DIL = ((128, 1), (512, 4), (2048, 16))

TM = 512
FF_T = D_FF // 2


def _params(*sem):
    return pltpu.CompilerParams(dimension_semantics=sem, vmem_limit_bytes=VMEM_LIMIT)


def _dot(a, b):
    return jnp.dot(a, b, preferred_element_type=F32)


def _dot_nt(a, b):
    return lax.dot_general(a, b, (((1,), (1,)), ((), ())), preferred_element_type=F32)


def _dot_tn(a, b):
    return lax.dot_general(a, b, (((0,), (0,)), ((), ())), preferred_element_type=F32)


def _rstd(x):
    return lax.rsqrt(jnp.mean(x * x, axis=-1, keepdims=True) + EPS)


def _ffn_perm(k):
    return (k % 2) * 2 + k // 2


def _resident(arr):
    return pl.BlockSpec(arr.shape, lambda *_: (0,) * arr.ndim, pipeline_mode=pl.Buffered(1))


def ffn_in(h, g, w, name):
    T, D = h.shape

    def body(h_ref, g_ref, w_ref, xn_ref, gu_ref, a_ref):
        x = h_ref[...]
        xn = (x * _rstd(x) * g_ref[...]).astype(BF16)
        xn_ref[...] = xn
        for j in range(2):
            gu = _dot(xn, w_ref[:, j * 2 * FF_T:(j + 1) * 2 * FF_T])
            gu_ref[:, j * 2 * FF_T:(j + 1) * 2 * FF_T] = gu.astype(BF16)
            gate, up = gu[:, :FF_T], gu[:, FF_T:]
            a_ref[:, j * FF_T:(j + 1) * FF_T] = (gate * jax.nn.sigmoid(gate) * up).astype(BF16)

    def rows(width):
        return pl.BlockSpec((TM, width), lambda i: (i, 0))

    return pl.pallas_call(
        body, name=name,
        grid=(T // TM,),
        in_specs=[rows(D), _resident(g), _resident(w)],
        out_specs=[rows(D), rows(2 * D_FF), rows(D_FF)],
        out_shape=[jax.ShapeDtypeStruct((T, D), BF16),
                   jax.ShapeDtypeStruct((T, 2 * D_FF), BF16),
                   jax.ShapeDtypeStruct((T, D_FF), BF16)],
        compiler_params=_params("parallel"),
    )(h, g, w)


def mm_norm_res(a, w, h_in, g, coef, name, target=None):
    T, K = a.shape
    D = w.shape[1]
    final = target is not None

    def body(*refs):
        if final:
            a_ref, w_ref, h_ref, g_ref, t_ref, f_ref, o_ref, l_ref = refs
        else:
            a_ref, w_ref, h_ref, g_ref, f_ref, o_ref = refs
        f = _dot(a_ref[...], w_ref[...])
        f_ref[...] = f
        y = h_ref[...] + coef * (f * _rstd(f) * g_ref[...])
        if final:
            err = y - t_ref[...]
            o_ref[...] = err * (1.0 / D)

            @pl.when(pl.program_id(0) == 0)
            def _():
                l_ref[...] = jnp.zeros_like(l_ref)

            l_ref[...] += jnp.sum(err * err)
        else:
            o_ref[...] = y

    row = pl.BlockSpec((TM, D), lambda i: (i, 0))
    in_specs = [pl.BlockSpec((TM, K), lambda i: (i, 0)),
                _resident(w),
                row, pl.BlockSpec((1, D), lambda i: (0, 0))]
    out_specs = [row, row]
    out_shape = [jax.ShapeDtypeStruct((T, D), F32), jax.ShapeDtypeStruct((T, D), F32)]
    args = [a, w, h_in, g]
    if final:
        in_specs.append(row)
        args.append(target)
        out_specs.append(pl.BlockSpec((8, 128), lambda i: (0, 0)))
        out_shape.append(jax.ShapeDtypeStruct((8, 128), F32))
    return pl.pallas_call(
        body, name=name, grid=(T // TM,), in_specs=in_specs, out_specs=out_specs, out_shape=out_shape,
        compiler_params=_params("arbitrary"),
    )(*args)


def _rope(x, cos, sin_signed):
    return x * cos + pltpu.roll(x, HEAD // 2, axis=1) * sin_signed


def _unrope(x, cos, sin_signed):
    return x * cos - pltpu.roll(x, HEAD // 2, axis=1) * sin_signed


ROTARY_HEADS = tuple(range(AQ, AV)) + tuple(range(BQ, BV))


def mix_in(h, g, w, w_gate, b_gate, cos, sin_signed, name):
    T, D = h.shape
    tn = 768

    def body(h_ref, g_ref, w_ref, wg_ref, b_ref, c_ref, s_ref, u_ref, o_ref, gt_ref):
        x = h_ref[...]
        u = (x * _rstd(x) * g_ref[...]).astype(BF16)
        u_ref[...] = u
        c, s = c_ref[...], s_ref[...]
        for j in range(QKV_W // tn):
            acc = _dot(u, w_ref[:, j * tn:(j + 1) * tn])
            for hd in range(tn // HEAD):
                head = j * (tn // HEAD) + hd
                part = acc[:, hd * HEAD:(hd + 1) * HEAD]
                if head in ROTARY_HEADS:
                    part = _rope(part, c, s)
                o_ref[:, head * HEAD:(head + 1) * HEAD] = part.astype(BF16)
        for j in range(w_gate.shape[1] // tn):
            cols = slice(j * tn, (j + 1) * tn)
            gt_ref[:, cols] = jax.nn.sigmoid(_dot(u, wg_ref[:, cols]) + b_ref[:, cols]).astype(BF16)

    def rows(width):
        return pl.BlockSpec((TM, width), lambda i: (i, 0))

    return pl.pallas_call(
        body, name=name,
        grid=(T // TM,),
        in_specs=[rows(D), _resident(g), _resident(w), _resident(w_gate), _resident(b_gate), rows(HEAD), rows(HEAD)],
        out_specs=[rows(D), rows(QKV_W), rows(w_gate.shape[1])],
        out_shape=[jax.ShapeDtypeStruct((T, D), BF16), jax.ShapeDtypeStruct((T, QKV_W), BF16),
                   jax.ShapeDtypeStruct((T, w_gate.shape[1]), BF16)],
        compiler_params=_params("parallel"),
    )(h, g, w, w_gate, b_gate, cos, sin_signed)


def gate_merge(gt, o_a, o_b, o_m, w_a, w_b, w_m, name):
    T = gt.shape[0]
    D = D_MODEL

    def body(gt_ref, oa_ref, ob_ref, om_ref, wa_ref, wb_ref, wm_ref, out_ref):
        acc = gt_ref[:, :D].astype(F32) * _dot(oa_ref[...], wa_ref[...])
        acc += gt_ref[:, D:2 * D].astype(F32) * _dot(ob_ref[...], wb_ref[...])
        acc += gt_ref[:, 2 * D:].astype(F32) * _dot(om_ref[...], wm_ref[...])
        out_ref[...] = acc.astype(BF16)

    def rows(width):
        return pl.BlockSpec((TM, width), lambda i: (i, 0))

    def whole(arr):
        return pl.BlockSpec(arr.shape, lambda i: (0, 0))

    return pl.pallas_call(
        body, name=name, grid=(T // TM,),
        in_specs=[rows(3 * D), rows(o_a.shape[1]), rows(o_b.shape[1]), rows(o_m.shape[1]),
                  whole(w_a), whole(w_b), whole(w_m)],
        out_specs=rows(D),
        out_shape=jax.ShapeDtypeStruct((T, D), BF16),
        compiler_params=_params("parallel"),
    )(gt, o_a, o_b, o_m, w_a, w_b, w_m)


def _band_rows(start, r):
    return pl.ds(start, HEAD) if r == 1 else pl.ds(start, HEAD, stride=r)


def _band_mask(max_dist, first_has_prev):
    row = lax.broadcasted_iota(jnp.int32, (HEAD, 2 * HEAD), 0)
    col = lax.broadcasted_iota(jnp.int32, (HEAD, 2 * HEAD), 1)
    dist = row + HEAD - col
    band = (dist >= 0) & (dist <= max_dist)
    return band, band & (col >= jnp.where(first_has_prev, 0, HEAD))


def band_fwd(qkv, sinks, *, r, q_off, k_off, v_off, hkv, grp, max_dist, out_dtype, name):
    T, W = qkv.shape
    SB = HEAD * r
    BT = min(2048, T)
    nsub, nib = BT // SB, T // BT
    hq = hkv * grp

    def body(sink_ref, q_ref, kc_ref, kp_ref, vc_ref, vp_ref, o_ref, l_ref, qf, kf, vf):
        kvh, ib = pl.program_id(0), pl.program_id(1)
        qf[...] = q_ref[...].astype(F32)
        kf[:SB] = kp_ref[...].astype(F32)
        kf[SB:] = kc_ref[...].astype(F32)
        vf[:SB] = vp_ref[...].astype(F32)
        vf[SB:] = vc_ref[...].astype(F32)
        band, band_first = _band_mask(max_dist, ib > 0)
        for j in range(nsub):
            mask = band_first if j == 0 else band
            for c in range(r):
                rows = _band_rows(j * SB + c, r)
                older, own = _band_rows(j * SB + c, r), _band_rows((j + 1) * SB + c, r)
                kcat = jnp.concatenate([kf[older], kf[own]], axis=0).astype(BF16)
                vcat = jnp.concatenate([vf[older], vf[own]], axis=0).astype(BF16)
                for gq in range(grp):
                    cols = slice(gq * HEAD, (gq + 1) * HEAD)
                    s = jnp.where(mask, _dot_nt(qf[rows, cols].astype(BF16), kcat) * ATT_SCALE, NEG_INF)
                    sk = sink_ref[kvh * grp + gq]
                    m = jnp.maximum(jnp.max(s, axis=-1, keepdims=True), sk)
                    p = jnp.exp(s - m)
                    tot = jnp.sum(p, axis=-1, keepdims=True) + jnp.exp(sk - m)
                    o_ref[rows, cols] = (_dot(p.astype(BF16), vcat) / tot).astype(out_dtype)
                    l_ref[rows, cols] = jnp.broadcast_to(m + jnp.log(tot), (HEAD, HEAD))

    def cur(off, width):
        return pl.BlockSpec((BT, width * HEAD), lambda h, i: (i, off // width + h))

    def prev(off):
        return pl.BlockSpec((SB, HEAD), lambda h, i: (jnp.maximum(i * nsub - 1, 0), off + h))

    out_spec = pl.BlockSpec((BT, grp * HEAD), lambda h, i: (i, h))
    return pl.pallas_call(
        body, name=name, grid=(hkv, nib),
        in_specs=[pl.BlockSpec(memory_space=pltpu.SMEM),
                  cur(q_off, grp), cur(k_off, 1), prev(k_off), cur(v_off, 1), prev(v_off)],
        out_specs=[out_spec, out_spec],
        out_shape=[jax.ShapeDtypeStruct((T, hq * HEAD), out_dtype), jax.ShapeDtypeStruct((T, hq * HEAD), F32)],
        scratch_shapes=[pltpu.VMEM((BT, grp * HEAD), F32), pltpu.VMEM((SB + BT, HEAD), F32),
                        pltpu.VMEM((SB + BT, HEAD), F32)],
        compiler_params=_params("parallel", "arbitrary"),
    )(sinks, qkv, qkv, qkv, qkv, qkv)


def band_bwd(qkv, do, o, lse, cos, sin_signed, sinks, *, r, q_off, k_off, v_off, hkv, grp, max_dist, name):
    T, W = qkv.shape
    SB = HEAD * r
    BT = min(2048, T)
    nsub, nib = BT // SB, T // BT
    nblk = T // SB
    hq = hkv * grp
    with_sink = sinks is not None

    def body(*refs):
        if with_sink:
            sink_ref, refs = refs[0], refs[1:]
        (q_ref, qn_ref, kc_ref, kp_ref, vc_ref, vp_ref, do_ref, don_ref, o_ref, on_ref, l_ref, ln_ref,
         c_ref, s_ref) = refs[:14]
        dq_ref, dk_ref, dv_ref = refs[14:17]
        ds_ref = refs[17] if with_sink else None
        qf, dof, of, kf, vf, dqf, dkacc, dvacc = refs[-8:]
        kvh, ib = pl.program_id(0), pl.program_id(1)
        for buf, cur_ref, nxt_ref in ((qf, q_ref, qn_ref), (dof, do_ref, don_ref), (of, o_ref, on_ref)):
            buf[:BT] = cur_ref[...].astype(F32)
            buf[BT:] = nxt_ref[...].astype(F32)
        kf[:SB] = kp_ref[...].astype(F32)
        kf[SB:] = kc_ref[...].astype(F32)
        vf[:SB] = vp_ref[...].astype(F32)
        vf[SB:] = vc_ref[...].astype(F32)
        dkacc[...] = jnp.zeros_like(dkacc)
        dvacc[...] = jnp.zeros_like(dvacc)
        band, band_first = _band_mask(max_dist, ib > 0)
        if with_sink:
            @pl.when(ib == 0)
            def _():
                ds_ref[...] = jnp.zeros_like(ds_ref)

        def grads(rows, cols, logz, keys, vals, mask):
            q, dout = qf[rows, cols].astype(BF16), dof[rows, cols].astype(BF16)
            delta = jnp.sum(dof[rows, cols] * of[rows, cols], axis=-1, keepdims=True)
            s = jnp.where(mask, _dot_nt(q, keys) * ATT_SCALE, NEG_INF)
            p = jnp.exp(s - logz[:, :1])
            ds = (p * (_dot_nt(dout, vals) - delta) * ATT_SCALE).astype(BF16)
            return q, dout, p.astype(BF16), ds, delta

        for j in range(nsub):
            mask = band_first if j == 0 else band
            for c in range(r):
                rows = _band_rows(j * SB + c, r)
                older, own = _band_rows(j * SB + c, r), _band_rows((j + 1) * SB + c, r)
                kcat = jnp.concatenate([kf[older], kf[own]], axis=0).astype(BF16)
                vcat = jnp.concatenate([vf[older], vf[own]], axis=0).astype(BF16)
                for gq in range(grp):
                    cols = slice(gq * HEAD, (gq + 1) * HEAD)
                    logz = l_ref[rows, cols]
                    q, dout, p, ds, delta = grads(rows, cols, logz, kcat, vcat, mask)
                    dqf[rows, cols] = _dot(ds, kcat)
                    dk = _dot_tn(ds, q)
                    dv = _dot_tn(p, dout)
                    dkacc[older] += dk[:HEAD]
                    dkacc[own] += dk[HEAD:]
                    dvacc[older] += dv[:HEAD]
                    dvacc[own] += dv[HEAD:]
                    if with_sink:
                        p_sink = jnp.exp(sink_ref[kvh * grp + gq] - logz[:, :1])
                        ds_ref[gq * 8:(gq + 1) * 8] += jnp.sum(p_sink * delta)

        row = lax.broadcasted_iota(jnp.int32, (HEAD, HEAD), 0)
        col = lax.broadcasted_iota(jnp.int32, (HEAD, HEAD), 1)
        reach = col >= row + jnp.where(ib < nib - 1, HEAD - max_dist, 2 * HEAD)
        for c in range(r):
            last = _band_rows(BT + c, r)
            keys, vals = kf[last].astype(BF16), vf[last].astype(BF16)
            for gq in range(grp):
                cols = slice(gq * HEAD, (gq + 1) * HEAD)
                q, dout, p, ds, _ = grads(last, cols, ln_ref[_band_rows(c, r), cols], keys, vals, reach)
                dkacc[last] += _dot_tn(ds, q)
                dvacc[last] += _dot_tn(p, dout)

        cs, sn = c_ref[...], s_ref[...]
        for gq in range(grp):
            cols = slice(gq * HEAD, (gq + 1) * HEAD)
            dq_ref[:, cols] = _unrope(dqf[:, cols], cs, sn).astype(BF16)
        dk_ref[...] = _unrope(dkacc[SB:], cs, sn).astype(BF16)
        dv_ref[...] = dvacc[SB:].astype(BF16)

    def cur(off, width):
        return pl.BlockSpec((BT, width * HEAD), lambda h, i: (i, off // width + h))

    def prev(off):
        return pl.BlockSpec((SB, HEAD), lambda h, i: (jnp.maximum(i * nsub - 1, 0), off + h))

    def nxt_row(i):
        return jnp.minimum((i + 1) * nsub, nblk - 1)

    q_next = pl.BlockSpec((SB, grp * HEAD), lambda h, i: (nxt_row(i), q_off // grp + h))
    head_cur = pl.BlockSpec((BT, grp * HEAD), lambda h, i: (i, h))
    head_next = pl.BlockSpec((SB, grp * HEAD), lambda h, i: (nxt_row(i), h))
    table = pl.BlockSpec((BT, HEAD), lambda h, i: (i, 0))
    kv_out = pl.BlockSpec((BT, HEAD), lambda h, i: (i, h))

    in_specs = [cur(q_off, grp), q_next, cur(k_off, 1), prev(k_off), cur(v_off, 1), prev(v_off),
                head_cur, head_next, head_cur, head_next, head_cur, head_next, table, table]
    args = [qkv, qkv, qkv, qkv, qkv, qkv, do, do, o, o, lse, lse, cos, sin_signed]
    out_specs = [head_cur, kv_out, kv_out]
    out_shape = [jax.ShapeDtypeStruct((T, hq * HEAD), BF16), jax.ShapeDtypeStruct((T, hkv * HEAD), BF16),
                 jax.ShapeDtypeStruct((T, hkv * HEAD), BF16)]
    if with_sink:
        in_specs.insert(0, pl.BlockSpec(memory_space=pltpu.SMEM))
        args.insert(0, sinks)
        out_specs.append(pl.BlockSpec((None, grp * 8, HEAD), lambda h, i: (h, 0, 0)))
        out_shape.append(jax.ShapeDtypeStruct((hkv, grp * 8, HEAD), F32))
    wide = pltpu.VMEM((BT + SB, grp * HEAD), F32)
    tall = pltpu.VMEM((SB + BT, HEAD), F32)
    return pl.pallas_call(
        body, name=name, grid=(hkv, nib), in_specs=in_specs, out_specs=out_specs, out_shape=out_shape,
        scratch_shapes=[wide, wide, wide, tall, tall, pltpu.VMEM((BT, grp * HEAD), F32), tall, tall],
        compiler_params=_params("parallel", "arbitrary"),
    )(*args)


def merge_groups(outs, lses, name):
    T, Wd = outs[0].shape
    tm = 1024

    def body(o0, o1, o2, l0, l1, l2, out_ref, lt_ref):
        a, b, c = l0[...], l1[...], l2[...]
        m = jnp.maximum(jnp.maximum(a, b), c)
        wa, wb, wc = jnp.exp(a - m), jnp.exp(b - m), jnp.exp(c - m)
        z = wa + wb + wc
        out_ref[...] = ((wa * o0[...] + wb * o1[...] + wc * o2[...]) / z).astype(BF16)
        lt_ref[...] = m + jnp.log(z)

    spec = pl.BlockSpec((tm, Wd), lambda i: (i, 0))
    return pl.pallas_call(
        body, name=name, grid=(T // tm,), in_specs=[spec] * 6, out_specs=[spec, spec],
        out_shape=[jax.ShapeDtypeStruct((T, Wd), BF16), jax.ShapeDtypeStruct((T, Wd), F32)],
        compiler_params=_params("parallel"),
    )(*outs, *lses)


M_HEADS = 4


def mem_kv(mem, g, w, name):
    n, D = mem.shape

    def body(m_ref, g_ref, w_ref, mn_ref, kv_ref):
        x = m_ref[...]
        mn = (x * _rstd(x) * g_ref[...]).astype(BF16)
        mn_ref[...] = mn
        kv_ref[...] = _dot(mn, w_ref[...]).astype(BF16)

    return pl.pallas_call(
        body, name=name,
        out_shape=[jax.ShapeDtypeStruct((n, D), BF16), jax.ShapeDtypeStruct((n, w.shape[1]), BF16)],
        compiler_params=pltpu.CompilerParams(vmem_limit_bytes=VMEM_LIMIT),
    )(mem, g, w)


def mem_fwd(qkv, mkv, name):
    T = qkv.shape[0]
    n = mkv.shape[0]
    RB = 1024

    def body(q_ref, k_ref, v_ref, o_ref, l_ref):
        s = _dot_nt(q_ref[...], k_ref[...]) * ATT_SCALE
        m = jnp.max(s, axis=-1, keepdims=True)
        p = jnp.exp(s - m)
        den = jnp.sum(p, axis=-1, keepdims=True)
        o_ref[...] = (_dot(p.astype(BF16), v_ref[...]) / den).astype(BF16)
        l_ref[...] = jnp.broadcast_to(m + jnp.log(den), (RB, HEAD))

    out = pl.BlockSpec((RB, HEAD), lambda h, i: (i, h))
    return pl.pallas_call(
        body, name=name, grid=(M_HEADS, T // RB),
        in_specs=[pl.BlockSpec((RB, HEAD), lambda h, i: (i, MQ + h)),
                  pl.BlockSpec((n, HEAD), lambda h, i: (0, h)),
                  pl.BlockSpec((n, HEAD), lambda h, i: (0, M_HEADS + h))],
        out_specs=[out, out],
        out_shape=[jax.ShapeDtypeStruct((T, M_HEADS * HEAD), BF16), jax.ShapeDtypeStruct((T, M_HEADS * HEAD), F32)],
        compiler_params=_params("parallel", "parallel"),
    )(qkv, mkv, mkv)


def mem_bwd(qkv, mkv, do, o, lse, name):
    T = qkv.shape[0]
    n = mkv.shape[0]
    RB = 1024

    def body(q_ref, k_ref, v_ref, do_ref, o_ref, l_ref, dq_ref, dk_ref, dv_ref):
        @pl.when(pl.program_id(1) == 0)
        def _():
            dk_ref[...] = jnp.zeros_like(dk_ref)
            dv_ref[...] = jnp.zeros_like(dv_ref)

        q, dout = q_ref[...], do_ref[...]
        delta = jnp.sum(dout.astype(F32) * o_ref[...].astype(F32), axis=-1, keepdims=True)
        p = jnp.exp(_dot_nt(q, k_ref[...]) * ATT_SCALE - l_ref[...][:, :1])
        ds = (p * (_dot_nt(dout, v_ref[...]) - delta) * ATT_SCALE).astype(BF16)
        dq_ref[...] = _dot(ds, k_ref[...]).astype(BF16)
        dk_ref[...] += _dot_tn(ds, q)
        dv_ref[...] += _dot_tn(p.astype(BF16), dout)

    tok = pl.BlockSpec((RB, HEAD), lambda h, i: (i, h))
    slot = pl.BlockSpec((n, HEAD), lambda h, i: (0, h))
    return pl.pallas_call(
        body, name=name, grid=(M_HEADS, T // RB),
        in_specs=[pl.BlockSpec((RB, HEAD), lambda h, i: (i, MQ + h)),
                  slot, pl.BlockSpec((n, HEAD), lambda h, i: (0, M_HEADS + h)), tok, tok, tok],
        out_specs=[tok, slot, slot],
        out_shape=[jax.ShapeDtypeStruct((T, M_HEADS * HEAD), BF16),
                   jax.ShapeDtypeStruct((n, M_HEADS * HEAD), F32),
                   jax.ShapeDtypeStruct((n, M_HEADS * HEAD), F32)],
        compiler_params=_params("parallel", "arbitrary"),
    )(qkv, mkv, mkv, do, o, lse)


def mem_kv_bwd(mem, g, mem_n, w, dmkv, name):
    n, D = mem.shape

    def body(m_ref, g_ref, mn_ref, w_ref, d_ref, dw_ref, dg_ref):
        d = d_ref[...].astype(BF16)
        dw_ref[...] = _dot_tn(mn_ref[...], d)
        x = m_ref[...]
        dg_ref[...] = jnp.sum(_dot_nt(d, w_ref[...]) * (x * _rstd(x)), axis=0, keepdims=True)

    return pl.pallas_call(
        body, name=name,
        out_shape=[jax.ShapeDtypeStruct(w.shape, F32), jax.ShapeDtypeStruct((1, D), F32)],
        compiler_params=pltpu.CompilerParams(vmem_limit_bytes=VMEM_LIMIT),
    )(mem, g, mem_n, w, dmkv)


def _rms_bwd(dn, f, g):
    r = _rstd(f)
    fhat = f * r
    dfhat = dn * g
    df = r * (dfhat - fhat * jnp.mean(dfhat * fhat, axis=-1, keepdims=True))
    return df, jnp.sum(dn * fhat, axis=0, keepdims=True)


def ffn_out_bwd(dh, f, g, w_out, gu, coef, name):
    T, D = dh.shape

    def body(dh_ref, f_ref, g_ref, w_ref, gu_ref, df_ref, dgu_ref, a_ref, dg_ref):
        i, j = pl.program_id(0), pl.program_id(1)

        @pl.when(j == 0)
        def _():
            df, dg = _rms_bwd(coef * dh_ref[...], f_ref[...], g_ref[...])
            df_ref[...] = df.astype(BF16)

            @pl.when(i == 0)
            def _():
                dg_ref[...] = jnp.zeros_like(dg_ref)

            dg_ref[...] += dg

        da = _dot_nt(df_ref[...], w_ref[...])
        gate = gu_ref[:, :FF_T].astype(F32)
        up = gu_ref[:, FF_T:].astype(F32)
        sig = jax.nn.sigmoid(gate)
        silu = gate * sig
        dgu_ref[:, :FF_T] = (da * up * sig * (1.0 + gate * (1.0 - sig))).astype(BF16)
        dgu_ref[:, FF_T:] = (da * silu).astype(BF16)
        a_ref[...] = (silu * up).astype(BF16)

    row = pl.BlockSpec((TM, D), lambda i, j: (i, 0))
    wide = pl.BlockSpec((TM, 2 * FF_T), lambda i, j: (i, j))
    return pl.pallas_call(
        body, name=name, grid=(T // TM, 2),
        in_specs=[row, row, pl.BlockSpec((1, D), lambda i, j: (0, 0)),
                  pl.BlockSpec((FF_T, D), lambda i, j: (j, 0)), wide],
        out_specs=[row, wide, pl.BlockSpec((TM, FF_T), lambda i, j: (i, j)),
                   pl.BlockSpec((1, D), lambda i, j: (0, 0))],
        out_shape=[jax.ShapeDtypeStruct((T, D), BF16), jax.ShapeDtypeStruct((T, 2 * D_FF), BF16),
                   jax.ShapeDtypeStruct((T, D_FF), BF16), jax.ShapeDtypeStruct((1, D), F32)],
        compiler_params=_params("arbitrary", "arbitrary"),
    )(dh, f, g, w_out, gu)


def mix_out_bwd(dh, f, g, w_out, name):
    T, D = dh.shape

    def body(dh_ref, f_ref, g_ref, w_ref, df_ref, dm_ref, dg_ref):
        df, dg = _rms_bwd(dh_ref[...], f_ref[...], g_ref[...])
        df = df.astype(BF16)
        df_ref[...] = df

        @pl.when(pl.program_id(0) == 0)
        def _():
            dg_ref[...] = jnp.zeros_like(dg_ref)

        dg_ref[...] += dg
        dm_ref[...] = _dot_nt(df, w_ref[...]).astype(BF16)

    row = pl.BlockSpec((TM, D), lambda i: (i, 0))
    vec = pl.BlockSpec((1, D), lambda i: (0, 0))
    return pl.pallas_call(
        body, name=name, grid=(T // TM,),
        in_specs=[row, row, vec, _resident(w_out)],
        out_specs=[row, row, vec],
        out_shape=[jax.ShapeDtypeStruct((T, D), BF16), jax.ShapeDtypeStruct((T, D), BF16),
                   jax.ShapeDtypeStruct((1, D), F32)],
        compiler_params=_params("arbitrary"),
    )(dh, f, g, w_out)


def mm_nt_norm_bwd(pieces, h_in, dh_out, g, name):
    T, D = h_in.shape

    def body(*refs):
        ab = refs[:2 * len(pieces)]
        h_ref, dh_ref, g_ref, o_ref, dg_ref = refs[2 * len(pieces):]
        dxn = _dot_nt(ab[0][...], ab[1][...])
        for p in range(1, len(pieces)):
            dxn += _dot_nt(ab[2 * p][...], ab[2 * p + 1][...])
        h = h_ref[...]
        r = _rstd(h)
        xhat = h * r
        dxhat = dxn * g_ref[...]
        o_ref[...] = dh_ref[...] + r * (dxhat - xhat * jnp.mean(dxhat * xhat, axis=-1, keepdims=True))

        @pl.when(pl.program_id(0) == 0)
        def _():
            dg_ref[...] = jnp.zeros_like(dg_ref)

        dg_ref[...] += jnp.sum(dxn * xhat, axis=0, keepdims=True)

    in_specs, args = [], []
    for a, w in pieces:
        in_specs += [pl.BlockSpec((TM, a.shape[1]), lambda i: (i, 0)), _resident(w)]
        args += [a, w]
    row = pl.BlockSpec((TM, D), lambda i: (i, 0))
    return pl.pallas_call(
        body, name=name, grid=(T // TM,),
        in_specs=in_specs + [row, row, _resident(g)],
        out_specs=[row, pl.BlockSpec((1, D), lambda i: (0, 0))],
        out_shape=[jax.ShapeDtypeStruct((T, D), F32), jax.ShapeDtypeStruct((1, D), F32)],
        compiler_params=_params("arbitrary"),
    )(*args, h_in, dh_out, g)


def gate_merge_bwd(dm, gt, o_a, o_b, o_m, w_a, w_b, w_m, name):
    T = dm.shape[0]
    D = D_MODEL
    branch = ((o_a, w_a), (o_b, w_b), (o_m, w_m))

    def body(dm_ref, gt_ref, oa_ref, ob_ref, om_ref, wa_ref, wb_ref, wm_ref,
             dgt_ref, dpa_ref, dpb_ref, dpm_ref, doa_ref, dob_ref, dom_ref, db_ref):
        @pl.when(pl.program_id(0) == 0)
        def _():
            db_ref[...] = jnp.zeros_like(db_ref)

        dmf = dm_ref[...].astype(F32)
        for x, (o_ref, w_ref, dp_ref, do_ref) in enumerate(((oa_ref, wa_ref, dpa_ref, doa_ref),
                                                           (ob_ref, wb_ref, dpb_ref, dob_ref),
                                                           (om_ref, wm_ref, dpm_ref, dom_ref))):
            cols = slice(x * D, (x + 1) * D)
            gx = gt_ref[:, cols].astype(F32)
            w = w_ref[...]
            dpre = dmf * _dot(o_ref[...], w) * gx * (1.0 - gx)
            dgt_ref[:, cols] = dpre.astype(BF16)
            db_ref[:, cols] += jnp.sum(dpre, axis=0, keepdims=True)
            dp = (dmf * gx).astype(BF16)
            dp_ref[...] = dp
            do_ref[...] = _dot_nt(dp, w).astype(BF16)

    def rows(width):
        return pl.BlockSpec((TM, width), lambda i: (i, 0))

    def whole(arr):
        return pl.BlockSpec(arr.shape, lambda i: (0, 0))

    widths = [o.shape[1] for o, _ in branch]
    return pl.pallas_call(
        body, name=name, grid=(T // TM,),
        in_specs=[rows(D), rows(3 * D)] + [rows(k) for k in widths] + [whole(w) for _, w in branch],
        out_specs=[rows(3 * D), rows(D), rows(D), rows(D)] + [rows(k) for k in widths]
                  + [pl.BlockSpec((1, 3 * D), lambda i: (0, 0))],
        out_shape=[jax.ShapeDtypeStruct((T, 3 * D), BF16)] + [jax.ShapeDtypeStruct((T, D), BF16)] * 3
                  + [jax.ShapeDtypeStruct((T, k), BF16) for k in widths]
                  + [jax.ShapeDtypeStruct((1, 3 * D), F32)],
        compiler_params=_params("arbitrary"),
    )(dm, gt, o_a, o_b, o_m, w_a, w_b, w_m)


def mm_tn(x, dy, tm, tn, name, shard_major=False, perm=None):
    T, M = x.shape
    N = dy.shape[1]
    tk = min(1024, T)
    perm = perm or (lambda j: j)

    def body(x_ref, dy_ref, o_ref):
        @pl.when(pl.program_id(2) == 0)
        def _():
            o_ref[...] = jnp.zeros_like(o_ref)

        o_ref[...] += _dot_tn(x_ref[...], dy_ref[...])

    if shard_major:
        out_spec = pl.BlockSpec((None, tm, tn), lambda i, j, k: (perm(j), i, 0))
        out_shape = jax.ShapeDtypeStruct((N // tn, M, tn), F32)
    else:
        out_spec = pl.BlockSpec((tm, tn), lambda i, j, k: (i, j))
        out_shape = jax.ShapeDtypeStruct((M, N), F32)
    return pl.pallas_call(
        body, name=name, grid=(M // tm, N // tn, T // tk),
        in_specs=[pl.BlockSpec((tk, tm), lambda i, j, k: (k, i)),
                  pl.BlockSpec((tk, tn), lambda i, j, k: (k, j))],
        out_specs=out_spec, out_shape=out_shape,
        compiler_params=_params("parallel", "parallel", "arbitrary"),
    )(x, dy)


def rope_tables(T):
    half = HEAD // 2
    inv = ROPE_THETA ** (-jnp.arange(half, dtype=F32) / half)
    ang = jnp.arange(T).astype(F32)[:, None] * inv[None, :]
    cos, sin = jnp.cos(ang), jnp.sin(ang)
    return jnp.concatenate([cos, cos], axis=1), jnp.concatenate([-sin, sin], axis=1)


def layer_step(x, mem, target, gains, sinks, b_gate, w):
    T = x.shape[0]
    cos, sin_signed = rope_tables(T)
    no_sink = jnp.full((2,), NEG_INF, F32)

    xn1, gu1, a1 = ffn_in(x, gains["ffn1_norm_pre"], w["ffn1_w_in"], "ffn1_in")
    f1, h1 = mm_norm_res(a1, w["ffn1_w_out"], x, gains["ffn1_norm_post"], 0.5, "ffn1_out")
    u, qkv, gt = mix_in(h1, gains["mix_norm_pre"], w["w_in"], w["w_gate"], b_gate, cos, sin_signed, "mix_in")
    outs, lses = [], []
    for gidx, (window, dil) in enumerate(DIL):
        o_g, l_g = band_fwd(qkv, no_sink, r=dil, q_off=AQ + 2 * gidx, k_off=AK + 2 * gidx, v_off=AV + 2 * gidx,
                            hkv=2, grp=1, max_dist=window // dil, out_dtype=F32, name=f"attn_a{gidx}_fwd")
        outs.append(o_g)
        lses.append(l_g)
    o_a, l_a = merge_groups(outs, lses, "attn_a_merge")
    o_b, l_b = band_fwd(qkv, sinks, r=1, q_off=BQ, k_off=BK, v_off=BV, hkv=2, grp=2, max_dist=HEAD - 1,
                        out_dtype=BF16, name="attn_b_fwd")
    mem_n, mkv = mem_kv(mem, gains["mem_norm"], w["w_mem_kv"], "mem_kv")
    o_m, l_m = mem_fwd(qkv, mkv, "attn_m_fwd")
    merged = gate_merge(gt, o_a, o_b, o_m, w["w_o_a"], w["w_o_b"], w["w_o_m"], "gate_merge")
    mo, h2 = mm_norm_res(merged, w["w_out"], h1, gains["mix_norm_post"], 1.0, "mix_out")
    xn2, gu2, a2 = ffn_in(h2, gains["ffn2_norm_pre"], w["ffn2_w_in"], "ffn2_in")
    f2, dy, sq = mm_norm_res(a2, w["ffn2_w_out"], h2, gains["ffn2_norm_post"], 0.5, "ffn2_out", target=target)
    del a1, a2

    grads = {}

    def ffn_bwd(tag, dh_out, f, gu, xn, h_in):
        df, dgu, a, grads[f"{tag}_norm_post"] = ffn_out_bwd(
            dh_out, f, gains[f"{tag}_norm_post"], w[f"{tag}_w_out"], gu, 0.5, f"{tag}_out_bwd")
        grads[f"{tag}_w_out"] = mm_tn(a, df, FF_T, D_MODEL, f"{tag}_w_out_grad")
        grads[f"{tag}_w_in"] = mm_tn(xn, dgu, D_MODEL, FF_T, f"{tag}_w_in_grad", shard_major=True, perm=_ffn_perm)
        dh_in, grads[f"{tag}_norm_pre"] = mm_nt_norm_bwd(
            [(dgu, w[f"{tag}_w_in"])], h_in, dh_out, gains[f"{tag}_norm_pre"], f"{tag}_in_bwd")
        return dh_in

    dh2 = ffn_bwd("ffn2", dy, f2, gu2, xn2, h2)

    dmo, dmerged, grads["mix_norm_post"] = mix_out_bwd(dh2, mo, gains["mix_norm_post"], w["w_out"], "mix_out_bwd")
    grads["w_out"] = mm_tn(merged, dmo, D_MODEL, D_MODEL, "w_out_grad")
    dgt, dpa, dpb, dpm, do_a, do_b, do_m, grads["b_gate"] = gate_merge_bwd(
        dmerged, gt, o_a, o_b, o_m, w["w_o_a"], w["w_o_b"], w["w_o_m"], "gate_merge_bwd")
    grads["w_o_a"] = mm_tn(o_a, dpa, o_a.shape[1], D_MODEL, "w_o_a_grad")
    grads["w_o_b"] = mm_tn(o_b, dpb, o_b.shape[1], D_MODEL, "w_o_b_grad")
    grads["w_o_m"] = mm_tn(o_m, dpm, o_m.shape[1], D_MODEL, "w_o_m_grad")

    dq_a, dk_a, dv_a = [], [], []
    for gidx, (window, dil) in enumerate(DIL):
        dq, dk, dv = band_bwd(qkv, do_a, o_a, l_a, cos, sin_signed, None, r=dil, q_off=AQ + 2 * gidx,
                              k_off=AK + 2 * gidx, v_off=AV + 2 * gidx, hkv=2, grp=1, max_dist=window // dil,
                              name=f"attn_a{gidx}_bwd")
        dq_a.append(dq)
        dk_a.append(dk)
        dv_a.append(dv)
    dq_b, dk_b, dv_b, dsink = band_bwd(qkv, do_b, o_b, l_b, cos, sin_signed, sinks, r=1, q_off=BQ, k_off=BK,
                                       v_off=BV, hkv=2, grp=2, max_dist=HEAD - 1, name="attn_b_bwd")
    grads["sinks"] = -dsink[:, ::8, 0].reshape(1, 4)
    dq_m, dmk, dmv = mem_bwd(qkv, mkv, do_m, o_m, l_m, "attn_m_bwd")
    grads["w_mem_kv"], grads["mem_norm"] = mem_kv_bwd(
        mem, gains["mem_norm"], mem_n, w["w_mem_kv"], jnp.concatenate([dmk, dmv], axis=1), "mem_kv_bwd")
    dqkv = jnp.concatenate(dq_a + dk_a + dv_a + [dq_b, dk_b, dv_b, dq_m], axis=1)

    grads["w_in"] = mm_tn(u, dqkv, D_MODEL, 1280, "w_in_grad")
    grads["w_gate"] = mm_tn(u, dgt, D_MODEL, 768, "w_gate_grad", shard_major=True)
    dh1, grads["mix_norm_pre"] = mm_nt_norm_bwd(
        [(dqkv, w["w_in"]), (dgt, w["w_gate"])], h1, dh2, gains["mix_norm_pre"], "mix_in_bwd")

    dx = ffn_bwd("ffn1", dh1, f1, gu1, xn1, x)
    return sq, dx, grads


def _place():
    return lax.axis_index("x"), lax.axis_index("y"), lax.axis_index("c")


def _other_chips(x, y):
    return [(1 - x, y), (x, 1 - y), (1 - x, 1 - y)]


def _hbm(n):
    return [pl.BlockSpec(memory_space=pltpu.HBM)] * n


def chip_all_gather(shards, name):
    n = len(shards)

    def body(*refs):
        ins, outs = refs[:n], refs[n:2 * n]
        send_sems, recv_sems, local_sems = refs[2 * n:]
        x, y, c = _place()
        me = 2 * x + y
        chips = _other_chips(x, y)

        def copy(i, j, slot):
            px, py = chips[j]
            return pltpu.make_async_remote_copy(
                src_ref=ins[i], dst_ref=outs[i].at[slot], send_sem=send_sems.at[3 * i + j],
                recv_sem=recv_sems.at[3 * i + j], device_id=(px, py, c), device_id_type=MESH)

        local = [pltpu.make_async_copy(ins[i], outs[i].at[me], local_sems.at[i]) for i in range(n)]
        for i in range(n):
            local[i].start()
            for j in range(3):
                copy(i, j, me).start()
        for i in range(n):
            for j, (px, py) in enumerate(chips):
                copy(i, j, 2 * px + py).wait()
            local[i].wait()

    return pl.pallas_call(
        body, name=name, in_specs=_hbm(n), out_specs=_hbm(n),
        out_shape=[jax.ShapeDtypeStruct((N_CHIPS,) + s.shape, s.dtype) for s in shards],
        scratch_shapes=[pltpu.SemaphoreType.DMA((3 * n,)), pltpu.SemaphoreType.DMA((3 * n,)),
                        pltpu.SemaphoreType.DMA((n,))],
    )(*shards)


def grad_exchange(grads_sm, small, name):
    n = len(grads_sm)
    flips = [(fx, fy, fc) for fx in (0, 1) for fy in (0, 1) for fc in (0, 1)][1:]

    def body(*refs):
        ins, small_in = refs[:n], refs[n]
        outs, small_out = refs[n + 1:2 * n + 1], refs[2 * n + 1]
        send_sems, recv_sems, local_sems = refs[2 * n + 2:]
        x, y, c = _place()
        me = 2 * x + y
        chips = _other_chips(x, y)

        def copy(i, j, src_slot, dst_slot):
            px, py = chips[j]
            return pltpu.make_async_remote_copy(
                src_ref=ins[i].at[src_slot], dst_ref=outs[i].at[dst_slot], send_sem=send_sems.at[3 * i + j],
                recv_sem=recv_sems.at[3 * i + j], device_id=(px, py, c), device_id_type=MESH)

        def small_copy(k, slot):
            fx, fy, fc = flips[k]
            return pltpu.make_async_remote_copy(
                src_ref=small_in, dst_ref=small_out.at[slot], send_sem=send_sems.at[3 * n + k],
                recv_sem=recv_sems.at[3 * n + k],
                device_id=(x ^ fx, y ^ fy, c ^ fc), device_id_type=MESH)

        def dev(k):
            fx, fy, fc = flips[k]
            return 4 * (x ^ fx) + 2 * (y ^ fy) + (c ^ fc)

        local = [pltpu.make_async_copy(ins[i].at[me], outs[i].at[me], local_sems.at[i]) for i in range(n)]
        local.append(pltpu.make_async_copy(small_in, small_out.at[4 * x + 2 * y + c], local_sems.at[n]))
        for k in range(len(flips)):
            small_copy(k, 4 * x + 2 * y + c).start()
        for i in range(n):
            local[i].start()
            for j, (px, py) in enumerate(chips):
                copy(i, j, 2 * px + py, me).start()
        local[n].start()
        for k in range(len(flips)):
            small_copy(k, dev(k)).wait()
        for i in range(n):
            for j, (px, py) in enumerate(chips):
                copy(i, j, 2 * px + py, 2 * px + py).wait()
            local[i].wait()
        local[n].wait()

    nsem = 3 * n + len(flips)
    return pl.pallas_call(
        body, name=name, in_specs=_hbm(n + 1), out_specs=_hbm(n + 1),
        out_shape=[jax.ShapeDtypeStruct(g.shape, g.dtype) for g in grads_sm]
                  + [jax.ShapeDtypeStruct((N_DEV,) + small.shape, small.dtype)],
        scratch_shapes=[pltpu.SemaphoreType.DMA((nsem,)), pltpu.SemaphoreType.DMA((nsem,)),
                        pltpu.SemaphoreType.DMA((n + 1,))],
    )(*grads_sm, small)


def sibling_exchange(parts, name):
    n = len(parts)

    def body(*refs):
        ins, outs = refs[:n], refs[n:2 * n]
        send_sems, recv_sems = refs[2 * n:]
        x, y, c = _place()
        copies = [pltpu.make_async_remote_copy(
            src_ref=ins[i], dst_ref=outs[i], send_sem=send_sems.at[i], recv_sem=recv_sems.at[i],
            device_id=(x, y, 1 - c), device_id_type=MESH) for i in range(n)]
        for cp in copies:
            cp.start()
        for cp in copies:
            cp.wait()

    return pl.pallas_call(
        body, name=name, in_specs=_hbm(n), out_specs=_hbm(n),
        out_shape=[jax.ShapeDtypeStruct(p.shape, p.dtype) for p in parts],
        scratch_shapes=[pltpu.SemaphoreType.DMA((n,)), pltpu.SemaphoreType.DMA((n,))],
    )(*parts)


def _row_tile(rows):
    for t in (256, 176, 128, 64, 32, 16, 8):
        if rows % t == 0:
            return t
    return rows


def chip_partial_sum(me, own_sm, recv, name):
    _, rows, cols = own_sm.shape
    tr = _row_tile(rows)

    def body(me_ref, own_ref, r0, r1, r2, r3, o_ref):
        acc = jnp.zeros((tr, cols), F32)
        for s, r_ref in enumerate((r0, r1, r2, r3)):
            acc = acc + jnp.where(me_ref[0] == s, own_ref[...], r_ref[...].astype(F32))
        o_ref[...] = acc

    def slot(s):
        return pl.BlockSpec((None, tr, cols), lambda i, me_ref, s=s: (s, i, 0))

    return pl.pallas_call(
        body, name=name,
        grid_spec=pltpu.PrefetchScalarGridSpec(
            num_scalar_prefetch=1, grid=(rows // tr,),
            in_specs=[pl.BlockSpec((None, tr, cols), lambda i, me_ref: (me_ref[0], i, 0))] + [slot(s) for s in range(4)],
            out_specs=pl.BlockSpec((tr, cols), lambda i, me_ref: (i, 0))),
        out_shape=jax.ShapeDtypeStruct((rows, cols), F32),
        compiler_params=_params("parallel"),
    )(me, own_sm, recv, recv, recv, recv)


def _adamw(w, g, m, v):
    m = ADAM_B1 * m + (1.0 - ADAM_B1) * g
    v = ADAM_B2 * v + (1.0 - ADAM_B2) * (g * g)
    m_hat = m / (1.0 - ADAM_B1 ** ADAM_STEP)
    v_hat = v / (1.0 - ADAM_B2 ** ADAM_STEP)
    delta = -ADAM_LR * (m_hat / (jnp.sqrt(v_hat) + ADAM_EPS) + ADAM_WD * w)
    return delta, m, v


def adamw_pair(part, sib, w, m, v, name):
    rows, cols = w.shape
    tr = _row_tile(rows)

    def body(p_ref, s_ref, w_ref, m_ref, v_ref, g_ref, d_ref, nm_ref, nv_ref):
        g = p_ref[...] + s_ref[...]
        g_ref[...] = g
        d_ref[...], nm_ref[...], nv_ref[...] = _adamw(w_ref[...], g, m_ref[...], v_ref[...])

    spec = pl.BlockSpec((tr, cols), lambda i: (i, 0))
    return pl.pallas_call(
        body, name=name, grid=(rows // tr,), in_specs=[spec] * 5, out_specs=[spec] * 4,
        out_shape=[jax.ShapeDtypeStruct((rows, cols), F32)] * 4,
        compiler_params=_params("parallel"),
    )(part, sib, w, m, v)


def adamw_small(g_all, w, m, v, name):
    def body(ga_ref, w_ref, m_ref, v_ref, g_ref, d_ref, nm_ref, nv_ref):
        g = ga_ref[0]
        for k in range(1, N_DEV):
            g = g + ga_ref[k]
        g_ref[...] = g
        d_ref[...], nm_ref[...], nv_ref[...] = _adamw(w_ref[...], g, m_ref[...], v_ref[...])

    return pl.pallas_call(
        body, name=name, out_shape=[jax.ShapeDtypeStruct(w.shape, F32)] * 4,
    )(g_all, w, m, v)


WEIGHTS = ("ffn1_norm_pre", "ffn1_w_in", "ffn1_w_out", "ffn1_norm_post", "mix_norm_pre", "w_in", "sinks",
           "mem_norm", "w_mem_kv", "w_gate", "b_gate", "w_o_a", "w_o_b", "w_o_m", "w_out", "mix_norm_post",
           "ffn2_norm_pre", "ffn2_w_in", "ffn2_w_out", "ffn2_norm_post")
BIG = ("ffn1_w_in", "ffn1_w_out", "w_in", "w_mem_kv", "w_gate", "w_o_a", "w_o_b", "w_o_m", "w_out",
       "ffn2_w_in", "ffn2_w_out")
COLUMN_SHARDED = ("ffn1_w_in", "ffn2_w_in", "w_in", "w_gate", "w_o_a", "w_o_b", "w_o_m")
KEPT_SHARD_MAJOR = ("ffn1_w_in", "ffn2_w_in", "w_gate")
GAINS = ("ffn1_norm_pre", "ffn1_norm_post", "mix_norm_pre", "mem_norm", "mix_norm_post", "ffn2_norm_pre",
         "ffn2_norm_post")
SMALL_ROWS = 16


def _pack_small(t):
    sinks = jnp.pad(t["sinks"], ((0, 0), (0, D_MODEL - t["sinks"].shape[1])))
    rows = [t[k] for k in GAINS] + [t["b_gate"].reshape(3, D_MODEL), sinks]
    packed = jnp.concatenate(rows, axis=0)
    return jnp.pad(packed, ((0, SMALL_ROWS - packed.shape[0]), (0, 0)))


def _unpack_small(p):
    out = {k: p[i:i + 1] for i, k in enumerate(GAINS)}
    out["b_gate"] = p[7:10].reshape(1, 3 * D_MODEL)
    out["sinks"] = p[10:11, :4]
    return out


def kernel(x, mem, ffn1_norm_pre, ffn1_w_in, ffn1_w_out, ffn1_norm_post, mix_norm_pre, w_in, sinks, mem_norm, w_mem_kv, w_gate, b_gate, w_o_a, w_o_b, w_o_m, w_out, mix_norm_post, ffn2_norm_pre, ffn2_w_in, ffn2_w_out, ffn2_norm_post, loss_target, m_ffn1_norm_pre, m_ffn1_w_in, m_ffn1_w_out, m_ffn1_norm_post, m_mix_norm_pre, m_w_in, m_sinks, m_mem_norm, m_w_mem_kv, m_w_gate, m_b_gate, m_w_o_a, m_w_o_b, m_w_o_m, m_w_out, m_mix_norm_post, m_ffn2_norm_pre, m_ffn2_w_in, m_ffn2_w_out, m_ffn2_norm_post, v_ffn1_norm_pre, v_ffn1_w_in, v_ffn1_w_out, v_ffn1_norm_post, v_mix_norm_pre, v_w_in, v_sinks, v_mem_norm, v_w_mem_kv, v_w_gate, v_b_gate, v_w_o_a, v_w_o_b, v_w_o_m, v_w_out, v_mix_norm_post, v_ffn2_norm_pre, v_ffn2_w_in, v_ffn2_w_out, v_ffn2_norm_post):
    given = dict(locals())
    wt = {k: given[k] for k in WEIGHTS}
    mom = {k: given["m_" + k] for k in WEIGHTS}
    var = {k: given["v_" + k] for k in WEIGHTS}
    me = (2 * lax.axis_index("x") + lax.axis_index("y")).astype(jnp.int32).reshape(1)

    gathered = chip_all_gather([wt[k][0].astype(BF16) for k in BIG], "weight_gather")
    full = {}
    for k, g in zip(BIG, gathered):
        if k in COLUMN_SHARDED:
            if k in ("ffn1_w_in", "ffn2_w_in"):
                g = jnp.stack([g[0], g[2], g[1], g[3]])
            full[k] = jnp.swapaxes(g, 0, 1).reshape(g.shape[1], N_CHIPS * g.shape[2])
        else:
            full[k] = g.reshape(N_CHIPS * g.shape[1], g.shape[2])

    gains = {k: wt[k] for k in GAINS}
    sq, dx, grads = layer_step(x[0], mem[0], loss_target[0], gains, sinks[0], b_gate, full)
    loss = lax.psum(0.5 * sq[0, 0] / D_MODEL, ("x", "y", "c"))

    grads_sm = []
    for k in BIG:
        g = grads[k]
        if k in KEPT_SHARD_MAJOR:
            pass
        elif k in COLUMN_SHARDED:
            g = jnp.swapaxes(g.reshape(g.shape[0], N_CHIPS, g.shape[1] // N_CHIPS), 0, 1)
        else:
            g = g.reshape(N_CHIPS, g.shape[0] // N_CHIPS, g.shape[1])
        grads_sm.append(g)
    small = _pack_small(grads)
    *received, small_all = grad_exchange([g.astype(BF16) for g in grads_sm], small, "grad_exchange")
    parts = [chip_partial_sum(me, g, r, f"{k}_chip_sum") for k, g, r in zip(BIG, grads_sm, received)]
    sibs = sibling_exchange(parts, "sibling_exchange")

    res = {}
    for k, p, s in zip(BIG, parts, sibs):
        res[k] = [t[None] for t in adamw_pair(p, s, wt[k][0], mom[k][0], var[k][0], f"{k}_adamw")]
    packed = adamw_small(small_all, _pack_small(wt), _pack_small(mom), _pack_small(var), "small_adamw")
    for idx, p in enumerate(packed):
        for k, t in _unpack_small(p).items():
            res.setdefault(k, [None] * 4)[idx] = t

    return (loss, dx[None], *[res[k][0] for k in WEIGHTS], *[res[k][1] for k in WEIGHTS],
            *[res[k][2] for k in WEIGHTS], *[res[k][3] for k in WEIGHTS])
```

```python
import functools

import jax
import jax.numpy as jnp
from jax import lax
from jax.experimental import pallas as pl
from jax.experimental.pallas import tpu as pltpu

F32 = jnp.float32
BF16 = jnp.bfloat16

D_MODEL = 1024
D_FF = 2816
HEAD = 128
N_CHIPS = 4
N_DEV = 8
EPS = 1e-6
NEG_INF = -1e30
ROPE_THETA = 10000.0
ATT_SCALE = HEAD ** -0.5

ADAM_LR = 0.001
ADAM_B1 = 0.9
ADAM_B2 = 0.999
ADAM_EPS = 1e-08
ADAM_WD = 0.01
ADAM_STEP = 10

VMEM_LIMIT = 52 * 2 ** 20
MESH = pl.DeviceIdType.MESH

QKV_W = 3840
AQ, AK, AV, BQ, BK, BV, MQ = 0, 6, 12, 18, 22, 24, 26
DIL = ((128, 1), (512, 4), (2048, 16))

TM = 512
FF_T = D_FF // 2


def _params(*sem):
    return pltpu.CompilerParams(dimension_semantics=sem, vmem_limit_bytes=VMEM_LIMIT)


def _dot(a, b):
    return jnp.dot(a, b, preferred_element_type=F32)


def _dot_nt(a, b):
    return lax.dot_general(a, b, (((1,), (1,)), ((), ())), preferred_element_type=F32)


def _dot_tn(a, b):
    return lax.dot_general(a, b, (((0,), (0,)), ((), ())), preferred_element_type=F32)


def _rstd(x):
    return lax.rsqrt(jnp.mean(x * x, axis=-1, keepdims=True) + EPS)


def _ffn_perm(k):
    return (k % 2) * 2 + k // 2


def _resident(arr):
    return pl.BlockSpec(arr.shape, lambda *_: (0,) * arr.ndim, pipeline_mode=pl.Buffered(1))


def ffn_in(h, g, w, name):
    T, D = h.shape

    def body(h_ref, g_ref, w_ref, xn_ref, gu_ref, a_ref):
        x = h_ref[...]
        xn = (x * _rstd(x) * g_ref[...]).astype(BF16)
        xn_ref[...] = xn
        for j in range(2):
            gu = _dot(xn, w_ref[:, j * 2 * FF_T:(j + 1) * 2 * FF_T])
            gu_ref[:, j * 2 * FF_T:(j + 1) * 2 * FF_T] = gu.astype(BF16)
            gate, up = gu[:, :FF_T], gu[:, FF_T:]
            a_ref[:, j * FF_T:(j + 1) * FF_T] = (gate * jax.nn.sigmoid(gate) * up).astype(BF16)

    def rows(width):
        return pl.BlockSpec((TM, width), lambda i: (i, 0))

    return pl.pallas_call(
        body, name=name,
        grid=(T // TM,),
        in_specs=[rows(D), _resident(g), _resident(w)],
        out_specs=[rows(D), rows(2 * D_FF), rows(D_FF)],
        out_shape=[jax.ShapeDtypeStruct((T, D), BF16),
                   jax.ShapeDtypeStruct((T, 2 * D_FF), BF16),
                   jax.ShapeDtypeStruct((T, D_FF), BF16)],
        compiler_params=_params("parallel"),
    )(h, g, w)


def mm_norm_res(a, w, h_in, g, coef, name, target=None):
    T, K = a.shape
    D = w.shape[1]
    final = target is not None

    def body(*refs):
        if final:
            a_ref, w_ref, h_ref, g_ref, t_ref, f_ref, o_ref, l_ref = refs
        else:
            a_ref, w_ref, h_ref, g_ref, f_ref, o_ref = refs
        f = _dot(a_ref[...], w_ref[...])
        f_ref[...] = f
        y = h_ref[...] + coef * (f * _rstd(f) * g_ref[...])
        if final:
            err = y - t_ref[...]
            o_ref[...] = err * (1.0 / D)

            @pl.when(pl.program_id(0) == 0)
            def _():
                l_ref[...] = jnp.zeros_like(l_ref)

            l_ref[...] += jnp.sum(err * err)
        else:
            o_ref[...] = y

    row = pl.BlockSpec((TM, D), lambda i: (i, 0))
    in_specs = [pl.BlockSpec((TM, K), lambda i: (i, 0)),
                _resident(w),
                row, pl.BlockSpec((1, D), lambda i: (0, 0))]
    out_specs = [row, row]
    out_shape = [jax.ShapeDtypeStruct((T, D), F32), jax.ShapeDtypeStruct((T, D), F32)]
    args = [a, w, h_in, g]
    if final:
        in_specs.append(row)
        args.append(target)
        out_specs.append(pl.BlockSpec((8, 128), lambda i: (0, 0)))
        out_shape.append(jax.ShapeDtypeStruct((8, 128), F32))
    return pl.pallas_call(
        body, name=name, grid=(T // TM,), in_specs=in_specs, out_specs=out_specs, out_shape=out_shape,
        compiler_params=_params("arbitrary"),
    )(*args)


def _rope(x, cos, sin_signed):
    return x * cos + pltpu.roll(x, HEAD // 2, axis=1) * sin_signed


def _unrope(x, cos, sin_signed):
    return x * cos - pltpu.roll(x, HEAD // 2, axis=1) * sin_signed


ROTARY_HEADS = tuple(range(AQ, AV)) + tuple(range(BQ, BV))


def mix_in(h, g, w, w_gate, b_gate, cos, sin_signed, name):
    T, D = h.shape
    tn = 768

    def body(h_ref, g_ref, w_ref, wg_ref, b_ref, c_ref, s_ref, u_ref, o_ref, gt_ref):
        x = h_ref[...]
        u = (x * _rstd(x) * g_ref[...]).astype(BF16)
        u_ref[...] = u
        c, s = c_ref[...], s_ref[...]
        for j in range(QKV_W // tn):
            acc = _dot(u, w_ref[:, j * tn:(j + 1) * tn])
            for hd in range(tn // HEAD):
                head = j * (tn // HEAD) + hd
                part = acc[:, hd * HEAD:(hd + 1) * HEAD]
                if head in ROTARY_HEADS:
                    part = _rope(part, c, s)
                o_ref[:, head * HEAD:(head + 1) * HEAD] = part.astype(BF16)
        for j in range(w_gate.shape[1] // tn):
            cols = slice(j * tn, (j + 1) * tn)
            gt_ref[:, cols] = jax.nn.sigmoid(_dot(u, wg_ref[:, cols]) + b_ref[:, cols]).astype(BF16)

    def rows(width):
        return pl.BlockSpec((TM, width), lambda i: (i, 0))

    return pl.pallas_call(
        body, name=name,
        grid=(T // TM,),
        in_specs=[rows(D), _resident(g), _resident(w), _resident(w_gate), _resident(b_gate), rows(HEAD), rows(HEAD)],
        out_specs=[rows(D), rows(QKV_W), rows(w_gate.shape[1])],
        out_shape=[jax.ShapeDtypeStruct((T, D), BF16), jax.ShapeDtypeStruct((T, QKV_W), BF16),
                   jax.ShapeDtypeStruct((T, w_gate.shape[1]), BF16)],
        compiler_params=_params("parallel"),
    )(h, g, w, w_gate, b_gate, cos, sin_signed)


def gate_merge(gt, o_a, o_b, o_m, w_a, w_b, w_m, name):
    T = gt.shape[0]
    D = D_MODEL

    def body(gt_ref, oa_ref, ob_ref, om_ref, wa_ref, wb_ref, wm_ref, out_ref):
        acc = gt_ref[:, :D].astype(F32) * _dot(oa_ref[...], wa_ref[...])
        acc += gt_ref[:, D:2 * D].astype(F32) * _dot(ob_ref[...], wb_ref[...])
        acc += gt_ref[:, 2 * D:].astype(F32) * _dot(om_ref[...], wm_ref[...])
        out_ref[...] = acc.astype(BF16)

    def rows(width):
        return pl.BlockSpec((TM, width), lambda i: (i, 0))

    def whole(arr):
        return pl.BlockSpec(arr.shape, lambda i: (0, 0))

    return pl.pallas_call(
        body, name=name, grid=(T // TM,),
        in_specs=[rows(3 * D), rows(o_a.shape[1]), rows(o_b.shape[1]), rows(o_m.shape[1]),
                  whole(w_a), whole(w_b), whole(w_m)],
        out_specs=rows(D),
        out_shape=jax.ShapeDtypeStruct((T, D), BF16),
        compiler_params=_params("parallel"),
    )(gt, o_a, o_b, o_m, w_a, w_b, w_m)


def _band_rows(start, r):
    return pl.ds(start, HEAD) if r == 1 else pl.ds(start, HEAD, stride=r)


def _band_mask(max_dist, first_has_prev):
    row = lax.broadcasted_iota(jnp.int32, (HEAD, 2 * HEAD), 0)
    col = lax.broadcasted_iota(jnp.int32, (HEAD, 2 * HEAD), 1)
    dist = row + HEAD - col
    band = (dist >= 0) & (dist <= max_dist)
    return band, band & (col >= jnp.where(first_has_prev, 0, HEAD))


def band_fwd(qkv, sinks, *, r, q_off, k_off, v_off, hkv, grp, max_dist, out_dtype, name):
    T, W = qkv.shape
    SB = HEAD * r
    BT = min(2048, T)
    nsub, nib = BT // SB, T // BT
    hq = hkv * grp

    def body(sink_ref, q_ref, kc_ref, kp_ref, vc_ref, vp_ref, o_ref, l_ref, qf, kf, vf):
        kvh, ib = pl.program_id(0), pl.program_id(1)
        qf[...] = q_ref[...].astype(F32)
        kf[:SB] = kp_ref[...].astype(F32)
        kf[SB:] = kc_ref[...].astype(F32)
        vf[:SB] = vp_ref[...].astype(F32)
        vf[SB:] = vc_ref[...].astype(F32)
        band, band_first = _band_mask(max_dist, ib > 0)
        for j in range(nsub):
            mask = band_first if j == 0 else band
            for c in range(r):
                rows = _band_rows(j * SB + c, r)
                older, own = _band_rows(j * SB + c, r), _band_rows((j + 1) * SB + c, r)
                kcat = jnp.concatenate([kf[older], kf[own]], axis=0).astype(BF16)
                vcat = jnp.concatenate([vf[older], vf[own]], axis=0).astype(BF16)
                for gq in range(grp):
                    cols = slice(gq * HEAD, (gq + 1) * HEAD)
                    s = jnp.where(mask, _dot_nt(qf[rows, cols].astype(BF16), kcat) * ATT_SCALE, NEG_INF)
                    sk = sink_ref[kvh * grp + gq]
                    m = jnp.maximum(jnp.max(s, axis=-1, keepdims=True), sk)
                    p = jnp.exp(s - m)
                    tot = jnp.sum(p, axis=-1, keepdims=True) + jnp.exp(sk - m)
                    o_ref[rows, cols] = (_dot(p.astype(BF16), vcat) / tot).astype(out_dtype)
                    l_ref[rows, cols] = jnp.broadcast_to(m + jnp.log(tot), (HEAD, HEAD))

    def cur(off, width):
        return pl.BlockSpec((BT, width * HEAD), lambda h, i: (i, off // width + h))

    def prev(off):
        return pl.BlockSpec((SB, HEAD), lambda h, i: (jnp.maximum(i * nsub - 1, 0), off + h))

    out_spec = pl.BlockSpec((BT, grp * HEAD), lambda h, i: (i, h))
    return pl.pallas_call(
        body, name=name, grid=(hkv, nib),
        in_specs=[pl.BlockSpec(memory_space=pltpu.SMEM),
                  cur(q_off, grp), cur(k_off, 1), prev(k_off), cur(v_off, 1), prev(v_off)],
        out_specs=[out_spec, out_spec],
        out_shape=[jax.ShapeDtypeStruct((T, hq * HEAD), out_dtype), jax.ShapeDtypeStruct((T, hq * HEAD), F32)],
        scratch_shapes=[pltpu.VMEM((BT, grp * HEAD), F32), pltpu.VMEM((SB + BT, HEAD), F32),
                        pltpu.VMEM((SB + BT, HEAD), F32)],
        compiler_params=_params("parallel", "arbitrary"),
    )(sinks, qkv, qkv, qkv, qkv, qkv)


def band_bwd(qkv, do, o, lse, cos, sin_signed, sinks, *, r, q_off, k_off, v_off, hkv, grp, max_dist, name):
    T, W = qkv.shape
    SB = HEAD * r
    BT = min(2048, T)
    nsub, nib = BT // SB, T // BT
    nblk = T // SB
    hq = hkv * grp
    with_sink = sinks is not None

    def body(*refs):
        if with_sink:
            sink_ref, refs = refs[0], refs[1:]
        (q_ref, qn_ref, kc_ref, kp_ref, vc_ref, vp_ref, do_ref, don_ref, o_ref, on_ref, l_ref, ln_ref,
         c_ref, s_ref) = refs[:14]
        dq_ref, dk_ref, dv_ref = refs[14:17]
        ds_ref = refs[17] if with_sink else None
        qf, dof, of, kf, vf, dqf, dkacc, dvacc = refs[-8:]
        kvh, ib = pl.program_id(0), pl.program_id(1)
        for buf, cur_ref, nxt_ref in ((qf, q_ref, qn_ref), (dof, do_ref, don_ref), (of, o_ref, on_ref)):
            buf[:BT] = cur_ref[...].astype(F32)
            buf[BT:] = nxt_ref[...].astype(F32)
        kf[:SB] = kp_ref[...].astype(F32)
        kf[SB:] = kc_ref[...].astype(F32)
        vf[:SB] = vp_ref[...].astype(F32)
        vf[SB:] = vc_ref[...].astype(F32)
        dkacc[...] = jnp.zeros_like(dkacc)
        dvacc[...] = jnp.zeros_like(dvacc)
        band, band_first = _band_mask(max_dist, ib > 0)
        if with_sink:
            @pl.when(ib == 0)
            def _():
                ds_ref[...] = jnp.zeros_like(ds_ref)

        def grads(rows, cols, logz, keys, vals, mask):
            q, dout = qf[rows, cols].astype(BF16), dof[rows, cols].astype(BF16)
            delta = jnp.sum(dof[rows, cols] * of[rows, cols], axis=-1, keepdims=True)
            s = jnp.where(mask, _dot_nt(q, keys) * ATT_SCALE, NEG_INF)
            p = jnp.exp(s - logz[:, :1])
            ds = (p * (_dot_nt(dout, vals) - delta) * ATT_SCALE).astype(BF16)
            return q, dout, p.astype(BF16), ds, delta

        for j in range(nsub):
            mask = band_first if j == 0 else band
            for c in range(r):
                rows = _band_rows(j * SB + c, r)
                older, own = _band_rows(j * SB + c, r), _band_rows((j + 1) * SB + c, r)
                kcat = jnp.concatenate([kf[older], kf[own]], axis=0).astype(BF16)
                vcat = jnp.concatenate([vf[older], vf[own]], axis=0).astype(BF16)
                for gq in range(grp):
                    cols = slice(gq * HEAD, (gq + 1) * HEAD)
                    logz = l_ref[rows, cols]
                    q, dout, p, ds, delta = grads(rows, cols, logz, kcat, vcat, mask)
                    dqf[rows, cols] = _dot(ds, kcat)
                    dk = _dot_tn(ds, q)
                    dv = _dot_tn(p, dout)
                    dkacc[older] += dk[:HEAD]
                    dkacc[own] += dk[HEAD:]
                    dvacc[older] += dv[:HEAD]
                    dvacc[own] += dv[HEAD:]
                    if with_sink:
                        p_sink = jnp.exp(sink_ref[kvh * grp + gq] - logz[:, :1])
                        ds_ref[gq * 8:(gq + 1) * 8] += jnp.sum(p_sink * delta)

        row = lax.broadcasted_iota(jnp.int32, (HEAD, HEAD), 0)
        col = lax.broadcasted_iota(jnp.int32, (HEAD, HEAD), 1)
        reach = col >= row + jnp.where(ib < nib - 1, HEAD - max_dist, 2 * HEAD)
        for c in range(r):
            last = _band_rows(BT + c, r)
            keys, vals = kf[last].astype(BF16), vf[last].astype(BF16)
            for gq in range(grp):
                cols = slice(gq * HEAD, (gq + 1) * HEAD)
                q, dout, p, ds, _ = grads(last, cols, ln_ref[_band_rows(c, r), cols], keys, vals, reach)
                dkacc[last] += _dot_tn(ds, q)
                dvacc[last] += _dot_tn(p, dout)

        cs, sn = c_ref[...], s_ref[...]
        for gq in range(grp):
            cols = slice(gq * HEAD, (gq + 1) * HEAD)
            dq_ref[:, cols] = _unrope(dqf[:, cols], cs, sn).astype(BF16)
        dk_ref[...] = _unrope(dkacc[SB:], cs, sn).astype(BF16)
        dv_ref[...] = dvacc[SB:].astype(BF16)

    def cur(off, width):
        return pl.BlockSpec((BT, width * HEAD), lambda h, i: (i, off // width + h))

    def prev(off):
        return pl.BlockSpec((SB, HEAD), lambda h, i: (jnp.maximum(i * nsub - 1, 0), off + h))

    def nxt_row(i):
        return jnp.minimum((i + 1) * nsub, nblk - 1)

    q_next = pl.BlockSpec((SB, grp * HEAD), lambda h, i: (nxt_row(i), q_off // grp + h))
    head_cur = pl.BlockSpec((BT, grp * HEAD), lambda h, i: (i, h))
    head_next = pl.BlockSpec((SB, grp * HEAD), lambda h, i: (nxt_row(i), h))
    table = pl.BlockSpec((BT, HEAD), lambda h, i: (i, 0))
    kv_out = pl.BlockSpec((BT, HEAD), lambda h, i: (i, h))

    in_specs = [cur(q_off, grp), q_next, cur(k_off, 1), prev(k_off), cur(v_off, 1), prev(v_off),
                head_cur, head_next, head_cur, head_next, head_cur, head_next, table, table]
    args = [qkv, qkv, qkv, qkv, qkv, qkv, do, do, o, o, lse, lse, cos, sin_signed]
    out_specs = [head_cur, kv_out, kv_out]
    out_shape = [jax.ShapeDtypeStruct((T, hq * HEAD), BF16), jax.ShapeDtypeStruct((T, hkv * HEAD), BF16),
                 jax.ShapeDtypeStruct((T, hkv * HEAD), BF16)]
    if with_sink:
        in_specs.insert(0, pl.BlockSpec(memory_space=pltpu.SMEM))
        args.insert(0, sinks)
        out_specs.append(pl.BlockSpec((None, grp * 8, HEAD), lambda h, i: (h, 0, 0)))
        out_shape.append(jax.ShapeDtypeStruct((hkv, grp * 8, HEAD), F32))
    wide = pltpu.VMEM((BT + SB, grp * HEAD), F32)
    tall = pltpu.VMEM((SB + BT, HEAD), F32)
    return pl.pallas_call(
        body, name=name, grid=(hkv, nib), in_specs=in_specs, out_specs=out_specs, out_shape=out_shape,
        scratch_shapes=[wide, wide, wide, tall, tall, pltpu.VMEM((BT, grp * HEAD), F32), tall, tall],
        compiler_params=_params("parallel", "arbitrary"),
    )(*args)


def merge_groups(outs, lses, name):
    T, Wd = outs[0].shape
    tm = 1024

    def body(o0, o1, o2, l0, l1, l2, out_ref, lt_ref):
        a, b, c = l0[...], l1[...], l2[...]
        m = jnp.maximum(jnp.maximum(a, b), c)
        wa, wb, wc = jnp.exp(a - m), jnp.exp(b - m), jnp.exp(c - m)
        z = wa + wb + wc
        out_ref[...] = ((wa * o0[...] + wb * o1[...] + wc * o2[...]) / z).astype(BF16)
        lt_ref[...] = m + jnp.log(z)

    spec = pl.BlockSpec((tm, Wd), lambda i: (i, 0))
    return pl.pallas_call(
        body, name=name, grid=(T // tm,), in_specs=[spec] * 6, out_specs=[spec, spec],
        out_shape=[jax.ShapeDtypeStruct((T, Wd), BF16), jax.ShapeDtypeStruct((T, Wd), F32)],
        compiler_params=_params("parallel"),
    )(*outs, *lses)


M_HEADS = 4


def mem_kv(mem, g, w, name):
    n, D = mem.shape

    def body(m_ref, g_ref, w_ref, mn_ref, kv_ref):
        x = m_ref[...]
        mn = (x * _rstd(x) * g_ref[...]).astype(BF16)
        mn_ref[...] = mn
        kv_ref[...] = _dot(mn, w_ref[...]).astype(BF16)

    return pl.pallas_call(
        body, name=name,
        out_shape=[jax.ShapeDtypeStruct((n, D), BF16), jax.ShapeDtypeStruct((n, w.shape[1]), BF16)],
        compiler_params=pltpu.CompilerParams(vmem_limit_bytes=VMEM_LIMIT),
    )(mem, g, w)


def mem_fwd(qkv, mkv, name):
    T = qkv.shape[0]
    n = mkv.shape[0]
    RB = 1024

    def body(q_ref, k_ref, v_ref, o_ref, l_ref):
        s = _dot_nt(q_ref[...], k_ref[...]) * ATT_SCALE
        m = jnp.max(s, axis=-1, keepdims=True)
        p = jnp.exp(s - m)
        den = jnp.sum(p, axis=-1, keepdims=True)
        o_ref[...] = (_dot(p.astype(BF16), v_ref[...]) / den).astype(BF16)
        l_ref[...] = jnp.broadcast_to(m + jnp.log(den), (RB, HEAD))

    out = pl.BlockSpec((RB, HEAD), lambda h, i: (i, h))
    return pl.pallas_call(
        body, name=name, grid=(M_HEADS, T // RB),
        in_specs=[pl.BlockSpec((RB, HEAD), lambda h, i: (i, MQ + h)),
                  pl.BlockSpec((n, HEAD), lambda h, i: (0, h)),
                  pl.BlockSpec((n, HEAD), lambda h, i: (0, M_HEADS + h))],
        out_specs=[out, out],
        out_shape=[jax.ShapeDtypeStruct((T, M_HEADS * HEAD), BF16), jax.ShapeDtypeStruct((T, M_HEADS * HEAD), F32)],
        compiler_params=_params("parallel", "parallel"),
    )(qkv, mkv, mkv)


def mem_bwd(qkv, mkv, do, o, lse, name):
    T = qkv.shape[0]
    n = mkv.shape[0]
    RB = 1024

    def body(q_ref, k_ref, v_ref, do_ref, o_ref, l_ref, dq_ref, dk_ref, dv_ref):
        @pl.when(pl.program_id(1) == 0)
        def _():
            dk_ref[...] = jnp.zeros_like(dk_ref)
            dv_ref[...] = jnp.zeros_like(dv_ref)

        q, dout = q_ref[...], do_ref[...]
        delta = jnp.sum(dout.astype(F32) * o_ref[...].astype(F32), axis=-1, keepdims=True)
        p = jnp.exp(_dot_nt(q, k_ref[...]) * ATT_SCALE - l_ref[...][:, :1])
        ds = (p * (_dot_nt(dout, v_ref[...]) - delta) * ATT_SCALE).astype(BF16)
        dq_ref[...] = _dot(ds, k_ref[...]).astype(BF16)
        dk_ref[...] += _dot_tn(ds, q)
        dv_ref[...] += _dot_tn(p.astype(BF16), dout)

    tok = pl.BlockSpec((RB, HEAD), lambda h, i: (i, h))
    slot = pl.BlockSpec((n, HEAD), lambda h, i: (0, h))
    return pl.pallas_call(
        body, name=name, grid=(M_HEADS, T // RB),
        in_specs=[pl.BlockSpec((RB, HEAD), lambda h, i: (i, MQ + h)),
                  slot, pl.BlockSpec((n, HEAD), lambda h, i: (0, M_HEADS + h)), tok, tok, tok],
        out_specs=[tok, slot, slot],
        out_shape=[jax.ShapeDtypeStruct((T, M_HEADS * HEAD), BF16),
                   jax.ShapeDtypeStruct((n, M_HEADS * HEAD), F32),
                   jax.ShapeDtypeStruct((n, M_HEADS * HEAD), F32)],
        compiler_params=_params("parallel", "arbitrary"),
    )(qkv, mkv, mkv, do, o, lse)


def mem_kv_bwd(mem, g, mem_n, w, dmkv, name):
    n, D = mem.shape

    def body(m_ref, g_ref, mn_ref, w_ref, d_ref, dw_ref, dg_ref):
        d = d_ref[...].astype(BF16)
        dw_ref[...] = _dot_tn(mn_ref[...], d)
        x = m_ref[...]
        dg_ref[...] = jnp.sum(_dot_nt(d, w_ref[...]) * (x * _rstd(x)), axis=0, keepdims=True)

    return pl.pallas_call(
        body, name=name,
        out_shape=[jax.ShapeDtypeStruct(w.shape, F32), jax.ShapeDtypeStruct((1, D), F32)],
        compiler_params=pltpu.CompilerParams(vmem_limit_bytes=VMEM_LIMIT),
    )(mem, g, mem_n, w, dmkv)


def _rms_bwd(dn, f, g):
    r = _rstd(f)
    fhat = f * r
    dfhat = dn * g
    df = r * (dfhat - fhat * jnp.mean(dfhat * fhat, axis=-1, keepdims=True))
    return df, jnp.sum(dn * fhat, axis=0, keepdims=True)


def ffn_out_bwd(dh, f, g, w_out, gu, coef, name):
    T, D = dh.shape

    def body(dh_ref, f_ref, g_ref, w_ref, gu_ref, df_ref, dgu_ref, a_ref, dg_ref):
        i, j = pl.program_id(0), pl.program_id(1)

        @pl.when(j == 0)
        def _():
            df, dg = _rms_bwd(coef * dh_ref[...], f_ref[...], g_ref[...])
            df_ref[...] = df.astype(BF16)

            @pl.when(i == 0)
            def _():
                dg_ref[...] = jnp.zeros_like(dg_ref)

            dg_ref[...] += dg

        da = _dot_nt(df_ref[...], w_ref[...])
        gate = gu_ref[:, :FF_T].astype(F32)
        up = gu_ref[:, FF_T:].astype(F32)
        sig = jax.nn.sigmoid(gate)
        silu = gate * sig
        dgu_ref[:, :FF_T] = (da * up * sig * (1.0 + gate * (1.0 - sig))).astype(BF16)
        dgu_ref[:, FF_T:] = (da * silu).astype(BF16)
        a_ref[...] = (silu * up).astype(BF16)

    row = pl.BlockSpec((TM, D), lambda i, j: (i, 0))
    wide = pl.BlockSpec((TM, 2 * FF_T), lambda i, j: (i, j))
    return pl.pallas_call(
        body, name=name, grid=(T // TM, 2),
        in_specs=[row, row, pl.BlockSpec((1, D), lambda i, j: (0, 0)),
                  pl.BlockSpec((FF_T, D), lambda i, j: (j, 0)), wide],
        out_specs=[row, wide, pl.BlockSpec((TM, FF_T), lambda i, j: (i, j)),
                   pl.BlockSpec((1, D), lambda i, j: (0, 0))],
        out_shape=[jax.ShapeDtypeStruct((T, D), BF16), jax.ShapeDtypeStruct((T, 2 * D_FF), BF16),
                   jax.ShapeDtypeStruct((T, D_FF), BF16), jax.ShapeDtypeStruct((1, D), F32)],
        compiler_params=_params("arbitrary", "arbitrary"),
    )(dh, f, g, w_out, gu)


def mix_out_bwd(dh, f, g, w_out, name):
    T, D = dh.shape

    def body(dh_ref, f_ref, g_ref, w_ref, df_ref, dm_ref, dg_ref):
        df, dg = _rms_bwd(dh_ref[...], f_ref[...], g_ref[...])
        df = df.astype(BF16)
        df_ref[...] = df

        @pl.when(pl.program_id(0) == 0)
        def _():
            dg_ref[...] = jnp.zeros_like(dg_ref)

        dg_ref[...] += dg
        dm_ref[...] = _dot_nt(df, w_ref[...]).astype(BF16)

    row = pl.BlockSpec((TM, D), lambda i: (i, 0))
    vec = pl.BlockSpec((1, D), lambda i: (0, 0))
    return pl.pallas_call(
        body, name=name, grid=(T // TM,),
        in_specs=[row, row, vec, _resident(w_out)],
        out_specs=[row, row, vec],
        out_shape=[jax.ShapeDtypeStruct((T, D), BF16), jax.ShapeDtypeStruct((T, D), BF16),
                   jax.ShapeDtypeStruct((1, D), F32)],
        compiler_params=_params("arbitrary"),
    )(dh, f, g, w_out)


def mm_nt_norm_bwd(pieces, h_in, dh_out, g, name):
    T, D = h_in.shape

    def body(*refs):
        ab = refs[:2 * len(pieces)]
        h_ref, dh_ref, g_ref, o_ref, dg_ref = refs[2 * len(pieces):]
        dxn = _dot_nt(ab[0][...], ab[1][...])
        for p in range(1, len(pieces)):
            dxn += _dot_nt(ab[2 * p][...], ab[2 * p + 1][...])
        h = h_ref[...]
        r = _rstd(h)
        xhat = h * r
        dxhat = dxn * g_ref[...]
        o_ref[...] = dh_ref[...] + r * (dxhat - xhat * jnp.mean(dxhat * xhat, axis=-1, keepdims=True))

        @pl.when(pl.program_id(0) == 0)
        def _():
            dg_ref[...] = jnp.zeros_like(dg_ref)

        dg_ref[...] += jnp.sum(dxn * xhat, axis=0, keepdims=True)

    in_specs, args = [], []
    for a, w in pieces:
        in_specs += [pl.BlockSpec((TM, a.shape[1]), lambda i: (i, 0)), _resident(w)]
        args += [a, w]
    row = pl.BlockSpec((TM, D), lambda i: (i, 0))
    return pl.pallas_call(
        body, name=name, grid=(T // TM,),
        in_specs=in_specs + [row, row, _resident(g)],
        out_specs=[row, pl.BlockSpec((1, D), lambda i: (0, 0))],
        out_shape=[jax.ShapeDtypeStruct((T, D), F32), jax.ShapeDtypeStruct((1, D), F32)],
        compiler_params=_params("arbitrary"),
    )(*args, h_in, dh_out, g)


def gate_merge_bwd(dm, gt, o_a, o_b, o_m, w_a, w_b, w_m, name):
    T = dm.shape[0]
    D = D_MODEL
    branch = ((o_a, w_a), (o_b, w_b), (o_m, w_m))

    def body(dm_ref, gt_ref, oa_ref, ob_ref, om_ref, wa_ref, wb_ref, wm_ref,
             dgt_ref, dpa_ref, dpb_ref, dpm_ref, doa_ref, dob_ref, dom_ref, db_ref):
        @pl.when(pl.program_id(0) == 0)
        def _():
            db_ref[...] = jnp.zeros_like(db_ref)

        dmf = dm_ref[...].astype(F32)
        for x, (o_ref, w_ref, dp_ref, do_ref) in enumerate(((oa_ref, wa_ref, dpa_ref, doa_ref),
                                                           (ob_ref, wb_ref, dpb_ref, dob_ref),
                                                           (om_ref, wm_ref, dpm_ref, dom_ref))):
            cols = slice(x * D, (x + 1) * D)
            gx = gt_ref[:, cols].astype(F32)
            w = w_ref[...]
            dpre = dmf * _dot(o_ref[...], w) * gx * (1.0 - gx)
            dgt_ref[:, cols] = dpre.astype(BF16)
            db_ref[:, cols] += jnp.sum(dpre, axis=0, keepdims=True)
            dp = (dmf * gx).astype(BF16)
            dp_ref[...] = dp
            do_ref[...] = _dot_nt(dp, w).astype(BF16)

    def rows(width):
        return pl.BlockSpec((TM, width), lambda i: (i, 0))

    def whole(arr):
        return pl.BlockSpec(arr.shape, lambda i: (0, 0))

    widths = [o.shape[1] for o, _ in branch]
    return pl.pallas_call(
        body, name=name, grid=(T // TM,),
        in_specs=[rows(D), rows(3 * D)] + [rows(k) for k in widths] + [whole(w) for _, w in branch],
        out_specs=[rows(3 * D), rows(D), rows(D), rows(D)] + [rows(k) for k in widths]
                  + [pl.BlockSpec((1, 3 * D), lambda i: (0, 0))],
        out_shape=[jax.ShapeDtypeStruct((T, 3 * D), BF16)] + [jax.ShapeDtypeStruct((T, D), BF16)] * 3
                  + [jax.ShapeDtypeStruct((T, k), BF16) for k in widths]
                  + [jax.ShapeDtypeStruct((1, 3 * D), F32)],
        compiler_params=_params("arbitrary"),
    )(dm, gt, o_a, o_b, o_m, w_a, w_b, w_m)


def mm_tn(x, dy, tm, tn, name, shard_major=False, perm=None):
    T, M = x.shape
    N = dy.shape[1]
    tk = min(1024, T)
    perm = perm or (lambda j: j)

    def body(x_ref, dy_ref, o_ref):
        @pl.when(pl.program_id(2) == 0)
        def _():
            o_ref[...] = jnp.zeros_like(o_ref)

        o_ref[...] += _dot_tn(x_ref[...], dy_ref[...])

    if shard_major:
        out_spec = pl.BlockSpec((None, tm, tn), lambda i, j, k: (perm(j), i, 0))
        out_shape = jax.ShapeDtypeStruct((N // tn, M, tn), F32)
    else:
        out_spec = pl.BlockSpec((tm, tn), lambda i, j, k: (i, j))
        out_shape = jax.ShapeDtypeStruct((M, N), F32)
    return pl.pallas_call(
        body, name=name, grid=(M // tm, N // tn, T // tk),
        in_specs=[pl.BlockSpec((tk, tm), lambda i, j, k: (k, i)),
                  pl.BlockSpec((tk, tn), lambda i, j, k: (k, j))],
        out_specs=out_spec, out_shape=out_shape,
        compiler_params=_params("parallel", "parallel", "arbitrary"),
    )(x, dy)


def rope_tables(T):
    half = HEAD // 2
    inv = ROPE_THETA ** (-jnp.arange(half, dtype=F32) / half)
    ang = jnp.arange(T).astype(F32)[:, None] * inv[None, :]
    cos, sin = jnp.cos(ang), jnp.sin(ang)
    return jnp.concatenate([cos, cos], axis=1), jnp.concatenate([-sin, sin], axis=1)


def layer_step(x, mem, target, gains, sinks, b_gate, weights_of, send_grads):
    T = x.shape[0]
    cos, sin_signed = rope_tables(T)
    no_sink = jnp.full((2,), NEG_INF, F32)

    w = dict(weights_of("ffn1", None))
    xn1, gu1, a1 = ffn_in(x, gains["ffn1_norm_pre"], w["ffn1_w_in"], "ffn1_in")
    f1, h1 = mm_norm_res(a1, w["ffn1_w_out"], x, gains["ffn1_norm_post"], 0.5, "ffn1_out")
    w.update(weights_of("mix", f1))
    u, qkv, gt = mix_in(h1, gains["mix_norm_pre"], w["w_in"], w["w_gate"], b_gate, cos, sin_signed, "mix_in")
    outs, lses = [], []
    for gidx, (window, dil) in enumerate(DIL):
        o_g, l_g = band_fwd(qkv, no_sink, r=dil, q_off=AQ + 2 * gidx, k_off=AK + 2 * gidx, v_off=AV + 2 * gidx,
                            hkv=2, grp=1, max_dist=window // dil, out_dtype=F32, name=f"attn_a{gidx}_fwd")
        outs.append(o_g)
        lses.append(l_g)
    o_a, l_a = merge_groups(outs, lses, "attn_a_merge")
    o_b, l_b = band_fwd(qkv, sinks, r=1, q_off=BQ, k_off=BK, v_off=BV, hkv=2, grp=2, max_dist=HEAD - 1,
                        out_dtype=BF16, name="attn_b_fwd")
    mem_n, mkv = mem_kv(mem, gains["mem_norm"], w["w_mem_kv"], "mem_kv")
    o_m, l_m = mem_fwd(qkv, mkv, "attn_m_fwd")
    merged = gate_merge(gt, o_a, o_b, o_m, w["w_o_a"], w["w_o_b"], w["w_o_m"], "gate_merge")
    mo, h2 = mm_norm_res(merged, w["w_out"], h1, gains["mix_norm_post"], 1.0, "mix_out")
    w.update(weights_of("ffn2", mo))
    xn2, gu2, a2 = ffn_in(h2, gains["ffn2_norm_pre"], w["ffn2_w_in"], "ffn2_in")
    f2, dy, sq = mm_norm_res(a2, w["ffn2_w_out"], h2, gains["ffn2_norm_post"], 0.5, "ffn2_out", target=target)
    del a1, a2

    grads = {}

    def ffn_bwd(tag, dh_out, f, gu, xn, h_in):
        df, dgu, a, grads[f"{tag}_norm_post"] = ffn_out_bwd(
            dh_out, f, gains[f"{tag}_norm_post"], w[f"{tag}_w_out"], gu, 0.5, f"{tag}_out_bwd")
        sent = send_grads(tag, {
            f"{tag}_w_out": mm_tn(a, df, FF_T, D_MODEL, f"{tag}_w_out_grad"),
            f"{tag}_w_in": mm_tn(xn, dgu, D_MODEL, FF_T, f"{tag}_w_in_grad", shard_major=True, perm=_ffn_perm)})
        dh_in, grads[f"{tag}_norm_pre"] = mm_nt_norm_bwd(
            [(dgu, w[f"{tag}_w_in"])], h_in, dh_out, gains[f"{tag}_norm_pre"] + sent, f"{tag}_in_bwd")
        return dh_in

    dh2 = ffn_bwd("ffn2", dy, f2, gu2, xn2, h2)

    mix = {}
    dmo, dmerged, grads["mix_norm_post"] = mix_out_bwd(dh2, mo, gains["mix_norm_post"], w["w_out"], "mix_out_bwd")
    mix["w_out"] = mm_tn(merged, dmo, D_MODEL, D_MODEL, "w_out_grad")
    dgt, dpa, dpb, dpm, do_a, do_b, do_m, grads["b_gate"] = gate_merge_bwd(
        dmerged, gt, o_a, o_b, o_m, w["w_o_a"], w["w_o_b"], w["w_o_m"], "gate_merge_bwd")
    mix["w_o_a"] = mm_tn(o_a, dpa, o_a.shape[1], D_MODEL, "w_o_a_grad")
    mix["w_o_b"] = mm_tn(o_b, dpb, o_b.shape[1], D_MODEL, "w_o_b_grad")
    mix["w_o_m"] = mm_tn(o_m, dpm, o_m.shape[1], D_MODEL, "w_o_m_grad")

    dq_a, dk_a, dv_a = [], [], []
    for gidx, (window, dil) in enumerate(DIL):
        dq, dk, dv = band_bwd(qkv, do_a, o_a, l_a, cos, sin_signed, None, r=dil, q_off=AQ + 2 * gidx,
                              k_off=AK + 2 * gidx, v_off=AV + 2 * gidx, hkv=2, grp=1, max_dist=window // dil,
                              name=f"attn_a{gidx}_bwd")
        dq_a.append(dq)
        dk_a.append(dk)
        dv_a.append(dv)
    dq_b, dk_b, dv_b, dsink = band_bwd(qkv, do_b, o_b, l_b, cos, sin_signed, sinks, r=1, q_off=BQ, k_off=BK,
                                       v_off=BV, hkv=2, grp=2, max_dist=HEAD - 1, name="attn_b_bwd")
    grads["sinks"] = -dsink[:, ::8, 0].reshape(1, 4)
    dq_m, dmk, dmv = mem_bwd(qkv, mkv, do_m, o_m, l_m, "attn_m_bwd")
    mix["w_mem_kv"], grads["mem_norm"] = mem_kv_bwd(
        mem, gains["mem_norm"], mem_n, w["w_mem_kv"], jnp.concatenate([dmk, dmv], axis=1), "mem_kv_bwd")
    dqkv = jnp.concatenate(dq_a + dk_a + dv_a + [dq_b, dk_b, dv_b, dq_m], axis=1)

    mix["w_in"] = mm_tn(u, dqkv, D_MODEL, 1280, "w_in_grad")
    mix["w_gate"] = mm_tn(u, dgt, D_MODEL, 768, "w_gate_grad", shard_major=True)
    sent = send_grads("mix", mix)
    dh1, grads["mix_norm_pre"] = mm_nt_norm_bwd(
        [(dqkv, w["w_in"]), (dgt, w["w_gate"])], h1, dh2, gains["mix_norm_pre"] + sent, "mix_in_bwd")

    dx = ffn_bwd("ffn1", dh1, f1, gu1, xn1, x)
    return sq, dx, grads


def _place():
    return lax.axis_index("x"), lax.axis_index("y"), lax.axis_index("c")


def _other_chips(x, y):
    return [(1 - x, y), (x, 1 - y), (1 - x, 1 - y)]


def _hbm(n):
    return [pl.BlockSpec(memory_space=pltpu.HBM)] * n


SEM = pl.BlockSpec(memory_space=pltpu.SEMAPHORE)
SIDE_EFFECT = pltpu.SideEffectType.DATAFLOW_SIDE_EFFECTING


def _chip_copy(src, land, sems, i, j, dst_slot, scatter):
    x, y, c = _place()
    px, py = _other_chips(x, y)[j]
    send_sems, recv_sems = sems
    return pltpu.make_async_remote_copy(
        src_ref=src[i].at[2 * px + py] if scatter else src[i], dst_ref=land[i].at[dst_slot],
        send_sem=send_sems.at[3 * i + j], recv_sem=recv_sems.at[3 * i + j],
        device_id=(px, py, c), device_id_type=MESH)


def chip_copies_start(srcs, lands, groups, scatter, name):
    n = len(srcs)

    def body(*refs):
        src, land = refs[:n], refs[n:2 * n]
        sems = refs[2 * n:2 * n + 2 * len(groups)]
        token = refs[-1]
        x, y, _ = _place()
        for g, members in enumerate(groups):
            part = ([src[i] for i in members], [land[i] for i in members])
            for t in range(len(members)):
                for j in range(3):
                    _chip_copy(*part, sems[2 * g:2 * g + 2], t, j, 2 * x + y, scatter).start()
        token[...] = jnp.zeros_like(token)

    sem_shapes = [pltpu.SemaphoreType.DMA((3 * len(m),)) for m in groups for _ in range(2)]
    thru = [pltpu.HBM(a.shape, a.dtype) for a in (*srcs, *lands)]
    res = pl.pallas_call(
        body, name=name,
        out_shape=(*sem_shapes, *thru, jax.ShapeDtypeStruct((8, 128), F32)),
        in_specs=_hbm(2 * n),
        out_specs=(*[SEM] * len(sem_shapes), *_hbm(2 * n), pl.BlockSpec(memory_space=pltpu.VMEM)),
        input_output_aliases={i: len(sem_shapes) + i for i in range(2 * n)},
        compiler_params=pltpu.CompilerParams(has_side_effects=SIDE_EFFECT),
    )(*[pltpu.with_memory_space_constraint(a, pltpu.HBM) for a in (*srcs, *lands)])
    k = len(sem_shapes)
    sems = [tuple(res[2 * g:2 * g + 2]) for g in range(len(groups))]
    return sems, list(res[k:k + n]), list(res[k + n:k + 2 * n]), res[-1]


def chip_copies_wait(srcs, lands, sems, after, scatter, name):
    n = len(srcs)

    def body(*refs):
        src, land = refs[:n], refs[n:2 * n]
        pair = refs[2 * n:2 * n + 2]
        x, y, _ = _place()
        for i in range(n):
            for j, (px, py) in enumerate(_other_chips(x, y)):
                copy = _chip_copy(src, land, pair, i, j, 2 * px + py, scatter)
                copy.wait_send()
                copy.wait_recv()

    res = pl.pallas_call(
        body, name=name,
        out_shape=[pltpu.HBM(a.shape, a.dtype) for a in (*srcs, *lands)],
        in_specs=[*_hbm(2 * n), SEM, SEM, pl.BlockSpec(memory_space=pl.ANY)],
        out_specs=_hbm(2 * n),
        input_output_aliases={i: i for i in range(2 * n)},
        compiler_params=pltpu.CompilerParams(has_side_effects=SIDE_EFFECT),
    )(*srcs, *lands, *sems, after)
    return list(res[n:])


def small_all_gather(small, name):
    flips = [(fx, fy, fc) for fx in (0, 1) for fy in (0, 1) for fc in (0, 1)][1:]

    def body(in_ref, out_ref, send_sems, recv_sems, local_sem):
        x, y, c = _place()
        me = 4 * x + 2 * y + c

        def copy(k, slot):
            fx, fy, fc = flips[k]
            return pltpu.make_async_remote_copy(
                src_ref=in_ref, dst_ref=out_ref.at[slot], send_sem=send_sems.at[k], recv_sem=recv_sems.at[k],
                device_id=(x ^ fx, y ^ fy, c ^ fc), device_id_type=MESH)

        local = pltpu.make_async_copy(in_ref, out_ref.at[me], local_sem)
        local.start()
        for k in range(len(flips)):
            copy(k, me).start()
        for k, (fx, fy, fc) in enumerate(flips):
            copy(k, 4 * (x ^ fx) + 2 * (y ^ fy) + (c ^ fc)).wait()
        local.wait()

    return pl.pallas_call(
        body, name=name, in_specs=_hbm(1), out_specs=_hbm(1)[0],
        out_shape=jax.ShapeDtypeStruct((N_DEV,) + small.shape, small.dtype),
        scratch_shapes=[pltpu.SemaphoreType.DMA((len(flips),)), pltpu.SemaphoreType.DMA((len(flips),)),
                        pltpu.SemaphoreType.DMA],
    )(small)


def sibling_exchange(parts, name):
    n = len(parts)

    def body(*refs):
        ins, outs = refs[:n], refs[n:2 * n]
        send_sems, recv_sems = refs[2 * n:]
        x, y, c = _place()
        copies = [pltpu.make_async_remote_copy(
            src_ref=ins[i], dst_ref=outs[i], send_sem=send_sems.at[i], recv_sem=recv_sems.at[i],
            device_id=(x, y, 1 - c), device_id_type=MESH) for i in range(n)]
        for cp in copies:
            cp.start()
        for cp in copies:
            cp.wait()

    return pl.pallas_call(
        body, name=name, in_specs=_hbm(n), out_specs=_hbm(n),
        out_shape=[jax.ShapeDtypeStruct(p.shape, p.dtype) for p in parts],
        scratch_shapes=[pltpu.SemaphoreType.DMA((n,)), pltpu.SemaphoreType.DMA((n,))],
    )(*parts)


def _row_tile(rows):
    for t in (256, 176, 128, 64, 32, 16, 8):
        if rows % t == 0:
            return t
    return rows


def chip_partial_sum(me, own_sm, recv, name):
    _, rows, cols = own_sm.shape
    tr = _row_tile(rows)

    def body(me_ref, own_ref, r0, r1, r2, r3, o_ref):
        acc = jnp.zeros((tr, cols), F32)
        for s, r_ref in enumerate((r0, r1, r2, r3)):
            acc = acc + jnp.where(me_ref[0] == s, own_ref[...], r_ref[...].astype(F32))
        o_ref[...] = acc

    def slot(s):
        return pl.BlockSpec((None, tr, cols), lambda i, me_ref, s=s: (s, i, 0))

    return pl.pallas_call(
        body, name=name,
        grid_spec=pltpu.PrefetchScalarGridSpec(
            num_scalar_prefetch=1, grid=(rows // tr,),
            in_specs=[pl.BlockSpec((None, tr, cols), lambda i, me_ref: (me_ref[0], i, 0))] + [slot(s) for s in range(4)],
            out_specs=pl.BlockSpec((tr, cols), lambda i, me_ref: (i, 0))),
        out_shape=jax.ShapeDtypeStruct((rows, cols), F32),
        compiler_params=_params("parallel"),
    )(me, own_sm, recv, recv, recv, recv)


def _adamw(w, g, m, v):
    m = ADAM_B1 * m + (1.0 - ADAM_B1) * g
    v = ADAM_B2 * v + (1.0 - ADAM_B2) * (g * g)
    m_hat = m / (1.0 - ADAM_B1 ** ADAM_STEP)
    v_hat = v / (1.0 - ADAM_B2 ** ADAM_STEP)
    delta = -ADAM_LR * (m_hat / (jnp.sqrt(v_hat) + ADAM_EPS) + ADAM_WD * w)
    return delta, m, v


def adamw_pair(part, sib, w, m, v, name):
    rows, cols = w.shape
    tr = _row_tile(rows)

    def body(p_ref, s_ref, w_ref, m_ref, v_ref, g_ref, d_ref, nm_ref, nv_ref):
        g = p_ref[...] + s_ref[...]
        g_ref[...] = g
        d_ref[...], nm_ref[...], nv_ref[...] = _adamw(w_ref[...], g, m_ref[...], v_ref[...])

    spec = pl.BlockSpec((tr, cols), lambda i: (i, 0))
    return pl.pallas_call(
        body, name=name, grid=(rows // tr,), in_specs=[spec] * 5, out_specs=[spec] * 4,
        out_shape=[jax.ShapeDtypeStruct((rows, cols), F32)] * 4,
        compiler_params=_params("parallel"),
    )(part, sib, w, m, v)


def adamw_small(g_all, w, m, v, name):
    def body(ga_ref, w_ref, m_ref, v_ref, g_ref, d_ref, nm_ref, nv_ref):
        g = ga_ref[0]
        for k in range(1, N_DEV):
            g = g + ga_ref[k]
        g_ref[...] = g
        d_ref[...], nm_ref[...], nv_ref[...] = _adamw(w_ref[...], g, m_ref[...], v_ref[...])

    return pl.pallas_call(
        body, name=name, out_shape=[jax.ShapeDtypeStruct(w.shape, F32)] * 4,
    )(g_all, w, m, v)


WEIGHTS = ("ffn1_norm_pre", "ffn1_w_in", "ffn1_w_out", "ffn1_norm_post", "mix_norm_pre", "w_in", "sinks",
           "mem_norm", "w_mem_kv", "w_gate", "b_gate", "w_o_a", "w_o_b", "w_o_m", "w_out", "mix_norm_post",
           "ffn2_norm_pre", "ffn2_w_in", "ffn2_w_out", "ffn2_norm_post")
BIG = ("ffn1_w_in", "ffn1_w_out", "w_in", "w_mem_kv", "w_gate", "w_o_a", "w_o_b", "w_o_m", "w_out",
       "ffn2_w_in", "ffn2_w_out")
GROUP_ORDER = ("ffn1", "mix", "ffn2")
GROUPS = {"ffn1": ("ffn1_w_in", "ffn1_w_out"),
          "mix": ("w_in", "w_gate", "w_mem_kv", "w_o_a", "w_o_b", "w_o_m", "w_out"),
          "ffn2": ("ffn2_w_in", "ffn2_w_out")}
COLUMN_SHARDED = ("ffn1_w_in", "ffn2_w_in", "w_in", "w_gate", "w_o_a", "w_o_b", "w_o_m")
KEPT_SHARD_MAJOR = ("ffn1_w_in", "ffn2_w_in", "w_gate")
GAINS = ("ffn1_norm_pre", "ffn1_norm_post", "mix_norm_pre", "mem_norm", "mix_norm_post", "ffn2_norm_pre",
         "ffn2_norm_post")
SMALL_ROWS = 16


def _pack_small(t):
    sinks = jnp.pad(t["sinks"], ((0, 0), (0, D_MODEL - t["sinks"].shape[1])))
    rows = [t[k] for k in GAINS] + [t["b_gate"].reshape(3, D_MODEL), sinks]
    packed = jnp.concatenate(rows, axis=0)
    return jnp.pad(packed, ((0, SMALL_ROWS - packed.shape[0]), (0, 0)))


def _unpack_small(p):
    out = {k: p[i:i + 1] for i, k in enumerate(GAINS)}
    out["b_gate"] = p[7:10].reshape(1, 3 * D_MODEL)
    out["sinks"] = p[10:11, :4]
    return out


def kernel(x, mem, ffn1_norm_pre, ffn1_w_in, ffn1_w_out, ffn1_norm_post, mix_norm_pre, w_in, sinks, mem_norm, w_mem_kv, w_gate, b_gate, w_o_a, w_o_b, w_o_m, w_out, mix_norm_post, ffn2_norm_pre, ffn2_w_in, ffn2_w_out, ffn2_norm_post, loss_target, m_ffn1_norm_pre, m_ffn1_w_in, m_ffn1_w_out, m_ffn1_norm_post, m_mix_norm_pre, m_w_in, m_sinks, m_mem_norm, m_w_mem_kv, m_w_gate, m_b_gate, m_w_o_a, m_w_o_b, m_w_o_m, m_w_out, m_mix_norm_post, m_ffn2_norm_pre, m_ffn2_w_in, m_ffn2_w_out, m_ffn2_norm_post, v_ffn1_norm_pre, v_ffn1_w_in, v_ffn1_w_out, v_ffn1_norm_post, v_mix_norm_pre, v_w_in, v_sinks, v_mem_norm, v_w_mem_kv, v_w_gate, v_b_gate, v_w_o_a, v_w_o_b, v_w_o_m, v_w_out, v_mix_norm_post, v_ffn2_norm_pre, v_ffn2_w_in, v_ffn2_w_out, v_ffn2_norm_post):
    given = dict(locals())
    wt = {k: given[k] for k in WEIGHTS}
    mom = {k: given["m_" + k] for k in WEIGHTS}
    var = {k: given["v_" + k] for k in WEIGHTS}
    chip = (2 * lax.axis_index("x") + lax.axis_index("y")).astype(jnp.int32)
    me = chip.reshape(1)

    def landing_zone(own):
        return lax.dynamic_update_slice_in_dim(lax.empty((N_CHIPS,) + own.shape, own.dtype), own[None], chip, 0)

    shards = [wt[k][0].astype(BF16) for k in BIG]
    members = [[BIG.index(k) for k in GROUPS[g]] for g in GROUP_ORDER]
    sems, shards, lands, token = chip_copies_start(
        shards, [landing_zone(s) for s in shards], members, False, "weight_gather_start")

    def weights_of(group, after):
        idx = members[GROUP_ORDER.index(group)]
        got = chip_copies_wait([shards[i] for i in idx], [lands[i] for i in idx], sems[GROUP_ORDER.index(group)],
                               token if after is None else after, False, f"weight_gather_wait_{group}")
        full = {}
        for k, g in zip(GROUPS[group], got):
            if k in COLUMN_SHARDED:
                if k in ("ffn1_w_in", "ffn2_w_in"):
                    g = jnp.stack([g[0], g[2], g[1], g[3]])
                full[k] = jnp.swapaxes(g, 0, 1).reshape(g.shape[1], N_CHIPS * g.shape[2])
            else:
                full[k] = g.reshape(N_CHIPS * g.shape[1], g.shape[2])
        return full

    in_flight = {}

    def send_grads(group, grads):
        own, wire = [], []
        for k in GROUPS[group]:
            g = grads[k]
            if k in KEPT_SHARD_MAJOR:
                pass
            elif k in COLUMN_SHARDED:
                g = jnp.swapaxes(g.reshape(g.shape[0], N_CHIPS, g.shape[1] // N_CHIPS), 0, 1)
            else:
                g = g.reshape(N_CHIPS, g.shape[0] // N_CHIPS, g.shape[1])
            own.append(g)
            wire.append(g.astype(BF16))
        zones = [landing_zone(lax.dynamic_index_in_dim(b, chip, 0, keepdims=False)) for b in wire]
        pair, wire, zones, sent = chip_copies_start(
            wire, zones, [list(range(len(wire)))], True, f"grad_scatter_start_{group}")
        in_flight[group] = (own, wire, zones, pair[0])
        return sent[0, 0]

    gains = {k: wt[k] for k in GAINS}
    sq, dx, grads = layer_step(x[0], mem[0], loss_target[0], gains, sinks[0], b_gate, weights_of, send_grads)
    loss = lax.psum(0.5 * sq[0, 0] / D_MODEL, ("x", "y", "c"))

    parts = {}
    for group in ("ffn2", "mix", "ffn1"):
        own, wire, zones, pair = in_flight[group]
        received = chip_copies_wait(wire, zones, pair, dx, True, f"grad_scatter_wait_{group}")
        for k, g, r in zip(GROUPS[group], own, received):
            parts[k] = chip_partial_sum(me, g, r, f"{k}_chip_sum")
    parts = [parts[k] for k in BIG]
    sibs = sibling_exchange(parts, "sibling_exchange")
    small_all = small_all_gather(_pack_small(grads), "small_grad_gather")

    res = {}
    for k, p, s in zip(BIG, parts, sibs):
        res[k] = [t[None] for t in adamw_pair(p, s, wt[k][0], mom[k][0], var[k][0], f"{k}_adamw")]
    packed = adamw_small(small_all, _pack_small(wt), _pack_small(mom), _pack_small(var), "small_adamw")
    for idx, p in enumerate(packed):
        for k, t in _unpack_small(p).items():
            res.setdefault(k, [None] * 4)[idx] = t

    return (loss, dx[None], *[res[k][0] for k in WEIGHTS], *[res[k][1] for k in WEIGHTS],
            *[res[k][2] for k in WEIGHTS], *[res[k][3] for k in WEIGHTS])
```

```python
import functools

import jax
import jax.numpy as jnp
from jax import lax
from jax.experimental import pallas as pl
from jax.experimental.pallas import tpu as pltpu

F32 = jnp.float32
BF16 = jnp.bfloat16

D_MODEL = 1024
D_FF = 2816
HEAD = 128
N_CHIPS = 4
N_DEV = 8
EPS = 1e-6
NEG_INF = -1e30
ROPE_THETA = 10000.0
ATT_SCALE = HEAD ** -0.5

ADAM_LR = 0.001
ADAM_B1 = 0.9
ADAM_B2 = 0.999
ADAM_EPS = 1e-08
ADAM_WD = 0.01
ADAM_STEP = 10

VMEM_LIMIT = 52 * 2 ** 20
MESH = pl.DeviceIdType.MESH

QKV_W = 3840
AQ, AK, AV, BQ, BK, BV, MQ = 0, 6, 12, 18, 22, 24, 26
DIL = ((128, 1), (512, 4), (2048, 16))

TM = 512
FF_T = D_FF // 2


def _params(*sem):
    return pltpu.CompilerParams(dimension_semantics=sem, vmem_limit_bytes=VMEM_LIMIT)


def _dot(a, b):
    return jnp.dot(a, b, preferred_element_type=F32)


def _dot_nt(a, b):
    return lax.dot_general(a, b, (((1,), (1,)), ((), ())), preferred_element_type=F32)


def _dot_tn(a, b):
    return lax.dot_general(a, b, (((0,), (0,)), ((), ())), preferred_element_type=F32)


def _rstd(x):
    return lax.rsqrt(jnp.mean(x * x, axis=-1, keepdims=True) + EPS)


def _sigmoid(x):
    return 0.5 * jnp.tanh(0.5 * x) + 0.5


def _ffn_perm(k):
    return (k % 2) * 2 + k // 2


UNREAD = pl.BlockSpec(memory_space=pl.ANY)


def _resident(arr):
    return pl.BlockSpec(arr.shape, lambda *_: (0,) * arr.ndim, pipeline_mode=pl.Buffered(1))


def ffn_in(h, g, w, name):
    T, D = h.shape

    def body(h_ref, g_ref, w_ref, xn_ref, gu_ref, a_ref):
        x = h_ref[...]
        xn = (x * _rstd(x) * g_ref[...]).astype(BF16)
        xn_ref[...] = xn
        for j in range(2):
            gu = _dot(xn, w_ref[:, j * 2 * FF_T:(j + 1) * 2 * FF_T])
            gu_ref[:, j * 2 * FF_T:(j + 1) * 2 * FF_T] = gu.astype(BF16)
            gate, up = gu[:, :FF_T], gu[:, FF_T:]
            a_ref[:, j * FF_T:(j + 1) * FF_T] = (gate * _sigmoid(gate) * up).astype(BF16)

    def rows(width):
        return pl.BlockSpec((TM, width), lambda i: (i, 0))

    return pl.pallas_call(
        body, name=name,
        grid=(T // TM,),
        in_specs=[rows(D), _resident(g), _resident(w)],
        out_specs=[rows(D), rows(2 * D_FF), rows(D_FF)],
        out_shape=[jax.ShapeDtypeStruct((T, D), BF16),
                   jax.ShapeDtypeStruct((T, 2 * D_FF), BF16),
                   jax.ShapeDtypeStruct((T, D_FF), BF16)],
        compiler_params=_params("parallel"),
    )(h, g, w)


def mm_norm_res(a, w, h_in, g, coef, name, target=None):
    T, K = a.shape
    D = w.shape[1]
    final = target is not None

    def body(*refs):
        if final:
            a_ref, w_ref, h_ref, g_ref, t_ref, f_ref, o_ref, l_ref = refs
        else:
            a_ref, w_ref, h_ref, g_ref, f_ref, o_ref = refs
        f = _dot(a_ref[...], w_ref[...])
        f_ref[...] = f
        y = h_ref[...] + coef * (f * _rstd(f) * g_ref[...])
        if final:
            err = y - t_ref[...]
            o_ref[...] = err * (1.0 / D)

            @pl.when(pl.program_id(0) == 0)
            def _():
                l_ref[...] = jnp.zeros_like(l_ref)

            l_ref[...] += jnp.sum(err * err)
        else:
            o_ref[...] = y

    row = pl.BlockSpec((TM, D), lambda i: (i, 0))
    in_specs = [pl.BlockSpec((TM, K), lambda i: (i, 0)),
                _resident(w),
                row, pl.BlockSpec((1, D), lambda i: (0, 0))]
    out_specs = [row, row]
    out_shape = [jax.ShapeDtypeStruct((T, D), F32), jax.ShapeDtypeStruct((T, D), F32)]
    args = [a, w, h_in, g]
    if final:
        in_specs.append(row)
        args.append(target)
        out_specs.append(pl.BlockSpec((8, 128), lambda i: (0, 0)))
        out_shape.append(jax.ShapeDtypeStruct((8, 128), F32))
    return pl.pallas_call(
        body, name=name, grid=(T // TM,), in_specs=in_specs, out_specs=out_specs, out_shape=out_shape,
        compiler_params=_params("arbitrary"),
    )(*args)


def _rope(x, cos, sin_signed):
    return x * cos + pltpu.roll(x, HEAD // 2, axis=1) * sin_signed


def _unrope(x, cos, sin_signed):
    return x * cos - pltpu.roll(x, HEAD // 2, axis=1) * sin_signed


ROTARY_HEADS = tuple(range(AQ, AV)) + tuple(range(BQ, BV))


def mix_in(h, g, w, w_gate, b_gate, cos, sin_signed, name):
    T, D = h.shape
    tn = 768

    def body(h_ref, g_ref, w_ref, wg_ref, b_ref, c_ref, s_ref, u_ref, o_ref, gt_ref):
        x = h_ref[...]
        u = (x * _rstd(x) * g_ref[...]).astype(BF16)
        u_ref[...] = u
        c, s = c_ref[...], s_ref[...]
        for j in range(QKV_W // tn):
            acc = _dot(u, w_ref[:, j * tn:(j + 1) * tn])
            for hd in range(tn // HEAD):
                head = j * (tn // HEAD) + hd
                part = acc[:, hd * HEAD:(hd + 1) * HEAD]
                if head in ROTARY_HEADS:
                    part = _rope(part, c, s)
                o_ref[:, head * HEAD:(head + 1) * HEAD] = part.astype(BF16)
        for j in range(w_gate.shape[1] // tn):
            cols = slice(j * tn, (j + 1) * tn)
            gt_ref[:, cols] = _sigmoid(_dot(u, wg_ref[:, cols]) + b_ref[:, cols]).astype(BF16)

    def rows(width):
        return pl.BlockSpec((TM, width), lambda i: (i, 0))

    return pl.pallas_call(
        body, name=name,
        grid=(T // TM,),
        in_specs=[rows(D), _resident(g), _resident(w), _resident(w_gate), _resident(b_gate), rows(HEAD), rows(HEAD)],
        out_specs=[rows(D), rows(QKV_W), rows(w_gate.shape[1])],
        out_shape=[jax.ShapeDtypeStruct((T, D), BF16), jax.ShapeDtypeStruct((T, QKV_W), BF16),
                   jax.ShapeDtypeStruct((T, w_gate.shape[1]), BF16)],
        compiler_params=_params("parallel"),
    )(h, g, w, w_gate, b_gate, cos, sin_signed)


def gate_merge(gt, o_a, o_b, o_m, w_a, w_b, w_m, name):
    T = gt.shape[0]
    D = D_MODEL

    def body(gt_ref, oa_ref, ob_ref, om_ref, wa_ref, wb_ref, wm_ref, out_ref):
        acc = gt_ref[:, :D].astype(F32) * _dot(oa_ref[...], wa_ref[...])
        acc += gt_ref[:, D:2 * D].astype(F32) * _dot(ob_ref[...], wb_ref[...])
        acc += gt_ref[:, 2 * D:].astype(F32) * _dot(om_ref[...], wm_ref[...])
        out_ref[...] = acc.astype(BF16)

    def rows(width):
        return pl.BlockSpec((TM, width), lambda i: (i, 0))

    def whole(arr):
        return pl.BlockSpec(arr.shape, lambda i: (0, 0))

    return pl.pallas_call(
        body, name=name, grid=(T // TM,),
        in_specs=[rows(3 * D), rows(o_a.shape[1]), rows(o_b.shape[1]), rows(o_m.shape[1]),
                  whole(w_a), whole(w_b), whole(w_m)],
        out_specs=rows(D),
        out_shape=jax.ShapeDtypeStruct((T, D), BF16),
        compiler_params=_params("parallel"),
    )(gt, o_a, o_b, o_m, w_a, w_b, w_m)


def _band_rows(start, r):
    return pl.ds(start, HEAD) if r == 1 else pl.ds(start, HEAD, stride=r)


def _band_mask(max_dist, first_has_prev):
    row = lax.broadcasted_iota(jnp.int32, (HEAD, 2 * HEAD), 0)
    col = lax.broadcasted_iota(jnp.int32, (HEAD, 2 * HEAD), 1)
    dist = row + HEAD - col
    band = (dist >= 0) & (dist <= max_dist)
    return band, band & (col >= jnp.where(first_has_prev, 0, HEAD))


def band_fwd(qkv, sinks, *, r, q_off, k_off, v_off, hkv, grp, max_dist, out_dtype, name):
    T, W = qkv.shape
    SB = HEAD * r
    BT = min(2048, T)
    nsub, nib = BT // SB, T // BT
    hq = hkv * grp

    def body(sink_ref, q_ref, kc_ref, kp_ref, vc_ref, vp_ref, o_ref, l_ref, qf, kf, vf):
        kvh, ib = pl.program_id(0), pl.program_id(1)
        qf[...] = q_ref[...].astype(F32)
        kf[:SB] = kp_ref[...].astype(F32)
        kf[SB:] = kc_ref[...].astype(F32)
        vf[:SB] = vp_ref[...].astype(F32)
        vf[SB:] = vc_ref[...].astype(F32)
        band, band_first = _band_mask(max_dist, ib > 0)
        for j in range(nsub):
            mask = band_first if j == 0 else band
            for c in range(r):
                rows = _band_rows(j * SB + c, r)
                older, own = _band_rows(j * SB + c, r), _band_rows((j + 1) * SB + c, r)
                kcat = jnp.concatenate([kf[older], kf[own]], axis=0).astype(BF16)
                vcat = jnp.concatenate([vf[older], vf[own]], axis=0).astype(BF16)
                for gq in range(grp):
                    cols = slice(gq * HEAD, (gq + 1) * HEAD)
                    s = jnp.where(mask, _dot_nt(qf[rows, cols].astype(BF16), kcat) * ATT_SCALE, NEG_INF)
                    sk = sink_ref[kvh * grp + gq]
                    m = jnp.maximum(jnp.max(s, axis=-1, keepdims=True), sk)
                    p = jnp.exp(s - m)
                    tot = jnp.sum(p, axis=-1, keepdims=True) + jnp.exp(sk - m)
                    o_ref[rows, cols] = (_dot(p.astype(BF16), vcat) / tot).astype(out_dtype)
                    l_ref[rows, cols] = jnp.broadcast_to(m + jnp.log(tot), (HEAD, HEAD))

    def cur(off, width):
        return pl.BlockSpec((BT, width * HEAD), lambda h, i: (i, off // width + h))

    def prev(off):
        return pl.BlockSpec((SB, HEAD), lambda h, i: (jnp.maximum(i * nsub - 1, 0), off + h))

    out_spec = pl.BlockSpec((BT, grp * HEAD), lambda h, i: (i, h))
    return pl.pallas_call(
        body, name=name, grid=(hkv, nib),
        in_specs=[pl.BlockSpec(memory_space=pltpu.SMEM),
                  cur(q_off, grp), cur(k_off, 1), prev(k_off), cur(v_off, 1), prev(v_off)],
        out_specs=[out_spec, out_spec],
        out_shape=[jax.ShapeDtypeStruct((T, hq * HEAD), out_dtype), jax.ShapeDtypeStruct((T, hq * HEAD), F32)],
        scratch_shapes=[pltpu.VMEM((BT, grp * HEAD), F32), pltpu.VMEM((SB + BT, HEAD), F32),
                        pltpu.VMEM((SB + BT, HEAD), F32)],
        compiler_params=_params("parallel", "arbitrary"),
    )(sinks, qkv, qkv, qkv, qkv, qkv)


def band_bwd(qkv, do, o, lse, cos, sin_signed, sinks, *, r, q_off, k_off, v_off, hkv, grp, max_dist, name):
    T, W = qkv.shape
    SB = HEAD * r
    BT = min(2048, T)
    nsub, nib = BT // SB, T // BT
    nblk = T // SB
    hq = hkv * grp
    with_sink = sinks is not None

    def body(*refs):
        if with_sink:
            sink_ref, refs = refs[0], refs[1:]
        (q_ref, qn_ref, kc_ref, kp_ref, vc_ref, vp_ref, do_ref, don_ref, o_ref, on_ref, l_ref, ln_ref,
         c_ref, s_ref) = refs[:14]
        dq_ref, dk_ref, dv_ref = refs[14:17]
        ds_ref = refs[17] if with_sink else None
        qf, dof, of, kf, vf, dqf, dkacc, dvacc = refs[-8:]
        kvh, ib = pl.program_id(0), pl.program_id(1)
        for buf, cur_ref, nxt_ref in ((qf, q_ref, qn_ref), (dof, do_ref, don_ref), (of, o_ref, on_ref)):
            buf[:BT] = cur_ref[...].astype(F32)
            buf[BT:] = nxt_ref[...].astype(F32)
        kf[:SB] = kp_ref[...].astype(F32)
        kf[SB:] = kc_ref[...].astype(F32)
        vf[:SB] = vp_ref[...].astype(F32)
        vf[SB:] = vc_ref[...].astype(F32)
        dkacc[...] = jnp.zeros_like(dkacc)
        dvacc[...] = jnp.zeros_like(dvacc)
        band, band_first = _band_mask(max_dist, ib > 0)
        if with_sink:
            @pl.when(ib == 0)
            def _():
                ds_ref[...] = jnp.zeros_like(ds_ref)

        def grads(rows, cols, logz, keys, vals, mask):
            q, dout = qf[rows, cols].astype(BF16), dof[rows, cols].astype(BF16)
            delta = jnp.sum(dof[rows, cols] * of[rows, cols], axis=-1, keepdims=True)
            s = jnp.where(mask, _dot_nt(q, keys) * ATT_SCALE, NEG_INF)
            p = jnp.exp(s - logz[:, :1])
            ds = (p * (_dot_nt(dout, vals) - delta) * ATT_SCALE).astype(BF16)
            return q, dout, p.astype(BF16), ds, delta

        for j in range(nsub):
            mask = band_first if j == 0 else band
            for c in range(r):
                rows = _band_rows(j * SB + c, r)
                older, own = _band_rows(j * SB + c, r), _band_rows((j + 1) * SB + c, r)
                kcat = jnp.concatenate([kf[older], kf[own]], axis=0).astype(BF16)
                vcat = jnp.concatenate([vf[older], vf[own]], axis=0).astype(BF16)
                for gq in range(grp):
                    cols = slice(gq * HEAD, (gq + 1) * HEAD)
                    logz = l_ref[rows, cols]
                    q, dout, p, ds, delta = grads(rows, cols, logz, kcat, vcat, mask)
                    dqf[rows, cols] = _dot(ds, kcat)
                    dk = _dot_tn(ds, q)
                    dv = _dot_tn(p, dout)
                    dkacc[older] += dk[:HEAD]
                    dkacc[own] += dk[HEAD:]
                    dvacc[older] += dv[:HEAD]
                    dvacc[own] += dv[HEAD:]
                    if with_sink:
                        p_sink = jnp.exp(sink_ref[kvh * grp + gq] - logz[:, :1])
                        ds_ref[gq * 8:(gq + 1) * 8] += jnp.sum(p_sink * delta)

        row = lax.broadcasted_iota(jnp.int32, (HEAD, HEAD), 0)
        col = lax.broadcasted_iota(jnp.int32, (HEAD, HEAD), 1)
        reach = col >= row + jnp.where(ib < nib - 1, HEAD - max_dist, 2 * HEAD)
        for c in range(r):
            last = _band_rows(BT + c, r)
            keys, vals = kf[last].astype(BF16), vf[last].astype(BF16)
            for gq in range(grp):
                cols = slice(gq * HEAD, (gq + 1) * HEAD)
                q, dout, p, ds, _ = grads(last, cols, ln_ref[_band_rows(c, r), cols], keys, vals, reach)
                dkacc[last] += _dot_tn(ds, q)
                dvacc[last] += _dot_tn(p, dout)

        cs, sn = c_ref[...], s_ref[...]
        for gq in range(grp):
            cols = slice(gq * HEAD, (gq + 1) * HEAD)
            dq_ref[:, cols] = _unrope(dqf[:, cols], cs, sn).astype(BF16)
        dk_ref[...] = _unrope(dkacc[SB:], cs, sn).astype(BF16)
        dv_ref[...] = dvacc[SB:].astype(BF16)

    def cur(off, width):
        return pl.BlockSpec((BT, width * HEAD), lambda h, i: (i, off // width + h))

    def prev(off):
        return pl.BlockSpec((SB, HEAD), lambda h, i: (jnp.maximum(i * nsub - 1, 0), off + h))

    def nxt_row(i):
        return jnp.minimum((i + 1) * nsub, nblk - 1)

    q_next = pl.BlockSpec((SB, grp * HEAD), lambda h, i: (nxt_row(i), q_off // grp + h))
    head_cur = pl.BlockSpec((BT, grp * HEAD), lambda h, i: (i, h))
    head_next = pl.BlockSpec((SB, grp * HEAD), lambda h, i: (nxt_row(i), h))
    table = pl.BlockSpec((BT, HEAD), lambda h, i: (i, 0))
    kv_out = pl.BlockSpec((BT, HEAD), lambda h, i: (i, h))

    in_specs = [cur(q_off, grp), q_next, cur(k_off, 1), prev(k_off), cur(v_off, 1), prev(v_off),
                head_cur, head_next, head_cur, head_next, head_cur, head_next, table, table]
    args = [qkv, qkv, qkv, qkv, qkv, qkv, do, do, o, o, lse, lse, cos, sin_signed]
    out_specs = [head_cur, kv_out, kv_out]
    out_shape = [jax.ShapeDtypeStruct((T, hq * HEAD), BF16), jax.ShapeDtypeStruct((T, hkv * HEAD), BF16),
                 jax.ShapeDtypeStruct((T, hkv * HEAD), BF16)]
    if with_sink:
        in_specs.insert(0, pl.BlockSpec(memory_space=pltpu.SMEM))
        args.insert(0, sinks)
        out_specs.append(pl.BlockSpec((None, grp * 8, HEAD), lambda h, i: (h, 0, 0)))
        out_shape.append(jax.ShapeDtypeStruct((hkv, grp * 8, HEAD), F32))
    wide = pltpu.VMEM((BT + SB, grp * HEAD), F32)
    tall = pltpu.VMEM((SB + BT, HEAD), F32)
    return pl.pallas_call(
        body, name=name, grid=(hkv, nib), in_specs=in_specs, out_specs=out_specs, out_shape=out_shape,
        scratch_shapes=[wide, wide, wide, tall, tall, pltpu.VMEM((BT, grp * HEAD), F32), tall, tall],
        compiler_params=_params("parallel", "arbitrary"),
    )(*args)


def merge_groups(outs, lses, name):
    T, Wd = outs[0].shape
    tm = 1024

    def body(o0, o1, o2, l0, l1, l2, out_ref, lt_ref):
        a, b, c = l0[...], l1[...], l2[...]
        m = jnp.maximum(jnp.maximum(a, b), c)
        wa, wb, wc = jnp.exp(a - m), jnp.exp(b - m), jnp.exp(c - m)
        z = wa + wb + wc
        out_ref[...] = ((wa * o0[...] + wb * o1[...] + wc * o2[...]) / z).astype(BF16)
        lt_ref[...] = m + jnp.log(z)

    spec = pl.BlockSpec((tm, Wd), lambda i: (i, 0))
    return pl.pallas_call(
        body, name=name, grid=(T // tm,), in_specs=[spec] * 6, out_specs=[spec, spec],
        out_shape=[jax.ShapeDtypeStruct((T, Wd), BF16), jax.ShapeDtypeStruct((T, Wd), F32)],
        compiler_params=_params("parallel"),
    )(*outs, *lses)


M_HEADS = 4


def mem_kv(mem, g, w, name):
    n, D = mem.shape

    def body(m_ref, g_ref, w_ref, mn_ref, kv_ref):
        x = m_ref[...]
        mn = (x * _rstd(x) * g_ref[...]).astype(BF16)
        mn_ref[...] = mn
        kv_ref[...] = _dot(mn, w_ref[...]).astype(BF16)

    return pl.pallas_call(
        body, name=name,
        out_shape=[jax.ShapeDtypeStruct((n, D), BF16), jax.ShapeDtypeStruct((n, w.shape[1]), BF16)],
        compiler_params=pltpu.CompilerParams(vmem_limit_bytes=VMEM_LIMIT),
    )(mem, g, w)


def mem_fwd(qkv, mkv, name):
    T = qkv.shape[0]
    n = mkv.shape[0]
    RB = 1024

    def body(q_ref, k_ref, v_ref, o_ref, l_ref):
        s = _dot_nt(q_ref[...], k_ref[...]) * ATT_SCALE
        m = jnp.max(s, axis=-1, keepdims=True)
        p = jnp.exp(s - m)
        den = jnp.sum(p, axis=-1, keepdims=True)
        o_ref[...] = (_dot(p.astype(BF16), v_ref[...]) / den).astype(BF16)
        l_ref[...] = jnp.broadcast_to(m + jnp.log(den), (RB, HEAD))

    out = pl.BlockSpec((RB, HEAD), lambda h, i: (i, h))
    return pl.pallas_call(
        body, name=name, grid=(M_HEADS, T // RB),
        in_specs=[pl.BlockSpec((RB, HEAD), lambda h, i: (i, MQ + h)),
                  pl.BlockSpec((n, HEAD), lambda h, i: (0, h)),
                  pl.BlockSpec((n, HEAD), lambda h, i: (0, M_HEADS + h))],
        out_specs=[out, out],
        out_shape=[jax.ShapeDtypeStruct((T, M_HEADS * HEAD), BF16), jax.ShapeDtypeStruct((T, M_HEADS * HEAD), F32)],
        compiler_params=_params("parallel", "parallel"),
    )(qkv, mkv, mkv)


def mem_bwd(qkv, mkv, do, o, lse, name):
    T = qkv.shape[0]
    n = mkv.shape[0]
    RB = 1024

    def body(q_ref, k_ref, v_ref, do_ref, o_ref, l_ref, dq_ref, dk_ref, dv_ref):
        @pl.when(pl.program_id(1) == 0)
        def _():
            dk_ref[...] = jnp.zeros_like(dk_ref)
            dv_ref[...] = jnp.zeros_like(dv_ref)

        q, dout = q_ref[...], do_ref[...]
        delta = jnp.sum(dout.astype(F32) * o_ref[...].astype(F32), axis=-1, keepdims=True)
        p = jnp.exp(_dot_nt(q, k_ref[...]) * ATT_SCALE - l_ref[...][:, :1])
        ds = (p * (_dot_nt(dout, v_ref[...]) - delta) * ATT_SCALE).astype(BF16)
        dq_ref[...] = _dot(ds, k_ref[...]).astype(BF16)
        dk_ref[...] += _dot_tn(ds, q)
        dv_ref[...] += _dot_tn(p.astype(BF16), dout)

    tok = pl.BlockSpec((RB, HEAD), lambda h, i: (i, h))
    slot = pl.BlockSpec((n, HEAD), lambda h, i: (0, h))
    return pl.pallas_call(
        body, name=name, grid=(M_HEADS, T // RB),
        in_specs=[pl.BlockSpec((RB, HEAD), lambda h, i: (i, MQ + h)),
                  slot, pl.BlockSpec((n, HEAD), lambda h, i: (0, M_HEADS + h)), tok, tok, tok],
        out_specs=[tok, slot, slot],
        out_shape=[jax.ShapeDtypeStruct((T, M_HEADS * HEAD), BF16),
                   jax.ShapeDtypeStruct((n, M_HEADS * HEAD), F32),
                   jax.ShapeDtypeStruct((n, M_HEADS * HEAD), F32)],
        compiler_params=_params("parallel", "arbitrary"),
    )(qkv, mkv, mkv, do, o, lse)


def mem_kv_bwd(mem, g, mem_n, w, dmkv, name):
    n, D = mem.shape

    def body(m_ref, g_ref, mn_ref, w_ref, d_ref, dw_ref, dg_ref):
        d = d_ref[...].astype(BF16)
        dw_ref[...] = _dot_tn(mn_ref[...], d)
        x = m_ref[...]
        dg_ref[...] = jnp.sum(_dot_nt(d, w_ref[...]) * (x * _rstd(x)), axis=0, keepdims=True)

    return pl.pallas_call(
        body, name=name,
        out_shape=[jax.ShapeDtypeStruct(w.shape, F32), jax.ShapeDtypeStruct((1, D), F32)],
        compiler_params=pltpu.CompilerParams(vmem_limit_bytes=VMEM_LIMIT),
    )(mem, g, mem_n, w, dmkv)


def _rms_bwd(dn, f, g):
    r = _rstd(f)
    fhat = f * r
    dfhat = dn * g
    df = r * (dfhat - fhat * jnp.mean(dfhat * fhat, axis=-1, keepdims=True))
    return df, jnp.sum(dn * fhat, axis=0, keepdims=True)


TM_FFN_BWD = 256


def ffn_tokens_bwd(dh, f, h_in, gu, g_pre, g_post, w_in, w_out, coef, name, after):
    T, D = dh.shape
    tm = TM_FFN_BWD

    def body(dh_ref, f_ref, h_ref, gu_ref, gpre_ref, gpost_ref, win_ref, wout_ref, _,
             df_ref, dgu_ref, dhin_ref, dgpre_ref, dgpost_ref):
        @pl.when(pl.program_id(0) == 0)
        def _():
            dgpre_ref[...] = jnp.zeros_like(dgpre_ref)
            dgpost_ref[...] = jnp.zeros_like(dgpost_ref)

        dh = dh_ref[...]
        df, dg_post = _rms_bwd(coef * dh, f_ref[...], gpost_ref[...])
        dgpost_ref[...] += dg_post
        df = df.astype(BF16)
        df_ref[...] = df
        dxn = jnp.zeros((tm, D), F32)
        for j in range(2):
            lo, mid, hi = 2 * j * FF_T, (2 * j + 1) * FF_T, (2 * j + 2) * FF_T
            da = _dot_nt(df, wout_ref[j * FF_T:(j + 1) * FF_T, :])
            gate = gu_ref[:, lo:mid].astype(F32)
            up = gu_ref[:, mid:hi].astype(F32)
            sig = _sigmoid(gate)
            dgate = (da * up * sig * (1.0 + gate * (1.0 - sig))).astype(BF16)
            dup = (da * gate * sig).astype(BF16)
            dgu_ref[:, lo:mid] = dgate
            dgu_ref[:, mid:hi] = dup
            dxn += _dot_nt(dgate, win_ref[:, lo:mid]) + _dot_nt(dup, win_ref[:, mid:hi])
        h = h_ref[...]
        r = _rstd(h)
        xhat = h * r
        dxhat = dxn * gpre_ref[...]
        dhin_ref[...] = dh + r * (dxhat - xhat * jnp.mean(dxhat * xhat, axis=-1, keepdims=True))
        dgpre_ref[...] += jnp.sum(dxn * xhat, axis=0, keepdims=True)

    def rows(width):
        return pl.BlockSpec((tm, width), lambda i: (i, 0))

    vec = pl.BlockSpec((1, D), lambda i: (0, 0))
    return pl.pallas_call(
        body, name=name, grid=(T // tm,),
        in_specs=[rows(D), rows(D), rows(D), rows(2 * D_FF), _resident(g_pre), _resident(g_post),
                  _resident(w_in), _resident(w_out), UNREAD],
        out_specs=[rows(D), rows(2 * D_FF), rows(D), vec, vec],
        out_shape=[jax.ShapeDtypeStruct((T, D), BF16), jax.ShapeDtypeStruct((T, 2 * D_FF), BF16),
                   jax.ShapeDtypeStruct((T, D), F32), jax.ShapeDtypeStruct((1, D), F32),
                   jax.ShapeDtypeStruct((1, D), F32)],
        compiler_params=_params("arbitrary"),
    )(dh, f, h_in, gu, g_pre, g_post, w_in, w_out, after)


def mix_out_bwd(dh, f, g, w_out, name, after):
    T, D = dh.shape

    def body(dh_ref, f_ref, g_ref, w_ref, _, df_ref, dm_ref, dg_ref):
        df, dg = _rms_bwd(dh_ref[...], f_ref[...], g_ref[...])
        df = df.astype(BF16)
        df_ref[...] = df

        @pl.when(pl.program_id(0) == 0)
        def _():
            dg_ref[...] = jnp.zeros_like(dg_ref)

        dg_ref[...] += dg
        dm_ref[...] = _dot_nt(df, w_ref[...]).astype(BF16)

    row = pl.BlockSpec((TM, D), lambda i: (i, 0))
    vec = pl.BlockSpec((1, D), lambda i: (0, 0))
    return pl.pallas_call(
        body, name=name, grid=(T // TM,),
        in_specs=[row, row, vec, _resident(w_out), UNREAD],
        out_specs=[row, row, vec],
        out_shape=[jax.ShapeDtypeStruct((T, D), BF16), jax.ShapeDtypeStruct((T, D), BF16),
                   jax.ShapeDtypeStruct((1, D), F32)],
        compiler_params=_params("arbitrary"),
    )(dh, f, g, w_out, after)


def mm_nt_norm_bwd(pieces, h_in, dh_out, g, name, after):
    T, D = h_in.shape

    def body(*refs):
        ab = refs[:2 * len(pieces)]
        h_ref, dh_ref, g_ref, _, o_ref, dg_ref = refs[2 * len(pieces):]
        dxn = _dot_nt(ab[0][...], ab[1][...])
        for p in range(1, len(pieces)):
            dxn += _dot_nt(ab[2 * p][...], ab[2 * p + 1][...])
        h = h_ref[...]
        r = _rstd(h)
        xhat = h * r
        dxhat = dxn * g_ref[...]
        o_ref[...] = dh_ref[...] + r * (dxhat - xhat * jnp.mean(dxhat * xhat, axis=-1, keepdims=True))

        @pl.when(pl.program_id(0) == 0)
        def _():
            dg_ref[...] = jnp.zeros_like(dg_ref)

        dg_ref[...] += jnp.sum(dxn * xhat, axis=0, keepdims=True)

    in_specs, args = [], []
    for a, w in pieces:
        in_specs += [pl.BlockSpec((TM, a.shape[1]), lambda i: (i, 0)), _resident(w)]
        args += [a, w]
    row = pl.BlockSpec((TM, D), lambda i: (i, 0))
    return pl.pallas_call(
        body, name=name, grid=(T // TM,),
        in_specs=in_specs + [row, row, _resident(g), UNREAD],
        out_specs=[row, pl.BlockSpec((1, D), lambda i: (0, 0))],
        out_shape=[jax.ShapeDtypeStruct((T, D), F32), jax.ShapeDtypeStruct((1, D), F32)],
        compiler_params=_params("arbitrary"),
    )(*args, h_in, dh_out, g, after)


def gate_merge_bwd(dm, gt, o_a, o_b, o_m, w_a, w_b, w_m, name):
    T = dm.shape[0]
    D = D_MODEL
    branch = ((o_a, w_a), (o_b, w_b), (o_m, w_m))

    def body(dm_ref, gt_ref, oa_ref, ob_ref, om_ref, wa_ref, wb_ref, wm_ref,
             dgt_ref, dpa_ref, dpb_ref, dpm_ref, doa_ref, dob_ref, dom_ref, db_ref):
        @pl.when(pl.program_id(0) == 0)
        def _():
            db_ref[...] = jnp.zeros_like(db_ref)

        dmf = dm_ref[...].astype(F32)
        for x, (o_ref, w_ref, dp_ref, do_ref) in enumerate(((oa_ref, wa_ref, dpa_ref, doa_ref),
                                                           (ob_ref, wb_ref, dpb_ref, dob_ref),
                                                           (om_ref, wm_ref, dpm_ref, dom_ref))):
            cols = slice(x * D, (x + 1) * D)
            gx = gt_ref[:, cols].astype(F32)
            w = w_ref[...]
            dpre = dmf * _dot(o_ref[...], w) * gx * (1.0 - gx)
            dgt_ref[:, cols] = dpre.astype(BF16)
            db_ref[:, cols] += jnp.sum(dpre, axis=0, keepdims=True)
            dp = (dmf * gx).astype(BF16)
            dp_ref[...] = dp
            do_ref[...] = _dot_nt(dp, w).astype(BF16)

    def rows(width):
        return pl.BlockSpec((TM, width), lambda i: (i, 0))

    def whole(arr):
        return pl.BlockSpec(arr.shape, lambda i: (0, 0))

    widths = [o.shape[1] for o, _ in branch]
    return pl.pallas_call(
        body, name=name, grid=(T // TM,),
        in_specs=[rows(D), rows(3 * D)] + [rows(k) for k in widths] + [whole(w) for _, w in branch],
        out_specs=[rows(3 * D), rows(D), rows(D), rows(D)] + [rows(k) for k in widths]
                  + [pl.BlockSpec((1, 3 * D), lambda i: (0, 0))],
        out_shape=[jax.ShapeDtypeStruct((T, 3 * D), BF16)] + [jax.ShapeDtypeStruct((T, D), BF16)] * 3
                  + [jax.ShapeDtypeStruct((T, k), BF16) for k in widths]
                  + [jax.ShapeDtypeStruct((1, 3 * D), F32)],
        compiler_params=_params("arbitrary"),
    )(dm, gt, o_a, o_b, o_m, w_a, w_b, w_m)


def mm_tn(x, dy, tm, tn, name, shard_major=False, perm=None, slabs=1, after=None):
    T, M = x.shape
    N = dy.shape[1]
    tk = min(1024, T)
    perm = perm or (lambda j: j)
    w = tn // slabs

    def body(x_ref, dy_ref, *rest):
        o_ref = rest[-1]

        @pl.when(pl.program_id(2) == 0)
        def _():
            o_ref[...] = jnp.zeros_like(o_ref)

        acc = _dot_tn(x_ref[...], dy_ref[...])
        if shard_major:
            for s in range(slabs):
                o_ref[s] += acc[:, s * w:(s + 1) * w]
        else:
            o_ref[...] += acc

    if shard_major:
        out_spec = pl.BlockSpec((slabs, tm, w), lambda i, j, k: (perm(j), i, 0))
        out_shape = jax.ShapeDtypeStruct((N // w, M, w), F32)
    else:
        out_spec = pl.BlockSpec((tm, tn), lambda i, j, k: (i, j))
        out_shape = jax.ShapeDtypeStruct((M, N), F32)
    return pl.pallas_call(
        body, name=name, grid=(M // tm, N // tn, T // tk),
        in_specs=[pl.BlockSpec((tk, tm), lambda i, j, k: (k, i)),
                  pl.BlockSpec((tk, tn), lambda i, j, k: (k, j))] + ([] if after is None else [UNREAD]),
        out_specs=out_spec, out_shape=out_shape,
        compiler_params=_params("parallel", "parallel", "arbitrary"),
    )(x, dy, *([] if after is None else [after]))


def rope_tables(T):
    half = HEAD // 2
    inv = ROPE_THETA ** (-jnp.arange(half, dtype=F32) / half)
    ang = jnp.arange(T).astype(F32)[:, None] * inv[None, :]
    cos, sin = jnp.cos(ang), jnp.sin(ang)
    return jnp.concatenate([cos, cos], axis=1), jnp.concatenate([-sin, sin], axis=1)


def layer_step(x, mem, target, gains, sinks, b_gate, weights_of, send_grads):
    T = x.shape[0]
    cos, sin_signed = rope_tables(T)
    no_sink = jnp.full((2,), NEG_INF, F32)

    w = dict(weights_of("ffn1_in", None))
    xn1, gu1, a1 = ffn_in(x, gains["ffn1_norm_pre"], w["ffn1_w_in"], "ffn1_in")
    w.update(weights_of("ffn1_out", xn1))
    f1, h1 = mm_norm_res(a1, w["ffn1_w_out"], x, gains["ffn1_norm_post"], 0.5, "ffn1_out")
    w.update(weights_of("mix", f1))
    u, qkv, gt = mix_in(h1, gains["mix_norm_pre"], w["w_in"], w["w_gate"], b_gate, cos, sin_signed, "mix_in")
    outs, lses = [], []
    for gidx, (window, dil) in enumerate(DIL):
        o_g, l_g = band_fwd(qkv, no_sink, r=dil, q_off=AQ + 2 * gidx, k_off=AK + 2 * gidx, v_off=AV + 2 * gidx,
                            hkv=2, grp=1, max_dist=window // dil, out_dtype=F32, name=f"attn_a{gidx}_fwd")
        outs.append(o_g)
        lses.append(l_g)
    o_a, l_a = merge_groups(outs, lses, "attn_a_merge")
    o_b, l_b = band_fwd(qkv, sinks, r=1, q_off=BQ, k_off=BK, v_off=BV, hkv=2, grp=2, max_dist=HEAD - 1,
                        out_dtype=BF16, name="attn_b_fwd")
    mem_n, mkv = mem_kv(mem, gains["mem_norm"], w["w_mem_kv"], "mem_kv")
    o_m, l_m = mem_fwd(qkv, mkv, "attn_m_fwd")
    merged = gate_merge(gt, o_a, o_b, o_m, w["w_o_a"], w["w_o_b"], w["w_o_m"], "gate_merge")
    mo, h2 = mm_norm_res(merged, w["w_out"], h1, gains["mix_norm_post"], 1.0, "mix_out")
    w.update(weights_of("ffn2", mo))
    xn2, gu2, a2 = ffn_in(h2, gains["ffn2_norm_pre"], w["ffn2_w_in"], "ffn2_in")
    f2, dy, sq = mm_norm_res(a2, w["ffn2_w_out"], h2, gains["ffn2_norm_post"], 0.5, "ffn2_out", target=target)

    grads = {}

    def ffn_bwd(tag, dh_out, f, gu, a, xn, h_in, after):
        df, dgu, dh_in, grads[f"{tag}_norm_pre"], grads[f"{tag}_norm_post"] = ffn_tokens_bwd(
            dh_out, f, h_in, gu, gains[f"{tag}_norm_pre"], gains[f"{tag}_norm_post"], w[f"{tag}_w_in"],
            w[f"{tag}_w_out"], 0.5, f"{tag}_tokens_bwd", after)
        sent = send_grads(f"{tag}_out", {f"{tag}_w_out": mm_tn(a, df, FF_T, D_MODEL, f"{tag}_w_out_grad")})
        sent = send_grads(f"{tag}_in", {f"{tag}_w_in": mm_tn(
            xn, dgu, D_MODEL, FF_T, f"{tag}_w_in_grad", shard_major=True, perm=_ffn_perm, after=sent)})
        return dh_in, sent

    dh2, sent = ffn_bwd("ffn2", dy, f2, gu2, a2, xn2, h2, dy)

    mix = {}
    dmo, dmerged, grads["mix_norm_post"] = mix_out_bwd(
        dh2, mo, gains["mix_norm_post"], w["w_out"], "mix_out_bwd", sent)
    mix["w_out"] = mm_tn(merged, dmo, D_MODEL, D_MODEL, "w_out_grad")
    dgt, dpa, dpb, dpm, do_a, do_b, do_m, grads["b_gate"] = gate_merge_bwd(
        dmerged, gt, o_a, o_b, o_m, w["w_o_a"], w["w_o_b"], w["w_o_m"], "gate_merge_bwd")
    mix["w_o_a"] = mm_tn(o_a, dpa, o_a.shape[1], D_MODEL, "w_o_a_grad")
    mix["w_o_b"] = mm_tn(o_b, dpb, o_b.shape[1], D_MODEL, "w_o_b_grad")
    mix["w_o_m"] = mm_tn(o_m, dpm, o_m.shape[1], D_MODEL, "w_o_m_grad")

    dq_a, dk_a, dv_a = [], [], []
    for gidx, (window, dil) in enumerate(DIL):
        dq, dk, dv = band_bwd(qkv, do_a, o_a, l_a, cos, sin_signed, None, r=dil, q_off=AQ + 2 * gidx,
                              k_off=AK + 2 * gidx, v_off=AV + 2 * gidx, hkv=2, grp=1, max_dist=window // dil,
                              name=f"attn_a{gidx}_bwd")
        dq_a.append(dq)
        dk_a.append(dk)
        dv_a.append(dv)
    dq_b, dk_b, dv_b, dsink = band_bwd(qkv, do_b, o_b, l_b, cos, sin_signed, sinks, r=1, q_off=BQ, k_off=BK,
                                       v_off=BV, hkv=2, grp=2, max_dist=HEAD - 1, name="attn_b_bwd")
    grads["sinks"] = -dsink[:, ::8, 0].reshape(1, 4)
    dq_m, dmk, dmv = mem_bwd(qkv, mkv, do_m, o_m, l_m, "attn_m_bwd")
    mix["w_mem_kv"], grads["mem_norm"] = mem_kv_bwd(
        mem, gains["mem_norm"], mem_n, w["w_mem_kv"], jnp.concatenate([dmk, dmv], axis=1), "mem_kv_bwd")
    dqkv = jnp.concatenate(dq_a + dk_a + dv_a + [dq_b, dk_b, dv_b, dq_m], axis=1)

    mix["w_in"] = mm_tn(u, dqkv, D_MODEL, 1280, "w_in_grad")
    mix["w_gate"] = mm_tn(u, dgt, D_MODEL, 1536, "w_gate_grad", shard_major=True, slabs=2)
    sent = send_grads("mix", mix)
    dh1, grads["mix_norm_pre"] = mm_nt_norm_bwd(
        [(dqkv, w["w_in"]), (dgt, w["w_gate"])], h1, dh2, gains["mix_norm_pre"], "mix_in_bwd", sent)

    dx, _ = ffn_bwd("ffn1", dh1, f1, gu1, a1, xn1, x, dh1)
    return sq, dx, grads


def _place():
    return lax.axis_index("x"), lax.axis_index("y"), lax.axis_index("c")


def _other_chips(x, y):
    return [(1 - x, y), (x, 1 - y), (1 - x, 1 - y)]


def _hbm(n):
    return [pl.BlockSpec(memory_space=pltpu.HBM)] * n


SEM = pl.BlockSpec(memory_space=pltpu.SEMAPHORE)
SIDE_EFFECT = pltpu.SideEffectType.DATAFLOW_SIDE_EFFECTING


def _chip_copy(src, land, sems, i, j, dst_slot, scatter):
    x, y, c = _place()
    px, py = _other_chips(x, y)[j]
    send_sems, recv_sems = sems
    return pltpu.make_async_remote_copy(
        src_ref=src[i].at[2 * px + py] if scatter else src[i], dst_ref=land[i].at[dst_slot],
        send_sem=send_sems.at[3 * i + j], recv_sem=recv_sems.at[3 * i + j],
        device_id=(px, py, c), device_id_type=MESH)


def chip_copies_start(srcs, lands, groups, scatter, name):
    n = len(srcs)

    def body(*refs):
        src, land = refs[:n], refs[n:2 * n]
        sems = refs[2 * n:2 * n + 2 * len(groups)]
        token = refs[-1]
        x, y, _ = _place()
        for g, members in enumerate(groups):
            part = ([src[i] for i in members], [land[i] for i in members])
            for t in range(len(members)):
                for j in range(3):
                    _chip_copy(*part, sems[2 * g:2 * g + 2], t, j, 2 * x + y, scatter).start()
        token[...] = jnp.zeros_like(token)

    sem_shapes = [pltpu.SemaphoreType.DMA((3 * len(m),)) for m in groups for _ in range(2)]
    thru = [pltpu.HBM(a.shape, a.dtype) for a in (*srcs, *lands)]
    res = pl.pallas_call(
        body, name=name,
        out_shape=(*sem_shapes, *thru, jax.ShapeDtypeStruct((8, 128), F32)),
        in_specs=_hbm(2 * n),
        out_specs=(*[SEM] * len(sem_shapes), *_hbm(2 * n), pl.BlockSpec(memory_space=pltpu.VMEM)),
        input_output_aliases={i: len(sem_shapes) + i for i in range(2 * n)},
        compiler_params=pltpu.CompilerParams(has_side_effects=SIDE_EFFECT),
    )(*[pltpu.with_memory_space_constraint(a, pltpu.HBM) for a in (*srcs, *lands)])
    k = len(sem_shapes)
    sems = [tuple(res[2 * g:2 * g + 2]) for g in range(len(groups))]
    return sems, list(res[k:k + n]), list(res[k + n:k + 2 * n]), res[-1]


def chip_copies_wait(srcs, lands, sems, after, scatter, name):
    n = len(srcs)

    def body(*refs):
        src, land = refs[:n], refs[n:2 * n]
        pair = refs[2 * n:2 * n + 2]
        x, y, _ = _place()
        for i in range(n):
            for j, (px, py) in enumerate(_other_chips(x, y)):
                copy = _chip_copy(src, land, pair, i, j, 2 * px + py, scatter)
                copy.wait_send()
                copy.wait_recv()

    res = pl.pallas_call(
        body, name=name,
        out_shape=[pltpu.HBM(a.shape, a.dtype) for a in (*srcs, *lands)],
        in_specs=[*_hbm(2 * n), SEM, SEM, pl.BlockSpec(memory_space=pl.ANY)],
        out_specs=_hbm(2 * n),
        input_output_aliases={i: i for i in range(2 * n)},
        compiler_params=pltpu.CompilerParams(has_side_effects=SIDE_EFFECT),
    )(*srcs, *lands, *sems, after)
    return list(res[n:])


def small_all_gather(small, name):
    flips = [(fx, fy, fc) for fx in (0, 1) for fy in (0, 1) for fc in (0, 1)][1:]

    def body(in_ref, out_ref, send_sems, recv_sems, local_sem):
        x, y, c = _place()
        me = 4 * x + 2 * y + c

        def copy(k, slot):
            fx, fy, fc = flips[k]
            return pltpu.make_async_remote_copy(
                src_ref=in_ref, dst_ref=out_ref.at[slot], send_sem=send_sems.at[k], recv_sem=recv_sems.at[k],
                device_id=(x ^ fx, y ^ fy, c ^ fc), device_id_type=MESH)

        local = pltpu.make_async_copy(in_ref, out_ref.at[me], local_sem)
        local.start()
        for k in range(len(flips)):
            copy(k, me).start()
        for k, (fx, fy, fc) in enumerate(flips):
            copy(k, 4 * (x ^ fx) + 2 * (y ^ fy) + (c ^ fc)).wait()
        local.wait()

    return pl.pallas_call(
        body, name=name, in_specs=_hbm(1), out_specs=_hbm(1)[0],
        out_shape=jax.ShapeDtypeStruct((N_DEV,) + small.shape, small.dtype),
        scratch_shapes=[pltpu.SemaphoreType.DMA((len(flips),)), pltpu.SemaphoreType.DMA((len(flips),)),
                        pltpu.SemaphoreType.DMA],
    )(small)


def sibling_exchange(parts, name):
    n = len(parts)

    def body(*refs):
        ins, outs = refs[:n], refs[n:2 * n]
        send_sems, recv_sems = refs[2 * n:]
        x, y, c = _place()
        copies = [pltpu.make_async_remote_copy(
            src_ref=ins[i], dst_ref=outs[i], send_sem=send_sems.at[i], recv_sem=recv_sems.at[i],
            device_id=(x, y, 1 - c), device_id_type=MESH) for i in range(n)]
        for cp in copies:
            cp.start()
        for cp in copies:
            cp.wait()

    return pl.pallas_call(
        body, name=name, in_specs=_hbm(n), out_specs=_hbm(n),
        out_shape=[jax.ShapeDtypeStruct(p.shape, p.dtype) for p in parts],
        scratch_shapes=[pltpu.SemaphoreType.DMA((n,)), pltpu.SemaphoreType.DMA((n,))],
    )(*parts)


def _row_tile(rows):
    for t in (256, 176, 128, 64, 32, 16, 8):
        if rows % t == 0:
            return t
    return rows


def chip_partial_sum(me, own_sm, recv, name):
    _, rows, cols = own_sm.shape
    tr = _row_tile(rows)

    def body(me_ref, own_ref, r0, r1, r2, r3, o_ref):
        acc = jnp.zeros((tr, cols), F32)
        for s, r_ref in enumerate((r0, r1, r2, r3)):
            acc = acc + jnp.where(me_ref[0] == s, own_ref[...], r_ref[...].astype(F32))
        o_ref[...] = acc

    def slot(s):
        return pl.BlockSpec((None, tr, cols), lambda i, me_ref, s=s: (s, i, 0))

    return pl.pallas_call(
        body, name=name,
        grid_spec=pltpu.PrefetchScalarGridSpec(
            num_scalar_prefetch=1, grid=(rows // tr,),
            in_specs=[pl.BlockSpec((None, tr, cols), lambda i, me_ref: (me_ref[0], i, 0))] + [slot(s) for s in range(4)],
            out_specs=pl.BlockSpec((tr, cols), lambda i, me_ref: (i, 0))),
        out_shape=jax.ShapeDtypeStruct((rows, cols), F32),
        compiler_params=_params("parallel"),
    )(me, own_sm, recv, recv, recv, recv)


def _adamw(w, g, m, v):
    m = ADAM_B1 * m + (1.0 - ADAM_B1) * g
    v = ADAM_B2 * v + (1.0 - ADAM_B2) * (g * g)
    m_hat = m / (1.0 - ADAM_B1 ** ADAM_STEP)
    v_hat = v / (1.0 - ADAM_B2 ** ADAM_STEP)
    delta = -ADAM_LR * (m_hat / (jnp.sqrt(v_hat) + ADAM_EPS) + ADAM_WD * w)
    return delta, m, v


def adamw_pair(part, sib, w, m, v, name):
    rows, cols = w.shape
    tr = _row_tile(rows)

    def body(p_ref, s_ref, w_ref, m_ref, v_ref, g_ref, d_ref, nm_ref, nv_ref):
        g = p_ref[...] + s_ref[...]
        g_ref[...] = g
        d_ref[...], nm_ref[...], nv_ref[...] = _adamw(w_ref[...], g, m_ref[...], v_ref[...])

    spec = pl.BlockSpec((tr, cols), lambda i: (i, 0))
    return pl.pallas_call(
        body, name=name, grid=(rows // tr,), in_specs=[spec] * 5, out_specs=[spec] * 4,
        out_shape=[jax.ShapeDtypeStruct((rows, cols), F32)] * 4,
        compiler_params=_params("parallel"),
    )(part, sib, w, m, v)


def adamw_small(g_all, w, m, v, name):
    def body(ga_ref, w_ref, m_ref, v_ref, g_ref, d_ref, nm_ref, nv_ref):
        g = ga_ref[0]
        for k in range(1, N_DEV):
            g = g + ga_ref[k]
        g_ref[...] = g
        d_ref[...], nm_ref[...], nv_ref[...] = _adamw(w_ref[...], g, m_ref[...], v_ref[...])

    return pl.pallas_call(
        body, name=name, out_shape=[jax.ShapeDtypeStruct(w.shape, F32)] * 4,
    )(g_all, w, m, v)


WEIGHTS = ("ffn1_norm_pre", "ffn1_w_in", "ffn1_w_out", "ffn1_norm_post", "mix_norm_pre", "w_in", "sinks",
           "mem_norm", "w_mem_kv", "w_gate", "b_gate", "w_o_a", "w_o_b", "w_o_m", "w_out", "mix_norm_post",
           "ffn2_norm_pre", "ffn2_w_in", "ffn2_w_out", "ffn2_norm_post")
BIG = ("ffn1_w_in", "ffn1_w_out", "w_in", "w_mem_kv", "w_gate", "w_o_a", "w_o_b", "w_o_m", "w_out",
       "ffn2_w_in", "ffn2_w_out")
GATHER_ORDER = ("ffn1_in", "ffn1_out", "mix", "ffn2")
GATHER_GROUPS = {"ffn1_in": ("ffn1_w_in",), "ffn1_out": ("ffn1_w_out",),
                 "mix": ("w_in", "w_gate", "w_mem_kv", "w_o_a", "w_o_b", "w_o_m", "w_out"),
                 "ffn2": ("ffn2_w_in", "ffn2_w_out")}
GROUPS = {"ffn1_in": ("ffn1_w_in",), "ffn1_out": ("ffn1_w_out",),
          "mix": ("w_in", "w_gate", "w_mem_kv", "w_o_a", "w_o_b", "w_o_m", "w_out"),
          "ffn2_in": ("ffn2_w_in",), "ffn2_out": ("ffn2_w_out",)}
COLUMN_SHARDED = ("ffn1_w_in", "ffn2_w_in", "w_in", "w_gate", "w_o_a", "w_o_b", "w_o_m")
KEPT_SHARD_MAJOR = ("ffn1_w_in", "ffn2_w_in", "w_gate")
GAINS = ("ffn1_norm_pre", "ffn1_norm_post", "mix_norm_pre", "mem_norm", "mix_norm_post", "ffn2_norm_pre",
         "ffn2_norm_post")
SMALL_ROWS = 16


def _pack_small(t):
    sinks = jnp.pad(t["sinks"], ((0, 0), (0, D_MODEL - t["sinks"].shape[1])))
    rows = [t[k] for k in GAINS] + [t["b_gate"].reshape(3, D_MODEL), sinks]
    packed = jnp.concatenate(rows, axis=0)
    return jnp.pad(packed, ((0, SMALL_ROWS - packed.shape[0]), (0, 0)))


def _unpack_small(p):
    out = {k: p[i:i + 1] for i, k in enumerate(GAINS)}
    out["b_gate"] = p[7:10].reshape(1, 3 * D_MODEL)
    out["sinks"] = p[10:11, :4]
    return out


def kernel(x, mem, ffn1_norm_pre, ffn1_w_in, ffn1_w_out, ffn1_norm_post, mix_norm_pre, w_in, sinks, mem_norm, w_mem_kv, w_gate, b_gate, w_o_a, w_o_b, w_o_m, w_out, mix_norm_post, ffn2_norm_pre, ffn2_w_in, ffn2_w_out, ffn2_norm_post, loss_target, m_ffn1_norm_pre, m_ffn1_w_in, m_ffn1_w_out, m_ffn1_norm_post, m_mix_norm_pre, m_w_in, m_sinks, m_mem_norm, m_w_mem_kv, m_w_gate, m_b_gate, m_w_o_a, m_w_o_b, m_w_o_m, m_w_out, m_mix_norm_post, m_ffn2_norm_pre, m_ffn2_w_in, m_ffn2_w_out, m_ffn2_norm_post, v_ffn1_norm_pre, v_ffn1_w_in, v_ffn1_w_out, v_ffn1_norm_post, v_mix_norm_pre, v_w_in, v_sinks, v_mem_norm, v_w_mem_kv, v_w_gate, v_b_gate, v_w_o_a, v_w_o_b, v_w_o_m, v_w_out, v_mix_norm_post, v_ffn2_norm_pre, v_ffn2_w_in, v_ffn2_w_out, v_ffn2_norm_post):
    given = dict(locals())
    wt = {k: given[k] for k in WEIGHTS}
    mom = {k: given["m_" + k] for k in WEIGHTS}
    var = {k: given["v_" + k] for k in WEIGHTS}
    chip = (2 * lax.axis_index("x") + lax.axis_index("y")).astype(jnp.int32)
    me = chip.reshape(1)

    def landing_zone(own):
        return lax.dynamic_update_slice_in_dim(lax.empty((N_CHIPS,) + own.shape, own.dtype), own[None], chip, 0)

    shards = [wt[k][0].astype(BF16) for k in BIG]
    members = [[BIG.index(k) for k in GATHER_GROUPS[g]] for g in GATHER_ORDER]
    sems, shards, lands, token = chip_copies_start(
        shards, [landing_zone(s) for s in shards], members, False, "weight_gather_start")

    def weights_of(group, after):
        idx = members[GATHER_ORDER.index(group)]
        got = chip_copies_wait([shards[i] for i in idx], [lands[i] for i in idx], sems[GATHER_ORDER.index(group)],
                               token if after is None else after, False, f"weight_gather_wait_{group}")
        full = {}
        for k, g in zip(GATHER_GROUPS[group], got):
            if k in COLUMN_SHARDED:
                if k in ("ffn1_w_in", "ffn2_w_in"):
                    g = jnp.stack([g[0], g[2], g[1], g[3]])
                full[k] = jnp.swapaxes(g, 0, 1).reshape(g.shape[1], N_CHIPS * g.shape[2])
            else:
                full[k] = g.reshape(N_CHIPS * g.shape[1], g.shape[2])
        return full

    in_flight = {}

    def send_grads(group, grads):
        own, wire = [], []
        for k in GROUPS[group]:
            g = grads[k]
            if k in KEPT_SHARD_MAJOR:
                pass
            elif k in COLUMN_SHARDED:
                g = jnp.swapaxes(g.reshape(g.shape[0], N_CHIPS, g.shape[1] // N_CHIPS), 0, 1)
            else:
                g = g.reshape(N_CHIPS, g.shape[0] // N_CHIPS, g.shape[1])
            own.append(g)
            wire.append(g.astype(BF16))
        zones = [landing_zone(lax.dynamic_index_in_dim(b, chip, 0, keepdims=False)) for b in wire]
        pair, wire, zones, sent = chip_copies_start(
            wire, zones, [list(range(len(wire)))], True, f"grad_scatter_start_{group}")
        in_flight[group] = (own, wire, zones, pair[0], sent)
        return sent

    gains = {k: wt[k] for k in GAINS}
    sq, dx, grads = layer_step(x[0], mem[0], loss_target[0], gains, sinks[0], b_gate, weights_of, send_grads)
    loss = lax.psum(0.5 * sq[0, 0] / D_MODEL, ("x", "y", "c"))

    res = {}
    after = in_flight["ffn1_in"][4]
    for stage in (("ffn2_out", "ffn2_in", "mix", "ffn1_out"), ("ffn1_in",)):
        names, parts = [], []
        for group in stage:
            own, wire, zones, pair, _ = in_flight[group]
            received = chip_copies_wait(wire, zones, pair, after, True, f"grad_scatter_wait_{group}")
            for k, g, r in zip(GROUPS[group], own, received):
                names.append(k)
                parts.append(chip_partial_sum(me, g, r, f"{k}_chip_sum"))
        sibs = sibling_exchange(parts, f"sibling_exchange_{stage[-1]}")
        for k, p, s in zip(names, parts, sibs):
            res[k] = [t[None] for t in adamw_pair(p, s, wt[k][0], mom[k][0], var[k][0], f"{k}_adamw")]
        after = res[names[-1]][0]
    small_all = small_all_gather(_pack_small(grads), "small_grad_gather")
    packed = adamw_small(small_all, _pack_small(wt), _pack_small(mom), _pack_small(var), "small_adamw")
    for idx, p in enumerate(packed):
        for k, t in _unpack_small(p).items():
            res.setdefault(k, [None] * 4)[idx] = t

    return (loss, dx[None], *[res[k][0] for k in WEIGHTS], *[res[k][1] for k in WEIGHTS],
            *[res[k][2] for k in WEIGHTS], *[res[k][3] for k in WEIGHTS])
```

```python
import functools

import jax
import jax.numpy as jnp
from jax import lax
from jax.experimental import pallas as pl
from jax.experimental.pallas import tpu as pltpu

F32 = jnp.float32
BF16 = jnp.bfloat16

D_MODEL = 1024
D_FF = 2816
HEAD = 128
N_CHIPS = 4
N_DEV = 8
EPS = 1e-6
NEG_INF = -1e30
ROPE_THETA = 10000.0
ATT_SCALE = HEAD ** -0.5

ADAM_LR = 0.001
ADAM_B1 = 0.9
ADAM_B2 = 0.999
ADAM_EPS = 1e-08
ADAM_WD = 0.01
ADAM_STEP = 10

VMEM_LIMIT = 52 * 2 ** 20
MESH = pl.DeviceIdType.MESH

QKV_W = 3840
AQ, AK, AV, BQ, BK, BV, MQ = 0, 6, 12, 18, 22, 24, 26
DIL = ((128, 1), (512, 4), (2048, 16))

TM = 512
FF_T = D_FF // 2


def _params(*sem):
    return pltpu.CompilerParams(dimension_semantics=sem, vmem_limit_bytes=VMEM_LIMIT)


def _dot(a, b):
    return jnp.dot(a, b, preferred_element_type=F32)


def _dot_nt(a, b):
    return lax.dot_general(a, b, (((1,), (1,)), ((), ())), preferred_element_type=F32)


def _dot_tn(a, b):
    return lax.dot_general(a, b, (((0,), (0,)), ((), ())), preferred_element_type=F32)


def _rstd(x):
    return lax.rsqrt(jnp.mean(x * x, axis=-1, keepdims=True) + EPS)


def _sigmoid(x):
    return 0.5 * jnp.tanh(0.5 * x) + 0.5


def _ffn_perm(k):
    return (k % 2) * 2 + k // 2


UNREAD = pl.BlockSpec(memory_space=pl.ANY)


def _resident(arr):
    return pl.BlockSpec(arr.shape, lambda *_: (0,) * arr.ndim, pipeline_mode=pl.Buffered(1))


def ffn_in(h, g, w, name):
    T, D = h.shape

    def body(h_ref, g_ref, w_ref, xn_ref, gu_ref, a_ref):
        x = h_ref[...]
        xn = (x * _rstd(x) * g_ref[...]).astype(BF16)
        xn_ref[...] = xn
        for j in range(2):
            gu = _dot(xn, w_ref[:, j * 2 * FF_T:(j + 1) * 2 * FF_T])
            gu_ref[:, j * 2 * FF_T:(j + 1) * 2 * FF_T] = gu.astype(BF16)
            gate, up = gu[:, :FF_T], gu[:, FF_T:]
            a_ref[:, j * FF_T:(j + 1) * FF_T] = (gate * _sigmoid(gate) * up).astype(BF16)

    def rows(width):
        return pl.BlockSpec((TM, width), lambda i: (i, 0))

    return pl.pallas_call(
        body, name=name,
        grid=(T // TM,),
        in_specs=[rows(D), _resident(g), _resident(w)],
        out_specs=[rows(D), rows(2 * D_FF), rows(D_FF)],
        out_shape=[jax.ShapeDtypeStruct((T, D), BF16),
                   jax.ShapeDtypeStruct((T, 2 * D_FF), BF16),
                   jax.ShapeDtypeStruct((T, D_FF), BF16)],
        compiler_params=_params("parallel"),
    )(h, g, w)


def mm_norm_res(a, w, h_in, g, coef, name, target=None):
    T, K = a.shape
    D = w.shape[1]
    final = target is not None

    def body(*refs):
        if final:
            a_ref, w_ref, h_ref, g_ref, t_ref, f_ref, o_ref, l_ref = refs
        else:
            a_ref, w_ref, h_ref, g_ref, f_ref, o_ref = refs
        f = _dot(a_ref[...], w_ref[...])
        f_ref[...] = f
        y = h_ref[...] + coef * (f * _rstd(f) * g_ref[...])
        if final:
            err = y - t_ref[...]
            o_ref[...] = err * (1.0 / D)

            @pl.when(pl.program_id(0) == 0)
            def _():
                l_ref[...] = jnp.zeros_like(l_ref)

            l_ref[...] += jnp.sum(err * err)
        else:
            o_ref[...] = y

    row = pl.BlockSpec((TM, D), lambda i: (i, 0))
    in_specs = [pl.BlockSpec((TM, K), lambda i: (i, 0)),
                _resident(w),
                row, pl.BlockSpec((1, D), lambda i: (0, 0))]
    out_specs = [row, row]
    out_shape = [jax.ShapeDtypeStruct((T, D), F32), jax.ShapeDtypeStruct((T, D), F32)]
    args = [a, w, h_in, g]
    if final:
        in_specs.append(row)
        args.append(target)
        out_specs.append(pl.BlockSpec((8, 128), lambda i: (0, 0)))
        out_shape.append(jax.ShapeDtypeStruct((8, 128), F32))
    return pl.pallas_call(
        body, name=name, grid=(T // TM,), in_specs=in_specs, out_specs=out_specs, out_shape=out_shape,
        compiler_params=_params("arbitrary"),
    )(*args)


def _rope(x, cos, sin_signed):
    return x * cos + pltpu.roll(x, HEAD // 2, axis=1) * sin_signed


def _unrope(x, cos, sin_signed):
    return x * cos - pltpu.roll(x, HEAD // 2, axis=1) * sin_signed


ROTARY_HEADS = tuple(range(AQ, AV)) + tuple(range(BQ, BV))


def mix_in(h, g, w, w_gate, b_gate, cos, sin_signed, name):
    T, D = h.shape
    tn = 768

    def body(h_ref, g_ref, w_ref, wg_ref, b_ref, c_ref, s_ref, u_ref, o_ref, gt_ref):
        x = h_ref[...]
        u = (x * _rstd(x) * g_ref[...]).astype(BF16)
        u_ref[...] = u
        c, s = c_ref[...], s_ref[...]
        for j in range(QKV_W // tn):
            acc = _dot(u, w_ref[:, j * tn:(j + 1) * tn])
            for hd in range(tn // HEAD):
                head = j * (tn // HEAD) + hd
                part = acc[:, hd * HEAD:(hd + 1) * HEAD]
                if head in ROTARY_HEADS:
                    part = _rope(part, c, s)
                o_ref[:, head * HEAD:(head + 1) * HEAD] = part.astype(BF16)
        for j in range(w_gate.shape[1] // tn):
            cols = slice(j * tn, (j + 1) * tn)
            gt_ref[:, cols] = _sigmoid(_dot(u, wg_ref[:, cols]) + b_ref[:, cols]).astype(BF16)

    def rows(width):
        return pl.BlockSpec((TM, width), lambda i: (i, 0))

    return pl.pallas_call(
        body, name=name,
        grid=(T // TM,),
        in_specs=[rows(D), _resident(g), _resident(w), _resident(w_gate), _resident(b_gate), rows(HEAD), rows(HEAD)],
        out_specs=[rows(D), rows(QKV_W), rows(w_gate.shape[1])],
        out_shape=[jax.ShapeDtypeStruct((T, D), BF16), jax.ShapeDtypeStruct((T, QKV_W), BF16),
                   jax.ShapeDtypeStruct((T, w_gate.shape[1]), BF16)],
        compiler_params=_params("parallel"),
    )(h, g, w, w_gate, b_gate, cos, sin_signed)


def gate_merge(gt, o_a, o_b, o_m, w_a, w_b, w_m, name):
    T = gt.shape[0]
    D = D_MODEL

    def body(gt_ref, oa_ref, ob_ref, om_ref, wa_ref, wb_ref, wm_ref, out_ref):
        acc = gt_ref[:, :D].astype(F32) * _dot(oa_ref[...], wa_ref[...])
        acc += gt_ref[:, D:2 * D].astype(F32) * _dot(ob_ref[...], wb_ref[...])
        acc += gt_ref[:, 2 * D:].astype(F32) * _dot(om_ref[...], wm_ref[...])
        out_ref[...] = acc.astype(BF16)

    def rows(width):
        return pl.BlockSpec((TM, width), lambda i: (i, 0))

    def whole(arr):
        return pl.BlockSpec(arr.shape, lambda i: (0, 0))

    return pl.pallas_call(
        body, name=name, grid=(T // TM,),
        in_specs=[rows(3 * D), rows(o_a.shape[1]), rows(o_b.shape[1]), rows(o_m.shape[1]),
                  whole(w_a), whole(w_b), whole(w_m)],
        out_specs=rows(D),
        out_shape=jax.ShapeDtypeStruct((T, D), BF16),
        compiler_params=_params("parallel"),
    )(gt, o_a, o_b, o_m, w_a, w_b, w_m)


def _band_rows(start, r):
    return pl.ds(start, HEAD) if r == 1 else pl.ds(start, HEAD, stride=r)


def _band_mask(max_dist, first_has_prev):
    row = lax.broadcasted_iota(jnp.int32, (HEAD, 2 * HEAD), 0)
    col = lax.broadcasted_iota(jnp.int32, (HEAD, 2 * HEAD), 1)
    dist = row + HEAD - col
    band = (dist >= 0) & (dist <= max_dist)
    return band, band & (col >= jnp.where(first_has_prev, 0, HEAD))


def band_fwd(qkv, sinks, *, r, q_off, k_off, v_off, hkv, grp, max_dist, out_dtype, name):
    T, W = qkv.shape
    SB = HEAD * r
    BT = min(2048, T)
    nsub, nib = BT // SB, T // BT
    hq = hkv * grp

    def body(sink_ref, q_ref, kc_ref, kp_ref, vc_ref, vp_ref, o_ref, l_ref, qf, kf, vf):
        kvh, ib = pl.program_id(0), pl.program_id(1)
        qf[...] = q_ref[...].astype(F32)
        kf[:SB] = kp_ref[...].astype(F32)
        kf[SB:] = kc_ref[...].astype(F32)
        vf[:SB] = vp_ref[...].astype(F32)
        vf[SB:] = vc_ref[...].astype(F32)
        band, band_first = _band_mask(max_dist, ib > 0)
        for c in range(r):
            k_old, v_old = kf[_band_rows(c, r)], vf[_band_rows(c, r)]
            for j in range(nsub):
                mask = band_first if j == 0 else band
                rows = _band_rows(j * SB + c, r)
                k_own, v_own = kf[_band_rows((j + 1) * SB + c, r)], vf[_band_rows((j + 1) * SB + c, r)]
                kcat = jnp.concatenate([k_old, k_own], axis=0).astype(BF16)
                vcat = jnp.concatenate([v_old, v_own], axis=0).astype(BF16)
                k_old, v_old = k_own, v_own
                for gq in range(grp):
                    cols = slice(gq * HEAD, (gq + 1) * HEAD)
                    s = jnp.where(mask, _dot_nt(qf[rows, cols].astype(BF16), kcat) * ATT_SCALE, NEG_INF)
                    sk = sink_ref[kvh * grp + gq]
                    m = jnp.maximum(jnp.max(s, axis=-1, keepdims=True), sk)
                    p = jnp.exp(s - m)
                    tot = jnp.sum(p, axis=-1, keepdims=True) + jnp.exp(sk - m)
                    o_ref[rows, cols] = (_dot(p.astype(BF16), vcat) / tot).astype(out_dtype)
                    l_ref[rows, cols] = jnp.broadcast_to(m + jnp.log(tot), (HEAD, HEAD))

    def cur(off, width):
        return pl.BlockSpec((BT, width * HEAD), lambda h, i: (i, off // width + h))

    def prev(off):
        return pl.BlockSpec((SB, HEAD), lambda h, i: (jnp.maximum(i * nsub - 1, 0), off + h))

    out_spec = pl.BlockSpec((BT, grp * HEAD), lambda h, i: (i, h))
    return pl.pallas_call(
        body, name=name, grid=(hkv, nib),
        in_specs=[pl.BlockSpec(memory_space=pltpu.SMEM),
                  cur(q_off, grp), cur(k_off, 1), prev(k_off), cur(v_off, 1), prev(v_off)],
        out_specs=[out_spec, out_spec],
        out_shape=[jax.ShapeDtypeStruct((T, hq * HEAD), out_dtype), jax.ShapeDtypeStruct((T, hq * HEAD), F32)],
        scratch_shapes=[pltpu.VMEM((BT, grp * HEAD), F32), pltpu.VMEM((SB + BT, HEAD), F32),
                        pltpu.VMEM((SB + BT, HEAD), F32)],
        compiler_params=_params("parallel", "arbitrary"),
    )(sinks, qkv, qkv, qkv, qkv, qkv)


def band_bwd(qkv, do, o, lse, cos, sin_signed, sinks, *, r, q_off, k_off, v_off, hkv, grp, max_dist, name):
    T, W = qkv.shape
    SB = HEAD * r
    BT = min(2048, T)
    nsub, nib = BT // SB, T // BT
    nblk = T // SB
    hq = hkv * grp
    with_sink = sinks is not None

    def body(*refs):
        if with_sink:
            sink_ref, refs = refs[0], refs[1:]
        (q_ref, qn_ref, kc_ref, kp_ref, vc_ref, vp_ref, do_ref, don_ref, o_ref, on_ref, l_ref, ln_ref,
         c_ref, s_ref) = refs[:14]
        dq_ref, dk_ref, dv_ref = refs[14:17]
        ds_ref = refs[17] if with_sink else None
        qf, dof, of, kf, vf, dqf, dkf, dvf = refs[-8:]
        kvh, ib = pl.program_id(0), pl.program_id(1)
        for buf, cur_ref, nxt_ref in ((qf, q_ref, qn_ref), (dof, do_ref, don_ref), (of, o_ref, on_ref)):
            buf[:BT] = cur_ref[...].astype(F32)
            buf[BT:] = nxt_ref[...].astype(F32)
        kf[:SB] = kp_ref[...].astype(F32)
        kf[SB:] = kc_ref[...].astype(F32)
        vf[:SB] = vp_ref[...].astype(F32)
        vf[SB:] = vc_ref[...].astype(F32)
        band, band_first = _band_mask(max_dist, ib > 0)
        if with_sink:
            @pl.when(ib == 0)
            def _():
                ds_ref[...] = jnp.zeros_like(ds_ref)

        def grads(rows, cols, logz, keys, vals, mask):
            q, dout = qf[rows, cols].astype(BF16), dof[rows, cols].astype(BF16)
            delta = jnp.sum(dof[rows, cols] * of[rows, cols], axis=-1, keepdims=True)
            s = jnp.where(mask, _dot_nt(q, keys) * ATT_SCALE, NEG_INF)
            p = jnp.exp(s - logz[:, :1])
            ds = (p * (_dot_nt(dout, vals) - delta) * ATT_SCALE).astype(BF16)
            return q, dout, p.astype(BF16), ds, delta

        row = lax.broadcasted_iota(jnp.int32, (HEAD, HEAD), 0)
        col = lax.broadcasted_iota(jnp.int32, (HEAD, HEAD), 1)
        reach = col >= row + jnp.where(ib < nib - 1, HEAD - max_dist, 2 * HEAD)
        for c in range(r):
            k_old, v_old = kf[_band_rows(c, r)], vf[_band_rows(c, r)]
            dk_own = dv_own = None
            for j in range(nsub):
                rows = _band_rows(j * SB + c, r)
                k_own, v_own = kf[_band_rows((j + 1) * SB + c, r)], vf[_band_rows((j + 1) * SB + c, r)]
                kcat = jnp.concatenate([k_old, k_own], axis=0).astype(BF16)
                vcat = jnp.concatenate([v_old, v_own], axis=0).astype(BF16)
                dk = dv = None
                for gq in range(grp):
                    cols = slice(gq * HEAD, (gq + 1) * HEAD)
                    logz = l_ref[rows, cols]
                    q, dout, p, ds, delta = grads(rows, cols, logz, kcat, vcat, band_first if j == 0 else band)
                    dqf[rows, cols] = _dot(ds, kcat)
                    dk = _dot_tn(ds, q) if dk is None else dk + _dot_tn(ds, q)
                    dv = _dot_tn(p, dout) if dv is None else dv + _dot_tn(p, dout)
                    if with_sink:
                        p_sink = jnp.exp(sink_ref[kvh * grp + gq] - logz[:, :1])
                        ds_ref[gq * 8:(gq + 1) * 8] += jnp.sum(p_sink * delta)
                if j > 0:
                    done = _band_rows((j - 1) * SB + c, r)
                    dkf[done] = dk_own + dk[:HEAD]
                    dvf[done] = dv_own + dv[:HEAD]
                dk_own, dv_own = dk[HEAD:], dv[HEAD:]
                k_old, v_old = k_own, v_own
            nxt = _band_rows(BT + c, r)
            keys, vals = k_old.astype(BF16), v_old.astype(BF16)
            for gq in range(grp):
                cols = slice(gq * HEAD, (gq + 1) * HEAD)
                q, dout, p, ds, _ = grads(nxt, cols, ln_ref[_band_rows(c, r), cols], keys, vals, reach)
                dk_own += _dot_tn(ds, q)
                dv_own += _dot_tn(p, dout)
            done = _band_rows((nsub - 1) * SB + c, r)
            dkf[done] = dk_own
            dvf[done] = dv_own

        cs, sn = c_ref[...], s_ref[...]
        for gq in range(grp):
            cols = slice(gq * HEAD, (gq + 1) * HEAD)
            dq_ref[:, cols] = _unrope(dqf[:, cols], cs, sn).astype(BF16)
        dk_ref[...] = _unrope(dkf[...], cs, sn).astype(BF16)
        dv_ref[...] = dvf[...].astype(BF16)

    def cur(off, width):
        return pl.BlockSpec((BT, width * HEAD), lambda h, i: (i, off // width + h))

    def prev(off):
        return pl.BlockSpec((SB, HEAD), lambda h, i: (jnp.maximum(i * nsub - 1, 0), off + h))

    def nxt_row(i):
        return jnp.minimum((i + 1) * nsub, nblk - 1)

    q_next = pl.BlockSpec((SB, grp * HEAD), lambda h, i: (nxt_row(i), q_off // grp + h))
    head_cur = pl.BlockSpec((BT, grp * HEAD), lambda h, i: (i, h))
    head_next = pl.BlockSpec((SB, grp * HEAD), lambda h, i: (nxt_row(i), h))
    table = pl.BlockSpec((BT, HEAD), lambda h, i: (i, 0))
    kv_out = pl.BlockSpec((BT, HEAD), lambda h, i: (i, h))

    in_specs = [cur(q_off, grp), q_next, cur(k_off, 1), prev(k_off), cur(v_off, 1), prev(v_off),
                head_cur, head_next, head_cur, head_next, head_cur, head_next, table, table]
    args = [qkv, qkv, qkv, qkv, qkv, qkv, do, do, o, o, lse, lse, cos, sin_signed]
    out_specs = [head_cur, kv_out, kv_out]
    out_shape = [jax.ShapeDtypeStruct((T, hq * HEAD), BF16), jax.ShapeDtypeStruct((T, hkv * HEAD), BF16),
                 jax.ShapeDtypeStruct((T, hkv * HEAD), BF16)]
    if with_sink:
        in_specs.insert(0, pl.BlockSpec(memory_space=pltpu.SMEM))
        args.insert(0, sinks)
        out_specs.append(pl.BlockSpec((None, grp * 8, HEAD), lambda h, i: (h, 0, 0)))
        out_shape.append(jax.ShapeDtypeStruct((hkv, grp * 8, HEAD), F32))
    wide = pltpu.VMEM((BT + SB, grp * HEAD), F32)
    tall = pltpu.VMEM((SB + BT, HEAD), F32)
    grad = pltpu.VMEM((BT, HEAD), F32)
    return pl.pallas_call(
        body, name=name, grid=(hkv, nib), in_specs=in_specs, out_specs=out_specs, out_shape=out_shape,
        scratch_shapes=[wide, wide, wide, tall, tall, pltpu.VMEM((BT, grp * HEAD), F32), grad, grad],
        compiler_params=_params("parallel", "arbitrary"),
    )(*args)


def merge_groups(outs, lses, name):
    T, Wd = outs[0].shape
    tm = 1024

    def body(o0, o1, o2, l0, l1, l2, out_ref, lt_ref):
        a, b, c = l0[...], l1[...], l2[...]
        m = jnp.maximum(jnp.maximum(a, b), c)
        wa, wb, wc = jnp.exp(a - m), jnp.exp(b - m), jnp.exp(c - m)
        z = wa + wb + wc
        out_ref[...] = ((wa * o0[...] + wb * o1[...] + wc * o2[...]) / z).astype(BF16)
        lt_ref[...] = m + jnp.log(z)

    spec = pl.BlockSpec((tm, Wd), lambda i: (i, 0))
    return pl.pallas_call(
        body, name=name, grid=(T // tm,), in_specs=[spec] * 6, out_specs=[spec, spec],
        out_shape=[jax.ShapeDtypeStruct((T, Wd), BF16), jax.ShapeDtypeStruct((T, Wd), F32)],
        compiler_params=_params("parallel"),
    )(*outs, *lses)


M_HEADS = 4


def mem_kv(mem, g, w, name):
    n, D = mem.shape

    def body(m_ref, g_ref, w_ref, mn_ref, kv_ref):
        x = m_ref[...]
        mn = (x * _rstd(x) * g_ref[...]).astype(BF16)
        mn_ref[...] = mn
        kv_ref[...] = _dot(mn, w_ref[...]).astype(BF16)

    return pl.pallas_call(
        body, name=name,
        out_shape=[jax.ShapeDtypeStruct((n, D), BF16), jax.ShapeDtypeStruct((n, w.shape[1]), BF16)],
        compiler_params=pltpu.CompilerParams(vmem_limit_bytes=VMEM_LIMIT),
    )(mem, g, w)


def mem_fwd(qkv, mkv, name):
    T = qkv.shape[0]
    n = mkv.shape[0]
    RB = 1024

    def body(q_ref, k_ref, v_ref, o_ref, l_ref):
        s = _dot_nt(q_ref[...], k_ref[...]) * ATT_SCALE
        m = jnp.max(s, axis=-1, keepdims=True)
        p = jnp.exp(s - m)
        den = jnp.sum(p, axis=-1, keepdims=True)
        o_ref[...] = (_dot(p.astype(BF16), v_ref[...]) / den).astype(BF16)
        l_ref[...] = jnp.broadcast_to(m + jnp.log(den), (RB, HEAD))

    out = pl.BlockSpec((RB, HEAD), lambda h, i: (i, h))
    return pl.pallas_call(
        body, name=name, grid=(M_HEADS, T // RB),
        in_specs=[pl.BlockSpec((RB, HEAD), lambda h, i: (i, MQ + h)),
                  pl.BlockSpec((n, HEAD), lambda h, i: (0, h)),
                  pl.BlockSpec((n, HEAD), lambda h, i: (0, M_HEADS + h))],
        out_specs=[out, out],
        out_shape=[jax.ShapeDtypeStruct((T, M_HEADS * HEAD), BF16), jax.ShapeDtypeStruct((T, M_HEADS * HEAD), F32)],
        compiler_params=_params("parallel", "parallel"),
    )(qkv, mkv, mkv)


def mem_bwd(qkv, mkv, do, o, lse, name):
    T = qkv.shape[0]
    n = mkv.shape[0]
    RB = 1024

    def body(q_ref, k_ref, v_ref, do_ref, o_ref, l_ref, dq_ref, dk_ref, dv_ref):
        @pl.when(pl.program_id(1) == 0)
        def _():
            dk_ref[...] = jnp.zeros_like(dk_ref)
            dv_ref[...] = jnp.zeros_like(dv_ref)

        q, dout = q_ref[...], do_ref[...]
        delta = jnp.sum(dout.astype(F32) * o_ref[...].astype(F32), axis=-1, keepdims=True)
        p = jnp.exp(_dot_nt(q, k_ref[...]) * ATT_SCALE - l_ref[...][:, :1])
        ds = (p * (_dot_nt(dout, v_ref[...]) - delta) * ATT_SCALE).astype(BF16)
        dq_ref[...] = _dot(ds, k_ref[...]).astype(BF16)
        dk_ref[...] += _dot_tn(ds, q)
        dv_ref[...] += _dot_tn(p.astype(BF16), dout)

    tok = pl.BlockSpec((RB, HEAD), lambda h, i: (i, h))
    slot = pl.BlockSpec((n, HEAD), lambda h, i: (0, h))
    return pl.pallas_call(
        body, name=name, grid=(M_HEADS, T // RB),
        in_specs=[pl.BlockSpec((RB, HEAD), lambda h, i: (i, MQ + h)),
                  slot, pl.BlockSpec((n, HEAD), lambda h, i: (0, M_HEADS + h)), tok, tok, tok],
        out_specs=[tok, slot, slot],
        out_shape=[jax.ShapeDtypeStruct((T, M_HEADS * HEAD), BF16),
                   jax.ShapeDtypeStruct((n, M_HEADS * HEAD), F32),
                   jax.ShapeDtypeStruct((n, M_HEADS * HEAD), F32)],
        compiler_params=_params("parallel", "arbitrary"),
    )(qkv, mkv, mkv, do, o, lse)


def mem_kv_bwd(mem, g, mem_n, w, dmkv, name):
    n, D = mem.shape

    def body(m_ref, g_ref, mn_ref, w_ref, d_ref, dw_ref, dg_ref):
        d = d_ref[...].astype(BF16)
        dw_ref[...] = _dot_tn(mn_ref[...], d)
        x = m_ref[...]
        dg_ref[...] = jnp.sum(_dot_nt(d, w_ref[...]) * (x * _rstd(x)), axis=0, keepdims=True)

    return pl.pallas_call(
        body, name=name,
        out_shape=[jax.ShapeDtypeStruct(w.shape, F32), jax.ShapeDtypeStruct((1, D), F32)],
        compiler_params=pltpu.CompilerParams(vmem_limit_bytes=VMEM_LIMIT),
    )(mem, g, mem_n, w, dmkv)


def _rms_bwd(dn, f, g):
    r = _rstd(f)
    fhat = f * r
    dfhat = dn * g
    df = r * (dfhat - fhat * jnp.mean(dfhat * fhat, axis=-1, keepdims=True))
    return df, jnp.sum(dn * fhat, axis=0, keepdims=True)


TM_FFN_BWD = 256


def ffn_tokens_bwd(dh, f, h_in, gu, g_pre, g_post, w_in, w_out, coef, name, after):
    T, D = dh.shape
    tm = TM_FFN_BWD

    def body(dh_ref, f_ref, h_ref, gu_ref, gpre_ref, gpost_ref, win_ref, wout_ref, _,
             df_ref, dgu_ref, dhin_ref, dgpre_ref, dgpost_ref):
        @pl.when(pl.program_id(0) == 0)
        def _():
            dgpre_ref[...] = jnp.zeros_like(dgpre_ref)
            dgpost_ref[...] = jnp.zeros_like(dgpost_ref)

        dh = dh_ref[...]
        df, dg_post = _rms_bwd(coef * dh, f_ref[...], gpost_ref[...])
        dgpost_ref[...] += dg_post
        df = df.astype(BF16)
        df_ref[...] = df
        dxn = jnp.zeros((tm, D), F32)
        for j in range(2):
            lo, mid, hi = 2 * j * FF_T, (2 * j + 1) * FF_T, (2 * j + 2) * FF_T
            da = _dot_nt(df, wout_ref[j * FF_T:(j + 1) * FF_T, :])
            gate = gu_ref[:, lo:mid].astype(F32)
            up = gu_ref[:, mid:hi].astype(F32)
            sig = _sigmoid(gate)
            dgate = (da * up * sig * (1.0 + gate * (1.0 - sig))).astype(BF16)
            dup = (da * gate * sig).astype(BF16)
            dgu_ref[:, lo:mid] = dgate
            dgu_ref[:, mid:hi] = dup
            dxn += _dot_nt(dgate, win_ref[:, lo:mid]) + _dot_nt(dup, win_ref[:, mid:hi])
        h = h_ref[...]
        r = _rstd(h)
        xhat = h * r
        dxhat = dxn * gpre_ref[...]
        dhin_ref[...] = dh + r * (dxhat - xhat * jnp.mean(dxhat * xhat, axis=-1, keepdims=True))
        dgpre_ref[...] += jnp.sum(dxn * xhat, axis=0, keepdims=True)

    def rows(width):
        return pl.BlockSpec((tm, width), lambda i: (i, 0))

    vec = pl.BlockSpec((1, D), lambda i: (0, 0))
    return pl.pallas_call(
        body, name=name, grid=(T // tm,),
        in_specs=[rows(D), rows(D), rows(D), rows(2 * D_FF), _resident(g_pre), _resident(g_post),
                  _resident(w_in), _resident(w_out), UNREAD],
        out_specs=[rows(D), rows(2 * D_FF), rows(D), vec, vec],
        out_shape=[jax.ShapeDtypeStruct((T, D), BF16), jax.ShapeDtypeStruct((T, 2 * D_FF), BF16),
                   jax.ShapeDtypeStruct((T, D), F32), jax.ShapeDtypeStruct((1, D), F32),
                   jax.ShapeDtypeStruct((1, D), F32)],
        compiler_params=_params("arbitrary"),
    )(dh, f, h_in, gu, g_pre, g_post, w_in, w_out, after)


def mix_out_bwd(dh, f, g, w_out, name, after):
    T, D = dh.shape

    def body(dh_ref, f_ref, g_ref, w_ref, _, df_ref, dm_ref, dg_ref):
        df, dg = _rms_bwd(dh_ref[...], f_ref[...], g_ref[...])
        df = df.astype(BF16)
        df_ref[...] = df

        @pl.when(pl.program_id(0) == 0)
        def _():
            dg_ref[...] = jnp.zeros_like(dg_ref)

        dg_ref[...] += dg
        dm_ref[...] = _dot_nt(df, w_ref[...]).astype(BF16)

    row = pl.BlockSpec((TM, D), lambda i: (i, 0))
    vec = pl.BlockSpec((1, D), lambda i: (0, 0))
    return pl.pallas_call(
        body, name=name, grid=(T // TM,),
        in_specs=[row, row, vec, _resident(w_out), UNREAD],
        out_specs=[row, row, vec],
        out_shape=[jax.ShapeDtypeStruct((T, D), BF16), jax.ShapeDtypeStruct((T, D), BF16),
                   jax.ShapeDtypeStruct((1, D), F32)],
        compiler_params=_params("arbitrary"),
    )(dh, f, g, w_out, after)


def mm_nt_norm_bwd(pieces, h_in, dh_out, g, name, after):
    T, D = h_in.shape

    def body(*refs):
        ab = refs[:2 * len(pieces)]
        h_ref, dh_ref, g_ref, _, o_ref, dg_ref = refs[2 * len(pieces):]
        dxn = _dot_nt(ab[0][...], ab[1][...])
        for p in range(1, len(pieces)):
            dxn += _dot_nt(ab[2 * p][...], ab[2 * p + 1][...])
        h = h_ref[...]
        r = _rstd(h)
        xhat = h * r
        dxhat = dxn * g_ref[...]
        o_ref[...] = dh_ref[...] + r * (dxhat - xhat * jnp.mean(dxhat * xhat, axis=-1, keepdims=True))

        @pl.when(pl.program_id(0) == 0)
        def _():
            dg_ref[...] = jnp.zeros_like(dg_ref)

        dg_ref[...] += jnp.sum(dxn * xhat, axis=0, keepdims=True)

    in_specs, args = [], []
    for a, w in pieces:
        in_specs += [pl.BlockSpec((TM, a.shape[1]), lambda i: (i, 0)), _resident(w)]
        args += [a, w]
    row = pl.BlockSpec((TM, D), lambda i: (i, 0))
    return pl.pallas_call(
        body, name=name, grid=(T // TM,),
        in_specs=in_specs + [row, row, _resident(g), UNREAD],
        out_specs=[row, pl.BlockSpec((1, D), lambda i: (0, 0))],
        out_shape=[jax.ShapeDtypeStruct((T, D), F32), jax.ShapeDtypeStruct((1, D), F32)],
        compiler_params=_params("arbitrary"),
    )(*args, h_in, dh_out, g, after)


def gate_merge_bwd(dm, gt, o_a, o_b, o_m, w_a, w_b, w_m, name):
    T = dm.shape[0]
    D = D_MODEL
    branch = ((o_a, w_a), (o_b, w_b), (o_m, w_m))

    def body(dm_ref, gt_ref, oa_ref, ob_ref, om_ref, wa_ref, wb_ref, wm_ref,
             dgt_ref, dpa_ref, dpb_ref, dpm_ref, doa_ref, dob_ref, dom_ref, db_ref):
        @pl.when(pl.program_id(0) == 0)
        def _():
            db_ref[...] = jnp.zeros_like(db_ref)

        dmf = dm_ref[...].astype(F32)
        for x, (o_ref, w_ref, dp_ref, do_ref) in enumerate(((oa_ref, wa_ref, dpa_ref, doa_ref),
                                                           (ob_ref, wb_ref, dpb_ref, dob_ref),
                                                           (om_ref, wm_ref, dpm_ref, dom_ref))):
            cols = slice(x * D, (x + 1) * D)
            gx = gt_ref[:, cols].astype(F32)
            w = w_ref[...]
            dpre = dmf * _dot(o_ref[...], w) * gx * (1.0 - gx)
            dgt_ref[:, cols] = dpre.astype(BF16)
            db_ref[:, cols] += jnp.sum(dpre, axis=0, keepdims=True)
            dp = (dmf * gx).astype(BF16)
            dp_ref[...] = dp
            do_ref[...] = _dot_nt(dp, w).astype(BF16)

    def rows(width):
        return pl.BlockSpec((TM, width), lambda i: (i, 0))

    def whole(arr):
        return pl.BlockSpec(arr.shape, lambda i: (0, 0))

    widths = [o.shape[1] for o, _ in branch]
    return pl.pallas_call(
        body, name=name, grid=(T // TM,),
        in_specs=[rows(D), rows(3 * D)] + [rows(k) for k in widths] + [whole(w) for _, w in branch],
        out_specs=[rows(3 * D), rows(D), rows(D), rows(D)] + [rows(k) for k in widths]
                  + [pl.BlockSpec((1, 3 * D), lambda i: (0, 0))],
        out_shape=[jax.ShapeDtypeStruct((T, 3 * D), BF16)] + [jax.ShapeDtypeStruct((T, D), BF16)] * 3
                  + [jax.ShapeDtypeStruct((T, k), BF16) for k in widths]
                  + [jax.ShapeDtypeStruct((1, 3 * D), F32)],
        compiler_params=_params("arbitrary"),
    )(dm, gt, o_a, o_b, o_m, w_a, w_b, w_m)


def mm_tn(x, dy, tm, tn, name, shard_major=False, perm=None, slabs=1, after=None):
    T, M = x.shape
    N = dy.shape[1]
    tk = min(1024, T)
    perm = perm or (lambda j: j)
    w = tn // slabs

    def body(x_ref, dy_ref, *rest):
        o_ref = rest[-1]

        @pl.when(pl.program_id(2) == 0)
        def _():
            o_ref[...] = jnp.zeros_like(o_ref)

        acc = _dot_tn(x_ref[...], dy_ref[...])
        if shard_major:
            for s in range(slabs):
                o_ref[s] += acc[:, s * w:(s + 1) * w]
        else:
            o_ref[...] += acc

    if shard_major:
        out_spec = pl.BlockSpec((slabs, tm, w), lambda i, j, k: (perm(j), i, 0))
        out_shape = jax.ShapeDtypeStruct((N // w, M, w), F32)
    else:
        out_spec = pl.BlockSpec((tm, tn), lambda i, j, k: (i, j))
        out_shape = jax.ShapeDtypeStruct((M, N), F32)
    return pl.pallas_call(
        body, name=name, grid=(M // tm, N // tn, T // tk),
        in_specs=[pl.BlockSpec((tk, tm), lambda i, j, k: (k, i)),
                  pl.BlockSpec((tk, tn), lambda i, j, k: (k, j))] + ([] if after is None else [UNREAD]),
        out_specs=out_spec, out_shape=out_shape,
        compiler_params=_params("parallel", "parallel", "arbitrary"),
    )(x, dy, *([] if after is None else [after]))


def rope_tables(T, zero):
    half = HEAD // 2
    inv = ROPE_THETA ** (-jnp.arange(half, dtype=F32) / half)
    ang = (jnp.arange(T).astype(F32) + zero)[:, None] * inv[None, :]
    cos, sin = jnp.cos(ang), jnp.sin(ang)
    return jnp.concatenate([cos, cos], axis=1), jnp.concatenate([-sin, sin], axis=1)


def layer_step(x, mem, target, gains, sinks, b_gate, weights_of, send_grads, zero):
    T = x.shape[0]
    cos, sin_signed = rope_tables(T, zero)
    no_sink = jnp.full((2,), NEG_INF, F32)

    w = dict(weights_of("ffn1_in", cos))
    xn1, gu1, a1 = ffn_in(x, gains["ffn1_norm_pre"], w["ffn1_w_in"], "ffn1_in")
    w.update(weights_of("ffn1_out", xn1))
    f1, h1 = mm_norm_res(a1, w["ffn1_w_out"], x, gains["ffn1_norm_post"], 0.5, "ffn1_out")
    w.update(weights_of("mix", f1))
    u, qkv, gt = mix_in(h1, gains["mix_norm_pre"], w["w_in"], w["w_gate"], b_gate, cos, sin_signed, "mix_in")
    outs, lses = [], []
    for gidx, (window, dil) in enumerate(DIL):
        o_g, l_g = band_fwd(qkv, no_sink, r=dil, q_off=AQ + 2 * gidx, k_off=AK + 2 * gidx, v_off=AV + 2 * gidx,
                            hkv=2, grp=1, max_dist=window // dil, out_dtype=F32, name=f"attn_a{gidx}_fwd")
        outs.append(o_g)
        lses.append(l_g)
    o_a, l_a = merge_groups(outs, lses, "attn_a_merge")
    o_b, l_b = band_fwd(qkv, sinks, r=1, q_off=BQ, k_off=BK, v_off=BV, hkv=2, grp=2, max_dist=HEAD - 1,
                        out_dtype=BF16, name="attn_b_fwd")
    mem_n, mkv = mem_kv(mem, gains["mem_norm"], w["w_mem_kv"], "mem_kv")
    o_m, l_m = mem_fwd(qkv, mkv, "attn_m_fwd")
    merged = gate_merge(gt, o_a, o_b, o_m, w["w_o_a"], w["w_o_b"], w["w_o_m"], "gate_merge")
    mo, h2 = mm_norm_res(merged, w["w_out"], h1, gains["mix_norm_post"], 1.0, "mix_out")
    w.update(weights_of("ffn2", mo))
    xn2, gu2, a2 = ffn_in(h2, gains["ffn2_norm_pre"], w["ffn2_w_in"], "ffn2_in")
    f2, dy, sq = mm_norm_res(a2, w["ffn2_w_out"], h2, gains["ffn2_norm_post"], 0.5, "ffn2_out", target=target)

    grads = {}

    def ffn_bwd(tag, dh_out, f, gu, a, xn, h_in, after):
        df, dgu, dh_in, grads[f"{tag}_norm_pre"], grads[f"{tag}_norm_post"] = ffn_tokens_bwd(
            dh_out, f, h_in, gu, gains[f"{tag}_norm_pre"], gains[f"{tag}_norm_post"], w[f"{tag}_w_in"],
            w[f"{tag}_w_out"], 0.5, f"{tag}_tokens_bwd", after)
        sent = send_grads(f"{tag}_out", {f"{tag}_w_out": mm_tn(a, df, FF_T, D_MODEL, f"{tag}_w_out_grad")})
        sent = send_grads(f"{tag}_in", {f"{tag}_w_in": mm_tn(
            xn, dgu, D_MODEL, FF_T, f"{tag}_w_in_grad", shard_major=True, perm=_ffn_perm, after=sent)})
        return dh_in, sent

    dh2, sent = ffn_bwd("ffn2", dy, f2, gu2, a2, xn2, h2, dy)

    mix = {}
    dmo, dmerged, grads["mix_norm_post"] = mix_out_bwd(
        dh2, mo, gains["mix_norm_post"], w["w_out"], "mix_out_bwd", sent)
    mix["w_out"] = mm_tn(merged, dmo, D_MODEL, D_MODEL, "w_out_grad")
    dgt, dpa, dpb, dpm, do_a, do_b, do_m, grads["b_gate"] = gate_merge_bwd(
        dmerged, gt, o_a, o_b, o_m, w["w_o_a"], w["w_o_b"], w["w_o_m"], "gate_merge_bwd")
    mix["w_o_a"] = mm_tn(o_a, dpa, o_a.shape[1], D_MODEL, "w_o_a_grad")
    mix["w_o_b"] = mm_tn(o_b, dpb, o_b.shape[1], D_MODEL, "w_o_b_grad")
    mix["w_o_m"] = mm_tn(o_m, dpm, o_m.shape[1], D_MODEL, "w_o_m_grad")

    dq_a, dk_a, dv_a = [], [], []
    for gidx, (window, dil) in enumerate(DIL):
        dq, dk, dv = band_bwd(qkv, do_a, o_a, l_a, cos, sin_signed, None, r=dil, q_off=AQ + 2 * gidx,
                              k_off=AK + 2 * gidx, v_off=AV + 2 * gidx, hkv=2, grp=1, max_dist=window // dil,
                              name=f"attn_a{gidx}_bwd")
        dq_a.append(dq)
        dk_a.append(dk)
        dv_a.append(dv)
    dq_b, dk_b, dv_b, dsink = band_bwd(qkv, do_b, o_b, l_b, cos, sin_signed, sinks, r=1, q_off=BQ, k_off=BK,
                                       v_off=BV, hkv=2, grp=2, max_dist=HEAD - 1, name="attn_b_bwd")
    grads["sinks"] = -dsink[:, ::8, 0].reshape(1, 4)
    dq_m, dmk, dmv = mem_bwd(qkv, mkv, do_m, o_m, l_m, "attn_m_bwd")
    mix["w_mem_kv"], grads["mem_norm"] = mem_kv_bwd(
        mem, gains["mem_norm"], mem_n, w["w_mem_kv"], jnp.concatenate([dmk, dmv], axis=1), "mem_kv_bwd")
    dqkv = jnp.concatenate(dq_a + dk_a + dv_a + [dq_b, dk_b, dv_b, dq_m], axis=1)

    mix["w_in"] = mm_tn(u, dqkv, D_MODEL, 1280, "w_in_grad")
    mix["w_gate"] = mm_tn(u, dgt, D_MODEL, 1536, "w_gate_grad", shard_major=True, slabs=2)
    sent = send_grads("mix", mix)
    dh1, grads["mix_norm_pre"] = mm_nt_norm_bwd(
        [(dqkv, w["w_in"]), (dgt, w["w_gate"])], h1, dh2, gains["mix_norm_pre"], "mix_in_bwd", sent)

    dx, _ = ffn_bwd("ffn1", dh1, f1, gu1, a1, xn1, x, dh1)
    return sq, dx, grads


def _place():
    return lax.axis_index("x"), lax.axis_index("y"), lax.axis_index("c")


def _other_chips(x, y):
    return [(1 - x, y), (x, 1 - y), (1 - x, 1 - y)]


def _hbm(n):
    return [pl.BlockSpec(memory_space=pltpu.HBM)] * n


SEM = pl.BlockSpec(memory_space=pltpu.SEMAPHORE)
SIDE_EFFECT = pltpu.SideEffectType.DATAFLOW_SIDE_EFFECTING


def _chip_copy(src, land, sems, i, j, dst_slot, scatter):
    x, y, c = _place()
    px, py = _other_chips(x, y)[j]
    send_sems, recv_sems = sems
    return pltpu.make_async_remote_copy(
        src_ref=src[i].at[2 * px + py] if scatter else src[i], dst_ref=land[i].at[dst_slot],
        send_sem=send_sems.at[3 * i + j], recv_sem=recv_sems.at[3 * i + j],
        device_id=(px, py, c), device_id_type=MESH)


def chip_copies_start(srcs, lands, groups, scatter, name):
    n = len(srcs)

    def body(*refs):
        src, land = refs[:n], refs[n:2 * n]
        sems = refs[2 * n:2 * n + 2 * len(groups)]
        token = refs[-1]
        x, y, _ = _place()
        for g, members in enumerate(groups):
            part = ([src[i] for i in members], [land[i] for i in members])
            for t in range(len(members)):
                for j in range(3):
                    _chip_copy(*part, sems[2 * g:2 * g + 2], t, j, 2 * x + y, scatter).start()
        token[...] = jnp.zeros_like(token)

    sem_shapes = [pltpu.SemaphoreType.DMA((3 * len(m),)) for m in groups for _ in range(2)]
    thru = [pltpu.HBM(a.shape, a.dtype) for a in (*srcs, *lands)]
    res = pl.pallas_call(
        body, name=name,
        out_shape=(*sem_shapes, *thru, jax.ShapeDtypeStruct((8, 128), F32)),
        in_specs=_hbm(2 * n),
        out_specs=(*[SEM] * len(sem_shapes), *_hbm(2 * n), pl.BlockSpec(memory_space=pltpu.VMEM)),
        input_output_aliases={i: len(sem_shapes) + i for i in range(2 * n)},
        compiler_params=pltpu.CompilerParams(has_side_effects=SIDE_EFFECT),
    )(*[pltpu.with_memory_space_constraint(a, pltpu.HBM) for a in (*srcs, *lands)])
    k = len(sem_shapes)
    sems = [tuple(res[2 * g:2 * g + 2]) for g in range(len(groups))]
    return sems, list(res[k:k + n]), list(res[k + n:k + 2 * n]), res[-1]


def chip_copies_wait(srcs, lands, sems, after, scatter, name):
    n = len(srcs)

    def body(*refs):
        src, land = refs[:n], refs[n:2 * n]
        pair = refs[2 * n:2 * n + 2]
        x, y, _ = _place()
        for i in range(n):
            for j, (px, py) in enumerate(_other_chips(x, y)):
                copy = _chip_copy(src, land, pair, i, j, 2 * px + py, scatter)
                copy.wait_send()
                copy.wait_recv()

    res = pl.pallas_call(
        body, name=name,
        out_shape=[pltpu.HBM(a.shape, a.dtype) for a in (*srcs, *lands)],
        in_specs=[*_hbm(2 * n), SEM, SEM, pl.BlockSpec(memory_space=pl.ANY)],
        out_specs=_hbm(2 * n),
        input_output_aliases={i: i for i in range(2 * n)},
        compiler_params=pltpu.CompilerParams(has_side_effects=SIDE_EFFECT),
    )(*srcs, *lands, *sems, after)
    return list(res[n:])


def small_all_gather(small, name):
    flips = [(fx, fy, fc) for fx in (0, 1) for fy in (0, 1) for fc in (0, 1)][1:]

    def body(in_ref, out_ref, send_sems, recv_sems, local_sem):
        x, y, c = _place()
        me = 4 * x + 2 * y + c

        def copy(k, slot):
            fx, fy, fc = flips[k]
            return pltpu.make_async_remote_copy(
                src_ref=in_ref, dst_ref=out_ref.at[slot], send_sem=send_sems.at[k], recv_sem=recv_sems.at[k],
                device_id=(x ^ fx, y ^ fy, c ^ fc), device_id_type=MESH)

        local = pltpu.make_async_copy(in_ref, out_ref.at[me], local_sem)
        local.start()
        for k in range(len(flips)):
            copy(k, me).start()
        for k, (fx, fy, fc) in enumerate(flips):
            copy(k, 4 * (x ^ fx) + 2 * (y ^ fy) + (c ^ fc)).wait()
        local.wait()

    return pl.pallas_call(
        body, name=name, in_specs=_hbm(1), out_specs=_hbm(1)[0],
        out_shape=jax.ShapeDtypeStruct((N_DEV,) + small.shape, small.dtype),
        scratch_shapes=[pltpu.SemaphoreType.DMA((len(flips),)), pltpu.SemaphoreType.DMA((len(flips),)),
                        pltpu.SemaphoreType.DMA],
    )(small)


def sibling_exchange(parts, name):
    n = len(parts)

    def body(*refs):
        ins, outs = refs[:n], refs[n:2 * n]
        send_sems, recv_sems = refs[2 * n:]
        x, y, c = _place()
        copies = [pltpu.make_async_remote_copy(
            src_ref=ins[i], dst_ref=outs[i], send_sem=send_sems.at[i], recv_sem=recv_sems.at[i],
            device_id=(x, y, 1 - c), device_id_type=MESH) for i in range(n)]
        for cp in copies:
            cp.start()
        for cp in copies:
            cp.wait()

    return pl.pallas_call(
        body, name=name, in_specs=_hbm(n), out_specs=_hbm(n),
        out_shape=[jax.ShapeDtypeStruct(p.shape, p.dtype) for p in parts],
        scratch_shapes=[pltpu.SemaphoreType.DMA((n,)), pltpu.SemaphoreType.DMA((n,))],
    )(*parts)


def _row_tile(rows):
    for t in (256, 176, 128, 64, 32, 16, 8):
        if rows % t == 0:
            return t
    return rows


def chip_partial_sum(me, own_sm, recv, name):
    _, rows, cols = own_sm.shape
    tr = _row_tile(rows)

    def body(me_ref, own_ref, r0, r1, r2, r3, o_ref):
        acc = jnp.zeros((tr, cols), F32)
        for s, r_ref in enumerate((r0, r1, r2, r3)):
            acc = acc + jnp.where(me_ref[0] == s, own_ref[...], r_ref[...].astype(F32))
        o_ref[...] = acc

    def slot(s):
        return pl.BlockSpec((None, tr, cols), lambda i, me_ref, s=s: (s, i, 0))

    return pl.pallas_call(
        body, name=name,
        grid_spec=pltpu.PrefetchScalarGridSpec(
            num_scalar_prefetch=1, grid=(rows // tr,),
            in_specs=[pl.BlockSpec((None, tr, cols), lambda i, me_ref: (me_ref[0], i, 0))] + [slot(s) for s in range(4)],
            out_specs=pl.BlockSpec((tr, cols), lambda i, me_ref: (i, 0))),
        out_shape=jax.ShapeDtypeStruct((rows, cols), F32),
        compiler_params=_params("parallel"),
    )(me, own_sm, recv, recv, recv, recv)


def _adamw(w, g, m, v):
    m = ADAM_B1 * m + (1.0 - ADAM_B1) * g
    v = ADAM_B2 * v + (1.0 - ADAM_B2) * (g * g)
    m_hat = m / (1.0 - ADAM_B1 ** ADAM_STEP)
    v_hat = v / (1.0 - ADAM_B2 ** ADAM_STEP)
    delta = -ADAM_LR * (m_hat / (jnp.sqrt(v_hat) + ADAM_EPS) + ADAM_WD * w)
    return delta, m, v


def adamw_pair(part, sib, w, m, v, name):
    rows, cols = w.shape
    tr = _row_tile(rows)

    def body(p_ref, s_ref, w_ref, m_ref, v_ref, g_ref, d_ref, nm_ref, nv_ref):
        g = p_ref[...] + s_ref[...]
        g_ref[...] = g
        d_ref[...], nm_ref[...], nv_ref[...] = _adamw(w_ref[...], g, m_ref[...], v_ref[...])

    spec = pl.BlockSpec((tr, cols), lambda i: (i, 0))
    return pl.pallas_call(
        body, name=name, grid=(rows // tr,), in_specs=[spec] * 5, out_specs=[spec] * 4,
        out_shape=[jax.ShapeDtypeStruct((rows, cols), F32)] * 4,
        compiler_params=_params("parallel"),
    )(part, sib, w, m, v)


def adamw_small(g_all, w, m, v, name):
    def body(ga_ref, w_ref, m_ref, v_ref, g_ref, d_ref, nm_ref, nv_ref):
        g = ga_ref[0]
        for k in range(1, N_DEV):
            g = g + ga_ref[k]
        g_ref[...] = g
        d_ref[...], nm_ref[...], nv_ref[...] = _adamw(w_ref[...], g, m_ref[...], v_ref[...])

    return pl.pallas_call(
        body, name=name, out_shape=[jax.ShapeDtypeStruct(w.shape, F32)] * 4,
    )(g_all, w, m, v)


WEIGHTS = ("ffn1_norm_pre", "ffn1_w_in", "ffn1_w_out", "ffn1_norm_post", "mix_norm_pre", "w_in", "sinks",
           "mem_norm", "w_mem_kv", "w_gate", "b_gate", "w_o_a", "w_o_b", "w_o_m", "w_out", "mix_norm_post",
           "ffn2_norm_pre", "ffn2_w_in", "ffn2_w_out", "ffn2_norm_post")
BIG = ("ffn1_w_in", "ffn1_w_out", "w_in", "w_mem_kv", "w_gate", "w_o_a", "w_o_b", "w_o_m", "w_out",
       "ffn2_w_in", "ffn2_w_out")
GATHER_ORDER = ("ffn1_in", "ffn1_out", "mix", "ffn2")
GATHER_GROUPS = {"ffn1_in": ("ffn1_w_in",), "ffn1_out": ("ffn1_w_out",),
                 "mix": ("w_in", "w_gate", "w_mem_kv", "w_o_a", "w_o_b", "w_o_m", "w_out"),
                 "ffn2": ("ffn2_w_in", "ffn2_w_out")}
GROUPS = {"ffn1_in": ("ffn1_w_in",), "ffn1_out": ("ffn1_w_out",),
          "mix": ("w_in", "w_gate", "w_mem_kv", "w_o_a", "w_o_b", "w_o_m", "w_out"),
          "ffn2_in": ("ffn2_w_in",), "ffn2_out": ("ffn2_w_out",)}
COLUMN_SHARDED = ("ffn1_w_in", "ffn2_w_in", "w_in", "w_gate", "w_o_a", "w_o_b", "w_o_m")
KEPT_SHARD_MAJOR = ("ffn1_w_in", "ffn2_w_in", "w_gate")
GAINS = ("ffn1_norm_pre", "ffn1_norm_post", "mix_norm_pre", "mem_norm", "mix_norm_post", "ffn2_norm_pre",
         "ffn2_norm_post")
SMALL_ROWS = 16


def _pack_small(t):
    sinks = jnp.pad(t["sinks"], ((0, 0), (0, D_MODEL - t["sinks"].shape[1])))
    rows = [t[k] for k in GAINS] + [t["b_gate"].reshape(3, D_MODEL), sinks]
    packed = jnp.concatenate(rows, axis=0)
    return jnp.pad(packed, ((0, SMALL_ROWS - packed.shape[0]), (0, 0)))


def _unpack_small(p):
    out = {k: p[i:i + 1] for i, k in enumerate(GAINS)}
    out["b_gate"] = p[7:10].reshape(1, 3 * D_MODEL)
    out["sinks"] = p[10:11, :4]
    return out


def kernel(x, mem, ffn1_norm_pre, ffn1_w_in, ffn1_w_out, ffn1_norm_post, mix_norm_pre, w_in, sinks, mem_norm, w_mem_kv, w_gate, b_gate, w_o_a, w_o_b, w_o_m, w_out, mix_norm_post, ffn2_norm_pre, ffn2_w_in, ffn2_w_out, ffn2_norm_post, loss_target, m_ffn1_norm_pre, m_ffn1_w_in, m_ffn1_w_out, m_ffn1_norm_post, m_mix_norm_pre, m_w_in, m_sinks, m_mem_norm, m_w_mem_kv, m_w_gate, m_b_gate, m_w_o_a, m_w_o_b, m_w_o_m, m_w_out, m_mix_norm_post, m_ffn2_norm_pre, m_ffn2_w_in, m_ffn2_w_out, m_ffn2_norm_post, v_ffn1_norm_pre, v_ffn1_w_in, v_ffn1_w_out, v_ffn1_norm_post, v_mix_norm_pre, v_w_in, v_sinks, v_mem_norm, v_w_mem_kv, v_w_gate, v_b_gate, v_w_o_a, v_w_o_b, v_w_o_m, v_w_out, v_mix_norm_post, v_ffn2_norm_pre, v_ffn2_w_in, v_ffn2_w_out, v_ffn2_norm_post):
    given = dict(locals())
    wt = {k: given[k] for k in WEIGHTS}
    mom = {k: given["m_" + k] for k in WEIGHTS}
    var = {k: given["v_" + k] for k in WEIGHTS}
    chip = (2 * lax.axis_index("x") + lax.axis_index("y")).astype(jnp.int32)
    me = chip.reshape(1)

    def landing_zone(own):
        return lax.dynamic_update_slice_in_dim(lax.empty((N_CHIPS,) + own.shape, own.dtype), own[None], chip, 0)

    shards = [wt[k][0].astype(BF16) for k in BIG]
    members = [[BIG.index(k) for k in GATHER_GROUPS[g]] for g in GATHER_ORDER]
    sems, shards, lands, token = chip_copies_start(
        shards, [landing_zone(s) for s in shards], members, False, "weight_gather_start")

    def weights_of(group, after):
        idx = members[GATHER_ORDER.index(group)]
        got = chip_copies_wait([shards[i] for i in idx], [lands[i] for i in idx], sems[GATHER_ORDER.index(group)],
                               after, False, f"weight_gather_wait_{group}")
        full = {}
        for k, g in zip(GATHER_GROUPS[group], got):
            if k in COLUMN_SHARDED:
                if k in ("ffn1_w_in", "ffn2_w_in"):
                    g = jnp.stack([g[0], g[2], g[1], g[3]])
                full[k] = jnp.swapaxes(g, 0, 1).reshape(g.shape[1], N_CHIPS * g.shape[2])
            else:
                full[k] = g.reshape(N_CHIPS * g.shape[1], g.shape[2])
        return full

    in_flight = {}

    def send_grads(group, grads):
        own, wire = [], []
        for k in GROUPS[group]:
            g = grads[k]
            if k in KEPT_SHARD_MAJOR:
                pass
            elif k in COLUMN_SHARDED:
                g = jnp.swapaxes(g.reshape(g.shape[0], N_CHIPS, g.shape[1] // N_CHIPS), 0, 1)
            else:
                g = g.reshape(N_CHIPS, g.shape[0] // N_CHIPS, g.shape[1])
            own.append(g)
            wire.append(g.astype(BF16))
        zones = [landing_zone(lax.dynamic_index_in_dim(b, chip, 0, keepdims=False)) for b in wire]
        pair, wire, zones, sent = chip_copies_start(
            wire, zones, [list(range(len(wire)))], True, f"grad_scatter_start_{group}")
        in_flight[group] = (own, wire, zones, pair[0], sent)
        return sent

    gains = {k: wt[k] for k in GAINS}
    sq, dx, grads = layer_step(
        x[0], mem[0], loss_target[0], gains, sinks[0], b_gate, weights_of, send_grads, token[0, 0])
    loss = lax.psum(0.5 * sq[0, 0] / D_MODEL, ("x", "y", "c"))

    res = {}
    after = in_flight["ffn1_in"][4]
    for stage in (("ffn2_out", "ffn2_in", "mix", "ffn1_out"), ("ffn1_in",)):
        names, parts = [], []
        for group in stage:
            own, wire, zones, pair, _ = in_flight[group]
            received = chip_copies_wait(wire, zones, pair, after, True, f"grad_scatter_wait_{group}")
            for k, g, r in zip(GROUPS[group], own, received):
                names.append(k)
                parts.append(chip_partial_sum(me, g, r, f"{k}_chip_sum"))
        sibs = sibling_exchange(parts, f"sibling_exchange_{stage[-1]}")
        for k, p, s in zip(names, parts, sibs):
            res[k] = [t[None] for t in adamw_pair(p, s, wt[k][0], mom[k][0], var[k][0], f"{k}_adamw")]
        after = res[names[-1]][0]
    small_all = small_all_gather(_pack_small(grads), "small_grad_gather")
    packed = adamw_small(small_all, _pack_small(wt), _pack_small(mom), _pack_small(var), "small_adamw")
    for idx, p in enumerate(packed):
        for k, t in _unpack_small(p).items():
            res.setdefault(k, [None] * 4)[idx] = t

    return (loss, dx[None], *[res[k][0] for k in WEIGHTS], *[res[k][1] for k in WEIGHTS],
            *[res[k][2] for k in WEIGHTS], *[res[k][3] for k in WEIGHTS])
```

```python
import functools

import jax
import jax.numpy as jnp
from jax import lax
from jax.experimental import pallas as pl
from jax.experimental.pallas import tpu as pltpu

F32 = jnp.float32
BF16 = jnp.bfloat16

D_MODEL = 1024
D_FF = 2816
HEAD = 128
N_CHIPS = 4
N_DEV = 8
EPS = 1e-6
NEG_INF = -1e30
ROPE_THETA = 10000.0
ATT_SCALE = HEAD ** -0.5

ADAM_LR = 0.001
ADAM_B1 = 0.9
ADAM_B2 = 0.999
ADAM_EPS = 1e-08
ADAM_WD = 0.01
ADAM_STEP = 10

VMEM_LIMIT = 52 * 2 ** 20
MESH = pl.DeviceIdType.MESH

QKV_W = 3840
DIL = ((128, 1), (512, 4), (2048, 16))
B_BASE, MQ, A_BASE = 0, 8, 12
_AQ, _AK, _AV, _BQ, _BK, _BV, _MQ = 0, 6, 12, 18, 22, 24, 26
HEAD_ORDER = tuple(
    [h for j in range(2) for h in (_BQ + 2 * j, _BQ + 2 * j + 1, _BK + j, _BV + j)]
    + [_MQ + i for i in range(4)]
    + [h for g in range(3) for i in range(2) for h in (_AQ + 2 * g + i, _AK + 2 * g + i, _AV + 2 * g + i)])
ROTARY_HEADS = tuple(p for p, h in enumerate(HEAD_ORDER) if h < _AV or _BQ <= h < _BV)


def to_kernel_heads(w):
    return jnp.concatenate([w[..., h * HEAD:(h + 1) * HEAD] for h in HEAD_ORDER], axis=-1)


def from_kernel_heads(w):
    place = {h: p for p, h in enumerate(HEAD_ORDER)}
    return jnp.concatenate([w[..., place[h] * HEAD:(place[h] + 1) * HEAD] for h in range(len(HEAD_ORDER))], axis=-1)

TM = 512
FF_T = D_FF // 2


def _params(*sem):
    return pltpu.CompilerParams(dimension_semantics=sem, vmem_limit_bytes=VMEM_LIMIT)


def _dot(a, b):
    return jnp.dot(a, b, preferred_element_type=F32)


def _dot_nt(a, b):
    return lax.dot_general(a, b, (((1,), (1,)), ((), ())), preferred_element_type=F32)


def _dot_tn(a, b):
    return lax.dot_general(a, b, (((0,), (0,)), ((), ())), preferred_element_type=F32)


def _rstd(x):
    return lax.rsqrt(jnp.mean(x * x, axis=-1, keepdims=True) + EPS)


def _sigmoid(x):
    return 0.5 * jnp.tanh(0.5 * x) + 0.5


def _ffn_perm(k):
    return (k % 2) * 2 + k // 2


UNREAD = pl.BlockSpec(memory_space=pl.ANY)


def _resident(arr):
    return pl.BlockSpec(arr.shape, lambda *_: (0,) * arr.ndim, pipeline_mode=pl.Buffered(1))


def ffn_in(h, g, w, name):
    T, D = h.shape

    def body(h_ref, g_ref, w_ref, xn_ref, gu_ref, a_ref):
        x = h_ref[...]
        xn = (x * _rstd(x) * g_ref[...]).astype(BF16)
        xn_ref[...] = xn
        for j in range(2):
            gu = _dot(xn, w_ref[:, j * 2 * FF_T:(j + 1) * 2 * FF_T])
            gu_ref[:, j * 2 * FF_T:(j + 1) * 2 * FF_T] = gu.astype(BF16)
            gate, up = gu[:, :FF_T], gu[:, FF_T:]
            a_ref[:, j * FF_T:(j + 1) * FF_T] = (gate * _sigmoid(gate) * up).astype(BF16)

    def rows(width):
        return pl.BlockSpec((TM, width), lambda i: (i, 0))

    return pl.pallas_call(
        body, name=name,
        grid=(T // TM,),
        in_specs=[rows(D), _resident(g), _resident(w)],
        out_specs=[rows(D), rows(2 * D_FF), rows(D_FF)],
        out_shape=[jax.ShapeDtypeStruct((T, D), BF16),
                   jax.ShapeDtypeStruct((T, 2 * D_FF), BF16),
                   jax.ShapeDtypeStruct((T, D_FF), BF16)],
        compiler_params=_params("parallel"),
    )(h, g, w)


def mm_norm_res(a, w, h_in, g, coef, name, target=None):
    T, K = a.shape
    D = w.shape[1]
    final = target is not None

    def body(*refs):
        if final:
            a_ref, w_ref, h_ref, g_ref, t_ref, f_ref, o_ref, l_ref = refs
        else:
            a_ref, w_ref, h_ref, g_ref, f_ref, o_ref = refs
        f = _dot(a_ref[...], w_ref[...])
        f_ref[...] = f
        y = h_ref[...] + coef * (f * _rstd(f) * g_ref[...])
        if final:
            err = y - t_ref[...]
            o_ref[...] = err * (1.0 / D)

            @pl.when(pl.program_id(0) == 0)
            def _():
                l_ref[...] = jnp.zeros_like(l_ref)

            l_ref[...] += jnp.sum(err * err)
        else:
            o_ref[...] = y

    row = pl.BlockSpec((TM, D), lambda i: (i, 0))
    in_specs = [pl.BlockSpec((TM, K), lambda i: (i, 0)),
                _resident(w),
                row, pl.BlockSpec((1, D), lambda i: (0, 0))]
    out_specs = [row, row]
    out_shape = [jax.ShapeDtypeStruct((T, D), F32), jax.ShapeDtypeStruct((T, D), F32)]
    args = [a, w, h_in, g]
    if final:
        in_specs.append(row)
        args.append(target)
        out_specs.append(pl.BlockSpec((8, 128), lambda i: (0, 0)))
        out_shape.append(jax.ShapeDtypeStruct((8, 128), F32))
    return pl.pallas_call(
        body, name=name, grid=(T // TM,), in_specs=in_specs, out_specs=out_specs, out_shape=out_shape,
        compiler_params=_params("arbitrary"),
    )(*args)


def _rope(x, cos, sin_signed):
    return x * cos + pltpu.roll(x, HEAD // 2, axis=1) * sin_signed


def _unrope(x, cos, sin_signed):
    return x * cos - pltpu.roll(x, HEAD // 2, axis=1) * sin_signed


def mix_in(h, g, w, w_gate, b_gate, cos, sin_signed, name):
    T, D = h.shape
    tn = 768

    def body(h_ref, g_ref, w_ref, wg_ref, b_ref, c_ref, s_ref, u_ref, o_ref, gt_ref):
        x = h_ref[...]
        u = (x * _rstd(x) * g_ref[...]).astype(BF16)
        u_ref[...] = u
        c, s = c_ref[...], s_ref[...]
        for j in range(QKV_W // tn):
            acc = _dot(u, w_ref[:, j * tn:(j + 1) * tn])
            for hd in range(tn // HEAD):
                head = j * (tn // HEAD) + hd
                part = acc[:, hd * HEAD:(hd + 1) * HEAD]
                if head in ROTARY_HEADS:
                    part = _rope(part, c, s)
                o_ref[:, head * HEAD:(head + 1) * HEAD] = part.astype(BF16)
        for j in range(w_gate.shape[1] // tn):
            cols = slice(j * tn, (j + 1) * tn)
            gt_ref[:, cols] = _sigmoid(_dot(u, wg_ref[:, cols]) + b_ref[:, cols]).astype(BF16)

    def rows(width):
        return pl.BlockSpec((TM, width), lambda i: (i, 0))

    return pl.pallas_call(
        body, name=name,
        grid=(T // TM,),
        in_specs=[rows(D), _resident(g), _resident(w), _resident(w_gate), _resident(b_gate), rows(HEAD), rows(HEAD)],
        out_specs=[rows(D), rows(QKV_W), rows(w_gate.shape[1])],
        out_shape=[jax.ShapeDtypeStruct((T, D), BF16), jax.ShapeDtypeStruct((T, QKV_W), BF16),
                   jax.ShapeDtypeStruct((T, w_gate.shape[1]), BF16)],
        compiler_params=_params("parallel"),
    )(h, g, w, w_gate, b_gate, cos, sin_signed)


def gate_merge(gt, o_a, o_b, o_m, w_a, w_b, w_m, name):
    T = gt.shape[0]
    D = D_MODEL

    def body(gt_ref, oa_ref, ob_ref, om_ref, wa_ref, wb_ref, wm_ref, out_ref):
        acc = gt_ref[:, :D].astype(F32) * _dot(oa_ref[...], wa_ref[...])
        acc += gt_ref[:, D:2 * D].astype(F32) * _dot(ob_ref[...], wb_ref[...])
        acc += gt_ref[:, 2 * D:].astype(F32) * _dot(om_ref[...], wm_ref[...])
        out_ref[...] = acc.astype(BF16)

    def rows(width):
        return pl.BlockSpec((TM, width), lambda i: (i, 0))

    def whole(arr):
        return pl.BlockSpec(arr.shape, lambda i: (0, 0))

    return pl.pallas_call(
        body, name=name, grid=(T // TM,),
        in_specs=[rows(3 * D), rows(o_a.shape[1]), rows(o_b.shape[1]), rows(o_m.shape[1]),
                  whole(w_a), whole(w_b), whole(w_m)],
        out_specs=rows(D),
        out_shape=jax.ShapeDtypeStruct((T, D), BF16),
        compiler_params=_params("parallel"),
    )(gt, o_a, o_b, o_m, w_a, w_b, w_m)


def _band_rows(start, r):
    return pl.ds(start, HEAD) if r == 1 else pl.ds(start, HEAD, stride=r)


def _band_mask(max_dist, first_has_prev):
    row = lax.broadcasted_iota(jnp.int32, (HEAD, 2 * HEAD), 0)
    col = lax.broadcasted_iota(jnp.int32, (HEAD, 2 * HEAD), 1)
    dist = row + HEAD - col
    band = (dist >= 0) & (dist <= max_dist)
    return band, band & (col >= jnp.where(first_has_prev, 0, HEAD))


def _stack(parts):
    return parts[0] if len(parts) == 1 else jnp.concatenate(parts, axis=0)


def _band_specs(BT, SB, nsub, base, grp):
    stride = grp + 2

    def cur(off, width):
        return pl.BlockSpec((BT, width * HEAD), lambda h, i: (i, (base + h * stride + off) // width))

    def prev(off):
        return pl.BlockSpec((SB, HEAD), lambda h, i: (jnp.maximum(i * nsub - 1, 0), base + h * stride + off))

    return cur(0, grp), cur(grp, 1), prev(grp), cur(grp + 1, 1), prev(grp + 1)


def band_fwd(qkv, sinks, *, r, base, hkv, grp, max_dist, out_dtype, name):
    T, W = qkv.shape
    SB = HEAD * r
    BT = min(2048, T)
    nsub, nib = BT // SB, T // BT
    hq = hkv * grp
    heads = [slice(g * HEAD, (g + 1) * HEAD) for g in range(grp)]

    def body(sink_ref, q_ref, kc_ref, kp_ref, vc_ref, vp_ref, o_ref, l_ref, qf, kf, vf):
        kvh, ib = pl.program_id(0), pl.program_id(1)
        qf[...] = q_ref[...].astype(F32)
        kf[:SB] = kp_ref[...].astype(F32)
        kf[SB:] = kc_ref[...].astype(F32)
        vf[:SB] = vp_ref[...].astype(F32)
        vf[SB:] = vc_ref[...].astype(F32)
        band, band_first = _band_mask(max_dist, ib > 0)
        for c in range(r):
            k_old, v_old = kf[_band_rows(c, r)], vf[_band_rows(c, r)]
            for j in range(nsub):
                mask = band_first if j == 0 else band
                rows = _band_rows(j * SB + c, r)
                k_own, v_own = kf[_band_rows((j + 1) * SB + c, r)], vf[_band_rows((j + 1) * SB + c, r)]
                kcat = jnp.concatenate([k_old, k_own], axis=0).astype(BF16)
                vcat = jnp.concatenate([v_old, v_own], axis=0).astype(BF16)
                k_old, v_old = k_own, v_own
                s_all = _dot_nt(_stack([qf[rows, cols] for cols in heads]).astype(BF16), kcat) * ATT_SCALE
                probs, tots = [], []
                for g, cols in enumerate(heads):
                    s = jnp.where(mask, s_all[cols], NEG_INF)
                    sk = sink_ref[kvh * grp + g]
                    m = jnp.maximum(jnp.max(s, axis=-1, keepdims=True), sk)
                    p = jnp.exp(s - m)
                    tot = jnp.sum(p, axis=-1, keepdims=True) + jnp.exp(sk - m)
                    probs.append(p.astype(BF16))
                    tots.append(tot)
                    l_ref[rows, cols] = jnp.broadcast_to(m + jnp.log(tot), (HEAD, HEAD))
                o_all = _dot(_stack(probs), vcat)
                for g, cols in enumerate(heads):
                    o_ref[rows, cols] = (o_all[cols] / tots[g]).astype(out_dtype)

    out_spec = pl.BlockSpec((BT, grp * HEAD), lambda h, i: (i, h))
    return pl.pallas_call(
        body, name=name, grid=(hkv, nib),
        in_specs=[pl.BlockSpec(memory_space=pltpu.SMEM), *_band_specs(BT, SB, nsub, base, grp)],
        out_specs=[out_spec, out_spec],
        out_shape=[jax.ShapeDtypeStruct((T, hq * HEAD), out_dtype), jax.ShapeDtypeStruct((T, hq * HEAD), F32)],
        scratch_shapes=[pltpu.VMEM((BT, grp * HEAD), F32), pltpu.VMEM((SB + BT, HEAD), F32),
                        pltpu.VMEM((SB + BT, HEAD), F32)],
        compiler_params=_params("parallel", "arbitrary"),
    )(sinks, qkv, qkv, qkv, qkv, qkv)


def band_bwd(qkv, dqkv, do, o, lse, cos, sin_signed, sinks, *, r, base, hkv, grp, max_dist, name):
    T, W = qkv.shape
    SB = HEAD * r
    BT = min(2048, T)
    nsub, nib = BT // SB, T // BT
    nblk = T // SB
    with_sink = sinks is not None
    heads = [slice(g * HEAD, (g + 1) * HEAD) for g in range(grp)]

    def body(*refs):
        if with_sink:
            sink_ref, refs = refs[0], refs[1:]
        (q_ref, kc_ref, kp_ref, vc_ref, vp_ref, qn_ref, do_ref, don_ref, o_ref, on_ref, l_ref, ln_ref,
         c_ref, s_ref, _) = refs[:15]
        out_ref = refs[15]
        ds_ref = refs[16] if with_sink else None
        qf, dof, of, kf, vf, dqf, dkf, dvf = refs[-8:]
        kvh, ib = pl.program_id(0), pl.program_id(1)
        for buf, cur_ref, nxt_ref in ((qf, q_ref, qn_ref), (dof, do_ref, don_ref), (of, o_ref, on_ref)):
            buf[:BT] = cur_ref[...].astype(F32)
            buf[BT:] = nxt_ref[...].astype(F32)
        kf[:SB] = kp_ref[...].astype(F32)
        kf[SB:] = kc_ref[...].astype(F32)
        vf[:SB] = vp_ref[...].astype(F32)
        vf[SB:] = vc_ref[...].astype(F32)
        band, band_first = _band_mask(max_dist, ib > 0)
        if with_sink:
            @pl.when(ib == 0)
            def _():
                ds_ref[...] = jnp.zeros_like(ds_ref)

        def grads(rows, logzs, keys, vals, mask):
            q = _stack([qf[rows, cols] for cols in heads]).astype(BF16)
            dout = _stack([dof[rows, cols] for cols in heads]).astype(BF16)
            s_all = _dot_nt(q, keys) * ATT_SCALE
            dp_all = _dot_nt(dout, vals)
            probs, dss, deltas = [], [], []
            for g, cols in enumerate(heads):
                delta = jnp.sum(dof[rows, cols] * of[rows, cols], axis=-1, keepdims=True)
                p = jnp.exp(jnp.where(mask, s_all[cols], NEG_INF) - logzs[g][:, :1])
                probs.append(p.astype(BF16))
                dss.append((p * (dp_all[cols] - delta) * ATT_SCALE).astype(BF16))
                deltas.append(delta)
            return q, dout, _stack(probs), _stack(dss), deltas

        row = lax.broadcasted_iota(jnp.int32, (HEAD, HEAD), 0)
        col = lax.broadcasted_iota(jnp.int32, (HEAD, HEAD), 1)
        reach = col >= row + jnp.where(ib < nib - 1, HEAD - max_dist, 2 * HEAD)
        for c in range(r):
            k_old, v_old = kf[_band_rows(c, r)], vf[_band_rows(c, r)]
            dk_own = dv_own = None
            for j in range(nsub):
                rows = _band_rows(j * SB + c, r)
                k_own, v_own = kf[_band_rows((j + 1) * SB + c, r)], vf[_band_rows((j + 1) * SB + c, r)]
                kcat = jnp.concatenate([k_old, k_own], axis=0).astype(BF16)
                vcat = jnp.concatenate([v_old, v_own], axis=0).astype(BF16)
                logzs = [l_ref[rows, cols] for cols in heads]
                q, dout, p, ds, deltas = grads(rows, logzs, kcat, vcat, band_first if j == 0 else band)
                dq = _dot(ds, kcat)
                for g, cols in enumerate(heads):
                    dqf[rows, cols] = dq[cols]
                    if with_sink:
                        p_sink = jnp.exp(sink_ref[kvh * grp + g] - logzs[g][:, :1])
                        ds_ref[g * 8:(g + 1) * 8] += jnp.sum(p_sink * deltas[g])
                dk, dv = _dot_tn(ds, q), _dot_tn(p, dout)
                if j > 0:
                    done = _band_rows((j - 1) * SB + c, r)
                    dkf[done] = dk_own + dk[:HEAD]
                    dvf[done] = dv_own + dv[:HEAD]
                dk_own, dv_own = dk[HEAD:], dv[HEAD:]
                k_old, v_old = k_own, v_own
            logzs = [ln_ref[_band_rows(c, r), cols] for cols in heads]
            q, dout, p, ds, _ = grads(_band_rows(BT + c, r), logzs, k_old.astype(BF16), v_old.astype(BF16), reach)
            done = _band_rows((nsub - 1) * SB + c, r)
            dkf[done] = dk_own + _dot_tn(ds, q)
            dvf[done] = dv_own + _dot_tn(p, dout)

        cs, sn = c_ref[...], s_ref[...]
        for cols in heads:
            out_ref[:, cols] = _unrope(dqf[:, cols], cs, sn).astype(BF16)
        out_ref[:, grp * HEAD:(grp + 1) * HEAD] = _unrope(dkf[...], cs, sn).astype(BF16)
        out_ref[:, (grp + 1) * HEAD:] = dvf[...].astype(BF16)

    def nxt_row(i):
        return jnp.minimum((i + 1) * nsub, nblk - 1)

    stride = grp + 2
    q_next = pl.BlockSpec((SB, grp * HEAD), lambda h, i: (nxt_row(i), (base + h * stride) // grp))
    head_cur = pl.BlockSpec((BT, grp * HEAD), lambda h, i: (i, h))
    head_next = pl.BlockSpec((SB, grp * HEAD), lambda h, i: (nxt_row(i), h))
    table = pl.BlockSpec((BT, HEAD), lambda h, i: (i, 0))

    in_specs = [*_band_specs(BT, SB, nsub, base, grp), q_next,
                head_cur, head_next, head_cur, head_next, head_cur, head_next, table, table, UNREAD]
    args = [qkv, qkv, qkv, qkv, qkv, qkv, do, do, o, o, lse, lse, cos, sin_signed, dqkv]
    out_specs = [pl.BlockSpec((BT, stride * HEAD), lambda h, i: (i, base // stride + h))]
    out_shape = [jax.ShapeDtypeStruct(dqkv.shape, dqkv.dtype)]
    if with_sink:
        in_specs.insert(0, pl.BlockSpec(memory_space=pltpu.SMEM))
        args.insert(0, sinks)
        out_specs.append(pl.BlockSpec((None, grp * 8, HEAD), lambda h, i: (h, 0, 0)))
        out_shape.append(jax.ShapeDtypeStruct((hkv, grp * 8, HEAD), F32))
    wide = pltpu.VMEM((BT + SB, grp * HEAD), F32)
    tall = pltpu.VMEM((SB + BT, HEAD), F32)
    grad = pltpu.VMEM((BT, HEAD), F32)
    return pl.pallas_call(
        body, name=name, grid=(hkv, nib), in_specs=in_specs, out_specs=out_specs, out_shape=out_shape,
        input_output_aliases={len(args) - 1: 0},
        scratch_shapes=[wide, wide, wide, tall, tall, pltpu.VMEM((BT, grp * HEAD), F32), grad, grad],
        compiler_params=_params("parallel", "arbitrary"),
    )(*args)


def merge_groups(outs, lses, name):
    T, Wd = outs[0].shape
    tm = 1024

    def body(o0, o1, o2, l0, l1, l2, out_ref, lt_ref):
        a, b, c = l0[...], l1[...], l2[...]
        m = jnp.maximum(jnp.maximum(a, b), c)
        wa, wb, wc = jnp.exp(a - m), jnp.exp(b - m), jnp.exp(c - m)
        z = wa + wb + wc
        out_ref[...] = ((wa * o0[...] + wb * o1[...] + wc * o2[...]) / z).astype(BF16)
        lt_ref[...] = m + jnp.log(z)

    spec = pl.BlockSpec((tm, Wd), lambda i: (i, 0))
    return pl.pallas_call(
        body, name=name, grid=(T // tm,), in_specs=[spec] * 6, out_specs=[spec, spec],
        out_shape=[jax.ShapeDtypeStruct((T, Wd), BF16), jax.ShapeDtypeStruct((T, Wd), F32)],
        compiler_params=_params("parallel"),
    )(*outs, *lses)


M_HEADS = 4


def mem_kv(mem, g, w, name):
    n, D = mem.shape

    def body(m_ref, g_ref, w_ref, mn_ref, kv_ref):
        x = m_ref[...]
        mn = (x * _rstd(x) * g_ref[...]).astype(BF16)
        mn_ref[...] = mn
        kv_ref[...] = _dot(mn, w_ref[...]).astype(BF16)

    return pl.pallas_call(
        body, name=name,
        out_shape=[jax.ShapeDtypeStruct((n, D), BF16), jax.ShapeDtypeStruct((n, w.shape[1]), BF16)],
        compiler_params=pltpu.CompilerParams(vmem_limit_bytes=VMEM_LIMIT),
    )(mem, g, w)


def mem_fwd(qkv, mkv, name):
    T = qkv.shape[0]
    n = mkv.shape[0]
    RB = 1024

    def body(q_ref, k_ref, v_ref, o_ref, l_ref):
        s = _dot_nt(q_ref[...], k_ref[...]) * ATT_SCALE
        m = jnp.max(s, axis=-1, keepdims=True)
        p = jnp.exp(s - m)
        den = jnp.sum(p, axis=-1, keepdims=True)
        o_ref[...] = (_dot(p.astype(BF16), v_ref[...]) / den).astype(BF16)
        l_ref[...] = jnp.broadcast_to(m + jnp.log(den), (RB, HEAD))

    out = pl.BlockSpec((RB, HEAD), lambda h, i: (i, h))
    return pl.pallas_call(
        body, name=name, grid=(M_HEADS, T // RB),
        in_specs=[pl.BlockSpec((RB, HEAD), lambda h, i: (i, MQ + h)),
                  pl.BlockSpec((n, HEAD), lambda h, i: (0, h)),
                  pl.BlockSpec((n, HEAD), lambda h, i: (0, M_HEADS + h))],
        out_specs=[out, out],
        out_shape=[jax.ShapeDtypeStruct((T, M_HEADS * HEAD), BF16), jax.ShapeDtypeStruct((T, M_HEADS * HEAD), F32)],
        compiler_params=_params("parallel", "parallel"),
    )(qkv, mkv, mkv)


def mem_bwd(qkv, dqkv, mkv, do, o, lse, name):
    T = qkv.shape[0]
    n = mkv.shape[0]
    RB = 1024

    def body(q_ref, k_ref, v_ref, do_ref, o_ref, l_ref, _, dq_ref, dk_ref, dv_ref):
        @pl.when(pl.program_id(1) == 0)
        def _():
            dk_ref[...] = jnp.zeros_like(dk_ref)
            dv_ref[...] = jnp.zeros_like(dv_ref)

        q, dout = q_ref[...], do_ref[...]
        delta = jnp.sum(dout.astype(F32) * o_ref[...].astype(F32), axis=-1, keepdims=True)
        p = jnp.exp(_dot_nt(q, k_ref[...]) * ATT_SCALE - l_ref[...][:, :1])
        ds = (p * (_dot_nt(dout, v_ref[...]) - delta) * ATT_SCALE).astype(BF16)
        dq_ref[...] = _dot(ds, k_ref[...]).astype(BF16)
        dk_ref[...] += _dot_tn(ds, q)
        dv_ref[...] += _dot_tn(p.astype(BF16), dout)

    tok = pl.BlockSpec((RB, HEAD), lambda h, i: (i, h))
    slot = pl.BlockSpec((n, HEAD), lambda h, i: (0, h))
    return pl.pallas_call(
        body, name=name, grid=(M_HEADS, T // RB),
        in_specs=[pl.BlockSpec((RB, HEAD), lambda h, i: (i, MQ + h)),
                  slot, pl.BlockSpec((n, HEAD), lambda h, i: (0, M_HEADS + h)), tok, tok, tok, UNREAD],
        out_specs=[pl.BlockSpec((RB, HEAD), lambda h, i: (i, MQ + h)), slot, slot],
        out_shape=[jax.ShapeDtypeStruct(dqkv.shape, dqkv.dtype),
                   jax.ShapeDtypeStruct((n, M_HEADS * HEAD), F32),
                   jax.ShapeDtypeStruct((n, M_HEADS * HEAD), F32)],
        input_output_aliases={6: 0},
        compiler_params=_params("parallel", "arbitrary"),
    )(qkv, mkv, mkv, do, o, lse, dqkv)


def mem_kv_bwd(mem, g, mem_n, w, dmkv, name):
    n, D = mem.shape

    def body(m_ref, g_ref, mn_ref, w_ref, d_ref, dw_ref, dg_ref):
        d = d_ref[...].astype(BF16)
        dw_ref[...] = _dot_tn(mn_ref[...], d)
        x = m_ref[...]
        dg_ref[...] = jnp.sum(_dot_nt(d, w_ref[...]) * (x * _rstd(x)), axis=0, keepdims=True)

    return pl.pallas_call(
        body, name=name,
        out_shape=[jax.ShapeDtypeStruct(w.shape, F32), jax.ShapeDtypeStruct((1, D), F32)],
        compiler_params=pltpu.CompilerParams(vmem_limit_bytes=VMEM_LIMIT),
    )(mem, g, mem_n, w, dmkv)


def _rms_bwd(dn, f, g):
    r = _rstd(f)
    fhat = f * r
    dfhat = dn * g
    df = r * (dfhat - fhat * jnp.mean(dfhat * fhat, axis=-1, keepdims=True))
    return df, jnp.sum(dn * fhat, axis=0, keepdims=True)


TM_FFN_BWD = 256


def ffn_tokens_bwd(dh, f, h_in, gu, g_pre, g_post, w_in, w_out, coef, name, after):
    T, D = dh.shape
    tm = TM_FFN_BWD

    def body(dh_ref, f_ref, h_ref, gu_ref, gpre_ref, gpost_ref, win_ref, wout_ref, _,
             df_ref, dgu_ref, dhin_ref, dgpre_ref, dgpost_ref):
        @pl.when(pl.program_id(0) == 0)
        def _():
            dgpre_ref[...] = jnp.zeros_like(dgpre_ref)
            dgpost_ref[...] = jnp.zeros_like(dgpost_ref)

        dh = dh_ref[...]
        df, dg_post = _rms_bwd(coef * dh, f_ref[...], gpost_ref[...])
        dgpost_ref[...] += dg_post
        df = df.astype(BF16)
        df_ref[...] = df
        dxn = jnp.zeros((tm, D), F32)
        for j in range(2):
            lo, mid, hi = 2 * j * FF_T, (2 * j + 1) * FF_T, (2 * j + 2) * FF_T
            da = _dot_nt(df, wout_ref[j * FF_T:(j + 1) * FF_T, :])
            gate = gu_ref[:, lo:mid].astype(F32)
            up = gu_ref[:, mid:hi].astype(F32)
            sig = _sigmoid(gate)
            dgate = (da * up * sig * (1.0 + gate * (1.0 - sig))).astype(BF16)
            dup = (da * gate * sig).astype(BF16)
            dgu_ref[:, lo:mid] = dgate
            dgu_ref[:, mid:hi] = dup
            dxn += _dot_nt(dgate, win_ref[:, lo:mid]) + _dot_nt(dup, win_ref[:, mid:hi])
        h = h_ref[...]
        r = _rstd(h)
        xhat = h * r
        dxhat = dxn * gpre_ref[...]
        dhin_ref[...] = dh + r * (dxhat - xhat * jnp.mean(dxhat * xhat, axis=-1, keepdims=True))
        dgpre_ref[...] += jnp.sum(dxn * xhat, axis=0, keepdims=True)

    def rows(width):
        return pl.BlockSpec((tm, width), lambda i: (i, 0))

    vec = pl.BlockSpec((1, D), lambda i: (0, 0))
    return pl.pallas_call(
        body, name=name, grid=(T // tm,),
        in_specs=[rows(D), rows(D), rows(D), rows(2 * D_FF), _resident(g_pre), _resident(g_post),
                  _resident(w_in), _resident(w_out), UNREAD],
        out_specs=[rows(D), rows(2 * D_FF), rows(D), vec, vec],
        out_shape=[jax.ShapeDtypeStruct((T, D), BF16), jax.ShapeDtypeStruct((T, 2 * D_FF), BF16),
                   jax.ShapeDtypeStruct((T, D), F32), jax.ShapeDtypeStruct((1, D), F32),
                   jax.ShapeDtypeStruct((1, D), F32)],
        compiler_params=_params("arbitrary"),
    )(dh, f, h_in, gu, g_pre, g_post, w_in, w_out, after)


def mix_out_bwd(dh, f, g, w_out, name, after):
    T, D = dh.shape

    def body(dh_ref, f_ref, g_ref, w_ref, _, df_ref, dm_ref, dg_ref):
        df, dg = _rms_bwd(dh_ref[...], f_ref[...], g_ref[...])
        df = df.astype(BF16)
        df_ref[...] = df

        @pl.when(pl.program_id(0) == 0)
        def _():
            dg_ref[...] = jnp.zeros_like(dg_ref)

        dg_ref[...] += dg
        dm_ref[...] = _dot_nt(df, w_ref[...]).astype(BF16)

    row = pl.BlockSpec((TM, D), lambda i: (i, 0))
    vec = pl.BlockSpec((1, D), lambda i: (0, 0))
    return pl.pallas_call(
        body, name=name, grid=(T // TM,),
        in_specs=[row, row, vec, _resident(w_out), UNREAD],
        out_specs=[row, row, vec],
        out_shape=[jax.ShapeDtypeStruct((T, D), BF16), jax.ShapeDtypeStruct((T, D), BF16),
                   jax.ShapeDtypeStruct((1, D), F32)],
        compiler_params=_params("arbitrary"),
    )(dh, f, g, w_out, after)


def mm_nt_norm_bwd(pieces, h_in, dh_out, g, name, after):
    T, D = h_in.shape

    def body(*refs):
        ab = refs[:2 * len(pieces)]
        h_ref, dh_ref, g_ref, _, o_ref, dg_ref = refs[2 * len(pieces):]
        dxn = _dot_nt(ab[0][...], ab[1][...])
        for p in range(1, len(pieces)):
            dxn += _dot_nt(ab[2 * p][...], ab[2 * p + 1][...])
        h = h_ref[...]
        r = _rstd(h)
        xhat = h * r
        dxhat = dxn * g_ref[...]
        o_ref[...] = dh_ref[...] + r * (dxhat - xhat * jnp.mean(dxhat * xhat, axis=-1, keepdims=True))

        @pl.when(pl.program_id(0) == 0)
        def _():
            dg_ref[...] = jnp.zeros_like(dg_ref)

        dg_ref[...] += jnp.sum(dxn * xhat, axis=0, keepdims=True)

    in_specs, args = [], []
    for a, w in pieces:
        in_specs += [pl.BlockSpec((TM, a.shape[1]), lambda i: (i, 0)), _resident(w)]
        args += [a, w]
    row = pl.BlockSpec((TM, D), lambda i: (i, 0))
    return pl.pallas_call(
        body, name=name, grid=(T // TM,),
        in_specs=in_specs + [row, row, _resident(g), UNREAD],
        out_specs=[row, pl.BlockSpec((1, D), lambda i: (0, 0))],
        out_shape=[jax.ShapeDtypeStruct((T, D), F32), jax.ShapeDtypeStruct((1, D), F32)],
        compiler_params=_params("arbitrary"),
    )(*args, h_in, dh_out, g, after)


def gate_merge_bwd(dm, gt, o_a, o_b, o_m, w_a, w_b, w_m, name):
    T = dm.shape[0]
    D = D_MODEL
    branch = ((o_a, w_a), (o_b, w_b), (o_m, w_m))

    def body(dm_ref, gt_ref, oa_ref, ob_ref, om_ref, wa_ref, wb_ref, wm_ref,
             dgt_ref, dpa_ref, dpb_ref, dpm_ref, doa_ref, dob_ref, dom_ref, db_ref):
        @pl.when(pl.program_id(0) == 0)
        def _():
            db_ref[...] = jnp.zeros_like(db_ref)

        dmf = dm_ref[...].astype(F32)
        for x, (o_ref, w_ref, dp_ref, do_ref) in enumerate(((oa_ref, wa_ref, dpa_ref, doa_ref),
                                                           (ob_ref, wb_ref, dpb_ref, dob_ref),
                                                           (om_ref, wm_ref, dpm_ref, dom_ref))):
            cols = slice(x * D, (x + 1) * D)
            gx = gt_ref[:, cols].astype(F32)
            w = w_ref[...]
            dpre = dmf * _dot(o_ref[...], w) * gx * (1.0 - gx)
            dgt_ref[:, cols] = dpre.astype(BF16)
            db_ref[:, cols] += jnp.sum(dpre, axis=0, keepdims=True)
            dp = (dmf * gx).astype(BF16)
            dp_ref[...] = dp
            do_ref[...] = _dot_nt(dp, w).astype(BF16)

    def rows(width):
        return pl.BlockSpec((TM, width), lambda i: (i, 0))

    def whole(arr):
        return pl.BlockSpec(arr.shape, lambda i: (0, 0))

    widths = [o.shape[1] for o, _ in branch]
    return pl.pallas_call(
        body, name=name, grid=(T // TM,),
        in_specs=[rows(D), rows(3 * D)] + [rows(k) for k in widths] + [whole(w) for _, w in branch],
        out_specs=[rows(3 * D), rows(D), rows(D), rows(D)] + [rows(k) for k in widths]
                  + [pl.BlockSpec((1, 3 * D), lambda i: (0, 0))],
        out_shape=[jax.ShapeDtypeStruct((T, 3 * D), BF16)] + [jax.ShapeDtypeStruct((T, D), BF16)] * 3
                  + [jax.ShapeDtypeStruct((T, k), BF16) for k in widths]
                  + [jax.ShapeDtypeStruct((1, 3 * D), F32)],
        compiler_params=_params("arbitrary"),
    )(dm, gt, o_a, o_b, o_m, w_a, w_b, w_m)


def mm_tn(x, dy, tm, tn, name, shard_major=False, perm=None, slabs=1, after=None):
    T, M = x.shape
    N = dy.shape[1]
    tk = min(2048, T)
    perm = perm or (lambda j: j)
    w = tn // slabs

    def body(x_ref, dy_ref, *rest):
        o_ref = rest[-1]

        @pl.when(pl.program_id(2) == 0)
        def _():
            o_ref[...] = jnp.zeros_like(o_ref)

        acc = _dot_tn(x_ref[...], dy_ref[...])
        if shard_major:
            for s in range(slabs):
                o_ref[s] += acc[:, s * w:(s + 1) * w]
        else:
            o_ref[...] += acc

    if shard_major:
        out_spec = pl.BlockSpec((slabs, tm, w), lambda i, j, k: (perm(j), i, 0))
        out_shape = jax.ShapeDtypeStruct((N // w, M, w), F32)
    else:
        out_spec = pl.BlockSpec((tm, tn), lambda i, j, k: (i, j))
        out_shape = jax.ShapeDtypeStruct((M, N), F32)
    return pl.pallas_call(
        body, name=name, grid=(M // tm, N // tn, T // tk),
        in_specs=[pl.BlockSpec((tk, tm), lambda i, j, k: (k, i)),
                  pl.BlockSpec((tk, tn), lambda i, j, k: (k, j))] + ([] if after is None else [UNREAD]),
        out_specs=out_spec, out_shape=out_shape,
        compiler_params=_params("parallel", "parallel", "arbitrary"),
    )(x, dy, *([] if after is None else [after]))


def rope_tables(T, zero):
    half = HEAD // 2
    inv = ROPE_THETA ** (-jnp.arange(half, dtype=F32) / half)
    ang = (jnp.arange(T).astype(F32) + zero)[:, None] * inv[None, :]
    cos, sin = jnp.cos(ang), jnp.sin(ang)
    return jnp.concatenate([cos, cos], axis=1), jnp.concatenate([-sin, sin], axis=1)


def layer_step(x, mem, target, gains, sinks, b_gate, weights_of, send_grads, zero):
    T = x.shape[0]
    cos, sin_signed = rope_tables(T, zero)
    no_sink = jnp.full((2,), NEG_INF, F32)

    w = dict(weights_of("ffn1_in", cos))
    xn1, gu1, a1 = ffn_in(x, gains["ffn1_norm_pre"], w["ffn1_w_in"], "ffn1_in")
    w.update(weights_of("ffn1_out", xn1))
    f1, h1 = mm_norm_res(a1, w["ffn1_w_out"], x, gains["ffn1_norm_post"], 0.5, "ffn1_out")
    w.update(weights_of("mix", f1))
    u, qkv, gt = mix_in(h1, gains["mix_norm_pre"], w["w_in"], w["w_gate"], b_gate, cos, sin_signed, "mix_in")
    outs, lses = [], []
    for gidx, (window, dil) in enumerate(DIL):
        o_g, l_g = band_fwd(qkv, no_sink, r=dil, base=A_BASE + 6 * gidx, hkv=2, grp=1, max_dist=window // dil,
                            out_dtype=F32, name=f"attn_a{gidx}_fwd")
        outs.append(o_g)
        lses.append(l_g)
    o_a, l_a = merge_groups(outs, lses, "attn_a_merge")
    o_b, l_b = band_fwd(qkv, sinks, r=1, base=B_BASE, hkv=2, grp=2, max_dist=HEAD - 1, out_dtype=BF16,
                        name="attn_b_fwd")
    mem_n, mkv = mem_kv(mem, gains["mem_norm"], w["w_mem_kv"], "mem_kv")
    o_m, l_m = mem_fwd(qkv, mkv, "attn_m_fwd")
    merged = gate_merge(gt, o_a, o_b, o_m, w["w_o_a"], w["w_o_b"], w["w_o_m"], "gate_merge")
    mo, h2 = mm_norm_res(merged, w["w_out"], h1, gains["mix_norm_post"], 1.0, "mix_out")
    w.update(weights_of("ffn2", mo))
    xn2, gu2, a2 = ffn_in(h2, gains["ffn2_norm_pre"], w["ffn2_w_in"], "ffn2_in")
    f2, dy, sq = mm_norm_res(a2, w["ffn2_w_out"], h2, gains["ffn2_norm_post"], 0.5, "ffn2_out", target=target)

    grads = {}

    def ffn_bwd(tag, dh_out, f, gu, a, xn, h_in, after):
        df, dgu, dh_in, grads[f"{tag}_norm_pre"], grads[f"{tag}_norm_post"] = ffn_tokens_bwd(
            dh_out, f, h_in, gu, gains[f"{tag}_norm_pre"], gains[f"{tag}_norm_post"], w[f"{tag}_w_in"],
            w[f"{tag}_w_out"], 0.5, f"{tag}_tokens_bwd", after)
        sent = send_grads(f"{tag}_out", {f"{tag}_w_out": mm_tn(a, df, FF_T, D_MODEL, f"{tag}_w_out_grad")})
        sent = send_grads(f"{tag}_in", {f"{tag}_w_in": mm_tn(
            xn, dgu, D_MODEL, FF_T, f"{tag}_w_in_grad", shard_major=True, perm=_ffn_perm, after=sent)})
        return dh_in, sent

    dh2, sent = ffn_bwd("ffn2", dy, f2, gu2, a2, xn2, h2, dy)

    mix = {}
    dmo, dmerged, grads["mix_norm_post"] = mix_out_bwd(
        dh2, mo, gains["mix_norm_post"], w["w_out"], "mix_out_bwd", sent)
    mix["w_out"] = mm_tn(merged, dmo, D_MODEL, D_MODEL, "w_out_grad")
    dgt, dpa, dpb, dpm, do_a, do_b, do_m, grads["b_gate"] = gate_merge_bwd(
        dmerged, gt, o_a, o_b, o_m, w["w_o_a"], w["w_o_b"], w["w_o_m"], "gate_merge_bwd")
    mix["w_o_a"] = mm_tn(o_a, dpa, o_a.shape[1], D_MODEL, "w_o_a_grad")
    mix["w_o_b"] = mm_tn(o_b, dpb, o_b.shape[1], D_MODEL, "w_o_b_grad")
    mix["w_o_m"] = mm_tn(o_m, dpm, o_m.shape[1], D_MODEL, "w_o_m_grad")

    dqkv = lax.empty(qkv.shape, qkv.dtype)
    for gidx, (window, dil) in enumerate(DIL):
        dqkv, = band_bwd(qkv, dqkv, do_a, o_a, l_a, cos, sin_signed, None, r=dil, base=A_BASE + 6 * gidx, hkv=2,
                         grp=1, max_dist=window // dil, name=f"attn_a{gidx}_bwd")
    dqkv, dsink = band_bwd(qkv, dqkv, do_b, o_b, l_b, cos, sin_signed, sinks, r=1, base=B_BASE, hkv=2, grp=2,
                           max_dist=HEAD - 1, name="attn_b_bwd")
    grads["sinks"] = -dsink[:, ::8, 0].reshape(1, 4)
    dqkv, dmk, dmv = mem_bwd(qkv, dqkv, mkv, do_m, o_m, l_m, "attn_m_bwd")
    mix["w_mem_kv"], grads["mem_norm"] = mem_kv_bwd(
        mem, gains["mem_norm"], mem_n, w["w_mem_kv"], jnp.concatenate([dmk, dmv], axis=1), "mem_kv_bwd")

    mix["w_in"] = mm_tn(u, dqkv, D_MODEL, 1280, "w_in_grad")
    mix["w_gate"] = mm_tn(u, dgt, D_MODEL, 1536, "w_gate_grad", shard_major=True, slabs=2)
    sent = send_grads("mix", mix)
    dh1, grads["mix_norm_pre"] = mm_nt_norm_bwd(
        [(dqkv, w["w_in"]), (dgt, w["w_gate"])], h1, dh2, gains["mix_norm_pre"], "mix_in_bwd", sent)

    dx, _ = ffn_bwd("ffn1", dh1, f1, gu1, a1, xn1, x, dh1)
    return sq, dx, grads


def _place():
    return lax.axis_index("x"), lax.axis_index("y"), lax.axis_index("c")


def _other_chips(x, y):
    return [(1 - x, y), (x, 1 - y), (1 - x, 1 - y)]


def _hbm(n):
    return [pl.BlockSpec(memory_space=pltpu.HBM)] * n


SEM = pl.BlockSpec(memory_space=pltpu.SEMAPHORE)
SIDE_EFFECT = pltpu.SideEffectType.DATAFLOW_SIDE_EFFECTING


def _chip_copy(src, land, sems, i, j, dst_slot, scatter):
    x, y, c = _place()
    px, py = _other_chips(x, y)[j]
    send_sems, recv_sems = sems
    return pltpu.make_async_remote_copy(
        src_ref=src[i].at[2 * px + py] if scatter else src[i], dst_ref=land[i].at[dst_slot],
        send_sem=send_sems.at[3 * i + j], recv_sem=recv_sems.at[3 * i + j],
        device_id=(px, py, c), device_id_type=MESH)


def chip_copies_start(srcs, lands, groups, scatter, name):
    n = len(srcs)

    def body(*refs):
        src, land = refs[:n], refs[n:2 * n]
        sems = refs[2 * n:2 * n + 2 * len(groups)]
        token = refs[-1]
        x, y, _ = _place()
        for g, members in enumerate(groups):
            part = ([src[i] for i in members], [land[i] for i in members])
            for t in range(len(members)):
                for j in range(3):
                    _chip_copy(*part, sems[2 * g:2 * g + 2], t, j, 2 * x + y, scatter).start()
        token[...] = jnp.zeros_like(token)

    sem_shapes = [pltpu.SemaphoreType.DMA((3 * len(m),)) for m in groups for _ in range(2)]
    thru = [pltpu.HBM(a.shape, a.dtype) for a in (*srcs, *lands)]
    res = pl.pallas_call(
        body, name=name,
        out_shape=(*sem_shapes, *thru, jax.ShapeDtypeStruct((8, 128), F32)),
        in_specs=_hbm(2 * n),
        out_specs=(*[SEM] * len(sem_shapes), *_hbm(2 * n), pl.BlockSpec(memory_space=pltpu.VMEM)),
        input_output_aliases={i: len(sem_shapes) + i for i in range(2 * n)},
        compiler_params=pltpu.CompilerParams(has_side_effects=SIDE_EFFECT),
    )(*[pltpu.with_memory_space_constraint(a, pltpu.HBM) for a in (*srcs, *lands)])
    k = len(sem_shapes)
    sems = [tuple(res[2 * g:2 * g + 2]) for g in range(len(groups))]
    return sems, list(res[k:k + n]), list(res[k + n:k + 2 * n]), res[-1]


def chip_copies_wait(srcs, lands, sems, after, scatter, name):
    n = len(srcs)

    def body(*refs):
        src, land = refs[:n], refs[n:2 * n]
        pair = refs[2 * n:2 * n + 2]
        x, y, _ = _place()
        for i in range(n):
            for j, (px, py) in enumerate(_other_chips(x, y)):
                copy = _chip_copy(src, land, pair, i, j, 2 * px + py, scatter)
                copy.wait_send()
                copy.wait_recv()

    res = pl.pallas_call(
        body, name=name,
        out_shape=[pltpu.HBM(a.shape, a.dtype) for a in (*srcs, *lands)],
        in_specs=[*_hbm(2 * n), SEM, SEM, pl.BlockSpec(memory_space=pl.ANY)],
        out_specs=_hbm(2 * n),
        input_output_aliases={i: i for i in range(2 * n)},
        compiler_params=pltpu.CompilerParams(has_side_effects=SIDE_EFFECT),
    )(*srcs, *lands, *sems, after)
    return list(res[n:])


def small_all_gather(small, name):
    flips = [(fx, fy, fc) for fx in (0, 1) for fy in (0, 1) for fc in (0, 1)][1:]

    def body(in_ref, out_ref, send_sems, recv_sems, local_sem):
        x, y, c = _place()
        me = 4 * x + 2 * y + c

        def copy(k, slot):
            fx, fy, fc = flips[k]
            return pltpu.make_async_remote_copy(
                src_ref=in_ref, dst_ref=out_ref.at[slot], send_sem=send_sems.at[k], recv_sem=recv_sems.at[k],
                device_id=(x ^ fx, y ^ fy, c ^ fc), device_id_type=MESH)

        local = pltpu.make_async_copy(in_ref, out_ref.at[me], local_sem)
        local.start()
        for k in range(len(flips)):
            copy(k, me).start()
        for k, (fx, fy, fc) in enumerate(flips):
            copy(k, 4 * (x ^ fx) + 2 * (y ^ fy) + (c ^ fc)).wait()
        local.wait()

    return pl.pallas_call(
        body, name=name, in_specs=_hbm(1), out_specs=_hbm(1)[0],
        out_shape=jax.ShapeDtypeStruct((N_DEV,) + small.shape, small.dtype),
        scratch_shapes=[pltpu.SemaphoreType.DMA((len(flips),)), pltpu.SemaphoreType.DMA((len(flips),)),
                        pltpu.SemaphoreType.DMA],
    )(small)


def sibling_exchange(parts, name):
    n = len(parts)

    def body(*refs):
        ins, outs = refs[:n], refs[n:2 * n]
        send_sems, recv_sems = refs[2 * n:]
        x, y, c = _place()
        copies = [pltpu.make_async_remote_copy(
            src_ref=ins[i], dst_ref=outs[i], send_sem=send_sems.at[i], recv_sem=recv_sems.at[i],
            device_id=(x, y, 1 - c), device_id_type=MESH) for i in range(n)]
        for cp in copies:
            cp.start()
        for cp in copies:
            cp.wait()

    return pl.pallas_call(
        body, name=name, in_specs=_hbm(n), out_specs=_hbm(n),
        out_shape=[jax.ShapeDtypeStruct(p.shape, p.dtype) for p in parts],
        scratch_shapes=[pltpu.SemaphoreType.DMA((n,)), pltpu.SemaphoreType.DMA((n,))],
    )(*parts)


def _row_tile(rows):
    for t in (256, 176, 128, 64, 32, 16, 8):
        if rows % t == 0:
            return t
    return rows


def chip_partial_sum(me, own_sm, recv, name):
    _, rows, cols = own_sm.shape
    tr = _row_tile(rows)

    def body(me_ref, own_ref, r0, r1, r2, r3, o_ref):
        acc = jnp.zeros((tr, cols), F32)
        for s, r_ref in enumerate((r0, r1, r2, r3)):
            acc = acc + jnp.where(me_ref[0] == s, own_ref[...], r_ref[...].astype(F32))
        o_ref[...] = acc

    def slot(s):
        return pl.BlockSpec((None, tr, cols), lambda i, me_ref, s=s: (s, i, 0))

    return pl.pallas_call(
        body, name=name,
        grid_spec=pltpu.PrefetchScalarGridSpec(
            num_scalar_prefetch=1, grid=(rows // tr,),
            in_specs=[pl.BlockSpec((None, tr, cols), lambda i, me_ref: (me_ref[0], i, 0))] + [slot(s) for s in range(4)],
            out_specs=pl.BlockSpec((tr, cols), lambda i, me_ref: (i, 0))),
        out_shape=jax.ShapeDtypeStruct((rows, cols), F32),
        compiler_params=_params("parallel"),
    )(me, own_sm, recv, recv, recv, recv)


def _adamw(w, g, m, v):
    m = ADAM_B1 * m + (1.0 - ADAM_B1) * g
    v = ADAM_B2 * v + (1.0 - ADAM_B2) * (g * g)
    m_hat = m / (1.0 - ADAM_B1 ** ADAM_STEP)
    v_hat = v / (1.0 - ADAM_B2 ** ADAM_STEP)
    delta = -ADAM_LR * (m_hat / (jnp.sqrt(v_hat) + ADAM_EPS) + ADAM_WD * w)
    return delta, m, v


def adamw_pair(part, sib, w, m, v, name):
    rows, cols = w.shape
    tr = _row_tile(rows)

    def body(p_ref, s_ref, w_ref, m_ref, v_ref, g_ref, d_ref, nm_ref, nv_ref):
        g = p_ref[...] + s_ref[...]
        g_ref[...] = g
        d_ref[...], nm_ref[...], nv_ref[...] = _adamw(w_ref[...], g, m_ref[...], v_ref[...])

    spec = pl.BlockSpec((tr, cols), lambda i: (i, 0))
    return pl.pallas_call(
        body, name=name, grid=(rows // tr,), in_specs=[spec] * 5, out_specs=[spec] * 4,
        out_shape=[jax.ShapeDtypeStruct((rows, cols), F32)] * 4,
        compiler_params=_params("parallel"),
    )(part, sib, w, m, v)


def adamw_small(g_all, w, m, v, name):
    def body(ga_ref, w_ref, m_ref, v_ref, g_ref, d_ref, nm_ref, nv_ref):
        g = ga_ref[0]
        for k in range(1, N_DEV):
            g = g + ga_ref[k]
        g_ref[...] = g
        d_ref[...], nm_ref[...], nv_ref[...] = _adamw(w_ref[...], g, m_ref[...], v_ref[...])

    return pl.pallas_call(
        body, name=name, out_shape=[jax.ShapeDtypeStruct(w.shape, F32)] * 4,
    )(g_all, w, m, v)


WEIGHTS = ("ffn1_norm_pre", "ffn1_w_in", "ffn1_w_out", "ffn1_norm_post", "mix_norm_pre", "w_in", "sinks",
           "mem_norm", "w_mem_kv", "w_gate", "b_gate", "w_o_a", "w_o_b", "w_o_m", "w_out", "mix_norm_post",
           "ffn2_norm_pre", "ffn2_w_in", "ffn2_w_out", "ffn2_norm_post")
BIG = ("ffn1_w_in", "ffn1_w_out", "w_in", "w_mem_kv", "w_gate", "w_o_a", "w_o_b", "w_o_m", "w_out",
       "ffn2_w_in", "ffn2_w_out")
GATHER_ORDER = ("ffn1_in", "ffn1_out", "mix", "ffn2")
GATHER_GROUPS = {"ffn1_in": ("ffn1_w_in",), "ffn1_out": ("ffn1_w_out",),
                 "mix": ("w_in", "w_gate", "w_mem_kv", "w_o_a", "w_o_b", "w_o_m", "w_out"),
                 "ffn2": ("ffn2_w_in", "ffn2_w_out")}
GROUPS = {"ffn1_in": ("ffn1_w_in",), "ffn1_out": ("ffn1_w_out",),
          "mix": ("w_in", "w_gate", "w_mem_kv", "w_o_a", "w_o_b", "w_o_m", "w_out"),
          "ffn2_in": ("ffn2_w_in",), "ffn2_out": ("ffn2_w_out",)}
COLUMN_SHARDED = ("ffn1_w_in", "ffn2_w_in", "w_in", "w_gate", "w_o_a", "w_o_b", "w_o_m")
KEPT_SHARD_MAJOR = ("ffn1_w_in", "ffn2_w_in", "w_gate")
GAINS = ("ffn1_norm_pre", "ffn1_norm_post", "mix_norm_pre", "mem_norm", "mix_norm_post", "ffn2_norm_pre",
         "ffn2_norm_post")
SMALL_ROWS = 16


def _pack_small(t):
    sinks = jnp.pad(t["sinks"], ((0, 0), (0, D_MODEL - t["sinks"].shape[1])))
    rows = [t[k] for k in GAINS] + [t["b_gate"].reshape(3, D_MODEL), sinks]
    packed = jnp.concatenate(rows, axis=0)
    return jnp.pad(packed, ((0, SMALL_ROWS - packed.shape[0]), (0, 0)))


def _unpack_small(p):
    out = {k: p[i:i + 1] for i, k in enumerate(GAINS)}
    out["b_gate"] = p[7:10].reshape(1, 3 * D_MODEL)
    out["sinks"] = p[10:11, :4]
    return out


def kernel(x, mem, ffn1_norm_pre, ffn1_w_in, ffn1_w_out, ffn1_norm_post, mix_norm_pre, w_in, sinks, mem_norm, w_mem_kv, w_gate, b_gate, w_o_a, w_o_b, w_o_m, w_out, mix_norm_post, ffn2_norm_pre, ffn2_w_in, ffn2_w_out, ffn2_norm_post, loss_target, m_ffn1_norm_pre, m_ffn1_w_in, m_ffn1_w_out, m_ffn1_norm_post, m_mix_norm_pre, m_w_in, m_sinks, m_mem_norm, m_w_mem_kv, m_w_gate, m_b_gate, m_w_o_a, m_w_o_b, m_w_o_m, m_w_out, m_mix_norm_post, m_ffn2_norm_pre, m_ffn2_w_in, m_ffn2_w_out, m_ffn2_norm_post, v_ffn1_norm_pre, v_ffn1_w_in, v_ffn1_w_out, v_ffn1_norm_post, v_mix_norm_pre, v_w_in, v_sinks, v_mem_norm, v_w_mem_kv, v_w_gate, v_b_gate, v_w_o_a, v_w_o_b, v_w_o_m, v_w_out, v_mix_norm_post, v_ffn2_norm_pre, v_ffn2_w_in, v_ffn2_w_out, v_ffn2_norm_post):
    given = dict(locals())
    wt = {k: given[k] for k in WEIGHTS}
    mom = {k: given["m_" + k] for k in WEIGHTS}
    var = {k: given["v_" + k] for k in WEIGHTS}
    chip = (2 * lax.axis_index("x") + lax.axis_index("y")).astype(jnp.int32)
    me = chip.reshape(1)

    def landing_zone(own):
        return lax.dynamic_update_slice_in_dim(lax.empty((N_CHIPS,) + own.shape, own.dtype), own[None], chip, 0)

    shards = [wt[k][0].astype(BF16) for k in BIG]
    members = [[BIG.index(k) for k in GATHER_GROUPS[g]] for g in GATHER_ORDER]
    sems, shards, lands, token = chip_copies_start(
        shards, [landing_zone(s) for s in shards], members, False, "weight_gather_start")

    def weights_of(group, after):
        idx = members[GATHER_ORDER.index(group)]
        got = chip_copies_wait([shards[i] for i in idx], [lands[i] for i in idx], sems[GATHER_ORDER.index(group)],
                               after, False, f"weight_gather_wait_{group}")
        full = {}
        for k, g in zip(GATHER_GROUPS[group], got):
            if k in COLUMN_SHARDED:
                if k in ("ffn1_w_in", "ffn2_w_in"):
                    g = jnp.stack([g[0], g[2], g[1], g[3]])
                full[k] = jnp.swapaxes(g, 0, 1).reshape(g.shape[1], N_CHIPS * g.shape[2])
                if k == "w_in":
                    full[k] = to_kernel_heads(full[k])
            else:
                full[k] = g.reshape(N_CHIPS * g.shape[1], g.shape[2])
        return full

    in_flight = {}

    def send_grads(group, grads):
        own, wire = [], []
        for k in GROUPS[group]:
            g = from_kernel_heads(grads[k]) if k == "w_in" else grads[k]
            if k in KEPT_SHARD_MAJOR:
                pass
            elif k in COLUMN_SHARDED:
                g = jnp.swapaxes(g.reshape(g.shape[0], N_CHIPS, g.shape[1] // N_CHIPS), 0, 1)
            else:
                g = g.reshape(N_CHIPS, g.shape[0] // N_CHIPS, g.shape[1])
            own.append(g)
            wire.append(g.astype(BF16))
        zones = [landing_zone(lax.dynamic_index_in_dim(b, chip, 0, keepdims=False)) for b in wire]
        pair, wire, zones, sent = chip_copies_start(
            wire, zones, [list(range(len(wire)))], True, f"grad_scatter_start_{group}")
        in_flight[group] = (own, wire, zones, pair[0], sent)
        return sent

    gains = {k: wt[k] for k in GAINS}
    sq, dx, grads = layer_step(
        x[0], mem[0], loss_target[0], gains, sinks[0], b_gate, weights_of, send_grads, token[0, 0])
    loss = lax.psum(0.5 * sq[0, 0] / D_MODEL, ("x", "y", "c"))

    res = {}
    after = in_flight["ffn1_in"][4]
    for stage in (("ffn2_out", "ffn2_in", "mix", "ffn1_out"), ("ffn1_in",)):
        names, parts = [], []
        for group in stage:
            own, wire, zones, pair, _ = in_flight[group]
            received = chip_copies_wait(wire, zones, pair, after, True, f"grad_scatter_wait_{group}")
            for k, g, r in zip(GROUPS[group], own, received):
                names.append(k)
                parts.append(chip_partial_sum(me, g, r, f"{k}_chip_sum"))
        sibs = sibling_exchange(parts, f"sibling_exchange_{stage[-1]}")
        for k, p, s in zip(names, parts, sibs):
            res[k] = [t[None] for t in adamw_pair(p, s, wt[k][0], mom[k][0], var[k][0], f"{k}_adamw")]
        after = res[names[-1]][0]
    small_all = small_all_gather(_pack_small(grads), "small_grad_gather")
    packed = adamw_small(small_all, _pack_small(wt), _pack_small(mom), _pack_small(var), "small_adamw")
    for idx, p in enumerate(packed):
        for k, t in _unpack_small(p).items():
            res.setdefault(k, [None] * 4)[idx] = t

    return (loss, dx[None], *[res[k][0] for k in WEIGHTS], *[res[k][1] for k in WEIGHTS],
            *[res[k][2] for k in WEIGHTS], *[res[k][3] for k in WEIGHTS])
```

```python
import functools

import jax
import jax.numpy as jnp
from jax import lax
from jax.experimental import pallas as pl
from jax.experimental.pallas import tpu as pltpu

F32 = jnp.float32
BF16 = jnp.bfloat16

D_MODEL = 1024
D_FF = 2816
HEAD = 128
N_CHIPS = 4
N_DEV = 8
EPS = 1e-6
NEG_INF = -1e30
ROPE_THETA = 10000.0
ATT_SCALE = HEAD ** -0.5

ADAM_LR = 0.001
ADAM_B1 = 0.9
ADAM_B2 = 0.999
ADAM_EPS = 1e-08
ADAM_WD = 0.01
ADAM_STEP = 10

VMEM_LIMIT = 52 * 2 ** 20
MESH = pl.DeviceIdType.MESH

QKV_W = 3840
DIL = ((128, 1), (512, 4), (2048, 16))
B_BASE, MQ, A_BASE = 0, 8, 12
_AQ, _AK, _AV, _BQ, _BK, _BV, _MQ = 0, 6, 12, 18, 22, 24, 26
HEAD_ORDER = tuple(
    [h for j in range(2) for h in (_BQ + 2 * j, _BQ + 2 * j + 1, _BK + j, _BV + j)]
    + [_MQ + i for i in range(4)]
    + [h for g in range(3) for i in range(2) for h in (_AQ + 2 * g + i, _AK + 2 * g + i, _AV + 2 * g + i)])
ROTARY_HEADS = tuple(p for p, h in enumerate(HEAD_ORDER) if h < _AV or _BQ <= h < _BV)


def to_kernel_heads(w):
    return jnp.concatenate([w[..., h * HEAD:(h + 1) * HEAD] for h in HEAD_ORDER], axis=-1)


def from_kernel_heads(w):
    place = {h: p for p, h in enumerate(HEAD_ORDER)}
    return jnp.concatenate([w[..., place[h] * HEAD:(place[h] + 1) * HEAD] for h in range(len(HEAD_ORDER))], axis=-1)

TM = 512
FF_T = D_FF // 2


def _params(*sem):
    return pltpu.CompilerParams(dimension_semantics=sem, vmem_limit_bytes=VMEM_LIMIT)


def _dot(a, b):
    return jnp.dot(a, b, preferred_element_type=F32)


def _dot_nt(a, b):
    return lax.dot_general(a, b, (((1,), (1,)), ((), ())), preferred_element_type=F32)


def _dot_tn(a, b):
    return lax.dot_general(a, b, (((0,), (0,)), ((), ())), preferred_element_type=F32)


def _rstd(x):
    return lax.rsqrt(jnp.mean(x * x, axis=-1, keepdims=True) + EPS)


def _sigmoid(x):
    return 0.5 * jnp.tanh(0.5 * x) + 0.5


def _ffn_perm(k):
    return (k % 2) * 2 + k // 2


UNREAD = pl.BlockSpec(memory_space=pl.ANY)


def _resident(arr):
    return pl.BlockSpec(arr.shape, lambda *_: (0,) * arr.ndim, pipeline_mode=pl.Buffered(1))


def ffn_in(h, g, w, name):
    T, D = h.shape

    def body(h_ref, g_ref, w_ref, xn_ref, gu_ref, a_ref):
        x = h_ref[...]
        xn = (x * _rstd(x) * g_ref[...]).astype(BF16)
        xn_ref[...] = xn
        for j in range(2):
            gu = _dot(xn, w_ref[:, j * 2 * FF_T:(j + 1) * 2 * FF_T])
            gu_ref[:, j * 2 * FF_T:(j + 1) * 2 * FF_T] = gu.astype(BF16)
            gate, up = gu[:, :FF_T], gu[:, FF_T:]
            a_ref[:, j * FF_T:(j + 1) * FF_T] = (gate * _sigmoid(gate) * up).astype(BF16)

    def rows(width):
        return pl.BlockSpec((TM, width), lambda i: (i, 0))

    return pl.pallas_call(
        body, name=name,
        grid=(T // TM,),
        in_specs=[rows(D), _resident(g), _resident(w)],
        out_specs=[rows(D), rows(2 * D_FF), rows(D_FF)],
        out_shape=[jax.ShapeDtypeStruct((T, D), BF16),
                   jax.ShapeDtypeStruct((T, 2 * D_FF), BF16),
                   jax.ShapeDtypeStruct((T, D_FF), BF16)],
        compiler_params=_params("parallel"),
    )(h, g, w)


def mm_norm_res(a, w, h_in, g, coef, name, target=None):
    T, K = a.shape
    D = w.shape[1]
    final = target is not None

    def body(*refs):
        if final:
            a_ref, w_ref, h_ref, g_ref, t_ref, f_ref, o_ref, l_ref = refs
        else:
            a_ref, w_ref, h_ref, g_ref, f_ref, o_ref = refs
        f = _dot(a_ref[...], w_ref[...])
        f_ref[...] = f
        y = h_ref[...] + coef * (f * _rstd(f) * g_ref[...])
        if final:
            err = y - t_ref[...]
            o_ref[...] = err * (1.0 / D)

            @pl.when(pl.program_id(0) == 0)
            def _():
                l_ref[...] = jnp.zeros_like(l_ref)

            l_ref[...] += jnp.sum(err * err)
        else:
            o_ref[...] = y

    row = pl.BlockSpec((TM, D), lambda i: (i, 0))
    in_specs = [pl.BlockSpec((TM, K), lambda i: (i, 0)),
                _resident(w),
                row, pl.BlockSpec((1, D), lambda i: (0, 0))]
    out_specs = [row, row]
    out_shape = [jax.ShapeDtypeStruct((T, D), F32), jax.ShapeDtypeStruct((T, D), F32)]
    args = [a, w, h_in, g]
    if final:
        in_specs.append(row)
        args.append(target)
        out_specs.append(pl.BlockSpec((8, 128), lambda i: (0, 0)))
        out_shape.append(jax.ShapeDtypeStruct((8, 128), F32))
    return pl.pallas_call(
        body, name=name, grid=(T // TM,), in_specs=in_specs, out_specs=out_specs, out_shape=out_shape,
        compiler_params=_params("arbitrary"),
    )(*args)


def _rope(x, cos, sin_signed):
    return x * cos + pltpu.roll(x, HEAD // 2, axis=1) * sin_signed


def _unrope(x, cos, sin_signed):
    return x * cos - pltpu.roll(x, HEAD // 2, axis=1) * sin_signed


def mix_in(h, g, w, w_gate, b_gate, cos, sin_signed, name):
    T, D = h.shape
    tn = 768

    def body(h_ref, g_ref, w_ref, wg_ref, b_ref, c_ref, s_ref, u_ref, o_ref, gt_ref):
        x = h_ref[...]
        u = (x * _rstd(x) * g_ref[...]).astype(BF16)
        u_ref[...] = u
        c, s = c_ref[...], s_ref[...]
        for j in range(QKV_W // tn):
            acc = _dot(u, w_ref[:, j * tn:(j + 1) * tn])
            for hd in range(tn // HEAD):
                head = j * (tn // HEAD) + hd
                part = acc[:, hd * HEAD:(hd + 1) * HEAD]
                if head in ROTARY_HEADS:
                    part = _rope(part, c, s)
                o_ref[:, head * HEAD:(head + 1) * HEAD] = part.astype(BF16)
        for j in range(w_gate.shape[1] // tn):
            cols = slice(j * tn, (j + 1) * tn)
            gt_ref[:, cols] = _sigmoid(_dot(u, wg_ref[:, cols]) + b_ref[:, cols]).astype(BF16)

    def rows(width):
        return pl.BlockSpec((TM, width), lambda i: (i, 0))

    return pl.pallas_call(
        body, name=name,
        grid=(T // TM,),
        in_specs=[rows(D), _resident(g), _resident(w), _resident(w_gate), _resident(b_gate), rows(HEAD), rows(HEAD)],
        out_specs=[rows(D), rows(QKV_W), rows(w_gate.shape[1])],
        out_shape=[jax.ShapeDtypeStruct((T, D), BF16), jax.ShapeDtypeStruct((T, QKV_W), BF16),
                   jax.ShapeDtypeStruct((T, w_gate.shape[1]), BF16)],
        compiler_params=_params("parallel"),
    )(h, g, w, w_gate, b_gate, cos, sin_signed)


def gate_merge(gt, o_a, o_b, o_m, w_a, w_b, w_m, name):
    T = gt.shape[0]
    D = D_MODEL

    def body(gt_ref, oa_ref, ob_ref, om_ref, wa_ref, wb_ref, wm_ref, out_ref):
        acc = gt_ref[:, :D].astype(F32) * _dot(oa_ref[...], wa_ref[...])
        acc += gt_ref[:, D:2 * D].astype(F32) * _dot(ob_ref[...], wb_ref[...])
        acc += gt_ref[:, 2 * D:].astype(F32) * _dot(om_ref[...], wm_ref[...])
        out_ref[...] = acc.astype(BF16)

    def rows(width):
        return pl.BlockSpec((TM, width), lambda i: (i, 0))

    def whole(arr):
        return pl.BlockSpec(arr.shape, lambda i: (0, 0))

    return pl.pallas_call(
        body, name=name, grid=(T // TM,),
        in_specs=[rows(3 * D), rows(o_a.shape[1]), rows(o_b.shape[1]), rows(o_m.shape[1]),
                  whole(w_a), whole(w_b), whole(w_m)],
        out_specs=rows(D),
        out_shape=jax.ShapeDtypeStruct((T, D), BF16),
        compiler_params=_params("parallel"),
    )(gt, o_a, o_b, o_m, w_a, w_b, w_m)


def _band_rows(start, r):
    return pl.ds(start, HEAD) if r == 1 else pl.ds(start, HEAD, stride=r)


def _band_mask(max_dist, first_has_prev):
    row = lax.broadcasted_iota(jnp.int32, (HEAD, 2 * HEAD), 0)
    col = lax.broadcasted_iota(jnp.int32, (HEAD, 2 * HEAD), 1)
    dist = row + HEAD - col
    band = (dist >= 0) & (dist <= max_dist)
    return band, band & (col >= jnp.where(first_has_prev, 0, HEAD))


def _stack(parts):
    return parts[0] if len(parts) == 1 else jnp.concatenate(parts, axis=0)


def _band_specs(BT, SB, nsub, base, grp):
    stride = grp + 2

    def cur(off, width):
        return pl.BlockSpec((BT, width * HEAD), lambda h, i: (i, (base + h * stride + off) // width))

    def prev(off):
        return pl.BlockSpec((SB, HEAD), lambda h, i: (jnp.maximum(i * nsub - 1, 0), base + h * stride + off))

    return cur(0, grp), cur(grp, 1), prev(grp), cur(grp + 1, 1), prev(grp + 1)


def band_fwd(qkv, sinks, *, r, base, hkv, grp, max_dist, out_dtype, name):
    T, W = qkv.shape
    SB = HEAD * r
    BT = min(2048, T)
    nsub, nib = BT // SB, T // BT
    hq = hkv * grp
    heads = [slice(g * HEAD, (g + 1) * HEAD) for g in range(grp)]

    def body(sink_ref, q_ref, kc_ref, kp_ref, vc_ref, vp_ref, o_ref, l_ref, qf, kf, vf):
        kvh, ib = pl.program_id(0), pl.program_id(1)
        qf[...] = q_ref[...].astype(F32)
        kf[:SB] = kp_ref[...].astype(F32)
        kf[SB:] = kc_ref[...].astype(F32)
        vf[:SB] = vp_ref[...].astype(F32)
        vf[SB:] = vc_ref[...].astype(F32)
        band, band_first = _band_mask(max_dist, ib > 0)
        for c in range(r):
            k_old, v_old = kf[_band_rows(c, r)], vf[_band_rows(c, r)]
            for j in range(nsub):
                mask = band_first if j == 0 else band
                rows = _band_rows(j * SB + c, r)
                k_own, v_own = kf[_band_rows((j + 1) * SB + c, r)], vf[_band_rows((j + 1) * SB + c, r)]
                kcat = jnp.concatenate([k_old, k_own], axis=0).astype(BF16)
                vcat = jnp.concatenate([v_old, v_own], axis=0).astype(BF16)
                k_old, v_old = k_own, v_own
                s_all = _dot_nt(_stack([qf[rows, cols] for cols in heads]).astype(BF16), kcat) * ATT_SCALE
                probs, tots = [], []
                for g, cols in enumerate(heads):
                    s = jnp.where(mask, s_all[cols], NEG_INF)
                    sk = sink_ref[kvh * grp + g]
                    m = jnp.maximum(jnp.max(s, axis=-1, keepdims=True), sk)
                    p = jnp.exp(s - m)
                    tot = jnp.sum(p, axis=-1, keepdims=True) + jnp.exp(sk - m)
                    probs.append(p.astype(BF16))
                    tots.append(tot)
                    l_ref[rows, cols] = jnp.broadcast_to(m + jnp.log(tot), (HEAD, HEAD))
                o_all = _dot(_stack(probs), vcat)
                for g, cols in enumerate(heads):
                    o_ref[rows, cols] = (o_all[cols] / tots[g]).astype(out_dtype)

    out_spec = pl.BlockSpec((BT, grp * HEAD), lambda h, i: (i, h))
    return pl.pallas_call(
        body, name=name, grid=(hkv, nib),
        in_specs=[pl.BlockSpec(memory_space=pltpu.SMEM), *_band_specs(BT, SB, nsub, base, grp)],
        out_specs=[out_spec, out_spec],
        out_shape=[jax.ShapeDtypeStruct((T, hq * HEAD), out_dtype), jax.ShapeDtypeStruct((T, hq * HEAD), F32)],
        scratch_shapes=[pltpu.VMEM((BT, grp * HEAD), F32), pltpu.VMEM((SB + BT, HEAD), F32),
                        pltpu.VMEM((SB + BT, HEAD), F32)],
        compiler_params=_params("parallel", "arbitrary"),
    )(sinks, qkv, qkv, qkv, qkv, qkv)


def band_bwd(qkv, dqkv, do, o, lse, cos, sin_signed, sinks, *, r, base, hkv, grp, max_dist, name):
    T, W = qkv.shape
    SB = HEAD * r
    BT = min(2048, T)
    nsub, nib = BT // SB, T // BT
    nblk = T // SB
    with_sink = sinks is not None
    heads = [slice(g * HEAD, (g + 1) * HEAD) for g in range(grp)]

    def body(*refs):
        if with_sink:
            sink_ref, refs = refs[0], refs[1:]
        (q_ref, kc_ref, kp_ref, vc_ref, vp_ref, qn_ref, do_ref, don_ref, o_ref, on_ref, l_ref, ln_ref,
         c_ref, s_ref, _) = refs[:15]
        out_ref = refs[15]
        ds_ref = refs[16] if with_sink else None
        qf, dof, of, kf, vf, dqf, dkf, dvf = refs[-8:]
        kvh, ib = pl.program_id(0), pl.program_id(1)
        for buf, cur_ref, nxt_ref in ((qf, q_ref, qn_ref), (dof, do_ref, don_ref), (of, o_ref, on_ref)):
            buf[:BT] = cur_ref[...].astype(F32)
            buf[BT:] = nxt_ref[...].astype(F32)
        kf[:SB] = kp_ref[...].astype(F32)
        kf[SB:] = kc_ref[...].astype(F32)
        vf[:SB] = vp_ref[...].astype(F32)
        vf[SB:] = vc_ref[...].astype(F32)
        band, band_first = _band_mask(max_dist, ib > 0)
        if with_sink:
            @pl.when(ib == 0)
            def _():
                ds_ref[...] = jnp.zeros_like(ds_ref)

        def grads(rows, logzs, keys, vals, mask):
            q = _stack([qf[rows, cols] for cols in heads]).astype(BF16)
            dout = _stack([dof[rows, cols] for cols in heads]).astype(BF16)
            s_all = _dot_nt(q, keys) * ATT_SCALE
            dp_all = _dot_nt(dout, vals)
            probs, dss, deltas = [], [], []
            for g, cols in enumerate(heads):
                delta = jnp.sum(dof[rows, cols] * of[rows, cols], axis=-1, keepdims=True)
                p = jnp.exp(jnp.where(mask, s_all[cols], NEG_INF) - logzs[g][:, :1])
                probs.append(p.astype(BF16))
                dss.append((p * (dp_all[cols] - delta) * ATT_SCALE).astype(BF16))
                deltas.append(delta)
            return q, dout, _stack(probs), _stack(dss), deltas

        row = lax.broadcasted_iota(jnp.int32, (HEAD, HEAD), 0)
        col = lax.broadcasted_iota(jnp.int32, (HEAD, HEAD), 1)
        reach = col >= row + jnp.where(ib < nib - 1, HEAD - max_dist, 2 * HEAD)
        for c in range(r):
            k_old, v_old = kf[_band_rows(c, r)], vf[_band_rows(c, r)]
            dk_own = dv_own = None
            for j in range(nsub):
                rows = _band_rows(j * SB + c, r)
                k_own, v_own = kf[_band_rows((j + 1) * SB + c, r)], vf[_band_rows((j + 1) * SB + c, r)]
                kcat = jnp.concatenate([k_old, k_own], axis=0).astype(BF16)
                vcat = jnp.concatenate([v_old, v_own], axis=0).astype(BF16)
                logzs = [l_ref[rows, cols] for cols in heads]
                q, dout, p, ds, deltas = grads(rows, logzs, kcat, vcat, band_first if j == 0 else band)
                dq = _dot(ds, kcat)
                for g, cols in enumerate(heads):
                    dqf[rows, cols] = dq[cols]
                    if with_sink:
                        p_sink = jnp.exp(sink_ref[kvh * grp + g] - logzs[g][:, :1])
                        ds_ref[g * 8:(g + 1) * 8] += jnp.sum(p_sink * deltas[g])
                dk, dv = _dot_tn(ds, q), _dot_tn(p, dout)
                if j > 0:
                    done = _band_rows((j - 1) * SB + c, r)
                    dkf[done] = dk_own + dk[:HEAD]
                    dvf[done] = dv_own + dv[:HEAD]
                dk_own, dv_own = dk[HEAD:], dv[HEAD:]
                k_old, v_old = k_own, v_own
            logzs = [ln_ref[_band_rows(c, r), cols] for cols in heads]
            q, dout, p, ds, _ = grads(_band_rows(BT + c, r), logzs, k_old.astype(BF16), v_old.astype(BF16), reach)
            done = _band_rows((nsub - 1) * SB + c, r)
            dkf[done] = dk_own + _dot_tn(ds, q)
            dvf[done] = dv_own + _dot_tn(p, dout)

        cs, sn = c_ref[...], s_ref[...]
        for cols in heads:
            out_ref[:, cols] = _unrope(dqf[:, cols], cs, sn).astype(BF16)
        out_ref[:, grp * HEAD:(grp + 1) * HEAD] = _unrope(dkf[...], cs, sn).astype(BF16)
        out_ref[:, (grp + 1) * HEAD:] = dvf[...].astype(BF16)

    def nxt_row(i):
        return jnp.minimum((i + 1) * nsub, nblk - 1)

    stride = grp + 2
    q_next = pl.BlockSpec((SB, grp * HEAD), lambda h, i: (nxt_row(i), (base + h * stride) // grp))
    head_cur = pl.BlockSpec((BT, grp * HEAD), lambda h, i: (i, h))
    head_next = pl.BlockSpec((SB, grp * HEAD), lambda h, i: (nxt_row(i), h))
    table = pl.BlockSpec((BT, HEAD), lambda h, i: (i, 0))

    in_specs = [*_band_specs(BT, SB, nsub, base, grp), q_next,
                head_cur, head_next, head_cur, head_next, head_cur, head_next, table, table, UNREAD]
    args = [qkv, qkv, qkv, qkv, qkv, qkv, do, do, o, o, lse, lse, cos, sin_signed, dqkv]
    out_specs = [pl.BlockSpec((BT, stride * HEAD), lambda h, i: (i, base // stride + h))]
    out_shape = [jax.ShapeDtypeStruct(dqkv.shape, dqkv.dtype)]
    if with_sink:
        in_specs.insert(0, pl.BlockSpec(memory_space=pltpu.SMEM))
        args.insert(0, sinks)
        out_specs.append(pl.BlockSpec((None, grp * 8, HEAD), lambda h, i: (h, 0, 0)))
        out_shape.append(jax.ShapeDtypeStruct((hkv, grp * 8, HEAD), F32))
    wide = pltpu.VMEM((BT + SB, grp * HEAD), F32)
    tall = pltpu.VMEM((SB + BT, HEAD), F32)
    grad = pltpu.VMEM((BT, HEAD), F32)
    return pl.pallas_call(
        body, name=name, grid=(hkv, nib), in_specs=in_specs, out_specs=out_specs, out_shape=out_shape,
        input_output_aliases={len(args) - 1: 0},
        scratch_shapes=[wide, wide, wide, tall, tall, pltpu.VMEM((BT, grp * HEAD), F32), grad, grad],
        compiler_params=_params("parallel", "arbitrary"),
    )(*args)


def merge_groups(outs, lses, name):
    T, Wd = outs[0].shape
    tm = 1024

    def body(o0, o1, o2, l0, l1, l2, out_ref, lt_ref):
        a, b, c = l0[...], l1[...], l2[...]
        m = jnp.maximum(jnp.maximum(a, b), c)
        wa, wb, wc = jnp.exp(a - m), jnp.exp(b - m), jnp.exp(c - m)
        z = wa + wb + wc
        out_ref[...] = ((wa * o0[...] + wb * o1[...] + wc * o2[...]) / z).astype(BF16)
        lt_ref[...] = m + jnp.log(z)

    spec = pl.BlockSpec((tm, Wd), lambda i: (i, 0))
    return pl.pallas_call(
        body, name=name, grid=(T // tm,), in_specs=[spec] * 6, out_specs=[spec, spec],
        out_shape=[jax.ShapeDtypeStruct((T, Wd), BF16), jax.ShapeDtypeStruct((T, Wd), F32)],
        compiler_params=_params("parallel"),
    )(*outs, *lses)


M_HEADS = 4


def mem_kv(mem, g, w, name):
    n, D = mem.shape

    def body(m_ref, g_ref, w_ref, mn_ref, kv_ref):
        x = m_ref[...]
        mn = (x * _rstd(x) * g_ref[...]).astype(BF16)
        mn_ref[...] = mn
        kv_ref[...] = _dot(mn, w_ref[...]).astype(BF16)

    return pl.pallas_call(
        body, name=name,
        out_shape=[jax.ShapeDtypeStruct((n, D), BF16), jax.ShapeDtypeStruct((n, w.shape[1]), BF16)],
        compiler_params=pltpu.CompilerParams(vmem_limit_bytes=VMEM_LIMIT),
    )(mem, g, w)


def mem_fwd(qkv, mkv, name):
    T = qkv.shape[0]
    n = mkv.shape[0]
    RB = 1024

    def body(q_ref, k_ref, v_ref, o_ref, l_ref):
        s = _dot_nt(q_ref[...], k_ref[...]) * ATT_SCALE
        m = jnp.max(s, axis=-1, keepdims=True)
        p = jnp.exp(s - m)
        den = jnp.sum(p, axis=-1, keepdims=True)
        o_ref[...] = (_dot(p.astype(BF16), v_ref[...]) / den).astype(BF16)
        l_ref[...] = jnp.broadcast_to(m + jnp.log(den), (RB, HEAD))

    out = pl.BlockSpec((RB, HEAD), lambda h, i: (i, h))
    return pl.pallas_call(
        body, name=name, grid=(M_HEADS, T // RB),
        in_specs=[pl.BlockSpec((RB, HEAD), lambda h, i: (i, MQ + h)),
                  pl.BlockSpec((n, HEAD), lambda h, i: (0, h)),
                  pl.BlockSpec((n, HEAD), lambda h, i: (0, M_HEADS + h))],
        out_specs=[out, out],
        out_shape=[jax.ShapeDtypeStruct((T, M_HEADS * HEAD), BF16), jax.ShapeDtypeStruct((T, M_HEADS * HEAD), F32)],
        compiler_params=_params("parallel", "parallel"),
    )(qkv, mkv, mkv)


def mem_bwd(qkv, dqkv, mkv, do, o, lse, name):
    T = qkv.shape[0]
    n = mkv.shape[0]
    RB = 1024

    def body(q_ref, k_ref, v_ref, do_ref, o_ref, l_ref, _, dq_ref, dk_ref, dv_ref):
        @pl.when(pl.program_id(1) == 0)
        def _():
            dk_ref[...] = jnp.zeros_like(dk_ref)
            dv_ref[...] = jnp.zeros_like(dv_ref)

        q, dout = q_ref[...], do_ref[...]
        delta = jnp.sum(dout.astype(F32) * o_ref[...].astype(F32), axis=-1, keepdims=True)
        p = jnp.exp(_dot_nt(q, k_ref[...]) * ATT_SCALE - l_ref[...][:, :1])
        ds = (p * (_dot_nt(dout, v_ref[...]) - delta) * ATT_SCALE).astype(BF16)
        dq_ref[...] = _dot(ds, k_ref[...]).astype(BF16)
        dk_ref[...] += _dot_tn(ds, q)
        dv_ref[...] += _dot_tn(p.astype(BF16), dout)

    tok = pl.BlockSpec((RB, HEAD), lambda h, i: (i, h))
    slot = pl.BlockSpec((n, HEAD), lambda h, i: (0, h))
    return pl.pallas_call(
        body, name=name, grid=(M_HEADS, T // RB),
        in_specs=[pl.BlockSpec((RB, HEAD), lambda h, i: (i, MQ + h)),
                  slot, pl.BlockSpec((n, HEAD), lambda h, i: (0, M_HEADS + h)), tok, tok, tok, UNREAD],
        out_specs=[pl.BlockSpec((RB, HEAD), lambda h, i: (i, MQ + h)), slot, slot],
        out_shape=[jax.ShapeDtypeStruct(dqkv.shape, dqkv.dtype),
                   jax.ShapeDtypeStruct((n, M_HEADS * HEAD), F32),
                   jax.ShapeDtypeStruct((n, M_HEADS * HEAD), F32)],
        input_output_aliases={6: 0},
        compiler_params=_params("parallel", "arbitrary"),
    )(qkv, mkv, mkv, do, o, lse, dqkv)


def mem_kv_bwd(mem, g, mem_n, w, dmkv, name):
    n, D = mem.shape

    def body(m_ref, g_ref, mn_ref, w_ref, d_ref, dw_ref, dg_ref):
        d = d_ref[...].astype(BF16)
        dw_ref[...] = _dot_tn(mn_ref[...], d)
        x = m_ref[...]
        dg_ref[...] = jnp.sum(_dot_nt(d, w_ref[...]) * (x * _rstd(x)), axis=0, keepdims=True)

    return pl.pallas_call(
        body, name=name,
        out_shape=[jax.ShapeDtypeStruct(w.shape, F32), jax.ShapeDtypeStruct((1, D), F32)],
        compiler_params=pltpu.CompilerParams(vmem_limit_bytes=VMEM_LIMIT),
    )(mem, g, mem_n, w, dmkv)


def _rms_bwd(dn, f, g):
    r = _rstd(f)
    fhat = f * r
    dfhat = dn * g
    df = r * (dfhat - fhat * jnp.mean(dfhat * fhat, axis=-1, keepdims=True))
    return df, jnp.sum(dn * fhat, axis=0, keepdims=True)


VMEM_LIMIT_FFN_BWD = 60 * 2 ** 20


def ffn_tokens_bwd(dh, f, h_in, gu, g_pre, g_post, w_in, w_out, coef, name, after):
    T, D = dh.shape

    def body(dh_ref, f_ref, h_ref, gu_ref, gpre_ref, gpost_ref, win_ref, wout_ref, _,
             df_ref, dgu_ref, dhin_ref, dgpre_ref, dgpost_ref, dxn_ref):
        i, j = pl.program_id(0), pl.program_id(1)

        @pl.when(j == 0)
        def _():
            @pl.when(i == 0)
            def _():
                dgpre_ref[...] = jnp.zeros_like(dgpre_ref)
                dgpost_ref[...] = jnp.zeros_like(dgpost_ref)

            df, dg_post = _rms_bwd(coef * dh_ref[...], f_ref[...], gpost_ref[...])
            dgpost_ref[...] += dg_post
            df_ref[...] = df.astype(BF16)

        for jj in range(2):
            @pl.when(j == jj)
            def _(jj=jj):
                lo, mid, hi = 2 * jj * FF_T, (2 * jj + 1) * FF_T, (2 * jj + 2) * FF_T
                da = _dot_nt(df_ref[...], wout_ref[jj * FF_T:(jj + 1) * FF_T, :])
                gate = gu_ref[:, :FF_T].astype(F32)
                up = gu_ref[:, FF_T:].astype(F32)
                sig = _sigmoid(gate)
                dgate = (da * up * sig * (1.0 + gate * (1.0 - sig))).astype(BF16)
                dup = (da * gate * sig).astype(BF16)
                dgu_ref[:, :FF_T] = dgate
                dgu_ref[:, FF_T:] = dup
                part = _dot_nt(dgate, win_ref[:, lo:mid]) + _dot_nt(dup, win_ref[:, mid:hi])
                if jj == 0:
                    dxn_ref[...] = part
                else:
                    h = h_ref[...]
                    r = _rstd(h)
                    xhat = h * r
                    dxn = dxn_ref[...] + part
                    dxhat = dxn * gpre_ref[...]
                    dhin_ref[...] = dh_ref[...] + r * (dxhat - xhat * jnp.mean(dxhat * xhat, axis=-1, keepdims=True))
                    dgpre_ref[...] += jnp.sum(dxn * xhat, axis=0, keepdims=True)

    row = pl.BlockSpec((TM, D), lambda i, j: (i, 0))
    wide = pl.BlockSpec((TM, 2 * FF_T), lambda i, j: (i, j))
    vec = pl.BlockSpec((1, D), lambda i, j: (0, 0))
    return pl.pallas_call(
        body, name=name, grid=(T // TM, 2),
        in_specs=[row, row, row, wide, _resident(g_pre), _resident(g_post), _resident(w_in), _resident(w_out),
                  UNREAD],
        out_specs=[row, wide, row, vec, vec],
        out_shape=[jax.ShapeDtypeStruct((T, D), BF16), jax.ShapeDtypeStruct((T, 2 * D_FF), BF16),
                   jax.ShapeDtypeStruct((T, D), F32), jax.ShapeDtypeStruct((1, D), F32),
                   jax.ShapeDtypeStruct((1, D), F32)],
        scratch_shapes=[pltpu.VMEM((TM, D), F32)],
        compiler_params=pltpu.CompilerParams(dimension_semantics=("arbitrary", "arbitrary"),
                                             vmem_limit_bytes=VMEM_LIMIT_FFN_BWD),
    )(dh, f, h_in, gu, g_pre, g_post, w_in, w_out, after)


def mix_out_bwd(dh, f, g, w_out, name, after):
    T, D = dh.shape

    def body(dh_ref, f_ref, g_ref, w_ref, _, df_ref, dm_ref, dg_ref):
        df, dg = _rms_bwd(dh_ref[...], f_ref[...], g_ref[...])
        df = df.astype(BF16)
        df_ref[...] = df

        @pl.when(pl.program_id(0) == 0)
        def _():
            dg_ref[...] = jnp.zeros_like(dg_ref)

        dg_ref[...] += dg
        dm_ref[...] = _dot_nt(df, w_ref[...]).astype(BF16)

    row = pl.BlockSpec((TM, D), lambda i: (i, 0))
    vec = pl.BlockSpec((1, D), lambda i: (0, 0))
    return pl.pallas_call(
        body, name=name, grid=(T // TM,),
        in_specs=[row, row, vec, _resident(w_out), UNREAD],
        out_specs=[row, row, vec],
        out_shape=[jax.ShapeDtypeStruct((T, D), BF16), jax.ShapeDtypeStruct((T, D), BF16),
                   jax.ShapeDtypeStruct((1, D), F32)],
        compiler_params=_params("arbitrary"),
    )(dh, f, g, w_out, after)


def mm_nt_norm_bwd(pieces, h_in, dh_out, g, name, after):
    T, D = h_in.shape

    def body(*refs):
        ab = refs[:2 * len(pieces)]
        h_ref, dh_ref, g_ref, _, o_ref, dg_ref = refs[2 * len(pieces):]
        dxn = _dot_nt(ab[0][...], ab[1][...])
        for p in range(1, len(pieces)):
            dxn += _dot_nt(ab[2 * p][...], ab[2 * p + 1][...])
        h = h_ref[...]
        r = _rstd(h)
        xhat = h * r
        dxhat = dxn * g_ref[...]
        o_ref[...] = dh_ref[...] + r * (dxhat - xhat * jnp.mean(dxhat * xhat, axis=-1, keepdims=True))

        @pl.when(pl.program_id(0) == 0)
        def _():
            dg_ref[...] = jnp.zeros_like(dg_ref)

        dg_ref[...] += jnp.sum(dxn * xhat, axis=0, keepdims=True)

    in_specs, args = [], []
    for a, w in pieces:
        in_specs += [pl.BlockSpec((TM, a.shape[1]), lambda i: (i, 0)), _resident(w)]
        args += [a, w]
    row = pl.BlockSpec((TM, D), lambda i: (i, 0))
    return pl.pallas_call(
        body, name=name, grid=(T // TM,),
        in_specs=in_specs + [row, row, _resident(g), UNREAD],
        out_specs=[row, pl.BlockSpec((1, D), lambda i: (0, 0))],
        out_shape=[jax.ShapeDtypeStruct((T, D), F32), jax.ShapeDtypeStruct((1, D), F32)],
        compiler_params=_params("arbitrary"),
    )(*args, h_in, dh_out, g, after)


def gate_merge_bwd(dm, gt, o_a, o_b, o_m, w_a, w_b, w_m, name):
    T = dm.shape[0]
    D = D_MODEL
    branch = ((o_a, w_a), (o_b, w_b), (o_m, w_m))

    def body(dm_ref, gt_ref, oa_ref, ob_ref, om_ref, wa_ref, wb_ref, wm_ref,
             dgt_ref, dpa_ref, dpb_ref, dpm_ref, doa_ref, dob_ref, dom_ref, db_ref):
        @pl.when(pl.program_id(0) == 0)
        def _():
            db_ref[...] = jnp.zeros_like(db_ref)

        dmf = dm_ref[...].astype(F32)
        for x, (o_ref, w_ref, dp_ref, do_ref) in enumerate(((oa_ref, wa_ref, dpa_ref, doa_ref),
                                                           (ob_ref, wb_ref, dpb_ref, dob_ref),
                                                           (om_ref, wm_ref, dpm_ref, dom_ref))):
            cols = slice(x * D, (x + 1) * D)
            gx = gt_ref[:, cols].astype(F32)
            w = w_ref[...]
            dpre = dmf * _dot(o_ref[...], w) * gx * (1.0 - gx)
            dgt_ref[:, cols] = dpre.astype(BF16)
            db_ref[:, cols] += jnp.sum(dpre, axis=0, keepdims=True)
            dp = (dmf * gx).astype(BF16)
            dp_ref[...] = dp
            do_ref[...] = _dot_nt(dp, w).astype(BF16)

    def rows(width):
        return pl.BlockSpec((TM, width), lambda i: (i, 0))

    def whole(arr):
        return pl.BlockSpec(arr.shape, lambda i: (0, 0))

    widths = [o.shape[1] for o, _ in branch]
    return pl.pallas_call(
        body, name=name, grid=(T // TM,),
        in_specs=[rows(D), rows(3 * D)] + [rows(k) for k in widths] + [whole(w) for _, w in branch],
        out_specs=[rows(3 * D), rows(D), rows(D), rows(D)] + [rows(k) for k in widths]
                  + [pl.BlockSpec((1, 3 * D), lambda i: (0, 0))],
        out_shape=[jax.ShapeDtypeStruct((T, 3 * D), BF16)] + [jax.ShapeDtypeStruct((T, D), BF16)] * 3
                  + [jax.ShapeDtypeStruct((T, k), BF16) for k in widths]
                  + [jax.ShapeDtypeStruct((1, 3 * D), F32)],
        compiler_params=_params("arbitrary"),
    )(dm, gt, o_a, o_b, o_m, w_a, w_b, w_m)


def mm_tn(x, dy, tm, tn, name, shard_major=False, perm=None, slabs=1, after=None):
    T, M = x.shape
    N = dy.shape[1]
    tk = min(2048, T)
    perm = perm or (lambda j: j)
    w = tn // slabs

    def body(x_ref, dy_ref, *rest):
        o_ref = rest[-1]

        @pl.when(pl.program_id(2) == 0)
        def _():
            o_ref[...] = jnp.zeros_like(o_ref)

        acc = _dot_tn(x_ref[...], dy_ref[...])
        if shard_major:
            for s in range(slabs):
                o_ref[s] += acc[:, s * w:(s + 1) * w]
        else:
            o_ref[...] += acc

    if shard_major:
        out_spec = pl.BlockSpec((slabs, tm, w), lambda i, j, k: (perm(j), i, 0))
        out_shape = jax.ShapeDtypeStruct((N // w, M, w), F32)
    else:
        out_spec = pl.BlockSpec((tm, tn), lambda i, j, k: (i, j))
        out_shape = jax.ShapeDtypeStruct((M, N), F32)
    return pl.pallas_call(
        body, name=name, grid=(M // tm, N // tn, T // tk),
        in_specs=[pl.BlockSpec((tk, tm), lambda i, j, k: (k, i)),
                  pl.BlockSpec((tk, tn), lambda i, j, k: (k, j))] + ([] if after is None else [UNREAD]),
        out_specs=out_spec, out_shape=out_shape,
        compiler_params=_params("parallel", "parallel", "arbitrary"),
    )(x, dy, *([] if after is None else [after]))


def rope_tables(T, zero):
    half = HEAD // 2
    inv = ROPE_THETA ** (-jnp.arange(half, dtype=F32) / half)
    ang = (jnp.arange(T).astype(F32) + zero)[:, None] * inv[None, :]
    cos, sin = jnp.cos(ang), jnp.sin(ang)
    return jnp.concatenate([cos, cos], axis=1), jnp.concatenate([-sin, sin], axis=1)


def layer_step(x, mem, target, gains, sinks, b_gate, weights_of, send_grads, zero):
    T = x.shape[0]
    cos, sin_signed = rope_tables(T, zero)
    no_sink = jnp.full((2,), NEG_INF, F32)

    w = dict(weights_of("ffn1_in", cos))
    xn1, gu1, a1 = ffn_in(x, gains["ffn1_norm_pre"], w["ffn1_w_in"], "ffn1_in")
    w.update(weights_of("ffn1_out", xn1))
    f1, h1 = mm_norm_res(a1, w["ffn1_w_out"], x, gains["ffn1_norm_post"], 0.5, "ffn1_out")
    w.update(weights_of("mix", f1))
    u, qkv, gt = mix_in(h1, gains["mix_norm_pre"], w["w_in"], w["w_gate"], b_gate, cos, sin_signed, "mix_in")
    outs, lses = [], []
    for gidx, (window, dil) in enumerate(DIL):
        o_g, l_g = band_fwd(qkv, no_sink, r=dil, base=A_BASE + 6 * gidx, hkv=2, grp=1, max_dist=window // dil,
                            out_dtype=F32, name=f"attn_a{gidx}_fwd")
        outs.append(o_g)
        lses.append(l_g)
    o_a, l_a = merge_groups(outs, lses, "attn_a_merge")
    o_b, l_b = band_fwd(qkv, sinks, r=1, base=B_BASE, hkv=2, grp=2, max_dist=HEAD - 1, out_dtype=BF16,
                        name="attn_b_fwd")
    mem_n, mkv = mem_kv(mem, gains["mem_norm"], w["w_mem_kv"], "mem_kv")
    o_m, l_m = mem_fwd(qkv, mkv, "attn_m_fwd")
    merged = gate_merge(gt, o_a, o_b, o_m, w["w_o_a"], w["w_o_b"], w["w_o_m"], "gate_merge")
    mo, h2 = mm_norm_res(merged, w["w_out"], h1, gains["mix_norm_post"], 1.0, "mix_out")
    w.update(weights_of("ffn2", mo))
    xn2, gu2, a2 = ffn_in(h2, gains["ffn2_norm_pre"], w["ffn2_w_in"], "ffn2_in")
    f2, dy, sq = mm_norm_res(a2, w["ffn2_w_out"], h2, gains["ffn2_norm_post"], 0.5, "ffn2_out", target=target)

    grads = {}

    def ffn_bwd(tag, dh_out, f, gu, a, xn, h_in, after):
        df, dgu, dh_in, grads[f"{tag}_norm_pre"], grads[f"{tag}_norm_post"] = ffn_tokens_bwd(
            dh_out, f, h_in, gu, gains[f"{tag}_norm_pre"], gains[f"{tag}_norm_post"], w[f"{tag}_w_in"],
            w[f"{tag}_w_out"], 0.5, f"{tag}_tokens_bwd", after)
        sent = send_grads(f"{tag}_out", {f"{tag}_w_out": mm_tn(a, df, FF_T, D_MODEL, f"{tag}_w_out_grad")})
        sent = send_grads(f"{tag}_in", {f"{tag}_w_in": mm_tn(
            xn, dgu, D_MODEL, FF_T, f"{tag}_w_in_grad", shard_major=True, perm=_ffn_perm, after=sent)})
        return dh_in, sent

    dh2, sent = ffn_bwd("ffn2", dy, f2, gu2, a2, xn2, h2, dy)

    mix = {}
    dmo, dmerged, grads["mix_norm_post"] = mix_out_bwd(
        dh2, mo, gains["mix_norm_post"], w["w_out"], "mix_out_bwd", sent)
    mix["w_out"] = mm_tn(merged, dmo, D_MODEL, D_MODEL, "w_out_grad")
    dgt, dpa, dpb, dpm, do_a, do_b, do_m, grads["b_gate"] = gate_merge_bwd(
        dmerged, gt, o_a, o_b, o_m, w["w_o_a"], w["w_o_b"], w["w_o_m"], "gate_merge_bwd")
    mix["w_o_a"] = mm_tn(o_a, dpa, o_a.shape[1], D_MODEL, "w_o_a_grad")
    mix["w_o_b"] = mm_tn(o_b, dpb, o_b.shape[1], D_MODEL, "w_o_b_grad")
    mix["w_o_m"] = mm_tn(o_m, dpm, o_m.shape[1], D_MODEL, "w_o_m_grad")

    dqkv = lax.empty(qkv.shape, qkv.dtype)
    for gidx, (window, dil) in enumerate(DIL):
        dqkv, = band_bwd(qkv, dqkv, do_a, o_a, l_a, cos, sin_signed, None, r=dil, base=A_BASE + 6 * gidx, hkv=2,
                         grp=1, max_dist=window // dil, name=f"attn_a{gidx}_bwd")
    dqkv, dsink = band_bwd(qkv, dqkv, do_b, o_b, l_b, cos, sin_signed, sinks, r=1, base=B_BASE, hkv=2, grp=2,
                           max_dist=HEAD - 1, name="attn_b_bwd")
    grads["sinks"] = -dsink[:, ::8, 0].reshape(1, 4)
    dqkv, dmk, dmv = mem_bwd(qkv, dqkv, mkv, do_m, o_m, l_m, "attn_m_bwd")
    mix["w_mem_kv"], grads["mem_norm"] = mem_kv_bwd(
        mem, gains["mem_norm"], mem_n, w["w_mem_kv"], jnp.concatenate([dmk, dmv], axis=1), "mem_kv_bwd")

    mix["w_in"] = mm_tn(u, dqkv, D_MODEL, 1280, "w_in_grad")
    mix["w_gate"] = mm_tn(u, dgt, D_MODEL, 1536, "w_gate_grad", shard_major=True, slabs=2)
    sent = send_grads("mix", mix)
    dh1, grads["mix_norm_pre"] = mm_nt_norm_bwd(
        [(dqkv, w["w_in"]), (dgt, w["w_gate"])], h1, dh2, gains["mix_norm_pre"], "mix_in_bwd", sent)

    dx, _ = ffn_bwd("ffn1", dh1, f1, gu1, a1, xn1, x, dh1)
    return sq, dx, grads


def _place():
    return lax.axis_index("x"), lax.axis_index("y"), lax.axis_index("c")


def _other_chips(x, y):
    return [(1 - x, y), (x, 1 - y), (1 - x, 1 - y)]


def _hbm(n):
    return [pl.BlockSpec(memory_space=pltpu.HBM)] * n


SEM = pl.BlockSpec(memory_space=pltpu.SEMAPHORE)
SIDE_EFFECT = pltpu.SideEffectType.DATAFLOW_SIDE_EFFECTING


def _chip_copy(src, land, sems, i, j, dst_slot, scatter):
    x, y, c = _place()
    px, py = _other_chips(x, y)[j]
    send_sems, recv_sems = sems
    return pltpu.make_async_remote_copy(
        src_ref=src[i].at[2 * px + py] if scatter else src[i], dst_ref=land[i].at[dst_slot],
        send_sem=send_sems.at[3 * i + j], recv_sem=recv_sems.at[3 * i + j],
        device_id=(px, py, c), device_id_type=MESH)


def chip_copies_start(srcs, lands, groups, scatter, name):
    n = len(srcs)

    def body(*refs):
        src, land = refs[:n], refs[n:2 * n]
        sems = refs[2 * n:2 * n + 2 * len(groups)]
        token = refs[-1]
        x, y, _ = _place()
        for g, members in enumerate(groups):
            part = ([src[i] for i in members], [land[i] for i in members])
            for t in range(len(members)):
                for j in range(3):
                    _chip_copy(*part, sems[2 * g:2 * g + 2], t, j, 2 * x + y, scatter).start()
        token[...] = jnp.zeros_like(token)

    sem_shapes = [pltpu.SemaphoreType.DMA((3 * len(m),)) for m in groups for _ in range(2)]
    thru = [pltpu.HBM(a.shape, a.dtype) for a in (*srcs, *lands)]
    res = pl.pallas_call(
        body, name=name,
        out_shape=(*sem_shapes, *thru, jax.ShapeDtypeStruct((8, 128), F32)),
        in_specs=_hbm(2 * n),
        out_specs=(*[SEM] * len(sem_shapes), *_hbm(2 * n), pl.BlockSpec(memory_space=pltpu.VMEM)),
        input_output_aliases={i: len(sem_shapes) + i for i in range(2 * n)},
        compiler_params=pltpu.CompilerParams(has_side_effects=SIDE_EFFECT),
    )(*[pltpu.with_memory_space_constraint(a, pltpu.HBM) for a in (*srcs, *lands)])
    k = len(sem_shapes)
    sems = [tuple(res[2 * g:2 * g + 2]) for g in range(len(groups))]
    return sems, list(res[k:k + n]), list(res[k + n:k + 2 * n]), res[-1]


def chip_copies_wait(srcs, lands, sems, after, scatter, name):
    n = len(srcs)

    def body(*refs):
        src, land = refs[:n], refs[n:2 * n]
        pair = refs[2 * n:2 * n + 2]
        x, y, _ = _place()
        for i in range(n):
            for j, (px, py) in enumerate(_other_chips(x, y)):
                copy = _chip_copy(src, land, pair, i, j, 2 * px + py, scatter)
                copy.wait_send()
                copy.wait_recv()

    res = pl.pallas_call(
        body, name=name,
        out_shape=[pltpu.HBM(a.shape, a.dtype) for a in (*srcs, *lands)],
        in_specs=[*_hbm(2 * n), SEM, SEM, pl.BlockSpec(memory_space=pl.ANY)],
        out_specs=_hbm(2 * n),
        input_output_aliases={i: i for i in range(2 * n)},
        compiler_params=pltpu.CompilerParams(has_side_effects=SIDE_EFFECT),
    )(*srcs, *lands, *sems, after)
    return list(res[n:])


def small_all_gather(small, name):
    flips = [(fx, fy, fc) for fx in (0, 1) for fy in (0, 1) for fc in (0, 1)][1:]

    def body(in_ref, out_ref, send_sems, recv_sems, local_sem):
        x, y, c = _place()
        me = 4 * x + 2 * y + c

        def copy(k, slot):
            fx, fy, fc = flips[k]
            return pltpu.make_async_remote_copy(
                src_ref=in_ref, dst_ref=out_ref.at[slot], send_sem=send_sems.at[k], recv_sem=recv_sems.at[k],
                device_id=(x ^ fx, y ^ fy, c ^ fc), device_id_type=MESH)

        local = pltpu.make_async_copy(in_ref, out_ref.at[me], local_sem)
        local.start()
        for k in range(len(flips)):
            copy(k, me).start()
        for k, (fx, fy, fc) in enumerate(flips):
            copy(k, 4 * (x ^ fx) + 2 * (y ^ fy) + (c ^ fc)).wait()
        local.wait()

    return pl.pallas_call(
        body, name=name, in_specs=_hbm(1), out_specs=_hbm(1)[0],
        out_shape=jax.ShapeDtypeStruct((N_DEV,) + small.shape, small.dtype),
        scratch_shapes=[pltpu.SemaphoreType.DMA((len(flips),)), pltpu.SemaphoreType.DMA((len(flips),)),
                        pltpu.SemaphoreType.DMA],
    )(small)


def sibling_exchange(parts, name):
    n = len(parts)

    def body(*refs):
        ins, outs = refs[:n], refs[n:2 * n]
        send_sems, recv_sems = refs[2 * n:]
        x, y, c = _place()
        copies = [pltpu.make_async_remote_copy(
            src_ref=ins[i], dst_ref=outs[i], send_sem=send_sems.at[i], recv_sem=recv_sems.at[i],
            device_id=(x, y, 1 - c), device_id_type=MESH) for i in range(n)]
        for cp in copies:
            cp.start()
        for cp in copies:
            cp.wait()

    return pl.pallas_call(
        body, name=name, in_specs=_hbm(n), out_specs=_hbm(n),
        out_shape=[jax.ShapeDtypeStruct(p.shape, p.dtype) for p in parts],
        scratch_shapes=[pltpu.SemaphoreType.DMA((n,)), pltpu.SemaphoreType.DMA((n,))],
    )(*parts)


def _row_tile(rows):
    for t in (256, 176, 128, 64, 32, 16, 8):
        if rows % t == 0:
            return t
    return rows


def chip_partial_sum(me, own_sm, recv, name):
    _, rows, cols = own_sm.shape
    tr = _row_tile(rows)

    def body(me_ref, own_ref, r0, r1, r2, r3, o_ref):
        acc = jnp.zeros((tr, cols), F32)
        for s, r_ref in enumerate((r0, r1, r2, r3)):
            acc = acc + jnp.where(me_ref[0] == s, own_ref[...], r_ref[...].astype(F32))
        o_ref[...] = acc

    def slot(s):
        return pl.BlockSpec((None, tr, cols), lambda i, me_ref, s=s: (s, i, 0))

    return pl.pallas_call(
        body, name=name,
        grid_spec=pltpu.PrefetchScalarGridSpec(
            num_scalar_prefetch=1, grid=(rows // tr,),
            in_specs=[pl.BlockSpec((None, tr, cols), lambda i, me_ref: (me_ref[0], i, 0))] + [slot(s) for s in range(4)],
            out_specs=pl.BlockSpec((tr, cols), lambda i, me_ref: (i, 0))),
        out_shape=jax.ShapeDtypeStruct((rows, cols), F32),
        compiler_params=_params("parallel"),
    )(me, own_sm, recv, recv, recv, recv)


def _adamw(w, g, m, v):
    m = ADAM_B1 * m + (1.0 - ADAM_B1) * g
    v = ADAM_B2 * v + (1.0 - ADAM_B2) * (g * g)
    m_hat = m / (1.0 - ADAM_B1 ** ADAM_STEP)
    v_hat = v / (1.0 - ADAM_B2 ** ADAM_STEP)
    delta = -ADAM_LR * (m_hat / (jnp.sqrt(v_hat) + ADAM_EPS) + ADAM_WD * w)
    return delta, m, v


def adamw_pair(part, sib, w, m, v, name):
    rows, cols = w.shape
    tr = _row_tile(rows)

    def body(p_ref, s_ref, w_ref, m_ref, v_ref, g_ref, d_ref, nm_ref, nv_ref):
        g = p_ref[...] + s_ref[...]
        g_ref[...] = g
        d_ref[...], nm_ref[...], nv_ref[...] = _adamw(w_ref[...], g, m_ref[...], v_ref[...])

    spec = pl.BlockSpec((tr, cols), lambda i: (i, 0))
    return pl.pallas_call(
        body, name=name, grid=(rows // tr,), in_specs=[spec] * 5, out_specs=[spec] * 4,
        out_shape=[jax.ShapeDtypeStruct((rows, cols), F32)] * 4,
        compiler_params=_params("parallel"),
    )(part, sib, w, m, v)


def adamw_small(g_all, w, m, v, name):
    def body(ga_ref, w_ref, m_ref, v_ref, g_ref, d_ref, nm_ref, nv_ref):
        g = ga_ref[0]
        for k in range(1, N_DEV):
            g = g + ga_ref[k]
        g_ref[...] = g
        d_ref[...], nm_ref[...], nv_ref[...] = _adamw(w_ref[...], g, m_ref[...], v_ref[...])

    return pl.pallas_call(
        body, name=name, out_shape=[jax.ShapeDtypeStruct(w.shape, F32)] * 4,
    )(g_all, w, m, v)


WEIGHTS = ("ffn1_norm_pre", "ffn1_w_in", "ffn1_w_out", "ffn1_norm_post", "mix_norm_pre", "w_in", "sinks",
           "mem_norm", "w_mem_kv", "w_gate", "b_gate", "w_o_a", "w_o_b", "w_o_m", "w_out", "mix_norm_post",
           "ffn2_norm_pre", "ffn2_w_in", "ffn2_w_out", "ffn2_norm_post")
BIG = ("ffn1_w_in", "ffn1_w_out", "w_in", "w_mem_kv", "w_gate", "w_o_a", "w_o_b", "w_o_m", "w_out",
       "ffn2_w_in", "ffn2_w_out")
GATHER_ORDER = ("ffn1_in", "ffn1_out", "mix", "ffn2")
GATHER_GROUPS = {"ffn1_in": ("ffn1_w_in",), "ffn1_out": ("ffn1_w_out",),
                 "mix": ("w_in", "w_gate", "w_mem_kv", "w_o_a", "w_o_b", "w_o_m", "w_out"),
                 "ffn2": ("ffn2_w_in", "ffn2_w_out")}
GROUPS = {"ffn1_in": ("ffn1_w_in",), "ffn1_out": ("ffn1_w_out",),
          "mix": ("w_in", "w_gate", "w_mem_kv", "w_o_a", "w_o_b", "w_o_m", "w_out"),
          "ffn2_in": ("ffn2_w_in",), "ffn2_out": ("ffn2_w_out",)}
COLUMN_SHARDED = ("ffn1_w_in", "ffn2_w_in", "w_in", "w_gate", "w_o_a", "w_o_b", "w_o_m")
KEPT_SHARD_MAJOR = ("ffn1_w_in", "ffn2_w_in", "w_gate")
GAINS = ("ffn1_norm_pre", "ffn1_norm_post", "mix_norm_pre", "mem_norm", "mix_norm_post", "ffn2_norm_pre",
         "ffn2_norm_post")
SMALL_ROWS = 16


def _pack_small(t):
    sinks = jnp.pad(t["sinks"], ((0, 0), (0, D_MODEL - t["sinks"].shape[1])))
    rows = [t[k] for k in GAINS] + [t["b_gate"].reshape(3, D_MODEL), sinks]
    packed = jnp.concatenate(rows, axis=0)
    return jnp.pad(packed, ((0, SMALL_ROWS - packed.shape[0]), (0, 0)))


def _unpack_small(p):
    out = {k: p[i:i + 1] for i, k in enumerate(GAINS)}
    out["b_gate"] = p[7:10].reshape(1, 3 * D_MODEL)
    out["sinks"] = p[10:11, :4]
    return out


def kernel(x, mem, ffn1_norm_pre, ffn1_w_in, ffn1_w_out, ffn1_norm_post, mix_norm_pre, w_in, sinks, mem_norm, w_mem_kv, w_gate, b_gate, w_o_a, w_o_b, w_o_m, w_out, mix_norm_post, ffn2_norm_pre, ffn2_w_in, ffn2_w_out, ffn2_norm_post, loss_target, m_ffn1_norm_pre, m_ffn1_w_in, m_ffn1_w_out, m_ffn1_norm_post, m_mix_norm_pre, m_w_in, m_sinks, m_mem_norm, m_w_mem_kv, m_w_gate, m_b_gate, m_w_o_a, m_w_o_b, m_w_o_m, m_w_out, m_mix_norm_post, m_ffn2_norm_pre, m_ffn2_w_in, m_ffn2_w_out, m_ffn2_norm_post, v_ffn1_norm_pre, v_ffn1_w_in, v_ffn1_w_out, v_ffn1_norm_post, v_mix_norm_pre, v_w_in, v_sinks, v_mem_norm, v_w_mem_kv, v_w_gate, v_b_gate, v_w_o_a, v_w_o_b, v_w_o_m, v_w_out, v_mix_norm_post, v_ffn2_norm_pre, v_ffn2_w_in, v_ffn2_w_out, v_ffn2_norm_post):
    given = dict(locals())
    wt = {k: given[k] for k in WEIGHTS}
    mom = {k: given["m_" + k] for k in WEIGHTS}
    var = {k: given["v_" + k] for k in WEIGHTS}
    chip = (2 * lax.axis_index("x") + lax.axis_index("y")).astype(jnp.int32)
    me = chip.reshape(1)

    def landing_zone(own):
        return lax.dynamic_update_slice_in_dim(lax.empty((N_CHIPS,) + own.shape, own.dtype), own[None], chip, 0)

    shards = [wt[k][0].astype(BF16) for k in BIG]
    members = [[BIG.index(k) for k in GATHER_GROUPS[g]] for g in GATHER_ORDER]
    sems, shards, lands, token = chip_copies_start(
        shards, [landing_zone(s) for s in shards], members, False, "weight_gather_start")

    def weights_of(group, after):
        idx = members[GATHER_ORDER.index(group)]
        got = chip_copies_wait([shards[i] for i in idx], [lands[i] for i in idx], sems[GATHER_ORDER.index(group)],
                               after, False, f"weight_gather_wait_{group}")
        full = {}
        for k, g in zip(GATHER_GROUPS[group], got):
            if k in COLUMN_SHARDED:
                if k in ("ffn1_w_in", "ffn2_w_in"):
                    g = jnp.stack([g[0], g[2], g[1], g[3]])
                full[k] = jnp.swapaxes(g, 0, 1).reshape(g.shape[1], N_CHIPS * g.shape[2])
                if k == "w_in":
                    full[k] = to_kernel_heads(full[k])
            else:
                full[k] = g.reshape(N_CHIPS * g.shape[1], g.shape[2])
        return full

    in_flight = {}

    def send_grads(group, grads):
        own, wire = [], []
        for k in GROUPS[group]:
            g = from_kernel_heads(grads[k]) if k == "w_in" else grads[k]
            if k in KEPT_SHARD_MAJOR:
                pass
            elif k in COLUMN_SHARDED:
                g = jnp.swapaxes(g.reshape(g.shape[0], N_CHIPS, g.shape[1] // N_CHIPS), 0, 1)
            else:
                g = g.reshape(N_CHIPS, g.shape[0] // N_CHIPS, g.shape[1])
            own.append(g)
            wire.append(g.astype(BF16))
        zones = [landing_zone(lax.dynamic_index_in_dim(b, chip, 0, keepdims=False)) for b in wire]
        pair, wire, zones, sent = chip_copies_start(
            wire, zones, [list(range(len(wire)))], True, f"grad_scatter_start_{group}")
        in_flight[group] = (own, wire, zones, pair[0], sent)
        return sent

    gains = {k: wt[k] for k in GAINS}
    sq, dx, grads = layer_step(
        x[0], mem[0], loss_target[0], gains, sinks[0], b_gate, weights_of, send_grads, token[0, 0])
    loss = lax.psum(0.5 * sq[0, 0] / D_MODEL, ("x", "y", "c"))

    res = {}
    after = in_flight["ffn1_in"][4]
    for stage in (("ffn2_out", "ffn2_in", "mix", "ffn1_out"), ("ffn1_in",)):
        names, parts = [], []
        for group in stage:
            own, wire, zones, pair, _ = in_flight[group]
            received = chip_copies_wait(wire, zones, pair, after, True, f"grad_scatter_wait_{group}")
            for k, g, r in zip(GROUPS[group], own, received):
                names.append(k)
                parts.append(chip_partial_sum(me, g, r, f"{k}_chip_sum"))
        sibs = sibling_exchange(parts, f"sibling_exchange_{stage[-1]}")
        for k, p, s in zip(names, parts, sibs):
            res[k] = [t[None] for t in adamw_pair(p, s, wt[k][0], mom[k][0], var[k][0], f"{k}_adamw")]
        after = res[names[-1]][0]
    small_all = small_all_gather(_pack_small(grads), "small_grad_gather")
    packed = adamw_small(small_all, _pack_small(wt), _pack_small(mom), _pack_small(var), "small_adamw")
    for idx, p in enumerate(packed):
        for k, t in _unpack_small(p).items():
            res.setdefault(k, [None] * 4)[idx] = t

    return (loss, dx[None], *[res[k][0] for k in WEIGHTS], *[res[k][1] for k in WEIGHTS],
            *[res[k][2] for k in WEIGHTS], *[res[k][3] for k in WEIGHTS])
```

```python
import functools

import jax
import jax.numpy as jnp
from jax import lax
from jax.experimental import pallas as pl
from jax.experimental.pallas import tpu as pltpu

F32 = jnp.float32
BF16 = jnp.bfloat16

D_MODEL = 1024
D_FF = 2816
HEAD = 128
N_CHIPS = 4
N_DEV = 8
EPS = 1e-6
NEG_INF = -1e30
ROPE_THETA = 10000.0
ATT_SCALE = HEAD ** -0.5

ADAM_LR = 0.001
ADAM_B1 = 0.9
ADAM_B2 = 0.999
ADAM_EPS = 1e-08
ADAM_WD = 0.01
ADAM_STEP = 10

VMEM_LIMIT = 52 * 2 ** 20
MESH = pl.DeviceIdType.MESH

QKV_W = 3840
DIL = ((128, 1), (512, 4), (2048, 16))
B_BASE, MQ, A_BASE = 0, 8, 12
_AQ, _AK, _AV, _BQ, _BK, _BV, _MQ = 0, 6, 12, 18, 22, 24, 26
HEAD_ORDER = tuple(
    [h for j in range(2) for h in (_BQ + 2 * j, _BQ + 2 * j + 1, _BK + j, _BV + j)]
    + [_MQ + i for i in range(4)]
    + [h for g in range(3) for i in range(2) for h in (_AQ + 2 * g + i, _AK + 2 * g + i, _AV + 2 * g + i)])
ROTARY_HEADS = tuple(p for p, h in enumerate(HEAD_ORDER) if h < _AV or _BQ <= h < _BV)


def to_kernel_heads(w):
    return jnp.concatenate([w[..., h * HEAD:(h + 1) * HEAD] for h in HEAD_ORDER], axis=-1)


def from_kernel_heads(w):
    place = {h: p for p, h in enumerate(HEAD_ORDER)}
    return jnp.concatenate([w[..., place[h] * HEAD:(place[h] + 1) * HEAD] for h in range(len(HEAD_ORDER))], axis=-1)

TM = 512
FF_T = D_FF // 2


def _params(*sem):
    return pltpu.CompilerParams(dimension_semantics=sem, vmem_limit_bytes=VMEM_LIMIT)


def _dot(a, b):
    return jnp.dot(a, b, preferred_element_type=F32)


def _dot_nt(a, b):
    return lax.dot_general(a, b, (((1,), (1,)), ((), ())), preferred_element_type=F32)


def _dot_tn(a, b):
    return lax.dot_general(a, b, (((0,), (0,)), ((), ())), preferred_element_type=F32)


def _rstd(x):
    return lax.rsqrt(jnp.mean(x * x, axis=-1, keepdims=True) + EPS)


def _sigmoid(x):
    return 0.5 * jnp.tanh(0.5 * x) + 0.5


def _ffn_perm(k):
    return (k % 2) * 2 + k // 2


UNREAD = pl.BlockSpec(memory_space=pl.ANY)


def _resident(arr):
    return pl.BlockSpec(arr.shape, lambda *_: (0,) * arr.ndim, pipeline_mode=pl.Buffered(1))


def ffn_in(h, g, w, name):
    T, D = h.shape

    def body(h_ref, g_ref, w_ref, xn_ref, gu_ref, a_ref):
        x = h_ref[...]
        xn = (x * _rstd(x) * g_ref[...]).astype(BF16)
        xn_ref[...] = xn
        for j in range(2):
            gu = _dot(xn, w_ref[:, j * 2 * FF_T:(j + 1) * 2 * FF_T])
            gu_ref[:, j * 2 * FF_T:(j + 1) * 2 * FF_T] = gu.astype(BF16)
            gate, up = gu[:, :FF_T], gu[:, FF_T:]
            a_ref[:, j * FF_T:(j + 1) * FF_T] = (gate * _sigmoid(gate) * up).astype(BF16)

    def rows(width):
        return pl.BlockSpec((TM, width), lambda i: (i, 0))

    return pl.pallas_call(
        body, name=name,
        grid=(T // TM,),
        in_specs=[rows(D), _resident(g), _resident(w)],
        out_specs=[rows(D), rows(2 * D_FF), rows(D_FF)],
        out_shape=[jax.ShapeDtypeStruct((T, D), BF16),
                   jax.ShapeDtypeStruct((T, 2 * D_FF), BF16),
                   jax.ShapeDtypeStruct((T, D_FF), BF16)],
        compiler_params=_params("parallel"),
    )(h, g, w)


def mm_norm_res(a, w, h_in, g, coef, name, target=None):
    T, K = a.shape
    D = w.shape[1]
    final = target is not None

    def body(*refs):
        if final:
            a_ref, w_ref, h_ref, g_ref, t_ref, f_ref, o_ref, l_ref = refs
        else:
            a_ref, w_ref, h_ref, g_ref, f_ref, o_ref = refs
        f = _dot(a_ref[...], w_ref[...])
        f_ref[...] = f
        y = h_ref[...] + coef * (f * _rstd(f) * g_ref[...])
        if final:
            err = y - t_ref[...]
            o_ref[...] = err * (1.0 / D)

            @pl.when(pl.program_id(0) == 0)
            def _():
                l_ref[...] = jnp.zeros_like(l_ref)

            l_ref[...] += jnp.sum(err * err)
        else:
            o_ref[...] = y

    row = pl.BlockSpec((TM, D), lambda i: (i, 0))
    in_specs = [pl.BlockSpec((TM, K), lambda i: (i, 0)),
                _resident(w),
                row, pl.BlockSpec((1, D), lambda i: (0, 0))]
    out_specs = [row, row]
    out_shape = [jax.ShapeDtypeStruct((T, D), F32), jax.ShapeDtypeStruct((T, D), F32)]
    args = [a, w, h_in, g]
    if final:
        in_specs.append(row)
        args.append(target)
        out_specs.append(pl.BlockSpec((8, 128), lambda i: (0, 0)))
        out_shape.append(jax.ShapeDtypeStruct((8, 128), F32))
    return pl.pallas_call(
        body, name=name, grid=(T // TM,), in_specs=in_specs, out_specs=out_specs, out_shape=out_shape,
        compiler_params=_params("arbitrary"),
    )(*args)


def _rope(x, cos, sin_signed):
    return x * cos + pltpu.roll(x, HEAD // 2, axis=1) * sin_signed


def _unrope(x, cos, sin_signed):
    return x * cos - pltpu.roll(x, HEAD // 2, axis=1) * sin_signed


def mix_in(h, g, w, w_gate, b_gate, cos, sin_signed, name):
    T, D = h.shape
    tn = 768

    def body(h_ref, g_ref, w_ref, wg_ref, b_ref, c_ref, s_ref, u_ref, o_ref, gt_ref):
        x = h_ref[...]
        u = (x * _rstd(x) * g_ref[...]).astype(BF16)
        u_ref[...] = u
        c, s = c_ref[...], s_ref[...]
        for j in range(QKV_W // tn):
            acc = _dot(u, w_ref[:, j * tn:(j + 1) * tn])
            for hd in range(tn // HEAD):
                head = j * (tn // HEAD) + hd
                part = acc[:, hd * HEAD:(hd + 1) * HEAD]
                if head in ROTARY_HEADS:
                    part = _rope(part, c, s)
                o_ref[:, head * HEAD:(head + 1) * HEAD] = part.astype(BF16)
        for j in range(w_gate.shape[1] // tn):
            cols = slice(j * tn, (j + 1) * tn)
            gt_ref[:, cols] = _sigmoid(_dot(u, wg_ref[:, cols]) + b_ref[:, cols]).astype(BF16)

    def rows(width):
        return pl.BlockSpec((TM, width), lambda i: (i, 0))

    return pl.pallas_call(
        body, name=name,
        grid=(T // TM,),
        in_specs=[rows(D), _resident(g), _resident(w), _resident(w_gate), _resident(b_gate), rows(HEAD), rows(HEAD)],
        out_specs=[rows(D), rows(QKV_W), rows(w_gate.shape[1])],
        out_shape=[jax.ShapeDtypeStruct((T, D), BF16), jax.ShapeDtypeStruct((T, QKV_W), BF16),
                   jax.ShapeDtypeStruct((T, w_gate.shape[1]), BF16)],
        compiler_params=_params("parallel"),
    )(h, g, w, w_gate, b_gate, cos, sin_signed)


def gate_merge_out(gt, o_a, o_b, o_m, w_a, w_b, w_m, w_out, h_in, g, name):
    T = gt.shape[0]
    D = D_MODEL

    def body(gt_ref, oa_ref, ob_ref, om_ref, wa_ref, wb_ref, wm_ref, wo_ref, h_ref, g_ref, m_ref, f_ref, o_ref):
        acc = gt_ref[:, :D].astype(F32) * _dot(oa_ref[...], wa_ref[...])
        acc += gt_ref[:, D:2 * D].astype(F32) * _dot(ob_ref[...], wb_ref[...])
        acc += gt_ref[:, 2 * D:].astype(F32) * _dot(om_ref[...], wm_ref[...])
        merged = acc.astype(BF16)
        m_ref[...] = merged
        f = _dot(merged, wo_ref[...])
        f_ref[...] = f
        o_ref[...] = h_ref[...] + f * _rstd(f) * g_ref[...]

    def rows(width):
        return pl.BlockSpec((TM, width), lambda i: (i, 0))

    return pl.pallas_call(
        body, name=name, grid=(T // TM,),
        in_specs=[rows(3 * D), rows(o_a.shape[1]), rows(o_b.shape[1]), rows(o_m.shape[1]),
                  _resident(w_a), _resident(w_b), _resident(w_m), _resident(w_out), rows(D), _resident(g)],
        out_specs=[rows(D), rows(D), rows(D)],
        out_shape=[jax.ShapeDtypeStruct((T, D), BF16), jax.ShapeDtypeStruct((T, D), F32),
                   jax.ShapeDtypeStruct((T, D), F32)],
        compiler_params=_params("parallel"),
    )(gt, o_a, o_b, o_m, w_a, w_b, w_m, w_out, h_in, g)


def _band_rows(start, r):
    return pl.ds(start, HEAD) if r == 1 else pl.ds(start, HEAD, stride=r)


def _band_mask(max_dist, first_has_prev):
    row = lax.broadcasted_iota(jnp.int32, (HEAD, 2 * HEAD), 0)
    col = lax.broadcasted_iota(jnp.int32, (HEAD, 2 * HEAD), 1)
    dist = row + HEAD - col
    band = (dist >= 0) & (dist <= max_dist)
    return band, band & (col >= jnp.where(first_has_prev, 0, HEAD))


def _stack(parts):
    return parts[0] if len(parts) == 1 else jnp.concatenate(parts, axis=0)


def _band_specs(BT, SB, nsub, base, grp):
    stride = grp + 2

    def cur(off, width):
        return pl.BlockSpec((BT, width * HEAD), lambda h, i: (i, (base + h * stride + off) // width))

    def prev(off):
        return pl.BlockSpec((SB, HEAD), lambda h, i: (jnp.maximum(i * nsub - 1, 0), base + h * stride + off))

    return cur(0, grp), cur(grp, 1), prev(grp), cur(grp + 1, 1), prev(grp + 1)


def band_fwd(qkv, sinks, *, r, base, hkv, grp, max_dist, out_dtype, name):
    T, W = qkv.shape
    SB = HEAD * r
    BT = min(2048, T)
    nsub, nib = BT // SB, T // BT
    hq = hkv * grp
    heads = [slice(g * HEAD, (g + 1) * HEAD) for g in range(grp)]

    def body(sink_ref, q_ref, kc_ref, kp_ref, vc_ref, vp_ref, o_ref, l_ref, qf, kf, vf):
        kvh, ib = pl.program_id(0), pl.program_id(1)
        qf[...] = q_ref[...].astype(F32)
        kf[:SB] = kp_ref[...].astype(F32)
        kf[SB:] = kc_ref[...].astype(F32)
        vf[:SB] = vp_ref[...].astype(F32)
        vf[SB:] = vc_ref[...].astype(F32)
        band, band_first = _band_mask(max_dist, ib > 0)
        for c in range(r):
            k_old, v_old = kf[_band_rows(c, r)], vf[_band_rows(c, r)]
            for j in range(nsub):
                mask = band_first if j == 0 else band
                rows = _band_rows(j * SB + c, r)
                k_own, v_own = kf[_band_rows((j + 1) * SB + c, r)], vf[_band_rows((j + 1) * SB + c, r)]
                kcat = jnp.concatenate([k_old, k_own], axis=0).astype(BF16)
                vcat = jnp.concatenate([v_old, v_own], axis=0).astype(BF16)
                k_old, v_old = k_own, v_own
                s_all = _dot_nt(_stack([qf[rows, cols] for cols in heads]).astype(BF16), kcat) * ATT_SCALE
                probs, tots = [], []
                for g, cols in enumerate(heads):
                    s = jnp.where(mask, s_all[cols], NEG_INF)
                    sk = sink_ref[kvh * grp + g]
                    m = jnp.maximum(jnp.max(s, axis=-1, keepdims=True), sk)
                    p = jnp.exp(s - m)
                    tot = jnp.sum(p, axis=-1, keepdims=True) + jnp.exp(sk - m)
                    probs.append(p.astype(BF16))
                    tots.append(tot)
                    l_ref[rows, cols] = jnp.broadcast_to(m + jnp.log(tot), (HEAD, HEAD))
                o_all = _dot(_stack(probs), vcat)
                for g, cols in enumerate(heads):
                    o_ref[rows, cols] = (o_all[cols] / tots[g]).astype(out_dtype)

    out_spec = pl.BlockSpec((BT, grp * HEAD), lambda h, i: (i, h))
    return pl.pallas_call(
        body, name=name, grid=(hkv, nib),
        in_specs=[pl.BlockSpec(memory_space=pltpu.SMEM), *_band_specs(BT, SB, nsub, base, grp)],
        out_specs=[out_spec, out_spec],
        out_shape=[jax.ShapeDtypeStruct((T, hq * HEAD), out_dtype), jax.ShapeDtypeStruct((T, hq * HEAD), F32)],
        scratch_shapes=[pltpu.VMEM((BT, grp * HEAD), F32), pltpu.VMEM((SB + BT, HEAD), F32),
                        pltpu.VMEM((SB + BT, HEAD), F32)],
        compiler_params=_params("parallel", "arbitrary"),
    )(sinks, qkv, qkv, qkv, qkv, qkv)


def band_bwd(qkv, dqkv, do, o, lse, cos, sin_signed, sinks, *, r, base, hkv, grp, max_dist, name):
    T, W = qkv.shape
    SB = HEAD * r
    BT = min(2048, T)
    nsub, nib = BT // SB, T // BT
    nblk = T // SB
    with_sink = sinks is not None
    heads = [slice(g * HEAD, (g + 1) * HEAD) for g in range(grp)]

    def body(*refs):
        if with_sink:
            sink_ref, refs = refs[0], refs[1:]
        (q_ref, kc_ref, kp_ref, vc_ref, vp_ref, qn_ref, do_ref, don_ref, o_ref, on_ref, l_ref, ln_ref,
         c_ref, s_ref, _) = refs[:15]
        out_ref = refs[15]
        ds_ref = refs[16] if with_sink else None
        qf, dof, of, kf, vf, dqf, dkf, dvf = refs[-8:]
        kvh, ib = pl.program_id(0), pl.program_id(1)
        for buf, cur_ref, nxt_ref in ((qf, q_ref, qn_ref), (dof, do_ref, don_ref), (of, o_ref, on_ref)):
            buf[:BT] = cur_ref[...].astype(F32)
            buf[BT:] = nxt_ref[...].astype(F32)
        kf[:SB] = kp_ref[...].astype(F32)
        kf[SB:] = kc_ref[...].astype(F32)
        vf[:SB] = vp_ref[...].astype(F32)
        vf[SB:] = vc_ref[...].astype(F32)
        band, band_first = _band_mask(max_dist, ib > 0)
        if with_sink:
            @pl.when(ib == 0)
            def _():
                ds_ref[...] = jnp.zeros_like(ds_ref)

        def grads(rows, logzs, keys, vals, mask):
            q = _stack([qf[rows, cols] for cols in heads]).astype(BF16)
            dout = _stack([dof[rows, cols] for cols in heads]).astype(BF16)
            s_all = _dot_nt(q, keys) * ATT_SCALE
            dp_all = _dot_nt(dout, vals)
            probs, dss, deltas = [], [], []
            for g, cols in enumerate(heads):
                delta = jnp.sum(dof[rows, cols] * of[rows, cols], axis=-1, keepdims=True)
                p = jnp.exp(jnp.where(mask, s_all[cols], NEG_INF) - logzs[g][:, :1])
                probs.append(p.astype(BF16))
                dss.append((p * (dp_all[cols] - delta) * ATT_SCALE).astype(BF16))
                deltas.append(delta)
            return q, dout, _stack(probs), _stack(dss), deltas

        row = lax.broadcasted_iota(jnp.int32, (HEAD, HEAD), 0)
        col = lax.broadcasted_iota(jnp.int32, (HEAD, HEAD), 1)
        reach = col >= row + jnp.where(ib < nib - 1, HEAD - max_dist, 2 * HEAD)
        for c in range(r):
            k_old, v_old = kf[_band_rows(c, r)], vf[_band_rows(c, r)]
            dk_own = dv_own = None
            for j in range(nsub):
                rows = _band_rows(j * SB + c, r)
                k_own, v_own = kf[_band_rows((j + 1) * SB + c, r)], vf[_band_rows((j + 1) * SB + c, r)]
                kcat = jnp.concatenate([k_old, k_own], axis=0).astype(BF16)
                vcat = jnp.concatenate([v_old, v_own], axis=0).astype(BF16)
                logzs = [l_ref[rows, cols] for cols in heads]
                q, dout, p, ds, deltas = grads(rows, logzs, kcat, vcat, band_first if j == 0 else band)
                dq = _dot(ds, kcat)
                for g, cols in enumerate(heads):
                    dqf[rows, cols] = dq[cols]
                    if with_sink:
                        p_sink = jnp.exp(sink_ref[kvh * grp + g] - logzs[g][:, :1])
                        ds_ref[g * 8:(g + 1) * 8] += jnp.sum(p_sink * deltas[g])
                dk, dv = _dot_tn(ds, q), _dot_tn(p, dout)
                if j > 0:
                    done = _band_rows((j - 1) * SB + c, r)
                    dkf[done] = dk_own + dk[:HEAD]
                    dvf[done] = dv_own + dv[:HEAD]
                dk_own, dv_own = dk[HEAD:], dv[HEAD:]
                k_old, v_old = k_own, v_own
            logzs = [ln_ref[_band_rows(c, r), cols] for cols in heads]
            q, dout, p, ds, _ = grads(_band_rows(BT + c, r), logzs, k_old.astype(BF16), v_old.astype(BF16), reach)
            done = _band_rows((nsub - 1) * SB + c, r)
            dkf[done] = dk_own + _dot_tn(ds, q)
            dvf[done] = dv_own + _dot_tn(p, dout)

        cs, sn = c_ref[...], s_ref[...]
        for cols in heads:
            out_ref[:, cols] = _unrope(dqf[:, cols], cs, sn).astype(BF16)
        out_ref[:, grp * HEAD:(grp + 1) * HEAD] = _unrope(dkf[...], cs, sn).astype(BF16)
        out_ref[:, (grp + 1) * HEAD:] = dvf[...].astype(BF16)

    def nxt_row(i):
        return jnp.minimum((i + 1) * nsub, nblk - 1)

    stride = grp + 2
    q_next = pl.BlockSpec((SB, grp * HEAD), lambda h, i: (nxt_row(i), (base + h * stride) // grp))
    head_cur = pl.BlockSpec((BT, grp * HEAD), lambda h, i: (i, h))
    head_next = pl.BlockSpec((SB, grp * HEAD), lambda h, i: (nxt_row(i), h))
    table = pl.BlockSpec((BT, HEAD), lambda h, i: (i, 0))

    in_specs = [*_band_specs(BT, SB, nsub, base, grp), q_next,
                head_cur, head_next, head_cur, head_next, head_cur, head_next, table, table, UNREAD]
    args = [qkv, qkv, qkv, qkv, qkv, qkv, do, do, o, o, lse, lse, cos, sin_signed, dqkv]
    out_specs = [pl.BlockSpec((BT, stride * HEAD), lambda h, i: (i, base // stride + h))]
    out_shape = [jax.ShapeDtypeStruct(dqkv.shape, dqkv.dtype)]
    if with_sink:
        in_specs.insert(0, pl.BlockSpec(memory_space=pltpu.SMEM))
        args.insert(0, sinks)
        out_specs.append(pl.BlockSpec((None, grp * 8, HEAD), lambda h, i: (h, 0, 0)))
        out_shape.append(jax.ShapeDtypeStruct((hkv, grp * 8, HEAD), F32))
    wide = pltpu.VMEM((BT + SB, grp * HEAD), F32)
    tall = pltpu.VMEM((SB + BT, HEAD), F32)
    grad = pltpu.VMEM((BT, HEAD), F32)
    return pl.pallas_call(
        body, name=name, grid=(hkv, nib), in_specs=in_specs, out_specs=out_specs, out_shape=out_shape,
        input_output_aliases={len(args) - 1: 0},
        scratch_shapes=[wide, wide, wide, tall, tall, pltpu.VMEM((BT, grp * HEAD), F32), grad, grad],
        compiler_params=_params("parallel", "arbitrary"),
    )(*args)


def merge_groups(outs, lses, name):
    T, Wd = outs[0].shape
    tm = 1024

    def body(o0, o1, o2, l0, l1, l2, out_ref, lt_ref):
        a, b, c = l0[...], l1[...], l2[...]
        m = jnp.maximum(jnp.maximum(a, b), c)
        wa, wb, wc = jnp.exp(a - m), jnp.exp(b - m), jnp.exp(c - m)
        z = wa + wb + wc
        out_ref[...] = ((wa * o0[...] + wb * o1[...] + wc * o2[...]) / z).astype(BF16)
        lt_ref[...] = m + jnp.log(z)

    spec = pl.BlockSpec((tm, Wd), lambda i: (i, 0))
    return pl.pallas_call(
        body, name=name, grid=(T // tm,), in_specs=[spec] * 6, out_specs=[spec, spec],
        out_shape=[jax.ShapeDtypeStruct((T, Wd), BF16), jax.ShapeDtypeStruct((T, Wd), F32)],
        compiler_params=_params("parallel"),
    )(*outs, *lses)


M_HEADS = 4


def mem_kv(mem, g, w, name):
    n, D = mem.shape

    def body(m_ref, g_ref, w_ref, mn_ref, kv_ref):
        x = m_ref[...]
        mn = (x * _rstd(x) * g_ref[...]).astype(BF16)
        mn_ref[...] = mn
        kv_ref[...] = _dot(mn, w_ref[...]).astype(BF16)

    return pl.pallas_call(
        body, name=name,
        out_shape=[jax.ShapeDtypeStruct((n, D), BF16), jax.ShapeDtypeStruct((n, w.shape[1]), BF16)],
        compiler_params=pltpu.CompilerParams(vmem_limit_bytes=VMEM_LIMIT),
    )(mem, g, w)


def mem_fwd(qkv, mkv, name):
    T = qkv.shape[0]
    n = mkv.shape[0]
    RB = 1024

    def body(q_ref, kv_ref, o_ref, l_ref):
        for h in range(M_HEADS):
            cols = slice(h * HEAD, (h + 1) * HEAD)
            s = _dot_nt(q_ref[:, cols], kv_ref[:, cols]) * ATT_SCALE
            m = jnp.max(s, axis=-1, keepdims=True)
            p = jnp.exp(s - m)
            den = jnp.sum(p, axis=-1, keepdims=True)
            vals = kv_ref[:, (M_HEADS + h) * HEAD:(M_HEADS + h + 1) * HEAD]
            o_ref[:, cols] = (_dot(p.astype(BF16), vals) / den).astype(BF16)
            l_ref[:, cols] = jnp.broadcast_to(m + jnp.log(den), (RB, HEAD))

    out = pl.BlockSpec((RB, M_HEADS * HEAD), lambda i: (i, 0))
    return pl.pallas_call(
        body, name=name, grid=(T // RB,),
        in_specs=[pl.BlockSpec((RB, M_HEADS * HEAD), lambda i: (i, MQ // M_HEADS)), _resident(mkv)],
        out_specs=[out, out],
        out_shape=[jax.ShapeDtypeStruct((T, M_HEADS * HEAD), BF16), jax.ShapeDtypeStruct((T, M_HEADS * HEAD), F32)],
        compiler_params=_params("parallel"),
    )(qkv, mkv)


def mem_bwd(qkv, dqkv, mkv, do, o, lse, name):
    T = qkv.shape[0]
    n = mkv.shape[0]
    RB = 1024

    def body(q_ref, kv_ref, do_ref, o_ref, l_ref, _, dq_ref, dk_ref, dv_ref):
        @pl.when(pl.program_id(0) == 0)
        def _():
            dk_ref[...] = jnp.zeros_like(dk_ref)
            dv_ref[...] = jnp.zeros_like(dv_ref)

        for h in range(M_HEADS):
            cols = slice(h * HEAD, (h + 1) * HEAD)
            keys, vals = kv_ref[:, cols], kv_ref[:, (M_HEADS + h) * HEAD:(M_HEADS + h + 1) * HEAD]
            q, dout = q_ref[:, cols], do_ref[:, cols]
            delta = jnp.sum(dout.astype(F32) * o_ref[:, cols].astype(F32), axis=-1, keepdims=True)
            p = jnp.exp(_dot_nt(q, keys) * ATT_SCALE - l_ref[:, cols][:, :1])
            ds = (p * (_dot_nt(dout, vals) - delta) * ATT_SCALE).astype(BF16)
            dq_ref[:, cols] = _dot(ds, keys).astype(BF16)
            dk_ref[:, cols] += _dot_tn(ds, q)
            dv_ref[:, cols] += _dot_tn(p.astype(BF16), dout)

    wide = M_HEADS * HEAD
    tok = pl.BlockSpec((RB, wide), lambda i: (i, 0))
    q_cols = pl.BlockSpec((RB, wide), lambda i: (i, MQ // M_HEADS))
    slot = pl.BlockSpec((n, wide), lambda i: (0, 0))
    return pl.pallas_call(
        body, name=name, grid=(T // RB,),
        in_specs=[q_cols, _resident(mkv), tok, tok, tok, UNREAD],
        out_specs=[q_cols, slot, slot],
        out_shape=[jax.ShapeDtypeStruct(dqkv.shape, dqkv.dtype),
                   jax.ShapeDtypeStruct((n, wide), F32), jax.ShapeDtypeStruct((n, wide), F32)],
        input_output_aliases={5: 0},
        compiler_params=_params("arbitrary"),
    )(qkv, mkv, do, o, lse, dqkv)


def mem_kv_bwd(mem, g, mem_n, w, dmkv, name):
    n, D = mem.shape

    def body(m_ref, g_ref, mn_ref, w_ref, d_ref, dw_ref, dg_ref):
        d = d_ref[...].astype(BF16)
        dw_ref[...] = _dot_tn(mn_ref[...], d)
        x = m_ref[...]
        dg_ref[...] = jnp.sum(_dot_nt(d, w_ref[...]) * (x * _rstd(x)), axis=0, keepdims=True)

    return pl.pallas_call(
        body, name=name,
        out_shape=[jax.ShapeDtypeStruct(w.shape, F32), jax.ShapeDtypeStruct((1, D), F32)],
        compiler_params=pltpu.CompilerParams(vmem_limit_bytes=VMEM_LIMIT),
    )(mem, g, mem_n, w, dmkv)


def _rms_bwd(dn, f, g):
    r = _rstd(f)
    fhat = f * r
    dfhat = dn * g
    df = r * (dfhat - fhat * jnp.mean(dfhat * fhat, axis=-1, keepdims=True))
    return df, jnp.sum(dn * fhat, axis=0, keepdims=True)


VMEM_LIMIT_FFN_BWD = 60 * 2 ** 20


def ffn_tokens_bwd(dh, f, h_in, gu, g_pre, g_post, w_in, w_out, coef, name, after):
    T, D = dh.shape

    def body(dh_ref, f_ref, h_ref, gu_ref, gpre_ref, gpost_ref, win_ref, wout_ref, _,
             df_ref, dgu_ref, dhin_ref, dgpre_ref, dgpost_ref, dxn_ref):
        i, j = pl.program_id(0), pl.program_id(1)

        @pl.when(j == 0)
        def _():
            @pl.when(i == 0)
            def _():
                dgpre_ref[...] = jnp.zeros_like(dgpre_ref)
                dgpost_ref[...] = jnp.zeros_like(dgpost_ref)

            df, dg_post = _rms_bwd(coef * dh_ref[...], f_ref[...], gpost_ref[...])
            dgpost_ref[...] += dg_post
            df_ref[...] = df.astype(BF16)

        for jj in range(2):
            @pl.when(j == jj)
            def _(jj=jj):
                lo, mid, hi = 2 * jj * FF_T, (2 * jj + 1) * FF_T, (2 * jj + 2) * FF_T
                da = _dot_nt(df_ref[...], wout_ref[jj * FF_T:(jj + 1) * FF_T, :])
                gate = gu_ref[:, :FF_T].astype(F32)
                up = gu_ref[:, FF_T:].astype(F32)
                sig = _sigmoid(gate)
                dgate = (da * up * sig * (1.0 + gate * (1.0 - sig))).astype(BF16)
                dup = (da * gate * sig).astype(BF16)
                dgu_ref[:, :FF_T] = dgate
                dgu_ref[:, FF_T:] = dup
                part = _dot_nt(dgate, win_ref[:, lo:mid]) + _dot_nt(dup, win_ref[:, mid:hi])
                if jj == 0:
                    dxn_ref[...] = part
                else:
                    h = h_ref[...]
                    r = _rstd(h)
                    xhat = h * r
                    dxn = dxn_ref[...] + part
                    dxhat = dxn * gpre_ref[...]
                    dhin_ref[...] = dh_ref[...] + r * (dxhat - xhat * jnp.mean(dxhat * xhat, axis=-1, keepdims=True))
                    dgpre_ref[...] += jnp.sum(dxn * xhat, axis=0, keepdims=True)

    row = pl.BlockSpec((TM, D), lambda i, j: (i, 0))
    wide = pl.BlockSpec((TM, 2 * FF_T), lambda i, j: (i, j))
    vec = pl.BlockSpec((1, D), lambda i, j: (0, 0))
    return pl.pallas_call(
        body, name=name, grid=(T // TM, 2),
        in_specs=[row, row, row, wide, _resident(g_pre), _resident(g_post), _resident(w_in), _resident(w_out),
                  UNREAD],
        out_specs=[row, wide, row, vec, vec],
        out_shape=[jax.ShapeDtypeStruct((T, D), BF16), jax.ShapeDtypeStruct((T, 2 * D_FF), BF16),
                   jax.ShapeDtypeStruct((T, D), F32), jax.ShapeDtypeStruct((1, D), F32),
                   jax.ShapeDtypeStruct((1, D), F32)],
        scratch_shapes=[pltpu.VMEM((TM, D), F32)],
        compiler_params=pltpu.CompilerParams(dimension_semantics=("arbitrary", "arbitrary"),
                                             vmem_limit_bytes=VMEM_LIMIT_FFN_BWD),
    )(dh, f, h_in, gu, g_pre, g_post, w_in, w_out, after)


def mm_nt_norm_bwd(pieces, h_in, dh_out, g, name, after):
    T, D = h_in.shape

    def body(*refs):
        ab = refs[:2 * len(pieces)]
        h_ref, dh_ref, g_ref, _, o_ref, dg_ref = refs[2 * len(pieces):]
        dxn = _dot_nt(ab[0][...], ab[1][...])
        for p in range(1, len(pieces)):
            dxn += _dot_nt(ab[2 * p][...], ab[2 * p + 1][...])
        h = h_ref[...]
        r = _rstd(h)
        xhat = h * r
        dxhat = dxn * g_ref[...]
        o_ref[...] = dh_ref[...] + r * (dxhat - xhat * jnp.mean(dxhat * xhat, axis=-1, keepdims=True))

        @pl.when(pl.program_id(0) == 0)
        def _():
            dg_ref[...] = jnp.zeros_like(dg_ref)

        dg_ref[...] += jnp.sum(dxn * xhat, axis=0, keepdims=True)

    in_specs, args = [], []
    for a, w in pieces:
        in_specs += [pl.BlockSpec((TM, a.shape[1]), lambda i: (i, 0)), _resident(w)]
        args += [a, w]
    row = pl.BlockSpec((TM, D), lambda i: (i, 0))
    return pl.pallas_call(
        body, name=name, grid=(T // TM,),
        in_specs=in_specs + [row, row, _resident(g), UNREAD],
        out_specs=[row, pl.BlockSpec((1, D), lambda i: (0, 0))],
        out_shape=[jax.ShapeDtypeStruct((T, D), F32), jax.ShapeDtypeStruct((1, D), F32)],
        compiler_params=_params("arbitrary"),
    )(*args, h_in, dh_out, g, after)


def gate_merge_out_bwd(dh, f, g, w_out, gt, o_a, o_b, o_m, w_a, w_b, w_m, name, after):
    T = dh.shape[0]
    D = D_MODEL
    branch = ((o_a, w_a), (o_b, w_b), (o_m, w_m))

    def body(dh_ref, f_ref, g_ref, wo_ref, gt_ref, oa_ref, ob_ref, om_ref, wa_ref, wb_ref, wm_ref, _,
             df_ref, dg_ref, dgt_ref, dpa_ref, dpb_ref, dpm_ref, doa_ref, dob_ref, dom_ref, db_ref):
        @pl.when(pl.program_id(0) == 0)
        def _():
            db_ref[...] = jnp.zeros_like(db_ref)
            dg_ref[...] = jnp.zeros_like(dg_ref)

        df, dg = _rms_bwd(dh_ref[...], f_ref[...], g_ref[...])
        dg_ref[...] += dg
        df = df.astype(BF16)
        df_ref[...] = df
        dmf = _dot_nt(df, wo_ref[...])
        for x, (o_ref, w_ref, dp_ref, do_ref) in enumerate(((oa_ref, wa_ref, dpa_ref, doa_ref),
                                                           (ob_ref, wb_ref, dpb_ref, dob_ref),
                                                           (om_ref, wm_ref, dpm_ref, dom_ref))):
            cols = slice(x * D, (x + 1) * D)
            gx = gt_ref[:, cols].astype(F32)
            w = w_ref[...]
            dpre = dmf * _dot(o_ref[...], w) * gx * (1.0 - gx)
            dgt_ref[:, cols] = dpre.astype(BF16)
            db_ref[:, cols] += jnp.sum(dpre, axis=0, keepdims=True)
            dp = (dmf * gx).astype(BF16)
            dp_ref[...] = dp
            do_ref[...] = _dot_nt(dp, w).astype(BF16)

    def rows(width):
        return pl.BlockSpec((TM, width), lambda i: (i, 0))

    widths = [o.shape[1] for o, _ in branch]
    return pl.pallas_call(
        body, name=name, grid=(T // TM,),
        in_specs=[rows(D), rows(D), _resident(g), _resident(w_out), rows(3 * D)] + [rows(k) for k in widths]
                 + [_resident(w) for _, w in branch] + [UNREAD],
        out_specs=[rows(D), pl.BlockSpec((1, D), lambda i: (0, 0)), rows(3 * D), rows(D), rows(D), rows(D)]
                  + [rows(k) for k in widths] + [pl.BlockSpec((1, 3 * D), lambda i: (0, 0))],
        out_shape=[jax.ShapeDtypeStruct((T, D), BF16), jax.ShapeDtypeStruct((1, D), F32),
                   jax.ShapeDtypeStruct((T, 3 * D), BF16)] + [jax.ShapeDtypeStruct((T, D), BF16)] * 3
                  + [jax.ShapeDtypeStruct((T, k), BF16) for k in widths]
                  + [jax.ShapeDtypeStruct((1, 3 * D), F32)],
        compiler_params=_params("arbitrary"),
    )(dh, f, g, w_out, gt, o_a, o_b, o_m, w_a, w_b, w_m, after)


def mm_tn(x, dy, tm, tn, name, shard_major=False, perm=None, slabs=1, after=None, wire=False):
    T, M = x.shape
    N = dy.shape[1]
    tk = min(2048, T)
    perm = perm or (lambda j: j)
    w = tn // slabs

    def body(x_ref, dy_ref, *rest):
        o_ref = rest[-2] if wire else rest[-1]

        @pl.when(pl.program_id(2) == 0)
        def _():
            o_ref[...] = jnp.zeros_like(o_ref)

        acc = _dot_tn(x_ref[...], dy_ref[...])
        if shard_major:
            for s in range(slabs):
                o_ref[s] += acc[:, s * w:(s + 1) * w]
        else:
            o_ref[...] += acc
        if wire:
            @pl.when(pl.program_id(2) == T // tk - 1)
            def _():
                rest[-1][...] = o_ref[...].astype(BF16)

    if shard_major:
        out_spec = pl.BlockSpec((slabs, tm, w), lambda i, j, k: (perm(j), i, 0))
        out_shape = jax.ShapeDtypeStruct((N // w, M, w), F32)
    else:
        out_spec = pl.BlockSpec((tm, tn), lambda i, j, k: (i, j))
        out_shape = jax.ShapeDtypeStruct((M, N), F32)
    return pl.pallas_call(
        body, name=name, grid=(M // tm, N // tn, T // tk),
        in_specs=[pl.BlockSpec((tk, tm), lambda i, j, k: (k, i)),
                  pl.BlockSpec((tk, tn), lambda i, j, k: (k, j))] + ([] if after is None else [UNREAD]),
        out_specs=[out_spec, out_spec] if wire else out_spec,
        out_shape=[out_shape, jax.ShapeDtypeStruct(out_shape.shape, BF16)] if wire else out_shape,
        compiler_params=_params("parallel", "parallel", "arbitrary"),
    )(x, dy, *([] if after is None else [after]))


def rope_tables(T, zero):
    half = HEAD // 2
    inv = ROPE_THETA ** (-jnp.arange(half, dtype=F32) / half)
    ang = (jnp.arange(T).astype(F32) + zero)[:, None] * inv[None, :]
    cos, sin = jnp.cos(ang), jnp.sin(ang)
    return jnp.concatenate([cos, cos], axis=1), jnp.concatenate([-sin, sin], axis=1)


def layer_step(x, mem, target, gains, sinks, b_gate, weights_of, send_grads, zero):
    T = x.shape[0]
    cos, sin_signed = rope_tables(T, zero)
    no_sink = jnp.full((2,), NEG_INF, F32)

    w = dict(weights_of("ffn1_in", cos))
    xn1, gu1, a1 = ffn_in(x, gains["ffn1_norm_pre"], w["ffn1_w_in"], "ffn1_in")
    w.update(weights_of("ffn1_out", xn1))
    f1, h1 = mm_norm_res(a1, w["ffn1_w_out"], x, gains["ffn1_norm_post"], 0.5, "ffn1_out")
    w.update(weights_of("mix", f1))
    u, qkv, gt = mix_in(h1, gains["mix_norm_pre"], w["w_in"], w["w_gate"], b_gate, cos, sin_signed, "mix_in")
    outs, lses = [], []
    for gidx, (window, dil) in enumerate(DIL):
        o_g, l_g = band_fwd(qkv, no_sink, r=dil, base=A_BASE + 6 * gidx, hkv=2, grp=1, max_dist=window // dil,
                            out_dtype=F32, name=f"attn_a{gidx}_fwd")
        outs.append(o_g)
        lses.append(l_g)
    o_a, l_a = merge_groups(outs, lses, "attn_a_merge")
    o_b, l_b = band_fwd(qkv, sinks, r=1, base=B_BASE, hkv=2, grp=2, max_dist=HEAD - 1, out_dtype=BF16,
                        name="attn_b_fwd")
    mem_n, mkv = mem_kv(mem, gains["mem_norm"], w["w_mem_kv"], "mem_kv")
    o_m, l_m = mem_fwd(qkv, mkv, "attn_m_fwd")
    merged, mo, h2 = gate_merge_out(gt, o_a, o_b, o_m, w["w_o_a"], w["w_o_b"], w["w_o_m"], w["w_out"], h1,
                                    gains["mix_norm_post"], "gate_merge_out")
    w.update(weights_of("ffn2", mo))
    xn2, gu2, a2 = ffn_in(h2, gains["ffn2_norm_pre"], w["ffn2_w_in"], "ffn2_in")
    f2, dy, sq = mm_norm_res(a2, w["ffn2_w_out"], h2, gains["ffn2_norm_post"], 0.5, "ffn2_out", target=target)

    grads = {}

    def ffn_bwd(tag, dh_out, f, gu, a, xn, h_in, after):
        df, dgu, dh_in, grads[f"{tag}_norm_pre"], grads[f"{tag}_norm_post"] = ffn_tokens_bwd(
            dh_out, f, h_in, gu, gains[f"{tag}_norm_pre"], gains[f"{tag}_norm_post"], w[f"{tag}_w_in"],
            w[f"{tag}_w_out"], 0.5, f"{tag}_tokens_bwd", after)
        sent = send_grads(f"{tag}_out", {f"{tag}_w_out": mm_tn(
            a, df, FF_T, D_MODEL, f"{tag}_w_out_grad", wire=True)})
        sent = send_grads(f"{tag}_in", {f"{tag}_w_in": mm_tn(
            xn, dgu, D_MODEL, FF_T, f"{tag}_w_in_grad", shard_major=True, perm=_ffn_perm, after=sent, wire=True)})
        return dh_in, sent

    dh2, sent = ffn_bwd("ffn2", dy, f2, gu2, a2, xn2, h2, dy)

    mix = {}
    dmo, grads["mix_norm_post"], dgt, dpa, dpb, dpm, do_a, do_b, do_m, grads["b_gate"] = gate_merge_out_bwd(
        dh2, mo, gains["mix_norm_post"], w["w_out"], gt, o_a, o_b, o_m, w["w_o_a"], w["w_o_b"], w["w_o_m"],
        "gate_merge_out_bwd", sent)
    mix["w_out"] = mm_tn(merged, dmo, D_MODEL, D_MODEL, "w_out_grad", wire=True)
    mix["w_o_a"] = mm_tn(o_a, dpa, o_a.shape[1], D_MODEL, "w_o_a_grad")
    mix["w_o_b"] = mm_tn(o_b, dpb, o_b.shape[1], D_MODEL, "w_o_b_grad")
    mix["w_o_m"] = mm_tn(o_m, dpm, o_m.shape[1], D_MODEL, "w_o_m_grad")

    dqkv = lax.empty(qkv.shape, qkv.dtype)
    for gidx, (window, dil) in enumerate(DIL):
        dqkv, = band_bwd(qkv, dqkv, do_a, o_a, l_a, cos, sin_signed, None, r=dil, base=A_BASE + 6 * gidx, hkv=2,
                         grp=1, max_dist=window // dil, name=f"attn_a{gidx}_bwd")
    dqkv, dsink = band_bwd(qkv, dqkv, do_b, o_b, l_b, cos, sin_signed, sinks, r=1, base=B_BASE, hkv=2, grp=2,
                           max_dist=HEAD - 1, name="attn_b_bwd")
    grads["sinks"] = -dsink[:, ::8, 0].reshape(1, 4)
    dqkv, dmk, dmv = mem_bwd(qkv, dqkv, mkv, do_m, o_m, l_m, "attn_m_bwd")
    mix["w_mem_kv"], grads["mem_norm"] = mem_kv_bwd(
        mem, gains["mem_norm"], mem_n, w["w_mem_kv"], jnp.concatenate([dmk, dmv], axis=1), "mem_kv_bwd")

    mix["w_in"] = mm_tn(u, dqkv, D_MODEL, 1280, "w_in_grad")
    mix["w_gate"] = mm_tn(u, dgt, D_MODEL, 1536, "w_gate_grad", shard_major=True, slabs=2, wire=True)
    sent = send_grads("mix", mix)
    dh1, grads["mix_norm_pre"] = mm_nt_norm_bwd(
        [(dqkv, w["w_in"]), (dgt, w["w_gate"])], h1, dh2, gains["mix_norm_pre"], "mix_in_bwd", sent)

    dx, _ = ffn_bwd("ffn1", dh1, f1, gu1, a1, xn1, x, dh1)
    return sq, dx, grads


def _place():
    return lax.axis_index("x"), lax.axis_index("y"), lax.axis_index("c")


def _other_chips(x, y):
    return [(1 - x, y), (x, 1 - y), (1 - x, 1 - y)]


def _hbm(n):
    return [pl.BlockSpec(memory_space=pltpu.HBM)] * n


SEM = pl.BlockSpec(memory_space=pltpu.SEMAPHORE)
SIDE_EFFECT = pltpu.SideEffectType.DATAFLOW_SIDE_EFFECTING


def _chip_copy(src, land, sems, i, j, dst_slot, scatter):
    x, y, c = _place()
    px, py = _other_chips(x, y)[j]
    send_sems, recv_sems = sems
    return pltpu.make_async_remote_copy(
        src_ref=src[i].at[2 * px + py] if scatter else src[i], dst_ref=land[i].at[dst_slot],
        send_sem=send_sems.at[3 * i + j], recv_sem=recv_sems.at[3 * i + j],
        device_id=(px, py, c), device_id_type=MESH)


def chip_copies_start(srcs, lands, groups, scatter, name):
    n = len(srcs)

    def body(*refs):
        src, land = refs[:n], refs[n:2 * n]
        sems = refs[2 * n:2 * n + 2 * len(groups)]
        token = refs[-1]
        x, y, _ = _place()
        for g, members in enumerate(groups):
            part = ([src[i] for i in members], [land[i] for i in members])
            for t in range(len(members)):
                for j in range(3):
                    _chip_copy(*part, sems[2 * g:2 * g + 2], t, j, 2 * x + y, scatter).start()
        token[...] = jnp.zeros_like(token)

    sem_shapes = [pltpu.SemaphoreType.DMA((3 * len(m),)) for m in groups for _ in range(2)]
    thru = [pltpu.HBM(a.shape, a.dtype) for a in (*srcs, *lands)]
    res = pl.pallas_call(
        body, name=name,
        out_shape=(*sem_shapes, *thru, jax.ShapeDtypeStruct((8, 128), F32)),
        in_specs=_hbm(2 * n),
        out_specs=(*[SEM] * len(sem_shapes), *_hbm(2 * n), pl.BlockSpec(memory_space=pltpu.VMEM)),
        input_output_aliases={i: len(sem_shapes) + i for i in range(2 * n)},
        compiler_params=pltpu.CompilerParams(has_side_effects=SIDE_EFFECT),
    )(*[pltpu.with_memory_space_constraint(a, pltpu.HBM) for a in (*srcs, *lands)])
    k = len(sem_shapes)
    sems = [tuple(res[2 * g:2 * g + 2]) for g in range(len(groups))]
    return sems, list(res[k:k + n]), list(res[k + n:k + 2 * n]), res[-1]


def chip_copies_wait(srcs, lands, sems, after, scatter, name):
    n = len(srcs)

    def body(*refs):
        src, land = refs[:n], refs[n:2 * n]
        pair = refs[2 * n:2 * n + 2]
        x, y, _ = _place()
        for i in range(n):
            for j, (px, py) in enumerate(_other_chips(x, y)):
                copy = _chip_copy(src, land, pair, i, j, 2 * px + py, scatter)
                copy.wait_send()
                copy.wait_recv()

    res = pl.pallas_call(
        body, name=name,
        out_shape=[pltpu.HBM(a.shape, a.dtype) for a in (*srcs, *lands)],
        in_specs=[*_hbm(2 * n), SEM, SEM, pl.BlockSpec(memory_space=pl.ANY)],
        out_specs=_hbm(2 * n),
        input_output_aliases={i: i for i in range(2 * n)},
        compiler_params=pltpu.CompilerParams(has_side_effects=SIDE_EFFECT),
    )(*srcs, *lands, *sems, after)
    return list(res[n:])


def small_all_gather(small, name):
    flips = [(fx, fy, fc) for fx in (0, 1) for fy in (0, 1) for fc in (0, 1)][1:]

    def body(in_ref, out_ref, send_sems, recv_sems, local_sem):
        x, y, c = _place()
        me = 4 * x + 2 * y + c

        def copy(k, slot):
            fx, fy, fc = flips[k]
            return pltpu.make_async_remote_copy(
                src_ref=in_ref, dst_ref=out_ref.at[slot], send_sem=send_sems.at[k], recv_sem=recv_sems.at[k],
                device_id=(x ^ fx, y ^ fy, c ^ fc), device_id_type=MESH)

        local = pltpu.make_async_copy(in_ref, out_ref.at[me], local_sem)
        local.start()
        for k in range(len(flips)):
            copy(k, me).start()
        for k, (fx, fy, fc) in enumerate(flips):
            copy(k, 4 * (x ^ fx) + 2 * (y ^ fy) + (c ^ fc)).wait()
        local.wait()

    return pl.pallas_call(
        body, name=name, in_specs=_hbm(1), out_specs=_hbm(1)[0],
        out_shape=jax.ShapeDtypeStruct((N_DEV,) + small.shape, small.dtype),
        scratch_shapes=[pltpu.SemaphoreType.DMA((len(flips),)), pltpu.SemaphoreType.DMA((len(flips),)),
                        pltpu.SemaphoreType.DMA],
    )(small)


def sibling_exchange(parts, name):
    n = len(parts)

    def body(*refs):
        ins, outs = refs[:n], refs[n:2 * n]
        send_sems, recv_sems = refs[2 * n:]
        x, y, c = _place()
        copies = [pltpu.make_async_remote_copy(
            src_ref=ins[i], dst_ref=outs[i], send_sem=send_sems.at[i], recv_sem=recv_sems.at[i],
            device_id=(x, y, 1 - c), device_id_type=MESH) for i in range(n)]
        for cp in copies:
            cp.start()
        for cp in copies:
            cp.wait()

    return pl.pallas_call(
        body, name=name, in_specs=_hbm(n), out_specs=_hbm(n),
        out_shape=[jax.ShapeDtypeStruct(p.shape, p.dtype) for p in parts],
        scratch_shapes=[pltpu.SemaphoreType.DMA((n,)), pltpu.SemaphoreType.DMA((n,))],
    )(*parts)


def _row_tile(rows):
    for t in (256, 176, 128, 64, 32, 16, 8):
        if rows % t == 0:
            return t
    return rows


def chip_partial_sum(me, own_sm, recv, name):
    _, rows, cols = own_sm.shape
    tr = _row_tile(rows)

    def body(me_ref, own_ref, r0, r1, r2, r3, o_ref):
        acc = jnp.zeros((tr, cols), F32)
        for s, r_ref in enumerate((r0, r1, r2, r3)):
            acc = acc + jnp.where(me_ref[0] == s, own_ref[...], r_ref[...].astype(F32))
        o_ref[...] = acc

    def slot(s):
        return pl.BlockSpec((None, tr, cols), lambda i, me_ref, s=s: (s, i, 0))

    return pl.pallas_call(
        body, name=name,
        grid_spec=pltpu.PrefetchScalarGridSpec(
            num_scalar_prefetch=1, grid=(rows // tr,),
            in_specs=[pl.BlockSpec((None, tr, cols), lambda i, me_ref: (me_ref[0], i, 0))] + [slot(s) for s in range(4)],
            out_specs=pl.BlockSpec((tr, cols), lambda i, me_ref: (i, 0))),
        out_shape=jax.ShapeDtypeStruct((rows, cols), F32),
        compiler_params=_params("parallel"),
    )(me, own_sm, recv, recv, recv, recv)


def _adamw(w, g, m, v):
    m = ADAM_B1 * m + (1.0 - ADAM_B1) * g
    v = ADAM_B2 * v + (1.0 - ADAM_B2) * (g * g)
    m_hat = m / (1.0 - ADAM_B1 ** ADAM_STEP)
    v_hat = v / (1.0 - ADAM_B2 ** ADAM_STEP)
    delta = -ADAM_LR * (m_hat / (jnp.sqrt(v_hat) + ADAM_EPS) + ADAM_WD * w)
    return delta, m, v


def adamw_pair(part, sib, w, m, v, name):
    rows, cols = w.shape
    tr = _row_tile(rows)

    def body(p_ref, s_ref, w_ref, m_ref, v_ref, g_ref, d_ref, nm_ref, nv_ref):
        g = p_ref[...] + s_ref[...]
        g_ref[...] = g
        d_ref[...], nm_ref[...], nv_ref[...] = _adamw(w_ref[...], g, m_ref[...], v_ref[...])

    spec = pl.BlockSpec((tr, cols), lambda i: (i, 0))
    return pl.pallas_call(
        body, name=name, grid=(rows // tr,), in_specs=[spec] * 5, out_specs=[spec] * 4,
        out_shape=[jax.ShapeDtypeStruct((rows, cols), F32)] * 4,
        compiler_params=_params("parallel"),
    )(part, sib, w, m, v)


def adamw_small(g_all, w, m, v, name):
    def body(ga_ref, w_ref, m_ref, v_ref, g_ref, d_ref, nm_ref, nv_ref):
        g = ga_ref[0]
        for k in range(1, N_DEV):
            g = g + ga_ref[k]
        g_ref[...] = g
        d_ref[...], nm_ref[...], nv_ref[...] = _adamw(w_ref[...], g, m_ref[...], v_ref[...])

    return pl.pallas_call(
        body, name=name, out_shape=[jax.ShapeDtypeStruct(w.shape, F32)] * 4,
    )(g_all, w, m, v)


WEIGHTS = ("ffn1_norm_pre", "ffn1_w_in", "ffn1_w_out", "ffn1_norm_post", "mix_norm_pre", "w_in", "sinks",
           "mem_norm", "w_mem_kv", "w_gate", "b_gate", "w_o_a", "w_o_b", "w_o_m", "w_out", "mix_norm_post",
           "ffn2_norm_pre", "ffn2_w_in", "ffn2_w_out", "ffn2_norm_post")
BIG = ("ffn1_w_in", "ffn1_w_out", "w_in", "w_mem_kv", "w_gate", "w_o_a", "w_o_b", "w_o_m", "w_out",
       "ffn2_w_in", "ffn2_w_out")
GATHER_ORDER = ("ffn1_in", "ffn1_out", "mix", "ffn2")
GATHER_GROUPS = {"ffn1_in": ("ffn1_w_in",), "ffn1_out": ("ffn1_w_out",),
                 "mix": ("w_in", "w_gate", "w_mem_kv", "w_o_a", "w_o_b", "w_o_m", "w_out"),
                 "ffn2": ("ffn2_w_in", "ffn2_w_out")}
GROUPS = {"ffn1_in": ("ffn1_w_in",), "ffn1_out": ("ffn1_w_out",),
          "mix": ("w_in", "w_gate", "w_mem_kv", "w_o_a", "w_o_b", "w_o_m", "w_out"),
          "ffn2_in": ("ffn2_w_in",), "ffn2_out": ("ffn2_w_out",)}
COLUMN_SHARDED = ("ffn1_w_in", "ffn2_w_in", "w_in", "w_gate", "w_o_a", "w_o_b", "w_o_m")
KEPT_SHARD_MAJOR = ("ffn1_w_in", "ffn2_w_in", "w_gate")
GAINS = ("ffn1_norm_pre", "ffn1_norm_post", "mix_norm_pre", "mem_norm", "mix_norm_post", "ffn2_norm_pre",
         "ffn2_norm_post")
SMALL_ROWS = 16


def _pack_small(t):
    sinks = jnp.pad(t["sinks"], ((0, 0), (0, D_MODEL - t["sinks"].shape[1])))
    rows = [t[k] for k in GAINS] + [t["b_gate"].reshape(3, D_MODEL), sinks]
    packed = jnp.concatenate(rows, axis=0)
    return jnp.pad(packed, ((0, SMALL_ROWS - packed.shape[0]), (0, 0)))


def _unpack_small(p):
    out = {k: p[i:i + 1] for i, k in enumerate(GAINS)}
    out["b_gate"] = p[7:10].reshape(1, 3 * D_MODEL)
    out["sinks"] = p[10:11, :4]
    return out


def kernel(x, mem, ffn1_norm_pre, ffn1_w_in, ffn1_w_out, ffn1_norm_post, mix_norm_pre, w_in, sinks, mem_norm, w_mem_kv, w_gate, b_gate, w_o_a, w_o_b, w_o_m, w_out, mix_norm_post, ffn2_norm_pre, ffn2_w_in, ffn2_w_out, ffn2_norm_post, loss_target, m_ffn1_norm_pre, m_ffn1_w_in, m_ffn1_w_out, m_ffn1_norm_post, m_mix_norm_pre, m_w_in, m_sinks, m_mem_norm, m_w_mem_kv, m_w_gate, m_b_gate, m_w_o_a, m_w_o_b, m_w_o_m, m_w_out, m_mix_norm_post, m_ffn2_norm_pre, m_ffn2_w_in, m_ffn2_w_out, m_ffn2_norm_post, v_ffn1_norm_pre, v_ffn1_w_in, v_ffn1_w_out, v_ffn1_norm_post, v_mix_norm_pre, v_w_in, v_sinks, v_mem_norm, v_w_mem_kv, v_w_gate, v_b_gate, v_w_o_a, v_w_o_b, v_w_o_m, v_w_out, v_mix_norm_post, v_ffn2_norm_pre, v_ffn2_w_in, v_ffn2_w_out, v_ffn2_norm_post):
    given = dict(locals())
    wt = {k: given[k] for k in WEIGHTS}
    mom = {k: given["m_" + k] for k in WEIGHTS}
    var = {k: given["v_" + k] for k in WEIGHTS}
    chip = (2 * lax.axis_index("x") + lax.axis_index("y")).astype(jnp.int32)
    me = chip.reshape(1)

    def landing_zone(own):
        return lax.dynamic_update_slice_in_dim(lax.empty((N_CHIPS,) + own.shape, own.dtype), own[None], chip, 0)

    shards = [wt[k][0].astype(BF16) for k in BIG]
    members = [[BIG.index(k) for k in GATHER_GROUPS[g]] for g in GATHER_ORDER]
    sems, shards, lands, token = chip_copies_start(
        shards, [landing_zone(s) for s in shards], members, False, "weight_gather_start")

    def weights_of(group, after):
        idx = members[GATHER_ORDER.index(group)]
        got = chip_copies_wait([shards[i] for i in idx], [lands[i] for i in idx], sems[GATHER_ORDER.index(group)],
                               after, False, f"weight_gather_wait_{group}")
        full = {}
        for k, g in zip(GATHER_GROUPS[group], got):
            if k in COLUMN_SHARDED:
                if k in ("ffn1_w_in", "ffn2_w_in"):
                    g = jnp.stack([g[0], g[2], g[1], g[3]])
                full[k] = jnp.swapaxes(g, 0, 1).reshape(g.shape[1], N_CHIPS * g.shape[2])
                if k == "w_in":
                    full[k] = to_kernel_heads(full[k])
            else:
                full[k] = g.reshape(N_CHIPS * g.shape[1], g.shape[2])
        return full

    in_flight = {}

    def send_grads(group, grads):
        def shard_major(k, g):
            if k in KEPT_SHARD_MAJOR:
                return g
            if k in COLUMN_SHARDED:
                return jnp.swapaxes(g.reshape(g.shape[0], N_CHIPS, g.shape[1] // N_CHIPS), 0, 1)
            return g.reshape(N_CHIPS, g.shape[0] // N_CHIPS, g.shape[1])

        own, wire = [], []
        for k in GROUPS[group]:
            g, rounded = grads[k] if isinstance(grads[k], (tuple, list)) else (grads[k], None)
            g = shard_major(k, from_kernel_heads(g) if k == "w_in" else g)
            own.append(g)
            wire.append(g.astype(BF16) if rounded is None else shard_major(k, rounded))
        zones = [landing_zone(lax.dynamic_index_in_dim(b, chip, 0, keepdims=False)) for b in wire]
        pair, wire, zones, sent = chip_copies_start(
            wire, zones, [list(range(len(wire)))], True, f"grad_scatter_start_{group}")
        in_flight[group] = (own, wire, zones, pair[0], sent)
        return sent

    gains = {k: wt[k] for k in GAINS}
    sq, dx, grads = layer_step(
        x[0], mem[0], loss_target[0], gains, sinks[0], b_gate, weights_of, send_grads, token[0, 0])
    loss = lax.psum(0.5 * sq[0, 0] / D_MODEL, ("x", "y", "c"))

    res = {}
    after = in_flight["ffn1_in"][4]
    for stage in (("ffn2_out", "ffn2_in", "mix", "ffn1_out"), ("ffn1_in",)):
        names, parts = [], []
        for group in stage:
            own, wire, zones, pair, _ = in_flight[group]
            received = chip_copies_wait(wire, zones, pair, after, True, f"grad_scatter_wait_{group}")
            for k, g, r in zip(GROUPS[group], own, received):
                names.append(k)
                parts.append(chip_partial_sum(me, g, r, f"{k}_chip_sum"))
        sibs = sibling_exchange(parts, f"sibling_exchange_{stage[-1]}")
        for k, p, s in zip(names, parts, sibs):
            res[k] = [t[None] for t in adamw_pair(p, s, wt[k][0], mom[k][0], var[k][0], f"{k}_adamw")]
        after = res[names[-1]][0]
    small_all = small_all_gather(_pack_small(grads), "small_grad_gather")
    packed = adamw_small(small_all, _pack_small(wt), _pack_small(mom), _pack_small(var), "small_adamw")
    for idx, p in enumerate(packed):
        for k, t in _unpack_small(p).items():
            res.setdefault(k, [None] * 4)[idx] = t

    return (loss, dx[None], *[res[k][0] for k in WEIGHTS], *[res[k][1] for k in WEIGHTS],
            *[res[k][2] for k in WEIGHTS], *[res[k][3] for k in WEIGHTS])
```

```python
import functools

import jax
import jax.numpy as jnp
from jax import lax
from jax.experimental import pallas as pl
from jax.experimental.pallas import tpu as pltpu

F32 = jnp.float32
BF16 = jnp.bfloat16

D_MODEL = 1024
D_FF = 2816
HEAD = 128
N_CHIPS = 4
N_DEV = 8
EPS = 1e-6
NEG_INF = -1e30
ROPE_THETA = 10000.0
ATT_SCALE = HEAD ** -0.5

ADAM_LR = 0.001
ADAM_B1 = 0.9
ADAM_B2 = 0.999
ADAM_EPS = 1e-08
ADAM_WD = 0.01
ADAM_STEP = 10

VMEM_LIMIT = 52 * 2 ** 20
VMEM_LIMIT_LARGE = 60 * 2 ** 20
MESH = pl.DeviceIdType.MESH

QKV_W = 3840
DIL = ((128, 1), (512, 4), (2048, 16))
B_BASE, MQ, A_BASE = 0, 8, 12
_AQ, _AK, _AV, _BQ, _BK, _BV, _MQ = 0, 6, 12, 18, 22, 24, 26
HEAD_ORDER = tuple(
    [h for j in range(2) for h in (_BQ + 2 * j, _BQ + 2 * j + 1, _BK + j, _BV + j)]
    + [_MQ + i for i in range(4)]
    + [h for g in range(3) for i in range(2) for h in (_AQ + 2 * g + i, _AK + 2 * g + i, _AV + 2 * g + i)])
ROTARY_HEADS = tuple(p for p, h in enumerate(HEAD_ORDER) if h < _AV or _BQ <= h < _BV)


def to_kernel_heads(w):
    return jnp.concatenate([w[..., h * HEAD:(h + 1) * HEAD] for h in HEAD_ORDER], axis=-1)


def from_kernel_heads(w):
    place = {h: p for p, h in enumerate(HEAD_ORDER)}
    return jnp.concatenate([w[..., place[h] * HEAD:(place[h] + 1) * HEAD] for h in range(len(HEAD_ORDER))], axis=-1)

TM = 512
FF_T = D_FF // 2


def _params(*sem):
    return pltpu.CompilerParams(dimension_semantics=sem, vmem_limit_bytes=VMEM_LIMIT)


def _dot(a, b):
    return jnp.dot(a, b, preferred_element_type=F32)


def _dot_nt(a, b):
    return lax.dot_general(a, b, (((1,), (1,)), ((), ())), preferred_element_type=F32)


def _dot_tn(a, b):
    return lax.dot_general(a, b, (((0,), (0,)), ((), ())), preferred_element_type=F32)


def _rstd(x):
    return lax.rsqrt(jnp.mean(x * x, axis=-1, keepdims=True) + EPS)


def _sigmoid(x):
    return 0.5 * jnp.tanh(0.5 * x) + 0.5


def _ffn_perm(k):
    return (k % 2) * 2 + k // 2


UNREAD = pl.BlockSpec(memory_space=pl.ANY)


def _resident(arr):
    return pl.BlockSpec(arr.shape, lambda *_: (0,) * arr.ndim, pipeline_mode=pl.Buffered(1))


def ffn_in(h, g, w, name):
    T, D = h.shape

    def body(h_ref, g_ref, w_ref, xn_ref, gu_ref, a_ref):
        x = h_ref[...]
        xn = (x * _rstd(x) * g_ref[...]).astype(BF16)
        xn_ref[...] = xn
        for j in range(2):
            gu = _dot(xn, w_ref[:, j * 2 * FF_T:(j + 1) * 2 * FF_T])
            gu_ref[:, j * 2 * FF_T:(j + 1) * 2 * FF_T] = gu.astype(BF16)
            gate, up = gu[:, :FF_T], gu[:, FF_T:]
            a_ref[:, j * FF_T:(j + 1) * FF_T] = (gate * _sigmoid(gate) * up).astype(BF16)

    def rows(width):
        return pl.BlockSpec((TM, width), lambda i: (i, 0))

    return pl.pallas_call(
        body, name=name,
        grid=(T // TM,),
        in_specs=[rows(D), _resident(g), _resident(w)],
        out_specs=[rows(D), rows(2 * D_FF), rows(D_FF)],
        out_shape=[jax.ShapeDtypeStruct((T, D), BF16),
                   jax.ShapeDtypeStruct((T, 2 * D_FF), BF16),
                   jax.ShapeDtypeStruct((T, D_FF), BF16)],
        compiler_params=_params("parallel"),
    )(h, g, w)


def mm_norm_res(a, w, h_in, g, coef, name, target=None):
    T, K = a.shape
    D = w.shape[1]
    final = target is not None

    def body(*refs):
        if final:
            a_ref, w_ref, h_ref, g_ref, t_ref, f_ref, o_ref, l_ref = refs
        else:
            a_ref, w_ref, h_ref, g_ref, f_ref, o_ref = refs
        f = _dot(a_ref[...], w_ref[...])
        f_ref[...] = f
        y = h_ref[...] + coef * (f * _rstd(f) * g_ref[...])
        if final:
            err = y - t_ref[...]
            o_ref[...] = err * (1.0 / D)

            @pl.when(pl.program_id(0) == 0)
            def _():
                l_ref[...] = jnp.zeros_like(l_ref)

            l_ref[...] += jnp.sum(err * err)
        else:
            o_ref[...] = y

    row = pl.BlockSpec((TM, D), lambda i: (i, 0))
    in_specs = [pl.BlockSpec((TM, K), lambda i: (i, 0)),
                _resident(w),
                row, pl.BlockSpec((1, D), lambda i: (0, 0))]
    out_specs = [row, row]
    out_shape = [jax.ShapeDtypeStruct((T, D), F32), jax.ShapeDtypeStruct((T, D), F32)]
    args = [a, w, h_in, g]
    if final:
        in_specs.append(row)
        args.append(target)
        out_specs.append(pl.BlockSpec((8, 128), lambda i: (0, 0)))
        out_shape.append(jax.ShapeDtypeStruct((8, 128), F32))
    return pl.pallas_call(
        body, name=name, grid=(T // TM,), in_specs=in_specs, out_specs=out_specs, out_shape=out_shape,
        compiler_params=_params("arbitrary"),
    )(*args)


def _rope(x, cos, sin_signed):
    return x * cos + pltpu.roll(x, HEAD // 2, axis=1) * sin_signed


def _unrope(x, cos, sin_signed):
    return x * cos - pltpu.roll(x, HEAD // 2, axis=1) * sin_signed


def mix_in(h, g, w, w_gate, b_gate, cos, sin_signed, name):
    T, D = h.shape
    tn = 768

    def body(h_ref, g_ref, w_ref, wg_ref, b_ref, c_ref, s_ref, u_ref, o_ref, gt_ref):
        x = h_ref[...]
        u = (x * _rstd(x) * g_ref[...]).astype(BF16)
        u_ref[...] = u
        c, s = c_ref[...], s_ref[...]
        for j in range(QKV_W // tn):
            acc = _dot(u, w_ref[:, j * tn:(j + 1) * tn])
            for hd in range(tn // HEAD):
                head = j * (tn // HEAD) + hd
                part = acc[:, hd * HEAD:(hd + 1) * HEAD]
                if head in ROTARY_HEADS:
                    part = _rope(part, c, s)
                o_ref[:, head * HEAD:(head + 1) * HEAD] = part.astype(BF16)
        for j in range(w_gate.shape[1] // tn):
            cols = slice(j * tn, (j + 1) * tn)
            gt_ref[:, cols] = _sigmoid(_dot(u, wg_ref[:, cols]) + b_ref[:, cols]).astype(BF16)

    def rows(width):
        return pl.BlockSpec((TM, width), lambda i: (i, 0))

    return pl.pallas_call(
        body, name=name,
        grid=(T // TM,),
        in_specs=[rows(D), _resident(g), _resident(w), _resident(w_gate), _resident(b_gate), rows(HEAD), rows(HEAD)],
        out_specs=[rows(D), rows(QKV_W), rows(w_gate.shape[1])],
        out_shape=[jax.ShapeDtypeStruct((T, D), BF16), jax.ShapeDtypeStruct((T, QKV_W), BF16),
                   jax.ShapeDtypeStruct((T, w_gate.shape[1]), BF16)],
        compiler_params=_params("parallel"),
    )(h, g, w, w_gate, b_gate, cos, sin_signed)


def gate_merge_out(gt, o_a, o_b, o_m, w_a, w_b, w_m, w_out, h_in, g, name):
    T = gt.shape[0]
    D = D_MODEL

    def body(gt_ref, oa_ref, ob_ref, om_ref, wa_ref, wb_ref, wm_ref, wo_ref, h_ref, g_ref, m_ref, f_ref, o_ref):
        acc = gt_ref[:, :D].astype(F32) * _dot(oa_ref[...], wa_ref[...])
        acc += gt_ref[:, D:2 * D].astype(F32) * _dot(ob_ref[...], wb_ref[...])
        acc += gt_ref[:, 2 * D:].astype(F32) * _dot(om_ref[...], wm_ref[...])
        merged = acc.astype(BF16)
        m_ref[...] = merged
        f = _dot(merged, wo_ref[...])
        f_ref[...] = f
        o_ref[...] = h_ref[...] + f * _rstd(f) * g_ref[...]

    def rows(width):
        return pl.BlockSpec((TM, width), lambda i: (i, 0))

    return pl.pallas_call(
        body, name=name, grid=(T // TM,),
        in_specs=[rows(3 * D), rows(o_a.shape[1]), rows(o_b.shape[1]), rows(o_m.shape[1]),
                  _resident(w_a), _resident(w_b), _resident(w_m), _resident(w_out), rows(D), _resident(g)],
        out_specs=[rows(D), rows(D), rows(D)],
        out_shape=[jax.ShapeDtypeStruct((T, D), BF16), jax.ShapeDtypeStruct((T, D), F32),
                   jax.ShapeDtypeStruct((T, D), F32)],
        compiler_params=_params("parallel"),
    )(gt, o_a, o_b, o_m, w_a, w_b, w_m, w_out, h_in, g)


def _band_rows(start, r):
    return pl.ds(start, HEAD) if r == 1 else pl.ds(start, HEAD, stride=r)


def _band_mask(max_dist, first_has_prev):
    row = lax.broadcasted_iota(jnp.int32, (HEAD, 2 * HEAD), 0)
    col = lax.broadcasted_iota(jnp.int32, (HEAD, 2 * HEAD), 1)
    dist = row + HEAD - col
    band = (dist >= 0) & (dist <= max_dist)
    return band, band & (col >= jnp.where(first_has_prev, 0, HEAD))


def _stack(parts):
    return parts[0] if len(parts) == 1 else jnp.concatenate(parts, axis=0)


def _band_specs(BT, SB, nsub, base, grp):
    stride = grp + 2

    def cur(off, width):
        return pl.BlockSpec((BT, width * HEAD), lambda h, i: (i, (base + h * stride + off) // width))

    def prev(off):
        return pl.BlockSpec((SB, HEAD), lambda h, i: (jnp.maximum(i * nsub - 1, 0), base + h * stride + off))

    return cur(0, grp), cur(grp, 1), prev(grp), cur(grp + 1, 1), prev(grp + 1)


def band_fwd(qkv, sinks, *, r, base, hkv, grp, max_dist, out_dtype, name):
    T, W = qkv.shape
    SB = HEAD * r
    BT = min(2048, T)
    nsub, nib = BT // SB, T // BT
    hq = hkv * grp
    heads = [slice(g * HEAD, (g + 1) * HEAD) for g in range(grp)]

    def body(sink_ref, q_ref, kc_ref, kp_ref, vc_ref, vp_ref, o_ref, l_ref, qf, kf, vf):
        kvh, ib = pl.program_id(0), pl.program_id(1)
        qf[...] = q_ref[...].astype(F32)
        kf[:SB] = kp_ref[...].astype(F32)
        kf[SB:] = kc_ref[...].astype(F32)
        vf[:SB] = vp_ref[...].astype(F32)
        vf[SB:] = vc_ref[...].astype(F32)
        band, band_first = _band_mask(max_dist, ib > 0)
        for c in range(r):
            k_old, v_old = kf[_band_rows(c, r)], vf[_band_rows(c, r)]
            for j in range(nsub):
                mask = band_first if j == 0 else band
                rows = _band_rows(j * SB + c, r)
                k_own, v_own = kf[_band_rows((j + 1) * SB + c, r)], vf[_band_rows((j + 1) * SB + c, r)]
                kcat = jnp.concatenate([k_old, k_own], axis=0).astype(BF16)
                vcat = jnp.concatenate([v_old, v_own], axis=0).astype(BF16)
                k_old, v_old = k_own, v_own
                s_all = _dot_nt(_stack([qf[rows, cols] for cols in heads]).astype(BF16), kcat) * ATT_SCALE
                probs, tots = [], []
                for g, cols in enumerate(heads):
                    s = jnp.where(mask, s_all[cols], NEG_INF)
                    sk = sink_ref[kvh * grp + g]
                    m = jnp.maximum(jnp.max(s, axis=-1, keepdims=True), sk)
                    p = jnp.exp(s - m)
                    tot = jnp.sum(p, axis=-1, keepdims=True) + jnp.exp(sk - m)
                    probs.append(p.astype(BF16))
                    tots.append(tot)
                    l_ref[rows, cols] = jnp.broadcast_to(m + jnp.log(tot), (HEAD, HEAD))
                o_all = _dot(_stack(probs), vcat)
                for g, cols in enumerate(heads):
                    o_ref[rows, cols] = (o_all[cols] / tots[g]).astype(out_dtype)

    out_spec = pl.BlockSpec((BT, grp * HEAD), lambda h, i: (i, h))
    return pl.pallas_call(
        body, name=name, grid=(hkv, nib),
        in_specs=[pl.BlockSpec(memory_space=pltpu.SMEM), *_band_specs(BT, SB, nsub, base, grp)],
        out_specs=[out_spec, out_spec],
        out_shape=[jax.ShapeDtypeStruct((T, hq * HEAD), out_dtype), jax.ShapeDtypeStruct((T, hq * HEAD), F32)],
        scratch_shapes=[pltpu.VMEM((BT, grp * HEAD), F32), pltpu.VMEM((SB + BT, HEAD), F32),
                        pltpu.VMEM((SB + BT, HEAD), F32)],
        compiler_params=_params("parallel", "arbitrary"),
    )(sinks, qkv, qkv, qkv, qkv, qkv)


def band_bwd(qkv, dqkv, do, o, lse, cos, sin_signed, sinks, *, r, base, hkv, grp, max_dist, name):
    T, W = qkv.shape
    SB = HEAD * r
    BT = min(max(2048, 2 * SB), T)
    nsub, nib = BT // SB, T // BT
    nblk = T // SB
    with_sink = sinks is not None
    heads = [slice(g * HEAD, (g + 1) * HEAD) for g in range(grp)]

    def body(*refs):
        if with_sink:
            sink_ref, refs = refs[0], refs[1:]
        (q_ref, kc_ref, kp_ref, vc_ref, vp_ref, qn_ref, do_ref, don_ref, o_ref, on_ref, l_ref, ln_ref,
         c_ref, s_ref, _) = refs[:15]
        out_ref = refs[15]
        ds_ref = refs[16] if with_sink else None
        qf, dof, of, kf, vf, dqf, dkf, dvf = refs[-8:]
        kvh, ib = pl.program_id(0), pl.program_id(1)
        for buf, cur_ref, nxt_ref in ((qf, q_ref, qn_ref), (dof, do_ref, don_ref), (of, o_ref, on_ref)):
            buf[:BT] = cur_ref[...].astype(F32)
            buf[BT:] = nxt_ref[...].astype(F32)
        kf[:SB] = kp_ref[...].astype(F32)
        kf[SB:] = kc_ref[...].astype(F32)
        vf[:SB] = vp_ref[...].astype(F32)
        vf[SB:] = vc_ref[...].astype(F32)
        band, band_first = _band_mask(max_dist, ib > 0)
        if with_sink:
            @pl.when(ib == 0)
            def _():
                ds_ref[...] = jnp.zeros_like(ds_ref)

        def grads(rows, logzs, keys, vals, mask):
            q = _stack([qf[rows, cols] for cols in heads]).astype(BF16)
            dout = _stack([dof[rows, cols] for cols in heads]).astype(BF16)
            s_all = _dot_nt(q, keys) * ATT_SCALE
            dp_all = _dot_nt(dout, vals)
            probs, dss, deltas = [], [], []
            for g, cols in enumerate(heads):
                delta = jnp.sum(dof[rows, cols] * of[rows, cols], axis=-1, keepdims=True)
                p = jnp.exp(jnp.where(mask, s_all[cols], NEG_INF) - logzs[g][:, :1])
                probs.append(p.astype(BF16))
                dss.append((p * (dp_all[cols] - delta) * ATT_SCALE).astype(BF16))
                deltas.append(delta)
            return q, dout, _stack(probs), _stack(dss), deltas

        row = lax.broadcasted_iota(jnp.int32, (HEAD, HEAD), 0)
        col = lax.broadcasted_iota(jnp.int32, (HEAD, HEAD), 1)
        reach = col >= row + jnp.where(ib < nib - 1, HEAD - max_dist, 2 * HEAD)
        for c in range(r):
            k_old, v_old = kf[_band_rows(c, r)], vf[_band_rows(c, r)]
            dk_own = dv_own = None
            for j in range(nsub):
                rows = _band_rows(j * SB + c, r)
                k_own, v_own = kf[_band_rows((j + 1) * SB + c, r)], vf[_band_rows((j + 1) * SB + c, r)]
                kcat = jnp.concatenate([k_old, k_own], axis=0).astype(BF16)
                vcat = jnp.concatenate([v_old, v_own], axis=0).astype(BF16)
                logzs = [l_ref[rows, cols] for cols in heads]
                q, dout, p, ds, deltas = grads(rows, logzs, kcat, vcat, band_first if j == 0 else band)
                dq = _dot(ds, kcat)
                for g, cols in enumerate(heads):
                    dqf[rows, cols] = dq[cols]
                    if with_sink:
                        p_sink = jnp.exp(sink_ref[kvh * grp + g] - logzs[g][:, :1])
                        ds_ref[g * 8:(g + 1) * 8] += jnp.sum(p_sink * deltas[g])
                dk, dv = _dot_tn(ds, q), _dot_tn(p, dout)
                if j > 0:
                    done = _band_rows((j - 1) * SB + c, r)
                    dkf[done] = dk_own + dk[:HEAD]
                    dvf[done] = dv_own + dv[:HEAD]
                dk_own, dv_own = dk[HEAD:], dv[HEAD:]
                k_old, v_old = k_own, v_own
            logzs = [ln_ref[_band_rows(c, r), cols] for cols in heads]
            q, dout, p, ds, _ = grads(_band_rows(BT + c, r), logzs, k_old.astype(BF16), v_old.astype(BF16), reach)
            done = _band_rows((nsub - 1) * SB + c, r)
            dkf[done] = dk_own + _dot_tn(ds, q)
            dvf[done] = dv_own + _dot_tn(p, dout)

        cs, sn = c_ref[...], s_ref[...]
        for cols in heads:
            out_ref[:, cols] = _unrope(dqf[:, cols], cs, sn).astype(BF16)
        out_ref[:, grp * HEAD:(grp + 1) * HEAD] = _unrope(dkf[...], cs, sn).astype(BF16)
        out_ref[:, (grp + 1) * HEAD:] = dvf[...].astype(BF16)

    def nxt_row(i):
        return jnp.minimum((i + 1) * nsub, nblk - 1)

    stride = grp + 2
    q_next = pl.BlockSpec((SB, grp * HEAD), lambda h, i: (nxt_row(i), (base + h * stride) // grp))
    head_cur = pl.BlockSpec((BT, grp * HEAD), lambda h, i: (i, h))
    head_next = pl.BlockSpec((SB, grp * HEAD), lambda h, i: (nxt_row(i), h))
    table = pl.BlockSpec((BT, HEAD), lambda h, i: (i, 0))

    in_specs = [*_band_specs(BT, SB, nsub, base, grp), q_next,
                head_cur, head_next, head_cur, head_next, head_cur, head_next, table, table, UNREAD]
    args = [qkv, qkv, qkv, qkv, qkv, qkv, do, do, o, o, lse, lse, cos, sin_signed, dqkv]
    out_specs = [pl.BlockSpec((BT, stride * HEAD), lambda h, i: (i, base // stride + h))]
    out_shape = [jax.ShapeDtypeStruct(dqkv.shape, dqkv.dtype)]
    if with_sink:
        in_specs.insert(0, pl.BlockSpec(memory_space=pltpu.SMEM))
        args.insert(0, sinks)
        out_specs.append(pl.BlockSpec((None, grp * 8, HEAD), lambda h, i: (h, 0, 0)))
        out_shape.append(jax.ShapeDtypeStruct((hkv, grp * 8, HEAD), F32))
    wide = pltpu.VMEM((BT + SB, grp * HEAD), F32)
    tall = pltpu.VMEM((SB + BT, HEAD), F32)
    grad = pltpu.VMEM((BT, HEAD), F32)
    return pl.pallas_call(
        body, name=name, grid=(hkv, nib), in_specs=in_specs, out_specs=out_specs, out_shape=out_shape,
        input_output_aliases={len(args) - 1: 0},
        scratch_shapes=[wide, wide, wide, tall, tall, pltpu.VMEM((BT, grp * HEAD), F32), grad, grad],
        compiler_params=pltpu.CompilerParams(dimension_semantics=("parallel", "arbitrary"),
                                             vmem_limit_bytes=VMEM_LIMIT_LARGE),
    )(*args)


def merge_groups(outs, lses, name):
    T, Wd = outs[0].shape
    tm = 1024

    def body(o0, o1, o2, l0, l1, l2, out_ref, lt_ref):
        a, b, c = l0[...], l1[...], l2[...]
        m = jnp.maximum(jnp.maximum(a, b), c)
        wa, wb, wc = jnp.exp(a - m), jnp.exp(b - m), jnp.exp(c - m)
        z = wa + wb + wc
        out_ref[...] = ((wa * o0[...] + wb * o1[...] + wc * o2[...]) / z).astype(BF16)
        lt_ref[...] = m + jnp.log(z)

    spec = pl.BlockSpec((tm, Wd), lambda i: (i, 0))
    return pl.pallas_call(
        body, name=name, grid=(T // tm,), in_specs=[spec] * 6, out_specs=[spec, spec],
        out_shape=[jax.ShapeDtypeStruct((T, Wd), BF16), jax.ShapeDtypeStruct((T, Wd), F32)],
        compiler_params=_params("parallel"),
    )(*outs, *lses)


M_HEADS = 4


def mem_kv(mem, g, w, name):
    n, D = mem.shape

    def body(m_ref, g_ref, w_ref, mn_ref, kv_ref):
        x = m_ref[...]
        mn = (x * _rstd(x) * g_ref[...]).astype(BF16)
        mn_ref[...] = mn
        kv_ref[...] = _dot(mn, w_ref[...]).astype(BF16)

    return pl.pallas_call(
        body, name=name,
        out_shape=[jax.ShapeDtypeStruct((n, D), BF16), jax.ShapeDtypeStruct((n, w.shape[1]), BF16)],
        compiler_params=pltpu.CompilerParams(vmem_limit_bytes=VMEM_LIMIT),
    )(mem, g, w)


def mem_fwd(qkv, mkv, name):
    T = qkv.shape[0]
    n = mkv.shape[0]
    RB = 1024

    def body(q_ref, kv_ref, o_ref, l_ref):
        for h in range(M_HEADS):
            cols = slice(h * HEAD, (h + 1) * HEAD)
            s = _dot_nt(q_ref[:, cols], kv_ref[:, cols]) * ATT_SCALE
            m = jnp.max(s, axis=-1, keepdims=True)
            p = jnp.exp(s - m)
            den = jnp.sum(p, axis=-1, keepdims=True)
            vals = kv_ref[:, (M_HEADS + h) * HEAD:(M_HEADS + h + 1) * HEAD]
            o_ref[:, cols] = (_dot(p.astype(BF16), vals) / den).astype(BF16)
            l_ref[:, cols] = jnp.broadcast_to(m + jnp.log(den), (RB, HEAD))

    out = pl.BlockSpec((RB, M_HEADS * HEAD), lambda i: (i, 0))
    return pl.pallas_call(
        body, name=name, grid=(T // RB,),
        in_specs=[pl.BlockSpec((RB, M_HEADS * HEAD), lambda i: (i, MQ // M_HEADS)), _resident(mkv)],
        out_specs=[out, out],
        out_shape=[jax.ShapeDtypeStruct((T, M_HEADS * HEAD), BF16), jax.ShapeDtypeStruct((T, M_HEADS * HEAD), F32)],
        compiler_params=_params("parallel"),
    )(qkv, mkv)


def mem_bwd(qkv, dqkv, mkv, do, o, lse, name):
    T = qkv.shape[0]
    n = mkv.shape[0]
    RB = 1024

    def body(q_ref, kv_ref, do_ref, o_ref, l_ref, _, dq_ref, dk_ref, dv_ref):
        @pl.when(pl.program_id(0) == 0)
        def _():
            dk_ref[...] = jnp.zeros_like(dk_ref)
            dv_ref[...] = jnp.zeros_like(dv_ref)

        for h in range(M_HEADS):
            cols = slice(h * HEAD, (h + 1) * HEAD)
            keys, vals = kv_ref[:, cols], kv_ref[:, (M_HEADS + h) * HEAD:(M_HEADS + h + 1) * HEAD]
            q, dout = q_ref[:, cols], do_ref[:, cols]
            delta = jnp.sum(dout.astype(F32) * o_ref[:, cols].astype(F32), axis=-1, keepdims=True)
            p = jnp.exp(_dot_nt(q, keys) * ATT_SCALE - l_ref[:, cols][:, :1])
            ds = (p * (_dot_nt(dout, vals) - delta) * ATT_SCALE).astype(BF16)
            dq_ref[:, cols] = _dot(ds, keys).astype(BF16)
            dk_ref[:, cols] += _dot_tn(ds, q)
            dv_ref[:, cols] += _dot_tn(p.astype(BF16), dout)

    wide = M_HEADS * HEAD
    tok = pl.BlockSpec((RB, wide), lambda i: (i, 0))
    q_cols = pl.BlockSpec((RB, wide), lambda i: (i, MQ // M_HEADS))
    slot = pl.BlockSpec((n, wide), lambda i: (0, 0))
    return pl.pallas_call(
        body, name=name, grid=(T // RB,),
        in_specs=[q_cols, _resident(mkv), tok, tok, tok, UNREAD],
        out_specs=[q_cols, slot, slot],
        out_shape=[jax.ShapeDtypeStruct(dqkv.shape, dqkv.dtype),
                   jax.ShapeDtypeStruct((n, wide), F32), jax.ShapeDtypeStruct((n, wide), F32)],
        input_output_aliases={5: 0},
        compiler_params=_params("arbitrary"),
    )(qkv, mkv, do, o, lse, dqkv)


def mem_kv_bwd(mem, g, mem_n, w, dmkv, name):
    n, D = mem.shape

    def body(m_ref, g_ref, mn_ref, w_ref, d_ref, dw_ref, dg_ref):
        d = d_ref[...].astype(BF16)
        dw_ref[...] = _dot_tn(mn_ref[...], d)
        x = m_ref[...]
        dg_ref[...] = jnp.sum(_dot_nt(d, w_ref[...]) * (x * _rstd(x)), axis=0, keepdims=True)

    return pl.pallas_call(
        body, name=name,
        out_shape=[jax.ShapeDtypeStruct(w.shape, F32), jax.ShapeDtypeStruct((1, D), F32)],
        compiler_params=pltpu.CompilerParams(vmem_limit_bytes=VMEM_LIMIT),
    )(mem, g, mem_n, w, dmkv)


def _rms_bwd(dn, f, g):
    r = _rstd(f)
    fhat = f * r
    dfhat = dn * g
    df = r * (dfhat - fhat * jnp.mean(dfhat * fhat, axis=-1, keepdims=True))
    return df, jnp.sum(dn * fhat, axis=0, keepdims=True)


def ffn_tokens_bwd(dh, f, h_in, gu, g_pre, g_post, w_in, w_out, coef, name, after):
    T, D = dh.shape

    def body(dh_ref, f_ref, h_ref, gu_ref, gpre_ref, gpost_ref, win_ref, wout_ref, _,
             df_ref, dgu_ref, dhin_ref, dgpre_ref, dgpost_ref, dxn_ref):
        i, j = pl.program_id(0), pl.program_id(1)

        @pl.when(j == 0)
        def _():
            @pl.when(i == 0)
            def _():
                dgpre_ref[...] = jnp.zeros_like(dgpre_ref)
                dgpost_ref[...] = jnp.zeros_like(dgpost_ref)

            df, dg_post = _rms_bwd(coef * dh_ref[...], f_ref[...], gpost_ref[...])
            dgpost_ref[...] += dg_post
            df_ref[...] = df.astype(BF16)

        for jj in range(2):
            @pl.when(j == jj)
            def _(jj=jj):
                lo, mid, hi = 2 * jj * FF_T, (2 * jj + 1) * FF_T, (2 * jj + 2) * FF_T
                da = _dot_nt(df_ref[...], wout_ref[jj * FF_T:(jj + 1) * FF_T, :])
                gate = gu_ref[:, :FF_T].astype(F32)
                up = gu_ref[:, FF_T:].astype(F32)
                sig = _sigmoid(gate)
                dgate = (da * up * sig * (1.0 + gate * (1.0 - sig))).astype(BF16)
                dup = (da * gate * sig).astype(BF16)
                dgu_ref[:, :FF_T] = dgate
                dgu_ref[:, FF_T:] = dup
                part = _dot_nt(dgate, win_ref[:, lo:mid]) + _dot_nt(dup, win_ref[:, mid:hi])
                if jj == 0:
                    dxn_ref[...] = part
                else:
                    h = h_ref[...]
                    r = _rstd(h)
                    xhat = h * r
                    dxn = dxn_ref[...] + part
                    dxhat = dxn * gpre_ref[...]
                    dhin_ref[...] = dh_ref[...] + r * (dxhat - xhat * jnp.mean(dxhat * xhat, axis=-1, keepdims=True))
                    dgpre_ref[...] += jnp.sum(dxn * xhat, axis=0, keepdims=True)

    row = pl.BlockSpec((TM, D), lambda i, j: (i, 0))
    wide = pl.BlockSpec((TM, 2 * FF_T), lambda i, j: (i, j))
    vec = pl.BlockSpec((1, D), lambda i, j: (0, 0))
    return pl.pallas_call(
        body, name=name, grid=(T // TM, 2),
        in_specs=[row, row, row, wide, _resident(g_pre), _resident(g_post), _resident(w_in), _resident(w_out),
                  UNREAD],
        out_specs=[row, wide, row, vec, vec],
        out_shape=[jax.ShapeDtypeStruct((T, D), BF16), jax.ShapeDtypeStruct((T, 2 * D_FF), BF16),
                   jax.ShapeDtypeStruct((T, D), F32), jax.ShapeDtypeStruct((1, D), F32),
                   jax.ShapeDtypeStruct((1, D), F32)],
        scratch_shapes=[pltpu.VMEM((TM, D), F32)],
        compiler_params=pltpu.CompilerParams(dimension_semantics=("arbitrary", "arbitrary"),
                                             vmem_limit_bytes=VMEM_LIMIT_LARGE),
    )(dh, f, h_in, gu, g_pre, g_post, w_in, w_out, after)


def mm_nt_norm_bwd(pieces, h_in, dh_out, g, name, after):
    T, D = h_in.shape

    def body(*refs):
        ab = refs[:2 * len(pieces)]
        h_ref, dh_ref, g_ref, _, o_ref, dg_ref = refs[2 * len(pieces):]
        dxn = _dot_nt(ab[0][...], ab[1][...])
        for p in range(1, len(pieces)):
            dxn += _dot_nt(ab[2 * p][...], ab[2 * p + 1][...])
        h = h_ref[...]
        r = _rstd(h)
        xhat = h * r
        dxhat = dxn * g_ref[...]
        o_ref[...] = dh_ref[...] + r * (dxhat - xhat * jnp.mean(dxhat * xhat, axis=-1, keepdims=True))

        @pl.when(pl.program_id(0) == 0)
        def _():
            dg_ref[...] = jnp.zeros_like(dg_ref)

        dg_ref[...] += jnp.sum(dxn * xhat, axis=0, keepdims=True)

    in_specs, args = [], []
    for a, w in pieces:
        in_specs += [pl.BlockSpec((TM, a.shape[1]), lambda i: (i, 0)), _resident(w)]
        args += [a, w]
    row = pl.BlockSpec((TM, D), lambda i: (i, 0))
    return pl.pallas_call(
        body, name=name, grid=(T // TM,),
        in_specs=in_specs + [row, row, _resident(g), UNREAD],
        out_specs=[row, pl.BlockSpec((1, D), lambda i: (0, 0))],
        out_shape=[jax.ShapeDtypeStruct((T, D), F32), jax.ShapeDtypeStruct((1, D), F32)],
        compiler_params=_params("arbitrary"),
    )(*args, h_in, dh_out, g, after)


def gate_merge_out_bwd(dh, f, g, w_out, gt, o_a, o_b, o_m, w_a, w_b, w_m, name, after):
    T = dh.shape[0]
    D = D_MODEL
    branch = ((o_a, w_a), (o_b, w_b), (o_m, w_m))

    def body(dh_ref, f_ref, g_ref, wo_ref, gt_ref, oa_ref, ob_ref, om_ref, wa_ref, wb_ref, wm_ref, _,
             df_ref, dg_ref, dgt_ref, dpa_ref, dpb_ref, dpm_ref, doa_ref, dob_ref, dom_ref, db_ref):
        @pl.when(pl.program_id(0) == 0)
        def _():
            db_ref[...] = jnp.zeros_like(db_ref)
            dg_ref[...] = jnp.zeros_like(dg_ref)

        df, dg = _rms_bwd(dh_ref[...], f_ref[...], g_ref[...])
        dg_ref[...] += dg
        df = df.astype(BF16)
        df_ref[...] = df
        dmf = _dot_nt(df, wo_ref[...])
        for x, (o_ref, w_ref, dp_ref, do_ref) in enumerate(((oa_ref, wa_ref, dpa_ref, doa_ref),
                                                           (ob_ref, wb_ref, dpb_ref, dob_ref),
                                                           (om_ref, wm_ref, dpm_ref, dom_ref))):
            cols = slice(x * D, (x + 1) * D)
            gx = gt_ref[:, cols].astype(F32)
            w = w_ref[...]
            dpre = dmf * _dot(o_ref[...], w) * gx * (1.0 - gx)
            dgt_ref[:, cols] = dpre.astype(BF16)
            db_ref[:, cols] += jnp.sum(dpre, axis=0, keepdims=True)
            dp = (dmf * gx).astype(BF16)
            dp_ref[...] = dp
            do_ref[...] = _dot_nt(dp, w).astype(BF16)

    def rows(width):
        return pl.BlockSpec((TM, width), lambda i: (i, 0))

    widths = [o.shape[1] for o, _ in branch]
    return pl.pallas_call(
        body, name=name, grid=(T // TM,),
        in_specs=[rows(D), rows(D), _resident(g), _resident(w_out), rows(3 * D)] + [rows(k) for k in widths]
                 + [_resident(w) for _, w in branch] + [UNREAD],
        out_specs=[rows(D), pl.BlockSpec((1, D), lambda i: (0, 0)), rows(3 * D), rows(D), rows(D), rows(D)]
                  + [rows(k) for k in widths] + [pl.BlockSpec((1, 3 * D), lambda i: (0, 0))],
        out_shape=[jax.ShapeDtypeStruct((T, D), BF16), jax.ShapeDtypeStruct((1, D), F32),
                   jax.ShapeDtypeStruct((T, 3 * D), BF16)] + [jax.ShapeDtypeStruct((T, D), BF16)] * 3
                  + [jax.ShapeDtypeStruct((T, k), BF16) for k in widths]
                  + [jax.ShapeDtypeStruct((1, 3 * D), F32)],
        compiler_params=_params("arbitrary"),
    )(dh, f, g, w_out, gt, o_a, o_b, o_m, w_a, w_b, w_m, after)


def mm_tn(x, dy, tm, tn, name, shard_major=False, perm=None, slabs=1, after=None, wire=False):
    T, M = x.shape
    N = dy.shape[1]
    tk = min(2048, T)
    perm = perm or (lambda j: j)
    w = tn // slabs

    def body(x_ref, dy_ref, *rest):
        o_ref = rest[-2] if wire else rest[-1]

        @pl.when(pl.program_id(2) == 0)
        def _():
            o_ref[...] = jnp.zeros_like(o_ref)

        acc = _dot_tn(x_ref[...], dy_ref[...])
        if shard_major:
            for s in range(slabs):
                o_ref[s] += acc[:, s * w:(s + 1) * w]
        else:
            o_ref[...] += acc
        if wire:
            @pl.when(pl.program_id(2) == T // tk - 1)
            def _():
                rest[-1][...] = o_ref[...].astype(BF16)

    if shard_major:
        out_spec = pl.BlockSpec((slabs, tm, w), lambda i, j, k: (perm(j), i, 0))
        out_shape = jax.ShapeDtypeStruct((N // w, M, w), F32)
    else:
        out_spec = pl.BlockSpec((tm, tn), lambda i, j, k: (i, j))
        out_shape = jax.ShapeDtypeStruct((M, N), F32)
    return pl.pallas_call(
        body, name=name, grid=(M // tm, N // tn, T // tk),
        in_specs=[pl.BlockSpec((tk, tm), lambda i, j, k: (k, i)),
                  pl.BlockSpec((tk, tn), lambda i, j, k: (k, j))] + ([] if after is None else [UNREAD]),
        out_specs=[out_spec, out_spec] if wire else out_spec,
        out_shape=[out_shape, jax.ShapeDtypeStruct(out_shape.shape, BF16)] if wire else out_shape,
        compiler_params=_params("parallel", "parallel", "arbitrary"),
    )(x, dy, *([] if after is None else [after]))


def rope_tables(T, zero):
    half = HEAD // 2
    inv = ROPE_THETA ** (-jnp.arange(half, dtype=F32) / half)
    ang = (jnp.arange(T).astype(F32) + zero)[:, None] * inv[None, :]
    cos, sin = jnp.cos(ang), jnp.sin(ang)
    return jnp.concatenate([cos, cos], axis=1), jnp.concatenate([-sin, sin], axis=1)


def layer_step(x, mem, target, gains, sinks, b_gate, weights_of, send_grads, zero):
    T = x.shape[0]
    cos, sin_signed = rope_tables(T, zero)
    no_sink = jnp.full((2,), NEG_INF, F32)

    w = dict(weights_of("ffn1_in", cos))
    xn1, gu1, a1 = ffn_in(x, gains["ffn1_norm_pre"], w["ffn1_w_in"], "ffn1_in")
    w.update(weights_of("ffn1_out", xn1))
    f1, h1 = mm_norm_res(a1, w["ffn1_w_out"], x, gains["ffn1_norm_post"], 0.5, "ffn1_out")
    w.update(weights_of("mix", f1))
    u, qkv, gt = mix_in(h1, gains["mix_norm_pre"], w["w_in"], w["w_gate"], b_gate, cos, sin_signed, "mix_in")
    outs, lses = [], []
    for gidx, (window, dil) in enumerate(DIL):
        o_g, l_g = band_fwd(qkv, no_sink, r=dil, base=A_BASE + 6 * gidx, hkv=2, grp=1, max_dist=window // dil,
                            out_dtype=F32, name=f"attn_a{gidx}_fwd")
        outs.append(o_g)
        lses.append(l_g)
    o_a, l_a = merge_groups(outs, lses, "attn_a_merge")
    o_b, l_b = band_fwd(qkv, sinks, r=1, base=B_BASE, hkv=2, grp=2, max_dist=HEAD - 1, out_dtype=BF16,
                        name="attn_b_fwd")
    mem_n, mkv = mem_kv(mem, gains["mem_norm"], w["w_mem_kv"], "mem_kv")
    o_m, l_m = mem_fwd(qkv, mkv, "attn_m_fwd")
    merged, mo, h2 = gate_merge_out(gt, o_a, o_b, o_m, w["w_o_a"], w["w_o_b"], w["w_o_m"], w["w_out"], h1,
                                    gains["mix_norm_post"], "gate_merge_out")
    w.update(weights_of("ffn2", mo))
    xn2, gu2, a2 = ffn_in(h2, gains["ffn2_norm_pre"], w["ffn2_w_in"], "ffn2_in")
    f2, dy, sq = mm_norm_res(a2, w["ffn2_w_out"], h2, gains["ffn2_norm_post"], 0.5, "ffn2_out", target=target)

    grads = {}

    def ffn_bwd(tag, dh_out, f, gu, a, xn, h_in, after):
        df, dgu, dh_in, grads[f"{tag}_norm_pre"], grads[f"{tag}_norm_post"] = ffn_tokens_bwd(
            dh_out, f, h_in, gu, gains[f"{tag}_norm_pre"], gains[f"{tag}_norm_post"], w[f"{tag}_w_in"],
            w[f"{tag}_w_out"], 0.5, f"{tag}_tokens_bwd", after)
        sent = send_grads(f"{tag}_in", {f"{tag}_w_in": mm_tn(
            xn, dgu, D_MODEL, FF_T, f"{tag}_w_in_grad", shard_major=True, perm=_ffn_perm, wire=True)})
        sent = send_grads(f"{tag}_out", {f"{tag}_w_out": mm_tn(
            a, df, FF_T, D_MODEL, f"{tag}_w_out_grad", after=sent, wire=True)})
        return dh_in, sent

    dh2, sent = ffn_bwd("ffn2", dy, f2, gu2, a2, xn2, h2, dy)

    mix = {}
    dmo, grads["mix_norm_post"], dgt, dpa, dpb, dpm, do_a, do_b, do_m, grads["b_gate"] = gate_merge_out_bwd(
        dh2, mo, gains["mix_norm_post"], w["w_out"], gt, o_a, o_b, o_m, w["w_o_a"], w["w_o_b"], w["w_o_m"],
        "gate_merge_out_bwd", sent)
    mix["w_out"] = mm_tn(merged, dmo, D_MODEL, D_MODEL, "w_out_grad", wire=True)
    mix["w_o_a"] = mm_tn(o_a, dpa, o_a.shape[1], D_MODEL, "w_o_a_grad")
    mix["w_o_b"] = mm_tn(o_b, dpb, o_b.shape[1], D_MODEL, "w_o_b_grad")
    mix["w_o_m"] = mm_tn(o_m, dpm, o_m.shape[1], D_MODEL, "w_o_m_grad")

    dqkv = lax.empty(qkv.shape, qkv.dtype)
    for gidx, (window, dil) in enumerate(DIL):
        dqkv, = band_bwd(qkv, dqkv, do_a, o_a, l_a, cos, sin_signed, None, r=dil, base=A_BASE + 6 * gidx, hkv=2,
                         grp=1, max_dist=window // dil, name=f"attn_a{gidx}_bwd")
    dqkv, dsink = band_bwd(qkv, dqkv, do_b, o_b, l_b, cos, sin_signed, sinks, r=1, base=B_BASE, hkv=2, grp=2,
                           max_dist=HEAD - 1, name="attn_b_bwd")
    grads["sinks"] = -dsink[:, ::8, 0].reshape(1, 4)
    dqkv, dmk, dmv = mem_bwd(qkv, dqkv, mkv, do_m, o_m, l_m, "attn_m_bwd")
    mix["w_mem_kv"], grads["mem_norm"] = mem_kv_bwd(
        mem, gains["mem_norm"], mem_n, w["w_mem_kv"], jnp.concatenate([dmk, dmv], axis=1), "mem_kv_bwd")

    mix["w_in"] = mm_tn(u, dqkv, D_MODEL, 1280, "w_in_grad")
    mix["w_gate"] = mm_tn(u, dgt, D_MODEL, 1536, "w_gate_grad", shard_major=True, slabs=2, wire=True)
    sent = send_grads("mix", mix)
    dh1, grads["mix_norm_pre"] = mm_nt_norm_bwd(
        [(dqkv, w["w_in"]), (dgt, w["w_gate"])], h1, dh2, gains["mix_norm_pre"], "mix_in_bwd", sent)

    dx, _ = ffn_bwd("ffn1", dh1, f1, gu1, a1, xn1, x, dh1)
    return sq, dx, grads


def _place():
    return lax.axis_index("x"), lax.axis_index("y"), lax.axis_index("c")


def _other_chips(x, y):
    return [(1 - x, y), (x, 1 - y), (1 - x, 1 - y)]


def _hbm(n):
    return [pl.BlockSpec(memory_space=pltpu.HBM)] * n


SEM = pl.BlockSpec(memory_space=pltpu.SEMAPHORE)
SIDE_EFFECT = pltpu.SideEffectType.DATAFLOW_SIDE_EFFECTING


def _chip_copy(src, land, sems, i, j, dst_slot, scatter):
    x, y, c = _place()
    px, py = _other_chips(x, y)[j]
    send_sems, recv_sems = sems
    return pltpu.make_async_remote_copy(
        src_ref=src[i].at[2 * px + py] if scatter else src[i], dst_ref=land[i].at[dst_slot],
        send_sem=send_sems.at[3 * i + j], recv_sem=recv_sems.at[3 * i + j],
        device_id=(px, py, c), device_id_type=MESH)


def chip_copies_start(srcs, lands, groups, scatter, name):
    n = len(srcs)

    def body(*refs):
        src, land = refs[:n], refs[n:2 * n]
        sems = refs[2 * n:2 * n + 2 * len(groups)]
        token = refs[-1]
        x, y, _ = _place()
        for g, members in enumerate(groups):
            part = ([src[i] for i in members], [land[i] for i in members])
            for t in range(len(members)):
                for j in range(3):
                    _chip_copy(*part, sems[2 * g:2 * g + 2], t, j, 2 * x + y, scatter).start()
        token[...] = jnp.zeros_like(token)

    sem_shapes = [pltpu.SemaphoreType.DMA((3 * len(m),)) for m in groups for _ in range(2)]
    thru = [pltpu.HBM(a.shape, a.dtype) for a in (*srcs, *lands)]
    res = pl.pallas_call(
        body, name=name,
        out_shape=(*sem_shapes, *thru, jax.ShapeDtypeStruct((8, 128), F32)),
        in_specs=_hbm(2 * n),
        out_specs=(*[SEM] * len(sem_shapes), *_hbm(2 * n), pl.BlockSpec(memory_space=pltpu.VMEM)),
        input_output_aliases={i: len(sem_shapes) + i for i in range(2 * n)},
        compiler_params=pltpu.CompilerParams(has_side_effects=SIDE_EFFECT),
    )(*[pltpu.with_memory_space_constraint(a, pltpu.HBM) for a in (*srcs, *lands)])
    k = len(sem_shapes)
    sems = [tuple(res[2 * g:2 * g + 2]) for g in range(len(groups))]
    return sems, list(res[k:k + n]), list(res[k + n:k + 2 * n]), res[-1]


def chip_copies_wait(srcs, lands, sems, after, scatter, name):
    n = len(srcs)

    def body(*refs):
        src, land = refs[:n], refs[n:2 * n]
        pair = refs[2 * n:2 * n + 2]
        x, y, _ = _place()
        for i in range(n):
            for j, (px, py) in enumerate(_other_chips(x, y)):
                copy = _chip_copy(src, land, pair, i, j, 2 * px + py, scatter)
                copy.wait_send()
                copy.wait_recv()

    res = pl.pallas_call(
        body, name=name,
        out_shape=[pltpu.HBM(a.shape, a.dtype) for a in (*srcs, *lands)],
        in_specs=[*_hbm(2 * n), SEM, SEM, pl.BlockSpec(memory_space=pl.ANY)],
        out_specs=_hbm(2 * n),
        input_output_aliases={i: i for i in range(2 * n)},
        compiler_params=pltpu.CompilerParams(has_side_effects=SIDE_EFFECT),
    )(*srcs, *lands, *sems, after)
    return list(res[n:])


def small_all_gather(small, name):
    flips = [(fx, fy, fc) for fx in (0, 1) for fy in (0, 1) for fc in (0, 1)][1:]

    def body(in_ref, out_ref, send_sems, recv_sems, local_sem):
        x, y, c = _place()
        me = 4 * x + 2 * y + c

        def copy(k, slot):
            fx, fy, fc = flips[k]
            return pltpu.make_async_remote_copy(
                src_ref=in_ref, dst_ref=out_ref.at[slot], send_sem=send_sems.at[k], recv_sem=recv_sems.at[k],
                device_id=(x ^ fx, y ^ fy, c ^ fc), device_id_type=MESH)

        local = pltpu.make_async_copy(in_ref, out_ref.at[me], local_sem)
        local.start()
        for k in range(len(flips)):
            copy(k, me).start()
        for k, (fx, fy, fc) in enumerate(flips):
            copy(k, 4 * (x ^ fx) + 2 * (y ^ fy) + (c ^ fc)).wait()
        local.wait()

    return pl.pallas_call(
        body, name=name, in_specs=_hbm(1), out_specs=_hbm(1)[0],
        out_shape=jax.ShapeDtypeStruct((N_DEV,) + small.shape, small.dtype),
        scratch_shapes=[pltpu.SemaphoreType.DMA((len(flips),)), pltpu.SemaphoreType.DMA((len(flips),)),
                        pltpu.SemaphoreType.DMA],
    )(small)


def _sibling_copy(src, land, sems, i):
    x, y, c = _place()
    return pltpu.make_async_remote_copy(
        src_ref=src[i], dst_ref=land[i], send_sem=sems[0].at[i], recv_sem=sems[1].at[i],
        device_id=(x, y, 1 - c), device_id_type=MESH)


def sibling_copies_start(parts, name):
    n = len(parts)
    lands = [lax.empty(p.shape, p.dtype) for p in parts]

    def body(*refs):
        src, land, sems, token = refs[:n], refs[n:2 * n], refs[2 * n:2 * n + 2], refs[-1]
        for i in range(n):
            _sibling_copy(src, land, sems, i).start()
        token[...] = jnp.zeros_like(token)

    res = pl.pallas_call(
        body, name=name,
        out_shape=(pltpu.SemaphoreType.DMA((n,)), pltpu.SemaphoreType.DMA((n,)),
                   *[pltpu.HBM(a.shape, a.dtype) for a in (*parts, *lands)], jax.ShapeDtypeStruct((8, 128), F32)),
        in_specs=_hbm(2 * n),
        out_specs=(SEM, SEM, *_hbm(2 * n), pl.BlockSpec(memory_space=pltpu.VMEM)),
        input_output_aliases={i: 2 + i for i in range(2 * n)},
        compiler_params=pltpu.CompilerParams(has_side_effects=SIDE_EFFECT),
    )(*[pltpu.with_memory_space_constraint(a, pltpu.HBM) for a in (*parts, *lands)])
    return tuple(res[:2]), list(res[2:2 + n]), list(res[2 + n:2 + 2 * n]), res[-1]


def sibling_copies_wait(parts, lands, sems, after, name):
    n = len(parts)

    def body(*refs):
        src, land, sems = refs[:n], refs[n:2 * n], refs[2 * n:2 * n + 2]
        for i in range(n):
            copy = _sibling_copy(src, land, sems, i)
            copy.wait_send()
            copy.wait_recv()

    res = pl.pallas_call(
        body, name=name,
        out_shape=[pltpu.HBM(a.shape, a.dtype) for a in (*parts, *lands)],
        in_specs=[*_hbm(2 * n), SEM, SEM, UNREAD],
        out_specs=_hbm(2 * n),
        input_output_aliases={i: i for i in range(2 * n)},
        compiler_params=pltpu.CompilerParams(has_side_effects=SIDE_EFFECT),
    )(*parts, *lands, *sems, after)
    return list(res[n:])


def _row_tile(rows):
    for t in (256, 176, 128, 64, 32, 16, 8):
        if rows % t == 0:
            return t
    return rows


def chip_partial_sum(me, own_sm, recv, name):
    _, rows, cols = own_sm.shape
    tr = _row_tile(rows)

    def body(me_ref, own_ref, r0, r1, r2, r3, o_ref):
        acc = jnp.zeros((tr, cols), F32)
        for s, r_ref in enumerate((r0, r1, r2, r3)):
            acc = acc + jnp.where(me_ref[0] == s, own_ref[...], r_ref[...].astype(F32))
        o_ref[...] = acc

    def slot(s):
        return pl.BlockSpec((None, tr, cols), lambda i, me_ref, s=s: (s, i, 0))

    return pl.pallas_call(
        body, name=name,
        grid_spec=pltpu.PrefetchScalarGridSpec(
            num_scalar_prefetch=1, grid=(rows // tr,),
            in_specs=[pl.BlockSpec((None, tr, cols), lambda i, me_ref: (me_ref[0], i, 0))] + [slot(s) for s in range(4)],
            out_specs=pl.BlockSpec((tr, cols), lambda i, me_ref: (i, 0))),
        out_shape=jax.ShapeDtypeStruct((rows, cols), F32),
        compiler_params=_params("parallel"),
    )(me, own_sm, recv, recv, recv, recv)


def _adamw(w, g, m, v):
    m = ADAM_B1 * m + (1.0 - ADAM_B1) * g
    v = ADAM_B2 * v + (1.0 - ADAM_B2) * (g * g)
    m_hat = m / (1.0 - ADAM_B1 ** ADAM_STEP)
    v_hat = v / (1.0 - ADAM_B2 ** ADAM_STEP)
    delta = -ADAM_LR * (m_hat / (jnp.sqrt(v_hat) + ADAM_EPS) + ADAM_WD * w)
    return delta, m, v


def adamw_pair(part, sib, w, m, v, name):
    rows, cols = w.shape
    tr = _row_tile(rows)

    def body(p_ref, s_ref, w_ref, m_ref, v_ref, g_ref, d_ref, nm_ref, nv_ref):
        g = p_ref[...] + s_ref[...]
        g_ref[...] = g
        d_ref[...], nm_ref[...], nv_ref[...] = _adamw(w_ref[...], g, m_ref[...], v_ref[...])

    spec = pl.BlockSpec((tr, cols), lambda i: (i, 0))
    return pl.pallas_call(
        body, name=name, grid=(rows // tr,), in_specs=[spec] * 5, out_specs=[spec] * 4,
        out_shape=[jax.ShapeDtypeStruct((rows, cols), F32)] * 4,
        compiler_params=_params("parallel"),
    )(part, sib, w, m, v)


def adamw_small(g_all, w, m, v, name):
    def body(ga_ref, w_ref, m_ref, v_ref, g_ref, d_ref, nm_ref, nv_ref):
        g = ga_ref[0]
        for k in range(1, N_DEV):
            g = g + ga_ref[k]
        g_ref[...] = g
        d_ref[...], nm_ref[...], nv_ref[...] = _adamw(w_ref[...], g, m_ref[...], v_ref[...])

    return pl.pallas_call(
        body, name=name, out_shape=[jax.ShapeDtypeStruct(w.shape, F32)] * 4,
    )(g_all, w, m, v)


WEIGHTS = ("ffn1_norm_pre", "ffn1_w_in", "ffn1_w_out", "ffn1_norm_post", "mix_norm_pre", "w_in", "sinks",
           "mem_norm", "w_mem_kv", "w_gate", "b_gate", "w_o_a", "w_o_b", "w_o_m", "w_out", "mix_norm_post",
           "ffn2_norm_pre", "ffn2_w_in", "ffn2_w_out", "ffn2_norm_post")
BIG = ("ffn1_w_in", "ffn1_w_out", "w_in", "w_mem_kv", "w_gate", "w_o_a", "w_o_b", "w_o_m", "w_out",
       "ffn2_w_in", "ffn2_w_out")
GATHER_ORDER = ("ffn1_in", "ffn1_out", "mix", "ffn2")
GATHER_GROUPS = {"ffn1_in": ("ffn1_w_in",), "ffn1_out": ("ffn1_w_out",),
                 "mix": ("w_in", "w_gate", "w_mem_kv", "w_o_a", "w_o_b", "w_o_m", "w_out"),
                 "ffn2": ("ffn2_w_in", "ffn2_w_out")}
GROUPS = {"ffn1_in": ("ffn1_w_in",), "ffn1_out": ("ffn1_w_out",),
          "mix": ("w_in", "w_gate", "w_mem_kv", "w_o_a", "w_o_b", "w_o_m", "w_out"),
          "ffn2_in": ("ffn2_w_in",), "ffn2_out": ("ffn2_w_out",)}
COLUMN_SHARDED = ("ffn1_w_in", "ffn2_w_in", "w_in", "w_gate", "w_o_a", "w_o_b", "w_o_m")
KEPT_SHARD_MAJOR = ("ffn1_w_in", "ffn2_w_in", "w_gate")
GAINS = ("ffn1_norm_pre", "ffn1_norm_post", "mix_norm_pre", "mem_norm", "mix_norm_post", "ffn2_norm_pre",
         "ffn2_norm_post")
SMALL_ROWS = 16


def _pack_small(t):
    sinks = jnp.pad(t["sinks"], ((0, 0), (0, D_MODEL - t["sinks"].shape[1])))
    rows = [t[k] for k in GAINS] + [t["b_gate"].reshape(3, D_MODEL), sinks]
    packed = jnp.concatenate(rows, axis=0)
    return jnp.pad(packed, ((0, SMALL_ROWS - packed.shape[0]), (0, 0)))


def _unpack_small(p):
    out = {k: p[i:i + 1] for i, k in enumerate(GAINS)}
    out["b_gate"] = p[7:10].reshape(1, 3 * D_MODEL)
    out["sinks"] = p[10:11, :4]
    return out


def kernel(x, mem, ffn1_norm_pre, ffn1_w_in, ffn1_w_out, ffn1_norm_post, mix_norm_pre, w_in, sinks, mem_norm, w_mem_kv, w_gate, b_gate, w_o_a, w_o_b, w_o_m, w_out, mix_norm_post, ffn2_norm_pre, ffn2_w_in, ffn2_w_out, ffn2_norm_post, loss_target, m_ffn1_norm_pre, m_ffn1_w_in, m_ffn1_w_out, m_ffn1_norm_post, m_mix_norm_pre, m_w_in, m_sinks, m_mem_norm, m_w_mem_kv, m_w_gate, m_b_gate, m_w_o_a, m_w_o_b, m_w_o_m, m_w_out, m_mix_norm_post, m_ffn2_norm_pre, m_ffn2_w_in, m_ffn2_w_out, m_ffn2_norm_post, v_ffn1_norm_pre, v_ffn1_w_in, v_ffn1_w_out, v_ffn1_norm_post, v_mix_norm_pre, v_w_in, v_sinks, v_mem_norm, v_w_mem_kv, v_w_gate, v_b_gate, v_w_o_a, v_w_o_b, v_w_o_m, v_w_out, v_mix_norm_post, v_ffn2_norm_pre, v_ffn2_w_in, v_ffn2_w_out, v_ffn2_norm_post):
    given = dict(locals())
    wt = {k: given[k] for k in WEIGHTS}
    mom = {k: given["m_" + k] for k in WEIGHTS}
    var = {k: given["v_" + k] for k in WEIGHTS}
    chip = (2 * lax.axis_index("x") + lax.axis_index("y")).astype(jnp.int32)
    me = chip.reshape(1)

    def landing_zone(own):
        return lax.dynamic_update_slice_in_dim(lax.empty((N_CHIPS,) + own.shape, own.dtype), own[None], chip, 0)

    shards = [wt[k][0].astype(BF16) for k in BIG]
    members = [[BIG.index(k) for k in GATHER_GROUPS[g]] for g in GATHER_ORDER]
    sems, shards, lands, token = chip_copies_start(
        shards, [landing_zone(s) for s in shards], members, False, "weight_gather_start")

    def weights_of(group, after):
        idx = members[GATHER_ORDER.index(group)]
        got = chip_copies_wait([shards[i] for i in idx], [lands[i] for i in idx], sems[GATHER_ORDER.index(group)],
                               after, False, f"weight_gather_wait_{group}")
        full = {}
        for k, g in zip(GATHER_GROUPS[group], got):
            if k in COLUMN_SHARDED:
                if k in ("ffn1_w_in", "ffn2_w_in"):
                    g = jnp.stack([g[0], g[2], g[1], g[3]])
                full[k] = jnp.swapaxes(g, 0, 1).reshape(g.shape[1], N_CHIPS * g.shape[2])
                if k == "w_in":
                    full[k] = to_kernel_heads(full[k])
            else:
                full[k] = g.reshape(N_CHIPS * g.shape[1], g.shape[2])
        return full

    in_flight = {}

    def send_grads(group, grads):
        def shard_major(k, g):
            if k in KEPT_SHARD_MAJOR:
                return g
            if k in COLUMN_SHARDED:
                return jnp.swapaxes(g.reshape(g.shape[0], N_CHIPS, g.shape[1] // N_CHIPS), 0, 1)
            return g.reshape(N_CHIPS, g.shape[0] // N_CHIPS, g.shape[1])

        own, wire = [], []
        for k in GROUPS[group]:
            g, rounded = grads[k] if isinstance(grads[k], (tuple, list)) else (grads[k], None)
            g = shard_major(k, from_kernel_heads(g) if k == "w_in" else g)
            own.append(g)
            wire.append(g.astype(BF16) if rounded is None else shard_major(k, rounded))
        zones = [landing_zone(lax.dynamic_index_in_dim(b, chip, 0, keepdims=False)) for b in wire]
        pair, wire, zones, sent = chip_copies_start(
            wire, zones, [list(range(len(wire)))], True, f"grad_scatter_start_{group}")
        in_flight[group] = (own, wire, zones, pair[0], sent)
        return sent

    gains = {k: wt[k] for k in GAINS}
    sq, dx, grads = layer_step(
        x[0], mem[0], loss_target[0], gains, sinks[0], b_gate, weights_of, send_grads, token[0, 0])
    loss = lax.psum(0.5 * sq[0, 0] / D_MODEL, ("x", "y", "c"))

    res = {}
    after = in_flight["ffn1_out"][4]
    swaps = []
    for stage in (("ffn2_in", "ffn2_out", "mix", "ffn1_in"), ("ffn1_out",)):
        names, parts = [], []
        for group in stage:
            own, wire, zones, pair, _ = in_flight[group]
            received = chip_copies_wait(wire, zones, pair, after, True, f"grad_scatter_wait_{group}")
            for k, g, r in zip(GROUPS[group], own, received):
                names.append(k)
                parts.append(chip_partial_sum(me, g, r, f"{k}_chip_sum"))
        pair, parts, lands, after = sibling_copies_start(parts, f"sibling_start_{stage[-1]}")
        swaps.append((stage[-1], names, parts, lands, pair))
    small_all = small_all_gather(_pack_small(grads), "small_grad_gather")
    packed = adamw_small(small_all, _pack_small(wt), _pack_small(mom), _pack_small(var), "small_adamw")
    after = packed[0]
    for tag, names, parts, lands, pair in swaps:
        sibs = sibling_copies_wait(parts, lands, pair, after, f"sibling_wait_{tag}")
        for k, p, s in zip(names, parts, sibs):
            res[k] = [t[None] for t in adamw_pair(p, s, wt[k][0], mom[k][0], var[k][0], f"{k}_adamw")]
        after = res[names[-1]][0]
    for idx, p in enumerate(packed):
        for k, t in _unpack_small(p).items():
            res.setdefault(k, [None] * 4)[idx] = t

    return (loss, dx[None], *[res[k][0] for k in WEIGHTS], *[res[k][1] for k in WEIGHTS],
            *[res[k][2] for k in WEIGHTS], *[res[k][3] for k in WEIGHTS])
```

```python
import functools

import jax
import jax.numpy as jnp
from jax import lax
from jax.experimental import pallas as pl
from jax.experimental.pallas import tpu as pltpu

F32 = jnp.float32
BF16 = jnp.bfloat16

D_MODEL = 1024
D_FF = 2816
HEAD = 128
N_CHIPS = 4
N_DEV = 8
EPS = 1e-6
NEG_INF = -1e30
ROPE_THETA = 10000.0
ATT_SCALE = HEAD ** -0.5

ADAM_LR = 0.001
ADAM_B1 = 0.9
ADAM_B2 = 0.999
ADAM_EPS = 1e-08
ADAM_WD = 0.01
ADAM_STEP = 10

VMEM_LIMIT = 52 * 2 ** 20
VMEM_LIMIT_LARGE = 60 * 2 ** 20
MESH = pl.DeviceIdType.MESH

QKV_W = 3840
DIL = ((128, 1), (512, 4), (2048, 16))
B_BASE, MQ, A_BASE = 0, 8, 12
_AQ, _AK, _AV, _BQ, _BK, _BV, _MQ = 0, 6, 12, 18, 22, 24, 26
HEAD_ORDER = tuple(
    [h for j in range(2) for h in (_BQ + 2 * j, _BQ + 2 * j + 1, _BK + j, _BV + j)]
    + [_MQ + i for i in range(4)]
    + [h for g in range(3) for i in range(2) for h in (_AQ + 2 * g + i, _AK + 2 * g + i, _AV + 2 * g + i)])
ROTARY_HEADS = tuple(p for p, h in enumerate(HEAD_ORDER) if h < _AV or _BQ <= h < _BV)


def to_kernel_heads(w):
    return jnp.concatenate([w[..., h * HEAD:(h + 1) * HEAD] for h in HEAD_ORDER], axis=-1)


def from_kernel_heads(w):
    place = {h: p for p, h in enumerate(HEAD_ORDER)}
    return jnp.concatenate([w[..., place[h] * HEAD:(place[h] + 1) * HEAD] for h in range(len(HEAD_ORDER))], axis=-1)

TM = 512
FF_T = D_FF // 2


def _params(*sem):
    return pltpu.CompilerParams(dimension_semantics=sem, vmem_limit_bytes=VMEM_LIMIT)


def _dot(a, b):
    return jnp.dot(a, b, preferred_element_type=F32)


def _dot_nt(a, b):
    return lax.dot_general(a, b, (((1,), (1,)), ((), ())), preferred_element_type=F32)


def _dot_tn(a, b):
    return lax.dot_general(a, b, (((0,), (0,)), ((), ())), preferred_element_type=F32)


def _rstd(x):
    return lax.rsqrt(jnp.mean(x * x, axis=-1, keepdims=True) + EPS)


def _sigmoid(x):
    return 0.5 * jnp.tanh(0.5 * x) + 0.5


def _ffn_perm(k):
    return (k % 2) * 2 + k // 2


UNREAD = pl.BlockSpec(memory_space=pl.ANY)


def _resident(arr):
    return pl.BlockSpec(arr.shape, lambda *_: (0,) * arr.ndim, pipeline_mode=pl.Buffered(1))


def rms_scale(x, g, name, after):
    T, D = x.shape
    tm = 1024

    def body(x_ref, g_ref, _, o_ref):
        v = x_ref[...]
        o_ref[...] = (v * _rstd(v) * g_ref[...]).astype(BF16)

    spec = pl.BlockSpec((tm, D), lambda i: (i, 0))
    return pl.pallas_call(
        body, name=name, grid=(T // tm,), in_specs=[spec, _resident(g), UNREAD], out_specs=spec,
        out_shape=jax.ShapeDtypeStruct((T, D), BF16), compiler_params=_params("parallel"),
    )(x, g, after)


def ffn_in(h, g, w, name, xn=None):
    T, D = h.shape
    normed = xn is not None

    def body(h_ref, g_ref, w_ref, *outs):
        if normed:
            xn, (gu_ref, a_ref) = h_ref[...], outs
        else:
            xn_ref, gu_ref, a_ref = outs
            x = h_ref[...]
            xn = (x * _rstd(x) * g_ref[...]).astype(BF16)
            xn_ref[...] = xn
        for j in range(2):
            gu = _dot(xn, w_ref[:, j * 2 * FF_T:(j + 1) * 2 * FF_T])
            gu_ref[:, j * 2 * FF_T:(j + 1) * 2 * FF_T] = gu.astype(BF16)
            gate, up = gu[:, :FF_T], gu[:, FF_T:]
            a_ref[:, j * FF_T:(j + 1) * FF_T] = (gate * _sigmoid(gate) * up).astype(BF16)

    def rows(width):
        return pl.BlockSpec((TM, width), lambda i: (i, 0))

    res = pl.pallas_call(
        body, name=name,
        grid=(T // TM,),
        in_specs=[rows(D), _resident(g), _resident(w)],
        out_specs=[rows(D)] * (not normed) + [rows(2 * D_FF), rows(D_FF)],
        out_shape=[jax.ShapeDtypeStruct((T, D), BF16)] * (not normed)
                  + [jax.ShapeDtypeStruct((T, 2 * D_FF), BF16), jax.ShapeDtypeStruct((T, D_FF), BF16)],
        compiler_params=_params("parallel"),
    )(xn if normed else h, g, w)
    return (xn, *res) if normed else tuple(res)


def mm_norm_res(a, w, h_in, g, coef, name, target=None):
    T, K = a.shape
    D = w.shape[1]
    final = target is not None

    def body(*refs):
        if final:
            a_ref, w_ref, h_ref, g_ref, t_ref, f_ref, o_ref, l_ref = refs
        else:
            a_ref, w_ref, h_ref, g_ref, f_ref, o_ref = refs
        f = _dot(a_ref[...], w_ref[...])
        f_ref[...] = f
        y = h_ref[...] + coef * (f * _rstd(f) * g_ref[...])
        if final:
            err = y - t_ref[...]
            o_ref[...] = err * (1.0 / D)

            @pl.when(pl.program_id(0) == 0)
            def _():
                l_ref[...] = jnp.zeros_like(l_ref)

            l_ref[...] += jnp.sum(err * err)
        else:
            o_ref[...] = y

    row = pl.BlockSpec((TM, D), lambda i: (i, 0))
    in_specs = [pl.BlockSpec((TM, K), lambda i: (i, 0)),
                _resident(w),
                row, pl.BlockSpec((1, D), lambda i: (0, 0))]
    out_specs = [row, row]
    out_shape = [jax.ShapeDtypeStruct((T, D), F32), jax.ShapeDtypeStruct((T, D), F32)]
    args = [a, w, h_in, g]
    if final:
        in_specs.append(row)
        args.append(target)
        out_specs.append(pl.BlockSpec((8, 128), lambda i: (0, 0)))
        out_shape.append(jax.ShapeDtypeStruct((8, 128), F32))
    return pl.pallas_call(
        body, name=name, grid=(T // TM,), in_specs=in_specs, out_specs=out_specs, out_shape=out_shape,
        compiler_params=_params("arbitrary"),
    )(*args)


def _rope(x, cos, sin_signed):
    return x * cos + pltpu.roll(x, HEAD // 2, axis=1) * sin_signed


def _unrope(x, cos, sin_signed):
    return x * cos - pltpu.roll(x, HEAD // 2, axis=1) * sin_signed


def mix_in(h, g, w, w_gate, b_gate, cos, sin_signed, name):
    T, D = h.shape
    tn = 768

    def body(h_ref, g_ref, w_ref, wg_ref, b_ref, c_ref, s_ref, u_ref, o_ref, gt_ref):
        x = h_ref[...]
        u = (x * _rstd(x) * g_ref[...]).astype(BF16)
        u_ref[...] = u
        c, s = c_ref[...], s_ref[...]
        for j in range(QKV_W // tn):
            acc = _dot(u, w_ref[:, j * tn:(j + 1) * tn])
            for hd in range(tn // HEAD):
                head = j * (tn // HEAD) + hd
                part = acc[:, hd * HEAD:(hd + 1) * HEAD]
                if head in ROTARY_HEADS:
                    part = _rope(part, c, s)
                o_ref[:, head * HEAD:(head + 1) * HEAD] = part.astype(BF16)
        for j in range(w_gate.shape[1] // tn):
            cols = slice(j * tn, (j + 1) * tn)
            gt_ref[:, cols] = _sigmoid(_dot(u, wg_ref[:, cols]) + b_ref[:, cols]).astype(BF16)

    def rows(width):
        return pl.BlockSpec((TM, width), lambda i: (i, 0))

    return pl.pallas_call(
        body, name=name,
        grid=(T // TM,),
        in_specs=[rows(D), _resident(g), _resident(w), _resident(w_gate), _resident(b_gate), rows(HEAD), rows(HEAD)],
        out_specs=[rows(D), rows(QKV_W), rows(w_gate.shape[1])],
        out_shape=[jax.ShapeDtypeStruct((T, D), BF16), jax.ShapeDtypeStruct((T, QKV_W), BF16),
                   jax.ShapeDtypeStruct((T, w_gate.shape[1]), BF16)],
        compiler_params=_params("parallel"),
    )(h, g, w, w_gate, b_gate, cos, sin_signed)


def gate_merge_out(gt, o_a, o_b, o_m, w_a, w_b, w_m, w_out, h_in, g, name):
    T = gt.shape[0]
    D = D_MODEL

    def body(gt_ref, oa_ref, ob_ref, om_ref, wa_ref, wb_ref, wm_ref, wo_ref, h_ref, g_ref, m_ref, f_ref, o_ref):
        acc = gt_ref[:, :D].astype(F32) * _dot(oa_ref[...], wa_ref[...])
        acc += gt_ref[:, D:2 * D].astype(F32) * _dot(ob_ref[...], wb_ref[...])
        acc += gt_ref[:, 2 * D:].astype(F32) * _dot(om_ref[...], wm_ref[...])
        merged = acc.astype(BF16)
        m_ref[...] = merged
        f = _dot(merged, wo_ref[...])
        f_ref[...] = f
        o_ref[...] = h_ref[...] + f * _rstd(f) * g_ref[...]

    def rows(width):
        return pl.BlockSpec((TM, width), lambda i: (i, 0))

    return pl.pallas_call(
        body, name=name, grid=(T // TM,),
        in_specs=[rows(3 * D), rows(o_a.shape[1]), rows(o_b.shape[1]), rows(o_m.shape[1]),
                  _resident(w_a), _resident(w_b), _resident(w_m), _resident(w_out), rows(D), _resident(g)],
        out_specs=[rows(D), rows(D), rows(D)],
        out_shape=[jax.ShapeDtypeStruct((T, D), BF16), jax.ShapeDtypeStruct((T, D), F32),
                   jax.ShapeDtypeStruct((T, D), F32)],
        compiler_params=_params("parallel"),
    )(gt, o_a, o_b, o_m, w_a, w_b, w_m, w_out, h_in, g)


def _band_rows(start, r):
    return pl.ds(start, HEAD) if r == 1 else pl.ds(start, HEAD, stride=r)


def _band_mask(max_dist, first_has_prev):
    row = lax.broadcasted_iota(jnp.int32, (HEAD, 2 * HEAD), 0)
    col = lax.broadcasted_iota(jnp.int32, (HEAD, 2 * HEAD), 1)
    dist = row + HEAD - col
    band = (dist >= 0) & (dist <= max_dist)
    return band, band & (col >= jnp.where(first_has_prev, 0, HEAD))


def _stack(parts):
    return parts[0] if len(parts) == 1 else jnp.concatenate(parts, axis=0)


def _band_specs(BT, SB, nsub, base, grp):
    stride = grp + 2

    def cur(off, width):
        return pl.BlockSpec((BT, width * HEAD), lambda h, i: (i, (base + h * stride + off) // width))

    def prev(off):
        return pl.BlockSpec((SB, HEAD), lambda h, i: (jnp.maximum(i * nsub - 1, 0), base + h * stride + off))

    return cur(0, grp), cur(grp, 1), prev(grp), cur(grp + 1, 1), prev(grp + 1)


def band_fwd(qkv, sinks, *, r, base, hkv, grp, max_dist, out_dtype, name):
    T, W = qkv.shape
    SB = HEAD * r
    BT = min(2048, T)
    nsub, nib = BT // SB, T // BT
    hq = hkv * grp
    heads = [slice(g * HEAD, (g + 1) * HEAD) for g in range(grp)]

    def body(sink_ref, q_ref, kc_ref, kp_ref, vc_ref, vp_ref, o_ref, l_ref, qf, kf, vf):
        kvh, ib = pl.program_id(0), pl.program_id(1)
        qf[...] = q_ref[...].astype(F32)
        kf[:SB] = kp_ref[...].astype(F32)
        kf[SB:] = kc_ref[...].astype(F32)
        vf[:SB] = vp_ref[...].astype(F32)
        vf[SB:] = vc_ref[...].astype(F32)
        band, band_first = _band_mask(max_dist, ib > 0)
        for c in range(r):
            k_old, v_old = kf[_band_rows(c, r)], vf[_band_rows(c, r)]
            for j in range(nsub):
                mask = band_first if j == 0 else band
                rows = _band_rows(j * SB + c, r)
                k_own, v_own = kf[_band_rows((j + 1) * SB + c, r)], vf[_band_rows((j + 1) * SB + c, r)]
                kcat = jnp.concatenate([k_old, k_own], axis=0).astype(BF16)
                vcat = jnp.concatenate([v_old, v_own], axis=0).astype(BF16)
                k_old, v_old = k_own, v_own
                s_all = _dot_nt(_stack([qf[rows, cols] for cols in heads]).astype(BF16), kcat) * ATT_SCALE
                probs, tots = [], []
                for g, cols in enumerate(heads):
                    s = jnp.where(mask, s_all[cols], NEG_INF)
                    sk = sink_ref[kvh * grp + g]
                    m = jnp.maximum(jnp.max(s, axis=-1, keepdims=True), sk)
                    p = jnp.exp(s - m)
                    tot = jnp.sum(p, axis=-1, keepdims=True) + jnp.exp(sk - m)
                    probs.append(p.astype(BF16))
                    tots.append(tot)
                    l_ref[rows, cols] = jnp.broadcast_to(m + jnp.log(tot), (HEAD, HEAD))
                o_all = _dot(_stack(probs), vcat)
                for g, cols in enumerate(heads):
                    o_ref[rows, cols] = (o_all[cols] / tots[g]).astype(out_dtype)

    out_spec = pl.BlockSpec((BT, grp * HEAD), lambda h, i: (i, h))
    return pl.pallas_call(
        body, name=name, grid=(hkv, nib),
        in_specs=[pl.BlockSpec(memory_space=pltpu.SMEM), *_band_specs(BT, SB, nsub, base, grp)],
        out_specs=[out_spec, out_spec],
        out_shape=[jax.ShapeDtypeStruct((T, hq * HEAD), out_dtype), jax.ShapeDtypeStruct((T, hq * HEAD), F32)],
        scratch_shapes=[pltpu.VMEM((BT, grp * HEAD), F32), pltpu.VMEM((SB + BT, HEAD), F32),
                        pltpu.VMEM((SB + BT, HEAD), F32)],
        compiler_params=_params("parallel", "arbitrary"),
    )(sinks, qkv, qkv, qkv, qkv, qkv)


def band_bwd(qkv, dqkv, do, o, lse, cos, sin_signed, sinks, *, r, base, hkv, grp, max_dist, name):
    T, W = qkv.shape
    SB = HEAD * r
    BT = min(max(2048, 2 * SB), T)
    nsub, nib = BT // SB, T // BT
    nblk = T // SB
    with_sink = sinks is not None
    heads = [slice(g * HEAD, (g + 1) * HEAD) for g in range(grp)]

    def body(*refs):
        if with_sink:
            sink_ref, refs = refs[0], refs[1:]
        (q_ref, kc_ref, kp_ref, vc_ref, vp_ref, qn_ref, do_ref, don_ref, o_ref, on_ref, l_ref, ln_ref,
         c_ref, s_ref, _) = refs[:15]
        out_ref = refs[15]
        ds_ref = refs[16] if with_sink else None
        qf, dof, of, kf, vf, dqf, dkf, dvf = refs[-8:]
        kvh, ib = pl.program_id(0), pl.program_id(1)
        for buf, cur_ref, nxt_ref in ((qf, q_ref, qn_ref), (dof, do_ref, don_ref), (of, o_ref, on_ref)):
            buf[:BT] = cur_ref[...].astype(F32)
            buf[BT:] = nxt_ref[...].astype(F32)
        kf[:SB] = kp_ref[...].astype(F32)
        kf[SB:] = kc_ref[...].astype(F32)
        vf[:SB] = vp_ref[...].astype(F32)
        vf[SB:] = vc_ref[...].astype(F32)
        band, band_first = _band_mask(max_dist, ib > 0)
        if with_sink:
            @pl.when(ib == 0)
            def _():
                ds_ref[...] = jnp.zeros_like(ds_ref)

        def grads(rows, logzs, keys, vals, mask):
            q = _stack([qf[rows, cols] for cols in heads]).astype(BF16)
            dout = _stack([dof[rows, cols] for cols in heads]).astype(BF16)
            s_all = _dot_nt(q, keys) * ATT_SCALE
            dp_all = _dot_nt(dout, vals)
            probs, dss, deltas = [], [], []
            for g, cols in enumerate(heads):
                delta = jnp.sum(dof[rows, cols] * of[rows, cols], axis=-1, keepdims=True)
                p = jnp.exp(jnp.where(mask, s_all[cols], NEG_INF) - logzs[g][:, :1])
                probs.append(p.astype(BF16))
                dss.append((p * (dp_all[cols] - delta) * ATT_SCALE).astype(BF16))
                deltas.append(delta)
            return q, dout, _stack(probs), _stack(dss), deltas

        row = lax.broadcasted_iota(jnp.int32, (HEAD, HEAD), 0)
        col = lax.broadcasted_iota(jnp.int32, (HEAD, HEAD), 1)
        reach = col >= row + jnp.where(ib < nib - 1, HEAD - max_dist, 2 * HEAD)
        for c in range(r):
            k_old, v_old = kf[_band_rows(c, r)], vf[_band_rows(c, r)]
            dk_own = dv_own = None
            for j in range(nsub):
                rows = _band_rows(j * SB + c, r)
                k_own, v_own = kf[_band_rows((j + 1) * SB + c, r)], vf[_band_rows((j + 1) * SB + c, r)]
                kcat = jnp.concatenate([k_old, k_own], axis=0).astype(BF16)
                vcat = jnp.concatenate([v_old, v_own], axis=0).astype(BF16)
                logzs = [l_ref[rows, cols] for cols in heads]
                q, dout, p, ds, deltas = grads(rows, logzs, kcat, vcat, band_first if j == 0 else band)
                dq = _dot(ds, kcat)
                for g, cols in enumerate(heads):
                    dqf[rows, cols] = dq[cols]
                    if with_sink:
                        p_sink = jnp.exp(sink_ref[kvh * grp + g] - logzs[g][:, :1])
                        ds_ref[g * 8:(g + 1) * 8] += jnp.sum(p_sink * deltas[g])
                dk, dv = _dot_tn(ds, q), _dot_tn(p, dout)
                if j > 0:
                    done = _band_rows((j - 1) * SB + c, r)
                    dkf[done] = dk_own + dk[:HEAD]
                    dvf[done] = dv_own + dv[:HEAD]
                dk_own, dv_own = dk[HEAD:], dv[HEAD:]
                k_old, v_old = k_own, v_own
            logzs = [ln_ref[_band_rows(c, r), cols] for cols in heads]
            q, dout, p, ds, _ = grads(_band_rows(BT + c, r), logzs, k_old.astype(BF16), v_old.astype(BF16), reach)
            done = _band_rows((nsub - 1) * SB + c, r)
            dkf[done] = dk_own + _dot_tn(ds, q)
            dvf[done] = dv_own + _dot_tn(p, dout)

        cs, sn = c_ref[...], s_ref[...]
        for cols in heads:
            out_ref[:, cols] = _unrope(dqf[:, cols], cs, sn).astype(BF16)
        out_ref[:, grp * HEAD:(grp + 1) * HEAD] = _unrope(dkf[...], cs, sn).astype(BF16)
        out_ref[:, (grp + 1) * HEAD:] = dvf[...].astype(BF16)

    def nxt_row(i):
        return jnp.minimum((i + 1) * nsub, nblk - 1)

    stride = grp + 2
    q_next = pl.BlockSpec((SB, grp * HEAD), lambda h, i: (nxt_row(i), (base + h * stride) // grp))
    head_cur = pl.BlockSpec((BT, grp * HEAD), lambda h, i: (i, h))
    head_next = pl.BlockSpec((SB, grp * HEAD), lambda h, i: (nxt_row(i), h))
    table = pl.BlockSpec((BT, HEAD), lambda h, i: (i, 0))

    in_specs = [*_band_specs(BT, SB, nsub, base, grp), q_next,
                head_cur, head_next, head_cur, head_next, head_cur, head_next, table, table, UNREAD]
    args = [qkv, qkv, qkv, qkv, qkv, qkv, do, do, o, o, lse, lse, cos, sin_signed, dqkv]
    out_specs = [pl.BlockSpec((BT, stride * HEAD), lambda h, i: (i, base // stride + h))]
    out_shape = [jax.ShapeDtypeStruct(dqkv.shape, dqkv.dtype)]
    if with_sink:
        in_specs.insert(0, pl.BlockSpec(memory_space=pltpu.SMEM))
        args.insert(0, sinks)
        out_specs.append(pl.BlockSpec((None, grp * 8, HEAD), lambda h, i: (h, 0, 0)))
        out_shape.append(jax.ShapeDtypeStruct((hkv, grp * 8, HEAD), F32))
    wide = pltpu.VMEM((BT + SB, grp * HEAD), F32)
    tall = pltpu.VMEM((SB + BT, HEAD), F32)
    grad = pltpu.VMEM((BT, HEAD), F32)
    return pl.pallas_call(
        body, name=name, grid=(hkv, nib), in_specs=in_specs, out_specs=out_specs, out_shape=out_shape,
        input_output_aliases={len(args) - 1: 0},
        scratch_shapes=[wide, wide, wide, tall, tall, pltpu.VMEM((BT, grp * HEAD), F32), grad, grad],
        compiler_params=pltpu.CompilerParams(dimension_semantics=("parallel", "arbitrary"),
                                             vmem_limit_bytes=VMEM_LIMIT_LARGE),
    )(*args)


def merge_groups(outs, lses, name):
    T, Wd = outs[0].shape
    tm = 1024

    def body(o0, o1, o2, l0, l1, l2, out_ref, lt_ref):
        a, b, c = l0[...], l1[...], l2[...]
        m = jnp.maximum(jnp.maximum(a, b), c)
        wa, wb, wc = jnp.exp(a - m), jnp.exp(b - m), jnp.exp(c - m)
        z = wa + wb + wc
        out_ref[...] = ((wa * o0[...] + wb * o1[...] + wc * o2[...]) / z).astype(BF16)
        lt_ref[...] = m + jnp.log(z)

    spec = pl.BlockSpec((tm, Wd), lambda i: (i, 0))
    return pl.pallas_call(
        body, name=name, grid=(T // tm,), in_specs=[spec] * 6, out_specs=[spec, spec],
        out_shape=[jax.ShapeDtypeStruct((T, Wd), BF16), jax.ShapeDtypeStruct((T, Wd), F32)],
        compiler_params=_params("parallel"),
    )(*outs, *lses)


M_HEADS = 4


def mem_kv(mem, g, w, name):
    n, D = mem.shape

    def body(m_ref, g_ref, w_ref, mn_ref, kv_ref):
        x = m_ref[...]
        mn = (x * _rstd(x) * g_ref[...]).astype(BF16)
        mn_ref[...] = mn
        kv_ref[...] = _dot(mn, w_ref[...]).astype(BF16)

    return pl.pallas_call(
        body, name=name,
        out_shape=[jax.ShapeDtypeStruct((n, D), BF16), jax.ShapeDtypeStruct((n, w.shape[1]), BF16)],
        compiler_params=pltpu.CompilerParams(vmem_limit_bytes=VMEM_LIMIT),
    )(mem, g, w)


def mem_fwd(qkv, mkv, name):
    T = qkv.shape[0]
    n = mkv.shape[0]
    RB = 1024

    def body(q_ref, kv_ref, o_ref, l_ref):
        for h in range(M_HEADS):
            cols = slice(h * HEAD, (h + 1) * HEAD)
            s = _dot_nt(q_ref[:, cols], kv_ref[:, cols]) * ATT_SCALE
            m = jnp.max(s, axis=-1, keepdims=True)
            p = jnp.exp(s - m)
            den = jnp.sum(p, axis=-1, keepdims=True)
            vals = kv_ref[:, (M_HEADS + h) * HEAD:(M_HEADS + h + 1) * HEAD]
            o_ref[:, cols] = (_dot(p.astype(BF16), vals) / den).astype(BF16)
            l_ref[:, cols] = jnp.broadcast_to(m + jnp.log(den), (RB, HEAD))

    out = pl.BlockSpec((RB, M_HEADS * HEAD), lambda i: (i, 0))
    return pl.pallas_call(
        body, name=name, grid=(T // RB,),
        in_specs=[pl.BlockSpec((RB, M_HEADS * HEAD), lambda i: (i, MQ // M_HEADS)), _resident(mkv)],
        out_specs=[out, out],
        out_shape=[jax.ShapeDtypeStruct((T, M_HEADS * HEAD), BF16), jax.ShapeDtypeStruct((T, M_HEADS * HEAD), F32)],
        compiler_params=_params("parallel"),
    )(qkv, mkv)


def mem_bwd(qkv, dqkv, mkv, do, o, lse, name):
    T = qkv.shape[0]
    n = mkv.shape[0]
    RB = 1024

    def body(q_ref, kv_ref, do_ref, o_ref, l_ref, _, dq_ref, dk_ref, dv_ref):
        @pl.when(pl.program_id(0) == 0)
        def _():
            dk_ref[...] = jnp.zeros_like(dk_ref)
            dv_ref[...] = jnp.zeros_like(dv_ref)

        for h in range(M_HEADS):
            cols = slice(h * HEAD, (h + 1) * HEAD)
            keys, vals = kv_ref[:, cols], kv_ref[:, (M_HEADS + h) * HEAD:(M_HEADS + h + 1) * HEAD]
            q, dout = q_ref[:, cols], do_ref[:, cols]
            delta = jnp.sum(dout.astype(F32) * o_ref[:, cols].astype(F32), axis=-1, keepdims=True)
            p = jnp.exp(_dot_nt(q, keys) * ATT_SCALE - l_ref[:, cols][:, :1])
            ds = (p * (_dot_nt(dout, vals) - delta) * ATT_SCALE).astype(BF16)
            dq_ref[:, cols] = _dot(ds, keys).astype(BF16)
            dk_ref[:, cols] += _dot_tn(ds, q)
            dv_ref[:, cols] += _dot_tn(p.astype(BF16), dout)

    wide = M_HEADS * HEAD
    tok = pl.BlockSpec((RB, wide), lambda i: (i, 0))
    q_cols = pl.BlockSpec((RB, wide), lambda i: (i, MQ // M_HEADS))
    slot = pl.BlockSpec((n, wide), lambda i: (0, 0))
    return pl.pallas_call(
        body, name=name, grid=(T // RB,),
        in_specs=[q_cols, _resident(mkv), tok, tok, tok, UNREAD],
        out_specs=[q_cols, slot, slot],
        out_shape=[jax.ShapeDtypeStruct(dqkv.shape, dqkv.dtype),
                   jax.ShapeDtypeStruct((n, wide), F32), jax.ShapeDtypeStruct((n, wide), F32)],
        input_output_aliases={5: 0},
        compiler_params=_params("arbitrary"),
    )(qkv, mkv, do, o, lse, dqkv)


def mem_kv_bwd(mem, g, mem_n, w, dmkv, name):
    n, D = mem.shape

    def body(m_ref, g_ref, mn_ref, w_ref, d_ref, dw_ref, dg_ref):
        d = d_ref[...].astype(BF16)
        dw_ref[...] = _dot_tn(mn_ref[...], d)
        x = m_ref[...]
        dg_ref[...] = jnp.sum(_dot_nt(d, w_ref[...]) * (x * _rstd(x)), axis=0, keepdims=True)

    return pl.pallas_call(
        body, name=name,
        out_shape=[jax.ShapeDtypeStruct(w.shape, F32), jax.ShapeDtypeStruct((1, D), F32)],
        compiler_params=pltpu.CompilerParams(vmem_limit_bytes=VMEM_LIMIT),
    )(mem, g, mem_n, w, dmkv)


def _rms_bwd(dn, f, g):
    r = _rstd(f)
    fhat = f * r
    dfhat = dn * g
    df = r * (dfhat - fhat * jnp.mean(dfhat * fhat, axis=-1, keepdims=True))
    return df, jnp.sum(dn * fhat, axis=0, keepdims=True)


def ffn_tokens_bwd(dh, f, h_in, gu, g_pre, g_post, w_in, w_out, coef, name, after):
    T, D = dh.shape

    def body(dh_ref, f_ref, h_ref, gu_ref, gpre_ref, gpost_ref, win_ref, wout_ref, _,
             df_ref, dgu_ref, dhin_ref, dgpre_ref, dgpost_ref, dxn_ref):
        i, j = pl.program_id(0), pl.program_id(1)

        @pl.when(j == 0)
        def _():
            @pl.when(i == 0)
            def _():
                dgpre_ref[...] = jnp.zeros_like(dgpre_ref)
                dgpost_ref[...] = jnp.zeros_like(dgpost_ref)

            df, dg_post = _rms_bwd(coef * dh_ref[...], f_ref[...], gpost_ref[...])
            dgpost_ref[...] += dg_post
            df_ref[...] = df.astype(BF16)

        for jj in range(2):
            @pl.when(j == jj)
            def _(jj=jj):
                lo, mid, hi = 2 * jj * FF_T, (2 * jj + 1) * FF_T, (2 * jj + 2) * FF_T
                da = _dot_nt(df_ref[...], wout_ref[jj * FF_T:(jj + 1) * FF_T, :])
                gate = gu_ref[:, :FF_T].astype(F32)
                up = gu_ref[:, FF_T:].astype(F32)
                sig = _sigmoid(gate)
                dgate = (da * up * sig * (1.0 + gate * (1.0 - sig))).astype(BF16)
                dup = (da * gate * sig).astype(BF16)
                dgu_ref[:, :FF_T] = dgate
                dgu_ref[:, FF_T:] = dup
                part = _dot_nt(dgate, win_ref[:, lo:mid]) + _dot_nt(dup, win_ref[:, mid:hi])
                if jj == 0:
                    dxn_ref[...] = part
                else:
                    h = h_ref[...]
                    r = _rstd(h)
                    xhat = h * r
                    dxn = dxn_ref[...] + part
                    dxhat = dxn * gpre_ref[...]
                    dhin_ref[...] = dh_ref[...] + r * (dxhat - xhat * jnp.mean(dxhat * xhat, axis=-1, keepdims=True))
                    dgpre_ref[...] += jnp.sum(dxn * xhat, axis=0, keepdims=True)

    row = pl.BlockSpec((TM, D), lambda i, j: (i, 0))
    wide = pl.BlockSpec((TM, 2 * FF_T), lambda i, j: (i, j))
    vec = pl.BlockSpec((1, D), lambda i, j: (0, 0))
    return pl.pallas_call(
        body, name=name, grid=(T // TM, 2),
        in_specs=[row, row, row, wide, _resident(g_pre), _resident(g_post), _resident(w_in), _resident(w_out),
                  UNREAD],
        out_specs=[row, wide, row, vec, vec],
        out_shape=[jax.ShapeDtypeStruct((T, D), BF16), jax.ShapeDtypeStruct((T, 2 * D_FF), BF16),
                   jax.ShapeDtypeStruct((T, D), F32), jax.ShapeDtypeStruct((1, D), F32),
                   jax.ShapeDtypeStruct((1, D), F32)],
        scratch_shapes=[pltpu.VMEM((TM, D), F32)],
        compiler_params=pltpu.CompilerParams(dimension_semantics=("arbitrary", "arbitrary"),
                                             vmem_limit_bytes=VMEM_LIMIT_LARGE),
    )(dh, f, h_in, gu, g_pre, g_post, w_in, w_out, after)


def mm_nt_norm_bwd(pieces, h_in, dh_out, g, name, after):
    T, D = h_in.shape

    def body(*refs):
        ab = refs[:2 * len(pieces)]
        h_ref, dh_ref, g_ref, _, o_ref, dg_ref = refs[2 * len(pieces):]
        dxn = _dot_nt(ab[0][...], ab[1][...])
        for p in range(1, len(pieces)):
            dxn += _dot_nt(ab[2 * p][...], ab[2 * p + 1][...])
        h = h_ref[...]
        r = _rstd(h)
        xhat = h * r
        dxhat = dxn * g_ref[...]
        o_ref[...] = dh_ref[...] + r * (dxhat - xhat * jnp.mean(dxhat * xhat, axis=-1, keepdims=True))

        @pl.when(pl.program_id(0) == 0)
        def _():
            dg_ref[...] = jnp.zeros_like(dg_ref)

        dg_ref[...] += jnp.sum(dxn * xhat, axis=0, keepdims=True)

    in_specs, args = [], []
    for a, w in pieces:
        in_specs += [pl.BlockSpec((TM, a.shape[1]), lambda i: (i, 0)), _resident(w)]
        args += [a, w]
    row = pl.BlockSpec((TM, D), lambda i: (i, 0))
    return pl.pallas_call(
        body, name=name, grid=(T // TM,),
        in_specs=in_specs + [row, row, _resident(g), UNREAD],
        out_specs=[row, pl.BlockSpec((1, D), lambda i: (0, 0))],
        out_shape=[jax.ShapeDtypeStruct((T, D), F32), jax.ShapeDtypeStruct((1, D), F32)],
        compiler_params=_params("arbitrary"),
    )(*args, h_in, dh_out, g, after)


def gate_merge_out_bwd(dh, f, g, w_out, gt, o_a, o_b, o_m, w_a, w_b, w_m, name, after):
    T = dh.shape[0]
    D = D_MODEL
    branch = ((o_a, w_a), (o_b, w_b), (o_m, w_m))

    def body(dh_ref, f_ref, g_ref, wo_ref, gt_ref, oa_ref, ob_ref, om_ref, wa_ref, wb_ref, wm_ref, _,
             df_ref, dg_ref, dgt_ref, dpa_ref, dpb_ref, dpm_ref, doa_ref, dob_ref, dom_ref, db_ref):
        @pl.when(pl.program_id(0) == 0)
        def _():
            db_ref[...] = jnp.zeros_like(db_ref)
            dg_ref[...] = jnp.zeros_like(dg_ref)

        df, dg = _rms_bwd(dh_ref[...], f_ref[...], g_ref[...])
        dg_ref[...] += dg
        df = df.astype(BF16)
        df_ref[...] = df
        dmf = _dot_nt(df, wo_ref[...])
        for x, (o_ref, w_ref, dp_ref, do_ref) in enumerate(((oa_ref, wa_ref, dpa_ref, doa_ref),
                                                           (ob_ref, wb_ref, dpb_ref, dob_ref),
                                                           (om_ref, wm_ref, dpm_ref, dom_ref))):
            cols = slice(x * D, (x + 1) * D)
            gx = gt_ref[:, cols].astype(F32)
            w = w_ref[...]
            dpre = dmf * _dot(o_ref[...], w) * gx * (1.0 - gx)
            dgt_ref[:, cols] = dpre.astype(BF16)
            db_ref[:, cols] += jnp.sum(dpre, axis=0, keepdims=True)
            dp = (dmf * gx).astype(BF16)
            dp_ref[...] = dp
            do_ref[...] = _dot_nt(dp, w).astype(BF16)

    def rows(width):
        return pl.BlockSpec((TM, width), lambda i: (i, 0))

    widths = [o.shape[1] for o, _ in branch]
    return pl.pallas_call(
        body, name=name, grid=(T // TM,),
        in_specs=[rows(D), rows(D), _resident(g), _resident(w_out), rows(3 * D)] + [rows(k) for k in widths]
                 + [_resident(w) for _, w in branch] + [UNREAD],
        out_specs=[rows(D), pl.BlockSpec((1, D), lambda i: (0, 0)), rows(3 * D), rows(D), rows(D), rows(D)]
                  + [rows(k) for k in widths] + [pl.BlockSpec((1, 3 * D), lambda i: (0, 0))],
        out_shape=[jax.ShapeDtypeStruct((T, D), BF16), jax.ShapeDtypeStruct((1, D), F32),
                   jax.ShapeDtypeStruct((T, 3 * D), BF16)] + [jax.ShapeDtypeStruct((T, D), BF16)] * 3
                  + [jax.ShapeDtypeStruct((T, k), BF16) for k in widths]
                  + [jax.ShapeDtypeStruct((1, 3 * D), F32)],
        compiler_params=_params("arbitrary"),
    )(dh, f, g, w_out, gt, o_a, o_b, o_m, w_a, w_b, w_m, after)


def mm_tn(x, dy, tm, tn, name, shard_major=False, perm=None, slabs=1, after=None, wire=False):
    T, M = x.shape
    N = dy.shape[1]
    tk = min(2048, T)
    perm = perm or (lambda j: j)
    w = tn // slabs

    def body(x_ref, dy_ref, *rest):
        o_ref = rest[-2] if wire else rest[-1]

        @pl.when(pl.program_id(2) == 0)
        def _():
            o_ref[...] = jnp.zeros_like(o_ref)

        acc = _dot_tn(x_ref[...], dy_ref[...])
        if shard_major:
            for s in range(slabs):
                o_ref[s] += acc[:, s * w:(s + 1) * w]
        else:
            o_ref[...] += acc
        if wire:
            @pl.when(pl.program_id(2) == T // tk - 1)
            def _():
                rest[-1][...] = o_ref[...].astype(BF16)

    if shard_major:
        out_spec = pl.BlockSpec((slabs, tm, w), lambda i, j, k: (perm(j), i, 0))
        out_shape = jax.ShapeDtypeStruct((N // w, M, w), F32)
    else:
        out_spec = pl.BlockSpec((tm, tn), lambda i, j, k: (i, j))
        out_shape = jax.ShapeDtypeStruct((M, N), F32)
    return pl.pallas_call(
        body, name=name, grid=(M // tm, N // tn, T // tk),
        in_specs=[pl.BlockSpec((tk, tm), lambda i, j, k: (k, i)),
                  pl.BlockSpec((tk, tn), lambda i, j, k: (k, j))] + ([] if after is None else [UNREAD]),
        out_specs=[out_spec, out_spec] if wire else out_spec,
        out_shape=[out_shape, jax.ShapeDtypeStruct(out_shape.shape, BF16)] if wire else out_shape,
        compiler_params=_params("parallel", "parallel", "arbitrary"),
    )(x, dy, *([] if after is None else [after]))


def rope_tables(T, zero):
    half = HEAD // 2
    inv = ROPE_THETA ** (-jnp.arange(half, dtype=F32) / half)
    ang = (jnp.arange(T).astype(F32) + zero)[:, None] * inv[None, :]
    cos, sin = jnp.cos(ang), jnp.sin(ang)
    return jnp.concatenate([cos, cos], axis=1), jnp.concatenate([-sin, sin], axis=1)


def layer_step(x, mem, target, gains, sinks, b_gate, weights_of, send_grads, zero):
    T = x.shape[0]
    cos, sin_signed = rope_tables(T, zero)
    no_sink = jnp.full((2,), NEG_INF, F32)

    xn1 = rms_scale(x, gains["ffn1_norm_pre"], "ffn1_norm", cos)
    w = dict(weights_of("ffn1_in", xn1))
    xn1, gu1, a1 = ffn_in(x, gains["ffn1_norm_pre"], w["ffn1_w_in"], "ffn1_in", xn=xn1)
    w.update(weights_of("ffn1_out", xn1))
    f1, h1 = mm_norm_res(a1, w["ffn1_w_out"], x, gains["ffn1_norm_post"], 0.5, "ffn1_out")
    w.update(weights_of("mix", f1))
    u, qkv, gt = mix_in(h1, gains["mix_norm_pre"], w["w_in"], w["w_gate"], b_gate, cos, sin_signed, "mix_in")
    outs, lses = [], []
    for gidx, (window, dil) in enumerate(DIL):
        o_g, l_g = band_fwd(qkv, no_sink, r=dil, base=A_BASE + 6 * gidx, hkv=2, grp=1, max_dist=window // dil,
                            out_dtype=F32, name=f"attn_a{gidx}_fwd")
        outs.append(o_g)
        lses.append(l_g)
    o_a, l_a = merge_groups(outs, lses, "attn_a_merge")
    o_b, l_b = band_fwd(qkv, sinks, r=1, base=B_BASE, hkv=2, grp=2, max_dist=HEAD - 1, out_dtype=BF16,
                        name="attn_b_fwd")
    mem_n, mkv = mem_kv(mem, gains["mem_norm"], w["w_mem_kv"], "mem_kv")
    o_m, l_m = mem_fwd(qkv, mkv, "attn_m_fwd")
    merged, mo, h2 = gate_merge_out(gt, o_a, o_b, o_m, w["w_o_a"], w["w_o_b"], w["w_o_m"], w["w_out"], h1,
                                    gains["mix_norm_post"], "gate_merge_out")
    w.update(weights_of("ffn2", mo))
    xn2, gu2, a2 = ffn_in(h2, gains["ffn2_norm_pre"], w["ffn2_w_in"], "ffn2_in")
    f2, dy, sq = mm_norm_res(a2, w["ffn2_w_out"], h2, gains["ffn2_norm_post"], 0.5, "ffn2_out", target=target)

    grads = {}

    def ffn_bwd(tag, dh_out, f, gu, a, xn, h_in, after):
        df, dgu, dh_in, grads[f"{tag}_norm_pre"], grads[f"{tag}_norm_post"] = ffn_tokens_bwd(
            dh_out, f, h_in, gu, gains[f"{tag}_norm_pre"], gains[f"{tag}_norm_post"], w[f"{tag}_w_in"],
            w[f"{tag}_w_out"], 0.5, f"{tag}_tokens_bwd", after)
        sent = send_grads(f"{tag}_in", {f"{tag}_w_in": mm_tn(
            xn, dgu, D_MODEL, FF_T, f"{tag}_w_in_grad", shard_major=True, perm=_ffn_perm, wire=True)})
        sent = send_grads(f"{tag}_out", {f"{tag}_w_out": mm_tn(
            a, df, FF_T, D_MODEL, f"{tag}_w_out_grad", after=sent, wire=True)})
        return dh_in, sent

    dh2, sent = ffn_bwd("ffn2", dy, f2, gu2, a2, xn2, h2, dy)

    mix = {}
    dmo, grads["mix_norm_post"], dgt, dpa, dpb, dpm, do_a, do_b, do_m, grads["b_gate"] = gate_merge_out_bwd(
        dh2, mo, gains["mix_norm_post"], w["w_out"], gt, o_a, o_b, o_m, w["w_o_a"], w["w_o_b"], w["w_o_m"],
        "gate_merge_out_bwd", sent)
    mix["w_out"] = mm_tn(merged, dmo, D_MODEL, D_MODEL, "w_out_grad", wire=True)
    mix["w_o_a"] = mm_tn(o_a, dpa, o_a.shape[1], D_MODEL, "w_o_a_grad")
    mix["w_o_b"] = mm_tn(o_b, dpb, o_b.shape[1], D_MODEL, "w_o_b_grad")
    mix["w_o_m"] = mm_tn(o_m, dpm, o_m.shape[1], D_MODEL, "w_o_m_grad")

    dqkv = lax.empty(qkv.shape, qkv.dtype)
    for gidx, (window, dil) in enumerate(DIL):
        dqkv, = band_bwd(qkv, dqkv, do_a, o_a, l_a, cos, sin_signed, None, r=dil, base=A_BASE + 6 * gidx, hkv=2,
                         grp=1, max_dist=window // dil, name=f"attn_a{gidx}_bwd")
    dqkv, dsink = band_bwd(qkv, dqkv, do_b, o_b, l_b, cos, sin_signed, sinks, r=1, base=B_BASE, hkv=2, grp=2,
                           max_dist=HEAD - 1, name="attn_b_bwd")
    grads["sinks"] = -dsink[:, ::8, 0].reshape(1, 4)
    dqkv, dmk, dmv = mem_bwd(qkv, dqkv, mkv, do_m, o_m, l_m, "attn_m_bwd")
    mix["w_mem_kv"], grads["mem_norm"] = mem_kv_bwd(
        mem, gains["mem_norm"], mem_n, w["w_mem_kv"], jnp.concatenate([dmk, dmv], axis=1), "mem_kv_bwd")

    mix["w_in"] = mm_tn(u, dqkv, D_MODEL, 1280, "w_in_grad")
    mix["w_gate"] = mm_tn(u, dgt, D_MODEL, 1536, "w_gate_grad", shard_major=True, slabs=2, wire=True)
    sent = send_grads("mix", mix)
    dh1, grads["mix_norm_pre"] = mm_nt_norm_bwd(
        [(dqkv, w["w_in"]), (dgt, w["w_gate"])], h1, dh2, gains["mix_norm_pre"], "mix_in_bwd", sent)

    dx, _ = ffn_bwd("ffn1", dh1, f1, gu1, a1, xn1, x, dh1)
    return sq, dx, grads


def _place():
    return lax.axis_index("x"), lax.axis_index("y"), lax.axis_index("c")


def _other_chips(x, y):
    return [(1 - x, y), (x, 1 - y), (1 - x, 1 - y)]


def _hbm(n):
    return [pl.BlockSpec(memory_space=pltpu.HBM)] * n


SEM = pl.BlockSpec(memory_space=pltpu.SEMAPHORE)
SIDE_EFFECT = pltpu.SideEffectType.DATAFLOW_SIDE_EFFECTING


def _chip_copy(src, land, sems, i, j, dst_slot, scatter):
    x, y, c = _place()
    px, py = _other_chips(x, y)[j]
    send_sems, recv_sems = sems
    return pltpu.make_async_remote_copy(
        src_ref=src[i].at[2 * px + py] if scatter else src[i], dst_ref=land[i].at[dst_slot],
        send_sem=send_sems.at[3 * i + j], recv_sem=recv_sems.at[3 * i + j],
        device_id=(px, py, c), device_id_type=MESH)


def chip_copies_start(srcs, lands, groups, scatter, name):
    n = len(srcs)

    def body(*refs):
        src, land = refs[:n], refs[n:2 * n]
        sems = refs[2 * n:2 * n + 2 * len(groups)]
        token = refs[-1]
        x, y, _ = _place()
        for g, members in enumerate(groups):
            part = ([src[i] for i in members], [land[i] for i in members])
            for t in range(len(members)):
                for j in range(3):
                    _chip_copy(*part, sems[2 * g:2 * g + 2], t, j, 2 * x + y, scatter).start()
        token[...] = jnp.zeros_like(token)

    sem_shapes = [pltpu.SemaphoreType.DMA((3 * len(m),)) for m in groups for _ in range(2)]
    thru = [pltpu.HBM(a.shape, a.dtype) for a in (*srcs, *lands)]
    res = pl.pallas_call(
        body, name=name,
        out_shape=(*sem_shapes, *thru, jax.ShapeDtypeStruct((8, 128), F32)),
        in_specs=_hbm(2 * n),
        out_specs=(*[SEM] * len(sem_shapes), *_hbm(2 * n), pl.BlockSpec(memory_space=pltpu.VMEM)),
        input_output_aliases={i: len(sem_shapes) + i for i in range(2 * n)},
        compiler_params=pltpu.CompilerParams(has_side_effects=SIDE_EFFECT),
    )(*[pltpu.with_memory_space_constraint(a, pltpu.HBM) for a in (*srcs, *lands)])
    k = len(sem_shapes)
    sems = [tuple(res[2 * g:2 * g + 2]) for g in range(len(groups))]
    return sems, list(res[k:k + n]), list(res[k + n:k + 2 * n]), res[-1]


def chip_copies_wait(srcs, lands, sems, after, scatter, name):
    n = len(srcs)

    def body(*refs):
        src, land = refs[:n], refs[n:2 * n]
        pair = refs[2 * n:2 * n + 2]
        x, y, _ = _place()
        for i in range(n):
            for j, (px, py) in enumerate(_other_chips(x, y)):
                copy = _chip_copy(src, land, pair, i, j, 2 * px + py, scatter)
                copy.wait_send()
                copy.wait_recv()

    res = pl.pallas_call(
        body, name=name,
        out_shape=[pltpu.HBM(a.shape, a.dtype) for a in (*srcs, *lands)],
        in_specs=[*_hbm(2 * n), SEM, SEM, pl.BlockSpec(memory_space=pl.ANY)],
        out_specs=_hbm(2 * n),
        input_output_aliases={i: i for i in range(2 * n)},
        compiler_params=pltpu.CompilerParams(has_side_effects=SIDE_EFFECT),
    )(*srcs, *lands, *sems, after)
    return list(res[n:])


def small_all_gather(small, name):
    flips = [(fx, fy, fc) for fx in (0, 1) for fy in (0, 1) for fc in (0, 1)][1:]

    def body(in_ref, out_ref, send_sems, recv_sems, local_sem):
        x, y, c = _place()
        me = 4 * x + 2 * y + c

        def copy(k, slot):
            fx, fy, fc = flips[k]
            return pltpu.make_async_remote_copy(
                src_ref=in_ref, dst_ref=out_ref.at[slot], send_sem=send_sems.at[k], recv_sem=recv_sems.at[k],
                device_id=(x ^ fx, y ^ fy, c ^ fc), device_id_type=MESH)

        local = pltpu.make_async_copy(in_ref, out_ref.at[me], local_sem)
        local.start()
        for k in range(len(flips)):
            copy(k, me).start()
        for k, (fx, fy, fc) in enumerate(flips):
            copy(k, 4 * (x ^ fx) + 2 * (y ^ fy) + (c ^ fc)).wait()
        local.wait()

    return pl.pallas_call(
        body, name=name, in_specs=_hbm(1), out_specs=_hbm(1)[0],
        out_shape=jax.ShapeDtypeStruct((N_DEV,) + small.shape, small.dtype),
        scratch_shapes=[pltpu.SemaphoreType.DMA((len(flips),)), pltpu.SemaphoreType.DMA((len(flips),)),
                        pltpu.SemaphoreType.DMA],
    )(small)


def _sibling_copy(src, land, sems, i):
    x, y, c = _place()
    return pltpu.make_async_remote_copy(
        src_ref=src[i], dst_ref=land[i], send_sem=sems[0].at[i], recv_sem=sems[1].at[i],
        device_id=(x, y, 1 - c), device_id_type=MESH)


def sibling_copies_start(parts, name):
    n = len(parts)
    lands = [lax.empty(p.shape, p.dtype) for p in parts]

    def body(*refs):
        src, land, sems, token = refs[:n], refs[n:2 * n], refs[2 * n:2 * n + 2], refs[-1]
        for i in range(n):
            _sibling_copy(src, land, sems, i).start()
        token[...] = jnp.zeros_like(token)

    res = pl.pallas_call(
        body, name=name,
        out_shape=(pltpu.SemaphoreType.DMA((n,)), pltpu.SemaphoreType.DMA((n,)),
                   *[pltpu.HBM(a.shape, a.dtype) for a in (*parts, *lands)], jax.ShapeDtypeStruct((8, 128), F32)),
        in_specs=_hbm(2 * n),
        out_specs=(SEM, SEM, *_hbm(2 * n), pl.BlockSpec(memory_space=pltpu.VMEM)),
        input_output_aliases={i: 2 + i for i in range(2 * n)},
        compiler_params=pltpu.CompilerParams(has_side_effects=SIDE_EFFECT),
    )(*[pltpu.with_memory_space_constraint(a, pltpu.HBM) for a in (*parts, *lands)])
    return tuple(res[:2]), list(res[2:2 + n]), list(res[2 + n:2 + 2 * n]), res[-1]


def sibling_copies_wait(parts, lands, sems, after, name):
    n = len(parts)

    def body(*refs):
        src, land, sems = refs[:n], refs[n:2 * n], refs[2 * n:2 * n + 2]
        for i in range(n):
            copy = _sibling_copy(src, land, sems, i)
            copy.wait_send()
            copy.wait_recv()

    res = pl.pallas_call(
        body, name=name,
        out_shape=[pltpu.HBM(a.shape, a.dtype) for a in (*parts, *lands)],
        in_specs=[*_hbm(2 * n), SEM, SEM, UNREAD],
        out_specs=_hbm(2 * n),
        input_output_aliases={i: i for i in range(2 * n)},
        compiler_params=pltpu.CompilerParams(has_side_effects=SIDE_EFFECT),
    )(*parts, *lands, *sems, after)
    return list(res[n:])


def _row_tile(rows):
    for t in (256, 176, 128, 64, 32, 16, 8):
        if rows % t == 0:
            return t
    return rows


def chip_partial_sum(me, own_sm, recv, name):
    _, rows, cols = own_sm.shape
    tr = _row_tile(rows)

    def body(me_ref, own_ref, r0, r1, r2, r3, o_ref):
        acc = jnp.zeros((tr, cols), F32)
        for s, r_ref in enumerate((r0, r1, r2, r3)):
            acc = acc + jnp.where(me_ref[0] == s, own_ref[...], r_ref[...].astype(F32))
        o_ref[...] = acc

    def slot(s):
        return pl.BlockSpec((None, tr, cols), lambda i, me_ref, s=s: (s, i, 0))

    return pl.pallas_call(
        body, name=name,
        grid_spec=pltpu.PrefetchScalarGridSpec(
            num_scalar_prefetch=1, grid=(rows // tr,),
            in_specs=[pl.BlockSpec((None, tr, cols), lambda i, me_ref: (me_ref[0], i, 0))] + [slot(s) for s in range(4)],
            out_specs=pl.BlockSpec((tr, cols), lambda i, me_ref: (i, 0))),
        out_shape=jax.ShapeDtypeStruct((rows, cols), F32),
        compiler_params=_params("parallel"),
    )(me, own_sm, recv, recv, recv, recv)


def _adamw(w, g, m, v):
    m = ADAM_B1 * m + (1.0 - ADAM_B1) * g
    v = ADAM_B2 * v + (1.0 - ADAM_B2) * (g * g)
    m_hat = m / (1.0 - ADAM_B1 ** ADAM_STEP)
    v_hat = v / (1.0 - ADAM_B2 ** ADAM_STEP)
    delta = -ADAM_LR * (m_hat / (jnp.sqrt(v_hat) + ADAM_EPS) + ADAM_WD * w)
    return delta, m, v


def adamw_pair(part, sib, w, m, v, name):
    rows, cols = w.shape
    tr = _row_tile(rows)

    def body(p_ref, s_ref, w_ref, m_ref, v_ref, g_ref, d_ref, nm_ref, nv_ref):
        g = p_ref[...] + s_ref[...]
        g_ref[...] = g
        d_ref[...], nm_ref[...], nv_ref[...] = _adamw(w_ref[...], g, m_ref[...], v_ref[...])

    spec = pl.BlockSpec((tr, cols), lambda i: (i, 0))
    return pl.pallas_call(
        body, name=name, grid=(rows // tr,), in_specs=[spec] * 5, out_specs=[spec] * 4,
        out_shape=[jax.ShapeDtypeStruct((rows, cols), F32)] * 4,
        compiler_params=_params("parallel"),
    )(part, sib, w, m, v)


def adamw_small(g_all, w, m, v, name):
    def body(ga_ref, w_ref, m_ref, v_ref, g_ref, d_ref, nm_ref, nv_ref):
        g = ga_ref[0]
        for k in range(1, N_DEV):
            g = g + ga_ref[k]
        g_ref[...] = g
        d_ref[...], nm_ref[...], nv_ref[...] = _adamw(w_ref[...], g, m_ref[...], v_ref[...])

    return pl.pallas_call(
        body, name=name, out_shape=[jax.ShapeDtypeStruct(w.shape, F32)] * 4,
    )(g_all, w, m, v)


WEIGHTS = ("ffn1_norm_pre", "ffn1_w_in", "ffn1_w_out", "ffn1_norm_post", "mix_norm_pre", "w_in", "sinks",
           "mem_norm", "w_mem_kv", "w_gate", "b_gate", "w_o_a", "w_o_b", "w_o_m", "w_out", "mix_norm_post",
           "ffn2_norm_pre", "ffn2_w_in", "ffn2_w_out", "ffn2_norm_post")
BIG = ("ffn1_w_in", "ffn1_w_out", "w_in", "w_mem_kv", "w_gate", "w_o_a", "w_o_b", "w_o_m", "w_out",
       "ffn2_w_in", "ffn2_w_out")
GATHER_ORDER = ("ffn1_in", "ffn1_out", "mix", "ffn2")
GATHER_GROUPS = {"ffn1_in": ("ffn1_w_in",), "ffn1_out": ("ffn1_w_out",),
                 "mix": ("w_in", "w_gate", "w_mem_kv", "w_o_a", "w_o_b", "w_o_m", "w_out"),
                 "ffn2": ("ffn2_w_in", "ffn2_w_out")}
GROUPS = {"ffn1_in": ("ffn1_w_in",), "ffn1_out": ("ffn1_w_out",),
          "mix": ("w_in", "w_gate", "w_mem_kv", "w_o_a", "w_o_b", "w_o_m", "w_out"),
          "ffn2_in": ("ffn2_w_in",), "ffn2_out": ("ffn2_w_out",)}
COLUMN_SHARDED = ("ffn1_w_in", "ffn2_w_in", "w_in", "w_gate", "w_o_a", "w_o_b", "w_o_m")
KEPT_SHARD_MAJOR = ("ffn1_w_in", "ffn2_w_in", "w_gate")
GAINS = ("ffn1_norm_pre", "ffn1_norm_post", "mix_norm_pre", "mem_norm", "mix_norm_post", "ffn2_norm_pre",
         "ffn2_norm_post")
SMALL_ROWS = 16


def _pack_small(t):
    sinks = jnp.pad(t["sinks"], ((0, 0), (0, D_MODEL - t["sinks"].shape[1])))
    rows = [t[k] for k in GAINS] + [t["b_gate"].reshape(3, D_MODEL), sinks]
    packed = jnp.concatenate(rows, axis=0)
    return jnp.pad(packed, ((0, SMALL_ROWS - packed.shape[0]), (0, 0)))


def _unpack_small(p):
    out = {k: p[i:i + 1] for i, k in enumerate(GAINS)}
    out["b_gate"] = p[7:10].reshape(1, 3 * D_MODEL)
    out["sinks"] = p[10:11, :4]
    return out


def kernel(x, mem, ffn1_norm_pre, ffn1_w_in, ffn1_w_out, ffn1_norm_post, mix_norm_pre, w_in, sinks, mem_norm, w_mem_kv, w_gate, b_gate, w_o_a, w_o_b, w_o_m, w_out, mix_norm_post, ffn2_norm_pre, ffn2_w_in, ffn2_w_out, ffn2_norm_post, loss_target, m_ffn1_norm_pre, m_ffn1_w_in, m_ffn1_w_out, m_ffn1_norm_post, m_mix_norm_pre, m_w_in, m_sinks, m_mem_norm, m_w_mem_kv, m_w_gate, m_b_gate, m_w_o_a, m_w_o_b, m_w_o_m, m_w_out, m_mix_norm_post, m_ffn2_norm_pre, m_ffn2_w_in, m_ffn2_w_out, m_ffn2_norm_post, v_ffn1_norm_pre, v_ffn1_w_in, v_ffn1_w_out, v_ffn1_norm_post, v_mix_norm_pre, v_w_in, v_sinks, v_mem_norm, v_w_mem_kv, v_w_gate, v_b_gate, v_w_o_a, v_w_o_b, v_w_o_m, v_w_out, v_mix_norm_post, v_ffn2_norm_pre, v_ffn2_w_in, v_ffn2_w_out, v_ffn2_norm_post):
    given = dict(locals())
    wt = {k: given[k] for k in WEIGHTS}
    mom = {k: given["m_" + k] for k in WEIGHTS}
    var = {k: given["v_" + k] for k in WEIGHTS}
    chip = (2 * lax.axis_index("x") + lax.axis_index("y")).astype(jnp.int32)
    me = chip.reshape(1)

    def landing_zone(own):
        return lax.dynamic_update_slice_in_dim(lax.empty((N_CHIPS,) + own.shape, own.dtype), own[None], chip, 0)

    started = {}
    token = None
    for groups, tag in ((GATHER_ORDER[:1], "first"), (GATHER_ORDER[1:], "rest")):
        keys = [k for g in groups for k in GATHER_GROUPS[g]]
        shards = [(wt[k][0] if token is None else wt[k][0] + token[0, 0]).astype(BF16) for k in keys]
        members = [[keys.index(k) for k in GATHER_GROUPS[g]] for g in groups]
        sems, shards, lands, token = chip_copies_start(
            shards, [landing_zone(s) for s in shards], members, False, f"weight_gather_start_{tag}")
        for g, idx, pair in zip(groups, members, sems):
            started[g] = ([shards[i] for i in idx], [lands[i] for i in idx], pair)

    def weights_of(group, after):
        got = chip_copies_wait(*started[group], after, False, f"weight_gather_wait_{group}")
        full = {}
        for k, g in zip(GATHER_GROUPS[group], got):
            if k in COLUMN_SHARDED:
                if k in ("ffn1_w_in", "ffn2_w_in"):
                    g = jnp.stack([g[0], g[2], g[1], g[3]])
                full[k] = jnp.swapaxes(g, 0, 1).reshape(g.shape[1], N_CHIPS * g.shape[2])
                if k == "w_in":
                    full[k] = to_kernel_heads(full[k])
            else:
                full[k] = g.reshape(N_CHIPS * g.shape[1], g.shape[2])
        return full

    in_flight = {}

    def send_grads(group, grads):
        def shard_major(k, g):
            if k in KEPT_SHARD_MAJOR:
                return g
            if k in COLUMN_SHARDED:
                return jnp.swapaxes(g.reshape(g.shape[0], N_CHIPS, g.shape[1] // N_CHIPS), 0, 1)
            return g.reshape(N_CHIPS, g.shape[0] // N_CHIPS, g.shape[1])

        own, wire = [], []
        for k in GROUPS[group]:
            g, rounded = grads[k] if isinstance(grads[k], (tuple, list)) else (grads[k], None)
            g = shard_major(k, from_kernel_heads(g) if k == "w_in" else g)
            own.append(g)
            wire.append(g.astype(BF16) if rounded is None else shard_major(k, rounded))
        zones = [landing_zone(lax.dynamic_index_in_dim(b, chip, 0, keepdims=False)) for b in wire]
        pair, wire, zones, sent = chip_copies_start(
            wire, zones, [list(range(len(wire)))], True, f"grad_scatter_start_{group}")
        in_flight[group] = (own, wire, zones, pair[0], sent)
        return sent

    gains = {k: wt[k] for k in GAINS}
    sq, dx, grads = layer_step(
        x[0], mem[0], loss_target[0], gains, sinks[0], b_gate, weights_of, send_grads, token[0, 0])
    loss = lax.psum(0.5 * sq[0, 0] / D_MODEL, ("x", "y", "c"))

    res = {}
    after = in_flight["ffn1_out"][4]
    swaps = []
    for stage in (("ffn2_in", "ffn2_out", "mix", "ffn1_in"), ("ffn1_out",)):
        names, parts = [], []
        for group in stage:
            own, wire, zones, pair, _ = in_flight[group]
            received = chip_copies_wait(wire, zones, pair, after, True, f"grad_scatter_wait_{group}")
            for k, g, r in zip(GROUPS[group], own, received):
                names.append(k)
                parts.append(chip_partial_sum(me, g, r, f"{k}_chip_sum"))
        pair, parts, lands, after = sibling_copies_start(parts, f"sibling_start_{stage[-1]}")
        swaps.append((stage[-1], names, parts, lands, pair))
    small_all = small_all_gather(_pack_small(grads), "small_grad_gather")
    packed = adamw_small(small_all, _pack_small(wt), _pack_small(mom), _pack_small(var), "small_adamw")
    after = packed[0]
    for tag, names, parts, lands, pair in swaps:
        sibs = sibling_copies_wait(parts, lands, pair, after, f"sibling_wait_{tag}")
        for k, p, s in zip(names, parts, sibs):
            res[k] = [t[None] for t in adamw_pair(p, s, wt[k][0], mom[k][0], var[k][0], f"{k}_adamw")]
        after = res[names[-1]][0]
    for idx, p in enumerate(packed):
        for k, t in _unpack_small(p).items():
            res.setdefault(k, [None] * 4)[idx] = t

    return (loss, dx[None], *[res[k][0] for k in WEIGHTS], *[res[k][1] for k in WEIGHTS],
            *[res[k][2] for k in WEIGHTS], *[res[k][3] for k in WEIGHTS])
```

```python
import functools

import jax
import jax.numpy as jnp
from jax import lax
from jax.experimental import pallas as pl
from jax.experimental.pallas import tpu as pltpu

F32 = jnp.float32
BF16 = jnp.bfloat16

D_MODEL = 1024
D_FF = 2816
HEAD = 128
N_CHIPS = 4
N_DEV = 8
EPS = 1e-6
NEG_INF = -1e30
ROPE_THETA = 10000.0
ATT_SCALE = HEAD ** -0.5

ADAM_LR = 0.001
ADAM_B1 = 0.9
ADAM_B2 = 0.999
ADAM_EPS = 1e-08
ADAM_WD = 0.01
ADAM_STEP = 10

VMEM_LIMIT = 52 * 2 ** 20
VMEM_LIMIT_LARGE = 60 * 2 ** 20
MESH = pl.DeviceIdType.MESH

QKV_W = 3840
DIL = ((128, 1), (512, 4), (2048, 16))
B_BASE, MQ, A_BASE = 0, 8, 12
_AQ, _AK, _AV, _BQ, _BK, _BV, _MQ = 0, 6, 12, 18, 22, 24, 26
HEAD_ORDER = tuple(
    [h for j in range(2) for h in (_BQ + 2 * j, _BQ + 2 * j + 1, _BK + j, _BV + j)]
    + [_MQ + i for i in range(4)]
    + [h for g in range(3) for i in range(2) for h in (_AQ + 2 * g + i, _AK + 2 * g + i, _AV + 2 * g + i)])
ROTARY_HEADS = tuple(p for p, h in enumerate(HEAD_ORDER) if h < _AV or _BQ <= h < _BV)


def to_kernel_heads(w):
    return jnp.concatenate([w[..., h * HEAD:(h + 1) * HEAD] for h in HEAD_ORDER], axis=-1)


def from_kernel_heads(w):
    place = {h: p for p, h in enumerate(HEAD_ORDER)}
    return jnp.concatenate([w[..., place[h] * HEAD:(place[h] + 1) * HEAD] for h in range(len(HEAD_ORDER))], axis=-1)

TM = 512
FF_T = D_FF // 2


def _params(*sem):
    return pltpu.CompilerParams(dimension_semantics=sem, vmem_limit_bytes=VMEM_LIMIT)


def _dot(a, b):
    return jnp.dot(a, b, preferred_element_type=F32)


def _dot_nt(a, b):
    return lax.dot_general(a, b, (((1,), (1,)), ((), ())), preferred_element_type=F32)


def _dot_tn(a, b):
    return lax.dot_general(a, b, (((0,), (0,)), ((), ())), preferred_element_type=F32)


def _rstd(x):
    return lax.rsqrt(jnp.mean(x * x, axis=-1, keepdims=True) + EPS)


def _sigmoid(x):
    return 0.5 * jnp.tanh(0.5 * x) + 0.5


def _ffn_perm(k):
    return (k % 2) * 2 + k // 2


UNREAD = pl.BlockSpec(memory_space=pl.ANY)


def _resident(arr):
    return pl.BlockSpec(arr.shape, lambda *_: (0,) * arr.ndim, pipeline_mode=pl.Buffered(1))


def rms_scale(x, g, name, after):
    T, D = x.shape
    tm = 1024

    def body(x_ref, g_ref, _, o_ref):
        v = x_ref[...]
        o_ref[...] = (v * _rstd(v) * g_ref[...]).astype(BF16)

    spec = pl.BlockSpec((tm, D), lambda i: (i, 0))
    return pl.pallas_call(
        body, name=name, grid=(T // tm,), in_specs=[spec, _resident(g), UNREAD], out_specs=spec,
        out_shape=jax.ShapeDtypeStruct((T, D), BF16), compiler_params=_params("parallel"),
    )(x, g, after)


def ffn_in(h, g, w, name, xn=None):
    T, D = h.shape
    normed = xn is not None

    def body(h_ref, g_ref, w_ref, *outs):
        if normed:
            xn, (gu_ref, a_ref) = h_ref[...], outs
        else:
            xn_ref, gu_ref, a_ref = outs
            x = h_ref[...]
            xn = (x * _rstd(x) * g_ref[...]).astype(BF16)
            xn_ref[...] = xn
        for j in range(2):
            gu = _dot(xn, w_ref[:, j * 2 * FF_T:(j + 1) * 2 * FF_T])
            gu_ref[:, j * 2 * FF_T:(j + 1) * 2 * FF_T] = gu.astype(BF16)
            gate, up = gu[:, :FF_T], gu[:, FF_T:]
            a_ref[:, j * FF_T:(j + 1) * FF_T] = (gate * _sigmoid(gate) * up).astype(BF16)

    def rows(width):
        return pl.BlockSpec((TM, width), lambda i: (i, 0))

    res = pl.pallas_call(
        body, name=name,
        grid=(T // TM,),
        in_specs=[rows(D), _resident(g), _resident(w)],
        out_specs=[rows(D)] * (not normed) + [rows(2 * D_FF), rows(D_FF)],
        out_shape=[jax.ShapeDtypeStruct((T, D), BF16)] * (not normed)
                  + [jax.ShapeDtypeStruct((T, 2 * D_FF), BF16), jax.ShapeDtypeStruct((T, D_FF), BF16)],
        compiler_params=_params("parallel"),
    )(xn if normed else h, g, w)
    return (xn, *res) if normed else tuple(res)


def mm_norm_res(a, w, h_in, g, coef, name, target=None):
    T, K = a.shape
    D = w.shape[1]
    final = target is not None

    def body(*refs):
        if final:
            a_ref, w_ref, h_ref, g_ref, t_ref, f_ref, o_ref, l_ref = refs
        else:
            a_ref, w_ref, h_ref, g_ref, f_ref, o_ref = refs
        f = _dot(a_ref[...], w_ref[...])
        f_ref[...] = f
        y = h_ref[...] + coef * (f * _rstd(f) * g_ref[...])
        if final:
            err = y - t_ref[...]
            o_ref[...] = err * (1.0 / D)

            @pl.when(pl.program_id(0) == 0)
            def _():
                l_ref[...] = jnp.zeros_like(l_ref)

            l_ref[...] += jnp.sum(err * err)
        else:
            o_ref[...] = y

    row = pl.BlockSpec((TM, D), lambda i: (i, 0))
    in_specs = [pl.BlockSpec((TM, K), lambda i: (i, 0)),
                _resident(w),
                row, pl.BlockSpec((1, D), lambda i: (0, 0))]
    out_specs = [row, row]
    out_shape = [jax.ShapeDtypeStruct((T, D), F32), jax.ShapeDtypeStruct((T, D), F32)]
    args = [a, w, h_in, g]
    if final:
        in_specs.append(row)
        args.append(target)
        out_specs.append(pl.BlockSpec((8, 128), lambda i: (0, 0)))
        out_shape.append(jax.ShapeDtypeStruct((8, 128), F32))
    return pl.pallas_call(
        body, name=name, grid=(T // TM,), in_specs=in_specs, out_specs=out_specs, out_shape=out_shape,
        compiler_params=_params("arbitrary"),
    )(*args)


def _rope(x, cos, sin_signed):
    return x * cos + pltpu.roll(x, HEAD // 2, axis=1) * sin_signed


def _unrope(x, cos, sin_signed):
    return x * cos - pltpu.roll(x, HEAD // 2, axis=1) * sin_signed


def mix_in(h, g, w, w_gate, b_gate, cos, sin_signed, name):
    T, D = h.shape
    tn = 768

    def body(h_ref, g_ref, w_ref, wg_ref, b_ref, c_ref, s_ref, u_ref, o_ref, gt_ref):
        x = h_ref[...]
        u = (x * _rstd(x) * g_ref[...]).astype(BF16)
        u_ref[...] = u
        c, s = c_ref[...], s_ref[...]
        for j in range(QKV_W // tn):
            acc = _dot(u, w_ref[:, j * tn:(j + 1) * tn])
            for hd in range(tn // HEAD):
                head = j * (tn // HEAD) + hd
                part = acc[:, hd * HEAD:(hd + 1) * HEAD]
                if head in ROTARY_HEADS:
                    part = _rope(part, c, s)
                o_ref[:, head * HEAD:(head + 1) * HEAD] = part.astype(BF16)
        for j in range(w_gate.shape[1] // tn):
            cols = slice(j * tn, (j + 1) * tn)
            gt_ref[:, cols] = _sigmoid(_dot(u, wg_ref[:, cols]) + b_ref[:, cols]).astype(BF16)

    def rows(width):
        return pl.BlockSpec((TM, width), lambda i: (i, 0))

    return pl.pallas_call(
        body, name=name,
        grid=(T // TM,),
        in_specs=[rows(D), _resident(g), _resident(w), _resident(w_gate), _resident(b_gate), rows(HEAD), rows(HEAD)],
        out_specs=[rows(D), rows(QKV_W), rows(w_gate.shape[1])],
        out_shape=[jax.ShapeDtypeStruct((T, D), BF16), jax.ShapeDtypeStruct((T, QKV_W), BF16),
                   jax.ShapeDtypeStruct((T, w_gate.shape[1]), BF16)],
        compiler_params=_params("parallel"),
    )(h, g, w, w_gate, b_gate, cos, sin_signed)


def gate_merge_out(gt, o_a, o_b, o_m, w_a, w_b, w_m, w_out, h_in, g, name):
    T = gt.shape[0]
    D = D_MODEL

    def body(gt_ref, oa_ref, ob_ref, om_ref, wa_ref, wb_ref, wm_ref, wo_ref, h_ref, g_ref, m_ref, f_ref, o_ref):
        acc = gt_ref[:, :D].astype(F32) * _dot(oa_ref[...], wa_ref[...])
        acc += gt_ref[:, D:2 * D].astype(F32) * _dot(ob_ref[...], wb_ref[...])
        acc += gt_ref[:, 2 * D:].astype(F32) * _dot(om_ref[...], wm_ref[...])
        merged = acc.astype(BF16)
        m_ref[...] = merged
        f = _dot(merged, wo_ref[...])
        f_ref[...] = f
        o_ref[...] = h_ref[...] + f * _rstd(f) * g_ref[...]

    def rows(width):
        return pl.BlockSpec((TM, width), lambda i: (i, 0))

    return pl.pallas_call(
        body, name=name, grid=(T // TM,),
        in_specs=[rows(3 * D), rows(o_a.shape[1]), rows(o_b.shape[1]), rows(o_m.shape[1]),
                  _resident(w_a), _resident(w_b), _resident(w_m), _resident(w_out), rows(D), _resident(g)],
        out_specs=[rows(D), rows(D), rows(D)],
        out_shape=[jax.ShapeDtypeStruct((T, D), BF16), jax.ShapeDtypeStruct((T, D), F32),
                   jax.ShapeDtypeStruct((T, D), F32)],
        compiler_params=_params("parallel"),
    )(gt, o_a, o_b, o_m, w_a, w_b, w_m, w_out, h_in, g)


def _band_rows(start, r):
    return pl.ds(start, HEAD) if r == 1 else pl.ds(start, HEAD, stride=r)


def _band_mask(max_dist, first_has_prev):
    row = lax.broadcasted_iota(jnp.int32, (HEAD, 2 * HEAD), 0)
    col = lax.broadcasted_iota(jnp.int32, (HEAD, 2 * HEAD), 1)
    dist = row + HEAD - col
    band = (dist >= 0) & (dist <= max_dist)
    return band, band & (col >= jnp.where(first_has_prev, 0, HEAD))


def _stack(parts):
    return parts[0] if len(parts) == 1 else jnp.concatenate(parts, axis=0)


def _band_specs(BT, SB, nsub, base, grp):
    stride = grp + 2

    def cur(off, width):
        return pl.BlockSpec((BT, width * HEAD), lambda h, i: (i, (base + h * stride + off) // width))

    def prev(off):
        return pl.BlockSpec((SB, HEAD), lambda h, i: (jnp.maximum(i * nsub - 1, 0), base + h * stride + off))

    return cur(0, grp), cur(grp, 1), prev(grp), cur(grp + 1, 1), prev(grp + 1)


def band_fwd(qkv, sinks, *, r, base, hkv, grp, max_dist, out_dtype, name):
    T, W = qkv.shape
    SB = HEAD * r
    BT = min(2048, T)
    nsub, nib = BT // SB, T // BT
    hq = hkv * grp
    heads = [slice(g * HEAD, (g + 1) * HEAD) for g in range(grp)]

    def body(sink_ref, q_ref, kc_ref, kp_ref, vc_ref, vp_ref, o_ref, l_ref, qf, kf, vf):
        kvh, ib = pl.program_id(0), pl.program_id(1)
        qf[...] = q_ref[...].astype(F32)
        kf[:SB] = kp_ref[...].astype(F32)
        kf[SB:] = kc_ref[...].astype(F32)
        vf[:SB] = vp_ref[...].astype(F32)
        vf[SB:] = vc_ref[...].astype(F32)
        band, band_first = _band_mask(max_dist, ib > 0)
        for c in range(r):
            k_old, v_old = kf[_band_rows(c, r)], vf[_band_rows(c, r)]
            for j in range(nsub):
                mask = band_first if j == 0 else band
                rows = _band_rows(j * SB + c, r)
                k_own, v_own = kf[_band_rows((j + 1) * SB + c, r)], vf[_band_rows((j + 1) * SB + c, r)]
                kcat = jnp.concatenate([k_old, k_own], axis=0).astype(BF16)
                vcat = jnp.concatenate([v_old, v_own], axis=0).astype(BF16)
                k_old, v_old = k_own, v_own
                s_all = _dot_nt(_stack([qf[rows, cols] for cols in heads]).astype(BF16), kcat) * ATT_SCALE
                probs, tots = [], []
                for g, cols in enumerate(heads):
                    s = jnp.where(mask, s_all[cols], NEG_INF)
                    sk = sink_ref[kvh * grp + g]
                    m = jnp.maximum(jnp.max(s, axis=-1, keepdims=True), sk)
                    p = jnp.exp(s - m)
                    tot = jnp.sum(p, axis=-1, keepdims=True) + jnp.exp(sk - m)
                    probs.append(p.astype(BF16))
                    tots.append(tot)
                    l_ref[rows, cols] = jnp.broadcast_to(m + jnp.log(tot), (HEAD, HEAD))
                o_all = _dot(_stack(probs), vcat)
                for g, cols in enumerate(heads):
                    o_ref[rows, cols] = (o_all[cols] / tots[g]).astype(out_dtype)

    out_spec = pl.BlockSpec((BT, grp * HEAD), lambda h, i: (i, h))
    return pl.pallas_call(
        body, name=name, grid=(hkv, nib),
        in_specs=[pl.BlockSpec(memory_space=pltpu.SMEM), *_band_specs(BT, SB, nsub, base, grp)],
        out_specs=[out_spec, out_spec],
        out_shape=[jax.ShapeDtypeStruct((T, hq * HEAD), out_dtype), jax.ShapeDtypeStruct((T, hq * HEAD), F32)],
        scratch_shapes=[pltpu.VMEM((BT, grp * HEAD), F32), pltpu.VMEM((SB + BT, HEAD), F32),
                        pltpu.VMEM((SB + BT, HEAD), F32)],
        compiler_params=_params("parallel", "arbitrary"),
    )(sinks, qkv, qkv, qkv, qkv, qkv)


def band_bwd(qkv, dqkv, do, o, lse, cos, sin_signed, sinks, *, r, base, hkv, grp, max_dist, name):
    T, W = qkv.shape
    SB = HEAD * r
    BT = min(max(2048, 2 * SB), T)
    nsub, nib = BT // SB, T // BT
    nblk = T // SB
    with_sink = sinks is not None
    heads = [slice(g * HEAD, (g + 1) * HEAD) for g in range(grp)]

    def body(*refs):
        if with_sink:
            sink_ref, refs = refs[0], refs[1:]
        (q_ref, kc_ref, kp_ref, vc_ref, vp_ref, qn_ref, do_ref, don_ref, o_ref, on_ref, l_ref, ln_ref,
         c_ref, s_ref, _) = refs[:15]
        out_ref = refs[15]
        ds_ref = refs[16] if with_sink else None
        qf, dof, of, kf, vf, dqf, dkf, dvf = refs[-8:]
        kvh, ib = pl.program_id(0), pl.program_id(1)
        for buf, cur_ref, nxt_ref in ((qf, q_ref, qn_ref), (dof, do_ref, don_ref), (of, o_ref, on_ref)):
            buf[:BT] = cur_ref[...].astype(F32)
            buf[BT:] = nxt_ref[...].astype(F32)
        kf[:SB] = kp_ref[...].astype(F32)
        kf[SB:] = kc_ref[...].astype(F32)
        vf[:SB] = vp_ref[...].astype(F32)
        vf[SB:] = vc_ref[...].astype(F32)
        band, band_first = _band_mask(max_dist, ib > 0)
        if with_sink:
            @pl.when(ib == 0)
            def _():
                ds_ref[...] = jnp.zeros_like(ds_ref)

        def grads(rows, logzs, keys, vals, mask):
            q = _stack([qf[rows, cols] for cols in heads]).astype(BF16)
            dout = _stack([dof[rows, cols] for cols in heads]).astype(BF16)
            s_all = _dot_nt(q, keys) * ATT_SCALE
            dp_all = _dot_nt(dout, vals)
            probs, dss, deltas = [], [], []
            for g, cols in enumerate(heads):
                delta = jnp.sum(dof[rows, cols] * of[rows, cols], axis=-1, keepdims=True)
                p = jnp.exp(jnp.where(mask, s_all[cols], NEG_INF) - logzs[g][:, :1])
                probs.append(p.astype(BF16))
                dss.append((p * (dp_all[cols] - delta) * ATT_SCALE).astype(BF16))
                deltas.append(delta)
            return q, dout, _stack(probs), _stack(dss), deltas

        row = lax.broadcasted_iota(jnp.int32, (HEAD, HEAD), 0)
        col = lax.broadcasted_iota(jnp.int32, (HEAD, HEAD), 1)
        reach = col >= row + jnp.where(ib < nib - 1, HEAD - max_dist, 2 * HEAD)
        for c in range(r):
            k_old, v_old = kf[_band_rows(c, r)], vf[_band_rows(c, r)]
            dk_own = dv_own = None
            for j in range(nsub):
                rows = _band_rows(j * SB + c, r)
                k_own, v_own = kf[_band_rows((j + 1) * SB + c, r)], vf[_band_rows((j + 1) * SB + c, r)]
                kcat = jnp.concatenate([k_old, k_own], axis=0).astype(BF16)
                vcat = jnp.concatenate([v_old, v_own], axis=0).astype(BF16)
                logzs = [l_ref[rows, cols] for cols in heads]
                q, dout, p, ds, deltas = grads(rows, logzs, kcat, vcat, band_first if j == 0 else band)
                dq = _dot(ds, kcat)
                for g, cols in enumerate(heads):
                    dqf[rows, cols] = dq[cols]
                    if with_sink:
                        p_sink = jnp.exp(sink_ref[kvh * grp + g] - logzs[g][:, :1])
                        ds_ref[g * 8:(g + 1) * 8] += jnp.sum(p_sink * deltas[g])
                dk, dv = _dot_tn(ds, q), _dot_tn(p, dout)
                if j > 0:
                    done = _band_rows((j - 1) * SB + c, r)
                    dkf[done] = dk_own + dk[:HEAD]
                    dvf[done] = dv_own + dv[:HEAD]
                dk_own, dv_own = dk[HEAD:], dv[HEAD:]
                k_old, v_old = k_own, v_own
            logzs = [ln_ref[_band_rows(c, r), cols] for cols in heads]
            q, dout, p, ds, _ = grads(_band_rows(BT + c, r), logzs, k_old.astype(BF16), v_old.astype(BF16), reach)
            done = _band_rows((nsub - 1) * SB + c, r)
            dkf[done] = dk_own + _dot_tn(ds, q)
            dvf[done] = dv_own + _dot_tn(p, dout)

        cs, sn = c_ref[...], s_ref[...]
        for cols in heads:
            out_ref[:, cols] = _unrope(dqf[:, cols], cs, sn).astype(BF16)
        out_ref[:, grp * HEAD:(grp + 1) * HEAD] = _unrope(dkf[...], cs, sn).astype(BF16)
        out_ref[:, (grp + 1) * HEAD:] = dvf[...].astype(BF16)

    def nxt_row(i):
        return jnp.minimum((i + 1) * nsub, nblk - 1)

    stride = grp + 2
    q_next = pl.BlockSpec((SB, grp * HEAD), lambda h, i: (nxt_row(i), (base + h * stride) // grp))
    head_cur = pl.BlockSpec((BT, grp * HEAD), lambda h, i: (i, h))
    head_next = pl.BlockSpec((SB, grp * HEAD), lambda h, i: (nxt_row(i), h))
    table = pl.BlockSpec((BT, HEAD), lambda h, i: (i, 0))

    in_specs = [*_band_specs(BT, SB, nsub, base, grp), q_next,
                head_cur, head_next, head_cur, head_next, head_cur, head_next, table, table, UNREAD]
    args = [qkv, qkv, qkv, qkv, qkv, qkv, do, do, o, o, lse, lse, cos, sin_signed, dqkv]
    out_specs = [pl.BlockSpec((BT, stride * HEAD), lambda h, i: (i, base // stride + h))]
    out_shape = [jax.ShapeDtypeStruct(dqkv.shape, dqkv.dtype)]
    if with_sink:
        in_specs.insert(0, pl.BlockSpec(memory_space=pltpu.SMEM))
        args.insert(0, sinks)
        out_specs.append(pl.BlockSpec((None, grp * 8, HEAD), lambda h, i: (h, 0, 0)))
        out_shape.append(jax.ShapeDtypeStruct((hkv, grp * 8, HEAD), F32))
    wide = pltpu.VMEM((BT + SB, grp * HEAD), F32)
    tall = pltpu.VMEM((SB + BT, HEAD), F32)
    grad = pltpu.VMEM((BT, HEAD), F32)
    return pl.pallas_call(
        body, name=name, grid=(hkv, nib), in_specs=in_specs, out_specs=out_specs, out_shape=out_shape,
        input_output_aliases={len(args) - 1: 0},
        scratch_shapes=[wide, wide, wide, tall, tall, pltpu.VMEM((BT, grp * HEAD), F32), grad, grad],
        compiler_params=pltpu.CompilerParams(dimension_semantics=("parallel", "arbitrary"),
                                             vmem_limit_bytes=VMEM_LIMIT_LARGE),
    )(*args)


def merge_groups(outs, lses, name):
    T, Wd = outs[0].shape
    tm = 1024

    def body(o0, o1, o2, l0, l1, l2, out_ref, lt_ref):
        a, b, c = l0[...], l1[...], l2[...]
        m = jnp.maximum(jnp.maximum(a, b), c)
        wa, wb, wc = jnp.exp(a - m), jnp.exp(b - m), jnp.exp(c - m)
        z = wa + wb + wc
        out_ref[...] = ((wa * o0[...] + wb * o1[...] + wc * o2[...]) / z).astype(BF16)
        lt_ref[...] = m + jnp.log(z)

    spec = pl.BlockSpec((tm, Wd), lambda i: (i, 0))
    return pl.pallas_call(
        body, name=name, grid=(T // tm,), in_specs=[spec] * 6, out_specs=[spec, spec],
        out_shape=[jax.ShapeDtypeStruct((T, Wd), BF16), jax.ShapeDtypeStruct((T, Wd), F32)],
        compiler_params=_params("parallel"),
    )(*outs, *lses)


M_HEADS = 4


def mem_kv(mem, g, w, name):
    n, D = mem.shape

    def body(m_ref, g_ref, w_ref, mn_ref, kv_ref):
        x = m_ref[...]
        mn = (x * _rstd(x) * g_ref[...]).astype(BF16)
        mn_ref[...] = mn
        kv_ref[...] = _dot(mn, w_ref[...]).astype(BF16)

    return pl.pallas_call(
        body, name=name,
        out_shape=[jax.ShapeDtypeStruct((n, D), BF16), jax.ShapeDtypeStruct((n, w.shape[1]), BF16)],
        compiler_params=pltpu.CompilerParams(vmem_limit_bytes=VMEM_LIMIT),
    )(mem, g, w)


def mem_fwd(qkv, mkv, name):
    T = qkv.shape[0]
    n = mkv.shape[0]
    RB = 1024

    def body(q_ref, kv_ref, o_ref, l_ref):
        for h in range(M_HEADS):
            cols = slice(h * HEAD, (h + 1) * HEAD)
            s = _dot_nt(q_ref[:, cols], kv_ref[:, cols]) * ATT_SCALE
            m = jnp.max(s, axis=-1, keepdims=True)
            p = jnp.exp(s - m)
            den = jnp.sum(p, axis=-1, keepdims=True)
            vals = kv_ref[:, (M_HEADS + h) * HEAD:(M_HEADS + h + 1) * HEAD]
            o_ref[:, cols] = (_dot(p.astype(BF16), vals) / den).astype(BF16)
            l_ref[:, cols] = jnp.broadcast_to(m + jnp.log(den), (RB, HEAD))

    out = pl.BlockSpec((RB, M_HEADS * HEAD), lambda i: (i, 0))
    return pl.pallas_call(
        body, name=name, grid=(T // RB,),
        in_specs=[pl.BlockSpec((RB, M_HEADS * HEAD), lambda i: (i, MQ // M_HEADS)), _resident(mkv)],
        out_specs=[out, out],
        out_shape=[jax.ShapeDtypeStruct((T, M_HEADS * HEAD), BF16), jax.ShapeDtypeStruct((T, M_HEADS * HEAD), F32)],
        compiler_params=_params("parallel"),
    )(qkv, mkv)


def mem_bwd(qkv, dqkv, mkv, do, o, lse, name):
    T = qkv.shape[0]
    n = mkv.shape[0]
    RB = 1024

    def body(q_ref, kv_ref, do_ref, o_ref, l_ref, _, dq_ref, dk_ref, dv_ref):
        @pl.when(pl.program_id(0) == 0)
        def _():
            dk_ref[...] = jnp.zeros_like(dk_ref)
            dv_ref[...] = jnp.zeros_like(dv_ref)

        for h in range(M_HEADS):
            cols = slice(h * HEAD, (h + 1) * HEAD)
            keys, vals = kv_ref[:, cols], kv_ref[:, (M_HEADS + h) * HEAD:(M_HEADS + h + 1) * HEAD]
            q, dout = q_ref[:, cols], do_ref[:, cols]
            delta = jnp.sum(dout.astype(F32) * o_ref[:, cols].astype(F32), axis=-1, keepdims=True)
            p = jnp.exp(_dot_nt(q, keys) * ATT_SCALE - l_ref[:, cols][:, :1])
            ds = (p * (_dot_nt(dout, vals) - delta) * ATT_SCALE).astype(BF16)
            dq_ref[:, cols] = _dot(ds, keys).astype(BF16)
            dk_ref[:, cols] += _dot_tn(ds, q)
            dv_ref[:, cols] += _dot_tn(p.astype(BF16), dout)

    wide = M_HEADS * HEAD
    tok = pl.BlockSpec((RB, wide), lambda i: (i, 0))
    q_cols = pl.BlockSpec((RB, wide), lambda i: (i, MQ // M_HEADS))
    slot = pl.BlockSpec((n, wide), lambda i: (0, 0))
    return pl.pallas_call(
        body, name=name, grid=(T // RB,),
        in_specs=[q_cols, _resident(mkv), tok, tok, tok, UNREAD],
        out_specs=[q_cols, slot, slot],
        out_shape=[jax.ShapeDtypeStruct(dqkv.shape, dqkv.dtype),
                   jax.ShapeDtypeStruct((n, wide), F32), jax.ShapeDtypeStruct((n, wide), F32)],
        input_output_aliases={5: 0},
        compiler_params=_params("arbitrary"),
    )(qkv, mkv, do, o, lse, dqkv)


def mem_kv_bwd(mem, g, mem_n, w, dmkv, name):
    n, D = mem.shape

    def body(m_ref, g_ref, mn_ref, w_ref, d_ref, dw_ref, dg_ref):
        d = d_ref[...].astype(BF16)
        dw_ref[...] = _dot_tn(mn_ref[...], d)
        x = m_ref[...]
        dg_ref[...] = jnp.sum(_dot_nt(d, w_ref[...]) * (x * _rstd(x)), axis=0, keepdims=True)

    return pl.pallas_call(
        body, name=name,
        out_shape=[jax.ShapeDtypeStruct(w.shape, F32), jax.ShapeDtypeStruct((1, D), F32)],
        compiler_params=pltpu.CompilerParams(vmem_limit_bytes=VMEM_LIMIT),
    )(mem, g, mem_n, w, dmkv)


def _rms_bwd(dn, f, g):
    r = _rstd(f)
    fhat = f * r
    dfhat = dn * g
    df = r * (dfhat - fhat * jnp.mean(dfhat * fhat, axis=-1, keepdims=True))
    return df, jnp.sum(dn * fhat, axis=0, keepdims=True)


def ffn_tokens_bwd(dh, f, h_in, gu, g_pre, g_post, w_in, w_out, coef, name, after):
    T, D = dh.shape

    def body(dh_ref, f_ref, h_ref, gu_ref, gpre_ref, gpost_ref, win_ref, wout_ref, _,
             df_ref, dgu_ref, dhin_ref, dgpre_ref, dgpost_ref, dxn_ref):
        i, j = pl.program_id(0), pl.program_id(1)

        @pl.when(j == 0)
        def _():
            @pl.when(i == 0)
            def _():
                dgpre_ref[...] = jnp.zeros_like(dgpre_ref)
                dgpost_ref[...] = jnp.zeros_like(dgpost_ref)

            df, dg_post = _rms_bwd(coef * dh_ref[...], f_ref[...], gpost_ref[...])
            dgpost_ref[...] += dg_post
            df_ref[...] = df.astype(BF16)

        for jj in range(2):
            @pl.when(j == jj)
            def _(jj=jj):
                lo, mid, hi = 2 * jj * FF_T, (2 * jj + 1) * FF_T, (2 * jj + 2) * FF_T
                da = _dot_nt(df_ref[...], wout_ref[jj * FF_T:(jj + 1) * FF_T, :])
                gate = gu_ref[:, :FF_T].astype(F32)
                up = gu_ref[:, FF_T:].astype(F32)
                sig = _sigmoid(gate)
                dgate = (da * up * sig * (1.0 + gate * (1.0 - sig))).astype(BF16)
                dup = (da * gate * sig).astype(BF16)
                dgu_ref[:, :FF_T] = dgate
                dgu_ref[:, FF_T:] = dup
                part = _dot_nt(dgate, win_ref[:, lo:mid]) + _dot_nt(dup, win_ref[:, mid:hi])
                if jj == 0:
                    dxn_ref[...] = part
                else:
                    h = h_ref[...]
                    r = _rstd(h)
                    xhat = h * r
                    dxn = dxn_ref[...] + part
                    dxhat = dxn * gpre_ref[...]
                    dhin_ref[...] = dh_ref[...] + r * (dxhat - xhat * jnp.mean(dxhat * xhat, axis=-1, keepdims=True))
                    dgpre_ref[...] += jnp.sum(dxn * xhat, axis=0, keepdims=True)

    row = pl.BlockSpec((TM, D), lambda i, j: (i, 0))
    wide = pl.BlockSpec((TM, 2 * FF_T), lambda i, j: (i, j))
    vec = pl.BlockSpec((1, D), lambda i, j: (0, 0))
    return pl.pallas_call(
        body, name=name, grid=(T // TM, 2),
        in_specs=[row, row, row, wide, _resident(g_pre), _resident(g_post), _resident(w_in), _resident(w_out),
                  UNREAD],
        out_specs=[row, wide, row, vec, vec],
        out_shape=[jax.ShapeDtypeStruct((T, D), BF16), jax.ShapeDtypeStruct((T, 2 * D_FF), BF16),
                   jax.ShapeDtypeStruct((T, D), F32), jax.ShapeDtypeStruct((1, D), F32),
                   jax.ShapeDtypeStruct((1, D), F32)],
        scratch_shapes=[pltpu.VMEM((TM, D), F32)],
        compiler_params=pltpu.CompilerParams(dimension_semantics=("arbitrary", "arbitrary"),
                                             vmem_limit_bytes=VMEM_LIMIT_LARGE),
    )(dh, f, h_in, gu, g_pre, g_post, w_in, w_out, after)


def mm_nt_norm_bwd(pieces, h_in, dh_out, g, name, after):
    T, D = h_in.shape

    def body(*refs):
        ab = refs[:2 * len(pieces)]
        h_ref, dh_ref, g_ref, _, o_ref, dg_ref = refs[2 * len(pieces):]
        dxn = _dot_nt(ab[0][...], ab[1][...])
        for p in range(1, len(pieces)):
            dxn += _dot_nt(ab[2 * p][...], ab[2 * p + 1][...])
        h = h_ref[...]
        r = _rstd(h)
        xhat = h * r
        dxhat = dxn * g_ref[...]
        o_ref[...] = dh_ref[...] + r * (dxhat - xhat * jnp.mean(dxhat * xhat, axis=-1, keepdims=True))

        @pl.when(pl.program_id(0) == 0)
        def _():
            dg_ref[...] = jnp.zeros_like(dg_ref)

        dg_ref[...] += jnp.sum(dxn * xhat, axis=0, keepdims=True)

    in_specs, args = [], []
    for a, w in pieces:
        in_specs += [pl.BlockSpec((TM, a.shape[1]), lambda i: (i, 0)), _resident(w)]
        args += [a, w]
    row = pl.BlockSpec((TM, D), lambda i: (i, 0))
    return pl.pallas_call(
        body, name=name, grid=(T // TM,),
        in_specs=in_specs + [row, row, _resident(g), UNREAD],
        out_specs=[row, pl.BlockSpec((1, D), lambda i: (0, 0))],
        out_shape=[jax.ShapeDtypeStruct((T, D), F32), jax.ShapeDtypeStruct((1, D), F32)],
        compiler_params=_params("arbitrary"),
    )(*args, h_in, dh_out, g, after)


def gate_merge_out_bwd(dh, f, g, w_out, gt, o_a, o_b, o_m, w_a, w_b, w_m, name, after):
    T = dh.shape[0]
    D = D_MODEL
    branch = ((o_a, w_a), (o_b, w_b), (o_m, w_m))

    def body(dh_ref, f_ref, g_ref, wo_ref, gt_ref, oa_ref, ob_ref, om_ref, wa_ref, wb_ref, wm_ref, _,
             df_ref, dg_ref, dgt_ref, dpa_ref, dpb_ref, dpm_ref, doa_ref, dob_ref, dom_ref, db_ref):
        @pl.when(pl.program_id(0) == 0)
        def _():
            db_ref[...] = jnp.zeros_like(db_ref)
            dg_ref[...] = jnp.zeros_like(dg_ref)

        df, dg = _rms_bwd(dh_ref[...], f_ref[...], g_ref[...])
        dg_ref[...] += dg
        df = df.astype(BF16)
        df_ref[...] = df
        dmf = _dot_nt(df, wo_ref[...])
        for x, (o_ref, w_ref, dp_ref, do_ref) in enumerate(((oa_ref, wa_ref, dpa_ref, doa_ref),
                                                           (ob_ref, wb_ref, dpb_ref, dob_ref),
                                                           (om_ref, wm_ref, dpm_ref, dom_ref))):
            cols = slice(x * D, (x + 1) * D)
            gx = gt_ref[:, cols].astype(F32)
            w = w_ref[...]
            dpre = dmf * _dot(o_ref[...], w) * gx * (1.0 - gx)
            dgt_ref[:, cols] = dpre.astype(BF16)
            db_ref[:, cols] += jnp.sum(dpre, axis=0, keepdims=True)
            dp = (dmf * gx).astype(BF16)
            dp_ref[...] = dp
            do_ref[...] = _dot_nt(dp, w).astype(BF16)

    def rows(width):
        return pl.BlockSpec((TM, width), lambda i: (i, 0))

    widths = [o.shape[1] for o, _ in branch]
    return pl.pallas_call(
        body, name=name, grid=(T // TM,),
        in_specs=[rows(D), rows(D), _resident(g), _resident(w_out), rows(3 * D)] + [rows(k) for k in widths]
                 + [_resident(w) for _, w in branch] + [UNREAD],
        out_specs=[rows(D), pl.BlockSpec((1, D), lambda i: (0, 0)), rows(3 * D), rows(D), rows(D), rows(D)]
                  + [rows(k) for k in widths] + [pl.BlockSpec((1, 3 * D), lambda i: (0, 0))],
        out_shape=[jax.ShapeDtypeStruct((T, D), BF16), jax.ShapeDtypeStruct((1, D), F32),
                   jax.ShapeDtypeStruct((T, 3 * D), BF16)] + [jax.ShapeDtypeStruct((T, D), BF16)] * 3
                  + [jax.ShapeDtypeStruct((T, k), BF16) for k in widths]
                  + [jax.ShapeDtypeStruct((1, 3 * D), F32)],
        compiler_params=_params("arbitrary"),
    )(dh, f, g, w_out, gt, o_a, o_b, o_m, w_a, w_b, w_m, after)


def mm_tn(x, dy, tm, tn, name, shard_major=False, perm=None, slabs=1, after=None, wire=False):
    T, M = x.shape
    N = dy.shape[1]
    tk = min(2048, T)
    perm = perm or (lambda j: j)
    w = tn // slabs

    def body(x_ref, dy_ref, *rest):
        o_ref = rest[-2] if wire else rest[-1]

        @pl.when(pl.program_id(2) == 0)
        def _():
            o_ref[...] = jnp.zeros_like(o_ref)

        acc = _dot_tn(x_ref[...], dy_ref[...])
        if shard_major:
            for s in range(slabs):
                o_ref[s] += acc[:, s * w:(s + 1) * w]
        else:
            o_ref[...] += acc
        if wire:
            @pl.when(pl.program_id(2) == T // tk - 1)
            def _():
                rest[-1][...] = o_ref[...].astype(BF16)

    if shard_major:
        out_spec = pl.BlockSpec((slabs, tm, w), lambda i, j, k: (perm(j), i, 0))
        out_shape = jax.ShapeDtypeStruct((N // w, M, w), F32)
    else:
        out_spec = pl.BlockSpec((tm, tn), lambda i, j, k: (i, j))
        out_shape = jax.ShapeDtypeStruct((M, N), F32)
    return pl.pallas_call(
        body, name=name, grid=(M // tm, N // tn, T // tk),
        in_specs=[pl.BlockSpec((tk, tm), lambda i, j, k: (k, i)),
                  pl.BlockSpec((tk, tn), lambda i, j, k: (k, j))] + ([] if after is None else [UNREAD]),
        out_specs=[out_spec, out_spec] if wire else out_spec,
        out_shape=[out_shape, jax.ShapeDtypeStruct(out_shape.shape, BF16)] if wire else out_shape,
        compiler_params=_params("parallel", "parallel", "arbitrary"),
    )(x, dy, *([] if after is None else [after]))


def rope_tables(T, zero):
    half = HEAD // 2
    inv = ROPE_THETA ** (-jnp.arange(half, dtype=F32) / half)
    ang = (jnp.arange(T).astype(F32) + zero)[:, None] * inv[None, :]
    cos, sin = jnp.cos(ang), jnp.sin(ang)
    return jnp.concatenate([cos, cos], axis=1), jnp.concatenate([-sin, sin], axis=1)


def layer_step(x, mem, target, gains, sinks, b_gate, weights_of, send_grads, zero):
    T = x.shape[0]
    cos, sin_signed = rope_tables(T, zero)
    no_sink = jnp.full((2,), NEG_INF, F32)

    xn1 = rms_scale(x, gains["ffn1_norm_pre"], "ffn1_norm", cos)
    w = dict(weights_of("ffn1_in", xn1))
    xn1, gu1, a1 = ffn_in(x, gains["ffn1_norm_pre"], w["ffn1_w_in"], "ffn1_in", xn=xn1)
    w.update(weights_of("ffn1_out", xn1))
    f1, h1 = mm_norm_res(a1, w["ffn1_w_out"], x, gains["ffn1_norm_post"], 0.5, "ffn1_out")
    w.update(weights_of("mix", f1))
    u, qkv, gt = mix_in(h1, gains["mix_norm_pre"], w["w_in"], w["w_gate"], b_gate, cos, sin_signed, "mix_in")
    outs, lses = [], []
    for gidx, (window, dil) in enumerate(DIL):
        o_g, l_g = band_fwd(qkv, no_sink, r=dil, base=A_BASE + 6 * gidx, hkv=2, grp=1, max_dist=window // dil,
                            out_dtype=F32, name=f"attn_a{gidx}_fwd")
        outs.append(o_g)
        lses.append(l_g)
    o_a, l_a = merge_groups(outs, lses, "attn_a_merge")
    o_b, l_b = band_fwd(qkv, sinks, r=1, base=B_BASE, hkv=2, grp=2, max_dist=HEAD - 1, out_dtype=BF16,
                        name="attn_b_fwd")
    mem_n, mkv = mem_kv(mem, gains["mem_norm"], w["w_mem_kv"], "mem_kv")
    o_m, l_m = mem_fwd(qkv, mkv, "attn_m_fwd")
    merged, mo, h2 = gate_merge_out(gt, o_a, o_b, o_m, w["w_o_a"], w["w_o_b"], w["w_o_m"], w["w_out"], h1,
                                    gains["mix_norm_post"], "gate_merge_out")
    w.update(weights_of("ffn2", mo))
    xn2, gu2, a2 = ffn_in(h2, gains["ffn2_norm_pre"], w["ffn2_w_in"], "ffn2_in")
    f2, dy, sq = mm_norm_res(a2, w["ffn2_w_out"], h2, gains["ffn2_norm_post"], 0.5, "ffn2_out", target=target)

    grads = {}

    def ffn_bwd(tag, dh_out, f, gu, a, xn, h_in, after):
        df, dgu, dh_in, grads[f"{tag}_norm_pre"], grads[f"{tag}_norm_post"] = ffn_tokens_bwd(
            dh_out, f, h_in, gu, gains[f"{tag}_norm_pre"], gains[f"{tag}_norm_post"], w[f"{tag}_w_in"],
            w[f"{tag}_w_out"], 0.5, f"{tag}_tokens_bwd", after)
        sent = send_grads(f"{tag}_in", {f"{tag}_w_in": mm_tn(
            xn, dgu, D_MODEL, FF_T, f"{tag}_w_in_grad", shard_major=True, perm=_ffn_perm, wire=True)})
        sent = send_grads(f"{tag}_out", {f"{tag}_w_out": mm_tn(
            a, df, FF_T, D_MODEL, f"{tag}_w_out_grad", after=sent, wire=True)})
        return dh_in, sent

    dh2, sent = ffn_bwd("ffn2", dy, f2, gu2, a2, xn2, h2, dy)

    mix = {}
    dmo, grads["mix_norm_post"], dgt, dpa, dpb, dpm, do_a, do_b, do_m, grads["b_gate"] = gate_merge_out_bwd(
        dh2, mo, gains["mix_norm_post"], w["w_out"], gt, o_a, o_b, o_m, w["w_o_a"], w["w_o_b"], w["w_o_m"],
        "gate_merge_out_bwd", sent)
    mix["w_out"] = mm_tn(merged, dmo, D_MODEL, D_MODEL, "w_out_grad", wire=True)
    mix["w_o_a"] = mm_tn(o_a, dpa, o_a.shape[1], D_MODEL, "w_o_a_grad")
    mix["w_o_b"] = mm_tn(o_b, dpb, o_b.shape[1], D_MODEL, "w_o_b_grad")
    mix["w_o_m"] = mm_tn(o_m, dpm, o_m.shape[1], D_MODEL, "w_o_m_grad")

    dqkv = lax.empty(qkv.shape, qkv.dtype)
    for gidx, (window, dil) in enumerate(DIL):
        dqkv, = band_bwd(qkv, dqkv, do_a, o_a, l_a, cos, sin_signed, None, r=dil, base=A_BASE + 6 * gidx, hkv=2,
                         grp=1, max_dist=window // dil, name=f"attn_a{gidx}_bwd")
    dqkv, dsink = band_bwd(qkv, dqkv, do_b, o_b, l_b, cos, sin_signed, sinks, r=1, base=B_BASE, hkv=2, grp=2,
                           max_dist=HEAD - 1, name="attn_b_bwd")
    grads["sinks"] = -dsink[:, ::8, 0].reshape(1, 4)
    dqkv, dmk, dmv = mem_bwd(qkv, dqkv, mkv, do_m, o_m, l_m, "attn_m_bwd")
    mix["w_mem_kv"], grads["mem_norm"] = mem_kv_bwd(
        mem, gains["mem_norm"], mem_n, w["w_mem_kv"], jnp.concatenate([dmk, dmv], axis=1), "mem_kv_bwd")

    mix["w_in"] = mm_tn(u, dqkv, D_MODEL, 1280, "w_in_grad")
    mix["w_gate"] = mm_tn(u, dgt, D_MODEL, 1536, "w_gate_grad", shard_major=True, slabs=2, wire=True)
    sent = send_grads("mix", mix)
    dh1, grads["mix_norm_pre"] = mm_nt_norm_bwd(
        [(dqkv, w["w_in"]), (dgt, w["w_gate"])], h1, dh2, gains["mix_norm_pre"], "mix_in_bwd", sent)

    dx, _ = ffn_bwd("ffn1", dh1, f1, gu1, a1, xn1, x, dh1)
    return sq, dx, grads


def _place():
    return lax.axis_index("x"), lax.axis_index("y"), lax.axis_index("c")


def _other_chips(x, y):
    return [(1 - x, y), (x, 1 - y), (1 - x, 1 - y)]


def _hbm(n):
    return [pl.BlockSpec(memory_space=pltpu.HBM)] * n


SEM = pl.BlockSpec(memory_space=pltpu.SEMAPHORE)
SIDE_EFFECT = pltpu.SideEffectType.DATAFLOW_SIDE_EFFECTING


def _chip_copy(src, land, sems, i, j, dst_slot, scatter):
    x, y, c = _place()
    px, py = _other_chips(x, y)[j]
    send_sems, recv_sems = sems
    return pltpu.make_async_remote_copy(
        src_ref=src[i].at[2 * px + py] if scatter else src[i], dst_ref=land[i].at[dst_slot],
        send_sem=send_sems.at[3 * i + j], recv_sem=recv_sems.at[3 * i + j],
        device_id=(px, py, c), device_id_type=MESH)


def chip_copies_start(srcs, lands, groups, scatter, name, after=None):
    n = len(srcs)

    def body(*refs):
        src, land = refs[:n], refs[n:2 * n]
        first_sem = 2 * n + (after is not None)
        sems = refs[first_sem:first_sem + 2 * len(groups)]
        token = refs[-1]
        x, y, _ = _place()
        for g, members in enumerate(groups):
            part = ([src[i] for i in members], [land[i] for i in members])
            for t in range(len(members)):
                for j in range(3):
                    _chip_copy(*part, sems[2 * g:2 * g + 2], t, j, 2 * x + y, scatter).start()
        token[...] = jnp.zeros_like(token)

    sem_shapes = [pltpu.SemaphoreType.DMA((3 * len(m),)) for m in groups for _ in range(2)]
    thru = [pltpu.HBM(a.shape, a.dtype) for a in (*srcs, *lands)]
    res = pl.pallas_call(
        body, name=name,
        out_shape=(*sem_shapes, *thru, jax.ShapeDtypeStruct((8, 128), F32)),
        in_specs=_hbm(2 * n) + ([] if after is None else [UNREAD]),
        out_specs=(*[SEM] * len(sem_shapes), *_hbm(2 * n), pl.BlockSpec(memory_space=pltpu.VMEM)),
        input_output_aliases={i: len(sem_shapes) + i for i in range(2 * n)},
        compiler_params=pltpu.CompilerParams(has_side_effects=SIDE_EFFECT),
    )(*[pltpu.with_memory_space_constraint(a, pltpu.HBM) for a in (*srcs, *lands)],
      *([] if after is None else [after]))
    k = len(sem_shapes)
    sems = [tuple(res[2 * g:2 * g + 2]) for g in range(len(groups))]
    return sems, list(res[k:k + n]), list(res[k + n:k + 2 * n]), res[-1]


def chip_copies_wait(srcs, lands, sems, after, scatter, name):
    n = len(srcs)

    def body(*refs):
        src, land = refs[:n], refs[n:2 * n]
        pair = refs[2 * n:2 * n + 2]
        x, y, _ = _place()
        for i in range(n):
            for j, (px, py) in enumerate(_other_chips(x, y)):
                copy = _chip_copy(src, land, pair, i, j, 2 * px + py, scatter)
                copy.wait_send()
                copy.wait_recv()

    res = pl.pallas_call(
        body, name=name,
        out_shape=[pltpu.HBM(a.shape, a.dtype) for a in (*srcs, *lands)],
        in_specs=[*_hbm(2 * n), SEM, SEM, pl.BlockSpec(memory_space=pl.ANY)],
        out_specs=_hbm(2 * n),
        input_output_aliases={i: i for i in range(2 * n)},
        compiler_params=pltpu.CompilerParams(has_side_effects=SIDE_EFFECT),
    )(*srcs, *lands, *sems, after)
    return list(res[n:])


def small_all_gather(small, name):
    flips = [(fx, fy, fc) for fx in (0, 1) for fy in (0, 1) for fc in (0, 1)][1:]

    def body(in_ref, out_ref, send_sems, recv_sems, local_sem):
        x, y, c = _place()
        me = 4 * x + 2 * y + c

        def copy(k, slot):
            fx, fy, fc = flips[k]
            return pltpu.make_async_remote_copy(
                src_ref=in_ref, dst_ref=out_ref.at[slot], send_sem=send_sems.at[k], recv_sem=recv_sems.at[k],
                device_id=(x ^ fx, y ^ fy, c ^ fc), device_id_type=MESH)

        local = pltpu.make_async_copy(in_ref, out_ref.at[me], local_sem)
        local.start()
        for k in range(len(flips)):
            copy(k, me).start()
        for k, (fx, fy, fc) in enumerate(flips):
            copy(k, 4 * (x ^ fx) + 2 * (y ^ fy) + (c ^ fc)).wait()
        local.wait()

    return pl.pallas_call(
        body, name=name, in_specs=_hbm(1), out_specs=_hbm(1)[0],
        out_shape=jax.ShapeDtypeStruct((N_DEV,) + small.shape, small.dtype),
        scratch_shapes=[pltpu.SemaphoreType.DMA((len(flips),)), pltpu.SemaphoreType.DMA((len(flips),)),
                        pltpu.SemaphoreType.DMA],
    )(small)


def _sibling_copy(src, land, sems, i):
    x, y, c = _place()
    return pltpu.make_async_remote_copy(
        src_ref=src[i], dst_ref=land[i], send_sem=sems[0].at[i], recv_sem=sems[1].at[i],
        device_id=(x, y, 1 - c), device_id_type=MESH)


def sibling_copies_start(parts, name):
    n = len(parts)
    lands = [lax.empty(p.shape, p.dtype) for p in parts]

    def body(*refs):
        src, land, sems, token = refs[:n], refs[n:2 * n], refs[2 * n:2 * n + 2], refs[-1]
        for i in range(n):
            _sibling_copy(src, land, sems, i).start()
        token[...] = jnp.zeros_like(token)

    res = pl.pallas_call(
        body, name=name,
        out_shape=(pltpu.SemaphoreType.DMA((n,)), pltpu.SemaphoreType.DMA((n,)),
                   *[pltpu.HBM(a.shape, a.dtype) for a in (*parts, *lands)], jax.ShapeDtypeStruct((8, 128), F32)),
        in_specs=_hbm(2 * n),
        out_specs=(SEM, SEM, *_hbm(2 * n), pl.BlockSpec(memory_space=pltpu.VMEM)),
        input_output_aliases={i: 2 + i for i in range(2 * n)},
        compiler_params=pltpu.CompilerParams(has_side_effects=SIDE_EFFECT),
    )(*[pltpu.with_memory_space_constraint(a, pltpu.HBM) for a in (*parts, *lands)])
    return tuple(res[:2]), list(res[2:2 + n]), list(res[2 + n:2 + 2 * n]), res[-1]


def sibling_copies_wait(parts, lands, sems, after, name):
    n = len(parts)

    def body(*refs):
        src, land, sems = refs[:n], refs[n:2 * n], refs[2 * n:2 * n + 2]
        for i in range(n):
            copy = _sibling_copy(src, land, sems, i)
            copy.wait_send()
            copy.wait_recv()

    res = pl.pallas_call(
        body, name=name,
        out_shape=[pltpu.HBM(a.shape, a.dtype) for a in (*parts, *lands)],
        in_specs=[*_hbm(2 * n), SEM, SEM, UNREAD],
        out_specs=_hbm(2 * n),
        input_output_aliases={i: i for i in range(2 * n)},
        compiler_params=pltpu.CompilerParams(has_side_effects=SIDE_EFFECT),
    )(*parts, *lands, *sems, after)
    return list(res[n:])


def _row_tile(rows):
    for t in (256, 176, 128, 64, 32, 16, 8):
        if rows % t == 0:
            return t
    return rows


def chip_partial_sum(me, own_sm, recv, name):
    _, rows, cols = own_sm.shape
    tr = _row_tile(rows)

    def body(me_ref, own_ref, r0, r1, r2, r3, o_ref):
        acc = jnp.zeros((tr, cols), F32)
        for s, r_ref in enumerate((r0, r1, r2, r3)):
            acc = acc + jnp.where(me_ref[0] == s, own_ref[...], r_ref[...].astype(F32))
        o_ref[...] = acc

    def slot(s):
        return pl.BlockSpec((None, tr, cols), lambda i, me_ref, s=s: (s, i, 0))

    return pl.pallas_call(
        body, name=name,
        grid_spec=pltpu.PrefetchScalarGridSpec(
            num_scalar_prefetch=1, grid=(rows // tr,),
            in_specs=[pl.BlockSpec((None, tr, cols), lambda i, me_ref: (me_ref[0], i, 0))] + [slot(s) for s in range(4)],
            out_specs=pl.BlockSpec((tr, cols), lambda i, me_ref: (i, 0))),
        out_shape=jax.ShapeDtypeStruct((rows, cols), F32),
        compiler_params=_params("parallel"),
    )(me, own_sm, recv, recv, recv, recv)


def _adamw(w, g, m, v):
    m = ADAM_B1 * m + (1.0 - ADAM_B1) * g
    v = ADAM_B2 * v + (1.0 - ADAM_B2) * (g * g)
    m_hat = m / (1.0 - ADAM_B1 ** ADAM_STEP)
    v_hat = v / (1.0 - ADAM_B2 ** ADAM_STEP)
    delta = -ADAM_LR * (m_hat / (jnp.sqrt(v_hat) + ADAM_EPS) + ADAM_WD * w)
    return delta, m, v


def adamw_pair(part, sib, w, m, v, name):
    rows, cols = w.shape
    tr = _row_tile(rows)

    def body(p_ref, s_ref, w_ref, m_ref, v_ref, g_ref, d_ref, nm_ref, nv_ref):
        g = p_ref[...] + s_ref[...]
        g_ref[...] = g
        d_ref[...], nm_ref[...], nv_ref[...] = _adamw(w_ref[...], g, m_ref[...], v_ref[...])

    spec = pl.BlockSpec((tr, cols), lambda i: (i, 0))
    return pl.pallas_call(
        body, name=name, grid=(rows // tr,), in_specs=[spec] * 5, out_specs=[spec] * 4,
        out_shape=[jax.ShapeDtypeStruct((rows, cols), F32)] * 4,
        compiler_params=_params("parallel"),
    )(part, sib, w, m, v)


def adamw_small(g_all, w, m, v, name):
    def body(ga_ref, w_ref, m_ref, v_ref, g_ref, d_ref, nm_ref, nv_ref):
        g = ga_ref[0]
        for k in range(1, N_DEV):
            g = g + ga_ref[k]
        g_ref[...] = g
        d_ref[...], nm_ref[...], nv_ref[...] = _adamw(w_ref[...], g, m_ref[...], v_ref[...])

    return pl.pallas_call(
        body, name=name, out_shape=[jax.ShapeDtypeStruct(w.shape, F32)] * 4,
    )(g_all, w, m, v)


WEIGHTS = ("ffn1_norm_pre", "ffn1_w_in", "ffn1_w_out", "ffn1_norm_post", "mix_norm_pre", "w_in", "sinks",
           "mem_norm", "w_mem_kv", "w_gate", "b_gate", "w_o_a", "w_o_b", "w_o_m", "w_out", "mix_norm_post",
           "ffn2_norm_pre", "ffn2_w_in", "ffn2_w_out", "ffn2_norm_post")
BIG = ("ffn1_w_in", "ffn1_w_out", "w_in", "w_mem_kv", "w_gate", "w_o_a", "w_o_b", "w_o_m", "w_out",
       "ffn2_w_in", "ffn2_w_out")
GATHER_STAGES = (("ffn1_in",), ("ffn1_out", "mix"), ("ffn2",))
GATHER_GROUPS = {"ffn1_in": ("ffn1_w_in",), "ffn1_out": ("ffn1_w_out",),
                 "mix": ("w_in", "w_gate", "w_mem_kv", "w_o_a", "w_o_b", "w_o_m", "w_out"),
                 "ffn2": ("ffn2_w_in", "ffn2_w_out")}
GROUPS = {"ffn1_in": ("ffn1_w_in",), "ffn1_out": ("ffn1_w_out",),
          "mix": ("w_in", "w_gate", "w_mem_kv", "w_o_a", "w_o_b", "w_o_m", "w_out"),
          "ffn2_in": ("ffn2_w_in",), "ffn2_out": ("ffn2_w_out",)}
COLUMN_SHARDED = ("ffn1_w_in", "ffn2_w_in", "w_in", "w_gate", "w_o_a", "w_o_b", "w_o_m")
KEPT_SHARD_MAJOR = ("ffn1_w_in", "ffn2_w_in", "w_gate")
GAINS = ("ffn1_norm_pre", "ffn1_norm_post", "mix_norm_pre", "mem_norm", "mix_norm_post", "ffn2_norm_pre",
         "ffn2_norm_post")
SMALL_ROWS = 16


def _pack_small(t):
    sinks = jnp.pad(t["sinks"], ((0, 0), (0, D_MODEL - t["sinks"].shape[1])))
    rows = [t[k] for k in GAINS] + [t["b_gate"].reshape(3, D_MODEL), sinks]
    packed = jnp.concatenate(rows, axis=0)
    return jnp.pad(packed, ((0, SMALL_ROWS - packed.shape[0]), (0, 0)))


def _unpack_small(p):
    out = {k: p[i:i + 1] for i, k in enumerate(GAINS)}
    out["b_gate"] = p[7:10].reshape(1, 3 * D_MODEL)
    out["sinks"] = p[10:11, :4]
    return out


def kernel(x, mem, ffn1_norm_pre, ffn1_w_in, ffn1_w_out, ffn1_norm_post, mix_norm_pre, w_in, sinks, mem_norm, w_mem_kv, w_gate, b_gate, w_o_a, w_o_b, w_o_m, w_out, mix_norm_post, ffn2_norm_pre, ffn2_w_in, ffn2_w_out, ffn2_norm_post, loss_target, m_ffn1_norm_pre, m_ffn1_w_in, m_ffn1_w_out, m_ffn1_norm_post, m_mix_norm_pre, m_w_in, m_sinks, m_mem_norm, m_w_mem_kv, m_w_gate, m_b_gate, m_w_o_a, m_w_o_b, m_w_o_m, m_w_out, m_mix_norm_post, m_ffn2_norm_pre, m_ffn2_w_in, m_ffn2_w_out, m_ffn2_norm_post, v_ffn1_norm_pre, v_ffn1_w_in, v_ffn1_w_out, v_ffn1_norm_post, v_mix_norm_pre, v_w_in, v_sinks, v_mem_norm, v_w_mem_kv, v_w_gate, v_b_gate, v_w_o_a, v_w_o_b, v_w_o_m, v_w_out, v_mix_norm_post, v_ffn2_norm_pre, v_ffn2_w_in, v_ffn2_w_out, v_ffn2_norm_post):
    given = dict(locals())
    wt = {k: given[k] for k in WEIGHTS}
    mom = {k: given["m_" + k] for k in WEIGHTS}
    var = {k: given["v_" + k] for k in WEIGHTS}
    chip = (2 * lax.axis_index("x") + lax.axis_index("y")).astype(jnp.int32)
    me = chip.reshape(1)

    def landing_zone(own):
        return lax.dynamic_update_slice_in_dim(lax.empty((N_CHIPS,) + own.shape, own.dtype), own[None], chip, 0)

    started = {}
    tokens = []

    def start_gather(stage, after):
        groups = GATHER_STAGES[stage]
        keys = [k for g in groups for k in GATHER_GROUPS[g]]
        shards = [(wt[k][0] + tokens[-1][0, 0] if tokens else wt[k][0]).astype(BF16) for k in keys]
        members = [[keys.index(k) for k in GATHER_GROUPS[g]] for g in groups]
        sems, shards, lands, token = chip_copies_start(
            shards, [landing_zone(s) for s in shards], members, False, f"weight_gather_start_{stage}", after)
        tokens.append(token)
        for g, idx, pair in zip(groups, members, sems):
            started[g] = ([shards[i] for i in idx], [lands[i] for i in idx], pair)

    start_gather(0, None)

    def weights_of(group, after):
        got = chip_copies_wait(*started[group], after, False, f"weight_gather_wait_{group}")
        stage = [s + 1 for s, groups in enumerate(GATHER_STAGES[:-1]) if groups[-1] == group]
        if stage:
            start_gather(stage[0], got[0])
        full = {}
        for k, g in zip(GATHER_GROUPS[group], got):
            if k in COLUMN_SHARDED:
                if k in ("ffn1_w_in", "ffn2_w_in"):
                    g = jnp.stack([g[0], g[2], g[1], g[3]])
                full[k] = jnp.swapaxes(g, 0, 1).reshape(g.shape[1], N_CHIPS * g.shape[2])
                if k == "w_in":
                    full[k] = to_kernel_heads(full[k])
            else:
                full[k] = g.reshape(N_CHIPS * g.shape[1], g.shape[2])
        return full

    in_flight = {}

    def send_grads(group, grads):
        def shard_major(k, g):
            if k in KEPT_SHARD_MAJOR:
                return g
            if k in COLUMN_SHARDED:
                return jnp.swapaxes(g.reshape(g.shape[0], N_CHIPS, g.shape[1] // N_CHIPS), 0, 1)
            return g.reshape(N_CHIPS, g.shape[0] // N_CHIPS, g.shape[1])

        own, wire = [], []
        for k in GROUPS[group]:
            g, rounded = grads[k] if isinstance(grads[k], (tuple, list)) else (grads[k], None)
            g = shard_major(k, from_kernel_heads(g) if k == "w_in" else g)
            own.append(g)
            wire.append(g.astype(BF16) if rounded is None else shard_major(k, rounded))
        zones = [landing_zone(lax.dynamic_index_in_dim(b, chip, 0, keepdims=False)) for b in wire]
        pair, wire, zones, sent = chip_copies_start(
            wire, zones, [list(range(len(wire)))], True, f"grad_scatter_start_{group}")
        in_flight[group] = (own, wire, zones, pair[0], sent)
        return sent

    gains = {k: wt[k] for k in GAINS}
    sq, dx, grads = layer_step(
        x[0], mem[0], loss_target[0], gains, sinks[0], b_gate, weights_of, send_grads, tokens[0][0, 0])
    loss = lax.psum(0.5 * sq[0, 0] / D_MODEL, ("x", "y", "c"))

    res = {}
    after = in_flight["ffn1_out"][4]
    swaps = []
    for stage in (("ffn2_in", "ffn2_out", "mix", "ffn1_in"), ("ffn1_out",)):
        names, parts = [], []
        for group in stage:
            own, wire, zones, pair, _ = in_flight[group]
            received = chip_copies_wait(wire, zones, pair, after, True, f"grad_scatter_wait_{group}")
            for k, g, r in zip(GROUPS[group], own, received):
                names.append(k)
                parts.append(chip_partial_sum(me, g, r, f"{k}_chip_sum"))
        pair, parts, lands, after = sibling_copies_start(parts, f"sibling_start_{stage[-1]}")
        swaps.append((stage[-1], names, parts, lands, pair))
    small_all = small_all_gather(_pack_small(grads), "small_grad_gather")
    packed = adamw_small(small_all, _pack_small(wt), _pack_small(mom), _pack_small(var), "small_adamw")
    after = packed[0]
    for tag, names, parts, lands, pair in swaps:
        sibs = sibling_copies_wait(parts, lands, pair, after, f"sibling_wait_{tag}")
        for k, p, s in zip(names, parts, sibs):
            res[k] = [t[None] for t in adamw_pair(p, s, wt[k][0], mom[k][0], var[k][0], f"{k}_adamw")]
        after = res[names[-1]][0]
    for idx, p in enumerate(packed):
        for k, t in _unpack_small(p).items():
            res.setdefault(k, [None] * 4)[idx] = t

    return (loss, dx[None], *[res[k][0] for k in WEIGHTS], *[res[k][1] for k in WEIGHTS],
            *[res[k][2] for k in WEIGHTS], *[res[k][3] for k in WEIGHTS])
```

```python
import functools

import jax
import jax.numpy as jnp
from jax import lax
from jax.experimental import pallas as pl
from jax.experimental.pallas import tpu as pltpu

F32 = jnp.float32
BF16 = jnp.bfloat16

D_MODEL = 1024
D_FF = 2816
HEAD = 128
N_CHIPS = 4
N_DEV = 8
EPS = 1e-6
NEG_INF = -1e30
ROPE_THETA = 10000.0
ATT_SCALE = HEAD ** -0.5

ADAM_LR = 0.001
ADAM_B1 = 0.9
ADAM_B2 = 0.999
ADAM_EPS = 1e-08
ADAM_WD = 0.01
ADAM_STEP = 10

VMEM_LIMIT = 52 * 2 ** 20
VMEM_LIMIT_LARGE = 60 * 2 ** 20
MESH = pl.DeviceIdType.MESH

QKV_W = 3840
DIL = ((128, 1), (512, 4), (2048, 16))
B_BASE, MQ, A_BASE = 0, 8, 12
_AQ, _AK, _AV, _BQ, _BK, _BV, _MQ = 0, 6, 12, 18, 22, 24, 26
HEAD_ORDER = tuple(
    [h for j in range(2) for h in (_BQ + 2 * j, _BQ + 2 * j + 1, _BK + j, _BV + j)]
    + [_MQ + i for i in range(4)]
    + [h for g in range(3) for i in range(2) for h in (_AQ + 2 * g + i, _AK + 2 * g + i, _AV + 2 * g + i)])
ROTARY_HEADS = tuple(p for p, h in enumerate(HEAD_ORDER) if h < _AV or _BQ <= h < _BV)


def to_kernel_heads(w):
    return jnp.concatenate([w[..., h * HEAD:(h + 1) * HEAD] for h in HEAD_ORDER], axis=-1)


def from_kernel_heads(w):
    place = {h: p for p, h in enumerate(HEAD_ORDER)}
    return jnp.concatenate([w[..., place[h] * HEAD:(place[h] + 1) * HEAD] for h in range(len(HEAD_ORDER))], axis=-1)

TM = 512
FF_T = D_FF // 2


def _params(*sem):
    return pltpu.CompilerParams(dimension_semantics=sem, vmem_limit_bytes=VMEM_LIMIT)


def _dot(a, b):
    return jnp.dot(a, b, preferred_element_type=F32)


def _dot_nt(a, b):
    return lax.dot_general(a, b, (((1,), (1,)), ((), ())), preferred_element_type=F32)


def _dot_tn(a, b):
    return lax.dot_general(a, b, (((0,), (0,)), ((), ())), preferred_element_type=F32)


def _rstd(x):
    return lax.rsqrt(jnp.mean(x * x, axis=-1, keepdims=True) + EPS)


def _sigmoid(x):
    return 0.5 * jnp.tanh(0.5 * x) + 0.5


def _ffn_perm(k):
    return (k % 2) * 2 + k // 2


UNREAD = pl.BlockSpec(memory_space=pl.ANY)


def _resident(arr):
    return pl.BlockSpec(arr.shape, lambda *_: (0,) * arr.ndim, pipeline_mode=pl.Buffered(1))


def rms_scale(x, g, name, after):
    T, D = x.shape
    tm = 1024

    def body(x_ref, g_ref, _, o_ref):
        v = x_ref[...]
        o_ref[...] = (v * _rstd(v) * g_ref[...]).astype(BF16)

    spec = pl.BlockSpec((tm, D), lambda i: (i, 0))
    return pl.pallas_call(
        body, name=name, grid=(T // tm,), in_specs=[spec, _resident(g), UNREAD], out_specs=spec,
        out_shape=jax.ShapeDtypeStruct((T, D), BF16), compiler_params=_params("parallel"),
    )(x, g, after)


def ffn_in(h, g, w, name, xn=None):
    T, D = h.shape
    normed = xn is not None

    def body(h_ref, g_ref, w_ref, *outs):
        if normed:
            xn, (gu_ref, a_ref) = h_ref[...], outs
        else:
            xn_ref, gu_ref, a_ref = outs
            x = h_ref[...]
            xn = (x * _rstd(x) * g_ref[...]).astype(BF16)
            xn_ref[...] = xn
        for j in range(2):
            gu = _dot(xn, w_ref[:, j * 2 * FF_T:(j + 1) * 2 * FF_T])
            gu_ref[:, j * 2 * FF_T:(j + 1) * 2 * FF_T] = gu.astype(BF16)
            gate, up = gu[:, :FF_T], gu[:, FF_T:]
            a_ref[:, j * FF_T:(j + 1) * FF_T] = (gate * _sigmoid(gate) * up).astype(BF16)

    def rows(width):
        return pl.BlockSpec((TM, width), lambda i: (i, 0))

    res = pl.pallas_call(
        body, name=name,
        grid=(T // TM,),
        in_specs=[rows(D), _resident(g), _resident(w)],
        out_specs=[rows(D)] * (not normed) + [rows(2 * D_FF), rows(D_FF)],
        out_shape=[jax.ShapeDtypeStruct((T, D), BF16)] * (not normed)
                  + [jax.ShapeDtypeStruct((T, 2 * D_FF), BF16), jax.ShapeDtypeStruct((T, D_FF), BF16)],
        compiler_params=_params("parallel"),
    )(xn if normed else h, g, w)
    return (xn, *res) if normed else tuple(res)


def mm_norm_res(a, w, h_in, g, coef, name, target=None):
    T, K = a.shape
    D = w.shape[1]
    final = target is not None

    def body(*refs):
        if final:
            a_ref, w_ref, h_ref, g_ref, t_ref, f_ref, o_ref, l_ref = refs
        else:
            a_ref, w_ref, h_ref, g_ref, f_ref, o_ref = refs
        f = _dot(a_ref[...], w_ref[...])
        f_ref[...] = f
        y = h_ref[...] + coef * (f * _rstd(f) * g_ref[...])
        if final:
            err = y - t_ref[...]
            o_ref[...] = err * (1.0 / D)

            @pl.when(pl.program_id(0) == 0)
            def _():
                l_ref[...] = jnp.zeros_like(l_ref)

            l_ref[...] += jnp.sum(err * err)
        else:
            o_ref[...] = y

    row = pl.BlockSpec((TM, D), lambda i: (i, 0))
    in_specs = [pl.BlockSpec((TM, K), lambda i: (i, 0)),
                _resident(w),
                row, pl.BlockSpec((1, D), lambda i: (0, 0))]
    out_specs = [row, row]
    out_shape = [jax.ShapeDtypeStruct((T, D), F32), jax.ShapeDtypeStruct((T, D), F32)]
    args = [a, w, h_in, g]
    if final:
        in_specs.append(row)
        args.append(target)
        out_specs.append(pl.BlockSpec((8, 128), lambda i: (0, 0)))
        out_shape.append(jax.ShapeDtypeStruct((8, 128), F32))
    return pl.pallas_call(
        body, name=name, grid=(T // TM,), in_specs=in_specs, out_specs=out_specs, out_shape=out_shape,
        compiler_params=_params("arbitrary"),
    )(*args)


def _rope(x, cos, sin_signed):
    return x * cos + pltpu.roll(x, HEAD // 2, axis=1) * sin_signed


def _unrope(x, cos, sin_signed):
    return x * cos - pltpu.roll(x, HEAD // 2, axis=1) * sin_signed


def mix_in(h, g, w, w_gate, b_gate, cos, sin_signed, name):
    T, D = h.shape
    tn = 768

    def body(h_ref, g_ref, w_ref, wg_ref, b_ref, c_ref, s_ref, u_ref, o_ref, gt_ref):
        x = h_ref[...]
        u = (x * _rstd(x) * g_ref[...]).astype(BF16)
        u_ref[...] = u
        c, s = c_ref[...], s_ref[...]
        for j in range(QKV_W // tn):
            acc = _dot(u, w_ref[:, j * tn:(j + 1) * tn])
            for hd in range(tn // HEAD):
                head = j * (tn // HEAD) + hd
                part = acc[:, hd * HEAD:(hd + 1) * HEAD]
                if head in ROTARY_HEADS:
                    part = _rope(part, c, s)
                o_ref[:, head * HEAD:(head + 1) * HEAD] = part.astype(BF16)
        for j in range(w_gate.shape[1] // tn):
            cols = slice(j * tn, (j + 1) * tn)
            gt_ref[:, cols] = _sigmoid(_dot(u, wg_ref[:, cols]) + b_ref[:, cols]).astype(BF16)

    def rows(width):
        return pl.BlockSpec((TM, width), lambda i: (i, 0))

    return pl.pallas_call(
        body, name=name,
        grid=(T // TM,),
        in_specs=[rows(D), _resident(g), _resident(w), _resident(w_gate), _resident(b_gate), rows(HEAD), rows(HEAD)],
        out_specs=[rows(D), rows(QKV_W), rows(w_gate.shape[1])],
        out_shape=[jax.ShapeDtypeStruct((T, D), BF16), jax.ShapeDtypeStruct((T, QKV_W), BF16),
                   jax.ShapeDtypeStruct((T, w_gate.shape[1]), BF16)],
        compiler_params=_params("parallel"),
    )(h, g, w, w_gate, b_gate, cos, sin_signed)


def gate_merge_out(gt, o_a, o_b, o_m, w_a, w_b, w_m, w_out, h_in, g, name):
    T = gt.shape[0]
    D = D_MODEL

    def body(gt_ref, oa_ref, ob_ref, om_ref, wa_ref, wb_ref, wm_ref, wo_ref, h_ref, g_ref, m_ref, f_ref, o_ref):
        acc = gt_ref[:, :D].astype(F32) * _dot(oa_ref[...], wa_ref[...])
        acc += gt_ref[:, D:2 * D].astype(F32) * _dot(ob_ref[...], wb_ref[...])
        acc += gt_ref[:, 2 * D:].astype(F32) * _dot(om_ref[...], wm_ref[...])
        merged = acc.astype(BF16)
        m_ref[...] = merged
        f = _dot(merged, wo_ref[...])
        f_ref[...] = f
        o_ref[...] = h_ref[...] + f * _rstd(f) * g_ref[...]

    def rows(width):
        return pl.BlockSpec((TM, width), lambda i: (i, 0))

    return pl.pallas_call(
        body, name=name, grid=(T // TM,),
        in_specs=[rows(3 * D), rows(o_a.shape[1]), rows(o_b.shape[1]), rows(o_m.shape[1]),
                  _resident(w_a), _resident(w_b), _resident(w_m), _resident(w_out), rows(D), _resident(g)],
        out_specs=[rows(D), rows(D), rows(D)],
        out_shape=[jax.ShapeDtypeStruct((T, D), BF16), jax.ShapeDtypeStruct((T, D), F32),
                   jax.ShapeDtypeStruct((T, D), F32)],
        compiler_params=_params("parallel"),
    )(gt, o_a, o_b, o_m, w_a, w_b, w_m, w_out, h_in, g)


def _band_rows(start, r):
    return pl.ds(start, HEAD) if r == 1 else pl.ds(start, HEAD, stride=r)


def _band_mask(max_dist, first_has_prev):
    row = lax.broadcasted_iota(jnp.int32, (HEAD, 2 * HEAD), 0)
    col = lax.broadcasted_iota(jnp.int32, (HEAD, 2 * HEAD), 1)
    dist = row + HEAD - col
    band = (dist >= 0) & (dist <= max_dist)
    return band, band & (col >= jnp.where(first_has_prev, 0, HEAD))


def _stack(parts):
    return parts[0] if len(parts) == 1 else jnp.concatenate(parts, axis=0)


def _band_specs(BT, SB, nsub, base, grp):
    stride = grp + 2

    def cur(off, width):
        return pl.BlockSpec((BT, width * HEAD), lambda h, i: (i, (base + h * stride + off) // width))

    def prev(off):
        return pl.BlockSpec((SB, HEAD), lambda h, i: (jnp.maximum(i * nsub - 1, 0), base + h * stride + off))

    return cur(0, grp), cur(grp, 1), prev(grp), cur(grp + 1, 1), prev(grp + 1)


def band_fwd(qkv, sinks, *, r, base, hkv, grp, max_dist, out_dtype, name):
    T, W = qkv.shape
    SB = HEAD * r
    BT = min(2048, T)
    nsub, nib = BT // SB, T // BT
    hq = hkv * grp
    heads = [slice(g * HEAD, (g + 1) * HEAD) for g in range(grp)]

    def body(sink_ref, q_ref, kc_ref, kp_ref, vc_ref, vp_ref, o_ref, l_ref, qf, kf, vf):
        kvh, ib = pl.program_id(0), pl.program_id(1)
        qf[...] = q_ref[...].astype(F32)
        kf[:SB] = kp_ref[...].astype(F32)
        kf[SB:] = kc_ref[...].astype(F32)
        vf[:SB] = vp_ref[...].astype(F32)
        vf[SB:] = vc_ref[...].astype(F32)
        band, band_first = _band_mask(max_dist, ib > 0)
        for c in range(r):
            k_old, v_old = kf[_band_rows(c, r)], vf[_band_rows(c, r)]
            for j in range(nsub):
                mask = band_first if j == 0 else band
                rows = _band_rows(j * SB + c, r)
                k_own, v_own = kf[_band_rows((j + 1) * SB + c, r)], vf[_band_rows((j + 1) * SB + c, r)]
                kcat = jnp.concatenate([k_old, k_own], axis=0).astype(BF16)
                vcat = jnp.concatenate([v_old, v_own], axis=0).astype(BF16)
                k_old, v_old = k_own, v_own
                s_all = _dot_nt(_stack([qf[rows, cols] for cols in heads]).astype(BF16), kcat) * ATT_SCALE
                probs, tots = [], []
                for g, cols in enumerate(heads):
                    s = jnp.where(mask, s_all[cols], NEG_INF)
                    sk = sink_ref[kvh * grp + g]
                    m = jnp.maximum(jnp.max(s, axis=-1, keepdims=True), sk)
                    p = jnp.exp(s - m)
                    tot = jnp.sum(p, axis=-1, keepdims=True) + jnp.exp(sk - m)
                    probs.append(p.astype(BF16))
                    tots.append(tot)
                    l_ref[rows, cols] = jnp.broadcast_to(m + jnp.log(tot), (HEAD, HEAD))
                o_all = _dot(_stack(probs), vcat)
                for g, cols in enumerate(heads):
                    o_ref[rows, cols] = (o_all[cols] / tots[g]).astype(out_dtype)

    out_spec = pl.BlockSpec((BT, grp * HEAD), lambda h, i: (i, h))
    return pl.pallas_call(
        body, name=name, grid=(hkv, nib),
        in_specs=[pl.BlockSpec(memory_space=pltpu.SMEM), *_band_specs(BT, SB, nsub, base, grp)],
        out_specs=[out_spec, out_spec],
        out_shape=[jax.ShapeDtypeStruct((T, hq * HEAD), out_dtype), jax.ShapeDtypeStruct((T, hq * HEAD), F32)],
        scratch_shapes=[pltpu.VMEM((BT, grp * HEAD), F32), pltpu.VMEM((SB + BT, HEAD), F32),
                        pltpu.VMEM((SB + BT, HEAD), F32)],
        compiler_params=_params("parallel", "arbitrary"),
    )(sinks, qkv, qkv, qkv, qkv, qkv)


def band_bwd(qkv, dqkv, do, o, lse, cos, sin_signed, sinks, *, r, base, hkv, grp, max_dist, name):
    T, W = qkv.shape
    SB = HEAD * r
    BT = min(max(2048, 2 * SB), T)
    nsub, nib = BT // SB, T // BT
    nblk = T // SB
    with_sink = sinks is not None
    heads = [slice(g * HEAD, (g + 1) * HEAD) for g in range(grp)]

    def body(*refs):
        if with_sink:
            sink_ref, refs = refs[0], refs[1:]
        (q_ref, kc_ref, kp_ref, vc_ref, vp_ref, qn_ref, do_ref, don_ref, o_ref, on_ref, l_ref, ln_ref,
         c_ref, s_ref, _) = refs[:15]
        out_ref = refs[15]
        ds_ref = refs[16] if with_sink else None
        qf, dof, of, kf, vf, dqf, dkf, dvf = refs[-8:]
        kvh, ib = pl.program_id(0), pl.program_id(1)
        for buf, cur_ref, nxt_ref in ((qf, q_ref, qn_ref), (dof, do_ref, don_ref), (of, o_ref, on_ref)):
            buf[:BT] = cur_ref[...].astype(F32)
            buf[BT:] = nxt_ref[...].astype(F32)
        kf[:SB] = kp_ref[...].astype(F32)
        kf[SB:] = kc_ref[...].astype(F32)
        vf[:SB] = vp_ref[...].astype(F32)
        vf[SB:] = vc_ref[...].astype(F32)
        band, band_first = _band_mask(max_dist, ib > 0)
        if with_sink:
            @pl.when(ib == 0)
            def _():
                ds_ref[...] = jnp.zeros_like(ds_ref)

        def grads(rows, logzs, keys, vals, mask):
            q = _stack([qf[rows, cols] for cols in heads]).astype(BF16)
            dout = _stack([dof[rows, cols] for cols in heads]).astype(BF16)
            s_all = _dot_nt(q, keys) * ATT_SCALE
            dp_all = _dot_nt(dout, vals)
            probs, dss, deltas = [], [], []
            for g, cols in enumerate(heads):
                delta = jnp.sum(dof[rows, cols] * of[rows, cols], axis=-1, keepdims=True)
                p = jnp.exp(jnp.where(mask, s_all[cols], NEG_INF) - logzs[g][:, :1])
                probs.append(p.astype(BF16))
                dss.append((p * (dp_all[cols] - delta) * ATT_SCALE).astype(BF16))
                deltas.append(delta)
            return q, dout, _stack(probs), _stack(dss), deltas

        row = lax.broadcasted_iota(jnp.int32, (HEAD, HEAD), 0)
        col = lax.broadcasted_iota(jnp.int32, (HEAD, HEAD), 1)
        reach = col >= row + jnp.where(ib < nib - 1, HEAD - max_dist, 2 * HEAD)
        for c in range(r):
            k_old, v_old = kf[_band_rows(c, r)], vf[_band_rows(c, r)]
            dk_own = dv_own = None
            for j in range(nsub):
                rows = _band_rows(j * SB + c, r)
                k_own, v_own = kf[_band_rows((j + 1) * SB + c, r)], vf[_band_rows((j + 1) * SB + c, r)]
                kcat = jnp.concatenate([k_old, k_own], axis=0).astype(BF16)
                vcat = jnp.concatenate([v_old, v_own], axis=0).astype(BF16)
                logzs = [l_ref[rows, cols] for cols in heads]
                q, dout, p, ds, deltas = grads(rows, logzs, kcat, vcat, band_first if j == 0 else band)
                dq = _dot(ds, kcat)
                for g, cols in enumerate(heads):
                    dqf[rows, cols] = dq[cols]
                    if with_sink:
                        p_sink = jnp.exp(sink_ref[kvh * grp + g] - logzs[g][:, :1])
                        ds_ref[g * 8:(g + 1) * 8] += jnp.sum(p_sink * deltas[g])
                dk, dv = _dot_tn(ds, q), _dot_tn(p, dout)
                if j > 0:
                    done = _band_rows((j - 1) * SB + c, r)
                    dkf[done] = dk_own + dk[:HEAD]
                    dvf[done] = dv_own + dv[:HEAD]
                dk_own, dv_own = dk[HEAD:], dv[HEAD:]
                k_old, v_old = k_own, v_own
            logzs = [ln_ref[_band_rows(c, r), cols] for cols in heads]
            q, dout, p, ds, _ = grads(_band_rows(BT + c, r), logzs, k_old.astype(BF16), v_old.astype(BF16), reach)
            done = _band_rows((nsub - 1) * SB + c, r)
            dkf[done] = dk_own + _dot_tn(ds, q)
            dvf[done] = dv_own + _dot_tn(p, dout)

        cs, sn = c_ref[...], s_ref[...]
        for cols in heads:
            out_ref[:, cols] = _unrope(dqf[:, cols], cs, sn).astype(BF16)
        out_ref[:, grp * HEAD:(grp + 1) * HEAD] = _unrope(dkf[...], cs, sn).astype(BF16)
        out_ref[:, (grp + 1) * HEAD:] = dvf[...].astype(BF16)

    def nxt_row(i):
        return jnp.minimum((i + 1) * nsub, nblk - 1)

    stride = grp + 2
    q_next = pl.BlockSpec((SB, grp * HEAD), lambda h, i: (nxt_row(i), (base + h * stride) // grp))
    head_cur = pl.BlockSpec((BT, grp * HEAD), lambda h, i: (i, h))
    head_next = pl.BlockSpec((SB, grp * HEAD), lambda h, i: (nxt_row(i), h))
    table = pl.BlockSpec((BT, HEAD), lambda h, i: (i, 0))

    in_specs = [*_band_specs(BT, SB, nsub, base, grp), q_next,
                head_cur, head_next, head_cur, head_next, head_cur, head_next, table, table, UNREAD]
    args = [qkv, qkv, qkv, qkv, qkv, qkv, do, do, o, o, lse, lse, cos, sin_signed, dqkv]
    out_specs = [pl.BlockSpec((BT, stride * HEAD), lambda h, i: (i, base // stride + h))]
    out_shape = [jax.ShapeDtypeStruct(dqkv.shape, dqkv.dtype)]
    if with_sink:
        in_specs.insert(0, pl.BlockSpec(memory_space=pltpu.SMEM))
        args.insert(0, sinks)
        out_specs.append(pl.BlockSpec((None, grp * 8, HEAD), lambda h, i: (h, 0, 0)))
        out_shape.append(jax.ShapeDtypeStruct((hkv, grp * 8, HEAD), F32))
    wide = pltpu.VMEM((BT + SB, grp * HEAD), F32)
    tall = pltpu.VMEM((SB + BT, HEAD), F32)
    grad = pltpu.VMEM((BT, HEAD), F32)
    return pl.pallas_call(
        body, name=name, grid=(hkv, nib), in_specs=in_specs, out_specs=out_specs, out_shape=out_shape,
        input_output_aliases={len(args) - 1: 0},
        scratch_shapes=[wide, wide, wide, tall, tall, pltpu.VMEM((BT, grp * HEAD), F32), grad, grad],
        compiler_params=pltpu.CompilerParams(dimension_semantics=("parallel", "arbitrary"),
                                             vmem_limit_bytes=VMEM_LIMIT_LARGE),
    )(*args)


def merge_groups(outs, lses, name):
    T, Wd = outs[0].shape
    tm = 1024

    def body(o0, o1, o2, l0, l1, l2, out_ref, lt_ref):
        a, b, c = l0[...], l1[...], l2[...]
        m = jnp.maximum(jnp.maximum(a, b), c)
        wa, wb, wc = jnp.exp(a - m), jnp.exp(b - m), jnp.exp(c - m)
        z = wa + wb + wc
        out_ref[...] = ((wa * o0[...] + wb * o1[...] + wc * o2[...]) / z).astype(BF16)
        lt_ref[...] = m + jnp.log(z)

    spec = pl.BlockSpec((tm, Wd), lambda i: (i, 0))
    return pl.pallas_call(
        body, name=name, grid=(T // tm,), in_specs=[spec] * 6, out_specs=[spec, spec],
        out_shape=[jax.ShapeDtypeStruct((T, Wd), BF16), jax.ShapeDtypeStruct((T, Wd), F32)],
        compiler_params=_params("parallel"),
    )(*outs, *lses)


M_HEADS = 4


def mem_kv(mem, g, w, name):
    n, D = mem.shape

    def body(m_ref, g_ref, w_ref, mn_ref, kv_ref):
        x = m_ref[...]
        mn = (x * _rstd(x) * g_ref[...]).astype(BF16)
        mn_ref[...] = mn
        kv_ref[...] = _dot(mn, w_ref[...]).astype(BF16)

    return pl.pallas_call(
        body, name=name,
        out_shape=[jax.ShapeDtypeStruct((n, D), BF16), jax.ShapeDtypeStruct((n, w.shape[1]), BF16)],
        compiler_params=pltpu.CompilerParams(vmem_limit_bytes=VMEM_LIMIT),
    )(mem, g, w)


def mem_fwd(qkv, mkv, name):
    T = qkv.shape[0]
    n = mkv.shape[0]
    RB = 1024

    def body(q_ref, kv_ref, o_ref, l_ref):
        for h in range(M_HEADS):
            cols = slice(h * HEAD, (h + 1) * HEAD)
            s = _dot_nt(q_ref[:, cols], kv_ref[:, cols]) * ATT_SCALE
            m = jnp.max(s, axis=-1, keepdims=True)
            p = jnp.exp(s - m)
            den = jnp.sum(p, axis=-1, keepdims=True)
            vals = kv_ref[:, (M_HEADS + h) * HEAD:(M_HEADS + h + 1) * HEAD]
            o_ref[:, cols] = (_dot(p.astype(BF16), vals) / den).astype(BF16)
            l_ref[:, cols] = jnp.broadcast_to(m + jnp.log(den), (RB, HEAD))

    out = pl.BlockSpec((RB, M_HEADS * HEAD), lambda i: (i, 0))
    return pl.pallas_call(
        body, name=name, grid=(T // RB,),
        in_specs=[pl.BlockSpec((RB, M_HEADS * HEAD), lambda i: (i, MQ // M_HEADS)), _resident(mkv)],
        out_specs=[out, out],
        out_shape=[jax.ShapeDtypeStruct((T, M_HEADS * HEAD), BF16), jax.ShapeDtypeStruct((T, M_HEADS * HEAD), F32)],
        compiler_params=_params("parallel"),
    )(qkv, mkv)


def mem_bwd(qkv, dqkv, mkv, do, o, lse, name):
    T = qkv.shape[0]
    n = mkv.shape[0]
    RB = 1024

    def body(q_ref, kv_ref, do_ref, o_ref, l_ref, _, dq_ref, dk_ref, dv_ref):
        @pl.when(pl.program_id(0) == 0)
        def _():
            dk_ref[...] = jnp.zeros_like(dk_ref)
            dv_ref[...] = jnp.zeros_like(dv_ref)

        for h in range(M_HEADS):
            cols = slice(h * HEAD, (h + 1) * HEAD)
            keys, vals = kv_ref[:, cols], kv_ref[:, (M_HEADS + h) * HEAD:(M_HEADS + h + 1) * HEAD]
            q, dout = q_ref[:, cols], do_ref[:, cols]
            delta = jnp.sum(dout.astype(F32) * o_ref[:, cols].astype(F32), axis=-1, keepdims=True)
            p = jnp.exp(_dot_nt(q, keys) * ATT_SCALE - l_ref[:, cols][:, :1])
            ds = (p * (_dot_nt(dout, vals) - delta) * ATT_SCALE).astype(BF16)
            dq_ref[:, cols] = _dot(ds, keys).astype(BF16)
            dk_ref[:, cols] += _dot_tn(ds, q)
            dv_ref[:, cols] += _dot_tn(p.astype(BF16), dout)

    wide = M_HEADS * HEAD
    tok = pl.BlockSpec((RB, wide), lambda i: (i, 0))
    q_cols = pl.BlockSpec((RB, wide), lambda i: (i, MQ // M_HEADS))
    slot = pl.BlockSpec((n, wide), lambda i: (0, 0))
    return pl.pallas_call(
        body, name=name, grid=(T // RB,),
        in_specs=[q_cols, _resident(mkv), tok, tok, tok, UNREAD],
        out_specs=[q_cols, slot, slot],
        out_shape=[jax.ShapeDtypeStruct(dqkv.shape, dqkv.dtype),
                   jax.ShapeDtypeStruct((n, wide), F32), jax.ShapeDtypeStruct((n, wide), F32)],
        input_output_aliases={5: 0},
        compiler_params=_params("arbitrary"),
    )(qkv, mkv, do, o, lse, dqkv)


def mem_kv_bwd(mem, g, mem_n, w, dmkv, name):
    n, D = mem.shape

    def body(m_ref, g_ref, mn_ref, w_ref, d_ref, dw_ref, dg_ref):
        d = d_ref[...].astype(BF16)
        dw_ref[...] = _dot_tn(mn_ref[...], d)
        x = m_ref[...]
        dg_ref[...] = jnp.sum(_dot_nt(d, w_ref[...]) * (x * _rstd(x)), axis=0, keepdims=True)

    return pl.pallas_call(
        body, name=name,
        out_shape=[jax.ShapeDtypeStruct(w.shape, F32), jax.ShapeDtypeStruct((1, D), F32)],
        compiler_params=pltpu.CompilerParams(vmem_limit_bytes=VMEM_LIMIT),
    )(mem, g, mem_n, w, dmkv)


def _rms_bwd(dn, f, g):
    r = _rstd(f)
    fhat = f * r
    dfhat = dn * g
    df = r * (dfhat - fhat * jnp.mean(dfhat * fhat, axis=-1, keepdims=True))
    return df, jnp.sum(dn * fhat, axis=0, keepdims=True)


def ffn_tokens_bwd(dh, f, h_in, gu, g_pre, g_post, w_in, w_out, coef, name, after):
    T, D = dh.shape

    def body(dh_ref, f_ref, h_ref, gu_ref, gpre_ref, gpost_ref, win_ref, wout_ref, _,
             df_ref, dgu_ref, dhin_ref, dgpre_ref, dgpost_ref, dxn_ref):
        i, j = pl.program_id(0), pl.program_id(1)

        @pl.when(j == 0)
        def _():
            @pl.when(i == 0)
            def _():
                dgpre_ref[...] = jnp.zeros_like(dgpre_ref)
                dgpost_ref[...] = jnp.zeros_like(dgpost_ref)

            df, dg_post = _rms_bwd(coef * dh_ref[...], f_ref[...], gpost_ref[...])
            dgpost_ref[...] += dg_post
            df_ref[...] = df.astype(BF16)

        for jj in range(2):
            @pl.when(j == jj)
            def _(jj=jj):
                lo, mid, hi = 2 * jj * FF_T, (2 * jj + 1) * FF_T, (2 * jj + 2) * FF_T
                da = _dot_nt(df_ref[...], wout_ref[jj * FF_T:(jj + 1) * FF_T, :])
                gate = gu_ref[:, :FF_T].astype(F32)
                up = gu_ref[:, FF_T:].astype(F32)
                sig = _sigmoid(gate)
                dgate = (da * up * sig * (1.0 + gate * (1.0 - sig))).astype(BF16)
                dup = (da * gate * sig).astype(BF16)
                dgu_ref[:, :FF_T] = dgate
                dgu_ref[:, FF_T:] = dup
                part = _dot_nt(dgate, win_ref[:, lo:mid]) + _dot_nt(dup, win_ref[:, mid:hi])
                if jj == 0:
                    dxn_ref[...] = part
                else:
                    h = h_ref[...]
                    r = _rstd(h)
                    xhat = h * r
                    dxn = dxn_ref[...] + part
                    dxhat = dxn * gpre_ref[...]
                    dhin_ref[...] = dh_ref[...] + r * (dxhat - xhat * jnp.mean(dxhat * xhat, axis=-1, keepdims=True))
                    dgpre_ref[...] += jnp.sum(dxn * xhat, axis=0, keepdims=True)

    row = pl.BlockSpec((TM, D), lambda i, j: (i, 0))
    wide = pl.BlockSpec((TM, 2 * FF_T), lambda i, j: (i, j))
    vec = pl.BlockSpec((1, D), lambda i, j: (0, 0))
    return pl.pallas_call(
        body, name=name, grid=(T // TM, 2),
        in_specs=[row, row, row, wide, _resident(g_pre), _resident(g_post), _resident(w_in), _resident(w_out),
                  UNREAD],
        out_specs=[row, wide, row, vec, vec],
        out_shape=[jax.ShapeDtypeStruct((T, D), BF16), jax.ShapeDtypeStruct((T, 2 * D_FF), BF16),
                   jax.ShapeDtypeStruct((T, D), F32), jax.ShapeDtypeStruct((1, D), F32),
                   jax.ShapeDtypeStruct((1, D), F32)],
        scratch_shapes=[pltpu.VMEM((TM, D), F32)],
        compiler_params=pltpu.CompilerParams(dimension_semantics=("arbitrary", "arbitrary"),
                                             vmem_limit_bytes=VMEM_LIMIT_LARGE),
    )(dh, f, h_in, gu, g_pre, g_post, w_in, w_out, after)


def mm_nt_norm_bwd(pieces, h_in, dh_out, g, name, after):
    T, D = h_in.shape

    def body(*refs):
        ab = refs[:2 * len(pieces)]
        h_ref, dh_ref, g_ref, _, o_ref, dg_ref = refs[2 * len(pieces):]
        dxn = _dot_nt(ab[0][...], ab[1][...])
        for p in range(1, len(pieces)):
            dxn += _dot_nt(ab[2 * p][...], ab[2 * p + 1][...])
        h = h_ref[...]
        r = _rstd(h)
        xhat = h * r
        dxhat = dxn * g_ref[...]
        o_ref[...] = dh_ref[...] + r * (dxhat - xhat * jnp.mean(dxhat * xhat, axis=-1, keepdims=True))

        @pl.when(pl.program_id(0) == 0)
        def _():
            dg_ref[...] = jnp.zeros_like(dg_ref)

        dg_ref[...] += jnp.sum(dxn * xhat, axis=0, keepdims=True)

    in_specs, args = [], []
    for a, w in pieces:
        in_specs += [pl.BlockSpec((TM, a.shape[1]), lambda i: (i, 0)), _resident(w)]
        args += [a, w]
    row = pl.BlockSpec((TM, D), lambda i: (i, 0))
    return pl.pallas_call(
        body, name=name, grid=(T // TM,),
        in_specs=in_specs + [row, row, _resident(g), UNREAD],
        out_specs=[row, pl.BlockSpec((1, D), lambda i: (0, 0))],
        out_shape=[jax.ShapeDtypeStruct((T, D), F32), jax.ShapeDtypeStruct((1, D), F32)],
        compiler_params=_params("arbitrary"),
    )(*args, h_in, dh_out, g, after)


def gate_merge_out_bwd(dh, f, g, w_out, gt, o_a, o_b, o_m, w_a, w_b, w_m, name, after):
    T = dh.shape[0]
    D = D_MODEL
    branch = ((o_a, w_a), (o_b, w_b), (o_m, w_m))

    def body(dh_ref, f_ref, g_ref, wo_ref, gt_ref, oa_ref, ob_ref, om_ref, wa_ref, wb_ref, wm_ref, _,
             df_ref, dg_ref, dgt_ref, dpa_ref, dpb_ref, dpm_ref, doa_ref, dob_ref, dom_ref, db_ref):
        @pl.when(pl.program_id(0) == 0)
        def _():
            db_ref[...] = jnp.zeros_like(db_ref)
            dg_ref[...] = jnp.zeros_like(dg_ref)

        df, dg = _rms_bwd(dh_ref[...], f_ref[...], g_ref[...])
        dg_ref[...] += dg
        df = df.astype(BF16)
        df_ref[...] = df
        dmf = _dot_nt(df, wo_ref[...])
        for x, (o_ref, w_ref, dp_ref, do_ref) in enumerate(((oa_ref, wa_ref, dpa_ref, doa_ref),
                                                           (ob_ref, wb_ref, dpb_ref, dob_ref),
                                                           (om_ref, wm_ref, dpm_ref, dom_ref))):
            cols = slice(x * D, (x + 1) * D)
            gx = gt_ref[:, cols].astype(F32)
            w = w_ref[...]
            dpre = dmf * _dot(o_ref[...], w) * gx * (1.0 - gx)
            dgt_ref[:, cols] = dpre.astype(BF16)
            db_ref[:, cols] += jnp.sum(dpre, axis=0, keepdims=True)
            dp = (dmf * gx).astype(BF16)
            dp_ref[...] = dp
            do_ref[...] = _dot_nt(dp, w).astype(BF16)

    def rows(width):
        return pl.BlockSpec((TM, width), lambda i: (i, 0))

    widths = [o.shape[1] for o, _ in branch]
    return pl.pallas_call(
        body, name=name, grid=(T // TM,),
        in_specs=[rows(D), rows(D), _resident(g), _resident(w_out), rows(3 * D)] + [rows(k) for k in widths]
                 + [_resident(w) for _, w in branch] + [UNREAD],
        out_specs=[rows(D), pl.BlockSpec((1, D), lambda i: (0, 0)), rows(3 * D), rows(D), rows(D), rows(D)]
                  + [rows(k) for k in widths] + [pl.BlockSpec((1, 3 * D), lambda i: (0, 0))],
        out_shape=[jax.ShapeDtypeStruct((T, D), BF16), jax.ShapeDtypeStruct((1, D), F32),
                   jax.ShapeDtypeStruct((T, 3 * D), BF16)] + [jax.ShapeDtypeStruct((T, D), BF16)] * 3
                  + [jax.ShapeDtypeStruct((T, k), BF16) for k in widths]
                  + [jax.ShapeDtypeStruct((1, 3 * D), F32)],
        compiler_params=_params("arbitrary"),
    )(dh, f, g, w_out, gt, o_a, o_b, o_m, w_a, w_b, w_m, after)


def mm_tn(x, dy, tm, tn, name, shard_major=False, perm=None, slabs=1, after=None, wire=False):
    T, M = x.shape
    N = dy.shape[1]
    tk = min(2048, T)
    perm = perm or (lambda j: j)
    w = tn // slabs

    def body(x_ref, dy_ref, *rest):
        o_ref = rest[-2] if wire else rest[-1]

        @pl.when(pl.program_id(2) == 0)
        def _():
            o_ref[...] = jnp.zeros_like(o_ref)

        acc = _dot_tn(x_ref[...], dy_ref[...])
        if shard_major:
            for s in range(slabs):
                o_ref[s] += acc[:, s * w:(s + 1) * w]
        else:
            o_ref[...] += acc
        if wire:
            @pl.when(pl.program_id(2) == T // tk - 1)
            def _():
                rest[-1][...] = o_ref[...].astype(BF16)

    if shard_major:
        out_spec = pl.BlockSpec((slabs, tm, w), lambda i, j, k: (perm(j), i, 0))
        out_shape = jax.ShapeDtypeStruct((N // w, M, w), F32)
    else:
        out_spec = pl.BlockSpec((tm, tn), lambda i, j, k: (i, j))
        out_shape = jax.ShapeDtypeStruct((M, N), F32)
    return pl.pallas_call(
        body, name=name, grid=(M // tm, N // tn, T // tk),
        in_specs=[pl.BlockSpec((tk, tm), lambda i, j, k: (k, i)),
                  pl.BlockSpec((tk, tn), lambda i, j, k: (k, j))] + ([] if after is None else [UNREAD]),
        out_specs=[out_spec, out_spec] if wire else out_spec,
        out_shape=[out_shape, jax.ShapeDtypeStruct(out_shape.shape, BF16)] if wire else out_shape,
        compiler_params=_params("parallel", "parallel", "arbitrary"),
    )(x, dy, *([] if after is None else [after]))


def rope_tables(T, zero):
    half = HEAD // 2
    inv = ROPE_THETA ** (-jnp.arange(half, dtype=F32) / half)
    ang = (jnp.arange(T).astype(F32) + zero)[:, None] * inv[None, :]
    cos, sin = jnp.cos(ang), jnp.sin(ang)
    return jnp.concatenate([cos, cos], axis=1), jnp.concatenate([-sin, sin], axis=1)


def layer_step(x, mem, target, gains, sinks, b_gate, weights_of, send_grads, zero):
    T = x.shape[0]
    cos, sin_signed = rope_tables(T, zero)
    no_sink = jnp.full((2,), NEG_INF, F32)

    xn1 = rms_scale(x, gains["ffn1_norm_pre"], "ffn1_norm", cos)
    w = dict(weights_of("ffn1_in", xn1))
    xn1, gu1, a1 = ffn_in(x, gains["ffn1_norm_pre"], w["ffn1_w_in"], "ffn1_in", xn=xn1)
    w.update(weights_of("ffn1_out", xn1))
    f1, h1 = mm_norm_res(a1, w["ffn1_w_out"], x, gains["ffn1_norm_post"], 0.5, "ffn1_out")
    w.update(weights_of("mix", f1))
    u, qkv, gt = mix_in(h1, gains["mix_norm_pre"], w["w_in"], w["w_gate"], b_gate, cos, sin_signed, "mix_in")
    outs, lses = [], []
    for gidx, (window, dil) in enumerate(DIL):
        o_g, l_g = band_fwd(qkv, no_sink, r=dil, base=A_BASE + 6 * gidx, hkv=2, grp=1, max_dist=window // dil,
                            out_dtype=F32, name=f"attn_a{gidx}_fwd")
        outs.append(o_g)
        lses.append(l_g)
    o_a, l_a = merge_groups(outs, lses, "attn_a_merge")
    o_b, l_b = band_fwd(qkv, sinks, r=1, base=B_BASE, hkv=2, grp=2, max_dist=HEAD - 1, out_dtype=BF16,
                        name="attn_b_fwd")
    mem_n, mkv = mem_kv(mem, gains["mem_norm"], w["w_mem_kv"], "mem_kv")
    o_m, l_m = mem_fwd(qkv, mkv, "attn_m_fwd")
    merged, mo, h2 = gate_merge_out(gt, o_a, o_b, o_m, w["w_o_a"], w["w_o_b"], w["w_o_m"], w["w_out"], h1,
                                    gains["mix_norm_post"], "gate_merge_out")
    w.update(weights_of("ffn2", mo))
    xn2, gu2, a2 = ffn_in(h2, gains["ffn2_norm_pre"], w["ffn2_w_in"], "ffn2_in")
    f2, dy, sq = mm_norm_res(a2, w["ffn2_w_out"], h2, gains["ffn2_norm_post"], 0.5, "ffn2_out", target=target)

    grads = {}

    def ffn_bwd(tag, dh_out, f, gu, a, xn, h_in, after):
        df, dgu, dh_in, grads[f"{tag}_norm_pre"], grads[f"{tag}_norm_post"] = ffn_tokens_bwd(
            dh_out, f, h_in, gu, gains[f"{tag}_norm_pre"], gains[f"{tag}_norm_post"], w[f"{tag}_w_in"],
            w[f"{tag}_w_out"], 0.5, f"{tag}_tokens_bwd", after)
        sent = send_grads(f"{tag}_in", {f"{tag}_w_in": mm_tn(
            xn, dgu, D_MODEL, FF_T, f"{tag}_w_in_grad", shard_major=True, perm=_ffn_perm, wire=True)})
        sent = send_grads(f"{tag}_out", {f"{tag}_w_out": mm_tn(
            a, df, FF_T, D_MODEL, f"{tag}_w_out_grad", after=sent, wire=True)})
        return dh_in, sent

    dh2, sent = ffn_bwd("ffn2", dy, f2, gu2, a2, xn2, h2, dy)

    mix = {}
    dmo, grads["mix_norm_post"], dgt, dpa, dpb, dpm, do_a, do_b, do_m, grads["b_gate"] = gate_merge_out_bwd(
        dh2, mo, gains["mix_norm_post"], w["w_out"], gt, o_a, o_b, o_m, w["w_o_a"], w["w_o_b"], w["w_o_m"],
        "gate_merge_out_bwd", sent)
    mix["w_out"] = mm_tn(merged, dmo, D_MODEL, D_MODEL, "w_out_grad", wire=True)
    mix["w_o_a"] = mm_tn(o_a, dpa, o_a.shape[1], D_MODEL, "w_o_a_grad")
    mix["w_o_b"] = mm_tn(o_b, dpb, o_b.shape[1], D_MODEL, "w_o_b_grad")
    mix["w_o_m"] = mm_tn(o_m, dpm, o_m.shape[1], D_MODEL, "w_o_m_grad")

    dqkv = lax.empty(qkv.shape, qkv.dtype)
    for gidx, (window, dil) in enumerate(DIL):
        dqkv, = band_bwd(qkv, dqkv, do_a, o_a, l_a, cos, sin_signed, None, r=dil, base=A_BASE + 6 * gidx, hkv=2,
                         grp=1, max_dist=window // dil, name=f"attn_a{gidx}_bwd")
    dqkv, dsink = band_bwd(qkv, dqkv, do_b, o_b, l_b, cos, sin_signed, sinks, r=1, base=B_BASE, hkv=2, grp=2,
                           max_dist=HEAD - 1, name="attn_b_bwd")
    grads["sinks"] = -dsink[:, ::8, 0].reshape(1, 4)
    dqkv, dmk, dmv = mem_bwd(qkv, dqkv, mkv, do_m, o_m, l_m, "attn_m_bwd")
    mix["w_mem_kv"], grads["mem_norm"] = mem_kv_bwd(
        mem, gains["mem_norm"], mem_n, w["w_mem_kv"], jnp.concatenate([dmk, dmv], axis=1), "mem_kv_bwd")

    mix["w_in"] = mm_tn(u, dqkv, D_MODEL, 1280, "w_in_grad")
    mix["w_gate"] = mm_tn(u, dgt, D_MODEL, 1536, "w_gate_grad", shard_major=True, slabs=2, wire=True)
    sent = send_grads("mix", mix)
    dh1, grads["mix_norm_pre"] = mm_nt_norm_bwd(
        [(dqkv, w["w_in"]), (dgt, w["w_gate"])], h1, dh2, gains["mix_norm_pre"], "mix_in_bwd", sent)

    dx, _ = ffn_bwd("ffn1", dh1, f1, gu1, a1, xn1, x, dh1)
    return sq, dx, grads


def _place():
    return lax.axis_index("x"), lax.axis_index("y"), lax.axis_index("c")


def _other_chips(x, y):
    return [(1 - x, y), (x, 1 - y), (1 - x, 1 - y)]


def _hbm(n):
    return [pl.BlockSpec(memory_space=pltpu.HBM)] * n


SEM = pl.BlockSpec(memory_space=pltpu.SEMAPHORE)
SIDE_EFFECT = pltpu.SideEffectType.DATAFLOW_SIDE_EFFECTING


def _chip_copy(src, land, sems, i, j, dst_slot, scatter):
    x, y, c = _place()
    px, py = _other_chips(x, y)[j]
    send_sems, recv_sems = sems
    return pltpu.make_async_remote_copy(
        src_ref=src[i].at[2 * px + py] if scatter else src[i], dst_ref=land[i].at[dst_slot],
        send_sem=send_sems.at[3 * i + j], recv_sem=recv_sems.at[3 * i + j],
        device_id=(px, py, c), device_id_type=MESH)


def chip_copies_start(srcs, lands, groups, scatter, name, after=None):
    n = len(srcs)

    def body(*refs):
        src, land = refs[:n], refs[n:2 * n]
        first_sem = 2 * n + (after is not None)
        sems = refs[first_sem:first_sem + 2 * len(groups)]
        token = refs[-1]
        x, y, _ = _place()
        for g, members in enumerate(groups):
            part = ([src[i] for i in members], [land[i] for i in members])
            for t in range(len(members)):
                for j in range(3):
                    _chip_copy(*part, sems[2 * g:2 * g + 2], t, j, 2 * x + y, scatter).start()
        token[...] = jnp.zeros_like(token)

    sem_shapes = [pltpu.SemaphoreType.DMA((3 * len(m),)) for m in groups for _ in range(2)]
    thru = [pltpu.HBM(a.shape, a.dtype) for a in (*srcs, *lands)]
    res = pl.pallas_call(
        body, name=name,
        out_shape=(*sem_shapes, *thru, jax.ShapeDtypeStruct((8, 128), F32)),
        in_specs=_hbm(2 * n) + ([] if after is None else [UNREAD]),
        out_specs=(*[SEM] * len(sem_shapes), *_hbm(2 * n), pl.BlockSpec(memory_space=pltpu.VMEM)),
        input_output_aliases={i: len(sem_shapes) + i for i in range(2 * n)},
        compiler_params=pltpu.CompilerParams(has_side_effects=SIDE_EFFECT),
    )(*[pltpu.with_memory_space_constraint(a, pltpu.HBM) for a in (*srcs, *lands)],
      *([] if after is None else [after]))
    k = len(sem_shapes)
    sems = [tuple(res[2 * g:2 * g + 2]) for g in range(len(groups))]
    return sems, list(res[k:k + n]), list(res[k + n:k + 2 * n]), res[-1]


def chip_copies_wait(srcs, lands, sems, after, scatter, name):
    n = len(srcs)

    def body(*refs):
        src, land = refs[:n], refs[n:2 * n]
        pair = refs[2 * n:2 * n + 2]
        x, y, _ = _place()
        for i in range(n):
            for j, (px, py) in enumerate(_other_chips(x, y)):
                copy = _chip_copy(src, land, pair, i, j, 2 * px + py, scatter)
                copy.wait_send()
                copy.wait_recv()

    res = pl.pallas_call(
        body, name=name,
        out_shape=[pltpu.HBM(a.shape, a.dtype) for a in (*srcs, *lands)],
        in_specs=[*_hbm(2 * n), SEM, SEM, pl.BlockSpec(memory_space=pl.ANY)],
        out_specs=_hbm(2 * n),
        input_output_aliases={i: i for i in range(2 * n)},
        compiler_params=pltpu.CompilerParams(has_side_effects=SIDE_EFFECT),
    )(*srcs, *lands, *sems, after)
    return list(res[n:])


def small_all_gather(small, name):
    flips = [(fx, fy, fc) for fx in (0, 1) for fy in (0, 1) for fc in (0, 1)][1:]

    def body(in_ref, out_ref, send_sems, recv_sems, local_sem):
        x, y, c = _place()
        me = 4 * x + 2 * y + c

        def copy(k, slot):
            fx, fy, fc = flips[k]
            return pltpu.make_async_remote_copy(
                src_ref=in_ref, dst_ref=out_ref.at[slot], send_sem=send_sems.at[k], recv_sem=recv_sems.at[k],
                device_id=(x ^ fx, y ^ fy, c ^ fc), device_id_type=MESH)

        local = pltpu.make_async_copy(in_ref, out_ref.at[me], local_sem)
        local.start()
        for k in range(len(flips)):
            copy(k, me).start()
        for k, (fx, fy, fc) in enumerate(flips):
            copy(k, 4 * (x ^ fx) + 2 * (y ^ fy) + (c ^ fc)).wait()
        local.wait()

    return pl.pallas_call(
        body, name=name, in_specs=_hbm(1), out_specs=_hbm(1)[0],
        out_shape=jax.ShapeDtypeStruct((N_DEV,) + small.shape, small.dtype),
        scratch_shapes=[pltpu.SemaphoreType.DMA((len(flips),)), pltpu.SemaphoreType.DMA((len(flips),)),
                        pltpu.SemaphoreType.DMA],
    )(small)


def _sibling_copy(src, land, sems, i):
    x, y, c = _place()
    return pltpu.make_async_remote_copy(
        src_ref=src[i], dst_ref=land[i], send_sem=sems[0].at[i], recv_sem=sems[1].at[i],
        device_id=(x, y, 1 - c), device_id_type=MESH)


def sibling_copies_start(parts, name):
    n = len(parts)
    lands = [lax.empty(p.shape, p.dtype) for p in parts]

    def body(*refs):
        src, land, sems, token = refs[:n], refs[n:2 * n], refs[2 * n:2 * n + 2], refs[-1]
        for i in range(n):
            _sibling_copy(src, land, sems, i).start()
        token[...] = jnp.zeros_like(token)

    res = pl.pallas_call(
        body, name=name,
        out_shape=(pltpu.SemaphoreType.DMA((n,)), pltpu.SemaphoreType.DMA((n,)),
                   *[pltpu.HBM(a.shape, a.dtype) for a in (*parts, *lands)], jax.ShapeDtypeStruct((8, 128), F32)),
        in_specs=_hbm(2 * n),
        out_specs=(SEM, SEM, *_hbm(2 * n), pl.BlockSpec(memory_space=pltpu.VMEM)),
        input_output_aliases={i: 2 + i for i in range(2 * n)},
        compiler_params=pltpu.CompilerParams(has_side_effects=SIDE_EFFECT),
    )(*[pltpu.with_memory_space_constraint(a, pltpu.HBM) for a in (*parts, *lands)])
    return tuple(res[:2]), list(res[2:2 + n]), list(res[2 + n:2 + 2 * n]), res[-1]


def sibling_copies_wait(parts, lands, sems, after, name):
    n = len(parts)

    def body(*refs):
        src, land, sems = refs[:n], refs[n:2 * n], refs[2 * n:2 * n + 2]
        for i in range(n):
            copy = _sibling_copy(src, land, sems, i)
            copy.wait_send()
            copy.wait_recv()

    res = pl.pallas_call(
        body, name=name,
        out_shape=[pltpu.HBM(a.shape, a.dtype) for a in (*parts, *lands)],
        in_specs=[*_hbm(2 * n), SEM, SEM, UNREAD],
        out_specs=_hbm(2 * n),
        input_output_aliases={i: i for i in range(2 * n)},
        compiler_params=pltpu.CompilerParams(has_side_effects=SIDE_EFFECT),
    )(*parts, *lands, *sems, after)
    return list(res[n:])


def _row_tile(rows):
    for t in (256, 176, 128, 64, 32, 16, 8):
        if rows % t == 0:
            return t
    return rows


def chip_partial_sum(me, own_sm, recv, name):
    _, rows, cols = own_sm.shape
    tr = _row_tile(rows)

    def body(me_ref, own_ref, r0, r1, r2, r3, o_ref):
        acc = jnp.zeros((tr, cols), F32)
        for s, r_ref in enumerate((r0, r1, r2, r3)):
            acc = acc + jnp.where(me_ref[0] == s, own_ref[...], r_ref[...].astype(F32))
        o_ref[...] = acc

    def slot(s):
        return pl.BlockSpec((None, tr, cols), lambda i, me_ref, s=s: (s, i, 0))

    return pl.pallas_call(
        body, name=name,
        grid_spec=pltpu.PrefetchScalarGridSpec(
            num_scalar_prefetch=1, grid=(rows // tr,),
            in_specs=[pl.BlockSpec((None, tr, cols), lambda i, me_ref: (me_ref[0], i, 0))] + [slot(s) for s in range(4)],
            out_specs=pl.BlockSpec((tr, cols), lambda i, me_ref: (i, 0))),
        out_shape=jax.ShapeDtypeStruct((rows, cols), F32),
        compiler_params=_params("parallel"),
    )(me, own_sm, recv, recv, recv, recv)


def _adamw(w, g, m, v):
    m = ADAM_B1 * m + (1.0 - ADAM_B1) * g
    v = ADAM_B2 * v + (1.0 - ADAM_B2) * (g * g)
    m_hat = m / (1.0 - ADAM_B1 ** ADAM_STEP)
    v_hat = v / (1.0 - ADAM_B2 ** ADAM_STEP)
    delta = -ADAM_LR * (m_hat / (jnp.sqrt(v_hat) + ADAM_EPS) + ADAM_WD * w)
    return delta, m, v


def adamw_pair(part, sib, w, m, v, name):
    rows, cols = w.shape
    tr = _row_tile(rows)

    def body(p_ref, s_ref, w_ref, m_ref, v_ref, g_ref, d_ref, nm_ref, nv_ref):
        g = p_ref[...] + s_ref[...]
        g_ref[...] = g
        d_ref[...], nm_ref[...], nv_ref[...] = _adamw(w_ref[...], g, m_ref[...], v_ref[...])

    spec = pl.BlockSpec((tr, cols), lambda i: (i, 0))
    return pl.pallas_call(
        body, name=name, grid=(rows // tr,), in_specs=[spec] * 5, out_specs=[spec] * 4,
        out_shape=[jax.ShapeDtypeStruct((rows, cols), F32)] * 4,
        compiler_params=_params("parallel"),
    )(part, sib, w, m, v)


def adamw_small(g_all, w, m, v, name):
    def body(ga_ref, w_ref, m_ref, v_ref, g_ref, d_ref, nm_ref, nv_ref):
        g = ga_ref[0]
        for k in range(1, N_DEV):
            g = g + ga_ref[k]
        g_ref[...] = g
        d_ref[...], nm_ref[...], nv_ref[...] = _adamw(w_ref[...], g, m_ref[...], v_ref[...])

    return pl.pallas_call(
        body, name=name, out_shape=[jax.ShapeDtypeStruct(w.shape, F32)] * 4,
    )(g_all, w, m, v)


WEIGHTS = ("ffn1_norm_pre", "ffn1_w_in", "ffn1_w_out", "ffn1_norm_post", "mix_norm_pre", "w_in", "sinks",
           "mem_norm", "w_mem_kv", "w_gate", "b_gate", "w_o_a", "w_o_b", "w_o_m", "w_out", "mix_norm_post",
           "ffn2_norm_pre", "ffn2_w_in", "ffn2_w_out", "ffn2_norm_post")
BIG = ("ffn1_w_in", "ffn1_w_out", "w_in", "w_mem_kv", "w_gate", "w_o_a", "w_o_b", "w_o_m", "w_out",
       "ffn2_w_in", "ffn2_w_out")
GATHER_STAGES = (("ffn1_in", "ffn1_out"), ("mix",), ("ffn2",))
GATHER_GROUPS = {"ffn1_in": ("ffn1_w_in",), "ffn1_out": ("ffn1_w_out",),
                 "mix": ("w_in", "w_gate", "w_mem_kv", "w_o_a", "w_o_b", "w_o_m", "w_out"),
                 "ffn2": ("ffn2_w_in", "ffn2_w_out")}
GROUPS = {"ffn1_in": ("ffn1_w_in",), "ffn1_out": ("ffn1_w_out",),
          "mix": ("w_in", "w_gate", "w_mem_kv", "w_o_a", "w_o_b", "w_o_m", "w_out"),
          "ffn2_in": ("ffn2_w_in",), "ffn2_out": ("ffn2_w_out",)}
COLUMN_SHARDED = ("ffn1_w_in", "ffn2_w_in", "w_in", "w_gate", "w_o_a", "w_o_b", "w_o_m")
KEPT_SHARD_MAJOR = ("ffn1_w_in", "ffn2_w_in", "w_gate")
GAINS = ("ffn1_norm_pre", "ffn1_norm_post", "mix_norm_pre", "mem_norm", "mix_norm_post", "ffn2_norm_pre",
         "ffn2_norm_post")
SMALL_ROWS = 16


def _pack_small(t):
    sinks = jnp.pad(t["sinks"], ((0, 0), (0, D_MODEL - t["sinks"].shape[1])))
    rows = [t[k] for k in GAINS] + [t["b_gate"].reshape(3, D_MODEL), sinks]
    packed = jnp.concatenate(rows, axis=0)
    return jnp.pad(packed, ((0, SMALL_ROWS - packed.shape[0]), (0, 0)))


def _unpack_small(p):
    out = {k: p[i:i + 1] for i, k in enumerate(GAINS)}
    out["b_gate"] = p[7:10].reshape(1, 3 * D_MODEL)
    out["sinks"] = p[10:11, :4]
    return out


def kernel(x, mem, ffn1_norm_pre, ffn1_w_in, ffn1_w_out, ffn1_norm_post, mix_norm_pre, w_in, sinks, mem_norm, w_mem_kv, w_gate, b_gate, w_o_a, w_o_b, w_o_m, w_out, mix_norm_post, ffn2_norm_pre, ffn2_w_in, ffn2_w_out, ffn2_norm_post, loss_target, m_ffn1_norm_pre, m_ffn1_w_in, m_ffn1_w_out, m_ffn1_norm_post, m_mix_norm_pre, m_w_in, m_sinks, m_mem_norm, m_w_mem_kv, m_w_gate, m_b_gate, m_w_o_a, m_w_o_b, m_w_o_m, m_w_out, m_mix_norm_post, m_ffn2_norm_pre, m_ffn2_w_in, m_ffn2_w_out, m_ffn2_norm_post, v_ffn1_norm_pre, v_ffn1_w_in, v_ffn1_w_out, v_ffn1_norm_post, v_mix_norm_pre, v_w_in, v_sinks, v_mem_norm, v_w_mem_kv, v_w_gate, v_b_gate, v_w_o_a, v_w_o_b, v_w_o_m, v_w_out, v_mix_norm_post, v_ffn2_norm_pre, v_ffn2_w_in, v_ffn2_w_out, v_ffn2_norm_post):
    given = dict(locals())
    wt = {k: given[k] for k in WEIGHTS}
    mom = {k: given["m_" + k] for k in WEIGHTS}
    var = {k: given["v_" + k] for k in WEIGHTS}
    chip = (2 * lax.axis_index("x") + lax.axis_index("y")).astype(jnp.int32)
    me = chip.reshape(1)

    def landing_zone(own):
        return lax.dynamic_update_slice_in_dim(lax.empty((N_CHIPS,) + own.shape, own.dtype), own[None], chip, 0)

    started = {}
    tokens = []

    def start_gather(stage, after):
        groups = GATHER_STAGES[stage]
        keys = [k for g in groups for k in GATHER_GROUPS[g]]
        shards = [(wt[k][0] + tokens[-1][0, 0] if tokens else wt[k][0]).astype(BF16) for k in keys]
        members = [[keys.index(k) for k in GATHER_GROUPS[g]] for g in groups]
        sems, shards, lands, token = chip_copies_start(
            shards, [landing_zone(s) for s in shards], members, False, f"weight_gather_start_{stage}", after)
        tokens.append(token)
        for g, idx, pair in zip(groups, members, sems):
            started[g] = ([shards[i] for i in idx], [lands[i] for i in idx], pair)

    start_gather(0, None)

    def weights_of(group, after):
        got = chip_copies_wait(*started[group], after, False, f"weight_gather_wait_{group}")
        stage = [s + 1 for s, groups in enumerate(GATHER_STAGES[:-1]) if groups[0] == group]
        if stage:
            start_gather(stage[0], got[0])
        full = {}
        for k, g in zip(GATHER_GROUPS[group], got):
            if k in COLUMN_SHARDED:
                if k in ("ffn1_w_in", "ffn2_w_in"):
                    g = jnp.stack([g[0], g[2], g[1], g[3]])
                full[k] = jnp.swapaxes(g, 0, 1).reshape(g.shape[1], N_CHIPS * g.shape[2])
                if k == "w_in":
                    full[k] = to_kernel_heads(full[k])
            else:
                full[k] = g.reshape(N_CHIPS * g.shape[1], g.shape[2])
        return full

    in_flight = {}

    def send_grads(group, grads):
        def shard_major(k, g):
            if k in KEPT_SHARD_MAJOR:
                return g
            if k in COLUMN_SHARDED:
                return jnp.swapaxes(g.reshape(g.shape[0], N_CHIPS, g.shape[1] // N_CHIPS), 0, 1)
            return g.reshape(N_CHIPS, g.shape[0] // N_CHIPS, g.shape[1])

        own, wire = [], []
        for k in GROUPS[group]:
            g, rounded = grads[k] if isinstance(grads[k], (tuple, list)) else (grads[k], None)
            g = shard_major(k, from_kernel_heads(g) if k == "w_in" else g)
            own.append(g)
            wire.append(g.astype(BF16) if rounded is None else shard_major(k, rounded))
        zones = [landing_zone(lax.dynamic_index_in_dim(b, chip, 0, keepdims=False)) for b in wire]
        pair, wire, zones, sent = chip_copies_start(
            wire, zones, [list(range(len(wire)))], True, f"grad_scatter_start_{group}")
        in_flight[group] = (own, wire, zones, pair[0], sent)
        return sent

    gains = {k: wt[k] for k in GAINS}
    sq, dx, grads = layer_step(
        x[0], mem[0], loss_target[0], gains, sinks[0], b_gate, weights_of, send_grads, tokens[0][0, 0])
    loss = lax.psum(0.5 * sq[0, 0] / D_MODEL, ("x", "y", "c"))

    res = {}
    after = in_flight["ffn1_out"][4]
    swaps = []
    for stage in (("ffn2_in", "ffn2_out", "mix", "ffn1_in"), ("ffn1_out",)):
        names, parts = [], []
        for group in stage:
            own, wire, zones, pair, _ = in_flight[group]
            received = chip_copies_wait(wire, zones, pair, after, True, f"grad_scatter_wait_{group}")
            for k, g, r in zip(GROUPS[group], own, received):
                names.append(k)
                parts.append(chip_partial_sum(me, g, r, f"{k}_chip_sum"))
        pair, parts, lands, after = sibling_copies_start(parts, f"sibling_start_{stage[-1]}")
        swaps.append((stage[-1], names, parts, lands, pair))
    small_all = small_all_gather(_pack_small(grads), "small_grad_gather")
    packed = adamw_small(small_all, _pack_small(wt), _pack_small(mom), _pack_small(var), "small_adamw")
    after = packed[0]
    for tag, names, parts, lands, pair in swaps:
        sibs = sibling_copies_wait(parts, lands, pair, after, f"sibling_wait_{tag}")
        for k, p, s in zip(names, parts, sibs):
            res[k] = [t[None] for t in adamw_pair(p, s, wt[k][0], mom[k][0], var[k][0], f"{k}_adamw")]
        after = res[names[-1]][0]
    for idx, p in enumerate(packed):
        for k, t in _unpack_small(p).items():
            res.setdefault(k, [None] * 4)[idx] = t

    return (loss, dx[None], *[res[k][0] for k in WEIGHTS], *[res[k][1] for k in WEIGHTS],
            *[res[k][2] for k in WEIGHTS], *[res[k][3] for k in WEIGHTS])
```

```python
import functools

import jax
import jax.numpy as jnp
from jax import lax
from jax.experimental import pallas as pl
from jax.experimental.pallas import tpu as pltpu

F32 = jnp.float32
BF16 = jnp.bfloat16

D_MODEL = 1024
D_FF = 2816
HEAD = 128
N_CHIPS = 4
N_DEV = 8
EPS = 1e-6
NEG_INF = -1e30
ROPE_THETA = 10000.0
ATT_SCALE = HEAD ** -0.5

ADAM_LR = 0.001
ADAM_B1 = 0.9
ADAM_B2 = 0.999
ADAM_EPS = 1e-08
ADAM_WD = 0.01
ADAM_STEP = 10

VMEM_LIMIT = 52 * 2 ** 20
VMEM_LIMIT_LARGE = 60 * 2 ** 20
MESH = pl.DeviceIdType.MESH

QKV_W = 3840
DIL = ((128, 1), (512, 4), (2048, 16))
B_BASE, MQ, A_BASE = 0, 8, 12
_AQ, _AK, _AV, _BQ, _BK, _BV, _MQ = 0, 6, 12, 18, 22, 24, 26
HEAD_ORDER = tuple(
    [h for j in range(2) for h in (_BQ + 2 * j, _BQ + 2 * j + 1, _BK + j, _BV + j)]
    + [_MQ + i for i in range(4)]
    + [h for g in range(3) for i in range(2) for h in (_AQ + 2 * g + i, _AK + 2 * g + i, _AV + 2 * g + i)])
ROTARY_HEADS = tuple(p for p, h in enumerate(HEAD_ORDER) if h < _AV or _BQ <= h < _BV)


def to_kernel_heads(w):
    return jnp.concatenate([w[..., h * HEAD:(h + 1) * HEAD] for h in HEAD_ORDER], axis=-1)


def from_kernel_heads(w):
    place = {h: p for p, h in enumerate(HEAD_ORDER)}
    return jnp.concatenate([w[..., place[h] * HEAD:(place[h] + 1) * HEAD] for h in range(len(HEAD_ORDER))], axis=-1)

TM = 512
FF_T = D_FF // 2


def _params(*sem):
    return pltpu.CompilerParams(dimension_semantics=sem, vmem_limit_bytes=VMEM_LIMIT)


def _dot(a, b):
    return jnp.dot(a, b, preferred_element_type=F32)


def _dot_nt(a, b):
    return lax.dot_general(a, b, (((1,), (1,)), ((), ())), preferred_element_type=F32)


def _dot_tn(a, b):
    return lax.dot_general(a, b, (((0,), (0,)), ((), ())), preferred_element_type=F32)


def _rstd(x):
    return lax.rsqrt(jnp.mean(x * x, axis=-1, keepdims=True) + EPS)


def _sigmoid(x):
    return 0.5 * jnp.tanh(0.5 * x) + 0.5


def _ffn_perm(k):
    return (k % 2) * 2 + k // 2


UNREAD = pl.BlockSpec(memory_space=pl.ANY)


def _resident(arr):
    return pl.BlockSpec(arr.shape, lambda *_: (0,) * arr.ndim, pipeline_mode=pl.Buffered(1))


def rms_scale(x, g, name, after):
    T, D = x.shape
    tm = 1024

    def body(x_ref, g_ref, _, o_ref):
        v = x_ref[...]
        o_ref[...] = (v * _rstd(v) * g_ref[...]).astype(BF16)

    spec = pl.BlockSpec((tm, D), lambda i: (i, 0))
    return pl.pallas_call(
        body, name=name, grid=(T // tm,), in_specs=[spec, _resident(g), UNREAD], out_specs=spec,
        out_shape=jax.ShapeDtypeStruct((T, D), BF16), compiler_params=_params("parallel"),
    )(x, g, after)


def ffn_in(h, g, w, name, xn=None):
    T, D = h.shape
    normed = xn is not None

    def body(h_ref, g_ref, w_ref, *outs):
        if normed:
            xn, (gu_ref, a_ref) = h_ref[...], outs
        else:
            xn_ref, gu_ref, a_ref = outs
            x = h_ref[...]
            xn = (x * _rstd(x) * g_ref[...]).astype(BF16)
            xn_ref[...] = xn
        for j in range(2):
            gu = _dot(xn, w_ref[:, j * 2 * FF_T:(j + 1) * 2 * FF_T])
            gu_ref[:, j * 2 * FF_T:(j + 1) * 2 * FF_T] = gu.astype(BF16)
            gate, up = gu[:, :FF_T], gu[:, FF_T:]
            a_ref[:, j * FF_T:(j + 1) * FF_T] = (gate * _sigmoid(gate) * up).astype(BF16)

    def rows(width):
        return pl.BlockSpec((TM, width), lambda i: (i, 0))

    res = pl.pallas_call(
        body, name=name,
        grid=(T // TM,),
        in_specs=[rows(D), _resident(g), _resident(w)],
        out_specs=[rows(D)] * (not normed) + [rows(2 * D_FF), rows(D_FF)],
        out_shape=[jax.ShapeDtypeStruct((T, D), BF16)] * (not normed)
                  + [jax.ShapeDtypeStruct((T, 2 * D_FF), BF16), jax.ShapeDtypeStruct((T, D_FF), BF16)],
        compiler_params=_params("parallel"),
    )(xn if normed else h, g, w)
    return (xn, *res) if normed else tuple(res)


def mm_norm_res(a, w, h_in, g, coef, name, target=None):
    T, K = a.shape
    D = w.shape[1]
    final = target is not None

    def body(*refs):
        if final:
            a_ref, w_ref, h_ref, g_ref, t_ref, f_ref, o_ref, l_ref = refs
        else:
            a_ref, w_ref, h_ref, g_ref, f_ref, o_ref = refs
        f = _dot(a_ref[...], w_ref[...])
        f_ref[...] = f
        y = h_ref[...] + coef * (f * _rstd(f) * g_ref[...])
        if final:
            err = y - t_ref[...]
            o_ref[...] = err * (1.0 / D)

            @pl.when(pl.program_id(0) == 0)
            def _():
                l_ref[...] = jnp.zeros_like(l_ref)

            l_ref[...] += jnp.sum(err * err)
        else:
            o_ref[...] = y

    row = pl.BlockSpec((TM, D), lambda i: (i, 0))
    in_specs = [pl.BlockSpec((TM, K), lambda i: (i, 0)),
                _resident(w),
                row, pl.BlockSpec((1, D), lambda i: (0, 0))]
    out_specs = [row, row]
    out_shape = [jax.ShapeDtypeStruct((T, D), F32), jax.ShapeDtypeStruct((T, D), F32)]
    args = [a, w, h_in, g]
    if final:
        in_specs.append(row)
        args.append(target)
        out_specs.append(pl.BlockSpec((8, 128), lambda i: (0, 0)))
        out_shape.append(jax.ShapeDtypeStruct((8, 128), F32))
    return pl.pallas_call(
        body, name=name, grid=(T // TM,), in_specs=in_specs, out_specs=out_specs, out_shape=out_shape,
        compiler_params=_params("arbitrary"),
    )(*args)


def _rope(x, cos, sin_signed):
    return x * cos + pltpu.roll(x, HEAD // 2, axis=1) * sin_signed


def _unrope(x, cos, sin_signed):
    return x * cos - pltpu.roll(x, HEAD // 2, axis=1) * sin_signed


def mix_in(h, g, w, w_gate, b_gate, cos, sin_signed, name):
    T, D = h.shape
    tn = 768

    def body(h_ref, g_ref, w_ref, wg_ref, b_ref, c_ref, s_ref, u_ref, o_ref, gt_ref):
        x = h_ref[...]
        u = (x * _rstd(x) * g_ref[...]).astype(BF16)
        u_ref[...] = u
        c, s = c_ref[...], s_ref[...]
        for j in range(QKV_W // tn):
            acc = _dot(u, w_ref[:, j * tn:(j + 1) * tn])
            for hd in range(tn // HEAD):
                head = j * (tn // HEAD) + hd
                part = acc[:, hd * HEAD:(hd + 1) * HEAD]
                if head in ROTARY_HEADS:
                    part = _rope(part, c, s)
                o_ref[:, head * HEAD:(head + 1) * HEAD] = part.astype(BF16)
        for j in range(w_gate.shape[1] // tn):
            cols = slice(j * tn, (j + 1) * tn)
            gt_ref[:, cols] = _sigmoid(_dot(u, wg_ref[:, cols]) + b_ref[:, cols]).astype(BF16)

    def rows(width):
        return pl.BlockSpec((TM, width), lambda i: (i, 0))

    return pl.pallas_call(
        body, name=name,
        grid=(T // TM,),
        in_specs=[rows(D), _resident(g), _resident(w), _resident(w_gate), _resident(b_gate), rows(HEAD), rows(HEAD)],
        out_specs=[rows(D), rows(QKV_W), rows(w_gate.shape[1])],
        out_shape=[jax.ShapeDtypeStruct((T, D), BF16), jax.ShapeDtypeStruct((T, QKV_W), BF16),
                   jax.ShapeDtypeStruct((T, w_gate.shape[1]), BF16)],
        compiler_params=_params("parallel"),
    )(h, g, w, w_gate, b_gate, cos, sin_signed)


def gate_merge_out(gt, o_a, o_b, o_m, w_a, w_b, w_m, w_out, h_in, g, name):
    T = gt.shape[0]
    D = D_MODEL

    def body(gt_ref, oa_ref, ob_ref, om_ref, wa_ref, wb_ref, wm_ref, wo_ref, h_ref, g_ref, m_ref, f_ref, o_ref):
        acc = gt_ref[:, :D].astype(F32) * _dot(oa_ref[...], wa_ref[...])
        acc += gt_ref[:, D:2 * D].astype(F32) * _dot(ob_ref[...], wb_ref[...])
        acc += gt_ref[:, 2 * D:].astype(F32) * _dot(om_ref[...], wm_ref[...])
        merged = acc.astype(BF16)
        m_ref[...] = merged
        f = _dot(merged, wo_ref[...])
        f_ref[...] = f
        o_ref[...] = h_ref[...] + f * _rstd(f) * g_ref[...]

    def rows(width):
        return pl.BlockSpec((TM, width), lambda i: (i, 0))

    return pl.pallas_call(
        body, name=name, grid=(T // TM,),
        in_specs=[rows(3 * D), rows(o_a.shape[1]), rows(o_b.shape[1]), rows(o_m.shape[1]),
                  _resident(w_a), _resident(w_b), _resident(w_m), _resident(w_out), rows(D), _resident(g)],
        out_specs=[rows(D), rows(D), rows(D)],
        out_shape=[jax.ShapeDtypeStruct((T, D), BF16), jax.ShapeDtypeStruct((T, D), F32),
                   jax.ShapeDtypeStruct((T, D), F32)],
        compiler_params=_params("parallel"),
    )(gt, o_a, o_b, o_m, w_a, w_b, w_m, w_out, h_in, g)


def _band_rows(start, r):
    return pl.ds(start, HEAD) if r == 1 else pl.ds(start, HEAD, stride=r)


def _band_mask(max_dist, first_has_prev):
    row = lax.broadcasted_iota(jnp.int32, (HEAD, 2 * HEAD), 0)
    col = lax.broadcasted_iota(jnp.int32, (HEAD, 2 * HEAD), 1)
    dist = row + HEAD - col
    band = (dist >= 0) & (dist <= max_dist)
    return band, band & (col >= jnp.where(first_has_prev, 0, HEAD))


def _stack(parts):
    return parts[0] if len(parts) == 1 else jnp.concatenate(parts, axis=0)


def _band_specs(BT, SB, nsub, base, grp):
    stride = grp + 2

    def cur(off, width):
        return pl.BlockSpec((BT, width * HEAD), lambda h, i: (i, (base + h * stride + off) // width))

    def prev(off):
        return pl.BlockSpec((SB, HEAD), lambda h, i: (jnp.maximum(i * nsub - 1, 0), base + h * stride + off))

    return cur(0, grp), cur(grp, 1), prev(grp), cur(grp + 1, 1), prev(grp + 1)


def band_fwd(qkv, sinks, *, r, base, hkv, grp, max_dist, out_dtype, name, merge=None):
    T, W = qkv.shape
    SB = HEAD * r
    BT = min(2048, T)
    nsub, nib = BT // SB, T // BT
    hq = hkv * grp
    heads = [slice(g * HEAD, (g + 1) * HEAD) for g in range(grp)]
    others = [] if merge is None else [*merge[0], *merge[1]]

    def body(sink_ref, q_ref, kc_ref, kp_ref, vc_ref, vp_ref, *rest):
        joint_o, joint_l = rest[len(others):len(others) + 2]
        qf, kf, vf = rest[len(others) + 2:len(others) + 5]
        o_ref, l_ref = rest[len(others) + 5:] if others else (joint_o, joint_l)
        kvh, ib = pl.program_id(0), pl.program_id(1)
        qf[...] = q_ref[...].astype(F32)
        kf[:SB] = kp_ref[...].astype(F32)
        kf[SB:] = kc_ref[...].astype(F32)
        vf[:SB] = vp_ref[...].astype(F32)
        vf[SB:] = vc_ref[...].astype(F32)
        band, band_first = _band_mask(max_dist, ib > 0)
        for c in range(r):
            k_old, v_old = kf[_band_rows(c, r)], vf[_band_rows(c, r)]
            for j in range(nsub):
                mask = band_first if j == 0 else band
                rows = _band_rows(j * SB + c, r)
                k_own, v_own = kf[_band_rows((j + 1) * SB + c, r)], vf[_band_rows((j + 1) * SB + c, r)]
                kcat = jnp.concatenate([k_old, k_own], axis=0).astype(BF16)
                vcat = jnp.concatenate([v_old, v_own], axis=0).astype(BF16)
                k_old, v_old = k_own, v_own
                s_all = _dot_nt(_stack([qf[rows, cols] for cols in heads]).astype(BF16), kcat) * ATT_SCALE
                probs, tots = [], []
                for g, cols in enumerate(heads):
                    s = jnp.where(mask, s_all[cols], NEG_INF)
                    sk = sink_ref[kvh * grp + g]
                    m = jnp.maximum(jnp.max(s, axis=-1, keepdims=True), sk)
                    p = jnp.exp(s - m)
                    tot = jnp.sum(p, axis=-1, keepdims=True) + jnp.exp(sk - m)
                    probs.append(p.astype(BF16))
                    tots.append(tot)
                    l_ref[rows, cols] = jnp.broadcast_to(m + jnp.log(tot), (HEAD, HEAD))
                o_all = _dot(_stack(probs), vcat)
                for g, cols in enumerate(heads):
                    o_ref[rows, cols] = (o_all[cols] / tots[g]).astype(o_ref.dtype)

        if others:
            half = len(others) // 2
            outs = [ref[...] for ref in rest[:half]] + [o_ref[...]]
            logs = [ref[...] for ref in rest[half:len(others)]] + [l_ref[...]]
            top = functools.reduce(jnp.maximum, logs)
            weights = [jnp.exp(lg - top) for lg in logs]
            total = functools.reduce(jnp.add, weights)
            mixed = functools.reduce(jnp.add, [wgt * out for wgt, out in zip(weights, outs)])
            joint_o[...] = (mixed / total).astype(out_dtype)
            joint_l[...] = top + jnp.log(total)

    out_spec = pl.BlockSpec((BT, grp * HEAD), lambda h, i: (i, h))
    own = [pltpu.VMEM((BT, grp * HEAD), F32)] * 2 if others else []
    return pl.pallas_call(
        body, name=name, grid=(hkv, nib),
        in_specs=[pl.BlockSpec(memory_space=pltpu.SMEM), *_band_specs(BT, SB, nsub, base, grp)]
                 + [out_spec] * len(others),
        out_specs=[out_spec, out_spec],
        out_shape=[jax.ShapeDtypeStruct((T, hq * HEAD), out_dtype), jax.ShapeDtypeStruct((T, hq * HEAD), F32)],
        scratch_shapes=[pltpu.VMEM((BT, grp * HEAD), F32), pltpu.VMEM((SB + BT, HEAD), F32),
                        pltpu.VMEM((SB + BT, HEAD), F32)] + own,
        compiler_params=_params("parallel", "arbitrary"),
    )(sinks, qkv, qkv, qkv, qkv, qkv, *others)


def band_bwd(qkv, dqkv, do, o, lse, cos, sin_signed, sinks, *, r, base, hkv, grp, max_dist, name):
    T, W = qkv.shape
    SB = HEAD * r
    BT = min(max(2048, 2 * SB), T)
    nsub, nib = BT // SB, T // BT
    nblk = T // SB
    with_sink = sinks is not None
    heads = [slice(g * HEAD, (g + 1) * HEAD) for g in range(grp)]

    def body(*refs):
        if with_sink:
            sink_ref, refs = refs[0], refs[1:]
        (q_ref, kc_ref, kp_ref, vc_ref, vp_ref, qn_ref, do_ref, don_ref, o_ref, on_ref, l_ref, ln_ref,
         c_ref, s_ref, _) = refs[:15]
        out_ref = refs[15]
        ds_ref = refs[16] if with_sink else None
        qf, dof, of, kf, vf, dqf, dkf, dvf = refs[-8:]
        kvh, ib = pl.program_id(0), pl.program_id(1)
        for buf, cur_ref, nxt_ref in ((qf, q_ref, qn_ref), (dof, do_ref, don_ref), (of, o_ref, on_ref)):
            buf[:BT] = cur_ref[...].astype(F32)
            buf[BT:] = nxt_ref[...].astype(F32)
        kf[:SB] = kp_ref[...].astype(F32)
        kf[SB:] = kc_ref[...].astype(F32)
        vf[:SB] = vp_ref[...].astype(F32)
        vf[SB:] = vc_ref[...].astype(F32)
        band, band_first = _band_mask(max_dist, ib > 0)
        if with_sink:
            @pl.when(ib == 0)
            def _():
                ds_ref[...] = jnp.zeros_like(ds_ref)

        def grads(rows, logzs, keys, vals, mask):
            q = _stack([qf[rows, cols] for cols in heads]).astype(BF16)
            dout = _stack([dof[rows, cols] for cols in heads]).astype(BF16)
            s_all = _dot_nt(q, keys) * ATT_SCALE
            dp_all = _dot_nt(dout, vals)
            probs, dss, deltas = [], [], []
            for g, cols in enumerate(heads):
                delta = jnp.sum(dof[rows, cols] * of[rows, cols], axis=-1, keepdims=True)
                p = jnp.exp(jnp.where(mask, s_all[cols], NEG_INF) - logzs[g][:, :1])
                probs.append(p.astype(BF16))
                dss.append((p * (dp_all[cols] - delta) * ATT_SCALE).astype(BF16))
                deltas.append(delta)
            return q, dout, _stack(probs), _stack(dss), deltas

        row = lax.broadcasted_iota(jnp.int32, (HEAD, HEAD), 0)
        col = lax.broadcasted_iota(jnp.int32, (HEAD, HEAD), 1)
        reach = col >= row + jnp.where(ib < nib - 1, HEAD - max_dist, 2 * HEAD)
        for c in range(r):
            k_old, v_old = kf[_band_rows(c, r)], vf[_band_rows(c, r)]
            dk_own = dv_own = None
            for j in range(nsub):
                rows = _band_rows(j * SB + c, r)
                k_own, v_own = kf[_band_rows((j + 1) * SB + c, r)], vf[_band_rows((j + 1) * SB + c, r)]
                kcat = jnp.concatenate([k_old, k_own], axis=0).astype(BF16)
                vcat = jnp.concatenate([v_old, v_own], axis=0).astype(BF16)
                logzs = [l_ref[rows, cols] for cols in heads]
                q, dout, p, ds, deltas = grads(rows, logzs, kcat, vcat, band_first if j == 0 else band)
                dq = _dot(ds, kcat)
                for g, cols in enumerate(heads):
                    dqf[rows, cols] = dq[cols]
                    if with_sink:
                        p_sink = jnp.exp(sink_ref[kvh * grp + g] - logzs[g][:, :1])
                        ds_ref[g * 8:(g + 1) * 8] += jnp.sum(p_sink * deltas[g])
                dk, dv = _dot_tn(ds, q), _dot_tn(p, dout)
                if j > 0:
                    done = _band_rows((j - 1) * SB + c, r)
                    dkf[done] = dk_own + dk[:HEAD]
                    dvf[done] = dv_own + dv[:HEAD]
                dk_own, dv_own = dk[HEAD:], dv[HEAD:]
                k_old, v_old = k_own, v_own
            logzs = [ln_ref[_band_rows(c, r), cols] for cols in heads]
            q, dout, p, ds, _ = grads(_band_rows(BT + c, r), logzs, k_old.astype(BF16), v_old.astype(BF16), reach)
            done = _band_rows((nsub - 1) * SB + c, r)
            dkf[done] = dk_own + _dot_tn(ds, q)
            dvf[done] = dv_own + _dot_tn(p, dout)

        cs, sn = c_ref[...], s_ref[...]
        for cols in heads:
            out_ref[:, cols] = _unrope(dqf[:, cols], cs, sn).astype(BF16)
        out_ref[:, grp * HEAD:(grp + 1) * HEAD] = _unrope(dkf[...], cs, sn).astype(BF16)
        out_ref[:, (grp + 1) * HEAD:] = dvf[...].astype(BF16)

    def nxt_row(i):
        return jnp.minimum((i + 1) * nsub, nblk - 1)

    stride = grp + 2
    q_next = pl.BlockSpec((SB, grp * HEAD), lambda h, i: (nxt_row(i), (base + h * stride) // grp))
    head_cur = pl.BlockSpec((BT, grp * HEAD), lambda h, i: (i, h))
    head_next = pl.BlockSpec((SB, grp * HEAD), lambda h, i: (nxt_row(i), h))
    table = pl.BlockSpec((BT, HEAD), lambda h, i: (i, 0))

    in_specs = [*_band_specs(BT, SB, nsub, base, grp), q_next,
                head_cur, head_next, head_cur, head_next, head_cur, head_next, table, table, UNREAD]
    args = [qkv, qkv, qkv, qkv, qkv, qkv, do, do, o, o, lse, lse, cos, sin_signed, dqkv]
    out_specs = [pl.BlockSpec((BT, stride * HEAD), lambda h, i: (i, base // stride + h))]
    out_shape = [jax.ShapeDtypeStruct(dqkv.shape, dqkv.dtype)]
    if with_sink:
        in_specs.insert(0, pl.BlockSpec(memory_space=pltpu.SMEM))
        args.insert(0, sinks)
        out_specs.append(pl.BlockSpec((None, grp * 8, HEAD), lambda h, i: (h, 0, 0)))
        out_shape.append(jax.ShapeDtypeStruct((hkv, grp * 8, HEAD), F32))
    wide = pltpu.VMEM((BT + SB, grp * HEAD), F32)
    tall = pltpu.VMEM((SB + BT, HEAD), F32)
    grad = pltpu.VMEM((BT, HEAD), F32)
    return pl.pallas_call(
        body, name=name, grid=(hkv, nib), in_specs=in_specs, out_specs=out_specs, out_shape=out_shape,
        input_output_aliases={len(args) - 1: 0},
        scratch_shapes=[wide, wide, wide, tall, tall, pltpu.VMEM((BT, grp * HEAD), F32), grad, grad],
        compiler_params=pltpu.CompilerParams(dimension_semantics=("parallel", "arbitrary"),
                                             vmem_limit_bytes=VMEM_LIMIT_LARGE),
    )(*args)


M_HEADS = 4


def mem_kv(mem, g, w, name):
    n, D = mem.shape

    def body(m_ref, g_ref, w_ref, mn_ref, kv_ref):
        x = m_ref[...]
        mn = (x * _rstd(x) * g_ref[...]).astype(BF16)
        mn_ref[...] = mn
        kv_ref[...] = _dot(mn, w_ref[...]).astype(BF16)

    return pl.pallas_call(
        body, name=name,
        out_shape=[jax.ShapeDtypeStruct((n, D), BF16), jax.ShapeDtypeStruct((n, w.shape[1]), BF16)],
        compiler_params=pltpu.CompilerParams(vmem_limit_bytes=VMEM_LIMIT),
    )(mem, g, w)


def mem_fwd(qkv, mkv, name):
    T = qkv.shape[0]
    n = mkv.shape[0]
    RB = 1024

    def body(q_ref, kv_ref, o_ref, l_ref):
        for h in range(M_HEADS):
            cols = slice(h * HEAD, (h + 1) * HEAD)
            s = _dot_nt(q_ref[:, cols], kv_ref[:, cols]) * ATT_SCALE
            m = jnp.max(s, axis=-1, keepdims=True)
            p = jnp.exp(s - m)
            den = jnp.sum(p, axis=-1, keepdims=True)
            vals = kv_ref[:, (M_HEADS + h) * HEAD:(M_HEADS + h + 1) * HEAD]
            o_ref[:, cols] = (_dot(p.astype(BF16), vals) / den).astype(BF16)
            l_ref[:, cols] = jnp.broadcast_to(m + jnp.log(den), (RB, HEAD))

    out = pl.BlockSpec((RB, M_HEADS * HEAD), lambda i: (i, 0))
    return pl.pallas_call(
        body, name=name, grid=(T // RB,),
        in_specs=[pl.BlockSpec((RB, M_HEADS * HEAD), lambda i: (i, MQ // M_HEADS)), _resident(mkv)],
        out_specs=[out, out],
        out_shape=[jax.ShapeDtypeStruct((T, M_HEADS * HEAD), BF16), jax.ShapeDtypeStruct((T, M_HEADS * HEAD), F32)],
        compiler_params=_params("parallel"),
    )(qkv, mkv)


def mem_bwd(qkv, dqkv, mkv, do, o, lse, name):
    T = qkv.shape[0]
    n = mkv.shape[0]
    RB = 1024

    def body(q_ref, kv_ref, do_ref, o_ref, l_ref, _, dq_ref, dk_ref, dv_ref):
        @pl.when(pl.program_id(0) == 0)
        def _():
            dk_ref[...] = jnp.zeros_like(dk_ref)
            dv_ref[...] = jnp.zeros_like(dv_ref)

        for h in range(M_HEADS):
            cols = slice(h * HEAD, (h + 1) * HEAD)
            keys, vals = kv_ref[:, cols], kv_ref[:, (M_HEADS + h) * HEAD:(M_HEADS + h + 1) * HEAD]
            q, dout = q_ref[:, cols], do_ref[:, cols]
            delta = jnp.sum(dout.astype(F32) * o_ref[:, cols].astype(F32), axis=-1, keepdims=True)
            p = jnp.exp(_dot_nt(q, keys) * ATT_SCALE - l_ref[:, cols][:, :1])
            ds = (p * (_dot_nt(dout, vals) - delta) * ATT_SCALE).astype(BF16)
            dq_ref[:, cols] = _dot(ds, keys).astype(BF16)
            dk_ref[:, cols] += _dot_tn(ds, q)
            dv_ref[:, cols] += _dot_tn(p.astype(BF16), dout)

    wide = M_HEADS * HEAD
    tok = pl.BlockSpec((RB, wide), lambda i: (i, 0))
    q_cols = pl.BlockSpec((RB, wide), lambda i: (i, MQ // M_HEADS))
    slot = pl.BlockSpec((n, wide), lambda i: (0, 0))
    return pl.pallas_call(
        body, name=name, grid=(T // RB,),
        in_specs=[q_cols, _resident(mkv), tok, tok, tok, UNREAD],
        out_specs=[q_cols, slot, slot],
        out_shape=[jax.ShapeDtypeStruct(dqkv.shape, dqkv.dtype),
                   jax.ShapeDtypeStruct((n, wide), F32), jax.ShapeDtypeStruct((n, wide), F32)],
        input_output_aliases={5: 0},
        compiler_params=_params("arbitrary"),
    )(qkv, mkv, do, o, lse, dqkv)


def mem_kv_bwd(mem, g, mem_n, w, dmkv, name):
    n, D = mem.shape

    def body(m_ref, g_ref, mn_ref, w_ref, d_ref, dw_ref, dg_ref):
        d = d_ref[...].astype(BF16)
        dw_ref[...] = _dot_tn(mn_ref[...], d)
        x = m_ref[...]
        dg_ref[...] = jnp.sum(_dot_nt(d, w_ref[...]) * (x * _rstd(x)), axis=0, keepdims=True)

    return pl.pallas_call(
        body, name=name,
        out_shape=[jax.ShapeDtypeStruct(w.shape, F32), jax.ShapeDtypeStruct((1, D), F32)],
        compiler_params=pltpu.CompilerParams(vmem_limit_bytes=VMEM_LIMIT),
    )(mem, g, mem_n, w, dmkv)


def _rms_bwd(dn, f, g):
    r = _rstd(f)
    fhat = f * r
    dfhat = dn * g
    df = r * (dfhat - fhat * jnp.mean(dfhat * fhat, axis=-1, keepdims=True))
    return df, jnp.sum(dn * fhat, axis=0, keepdims=True)


def ffn_tokens_bwd(dh, f, h_in, gu, g_pre, g_post, w_in, w_out, coef, name, after):
    T, D = dh.shape

    def body(dh_ref, f_ref, h_ref, gu_ref, gpre_ref, gpost_ref, win_ref, wout_ref, _,
             df_ref, dgu_ref, dhin_ref, dgpre_ref, dgpost_ref, dxn_ref):
        i, j = pl.program_id(0), pl.program_id(1)

        @pl.when(j == 0)
        def _():
            @pl.when(i == 0)
            def _():
                dgpre_ref[...] = jnp.zeros_like(dgpre_ref)
                dgpost_ref[...] = jnp.zeros_like(dgpost_ref)

            df, dg_post = _rms_bwd(coef * dh_ref[...], f_ref[...], gpost_ref[...])
            dgpost_ref[...] += dg_post
            df_ref[...] = df.astype(BF16)

        for jj in range(2):
            @pl.when(j == jj)
            def _(jj=jj):
                lo, mid, hi = 2 * jj * FF_T, (2 * jj + 1) * FF_T, (2 * jj + 2) * FF_T
                da = _dot_nt(df_ref[...], wout_ref[jj * FF_T:(jj + 1) * FF_T, :])
                gate = gu_ref[:, :FF_T].astype(F32)
                up = gu_ref[:, FF_T:].astype(F32)
                sig = _sigmoid(gate)
                dgate = (da * up * sig * (1.0 + gate * (1.0 - sig))).astype(BF16)
                dup = (da * gate * sig).astype(BF16)
                dgu_ref[:, :FF_T] = dgate
                dgu_ref[:, FF_T:] = dup
                part = _dot_nt(dgate, win_ref[:, lo:mid]) + _dot_nt(dup, win_ref[:, mid:hi])
                if jj == 0:
                    dxn_ref[...] = part
                else:
                    h = h_ref[...]
                    r = _rstd(h)
                    xhat = h * r
                    dxn = dxn_ref[...] + part
                    dxhat = dxn * gpre_ref[...]
                    dhin_ref[...] = dh_ref[...] + r * (dxhat - xhat * jnp.mean(dxhat * xhat, axis=-1, keepdims=True))
                    dgpre_ref[...] += jnp.sum(dxn * xhat, axis=0, keepdims=True)

    row = pl.BlockSpec((TM, D), lambda i, j: (i, 0))
    wide = pl.BlockSpec((TM, 2 * FF_T), lambda i, j: (i, j))
    vec = pl.BlockSpec((1, D), lambda i, j: (0, 0))
    return pl.pallas_call(
        body, name=name, grid=(T // TM, 2),
        in_specs=[row, row, row, wide, _resident(g_pre), _resident(g_post), _resident(w_in), _resident(w_out),
                  UNREAD],
        out_specs=[row, wide, row, vec, vec],
        out_shape=[jax.ShapeDtypeStruct((T, D), BF16), jax.ShapeDtypeStruct((T, 2 * D_FF), BF16),
                   jax.ShapeDtypeStruct((T, D), F32), jax.ShapeDtypeStruct((1, D), F32),
                   jax.ShapeDtypeStruct((1, D), F32)],
        scratch_shapes=[pltpu.VMEM((TM, D), F32)],
        compiler_params=pltpu.CompilerParams(dimension_semantics=("arbitrary", "arbitrary"),
                                             vmem_limit_bytes=VMEM_LIMIT_LARGE),
    )(dh, f, h_in, gu, g_pre, g_post, w_in, w_out, after)


def mm_nt_norm_bwd(pieces, h_in, dh_out, g, name, after):
    T, D = h_in.shape

    def body(*refs):
        ab = refs[:2 * len(pieces)]
        h_ref, dh_ref, g_ref, _, o_ref, dg_ref = refs[2 * len(pieces):]
        dxn = _dot_nt(ab[0][...], ab[1][...])
        for p in range(1, len(pieces)):
            dxn += _dot_nt(ab[2 * p][...], ab[2 * p + 1][...])
        h = h_ref[...]
        r = _rstd(h)
        xhat = h * r
        dxhat = dxn * g_ref[...]
        o_ref[...] = dh_ref[...] + r * (dxhat - xhat * jnp.mean(dxhat * xhat, axis=-1, keepdims=True))

        @pl.when(pl.program_id(0) == 0)
        def _():
            dg_ref[...] = jnp.zeros_like(dg_ref)

        dg_ref[...] += jnp.sum(dxn * xhat, axis=0, keepdims=True)

    in_specs, args = [], []
    for a, w in pieces:
        in_specs += [pl.BlockSpec((TM, a.shape[1]), lambda i: (i, 0)), _resident(w)]
        args += [a, w]
    row = pl.BlockSpec((TM, D), lambda i: (i, 0))
    return pl.pallas_call(
        body, name=name, grid=(T // TM,),
        in_specs=in_specs + [row, row, _resident(g), UNREAD],
        out_specs=[row, pl.BlockSpec((1, D), lambda i: (0, 0))],
        out_shape=[jax.ShapeDtypeStruct((T, D), F32), jax.ShapeDtypeStruct((1, D), F32)],
        compiler_params=_params("arbitrary"),
    )(*args, h_in, dh_out, g, after)


def gate_merge_out_bwd(dh, f, g, w_out, gt, o_a, o_b, o_m, w_a, w_b, w_m, name, after):
    T = dh.shape[0]
    D = D_MODEL
    branch = ((o_a, w_a), (o_b, w_b), (o_m, w_m))

    def body(dh_ref, f_ref, g_ref, wo_ref, gt_ref, oa_ref, ob_ref, om_ref, wa_ref, wb_ref, wm_ref, _,
             df_ref, dg_ref, dgt_ref, dpa_ref, dpb_ref, dpm_ref, doa_ref, dob_ref, dom_ref, db_ref):
        @pl.when(pl.program_id(0) == 0)
        def _():
            db_ref[...] = jnp.zeros_like(db_ref)
            dg_ref[...] = jnp.zeros_like(dg_ref)

        df, dg = _rms_bwd(dh_ref[...], f_ref[...], g_ref[...])
        dg_ref[...] += dg
        df = df.astype(BF16)
        df_ref[...] = df
        dmf = _dot_nt(df, wo_ref[...])
        for x, (o_ref, w_ref, dp_ref, do_ref) in enumerate(((oa_ref, wa_ref, dpa_ref, doa_ref),
                                                           (ob_ref, wb_ref, dpb_ref, dob_ref),
                                                           (om_ref, wm_ref, dpm_ref, dom_ref))):
            cols = slice(x * D, (x + 1) * D)
            gx = gt_ref[:, cols].astype(F32)
            w = w_ref[...]
            dpre = dmf * _dot(o_ref[...], w) * gx * (1.0 - gx)
            dgt_ref[:, cols] = dpre.astype(BF16)
            db_ref[:, cols] += jnp.sum(dpre, axis=0, keepdims=True)
            dp = (dmf * gx).astype(BF16)
            dp_ref[...] = dp
            do_ref[...] = _dot_nt(dp, w).astype(BF16)

    def rows(width):
        return pl.BlockSpec((TM, width), lambda i: (i, 0))

    widths = [o.shape[1] for o, _ in branch]
    return pl.pallas_call(
        body, name=name, grid=(T // TM,),
        in_specs=[rows(D), rows(D), _resident(g), _resident(w_out), rows(3 * D)] + [rows(k) for k in widths]
                 + [_resident(w) for _, w in branch] + [UNREAD],
        out_specs=[rows(D), pl.BlockSpec((1, D), lambda i: (0, 0)), rows(3 * D), rows(D), rows(D), rows(D)]
                  + [rows(k) for k in widths] + [pl.BlockSpec((1, 3 * D), lambda i: (0, 0))],
        out_shape=[jax.ShapeDtypeStruct((T, D), BF16), jax.ShapeDtypeStruct((1, D), F32),
                   jax.ShapeDtypeStruct((T, 3 * D), BF16)] + [jax.ShapeDtypeStruct((T, D), BF16)] * 3
                  + [jax.ShapeDtypeStruct((T, k), BF16) for k in widths]
                  + [jax.ShapeDtypeStruct((1, 3 * D), F32)],
        compiler_params=_params("arbitrary"),
    )(dh, f, g, w_out, gt, o_a, o_b, o_m, w_a, w_b, w_m, after)


def mm_tn(x, dy, tm, tn, name, shard_major=False, perm=None, slabs=1, after=None, wire=False):
    T, M = x.shape
    N = dy.shape[1]
    tk = min(2048, T)
    perm = perm or (lambda j: j)
    w = tn // slabs

    def body(x_ref, dy_ref, *rest):
        o_ref = rest[-2] if wire else rest[-1]

        @pl.when(pl.program_id(2) == 0)
        def _():
            o_ref[...] = jnp.zeros_like(o_ref)

        acc = _dot_tn(x_ref[...], dy_ref[...])
        if shard_major:
            for s in range(slabs):
                o_ref[s] += acc[:, s * w:(s + 1) * w]
        else:
            o_ref[...] += acc
        if wire:
            @pl.when(pl.program_id(2) == T // tk - 1)
            def _():
                rest[-1][...] = o_ref[...].astype(BF16)

    if shard_major:
        out_spec = pl.BlockSpec((slabs, tm, w), lambda i, j, k: (perm(j), i, 0))
        out_shape = jax.ShapeDtypeStruct((N // w, M, w), F32)
    else:
        out_spec = pl.BlockSpec((tm, tn), lambda i, j, k: (i, j))
        out_shape = jax.ShapeDtypeStruct((M, N), F32)
    return pl.pallas_call(
        body, name=name, grid=(M // tm, N // tn, T // tk),
        in_specs=[pl.BlockSpec((tk, tm), lambda i, j, k: (k, i)),
                  pl.BlockSpec((tk, tn), lambda i, j, k: (k, j))] + ([] if after is None else [UNREAD]),
        out_specs=[out_spec, out_spec] if wire else out_spec,
        out_shape=[out_shape, jax.ShapeDtypeStruct(out_shape.shape, BF16)] if wire else out_shape,
        compiler_params=_params("parallel", "parallel", "arbitrary"),
    )(x, dy, *([] if after is None else [after]))


def rope_tables(T, zero):
    half = HEAD // 2
    inv = ROPE_THETA ** (-jnp.arange(half, dtype=F32) / half)
    ang = (jnp.arange(T).astype(F32) + zero)[:, None] * inv[None, :]
    cos, sin = jnp.cos(ang), jnp.sin(ang)
    return jnp.concatenate([cos, cos], axis=1), jnp.concatenate([-sin, sin], axis=1)


def layer_step(x, mem, target, gains, sinks, b_gate, weights_of, send_grads, zero):
    T = x.shape[0]
    cos, sin_signed = rope_tables(T, zero)
    no_sink = jnp.full((2,), NEG_INF, F32)

    xn1 = rms_scale(x, gains["ffn1_norm_pre"], "ffn1_norm", cos)
    w = dict(weights_of("ffn1_in", xn1))
    xn1, gu1, a1 = ffn_in(x, gains["ffn1_norm_pre"], w["ffn1_w_in"], "ffn1_in", xn=xn1)
    w.update(weights_of("ffn1_out", xn1))
    f1, h1 = mm_norm_res(a1, w["ffn1_w_out"], x, gains["ffn1_norm_post"], 0.5, "ffn1_out")
    w.update(weights_of("mix", f1))
    u, qkv, gt = mix_in(h1, gains["mix_norm_pre"], w["w_in"], w["w_gate"], b_gate, cos, sin_signed, "mix_in")
    outs, lses = [], []
    for gidx, (window, dil) in enumerate(DIL):
        last = gidx == len(DIL) - 1
        o_g, l_g = band_fwd(qkv, no_sink, r=dil, base=A_BASE + 6 * gidx, hkv=2, grp=1, max_dist=window // dil,
                            out_dtype=BF16 if last else F32, name=f"attn_a{gidx}_fwd",
                            merge=(outs, lses) if last else None)
        outs.append(o_g)
        lses.append(l_g)
    o_a, l_a = outs[-1], lses[-1]
    o_b, l_b = band_fwd(qkv, sinks, r=1, base=B_BASE, hkv=2, grp=2, max_dist=HEAD - 1, out_dtype=BF16,
                        name="attn_b_fwd")
    mem_n, mkv = mem_kv(mem, gains["mem_norm"], w["w_mem_kv"], "mem_kv")
    o_m, l_m = mem_fwd(qkv, mkv, "attn_m_fwd")
    merged, mo, h2 = gate_merge_out(gt, o_a, o_b, o_m, w["w_o_a"], w["w_o_b"], w["w_o_m"], w["w_out"], h1,
                                    gains["mix_norm_post"], "gate_merge_out")
    w.update(weights_of("ffn2", mo))
    xn2, gu2, a2 = ffn_in(h2, gains["ffn2_norm_pre"], w["ffn2_w_in"], "ffn2_in")
    f2, dy, sq = mm_norm_res(a2, w["ffn2_w_out"], h2, gains["ffn2_norm_post"], 0.5, "ffn2_out", target=target)

    grads = {}

    def ffn_bwd(tag, dh_out, f, gu, a, xn, h_in, after):
        df, dgu, dh_in, grads[f"{tag}_norm_pre"], grads[f"{tag}_norm_post"] = ffn_tokens_bwd(
            dh_out, f, h_in, gu, gains[f"{tag}_norm_pre"], gains[f"{tag}_norm_post"], w[f"{tag}_w_in"],
            w[f"{tag}_w_out"], 0.5, f"{tag}_tokens_bwd", after)
        sent = send_grads(f"{tag}_in", {f"{tag}_w_in": mm_tn(
            xn, dgu, D_MODEL, FF_T, f"{tag}_w_in_grad", shard_major=True, perm=_ffn_perm, wire=True)})
        sent = send_grads(f"{tag}_out", {f"{tag}_w_out": mm_tn(
            a, df, FF_T, D_MODEL, f"{tag}_w_out_grad", after=sent, wire=True)})
        return dh_in, sent

    dh2, sent = ffn_bwd("ffn2", dy, f2, gu2, a2, xn2, h2, dy)

    mix = {}
    dmo, grads["mix_norm_post"], dgt, dpa, dpb, dpm, do_a, do_b, do_m, grads["b_gate"] = gate_merge_out_bwd(
        dh2, mo, gains["mix_norm_post"], w["w_out"], gt, o_a, o_b, o_m, w["w_o_a"], w["w_o_b"], w["w_o_m"],
        "gate_merge_out_bwd", sent)
    mix["w_out"] = mm_tn(merged, dmo, D_MODEL, D_MODEL, "w_out_grad", wire=True)
    mix["w_o_a"] = mm_tn(o_a, dpa, o_a.shape[1], D_MODEL, "w_o_a_grad")
    mix["w_o_b"] = mm_tn(o_b, dpb, o_b.shape[1], D_MODEL, "w_o_b_grad")
    mix["w_o_m"] = mm_tn(o_m, dpm, o_m.shape[1], D_MODEL, "w_o_m_grad")

    dqkv = lax.empty(qkv.shape, qkv.dtype)
    for gidx, (window, dil) in enumerate(DIL):
        dqkv, = band_bwd(qkv, dqkv, do_a, o_a, l_a, cos, sin_signed, None, r=dil, base=A_BASE + 6 * gidx, hkv=2,
                         grp=1, max_dist=window // dil, name=f"attn_a{gidx}_bwd")
    dqkv, dsink = band_bwd(qkv, dqkv, do_b, o_b, l_b, cos, sin_signed, sinks, r=1, base=B_BASE, hkv=2, grp=2,
                           max_dist=HEAD - 1, name="attn_b_bwd")
    grads["sinks"] = -dsink[:, ::8, 0].reshape(1, 4)
    dqkv, dmk, dmv = mem_bwd(qkv, dqkv, mkv, do_m, o_m, l_m, "attn_m_bwd")
    mix["w_mem_kv"], grads["mem_norm"] = mem_kv_bwd(
        mem, gains["mem_norm"], mem_n, w["w_mem_kv"], jnp.concatenate([dmk, dmv], axis=1), "mem_kv_bwd")

    mix["w_in"] = mm_tn(u, dqkv, D_MODEL, 1280, "w_in_grad")
    mix["w_gate"] = mm_tn(u, dgt, D_MODEL, 1536, "w_gate_grad", shard_major=True, slabs=2, wire=True)
    sent = send_grads("mix", mix)
    dh1, grads["mix_norm_pre"] = mm_nt_norm_bwd(
        [(dqkv, w["w_in"]), (dgt, w["w_gate"])], h1, dh2, gains["mix_norm_pre"], "mix_in_bwd", sent)

    dx, _ = ffn_bwd("ffn1", dh1, f1, gu1, a1, xn1, x, dh1)
    return sq, dx, grads


def _place():
    return lax.axis_index("x"), lax.axis_index("y"), lax.axis_index("c")


def _other_chips(x, y):
    return [(1 - x, y), (x, 1 - y), (1 - x, 1 - y)]


def _hbm(n):
    return [pl.BlockSpec(memory_space=pltpu.HBM)] * n


SEM = pl.BlockSpec(memory_space=pltpu.SEMAPHORE)
SIDE_EFFECT = pltpu.SideEffectType.DATAFLOW_SIDE_EFFECTING


def _chip_copy(src, land, sems, i, j, dst_slot, scatter):
    x, y, c = _place()
    px, py = _other_chips(x, y)[j]
    send_sems, recv_sems = sems
    return pltpu.make_async_remote_copy(
        src_ref=src[i].at[2 * px + py] if scatter else src[i], dst_ref=land[i].at[dst_slot],
        send_sem=send_sems.at[3 * i + j], recv_sem=recv_sems.at[3 * i + j],
        device_id=(px, py, c), device_id_type=MESH)


def chip_copies_start(srcs, lands, groups, scatter, name, after=None):
    n = len(srcs)

    def body(*refs):
        src, land = refs[:n], refs[n:2 * n]
        first_sem = 2 * n + (after is not None)
        sems = refs[first_sem:first_sem + 2 * len(groups)]
        token = refs[-1]
        x, y, _ = _place()
        for g, members in enumerate(groups):
            part = ([src[i] for i in members], [land[i] for i in members])
            for t in range(len(members)):
                for j in range(3):
                    _chip_copy(*part, sems[2 * g:2 * g + 2], t, j, 2 * x + y, scatter).start()
        token[...] = jnp.zeros_like(token)

    sem_shapes = [pltpu.SemaphoreType.DMA((3 * len(m),)) for m in groups for _ in range(2)]
    thru = [pltpu.HBM(a.shape, a.dtype) for a in (*srcs, *lands)]
    res = pl.pallas_call(
        body, name=name,
        out_shape=(*sem_shapes, *thru, jax.ShapeDtypeStruct((8, 128), F32)),
        in_specs=_hbm(2 * n) + ([] if after is None else [UNREAD]),
        out_specs=(*[SEM] * len(sem_shapes), *_hbm(2 * n), pl.BlockSpec(memory_space=pltpu.VMEM)),
        input_output_aliases={i: len(sem_shapes) + i for i in range(2 * n)},
        compiler_params=pltpu.CompilerParams(has_side_effects=SIDE_EFFECT),
    )(*[pltpu.with_memory_space_constraint(a, pltpu.HBM) for a in (*srcs, *lands)],
      *([] if after is None else [after]))
    k = len(sem_shapes)
    sems = [tuple(res[2 * g:2 * g + 2]) for g in range(len(groups))]
    return sems, list(res[k:k + n]), list(res[k + n:k + 2 * n]), res[-1]


def chip_copies_wait(srcs, lands, sems, after, scatter, name):
    n = len(srcs)

    def body(*refs):
        src, land = refs[:n], refs[n:2 * n]
        pair = refs[2 * n:2 * n + 2]
        x, y, _ = _place()
        for i in range(n):
            for j, (px, py) in enumerate(_other_chips(x, y)):
                copy = _chip_copy(src, land, pair, i, j, 2 * px + py, scatter)
                copy.wait_send()
                copy.wait_recv()

    res = pl.pallas_call(
        body, name=name,
        out_shape=[pltpu.HBM(a.shape, a.dtype) for a in (*srcs, *lands)],
        in_specs=[*_hbm(2 * n), SEM, SEM, pl.BlockSpec(memory_space=pl.ANY)],
        out_specs=_hbm(2 * n),
        input_output_aliases={i: i for i in range(2 * n)},
        compiler_params=pltpu.CompilerParams(has_side_effects=SIDE_EFFECT),
    )(*srcs, *lands, *sems, after)
    return list(res[n:])


def small_all_gather(small, name):
    flips = [(fx, fy, fc) for fx in (0, 1) for fy in (0, 1) for fc in (0, 1)][1:]

    def body(in_ref, out_ref, send_sems, recv_sems, local_sem):
        x, y, c = _place()
        me = 4 * x + 2 * y + c

        def copy(k, slot):
            fx, fy, fc = flips[k]
            return pltpu.make_async_remote_copy(
                src_ref=in_ref, dst_ref=out_ref.at[slot], send_sem=send_sems.at[k], recv_sem=recv_sems.at[k],
                device_id=(x ^ fx, y ^ fy, c ^ fc), device_id_type=MESH)

        local = pltpu.make_async_copy(in_ref, out_ref.at[me], local_sem)
        local.start()
        for k in range(len(flips)):
            copy(k, me).start()
        for k, (fx, fy, fc) in enumerate(flips):
            copy(k, 4 * (x ^ fx) + 2 * (y ^ fy) + (c ^ fc)).wait()
        local.wait()

    return pl.pallas_call(
        body, name=name, in_specs=_hbm(1), out_specs=_hbm(1)[0],
        out_shape=jax.ShapeDtypeStruct((N_DEV,) + small.shape, small.dtype),
        scratch_shapes=[pltpu.SemaphoreType.DMA((len(flips),)), pltpu.SemaphoreType.DMA((len(flips),)),
                        pltpu.SemaphoreType.DMA],
    )(small)


def _sibling_copy(src, land, sems, i):
    x, y, c = _place()
    return pltpu.make_async_remote_copy(
        src_ref=src[i], dst_ref=land[i], send_sem=sems[0].at[i], recv_sem=sems[1].at[i],
        device_id=(x, y, 1 - c), device_id_type=MESH)


def sibling_copies_start(parts, name):
    n = len(parts)
    lands = [lax.empty(p.shape, p.dtype) for p in parts]

    def body(*refs):
        src, land, sems, token = refs[:n], refs[n:2 * n], refs[2 * n:2 * n + 2], refs[-1]
        for i in range(n):
            _sibling_copy(src, land, sems, i).start()
        token[...] = jnp.zeros_like(token)

    res = pl.pallas_call(
        body, name=name,
        out_shape=(pltpu.SemaphoreType.DMA((n,)), pltpu.SemaphoreType.DMA((n,)),
                   *[pltpu.HBM(a.shape, a.dtype) for a in (*parts, *lands)], jax.ShapeDtypeStruct((8, 128), F32)),
        in_specs=_hbm(2 * n),
        out_specs=(SEM, SEM, *_hbm(2 * n), pl.BlockSpec(memory_space=pltpu.VMEM)),
        input_output_aliases={i: 2 + i for i in range(2 * n)},
        compiler_params=pltpu.CompilerParams(has_side_effects=SIDE_EFFECT),
    )(*[pltpu.with_memory_space_constraint(a, pltpu.HBM) for a in (*parts, *lands)])
    return tuple(res[:2]), list(res[2:2 + n]), list(res[2 + n:2 + 2 * n]), res[-1]


def sibling_copies_wait(parts, lands, sems, after, name):
    n = len(parts)

    def body(*refs):
        src, land, sems = refs[:n], refs[n:2 * n], refs[2 * n:2 * n + 2]
        for i in range(n):
            copy = _sibling_copy(src, land, sems, i)
            copy.wait_send()
            copy.wait_recv()

    res = pl.pallas_call(
        body, name=name,
        out_shape=[pltpu.HBM(a.shape, a.dtype) for a in (*parts, *lands)],
        in_specs=[*_hbm(2 * n), SEM, SEM, UNREAD],
        out_specs=_hbm(2 * n),
        input_output_aliases={i: i for i in range(2 * n)},
        compiler_params=pltpu.CompilerParams(has_side_effects=SIDE_EFFECT),
    )(*parts, *lands, *sems, after)
    return list(res[n:])


def _row_tile(rows):
    for t in (256, 176, 128, 64, 32, 16, 8):
        if rows % t == 0:
            return t
    return rows


def chip_partial_sum(me, own_sm, recv, name):
    _, rows, cols = own_sm.shape
    tr = _row_tile(rows)

    def body(me_ref, own_ref, r0, r1, r2, r3, o_ref):
        acc = jnp.zeros((tr, cols), F32)
        for s, r_ref in enumerate((r0, r1, r2, r3)):
            acc = acc + jnp.where(me_ref[0] == s, own_ref[...], r_ref[...].astype(F32))
        o_ref[...] = acc

    def slot(s):
        return pl.BlockSpec((None, tr, cols), lambda i, me_ref, s=s: (s, i, 0))

    return pl.pallas_call(
        body, name=name,
        grid_spec=pltpu.PrefetchScalarGridSpec(
            num_scalar_prefetch=1, grid=(rows // tr,),
            in_specs=[pl.BlockSpec((None, tr, cols), lambda i, me_ref: (me_ref[0], i, 0))] + [slot(s) for s in range(4)],
            out_specs=pl.BlockSpec((tr, cols), lambda i, me_ref: (i, 0))),
        out_shape=jax.ShapeDtypeStruct((rows, cols), F32),
        compiler_params=_params("parallel"),
    )(me, own_sm, recv, recv, recv, recv)


def _adamw(w, g, m, v):
    m = ADAM_B1 * m + (1.0 - ADAM_B1) * g
    v = ADAM_B2 * v + (1.0 - ADAM_B2) * (g * g)
    m_hat = m / (1.0 - ADAM_B1 ** ADAM_STEP)
    v_hat = v / (1.0 - ADAM_B2 ** ADAM_STEP)
    delta = -ADAM_LR * (m_hat / (jnp.sqrt(v_hat) + ADAM_EPS) + ADAM_WD * w)
    return delta, m, v


def adamw_pair(part, sib, w, m, v, name):
    rows, cols = w.shape
    tr = _row_tile(rows)

    def body(p_ref, s_ref, w_ref, m_ref, v_ref, g_ref, d_ref, nm_ref, nv_ref):
        g = p_ref[...] + s_ref[...]
        g_ref[...] = g
        d_ref[...], nm_ref[...], nv_ref[...] = _adamw(w_ref[...], g, m_ref[...], v_ref[...])

    spec = pl.BlockSpec((tr, cols), lambda i: (i, 0))
    return pl.pallas_call(
        body, name=name, grid=(rows // tr,), in_specs=[spec] * 5, out_specs=[spec] * 4,
        out_shape=[jax.ShapeDtypeStruct((rows, cols), F32)] * 4,
        compiler_params=_params("parallel"),
    )(part, sib, w, m, v)


def adamw_small(g_all, w, m, v, name):
    def body(ga_ref, w_ref, m_ref, v_ref, g_ref, d_ref, nm_ref, nv_ref):
        g = ga_ref[0]
        for k in range(1, N_DEV):
            g = g + ga_ref[k]
        g_ref[...] = g
        d_ref[...], nm_ref[...], nv_ref[...] = _adamw(w_ref[...], g, m_ref[...], v_ref[...])

    return pl.pallas_call(
        body, name=name, out_shape=[jax.ShapeDtypeStruct(w.shape, F32)] * 4,
    )(g_all, w, m, v)


WEIGHTS = ("ffn1_norm_pre", "ffn1_w_in", "ffn1_w_out", "ffn1_norm_post", "mix_norm_pre", "w_in", "sinks",
           "mem_norm", "w_mem_kv", "w_gate", "b_gate", "w_o_a", "w_o_b", "w_o_m", "w_out", "mix_norm_post",
           "ffn2_norm_pre", "ffn2_w_in", "ffn2_w_out", "ffn2_norm_post")
BIG = ("ffn1_w_in", "ffn1_w_out", "w_in", "w_mem_kv", "w_gate", "w_o_a", "w_o_b", "w_o_m", "w_out",
       "ffn2_w_in", "ffn2_w_out")
GATHER_STAGES = (("ffn1_in", "ffn1_out"), ("mix",), ("ffn2",))
GATHER_GROUPS = {"ffn1_in": ("ffn1_w_in",), "ffn1_out": ("ffn1_w_out",),
                 "mix": ("w_in", "w_gate", "w_mem_kv", "w_o_a", "w_o_b", "w_o_m", "w_out"),
                 "ffn2": ("ffn2_w_in", "ffn2_w_out")}
GROUPS = {"ffn1_in": ("ffn1_w_in",), "ffn1_out": ("ffn1_w_out",),
          "mix": ("w_in", "w_gate", "w_mem_kv", "w_o_a", "w_o_b", "w_o_m", "w_out"),
          "ffn2_in": ("ffn2_w_in",), "ffn2_out": ("ffn2_w_out",)}
COLUMN_SHARDED = ("ffn1_w_in", "ffn2_w_in", "w_in", "w_gate", "w_o_a", "w_o_b", "w_o_m")
KEPT_SHARD_MAJOR = ("ffn1_w_in", "ffn2_w_in", "w_gate")
GAINS = ("ffn1_norm_pre", "ffn1_norm_post", "mix_norm_pre", "mem_norm", "mix_norm_post", "ffn2_norm_pre",
         "ffn2_norm_post")
SMALL_ROWS = 16


def _pack_small(t):
    sinks = jnp.pad(t["sinks"], ((0, 0), (0, D_MODEL - t["sinks"].shape[1])))
    rows = [t[k] for k in GAINS] + [t["b_gate"].reshape(3, D_MODEL), sinks]
    packed = jnp.concatenate(rows, axis=0)
    return jnp.pad(packed, ((0, SMALL_ROWS - packed.shape[0]), (0, 0)))


def _unpack_small(p):
    out = {k: p[i:i + 1] for i, k in enumerate(GAINS)}
    out["b_gate"] = p[7:10].reshape(1, 3 * D_MODEL)
    out["sinks"] = p[10:11, :4]
    return out


def kernel(x, mem, ffn1_norm_pre, ffn1_w_in, ffn1_w_out, ffn1_norm_post, mix_norm_pre, w_in, sinks, mem_norm, w_mem_kv, w_gate, b_gate, w_o_a, w_o_b, w_o_m, w_out, mix_norm_post, ffn2_norm_pre, ffn2_w_in, ffn2_w_out, ffn2_norm_post, loss_target, m_ffn1_norm_pre, m_ffn1_w_in, m_ffn1_w_out, m_ffn1_norm_post, m_mix_norm_pre, m_w_in, m_sinks, m_mem_norm, m_w_mem_kv, m_w_gate, m_b_gate, m_w_o_a, m_w_o_b, m_w_o_m, m_w_out, m_mix_norm_post, m_ffn2_norm_pre, m_ffn2_w_in, m_ffn2_w_out, m_ffn2_norm_post, v_ffn1_norm_pre, v_ffn1_w_in, v_ffn1_w_out, v_ffn1_norm_post, v_mix_norm_pre, v_w_in, v_sinks, v_mem_norm, v_w_mem_kv, v_w_gate, v_b_gate, v_w_o_a, v_w_o_b, v_w_o_m, v_w_out, v_mix_norm_post, v_ffn2_norm_pre, v_ffn2_w_in, v_ffn2_w_out, v_ffn2_norm_post):
    given = dict(locals())
    wt = {k: given[k] for k in WEIGHTS}
    mom = {k: given["m_" + k] for k in WEIGHTS}
    var = {k: given["v_" + k] for k in WEIGHTS}
    chip = (2 * lax.axis_index("x") + lax.axis_index("y")).astype(jnp.int32)
    me = chip.reshape(1)

    def landing_zone(own):
        return lax.dynamic_update_slice_in_dim(lax.empty((N_CHIPS,) + own.shape, own.dtype), own[None], chip, 0)

    started = {}
    tokens = []

    def stage_keys(stage):
        return [k for g in GATHER_STAGES[stage] for k in GATHER_GROUPS[g]]

    def prepare(stage):
        shards = [(wt[k][0] + tokens[0][0, 0] if tokens else wt[k][0]).astype(BF16) for k in stage_keys(stage)]
        return shards, [landing_zone(s) for s in shards]

    def start_gather(stage, after):
        groups, keys = GATHER_STAGES[stage], stage_keys(stage)
        members = [[keys.index(k) for k in GATHER_GROUPS[g]] for g in groups]
        sems, shards, lands, token = chip_copies_start(
            *prepared[stage], members, False, f"weight_gather_start_{stage}", after)
        tokens.append(token)
        for g, idx, pair in zip(groups, members, sems):
            started[g] = ([shards[i] for i in idx], [lands[i] for i in idx], pair)

    prepared = {0: prepare(0)}
    start_gather(0, None)
    prepared.update({stage: prepare(stage) for stage in range(1, len(GATHER_STAGES))})

    def weights_of(group, after):
        got = chip_copies_wait(*started[group], after, False, f"weight_gather_wait_{group}")
        stage = [s + 1 for s, groups in enumerate(GATHER_STAGES[:-1]) if groups[0] == group]
        if stage:
            start_gather(stage[0], got[0])
        full = {}
        for k, g in zip(GATHER_GROUPS[group], got):
            if k in COLUMN_SHARDED:
                if k in ("ffn1_w_in", "ffn2_w_in"):
                    g = jnp.stack([g[0], g[2], g[1], g[3]])
                full[k] = jnp.swapaxes(g, 0, 1).reshape(g.shape[1], N_CHIPS * g.shape[2])
                if k == "w_in":
                    full[k] = to_kernel_heads(full[k])
            else:
                full[k] = g.reshape(N_CHIPS * g.shape[1], g.shape[2])
        return full

    in_flight = {}

    def send_grads(group, grads):
        def shard_major(k, g):
            if k in KEPT_SHARD_MAJOR:
                return g
            if k in COLUMN_SHARDED:
                return jnp.swapaxes(g.reshape(g.shape[0], N_CHIPS, g.shape[1] // N_CHIPS), 0, 1)
            return g.reshape(N_CHIPS, g.shape[0] // N_CHIPS, g.shape[1])

        own, wire = [], []
        for k in GROUPS[group]:
            g, rounded = grads[k] if isinstance(grads[k], (tuple, list)) else (grads[k], None)
            g = shard_major(k, from_kernel_heads(g) if k == "w_in" else g)
            own.append(g)
            wire.append(g.astype(BF16) if rounded is None else shard_major(k, rounded))
        zones = [landing_zone(lax.dynamic_index_in_dim(b, chip, 0, keepdims=False)) for b in wire]
        pair, wire, zones, sent = chip_copies_start(
            wire, zones, [list(range(len(wire)))], True, f"grad_scatter_start_{group}")
        in_flight[group] = (own, wire, zones, pair[0], sent)
        return sent

    gains = {k: wt[k] for k in GAINS}
    sq, dx, grads = layer_step(
        x[0], mem[0], loss_target[0], gains, sinks[0], b_gate, weights_of, send_grads, tokens[0][0, 0])
    loss = lax.psum(0.5 * sq[0, 0] / D_MODEL, ("x", "y", "c"))

    res = {}
    after = in_flight["ffn1_out"][4]
    swaps = []
    for stage in (("ffn2_in", "ffn2_out", "mix", "ffn1_in"), ("ffn1_out",)):
        names, parts = [], []
        for group in stage:
            own, wire, zones, pair, _ = in_flight[group]
            received = chip_copies_wait(wire, zones, pair, after, True, f"grad_scatter_wait_{group}")
            for k, g, r in zip(GROUPS[group], own, received):
                names.append(k)
                parts.append(chip_partial_sum(me, g, r, f"{k}_chip_sum"))
        pair, parts, lands, after = sibling_copies_start(parts, f"sibling_start_{stage[-1]}")
        swaps.append((stage[-1], names, parts, lands, pair))
    small_all = small_all_gather(_pack_small(grads), "small_grad_gather")
    packed = adamw_small(small_all, _pack_small(wt), _pack_small(mom), _pack_small(var), "small_adamw")
    after = packed[0]
    for tag, names, parts, lands, pair in swaps:
        sibs = sibling_copies_wait(parts, lands, pair, after, f"sibling_wait_{tag}")
        for k, p, s in zip(names, parts, sibs):
            res[k] = [t[None] for t in adamw_pair(p, s, wt[k][0], mom[k][0], var[k][0], f"{k}_adamw")]
        after = res[names[-1]][0]
    for idx, p in enumerate(packed):
        for k, t in _unpack_small(p).items():
            res.setdefault(k, [None] * 4)[idx] = t

    return (loss, dx[None], *[res[k][0] for k in WEIGHTS], *[res[k][1] for k in WEIGHTS],
            *[res[k][2] for k in WEIGHTS], *[res[k][3] for k in WEIGHTS])
```

```python
import functools

import jax
import jax.numpy as jnp
from jax import lax
from jax.experimental import pallas as pl
from jax.experimental.pallas import tpu as pltpu

F32 = jnp.float32
BF16 = jnp.bfloat16

D_MODEL = 1024
D_FF = 2816
HEAD = 128
N_CHIPS = 4
N_DEV = 8
EPS = 1e-6
NEG_INF = -1e30
ROPE_THETA = 10000.0
ATT_SCALE = HEAD ** -0.5

ADAM_LR = 0.001
ADAM_B1 = 0.9
ADAM_B2 = 0.999
ADAM_EPS = 1e-08
ADAM_WD = 0.01
ADAM_STEP = 10

VMEM_LIMIT = 52 * 2 ** 20
VMEM_LIMIT_LARGE = 60 * 2 ** 20
MESH = pl.DeviceIdType.MESH

QKV_W = 3840
DIL = ((128, 1), (512, 4), (2048, 16))
B_BASE, MQ, A_BASE = 0, 8, 12
_AQ, _AK, _AV, _BQ, _BK, _BV, _MQ = 0, 6, 12, 18, 22, 24, 26
HEAD_ORDER = tuple(
    [h for j in range(2) for h in (_BQ + 2 * j, _BQ + 2 * j + 1, _BK + j, _BV + j)]
    + [_MQ + i for i in range(4)]
    + [h for g in range(3) for i in range(2) for h in (_AQ + 2 * g + i, _AK + 2 * g + i, _AV + 2 * g + i)])
ROTARY_HEADS = tuple(p for p, h in enumerate(HEAD_ORDER) if h < _AV or _BQ <= h < _BV)


def to_kernel_heads(w):
    return jnp.concatenate([w[..., h * HEAD:(h + 1) * HEAD] for h in HEAD_ORDER], axis=-1)


def from_kernel_heads(w):
    place = {h: p for p, h in enumerate(HEAD_ORDER)}
    return jnp.concatenate([w[..., place[h] * HEAD:(place[h] + 1) * HEAD] for h in range(len(HEAD_ORDER))], axis=-1)

TM = 512
FF_T = D_FF // 2


def _params(*sem):
    return pltpu.CompilerParams(dimension_semantics=sem, vmem_limit_bytes=VMEM_LIMIT)


def _dot(a, b):
    return jnp.dot(a, b, preferred_element_type=F32)


def _dot_nt(a, b):
    return lax.dot_general(a, b, (((1,), (1,)), ((), ())), preferred_element_type=F32)


def _dot_tn(a, b):
    return lax.dot_general(a, b, (((0,), (0,)), ((), ())), preferred_element_type=F32)


def _rstd(x):
    return lax.rsqrt(jnp.mean(x * x, axis=-1, keepdims=True) + EPS)


def _sigmoid(x):
    return 0.5 * jnp.tanh(0.5 * x) + 0.5


def _ffn_perm(k):
    return (k % 2) * 2 + k // 2


UNREAD = pl.BlockSpec(memory_space=pl.ANY)


def _resident(arr):
    return pl.BlockSpec(arr.shape, lambda *_: (0,) * arr.ndim, pipeline_mode=pl.Buffered(1))


def rms_scale(x, g, name, after):
    T, D = x.shape
    tm = 1024

    def body(x_ref, g_ref, _, o_ref):
        v = x_ref[...]
        o_ref[...] = (v * _rstd(v) * g_ref[...]).astype(BF16)

    spec = pl.BlockSpec((tm, D), lambda i: (i, 0))
    return pl.pallas_call(
        body, name=name, grid=(T // tm,), in_specs=[spec, _resident(g), UNREAD], out_specs=spec,
        out_shape=jax.ShapeDtypeStruct((T, D), BF16), compiler_params=_params("parallel"),
    )(x, g, after)


def ffn_in(h, g, w, name, xn=None):
    T, D = h.shape
    normed = xn is not None

    def body(h_ref, g_ref, w_ref, *outs):
        if normed:
            xn, (gu_ref, a_ref) = h_ref[...], outs
        else:
            xn_ref, gu_ref, a_ref = outs
            x = h_ref[...]
            xn = (x * _rstd(x) * g_ref[...]).astype(BF16)
            xn_ref[...] = xn
        for j in range(2):
            gu = _dot(xn, w_ref[:, j * 2 * FF_T:(j + 1) * 2 * FF_T])
            gu_ref[:, j * 2 * FF_T:(j + 1) * 2 * FF_T] = gu.astype(BF16)
            gate, up = gu[:, :FF_T], gu[:, FF_T:]
            a_ref[:, j * FF_T:(j + 1) * FF_T] = (gate * _sigmoid(gate) * up).astype(BF16)

    def rows(width):
        return pl.BlockSpec((TM, width), lambda i: (i, 0))

    res = pl.pallas_call(
        body, name=name,
        grid=(T // TM,),
        in_specs=[rows(D), _resident(g), _resident(w)],
        out_specs=[rows(D)] * (not normed) + [rows(2 * D_FF), rows(D_FF)],
        out_shape=[jax.ShapeDtypeStruct((T, D), BF16)] * (not normed)
                  + [jax.ShapeDtypeStruct((T, 2 * D_FF), BF16), jax.ShapeDtypeStruct((T, D_FF), BF16)],
        compiler_params=_params("parallel"),
    )(xn if normed else h, g, w)
    return (xn, *res) if normed else tuple(res)


def mm_norm_res(a, w, h_in, g, coef, name, target=None):
    T, K = a.shape
    D = w.shape[1]
    final = target is not None

    def body(*refs):
        if final:
            a_ref, w_ref, h_ref, g_ref, t_ref, f_ref, o_ref, l_ref = refs
        else:
            a_ref, w_ref, h_ref, g_ref, f_ref, o_ref = refs
        f = _dot(a_ref[...], w_ref[...])
        f_ref[...] = f
        y = h_ref[...] + coef * (f * _rstd(f) * g_ref[...])
        if final:
            err = y - t_ref[...]
            o_ref[...] = err * (1.0 / D)

            @pl.when(pl.program_id(0) == 0)
            def _():
                l_ref[...] = jnp.zeros_like(l_ref)

            l_ref[...] += jnp.sum(err * err)
        else:
            o_ref[...] = y

    row = pl.BlockSpec((TM, D), lambda i: (i, 0))
    in_specs = [pl.BlockSpec((TM, K), lambda i: (i, 0)),
                _resident(w),
                row, pl.BlockSpec((1, D), lambda i: (0, 0))]
    out_specs = [row, row]
    out_shape = [jax.ShapeDtypeStruct((T, D), F32), jax.ShapeDtypeStruct((T, D), F32)]
    args = [a, w, h_in, g]
    if final:
        in_specs.append(row)
        args.append(target)
        out_specs.append(pl.BlockSpec((8, 128), lambda i: (0, 0)))
        out_shape.append(jax.ShapeDtypeStruct((8, 128), F32))
    return pl.pallas_call(
        body, name=name, grid=(T // TM,), in_specs=in_specs, out_specs=out_specs, out_shape=out_shape,
        compiler_params=_params("arbitrary"),
    )(*args)


def _rope(x, cos, sin_signed):
    return x * cos + pltpu.roll(x, HEAD // 2, axis=1) * sin_signed


def _unrope(x, cos, sin_signed):
    return x * cos - pltpu.roll(x, HEAD // 2, axis=1) * sin_signed


def mix_in(h, g, w, w_gate, b_gate, cos, sin_signed, name):
    T, D = h.shape
    tn = 768

    def body(h_ref, g_ref, w_ref, wg_ref, b_ref, c_ref, s_ref, u_ref, o_ref, gt_ref):
        x = h_ref[...]
        u = (x * _rstd(x) * g_ref[...]).astype(BF16)
        u_ref[...] = u
        c, s = c_ref[...], s_ref[...]
        for j in range(QKV_W // tn):
            acc = _dot(u, w_ref[:, j * tn:(j + 1) * tn])
            for hd in range(tn // HEAD):
                head = j * (tn // HEAD) + hd
                part = acc[:, hd * HEAD:(hd + 1) * HEAD]
                if head in ROTARY_HEADS:
                    part = _rope(part, c, s)
                o_ref[:, head * HEAD:(head + 1) * HEAD] = part.astype(BF16)
        for j in range(w_gate.shape[1] // tn):
            cols = slice(j * tn, (j + 1) * tn)
            gt_ref[:, cols] = _sigmoid(_dot(u, wg_ref[:, cols]) + b_ref[:, cols]).astype(BF16)

    def rows(width):
        return pl.BlockSpec((TM, width), lambda i: (i, 0))

    return pl.pallas_call(
        body, name=name,
        grid=(T // TM,),
        in_specs=[rows(D), _resident(g), _resident(w), _resident(w_gate), _resident(b_gate), rows(HEAD), rows(HEAD)],
        out_specs=[rows(D), rows(QKV_W), rows(w_gate.shape[1])],
        out_shape=[jax.ShapeDtypeStruct((T, D), BF16), jax.ShapeDtypeStruct((T, QKV_W), BF16),
                   jax.ShapeDtypeStruct((T, w_gate.shape[1]), BF16)],
        compiler_params=_params("parallel"),
    )(h, g, w, w_gate, b_gate, cos, sin_signed)


def gate_merge_out(gt, o_a, o_b, o_m, w_a, w_b, w_m, w_out, h_in, g, name):
    T = gt.shape[0]
    D = D_MODEL

    def body(gt_ref, oa_ref, ob_ref, om_ref, wa_ref, wb_ref, wm_ref, wo_ref, h_ref, g_ref, m_ref, f_ref, o_ref):
        acc = gt_ref[:, :D].astype(F32) * _dot(oa_ref[...], wa_ref[...])
        acc += gt_ref[:, D:2 * D].astype(F32) * _dot(ob_ref[...], wb_ref[...])
        acc += gt_ref[:, 2 * D:].astype(F32) * _dot(om_ref[...], wm_ref[...])
        merged = acc.astype(BF16)
        m_ref[...] = merged
        f = _dot(merged, wo_ref[...])
        f_ref[...] = f
        o_ref[...] = h_ref[...] + f * _rstd(f) * g_ref[...]

    def rows(width):
        return pl.BlockSpec((TM, width), lambda i: (i, 0))

    return pl.pallas_call(
        body, name=name, grid=(T // TM,),
        in_specs=[rows(3 * D), rows(o_a.shape[1]), rows(o_b.shape[1]), rows(o_m.shape[1]),
                  _resident(w_a), _resident(w_b), _resident(w_m), _resident(w_out), rows(D), _resident(g)],
        out_specs=[rows(D), rows(D), rows(D)],
        out_shape=[jax.ShapeDtypeStruct((T, D), BF16), jax.ShapeDtypeStruct((T, D), F32),
                   jax.ShapeDtypeStruct((T, D), F32)],
        compiler_params=_params("parallel"),
    )(gt, o_a, o_b, o_m, w_a, w_b, w_m, w_out, h_in, g)


def _band_rows(start, r):
    return pl.ds(start, HEAD) if r == 1 else pl.ds(start, HEAD, stride=r)


def _band_mask(max_dist, first_has_prev):
    row = lax.broadcasted_iota(jnp.int32, (HEAD, 2 * HEAD), 0)
    col = lax.broadcasted_iota(jnp.int32, (HEAD, 2 * HEAD), 1)
    dist = row + HEAD - col
    band = (dist >= 0) & (dist <= max_dist)
    return band, band & (col >= jnp.where(first_has_prev, 0, HEAD))


def _stack(parts):
    return parts[0] if len(parts) == 1 else jnp.concatenate(parts, axis=0)


def _band_specs(BT, SB, nsub, base, grp):
    stride = grp + 2

    def cur(off, width):
        return pl.BlockSpec((BT, width * HEAD), lambda h, i: (i, (base + h * stride + off) // width))

    def prev(off):
        return pl.BlockSpec((SB, HEAD), lambda h, i: (jnp.maximum(i * nsub - 1, 0), base + h * stride + off))

    return cur(0, grp), cur(grp, 1), prev(grp), cur(grp + 1, 1), prev(grp + 1)


def band_fwd(qkv, sinks, *, r, base, hkv, grp, max_dist, out_dtype, name, merge=None):
    T, W = qkv.shape
    SB = HEAD * r
    BT = min(2048, T)
    nsub, nib = BT // SB, T // BT
    hq = hkv * grp
    heads = [slice(g * HEAD, (g + 1) * HEAD) for g in range(grp)]
    others = [] if merge is None else [*merge[0], *merge[1]]

    def body(sink_ref, q_ref, kc_ref, kp_ref, vc_ref, vp_ref, *rest):
        joint_o, joint_l = rest[len(others):len(others) + 2]
        qf, kf, vf = rest[len(others) + 2:len(others) + 5]
        o_ref, l_ref = rest[len(others) + 5:] if others else (joint_o, joint_l)
        kvh, ib = pl.program_id(0), pl.program_id(1)
        qf[...] = q_ref[...].astype(F32)
        kf[:SB] = kp_ref[...].astype(F32)
        kf[SB:] = kc_ref[...].astype(F32)
        vf[:SB] = vp_ref[...].astype(F32)
        vf[SB:] = vc_ref[...].astype(F32)
        band, band_first = _band_mask(max_dist, ib > 0)
        for c in range(r):
            k_old, v_old = kf[_band_rows(c, r)], vf[_band_rows(c, r)]
            for j in range(nsub):
                mask = band_first if j == 0 else band
                rows = _band_rows(j * SB + c, r)
                k_own, v_own = kf[_band_rows((j + 1) * SB + c, r)], vf[_band_rows((j + 1) * SB + c, r)]
                kcat = jnp.concatenate([k_old, k_own], axis=0).astype(BF16)
                vcat = jnp.concatenate([v_old, v_own], axis=0).astype(BF16)
                k_old, v_old = k_own, v_own
                s_all = _dot_nt(_stack([qf[rows, cols] for cols in heads]).astype(BF16), kcat) * ATT_SCALE
                probs, tots = [], []
                for g, cols in enumerate(heads):
                    s = jnp.where(mask, s_all[cols], NEG_INF)
                    sk = sink_ref[kvh * grp + g]
                    m = jnp.maximum(jnp.max(s, axis=-1, keepdims=True), sk)
                    p = jnp.exp(s - m)
                    tot = jnp.sum(p, axis=-1, keepdims=True) + jnp.exp(sk - m)
                    probs.append(p.astype(BF16))
                    tots.append(tot)
                    l_ref[rows, cols] = jnp.broadcast_to(m + jnp.log(tot), (HEAD, HEAD))
                o_all = _dot(_stack(probs), vcat)
                for g, cols in enumerate(heads):
                    o_ref[rows, cols] = (o_all[cols] / tots[g]).astype(o_ref.dtype)

        if others:
            half = len(others) // 2
            outs = [ref[...] for ref in rest[:half]] + [o_ref[...]]
            logs = [ref[...] for ref in rest[half:len(others)]] + [l_ref[...]]
            top = functools.reduce(jnp.maximum, logs)
            weights = [jnp.exp(lg - top) for lg in logs]
            total = functools.reduce(jnp.add, weights)
            mixed = functools.reduce(jnp.add, [wgt * out for wgt, out in zip(weights, outs)])
            joint_o[...] = (mixed / total).astype(out_dtype)
            joint_l[...] = top + jnp.log(total)

    out_spec = pl.BlockSpec((BT, grp * HEAD), lambda h, i: (i, h))
    own = [pltpu.VMEM((BT, grp * HEAD), F32)] * 2 if others else []
    return pl.pallas_call(
        body, name=name, grid=(hkv, nib),
        in_specs=[pl.BlockSpec(memory_space=pltpu.SMEM), *_band_specs(BT, SB, nsub, base, grp)]
                 + [out_spec] * len(others),
        out_specs=[out_spec, out_spec],
        out_shape=[jax.ShapeDtypeStruct((T, hq * HEAD), out_dtype), jax.ShapeDtypeStruct((T, hq * HEAD), F32)],
        scratch_shapes=[pltpu.VMEM((BT, grp * HEAD), F32), pltpu.VMEM((SB + BT, HEAD), F32),
                        pltpu.VMEM((SB + BT, HEAD), F32)] + own,
        compiler_params=_params("parallel", "arbitrary"),
    )(sinks, qkv, qkv, qkv, qkv, qkv, *others)


def band_bwd(qkv, dqkv, do, o, lse, cos, sin_signed, sinks, *, r, base, hkv, grp, max_dist, name):
    T, W = qkv.shape
    SB = HEAD * r
    BT = min(max(2048, 2 * SB), T)
    nsub, nib = BT // SB, T // BT
    nblk = T // SB
    with_sink = sinks is not None
    heads = [slice(g * HEAD, (g + 1) * HEAD) for g in range(grp)]

    def body(*refs):
        if with_sink:
            sink_ref, refs = refs[0], refs[1:]
        (q_ref, kc_ref, kp_ref, vc_ref, vp_ref, qn_ref, do_ref, don_ref, o_ref, on_ref, l_ref, ln_ref,
         c_ref, s_ref, _) = refs[:15]
        out_ref = refs[15]
        ds_ref = refs[16] if with_sink else None
        qf, dof, of, kf, vf, dqf, dkf, dvf = refs[-8:]
        kvh, ib = pl.program_id(0), pl.program_id(1)
        for buf, cur_ref, nxt_ref in ((qf, q_ref, qn_ref), (dof, do_ref, don_ref), (of, o_ref, on_ref)):
            buf[:BT] = cur_ref[...].astype(F32)
            buf[BT:] = nxt_ref[...].astype(F32)
        kf[:SB] = kp_ref[...].astype(F32)
        kf[SB:] = kc_ref[...].astype(F32)
        vf[:SB] = vp_ref[...].astype(F32)
        vf[SB:] = vc_ref[...].astype(F32)
        band, band_first = _band_mask(max_dist, ib > 0)
        if with_sink:
            @pl.when(ib == 0)
            def _():
                ds_ref[...] = jnp.zeros_like(ds_ref)

        def grads(rows, logzs, keys, vals, mask):
            q = _stack([qf[rows, cols] for cols in heads]).astype(BF16)
            dout = _stack([dof[rows, cols] for cols in heads]).astype(BF16)
            s_all = _dot_nt(q, keys) * ATT_SCALE
            dp_all = _dot_nt(dout, vals)
            probs, dss, deltas = [], [], []
            for g, cols in enumerate(heads):
                delta = jnp.sum(dof[rows, cols] * of[rows, cols], axis=-1, keepdims=True)
                p = jnp.exp(jnp.where(mask, s_all[cols], NEG_INF) - logzs[g][:, :1])
                probs.append(p.astype(BF16))
                dss.append((p * (dp_all[cols] - delta) * ATT_SCALE).astype(BF16))
                deltas.append(delta)
            return q, dout, _stack(probs), _stack(dss), deltas

        row = lax.broadcasted_iota(jnp.int32, (HEAD, HEAD), 0)
        col = lax.broadcasted_iota(jnp.int32, (HEAD, HEAD), 1)
        reach = col >= row + jnp.where(ib < nib - 1, HEAD - max_dist, 2 * HEAD)
        for c in range(r):
            k_old, v_old = kf[_band_rows(c, r)], vf[_band_rows(c, r)]
            dk_own = dv_own = None
            for j in range(nsub):
                rows = _band_rows(j * SB + c, r)
                k_own, v_own = kf[_band_rows((j + 1) * SB + c, r)], vf[_band_rows((j + 1) * SB + c, r)]
                kcat = jnp.concatenate([k_old, k_own], axis=0).astype(BF16)
                vcat = jnp.concatenate([v_old, v_own], axis=0).astype(BF16)
                logzs = [l_ref[rows, cols] for cols in heads]
                q, dout, p, ds, deltas = grads(rows, logzs, kcat, vcat, band_first if j == 0 else band)
                dq = _dot(ds, kcat)
                for g, cols in enumerate(heads):
                    dqf[rows, cols] = dq[cols]
                    if with_sink:
                        p_sink = jnp.exp(sink_ref[kvh * grp + g] - logzs[g][:, :1])
                        ds_ref[g * 8:(g + 1) * 8] += jnp.sum(p_sink * deltas[g])
                dk, dv = _dot_tn(ds, q), _dot_tn(p, dout)
                if j > 0:
                    done = _band_rows((j - 1) * SB + c, r)
                    dkf[done] = dk_own + dk[:HEAD]
                    dvf[done] = dv_own + dv[:HEAD]
                dk_own, dv_own = dk[HEAD:], dv[HEAD:]
                k_old, v_old = k_own, v_own
            logzs = [ln_ref[_band_rows(c, r), cols] for cols in heads]
            q, dout, p, ds, _ = grads(_band_rows(BT + c, r), logzs, k_old.astype(BF16), v_old.astype(BF16), reach)
            done = _band_rows((nsub - 1) * SB + c, r)
            dkf[done] = dk_own + _dot_tn(ds, q)
            dvf[done] = dv_own + _dot_tn(p, dout)

        cs, sn = c_ref[...], s_ref[...]
        for cols in heads:
            out_ref[:, cols] = _unrope(dqf[:, cols], cs, sn).astype(BF16)
        out_ref[:, grp * HEAD:(grp + 1) * HEAD] = _unrope(dkf[...], cs, sn).astype(BF16)
        out_ref[:, (grp + 1) * HEAD:] = dvf[...].astype(BF16)

    def nxt_row(i):
        return jnp.minimum((i + 1) * nsub, nblk - 1)

    stride = grp + 2
    q_next = pl.BlockSpec((SB, grp * HEAD), lambda h, i: (nxt_row(i), (base + h * stride) // grp))
    head_cur = pl.BlockSpec((BT, grp * HEAD), lambda h, i: (i, h))
    head_next = pl.BlockSpec((SB, grp * HEAD), lambda h, i: (nxt_row(i), h))
    table = pl.BlockSpec((BT, HEAD), lambda h, i: (i, 0))

    in_specs = [*_band_specs(BT, SB, nsub, base, grp), q_next,
                head_cur, head_next, head_cur, head_next, head_cur, head_next, table, table, UNREAD]
    args = [qkv, qkv, qkv, qkv, qkv, qkv, do, do, o, o, lse, lse, cos, sin_signed, dqkv]
    out_specs = [pl.BlockSpec((BT, stride * HEAD), lambda h, i: (i, base // stride + h))]
    out_shape = [jax.ShapeDtypeStruct(dqkv.shape, dqkv.dtype)]
    if with_sink:
        in_specs.insert(0, pl.BlockSpec(memory_space=pltpu.SMEM))
        args.insert(0, sinks)
        out_specs.append(pl.BlockSpec((None, grp * 8, HEAD), lambda h, i: (h, 0, 0)))
        out_shape.append(jax.ShapeDtypeStruct((hkv, grp * 8, HEAD), F32))
    wide = pltpu.VMEM((BT + SB, grp * HEAD), F32)
    tall = pltpu.VMEM((SB + BT, HEAD), F32)
    grad = pltpu.VMEM((BT, HEAD), F32)
    return pl.pallas_call(
        body, name=name, grid=(hkv, nib), in_specs=in_specs, out_specs=out_specs, out_shape=out_shape,
        input_output_aliases={len(args) - 1: 0},
        scratch_shapes=[wide, wide, wide, tall, tall, pltpu.VMEM((BT, grp * HEAD), F32), grad, grad],
        compiler_params=pltpu.CompilerParams(dimension_semantics=("parallel", "arbitrary"),
                                             vmem_limit_bytes=VMEM_LIMIT_LARGE),
    )(*args)


M_HEADS = 4


def mem_kv(mem, g, w, name):
    n, D = mem.shape

    def body(m_ref, g_ref, w_ref, mn_ref, kv_ref):
        x = m_ref[...]
        mn = (x * _rstd(x) * g_ref[...]).astype(BF16)
        mn_ref[...] = mn
        kv_ref[...] = _dot(mn, w_ref[...]).astype(BF16)

    return pl.pallas_call(
        body, name=name,
        out_shape=[jax.ShapeDtypeStruct((n, D), BF16), jax.ShapeDtypeStruct((n, w.shape[1]), BF16)],
        compiler_params=pltpu.CompilerParams(vmem_limit_bytes=VMEM_LIMIT),
    )(mem, g, w)


def mem_fwd(qkv, mkv, name):
    T = qkv.shape[0]
    n = mkv.shape[0]
    RB = 1024

    def body(q_ref, kv_ref, o_ref, l_ref):
        for h in range(M_HEADS):
            cols = slice(h * HEAD, (h + 1) * HEAD)
            s = _dot_nt(q_ref[:, cols], kv_ref[:, cols]) * ATT_SCALE
            m = jnp.max(s, axis=-1, keepdims=True)
            p = jnp.exp(s - m)
            den = jnp.sum(p, axis=-1, keepdims=True)
            vals = kv_ref[:, (M_HEADS + h) * HEAD:(M_HEADS + h + 1) * HEAD]
            o_ref[:, cols] = (_dot(p.astype(BF16), vals) / den).astype(BF16)
            l_ref[:, cols] = jnp.broadcast_to(m + jnp.log(den), (RB, HEAD))

    out = pl.BlockSpec((RB, M_HEADS * HEAD), lambda i: (i, 0))
    return pl.pallas_call(
        body, name=name, grid=(T // RB,),
        in_specs=[pl.BlockSpec((RB, M_HEADS * HEAD), lambda i: (i, MQ // M_HEADS)), _resident(mkv)],
        out_specs=[out, out],
        out_shape=[jax.ShapeDtypeStruct((T, M_HEADS * HEAD), BF16), jax.ShapeDtypeStruct((T, M_HEADS * HEAD), F32)],
        compiler_params=_params("parallel"),
    )(qkv, mkv)


def mem_bwd(qkv, dqkv, mkv, do, o, lse, name):
    T = qkv.shape[0]
    n = mkv.shape[0]
    RB = 1024

    def body(q_ref, kv_ref, do_ref, o_ref, l_ref, _, dq_ref, dk_ref, dv_ref):
        @pl.when(pl.program_id(0) == 0)
        def _():
            dk_ref[...] = jnp.zeros_like(dk_ref)
            dv_ref[...] = jnp.zeros_like(dv_ref)

        for h in range(M_HEADS):
            cols = slice(h * HEAD, (h + 1) * HEAD)
            keys, vals = kv_ref[:, cols], kv_ref[:, (M_HEADS + h) * HEAD:(M_HEADS + h + 1) * HEAD]
            q, dout = q_ref[:, cols], do_ref[:, cols]
            delta = jnp.sum(dout.astype(F32) * o_ref[:, cols].astype(F32), axis=-1, keepdims=True)
            p = jnp.exp(_dot_nt(q, keys) * ATT_SCALE - l_ref[:, cols][:, :1])
            ds = (p * (_dot_nt(dout, vals) - delta) * ATT_SCALE).astype(BF16)
            dq_ref[:, cols] = _dot(ds, keys).astype(BF16)
            dk_ref[:, cols] += _dot_tn(ds, q)
            dv_ref[:, cols] += _dot_tn(p.astype(BF16), dout)

    wide = M_HEADS * HEAD
    tok = pl.BlockSpec((RB, wide), lambda i: (i, 0))
    q_cols = pl.BlockSpec((RB, wide), lambda i: (i, MQ // M_HEADS))
    slot = pl.BlockSpec((n, wide), lambda i: (0, 0))
    return pl.pallas_call(
        body, name=name, grid=(T // RB,),
        in_specs=[q_cols, _resident(mkv), tok, tok, tok, UNREAD],
        out_specs=[q_cols, slot, slot],
        out_shape=[jax.ShapeDtypeStruct(dqkv.shape, dqkv.dtype),
                   jax.ShapeDtypeStruct((n, wide), F32), jax.ShapeDtypeStruct((n, wide), F32)],
        input_output_aliases={5: 0},
        compiler_params=_params("arbitrary"),
    )(qkv, mkv, do, o, lse, dqkv)


def mem_kv_bwd(mem, g, mem_n, w, dmkv, name):
    n, D = mem.shape

    def body(m_ref, g_ref, mn_ref, w_ref, d_ref, dw_ref, dg_ref):
        d = d_ref[...].astype(BF16)
        dw_ref[...] = _dot_tn(mn_ref[...], d)
        x = m_ref[...]
        dg_ref[...] = jnp.sum(_dot_nt(d, w_ref[...]) * (x * _rstd(x)), axis=0, keepdims=True)

    return pl.pallas_call(
        body, name=name,
        out_shape=[jax.ShapeDtypeStruct(w.shape, F32), jax.ShapeDtypeStruct((1, D), F32)],
        compiler_params=pltpu.CompilerParams(vmem_limit_bytes=VMEM_LIMIT),
    )(mem, g, mem_n, w, dmkv)


def _rms_bwd(dn, f, g):
    r = _rstd(f)
    fhat = f * r
    dfhat = dn * g
    df = r * (dfhat - fhat * jnp.mean(dfhat * fhat, axis=-1, keepdims=True))
    return df, jnp.sum(dn * fhat, axis=0, keepdims=True)


def ffn_tokens_bwd(dh, f, h_in, gu, g_pre, g_post, w_in, w_out, coef, name, after):
    T, D = dh.shape

    def body(dh_ref, f_ref, h_ref, gu_ref, gpre_ref, gpost_ref, win_ref, wout_ref, _,
             df_ref, dgu_ref, dhin_ref, dgpre_ref, dgpost_ref, dxn_ref):
        i, j = pl.program_id(0), pl.program_id(1)

        @pl.when(j == 0)
        def _():
            @pl.when(i == 0)
            def _():
                dgpre_ref[...] = jnp.zeros_like(dgpre_ref)
                dgpost_ref[...] = jnp.zeros_like(dgpost_ref)

            df, dg_post = _rms_bwd(coef * dh_ref[...], f_ref[...], gpost_ref[...])
            dgpost_ref[...] += dg_post
            df_ref[...] = df.astype(BF16)

        for jj in range(2):
            @pl.when(j == jj)
            def _(jj=jj):
                lo, mid, hi = 2 * jj * FF_T, (2 * jj + 1) * FF_T, (2 * jj + 2) * FF_T
                da = _dot_nt(df_ref[...], wout_ref[jj * FF_T:(jj + 1) * FF_T, :])
                gate = gu_ref[:, :FF_T].astype(F32)
                up = gu_ref[:, FF_T:].astype(F32)
                sig = _sigmoid(gate)
                dgate = (da * up * sig * (1.0 + gate * (1.0 - sig))).astype(BF16)
                dup = (da * gate * sig).astype(BF16)
                dgu_ref[:, :FF_T] = dgate
                dgu_ref[:, FF_T:] = dup
                part = _dot_nt(dgate, win_ref[:, lo:mid]) + _dot_nt(dup, win_ref[:, mid:hi])
                if jj == 0:
                    dxn_ref[...] = part
                else:
                    h = h_ref[...]
                    r = _rstd(h)
                    xhat = h * r
                    dxn = dxn_ref[...] + part
                    dxhat = dxn * gpre_ref[...]
                    dhin_ref[...] = dh_ref[...] + r * (dxhat - xhat * jnp.mean(dxhat * xhat, axis=-1, keepdims=True))
                    dgpre_ref[...] += jnp.sum(dxn * xhat, axis=0, keepdims=True)

    row = pl.BlockSpec((TM, D), lambda i, j: (i, 0))
    wide = pl.BlockSpec((TM, 2 * FF_T), lambda i, j: (i, j))
    vec = pl.BlockSpec((1, D), lambda i, j: (0, 0))
    return pl.pallas_call(
        body, name=name, grid=(T // TM, 2),
        in_specs=[row, row, row, wide, _resident(g_pre), _resident(g_post), _resident(w_in), _resident(w_out),
                  UNREAD],
        out_specs=[row, wide, row, vec, vec],
        out_shape=[jax.ShapeDtypeStruct((T, D), BF16), jax.ShapeDtypeStruct((T, 2 * D_FF), BF16),
                   jax.ShapeDtypeStruct((T, D), F32), jax.ShapeDtypeStruct((1, D), F32),
                   jax.ShapeDtypeStruct((1, D), F32)],
        scratch_shapes=[pltpu.VMEM((TM, D), F32)],
        compiler_params=pltpu.CompilerParams(dimension_semantics=("arbitrary", "arbitrary"),
                                             vmem_limit_bytes=VMEM_LIMIT_LARGE),
    )(dh, f, h_in, gu, g_pre, g_post, w_in, w_out, after)


def mm_nt_norm_bwd(pieces, h_in, dh_out, g, name, after):
    T, D = h_in.shape

    def body(*refs):
        ab = refs[:2 * len(pieces)]
        h_ref, dh_ref, g_ref, _, o_ref, dg_ref = refs[2 * len(pieces):]
        dxn = _dot_nt(ab[0][...], ab[1][...])
        for p in range(1, len(pieces)):
            dxn += _dot_nt(ab[2 * p][...], ab[2 * p + 1][...])
        h = h_ref[...]
        r = _rstd(h)
        xhat = h * r
        dxhat = dxn * g_ref[...]
        o_ref[...] = dh_ref[...] + r * (dxhat - xhat * jnp.mean(dxhat * xhat, axis=-1, keepdims=True))

        @pl.when(pl.program_id(0) == 0)
        def _():
            dg_ref[...] = jnp.zeros_like(dg_ref)

        dg_ref[...] += jnp.sum(dxn * xhat, axis=0, keepdims=True)

    in_specs, args = [], []
    for a, w in pieces:
        in_specs += [pl.BlockSpec((TM, a.shape[1]), lambda i: (i, 0)), _resident(w)]
        args += [a, w]
    row = pl.BlockSpec((TM, D), lambda i: (i, 0))
    return pl.pallas_call(
        body, name=name, grid=(T // TM,),
        in_specs=in_specs + [row, row, _resident(g), UNREAD],
        out_specs=[row, pl.BlockSpec((1, D), lambda i: (0, 0))],
        out_shape=[jax.ShapeDtypeStruct((T, D), F32), jax.ShapeDtypeStruct((1, D), F32)],
        compiler_params=_params("arbitrary"),
    )(*args, h_in, dh_out, g, after)


def gate_merge_out_bwd(dh, f, g, w_out, merged, gt, o_a, o_b, o_m, w_a, w_b, w_m, name, after):
    T = dh.shape[0]
    D = D_MODEL
    branch = ((o_a, w_a), (o_b, w_b), (o_m, w_m))

    def body(dh_ref, f_ref, g_ref, wo_ref, m_ref, gt_ref, oa_ref, ob_ref, om_ref, wa_ref, wb_ref, wm_ref, _,
             dg_ref, dwo_ref, dgt_ref, doa_ref, dob_ref, dom_ref, db_ref, dwa_ref, dwb_ref, dwm_ref):
        @pl.when(pl.program_id(0) == 0)
        def _():
            for acc in (dg_ref, dwo_ref, db_ref, dwa_ref, dwb_ref, dwm_ref):
                acc[...] = jnp.zeros_like(acc)

        df, dg = _rms_bwd(dh_ref[...], f_ref[...], g_ref[...])
        dg_ref[...] += dg
        df = df.astype(BF16)
        dwo_ref[...] += _dot_tn(m_ref[...], df)
        dmf = _dot_nt(df, wo_ref[...])
        for x, (o_ref, w_ref, do_ref, dw_ref) in enumerate(((oa_ref, wa_ref, doa_ref, dwa_ref),
                                                           (ob_ref, wb_ref, dob_ref, dwb_ref),
                                                           (om_ref, wm_ref, dom_ref, dwm_ref))):
            cols = slice(x * D, (x + 1) * D)
            gx = gt_ref[:, cols].astype(F32)
            w = w_ref[...]
            dpre = dmf * _dot(o_ref[...], w) * gx * (1.0 - gx)
            dgt_ref[:, cols] = dpre.astype(BF16)
            db_ref[:, cols] += jnp.sum(dpre, axis=0, keepdims=True)
            dp = (dmf * gx).astype(BF16)
            do_ref[...] = _dot_nt(dp, w).astype(BF16)
            dw_ref[...] += _dot_tn(dp, o_ref[...])

    def rows(width):
        return pl.BlockSpec((TM, width), lambda i: (i, 0))

    def kept(shape):
        return pl.BlockSpec(shape, lambda i: (0,) * len(shape))

    widths = [o.shape[1] for o, _ in branch]
    sums = [(1, D), (D, D), (1, 3 * D)] + [(D, k) for k in widths]
    return pl.pallas_call(
        body, name=name, grid=(T // TM,),
        in_specs=[rows(D), rows(D), _resident(g), _resident(w_out), rows(D), rows(3 * D)]
                 + [rows(k) for k in widths] + [_resident(w) for _, w in branch] + [UNREAD],
        out_specs=[kept(sums[0]), kept(sums[1]), rows(3 * D)] + [rows(k) for k in widths]
                  + [kept(shape) for shape in sums[2:]],
        out_shape=[jax.ShapeDtypeStruct(sums[0], F32), jax.ShapeDtypeStruct(sums[1], F32),
                   jax.ShapeDtypeStruct((T, 3 * D), BF16)] + [jax.ShapeDtypeStruct((T, k), BF16) for k in widths]
                  + [jax.ShapeDtypeStruct(shape, F32) for shape in sums[2:]],
        compiler_params=pltpu.CompilerParams(dimension_semantics=("arbitrary",), vmem_limit_bytes=VMEM_LIMIT_LARGE),
    )(dh, f, g, w_out, merged, gt, o_a, o_b, o_m, w_a, w_b, w_m, after)


def mm_tn(x, dy, tm, tn, name, shard_major=False, perm=None, slabs=1, after=None, wire=False):
    T, M = x.shape
    N = dy.shape[1]
    tk = min(2048, T)
    perm = perm or (lambda j: j)
    w = tn // slabs

    def body(x_ref, dy_ref, *rest):
        o_ref = rest[-2] if wire else rest[-1]

        @pl.when(pl.program_id(2) == 0)
        def _():
            o_ref[...] = jnp.zeros_like(o_ref)

        acc = _dot_tn(x_ref[...], dy_ref[...])
        if shard_major:
            for s in range(slabs):
                o_ref[s] += acc[:, s * w:(s + 1) * w]
        else:
            o_ref[...] += acc
        if wire:
            @pl.when(pl.program_id(2) == T // tk - 1)
            def _():
                rest[-1][...] = o_ref[...].astype(BF16)

    if shard_major:
        out_spec = pl.BlockSpec((slabs, tm, w), lambda i, j, k: (perm(j), i, 0))
        out_shape = jax.ShapeDtypeStruct((N // w, M, w), F32)
    else:
        out_spec = pl.BlockSpec((tm, tn), lambda i, j, k: (i, j))
        out_shape = jax.ShapeDtypeStruct((M, N), F32)
    return pl.pallas_call(
        body, name=name, grid=(M // tm, N // tn, T // tk),
        in_specs=[pl.BlockSpec((tk, tm), lambda i, j, k: (k, i)),
                  pl.BlockSpec((tk, tn), lambda i, j, k: (k, j))] + ([] if after is None else [UNREAD]),
        out_specs=[out_spec, out_spec] if wire else out_spec,
        out_shape=[out_shape, jax.ShapeDtypeStruct(out_shape.shape, BF16)] if wire else out_shape,
        compiler_params=_params("parallel", "parallel", "arbitrary"),
    )(x, dy, *([] if after is None else [after]))


def rope_tables(T, zero):
    half = HEAD // 2
    inv = ROPE_THETA ** (-jnp.arange(half, dtype=F32) / half)
    ang = (jnp.arange(T).astype(F32) + zero)[:, None] * inv[None, :]
    cos, sin = jnp.cos(ang), jnp.sin(ang)
    return jnp.concatenate([cos, cos], axis=1), jnp.concatenate([-sin, sin], axis=1)


def layer_step(x, mem, target, gains, sinks, b_gate, weights_of, send_grads, zero):
    T = x.shape[0]
    cos, sin_signed = rope_tables(T, zero)
    no_sink = jnp.full((2,), NEG_INF, F32)

    xn1 = rms_scale(x, gains["ffn1_norm_pre"], "ffn1_norm", cos)
    w = dict(weights_of("ffn1_in", xn1))
    xn1, gu1, a1 = ffn_in(x, gains["ffn1_norm_pre"], w["ffn1_w_in"], "ffn1_in", xn=xn1)
    w.update(weights_of("ffn1_out", xn1))
    f1, h1 = mm_norm_res(a1, w["ffn1_w_out"], x, gains["ffn1_norm_post"], 0.5, "ffn1_out")
    w.update(weights_of("mix", f1))
    u, qkv, gt = mix_in(h1, gains["mix_norm_pre"], w["w_in"], w["w_gate"], b_gate, cos, sin_signed, "mix_in")
    outs, lses = [], []
    for gidx, (window, dil) in enumerate(DIL):
        last = gidx == len(DIL) - 1
        o_g, l_g = band_fwd(qkv, no_sink, r=dil, base=A_BASE + 6 * gidx, hkv=2, grp=1, max_dist=window // dil,
                            out_dtype=BF16 if last else F32, name=f"attn_a{gidx}_fwd",
                            merge=(outs, lses) if last else None)
        outs.append(o_g)
        lses.append(l_g)
    o_a, l_a = outs[-1], lses[-1]
    o_b, l_b = band_fwd(qkv, sinks, r=1, base=B_BASE, hkv=2, grp=2, max_dist=HEAD - 1, out_dtype=BF16,
                        name="attn_b_fwd")
    mem_n, mkv = mem_kv(mem, gains["mem_norm"], w["w_mem_kv"], "mem_kv")
    o_m, l_m = mem_fwd(qkv, mkv, "attn_m_fwd")
    merged, mo, h2 = gate_merge_out(gt, o_a, o_b, o_m, w["w_o_a"], w["w_o_b"], w["w_o_m"], w["w_out"], h1,
                                    gains["mix_norm_post"], "gate_merge_out")
    w.update(weights_of("ffn2", mo))
    xn2, gu2, a2 = ffn_in(h2, gains["ffn2_norm_pre"], w["ffn2_w_in"], "ffn2_in")
    f2, dy, sq = mm_norm_res(a2, w["ffn2_w_out"], h2, gains["ffn2_norm_post"], 0.5, "ffn2_out", target=target)

    grads = {}

    def ffn_bwd(tag, dh_out, f, gu, a, xn, h_in, after):
        df, dgu, dh_in, grads[f"{tag}_norm_pre"], grads[f"{tag}_norm_post"] = ffn_tokens_bwd(
            dh_out, f, h_in, gu, gains[f"{tag}_norm_pre"], gains[f"{tag}_norm_post"], w[f"{tag}_w_in"],
            w[f"{tag}_w_out"], 0.5, f"{tag}_tokens_bwd", after)
        sent = send_grads(f"{tag}_in", {f"{tag}_w_in": mm_tn(
            xn, dgu, D_MODEL, FF_T, f"{tag}_w_in_grad", shard_major=True, perm=_ffn_perm, wire=True)})
        sent = send_grads(f"{tag}_out", {f"{tag}_w_out": mm_tn(
            a, df, FF_T, D_MODEL, f"{tag}_w_out_grad", after=sent, wire=True)})
        return dh_in, sent

    dh2, sent = ffn_bwd("ffn2", dy, f2, gu2, a2, xn2, h2, dy)

    mix = {}
    (grads["mix_norm_post"], mix["w_out"], dgt, do_a, do_b, do_m, grads["b_gate"],
     dwa_t, dwb_t, dwm_t) = gate_merge_out_bwd(
        dh2, mo, gains["mix_norm_post"], w["w_out"], merged, gt, o_a, o_b, o_m, w["w_o_a"], w["w_o_b"],
        w["w_o_m"], "gate_merge_out_bwd", sent)
    mix["w_o_a"], mix["w_o_b"], mix["w_o_m"] = dwa_t.T, dwb_t.T, dwm_t.T

    dqkv = lax.empty(qkv.shape, qkv.dtype)
    for gidx, (window, dil) in enumerate(DIL):
        dqkv, = band_bwd(qkv, dqkv, do_a, o_a, l_a, cos, sin_signed, None, r=dil, base=A_BASE + 6 * gidx, hkv=2,
                         grp=1, max_dist=window // dil, name=f"attn_a{gidx}_bwd")
    dqkv, dsink = band_bwd(qkv, dqkv, do_b, o_b, l_b, cos, sin_signed, sinks, r=1, base=B_BASE, hkv=2, grp=2,
                           max_dist=HEAD - 1, name="attn_b_bwd")
    grads["sinks"] = -dsink[:, ::8, 0].reshape(1, 4)
    dqkv, dmk, dmv = mem_bwd(qkv, dqkv, mkv, do_m, o_m, l_m, "attn_m_bwd")
    mix["w_mem_kv"], grads["mem_norm"] = mem_kv_bwd(
        mem, gains["mem_norm"], mem_n, w["w_mem_kv"], jnp.concatenate([dmk, dmv], axis=1), "mem_kv_bwd")

    mix["w_in"] = mm_tn(u, dqkv, D_MODEL, 1280, "w_in_grad")
    mix["w_gate"] = mm_tn(u, dgt, D_MODEL, 1536, "w_gate_grad", shard_major=True, slabs=2, wire=True)
    sent = send_grads("mix", mix)
    dh1, grads["mix_norm_pre"] = mm_nt_norm_bwd(
        [(dqkv, w["w_in"]), (dgt, w["w_gate"])], h1, dh2, gains["mix_norm_pre"], "mix_in_bwd", sent)

    dx, _ = ffn_bwd("ffn1", dh1, f1, gu1, a1, xn1, x, dh1)
    return sq, dx, grads


def _place():
    return lax.axis_index("x"), lax.axis_index("y"), lax.axis_index("c")


def _other_chips(x, y):
    return [(1 - x, y), (x, 1 - y), (1 - x, 1 - y)]


def _hbm(n):
    return [pl.BlockSpec(memory_space=pltpu.HBM)] * n


SEM = pl.BlockSpec(memory_space=pltpu.SEMAPHORE)
SIDE_EFFECT = pltpu.SideEffectType.DATAFLOW_SIDE_EFFECTING


def _chip_copy(src, land, sems, i, j, dst_slot, scatter):
    x, y, c = _place()
    px, py = _other_chips(x, y)[j]
    send_sems, recv_sems = sems
    return pltpu.make_async_remote_copy(
        src_ref=src[i].at[2 * px + py] if scatter else src[i], dst_ref=land[i].at[dst_slot],
        send_sem=send_sems.at[3 * i + j], recv_sem=recv_sems.at[3 * i + j],
        device_id=(px, py, c), device_id_type=MESH)


def chip_copies_start(srcs, lands, groups, scatter, name, after=None):
    n = len(srcs)

    def body(*refs):
        src, land = refs[:n], refs[n:2 * n]
        first_sem = 2 * n + (after is not None)
        sems = refs[first_sem:first_sem + 2 * len(groups)]
        token = refs[-1]
        x, y, _ = _place()
        for g, members in enumerate(groups):
            part = ([src[i] for i in members], [land[i] for i in members])
            for t in range(len(members)):
                for j in range(3):
                    _chip_copy(*part, sems[2 * g:2 * g + 2], t, j, 2 * x + y, scatter).start()
        token[...] = jnp.zeros_like(token)

    sem_shapes = [pltpu.SemaphoreType.DMA((3 * len(m),)) for m in groups for _ in range(2)]
    thru = [pltpu.HBM(a.shape, a.dtype) for a in (*srcs, *lands)]
    res = pl.pallas_call(
        body, name=name,
        out_shape=(*sem_shapes, *thru, jax.ShapeDtypeStruct((8, 128), F32)),
        in_specs=_hbm(2 * n) + ([] if after is None else [UNREAD]),
        out_specs=(*[SEM] * len(sem_shapes), *_hbm(2 * n), pl.BlockSpec(memory_space=pltpu.VMEM)),
        input_output_aliases={i: len(sem_shapes) + i for i in range(2 * n)},
        compiler_params=pltpu.CompilerParams(has_side_effects=SIDE_EFFECT),
    )(*[pltpu.with_memory_space_constraint(a, pltpu.HBM) for a in (*srcs, *lands)],
      *([] if after is None else [after]))
    k = len(sem_shapes)
    sems = [tuple(res[2 * g:2 * g + 2]) for g in range(len(groups))]
    return sems, list(res[k:k + n]), list(res[k + n:k + 2 * n]), res[-1]


def chip_copies_wait(srcs, lands, sems, after, scatter, name):
    n = len(srcs)

    def body(*refs):
        src, land = refs[:n], refs[n:2 * n]
        pair = refs[2 * n:2 * n + 2]
        x, y, _ = _place()
        for i in range(n):
            for j, (px, py) in enumerate(_other_chips(x, y)):
                copy = _chip_copy(src, land, pair, i, j, 2 * px + py, scatter)
                copy.wait_send()
                copy.wait_recv()

    res = pl.pallas_call(
        body, name=name,
        out_shape=[pltpu.HBM(a.shape, a.dtype) for a in (*srcs, *lands)],
        in_specs=[*_hbm(2 * n), SEM, SEM, pl.BlockSpec(memory_space=pl.ANY)],
        out_specs=_hbm(2 * n),
        input_output_aliases={i: i for i in range(2 * n)},
        compiler_params=pltpu.CompilerParams(has_side_effects=SIDE_EFFECT),
    )(*srcs, *lands, *sems, after)
    return list(res[n:])


def small_all_gather(small, name):
    flips = [(fx, fy, fc) for fx in (0, 1) for fy in (0, 1) for fc in (0, 1)][1:]

    def body(in_ref, out_ref, send_sems, recv_sems, local_sem):
        x, y, c = _place()
        me = 4 * x + 2 * y + c

        def copy(k, slot):
            fx, fy, fc = flips[k]
            return pltpu.make_async_remote_copy(
                src_ref=in_ref, dst_ref=out_ref.at[slot], send_sem=send_sems.at[k], recv_sem=recv_sems.at[k],
                device_id=(x ^ fx, y ^ fy, c ^ fc), device_id_type=MESH)

        local = pltpu.make_async_copy(in_ref, out_ref.at[me], local_sem)
        local.start()
        for k in range(len(flips)):
            copy(k, me).start()
        for k, (fx, fy, fc) in enumerate(flips):
            copy(k, 4 * (x ^ fx) + 2 * (y ^ fy) + (c ^ fc)).wait()
        local.wait()

    return pl.pallas_call(
        body, name=name, in_specs=_hbm(1), out_specs=_hbm(1)[0],
        out_shape=jax.ShapeDtypeStruct((N_DEV,) + small.shape, small.dtype),
        scratch_shapes=[pltpu.SemaphoreType.DMA((len(flips),)), pltpu.SemaphoreType.DMA((len(flips),)),
                        pltpu.SemaphoreType.DMA],
    )(small)


def _sibling_copy(src, land, sems, i):
    x, y, c = _place()
    return pltpu.make_async_remote_copy(
        src_ref=src[i], dst_ref=land[i], send_sem=sems[0].at[i], recv_sem=sems[1].at[i],
        device_id=(x, y, 1 - c), device_id_type=MESH)


def sibling_copies_start(parts, name):
    n = len(parts)
    lands = [lax.empty(p.shape, p.dtype) for p in parts]

    def body(*refs):
        src, land, sems, token = refs[:n], refs[n:2 * n], refs[2 * n:2 * n + 2], refs[-1]
        for i in range(n):
            _sibling_copy(src, land, sems, i).start()
        token[...] = jnp.zeros_like(token)

    res = pl.pallas_call(
        body, name=name,
        out_shape=(pltpu.SemaphoreType.DMA((n,)), pltpu.SemaphoreType.DMA((n,)),
                   *[pltpu.HBM(a.shape, a.dtype) for a in (*parts, *lands)], jax.ShapeDtypeStruct((8, 128), F32)),
        in_specs=_hbm(2 * n),
        out_specs=(SEM, SEM, *_hbm(2 * n), pl.BlockSpec(memory_space=pltpu.VMEM)),
        input_output_aliases={i: 2 + i for i in range(2 * n)},
        compiler_params=pltpu.CompilerParams(has_side_effects=SIDE_EFFECT),
    )(*[pltpu.with_memory_space_constraint(a, pltpu.HBM) for a in (*parts, *lands)])
    return tuple(res[:2]), list(res[2:2 + n]), list(res[2 + n:2 + 2 * n]), res[-1]


def sibling_copies_wait(parts, lands, sems, after, name):
    n = len(parts)

    def body(*refs):
        src, land, sems = refs[:n], refs[n:2 * n], refs[2 * n:2 * n + 2]
        for i in range(n):
            copy = _sibling_copy(src, land, sems, i)
            copy.wait_send()
            copy.wait_recv()

    res = pl.pallas_call(
        body, name=name,
        out_shape=[pltpu.HBM(a.shape, a.dtype) for a in (*parts, *lands)],
        in_specs=[*_hbm(2 * n), SEM, SEM, UNREAD],
        out_specs=_hbm(2 * n),
        input_output_aliases={i: i for i in range(2 * n)},
        compiler_params=pltpu.CompilerParams(has_side_effects=SIDE_EFFECT),
    )(*parts, *lands, *sems, after)
    return list(res[n:])


def _row_tile(rows):
    for t in (256, 176, 128, 64, 32, 16, 8):
        if rows % t == 0:
            return t
    return rows


def chip_partial_sum(me, own_sm, recv, name):
    _, rows, cols = own_sm.shape
    tr = _row_tile(rows)

    def body(me_ref, own_ref, r0, r1, r2, r3, o_ref):
        acc = jnp.zeros((tr, cols), F32)
        for s, r_ref in enumerate((r0, r1, r2, r3)):
            acc = acc + jnp.where(me_ref[0] == s, own_ref[...], r_ref[...].astype(F32))
        o_ref[...] = acc

    def slot(s):
        return pl.BlockSpec((None, tr, cols), lambda i, me_ref, s=s: (s, i, 0))

    return pl.pallas_call(
        body, name=name,
        grid_spec=pltpu.PrefetchScalarGridSpec(
            num_scalar_prefetch=1, grid=(rows // tr,),
            in_specs=[pl.BlockSpec((None, tr, cols), lambda i, me_ref: (me_ref[0], i, 0))] + [slot(s) for s in range(4)],
            out_specs=pl.BlockSpec((tr, cols), lambda i, me_ref: (i, 0))),
        out_shape=jax.ShapeDtypeStruct((rows, cols), F32),
        compiler_params=_params("parallel"),
    )(me, own_sm, recv, recv, recv, recv)


def _adamw(w, g, m, v):
    m = ADAM_B1 * m + (1.0 - ADAM_B1) * g
    v = ADAM_B2 * v + (1.0 - ADAM_B2) * (g * g)
    m_hat = m / (1.0 - ADAM_B1 ** ADAM_STEP)
    v_hat = v / (1.0 - ADAM_B2 ** ADAM_STEP)
    delta = -ADAM_LR * (m_hat / (jnp.sqrt(v_hat) + ADAM_EPS) + ADAM_WD * w)
    return delta, m, v


def adamw_pair(part, sib, w, m, v, name):
    rows, cols = w.shape
    tr = _row_tile(rows)

    def body(p_ref, s_ref, w_ref, m_ref, v_ref, g_ref, d_ref, nm_ref, nv_ref):
        g = p_ref[...] + s_ref[...]
        g_ref[...] = g
        d_ref[...], nm_ref[...], nv_ref[...] = _adamw(w_ref[...], g, m_ref[...], v_ref[...])

    spec = pl.BlockSpec((tr, cols), lambda i: (i, 0))
    return pl.pallas_call(
        body, name=name, grid=(rows // tr,), in_specs=[spec] * 5, out_specs=[spec] * 4,
        out_shape=[jax.ShapeDtypeStruct((rows, cols), F32)] * 4,
        compiler_params=_params("parallel"),
    )(part, sib, w, m, v)


def adamw_small(g_all, w, m, v, name):
    def body(ga_ref, w_ref, m_ref, v_ref, g_ref, d_ref, nm_ref, nv_ref):
        g = ga_ref[0]
        for k in range(1, N_DEV):
            g = g + ga_ref[k]
        g_ref[...] = g
        d_ref[...], nm_ref[...], nv_ref[...] = _adamw(w_ref[...], g, m_ref[...], v_ref[...])

    return pl.pallas_call(
        body, name=name, out_shape=[jax.ShapeDtypeStruct(w.shape, F32)] * 4,
    )(g_all, w, m, v)


WEIGHTS = ("ffn1_norm_pre", "ffn1_w_in", "ffn1_w_out", "ffn1_norm_post", "mix_norm_pre", "w_in", "sinks",
           "mem_norm", "w_mem_kv", "w_gate", "b_gate", "w_o_a", "w_o_b", "w_o_m", "w_out", "mix_norm_post",
           "ffn2_norm_pre", "ffn2_w_in", "ffn2_w_out", "ffn2_norm_post")
BIG = ("ffn1_w_in", "ffn1_w_out", "w_in", "w_mem_kv", "w_gate", "w_o_a", "w_o_b", "w_o_m", "w_out",
       "ffn2_w_in", "ffn2_w_out")
GATHER_STAGES = (("ffn1_in", "ffn1_out"), ("mix",), ("ffn2",))
GATHER_GROUPS = {"ffn1_in": ("ffn1_w_in",), "ffn1_out": ("ffn1_w_out",),
                 "mix": ("w_in", "w_gate", "w_mem_kv", "w_o_a", "w_o_b", "w_o_m", "w_out"),
                 "ffn2": ("ffn2_w_in", "ffn2_w_out")}
GROUPS = {"ffn1_in": ("ffn1_w_in",), "ffn1_out": ("ffn1_w_out",),
          "mix": ("w_in", "w_gate", "w_mem_kv", "w_o_a", "w_o_b", "w_o_m", "w_out"),
          "ffn2_in": ("ffn2_w_in",), "ffn2_out": ("ffn2_w_out",)}
COLUMN_SHARDED = ("ffn1_w_in", "ffn2_w_in", "w_in", "w_gate", "w_o_a", "w_o_b", "w_o_m")
KEPT_SHARD_MAJOR = ("ffn1_w_in", "ffn2_w_in", "w_gate")
GAINS = ("ffn1_norm_pre", "ffn1_norm_post", "mix_norm_pre", "mem_norm", "mix_norm_post", "ffn2_norm_pre",
         "ffn2_norm_post")
SMALL_ROWS = 16


def _pack_small(t):
    sinks = jnp.pad(t["sinks"], ((0, 0), (0, D_MODEL - t["sinks"].shape[1])))
    rows = [t[k] for k in GAINS] + [t["b_gate"].reshape(3, D_MODEL), sinks]
    packed = jnp.concatenate(rows, axis=0)
    return jnp.pad(packed, ((0, SMALL_ROWS - packed.shape[0]), (0, 0)))


def _unpack_small(p):
    out = {k: p[i:i + 1] for i, k in enumerate(GAINS)}
    out["b_gate"] = p[7:10].reshape(1, 3 * D_MODEL)
    out["sinks"] = p[10:11, :4]
    return out


def kernel(x, mem, ffn1_norm_pre, ffn1_w_in, ffn1_w_out, ffn1_norm_post, mix_norm_pre, w_in, sinks, mem_norm, w_mem_kv, w_gate, b_gate, w_o_a, w_o_b, w_o_m, w_out, mix_norm_post, ffn2_norm_pre, ffn2_w_in, ffn2_w_out, ffn2_norm_post, loss_target, m_ffn1_norm_pre, m_ffn1_w_in, m_ffn1_w_out, m_ffn1_norm_post, m_mix_norm_pre, m_w_in, m_sinks, m_mem_norm, m_w_mem_kv, m_w_gate, m_b_gate, m_w_o_a, m_w_o_b, m_w_o_m, m_w_out, m_mix_norm_post, m_ffn2_norm_pre, m_ffn2_w_in, m_ffn2_w_out, m_ffn2_norm_post, v_ffn1_norm_pre, v_ffn1_w_in, v_ffn1_w_out, v_ffn1_norm_post, v_mix_norm_pre, v_w_in, v_sinks, v_mem_norm, v_w_mem_kv, v_w_gate, v_b_gate, v_w_o_a, v_w_o_b, v_w_o_m, v_w_out, v_mix_norm_post, v_ffn2_norm_pre, v_ffn2_w_in, v_ffn2_w_out, v_ffn2_norm_post):
    given = dict(locals())
    wt = {k: given[k] for k in WEIGHTS}
    mom = {k: given["m_" + k] for k in WEIGHTS}
    var = {k: given["v_" + k] for k in WEIGHTS}
    chip = (2 * lax.axis_index("x") + lax.axis_index("y")).astype(jnp.int32)
    me = chip.reshape(1)

    def landing_zone(own):
        return lax.dynamic_update_slice_in_dim(lax.empty((N_CHIPS,) + own.shape, own.dtype), own[None], chip, 0)

    started = {}
    tokens = []

    def stage_keys(stage):
        return [k for g in GATHER_STAGES[stage] for k in GATHER_GROUPS[g]]

    def prepare(stage):
        shards = [(wt[k][0] + tokens[0][0, 0] if tokens else wt[k][0]).astype(BF16) for k in stage_keys(stage)]
        return shards, [landing_zone(s) for s in shards]

    def start_gather(stage, after):
        groups, keys = GATHER_STAGES[stage], stage_keys(stage)
        members = [[keys.index(k) for k in GATHER_GROUPS[g]] for g in groups]
        sems, shards, lands, token = chip_copies_start(
            *prepared[stage], members, False, f"weight_gather_start_{stage}", after)
        tokens.append(token)
        for g, idx, pair in zip(groups, members, sems):
            started[g] = ([shards[i] for i in idx], [lands[i] for i in idx], pair)

    prepared = {0: prepare(0)}
    start_gather(0, None)
    prepared.update({stage: prepare(stage) for stage in range(1, len(GATHER_STAGES))})

    def weights_of(group, after):
        got = chip_copies_wait(*started[group], after, False, f"weight_gather_wait_{group}")
        stage = [s + 1 for s, groups in enumerate(GATHER_STAGES[:-1]) if groups[0] == group]
        if stage:
            start_gather(stage[0], got[0])
        full = {}
        for k, g in zip(GATHER_GROUPS[group], got):
            if k in COLUMN_SHARDED:
                if k in ("ffn1_w_in", "ffn2_w_in"):
                    g = jnp.stack([g[0], g[2], g[1], g[3]])
                full[k] = jnp.swapaxes(g, 0, 1).reshape(g.shape[1], N_CHIPS * g.shape[2])
                if k == "w_in":
                    full[k] = to_kernel_heads(full[k])
            else:
                full[k] = g.reshape(N_CHIPS * g.shape[1], g.shape[2])
        return full

    in_flight = {}

    def send_grads(group, grads):
        def shard_major(k, g):
            if k in KEPT_SHARD_MAJOR:
                return g
            if k in COLUMN_SHARDED:
                return jnp.swapaxes(g.reshape(g.shape[0], N_CHIPS, g.shape[1] // N_CHIPS), 0, 1)
            return g.reshape(N_CHIPS, g.shape[0] // N_CHIPS, g.shape[1])

        own, wire = [], []
        for k in GROUPS[group]:
            g, rounded = grads[k] if isinstance(grads[k], (tuple, list)) else (grads[k], None)
            g = shard_major(k, from_kernel_heads(g) if k == "w_in" else g)
            own.append(g)
            wire.append(g.astype(BF16) if rounded is None else shard_major(k, rounded))
        zones = [landing_zone(lax.dynamic_index_in_dim(b, chip, 0, keepdims=False)) for b in wire]
        pair, wire, zones, sent = chip_copies_start(
            wire, zones, [list(range(len(wire)))], True, f"grad_scatter_start_{group}")
        in_flight[group] = (own, wire, zones, pair[0], sent)
        return sent

    gains = {k: wt[k] for k in GAINS}
    sq, dx, grads = layer_step(
        x[0], mem[0], loss_target[0], gains, sinks[0], b_gate, weights_of, send_grads, tokens[0][0, 0])
    loss = lax.psum(0.5 * sq[0, 0] / D_MODEL, ("x", "y", "c"))

    res = {}
    after = in_flight["ffn1_out"][4]
    swaps = []
    for stage in (("ffn2_in", "ffn2_out", "mix", "ffn1_in"), ("ffn1_out",)):
        names, parts = [], []
        for group in stage:
            own, wire, zones, pair, _ = in_flight[group]
            received = chip_copies_wait(wire, zones, pair, after, True, f"grad_scatter_wait_{group}")
            for k, g, r in zip(GROUPS[group], own, received):
                names.append(k)
                parts.append(chip_partial_sum(me, g, r, f"{k}_chip_sum"))
        pair, parts, lands, after = sibling_copies_start(parts, f"sibling_start_{stage[-1]}")
        swaps.append((stage[-1], names, parts, lands, pair))
    small_all = small_all_gather(_pack_small(grads), "small_grad_gather")
    packed = adamw_small(small_all, _pack_small(wt), _pack_small(mom), _pack_small(var), "small_adamw")
    after = packed[0]
    for tag, names, parts, lands, pair in swaps:
        sibs = sibling_copies_wait(parts, lands, pair, after, f"sibling_wait_{tag}")
        for k, p, s in zip(names, parts, sibs):
            res[k] = [t[None] for t in adamw_pair(p, s, wt[k][0], mom[k][0], var[k][0], f"{k}_adamw")]
        after = res[names[-1]][0]
    for idx, p in enumerate(packed):
        for k, t in _unpack_small(p).items():
            res.setdefault(k, [None] * 4)[idx] = t

    return (loss, dx[None], *[res[k][0] for k in WEIGHTS], *[res[k][1] for k in WEIGHTS],
            *[res[k][2] for k in WEIGHTS], *[res[k][3] for k in WEIGHTS])
```

```python
import functools

import jax
import jax.numpy as jnp
from jax import lax
from jax.experimental import pallas as pl
from jax.experimental.pallas import tpu as pltpu

F32 = jnp.float32
BF16 = jnp.bfloat16

D_MODEL = 1024
D_FF = 2816
HEAD = 128
N_CHIPS = 4
N_DEV = 8
EPS = 1e-6
NEG_INF = -1e30
ROPE_THETA = 10000.0
ATT_SCALE = HEAD ** -0.5

ADAM_LR = 0.001
ADAM_B1 = 0.9
ADAM_B2 = 0.999
ADAM_EPS = 1e-08
ADAM_WD = 0.01
ADAM_STEP = 10

VMEM_LIMIT = 52 * 2 ** 20
VMEM_LIMIT_LARGE = 60 * 2 ** 20
MESH = pl.DeviceIdType.MESH

QKV_W = 3840
DIL = ((128, 1), (512, 4), (2048, 16))
B_BASE, MQ, A_BASE = 0, 8, 12
_AQ, _AK, _AV, _BQ, _BK, _BV, _MQ = 0, 6, 12, 18, 22, 24, 26
HEAD_ORDER = tuple(
    [h for j in range(2) for h in (_BQ + 2 * j, _BQ + 2 * j + 1, _BK + j, _BV + j)]
    + [_MQ + i for i in range(4)]
    + [h for g in range(3) for i in range(2) for h in (_AQ + 2 * g + i, _AK + 2 * g + i, _AV + 2 * g + i)])
ROTARY_HEADS = tuple(p for p, h in enumerate(HEAD_ORDER) if h < _AV or _BQ <= h < _BV)


def to_kernel_heads(w):
    return jnp.concatenate([w[..., h * HEAD:(h + 1) * HEAD] for h in HEAD_ORDER], axis=-1)


def from_kernel_heads(w):
    place = {h: p for p, h in enumerate(HEAD_ORDER)}
    return jnp.concatenate([w[..., place[h] * HEAD:(place[h] + 1) * HEAD] for h in range(len(HEAD_ORDER))], axis=-1)

TM = 512
FF_T = D_FF // 2


def _params(*sem):
    return pltpu.CompilerParams(dimension_semantics=sem, vmem_limit_bytes=VMEM_LIMIT)


def _dot(a, b):
    return jnp.dot(a, b, preferred_element_type=F32)


def _dot_nt(a, b):
    return lax.dot_general(a, b, (((1,), (1,)), ((), ())), preferred_element_type=F32)


def _dot_tn(a, b):
    return lax.dot_general(a, b, (((0,), (0,)), ((), ())), preferred_element_type=F32)


def _rstd(x):
    return lax.rsqrt(jnp.mean(x * x, axis=-1, keepdims=True) + EPS)


def _sigmoid(x):
    return 0.5 * jnp.tanh(0.5 * x) + 0.5


def _ffn_perm(k):
    return (k % 2) * 2 + k // 2


UNREAD = pl.BlockSpec(memory_space=pl.ANY)


def _resident(arr):
    return pl.BlockSpec(arr.shape, lambda *_: (0,) * arr.ndim, pipeline_mode=pl.Buffered(1))


def rms_scale(x, g, name, after):
    T, D = x.shape
    tm = 1024

    def body(x_ref, g_ref, _, o_ref):
        v = x_ref[...]
        o_ref[...] = (v * _rstd(v) * g_ref[...]).astype(BF16)

    spec = pl.BlockSpec((tm, D), lambda i: (i, 0))
    return pl.pallas_call(
        body, name=name, grid=(T // tm,), in_specs=[spec, _resident(g), UNREAD], out_specs=spec,
        out_shape=jax.ShapeDtypeStruct((T, D), BF16), compiler_params=_params("parallel"),
    )(x, g, after)


def ffn_in(h, g, w, name, xn=None):
    T, D = h.shape
    normed = xn is not None

    def body(h_ref, g_ref, w_ref, *outs):
        if normed:
            xn, (gu_ref, a_ref) = h_ref[...], outs
        else:
            xn_ref, gu_ref, a_ref = outs
            x = h_ref[...]
            xn = (x * _rstd(x) * g_ref[...]).astype(BF16)
            xn_ref[...] = xn
        for j in range(2):
            gu = _dot(xn, w_ref[:, j * 2 * FF_T:(j + 1) * 2 * FF_T])
            gu_ref[:, j * 2 * FF_T:(j + 1) * 2 * FF_T] = gu.astype(BF16)
            gate, up = gu[:, :FF_T], gu[:, FF_T:]
            a_ref[:, j * FF_T:(j + 1) * FF_T] = (gate * _sigmoid(gate) * up).astype(BF16)

    def rows(width):
        return pl.BlockSpec((TM, width), lambda i: (i, 0))

    res = pl.pallas_call(
        body, name=name,
        grid=(T // TM,),
        in_specs=[rows(D), _resident(g), _resident(w)],
        out_specs=[rows(D)] * (not normed) + [rows(2 * D_FF), rows(D_FF)],
        out_shape=[jax.ShapeDtypeStruct((T, D), BF16)] * (not normed)
                  + [jax.ShapeDtypeStruct((T, 2 * D_FF), BF16), jax.ShapeDtypeStruct((T, D_FF), BF16)],
        compiler_params=_params("parallel"),
    )(xn if normed else h, g, w)
    return (xn, *res) if normed else tuple(res)


def mm_norm_res(a, w, h_in, g, coef, name, target=None):
    T, K = a.shape
    D = w.shape[1]
    final = target is not None

    def body(*refs):
        if final:
            a_ref, w_ref, h_ref, g_ref, t_ref, f_ref, o_ref, l_ref = refs
        else:
            a_ref, w_ref, h_ref, g_ref, f_ref, o_ref = refs
        f = _dot(a_ref[...], w_ref[...])
        f_ref[...] = f
        y = h_ref[...] + coef * (f * _rstd(f) * g_ref[...])
        if final:
            err = y - t_ref[...]
            o_ref[...] = err * (1.0 / D)

            @pl.when(pl.program_id(0) == 0)
            def _():
                l_ref[...] = jnp.zeros_like(l_ref)

            l_ref[...] += jnp.sum(err * err)
        else:
            o_ref[...] = y

    row = pl.BlockSpec((TM, D), lambda i: (i, 0))
    in_specs = [pl.BlockSpec((TM, K), lambda i: (i, 0)),
                _resident(w),
                row, pl.BlockSpec((1, D), lambda i: (0, 0))]
    out_specs = [row, row]
    out_shape = [jax.ShapeDtypeStruct((T, D), F32), jax.ShapeDtypeStruct((T, D), F32)]
    args = [a, w, h_in, g]
    if final:
        in_specs.append(row)
        args.append(target)
        out_specs.append(pl.BlockSpec((8, 128), lambda i: (0, 0)))
        out_shape.append(jax.ShapeDtypeStruct((8, 128), F32))
    return pl.pallas_call(
        body, name=name, grid=(T // TM,), in_specs=in_specs, out_specs=out_specs, out_shape=out_shape,
        compiler_params=_params("arbitrary"),
    )(*args)


def _rope(x, cos, sin_signed):
    return x * cos + pltpu.roll(x, HEAD // 2, axis=1) * sin_signed


def _unrope(x, cos, sin_signed):
    return x * cos - pltpu.roll(x, HEAD // 2, axis=1) * sin_signed


def mix_in(h, g, w, w_gate, b_gate, cos, sin_signed, name):
    T, D = h.shape
    tn = 768

    def body(h_ref, g_ref, w_ref, wg_ref, b_ref, c_ref, s_ref, u_ref, o_ref, gt_ref):
        x = h_ref[...]
        u = (x * _rstd(x) * g_ref[...]).astype(BF16)
        u_ref[...] = u
        c, s = c_ref[...], s_ref[...]
        for j in range(QKV_W // tn):
            acc = _dot(u, w_ref[:, j * tn:(j + 1) * tn])
            for hd in range(tn // HEAD):
                head = j * (tn // HEAD) + hd
                part = acc[:, hd * HEAD:(hd + 1) * HEAD]
                if head in ROTARY_HEADS:
                    part = _rope(part, c, s)
                o_ref[:, head * HEAD:(head + 1) * HEAD] = part.astype(BF16)
        for j in range(w_gate.shape[1] // tn):
            cols = slice(j * tn, (j + 1) * tn)
            gt_ref[:, cols] = _sigmoid(_dot(u, wg_ref[:, cols]) + b_ref[:, cols]).astype(BF16)

    def rows(width):
        return pl.BlockSpec((TM, width), lambda i: (i, 0))

    return pl.pallas_call(
        body, name=name,
        grid=(T // TM,),
        in_specs=[rows(D), _resident(g), _resident(w), _resident(w_gate), _resident(b_gate), rows(HEAD), rows(HEAD)],
        out_specs=[rows(D), rows(QKV_W), rows(w_gate.shape[1])],
        out_shape=[jax.ShapeDtypeStruct((T, D), BF16), jax.ShapeDtypeStruct((T, QKV_W), BF16),
                   jax.ShapeDtypeStruct((T, w_gate.shape[1]), BF16)],
        compiler_params=_params("parallel"),
    )(h, g, w, w_gate, b_gate, cos, sin_signed)


def gate_merge_out(gt, o_a, o_b, o_m, w_a, w_b, w_m, w_out, h_in, g, name):
    T = gt.shape[0]
    D = D_MODEL

    def body(gt_ref, oa_ref, ob_ref, om_ref, wa_ref, wb_ref, wm_ref, wo_ref, h_ref, g_ref, m_ref, f_ref, o_ref):
        acc = gt_ref[:, :D].astype(F32) * _dot(oa_ref[...], wa_ref[...])
        acc += gt_ref[:, D:2 * D].astype(F32) * _dot(ob_ref[...], wb_ref[...])
        acc += gt_ref[:, 2 * D:].astype(F32) * _dot(om_ref[...], wm_ref[...])
        merged = acc.astype(BF16)
        m_ref[...] = merged
        f = _dot(merged, wo_ref[...])
        f_ref[...] = f
        o_ref[...] = h_ref[...] + f * _rstd(f) * g_ref[...]

    def rows(width):
        return pl.BlockSpec((TM, width), lambda i: (i, 0))

    return pl.pallas_call(
        body, name=name, grid=(T // TM,),
        in_specs=[rows(3 * D), rows(o_a.shape[1]), rows(o_b.shape[1]), rows(o_m.shape[1]),
                  _resident(w_a), _resident(w_b), _resident(w_m), _resident(w_out), rows(D), _resident(g)],
        out_specs=[rows(D), rows(D), rows(D)],
        out_shape=[jax.ShapeDtypeStruct((T, D), BF16), jax.ShapeDtypeStruct((T, D), F32),
                   jax.ShapeDtypeStruct((T, D), F32)],
        compiler_params=_params("parallel"),
    )(gt, o_a, o_b, o_m, w_a, w_b, w_m, w_out, h_in, g)


def _band_rows(start, r):
    return pl.ds(start, HEAD) if r == 1 else pl.ds(start, HEAD, stride=r)


def _band_mask(max_dist, first_has_prev):
    row = lax.broadcasted_iota(jnp.int32, (HEAD, 2 * HEAD), 0)
    col = lax.broadcasted_iota(jnp.int32, (HEAD, 2 * HEAD), 1)
    dist = row + HEAD - col
    band = (dist >= 0) & (dist <= max_dist)
    return band, band & (col >= jnp.where(first_has_prev, 0, HEAD))


def _stack(parts):
    return parts[0] if len(parts) == 1 else jnp.concatenate(parts, axis=0)


def _band_specs(BT, SB, nsub, base, grp):
    stride = grp + 2

    def cur(off, width):
        return pl.BlockSpec((BT, width * HEAD), lambda h, i: (i, (base + h * stride + off) // width))

    def prev(off):
        return pl.BlockSpec((SB, HEAD), lambda h, i: (jnp.maximum(i * nsub - 1, 0), base + h * stride + off))

    return cur(0, grp), cur(grp, 1), prev(grp), cur(grp + 1, 1), prev(grp + 1)


def band_fwd(qkv, sinks, *, r, base, hkv, grp, max_dist, out_dtype, name, merge=None):
    T, W = qkv.shape
    SB = HEAD * r
    BT = min(2048, T)
    nsub, nib = BT // SB, T // BT
    hq = hkv * grp
    heads = [slice(g * HEAD, (g + 1) * HEAD) for g in range(grp)]
    others = [] if merge is None else [*merge[0], *merge[1]]

    def body(sink_ref, q_ref, kc_ref, kp_ref, vc_ref, vp_ref, *rest):
        joint_o, joint_l = rest[len(others):len(others) + 2]
        qf, kf, vf = rest[len(others) + 2:len(others) + 5]
        o_ref, l_ref = rest[len(others) + 5:] if others else (joint_o, joint_l)
        kvh, ib = pl.program_id(0), pl.program_id(1)
        qf[...] = q_ref[...].astype(F32)
        kf[:SB] = kp_ref[...].astype(F32)
        kf[SB:] = kc_ref[...].astype(F32)
        vf[:SB] = vp_ref[...].astype(F32)
        vf[SB:] = vc_ref[...].astype(F32)
        band, band_first = _band_mask(max_dist, ib > 0)
        for c in range(r):
            k_old, v_old = kf[_band_rows(c, r)], vf[_band_rows(c, r)]
            for j in range(nsub):
                mask = band_first if j == 0 else band
                rows = _band_rows(j * SB + c, r)
                k_own, v_own = kf[_band_rows((j + 1) * SB + c, r)], vf[_band_rows((j + 1) * SB + c, r)]
                kcat = jnp.concatenate([k_old, k_own], axis=0).astype(BF16)
                vcat = jnp.concatenate([v_old, v_own], axis=0).astype(BF16)
                k_old, v_old = k_own, v_own
                s_all = _dot_nt(_stack([qf[rows, cols] for cols in heads]).astype(BF16), kcat) * ATT_SCALE
                probs, tots = [], []
                for g, cols in enumerate(heads):
                    s = jnp.where(mask, s_all[cols], NEG_INF)
                    sk = sink_ref[kvh * grp + g]
                    m = jnp.maximum(jnp.max(s, axis=-1, keepdims=True), sk)
                    p = jnp.exp(s - m)
                    tot = jnp.sum(p, axis=-1, keepdims=True) + jnp.exp(sk - m)
                    probs.append(p.astype(BF16))
                    tots.append(tot)
                    l_ref[rows, cols] = jnp.broadcast_to(m + jnp.log(tot), (HEAD, HEAD))
                o_all = _dot(_stack(probs), vcat)
                for g, cols in enumerate(heads):
                    o_ref[rows, cols] = (o_all[cols] / tots[g]).astype(o_ref.dtype)

        if others:
            half = len(others) // 2
            outs = [ref[...] for ref in rest[:half]] + [o_ref[...]]
            logs = [ref[...] for ref in rest[half:len(others)]] + [l_ref[...]]
            top = functools.reduce(jnp.maximum, logs)
            weights = [jnp.exp(lg - top) for lg in logs]
            total = functools.reduce(jnp.add, weights)
            mixed = functools.reduce(jnp.add, [wgt * out for wgt, out in zip(weights, outs)])
            joint_o[...] = (mixed / total).astype(out_dtype)
            joint_l[...] = top + jnp.log(total)

    out_spec = pl.BlockSpec((BT, grp * HEAD), lambda h, i: (i, h))
    own = [pltpu.VMEM((BT, grp * HEAD), F32)] * 2 if others else []
    return pl.pallas_call(
        body, name=name, grid=(hkv, nib),
        in_specs=[pl.BlockSpec(memory_space=pltpu.SMEM), *_band_specs(BT, SB, nsub, base, grp)]
                 + [out_spec] * len(others),
        out_specs=[out_spec, out_spec],
        out_shape=[jax.ShapeDtypeStruct((T, hq * HEAD), out_dtype), jax.ShapeDtypeStruct((T, hq * HEAD), F32)],
        scratch_shapes=[pltpu.VMEM((BT, grp * HEAD), F32), pltpu.VMEM((SB + BT, HEAD), F32),
                        pltpu.VMEM((SB + BT, HEAD), F32)] + own,
        compiler_params=_params("parallel", "arbitrary"),
    )(sinks, qkv, qkv, qkv, qkv, qkv, *others)


def band_bwd(qkv, dqkv, do, o, lse, cos, sin_signed, sinks, *, r, base, hkv, grp, max_dist, name):
    T, W = qkv.shape
    SB = HEAD * r
    BT = min(max(2048, 2 * SB), T)
    nsub, nib = BT // SB, T // BT
    nblk = T // SB
    with_sink = sinks is not None
    heads = [slice(g * HEAD, (g + 1) * HEAD) for g in range(grp)]

    def body(*refs):
        if with_sink:
            sink_ref, refs = refs[0], refs[1:]
        (q_ref, kc_ref, kp_ref, vc_ref, vp_ref, qn_ref, do_ref, don_ref, o_ref, on_ref, l_ref, ln_ref,
         c_ref, s_ref, _) = refs[:15]
        out_ref = refs[15]
        ds_ref = refs[16] if with_sink else None
        qf, dof, of, kf, vf, dqf, dkf, dvf = refs[-8:]
        kvh, ib = pl.program_id(0), pl.program_id(1)
        for buf, cur_ref, nxt_ref in ((qf, q_ref, qn_ref), (dof, do_ref, don_ref), (of, o_ref, on_ref)):
            buf[:BT] = cur_ref[...].astype(F32)
            buf[BT:] = nxt_ref[...].astype(F32)
        kf[:SB] = kp_ref[...].astype(F32)
        kf[SB:] = kc_ref[...].astype(F32)
        vf[:SB] = vp_ref[...].astype(F32)
        vf[SB:] = vc_ref[...].astype(F32)
        band, band_first = _band_mask(max_dist, ib > 0)
        if with_sink:
            @pl.when(ib == 0)
            def _():
                ds_ref[...] = jnp.zeros_like(ds_ref)

        def grads(rows, logzs, keys, vals, mask):
            q = _stack([qf[rows, cols] for cols in heads]).astype(BF16)
            dout = _stack([dof[rows, cols] for cols in heads]).astype(BF16)
            s_all = _dot_nt(q, keys) * ATT_SCALE
            dp_all = _dot_nt(dout, vals)
            probs, dss, deltas = [], [], []
            for g, cols in enumerate(heads):
                delta = jnp.sum(dof[rows, cols] * of[rows, cols], axis=-1, keepdims=True)
                p = jnp.exp(jnp.where(mask, s_all[cols], NEG_INF) - logzs[g][:, :1])
                probs.append(p.astype(BF16))
                dss.append((p * (dp_all[cols] - delta) * ATT_SCALE).astype(BF16))
                deltas.append(delta)
            return q, dout, _stack(probs), _stack(dss), deltas

        row = lax.broadcasted_iota(jnp.int32, (HEAD, HEAD), 0)
        col = lax.broadcasted_iota(jnp.int32, (HEAD, HEAD), 1)
        reach = col >= row + jnp.where(ib < nib - 1, HEAD - max_dist, 2 * HEAD)
        for c in range(r):
            k_old, v_old = kf[_band_rows(c, r)], vf[_band_rows(c, r)]
            dk_own = dv_own = None
            for j in range(nsub):
                rows = _band_rows(j * SB + c, r)
                k_own, v_own = kf[_band_rows((j + 1) * SB + c, r)], vf[_band_rows((j + 1) * SB + c, r)]
                kcat = jnp.concatenate([k_old, k_own], axis=0).astype(BF16)
                vcat = jnp.concatenate([v_old, v_own], axis=0).astype(BF16)
                logzs = [l_ref[rows, cols] for cols in heads]
                q, dout, p, ds, deltas = grads(rows, logzs, kcat, vcat, band_first if j == 0 else band)
                dq = _dot(ds, kcat)
                for g, cols in enumerate(heads):
                    dqf[rows, cols] = dq[cols]
                    if with_sink:
                        p_sink = jnp.exp(sink_ref[kvh * grp + g] - logzs[g][:, :1])
                        ds_ref[g * 8:(g + 1) * 8] += jnp.sum(p_sink * deltas[g])
                dk, dv = _dot_tn(ds, q), _dot_tn(p, dout)
                if j > 0:
                    done = _band_rows((j - 1) * SB + c, r)
                    dkf[done] = dk_own + dk[:HEAD]
                    dvf[done] = dv_own + dv[:HEAD]
                dk_own, dv_own = dk[HEAD:], dv[HEAD:]
                k_old, v_old = k_own, v_own
            logzs = [ln_ref[_band_rows(c, r), cols] for cols in heads]
            q, dout, p, ds, _ = grads(_band_rows(BT + c, r), logzs, k_old.astype(BF16), v_old.astype(BF16), reach)
            done = _band_rows((nsub - 1) * SB + c, r)
            dkf[done] = dk_own + _dot_tn(ds, q)
            dvf[done] = dv_own + _dot_tn(p, dout)

        cs, sn = c_ref[...], s_ref[...]
        for cols in heads:
            out_ref[:, cols] = _unrope(dqf[:, cols], cs, sn).astype(BF16)
        out_ref[:, grp * HEAD:(grp + 1) * HEAD] = _unrope(dkf[...], cs, sn).astype(BF16)
        out_ref[:, (grp + 1) * HEAD:] = dvf[...].astype(BF16)

    def nxt_row(i):
        return jnp.minimum((i + 1) * nsub, nblk - 1)

    stride = grp + 2
    q_next = pl.BlockSpec((SB, grp * HEAD), lambda h, i: (nxt_row(i), (base + h * stride) // grp))
    head_cur = pl.BlockSpec((BT, grp * HEAD), lambda h, i: (i, h))
    head_next = pl.BlockSpec((SB, grp * HEAD), lambda h, i: (nxt_row(i), h))
    table = pl.BlockSpec((BT, HEAD), lambda h, i: (i, 0))

    in_specs = [*_band_specs(BT, SB, nsub, base, grp), q_next,
                head_cur, head_next, head_cur, head_next, head_cur, head_next, table, table, UNREAD]
    args = [qkv, qkv, qkv, qkv, qkv, qkv, do, do, o, o, lse, lse, cos, sin_signed, dqkv]
    out_specs = [pl.BlockSpec((BT, stride * HEAD), lambda h, i: (i, base // stride + h))]
    out_shape = [jax.ShapeDtypeStruct(dqkv.shape, dqkv.dtype)]
    if with_sink:
        in_specs.insert(0, pl.BlockSpec(memory_space=pltpu.SMEM))
        args.insert(0, sinks)
        out_specs.append(pl.BlockSpec((None, grp * 8, HEAD), lambda h, i: (h, 0, 0)))
        out_shape.append(jax.ShapeDtypeStruct((hkv, grp * 8, HEAD), F32))
    wide = pltpu.VMEM((BT + SB, grp * HEAD), F32)
    tall = pltpu.VMEM((SB + BT, HEAD), F32)
    grad = pltpu.VMEM((BT, HEAD), F32)
    return pl.pallas_call(
        body, name=name, grid=(hkv, nib), in_specs=in_specs, out_specs=out_specs, out_shape=out_shape,
        input_output_aliases={len(args) - 1: 0},
        scratch_shapes=[wide, wide, wide, tall, tall, pltpu.VMEM((BT, grp * HEAD), F32), grad, grad],
        compiler_params=pltpu.CompilerParams(dimension_semantics=("parallel", "arbitrary"),
                                             vmem_limit_bytes=VMEM_LIMIT_LARGE),
    )(*args)


M_HEADS = 4


def mem_kv(mem, g, w, name):
    n, D = mem.shape

    def body(m_ref, g_ref, w_ref, mn_ref, kv_ref):
        x = m_ref[...]
        mn = (x * _rstd(x) * g_ref[...]).astype(BF16)
        mn_ref[...] = mn
        kv_ref[...] = _dot(mn, w_ref[...]).astype(BF16)

    return pl.pallas_call(
        body, name=name,
        out_shape=[jax.ShapeDtypeStruct((n, D), BF16), jax.ShapeDtypeStruct((n, w.shape[1]), BF16)],
        compiler_params=pltpu.CompilerParams(vmem_limit_bytes=VMEM_LIMIT),
    )(mem, g, w)


def mem_fwd(qkv, mkv, name):
    T = qkv.shape[0]
    n = mkv.shape[0]
    RB = 1024

    def body(q_ref, kv_ref, o_ref, l_ref):
        for h in range(M_HEADS):
            cols = slice(h * HEAD, (h + 1) * HEAD)
            s = _dot_nt(q_ref[:, cols], kv_ref[:, cols]) * ATT_SCALE
            m = jnp.max(s, axis=-1, keepdims=True)
            p = jnp.exp(s - m)
            den = jnp.sum(p, axis=-1, keepdims=True)
            vals = kv_ref[:, (M_HEADS + h) * HEAD:(M_HEADS + h + 1) * HEAD]
            o_ref[:, cols] = (_dot(p.astype(BF16), vals) / den).astype(BF16)
            l_ref[:, cols] = jnp.broadcast_to(m + jnp.log(den), (RB, HEAD))

    out = pl.BlockSpec((RB, M_HEADS * HEAD), lambda i: (i, 0))
    return pl.pallas_call(
        body, name=name, grid=(T // RB,),
        in_specs=[pl.BlockSpec((RB, M_HEADS * HEAD), lambda i: (i, MQ // M_HEADS)), _resident(mkv)],
        out_specs=[out, out],
        out_shape=[jax.ShapeDtypeStruct((T, M_HEADS * HEAD), BF16), jax.ShapeDtypeStruct((T, M_HEADS * HEAD), F32)],
        compiler_params=_params("parallel"),
    )(qkv, mkv)


def mem_bwd(qkv, dqkv, mkv, do, o, lse, name):
    T = qkv.shape[0]
    n = mkv.shape[0]
    RB = 1024

    def body(q_ref, kv_ref, do_ref, o_ref, l_ref, _, dq_ref, dk_ref, dv_ref):
        @pl.when(pl.program_id(0) == 0)
        def _():
            dk_ref[...] = jnp.zeros_like(dk_ref)
            dv_ref[...] = jnp.zeros_like(dv_ref)

        for h in range(M_HEADS):
            cols = slice(h * HEAD, (h + 1) * HEAD)
            keys, vals = kv_ref[:, cols], kv_ref[:, (M_HEADS + h) * HEAD:(M_HEADS + h + 1) * HEAD]
            q, dout = q_ref[:, cols], do_ref[:, cols]
            delta = jnp.sum(dout.astype(F32) * o_ref[:, cols].astype(F32), axis=-1, keepdims=True)
            p = jnp.exp(_dot_nt(q, keys) * ATT_SCALE - l_ref[:, cols][:, :1])
            ds = (p * (_dot_nt(dout, vals) - delta) * ATT_SCALE).astype(BF16)
            dq_ref[:, cols] = _dot(ds, keys).astype(BF16)
            dk_ref[:, cols] += _dot_tn(ds, q)
            dv_ref[:, cols] += _dot_tn(p.astype(BF16), dout)

    wide = M_HEADS * HEAD
    tok = pl.BlockSpec((RB, wide), lambda i: (i, 0))
    q_cols = pl.BlockSpec((RB, wide), lambda i: (i, MQ // M_HEADS))
    slot = pl.BlockSpec((n, wide), lambda i: (0, 0))
    return pl.pallas_call(
        body, name=name, grid=(T // RB,),
        in_specs=[q_cols, _resident(mkv), tok, tok, tok, UNREAD],
        out_specs=[q_cols, slot, slot],
        out_shape=[jax.ShapeDtypeStruct(dqkv.shape, dqkv.dtype),
                   jax.ShapeDtypeStruct((n, wide), F32), jax.ShapeDtypeStruct((n, wide), F32)],
        input_output_aliases={5: 0},
        compiler_params=_params("arbitrary"),
    )(qkv, mkv, do, o, lse, dqkv)


def mem_kv_bwd(mem, g, mem_n, w, dmkv, name):
    n, D = mem.shape

    def body(m_ref, g_ref, mn_ref, w_ref, d_ref, dw_ref, dg_ref):
        d = d_ref[...].astype(BF16)
        dw_ref[...] = _dot_tn(mn_ref[...], d)
        x = m_ref[...]
        dg_ref[...] = jnp.sum(_dot_nt(d, w_ref[...]) * (x * _rstd(x)), axis=0, keepdims=True)

    return pl.pallas_call(
        body, name=name,
        out_shape=[jax.ShapeDtypeStruct(w.shape, F32), jax.ShapeDtypeStruct((1, D), F32)],
        compiler_params=pltpu.CompilerParams(vmem_limit_bytes=VMEM_LIMIT),
    )(mem, g, mem_n, w, dmkv)


def _rms_bwd(dn, f, g):
    r = _rstd(f)
    fhat = f * r
    dfhat = dn * g
    df = r * (dfhat - fhat * jnp.mean(dfhat * fhat, axis=-1, keepdims=True))
    return df, jnp.sum(dn * fhat, axis=0, keepdims=True)


def ffn_tokens_bwd(dh, f, h_in, gu, g_pre, g_post, w_in, w_out, coef, name, after):
    T, D = dh.shape

    def body(dh_ref, f_ref, h_ref, gu_ref, gpre_ref, gpost_ref, win_ref, wout_ref, _,
             df_ref, dgu_ref, dhin_ref, dgpre_ref, dgpost_ref, dxn_ref):
        i, j = pl.program_id(0), pl.program_id(1)

        @pl.when(j == 0)
        def _():
            @pl.when(i == 0)
            def _():
                dgpre_ref[...] = jnp.zeros_like(dgpre_ref)
                dgpost_ref[...] = jnp.zeros_like(dgpost_ref)

            df, dg_post = _rms_bwd(coef * dh_ref[...], f_ref[...], gpost_ref[...])
            dgpost_ref[...] += dg_post
            df_ref[...] = df.astype(BF16)

        for jj in range(2):
            @pl.when(j == jj)
            def _(jj=jj):
                lo, mid, hi = 2 * jj * FF_T, (2 * jj + 1) * FF_T, (2 * jj + 2) * FF_T
                da = _dot_nt(df_ref[...], wout_ref[jj * FF_T:(jj + 1) * FF_T, :])
                gate = gu_ref[:, :FF_T].astype(F32)
                up = gu_ref[:, FF_T:].astype(F32)
                sig = _sigmoid(gate)
                dgate = (da * up * sig * (1.0 + gate * (1.0 - sig))).astype(BF16)
                dup = (da * gate * sig).astype(BF16)
                dgu_ref[:, :FF_T] = dgate
                dgu_ref[:, FF_T:] = dup
                part = _dot_nt(dgate, win_ref[:, lo:mid]) + _dot_nt(dup, win_ref[:, mid:hi])
                if jj == 0:
                    dxn_ref[...] = part
                else:
                    h = h_ref[...]
                    r = _rstd(h)
                    xhat = h * r
                    dxn = dxn_ref[...] + part
                    dxhat = dxn * gpre_ref[...]
                    dhin_ref[...] = dh_ref[...] + r * (dxhat - xhat * jnp.mean(dxhat * xhat, axis=-1, keepdims=True))
                    dgpre_ref[...] += jnp.sum(dxn * xhat, axis=0, keepdims=True)

    row = pl.BlockSpec((TM, D), lambda i, j: (i, 0))
    wide = pl.BlockSpec((TM, 2 * FF_T), lambda i, j: (i, j))
    vec = pl.BlockSpec((1, D), lambda i, j: (0, 0))
    return pl.pallas_call(
        body, name=name, grid=(T // TM, 2),
        in_specs=[row, row, row, wide, _resident(g_pre), _resident(g_post), _resident(w_in), _resident(w_out),
                  UNREAD],
        out_specs=[row, wide, row, vec, vec],
        out_shape=[jax.ShapeDtypeStruct((T, D), BF16), jax.ShapeDtypeStruct((T, 2 * D_FF), BF16),
                   jax.ShapeDtypeStruct((T, D), F32), jax.ShapeDtypeStruct((1, D), F32),
                   jax.ShapeDtypeStruct((1, D), F32)],
        scratch_shapes=[pltpu.VMEM((TM, D), F32)],
        compiler_params=pltpu.CompilerParams(dimension_semantics=("arbitrary", "arbitrary"),
                                             vmem_limit_bytes=VMEM_LIMIT_LARGE),
    )(dh, f, h_in, gu, g_pre, g_post, w_in, w_out, after)


def mm_nt_norm_bwd(pieces, h_in, dh_out, g, name, after):
    T, D = h_in.shape

    def body(*refs):
        ab = refs[:2 * len(pieces)]
        h_ref, dh_ref, g_ref, _, o_ref, dg_ref = refs[2 * len(pieces):]
        dxn = _dot_nt(ab[0][...], ab[1][...])
        for p in range(1, len(pieces)):
            dxn += _dot_nt(ab[2 * p][...], ab[2 * p + 1][...])
        h = h_ref[...]
        r = _rstd(h)
        xhat = h * r
        dxhat = dxn * g_ref[...]
        o_ref[...] = dh_ref[...] + r * (dxhat - xhat * jnp.mean(dxhat * xhat, axis=-1, keepdims=True))

        @pl.when(pl.program_id(0) == 0)
        def _():
            dg_ref[...] = jnp.zeros_like(dg_ref)

        dg_ref[...] += jnp.sum(dxn * xhat, axis=0, keepdims=True)

    in_specs, args = [], []
    for a, w in pieces:
        in_specs += [pl.BlockSpec((TM, a.shape[1]), lambda i: (i, 0)), _resident(w)]
        args += [a, w]
    row = pl.BlockSpec((TM, D), lambda i: (i, 0))
    return pl.pallas_call(
        body, name=name, grid=(T // TM,),
        in_specs=in_specs + [row, row, _resident(g), UNREAD],
        out_specs=[row, pl.BlockSpec((1, D), lambda i: (0, 0))],
        out_shape=[jax.ShapeDtypeStruct((T, D), F32), jax.ShapeDtypeStruct((1, D), F32)],
        compiler_params=_params("arbitrary"),
    )(*args, h_in, dh_out, g, after)


def gate_merge_out_bwd(dh, f, g, w_out, merged, gt, o_a, o_b, o_m, w_a, w_b, w_m, name, after):
    T = dh.shape[0]
    D = D_MODEL
    branch = ((o_a, w_a), (o_b, w_b), (o_m, w_m))

    def body(dh_ref, f_ref, g_ref, wo_ref, m_ref, gt_ref, oa_ref, ob_ref, om_ref, wa_ref, wb_ref, wm_ref, _,
             dg_ref, dwo_ref, dgt_ref, doa_ref, dob_ref, dom_ref, db_ref, dwa_ref, dwb_ref, dwm_ref):
        @pl.when(pl.program_id(0) == 0)
        def _():
            for acc in (dg_ref, dwo_ref, db_ref, dwa_ref, dwb_ref, dwm_ref):
                acc[...] = jnp.zeros_like(acc)

        df, dg = _rms_bwd(dh_ref[...], f_ref[...], g_ref[...])
        dg_ref[...] += dg
        df = df.astype(BF16)
        dwo_ref[...] += _dot_tn(m_ref[...], df)
        dmf = _dot_nt(df, wo_ref[...])
        for x, (o_ref, w_ref, do_ref, dw_ref) in enumerate(((oa_ref, wa_ref, doa_ref, dwa_ref),
                                                           (ob_ref, wb_ref, dob_ref, dwb_ref),
                                                           (om_ref, wm_ref, dom_ref, dwm_ref))):
            cols = slice(x * D, (x + 1) * D)
            gx = gt_ref[:, cols].astype(F32)
            w = w_ref[...]
            dpre = dmf * _dot(o_ref[...], w) * gx * (1.0 - gx)
            dgt_ref[:, cols] = dpre.astype(BF16)
            db_ref[:, cols] += jnp.sum(dpre, axis=0, keepdims=True)
            dp = (dmf * gx).astype(BF16)
            do_ref[...] = _dot_nt(dp, w).astype(BF16)
            dw_ref[...] += _dot_tn(dp, o_ref[...])

    def rows(width):
        return pl.BlockSpec((TM, width), lambda i: (i, 0))

    def kept(shape):
        return pl.BlockSpec(shape, lambda i: (0,) * len(shape))

    widths = [o.shape[1] for o, _ in branch]
    sums = [(1, D), (D, D), (1, 3 * D)] + [(D, k) for k in widths]
    return pl.pallas_call(
        body, name=name, grid=(T // TM,),
        in_specs=[rows(D), rows(D), _resident(g), _resident(w_out), rows(D), rows(3 * D)]
                 + [rows(k) for k in widths] + [_resident(w) for _, w in branch] + [UNREAD],
        out_specs=[kept(sums[0]), kept(sums[1]), rows(3 * D)] + [rows(k) for k in widths]
                  + [kept(shape) for shape in sums[2:]],
        out_shape=[jax.ShapeDtypeStruct(sums[0], F32), jax.ShapeDtypeStruct(sums[1], F32),
                   jax.ShapeDtypeStruct((T, 3 * D), BF16)] + [jax.ShapeDtypeStruct((T, k), BF16) for k in widths]
                  + [jax.ShapeDtypeStruct(shape, F32) for shape in sums[2:]],
        compiler_params=pltpu.CompilerParams(dimension_semantics=("arbitrary",), vmem_limit_bytes=VMEM_LIMIT_LARGE),
    )(dh, f, g, w_out, merged, gt, o_a, o_b, o_m, w_a, w_b, w_m, after)


def mm_tn(x, dy, tm, tn, name, shard_major=False, perm=None, slabs=1, after=None, wire=False):
    T, M = x.shape
    N = dy.shape[1]
    tk = min(2048, T)
    perm = perm or (lambda j: j)
    w = tn // slabs

    def body(x_ref, dy_ref, *rest):
        o_ref = rest[-2] if wire else rest[-1]

        @pl.when(pl.program_id(2) == 0)
        def _():
            o_ref[...] = jnp.zeros_like(o_ref)

        acc = _dot_tn(x_ref[...], dy_ref[...])
        if shard_major:
            for s in range(slabs):
                o_ref[s] += acc[:, s * w:(s + 1) * w]
        else:
            o_ref[...] += acc
        if wire:
            @pl.when(pl.program_id(2) == T // tk - 1)
            def _():
                rest[-1][...] = o_ref[...].astype(BF16)

    if shard_major:
        out_spec = pl.BlockSpec((slabs, tm, w), lambda i, j, k: (perm(j), i, 0))
        out_shape = jax.ShapeDtypeStruct((N // w, M, w), F32)
    else:
        out_spec = pl.BlockSpec((tm, tn), lambda i, j, k: (i, j))
        out_shape = jax.ShapeDtypeStruct((M, N), F32)
    return pl.pallas_call(
        body, name=name, grid=(M // tm, N // tn, T // tk),
        in_specs=[pl.BlockSpec((tk, tm), lambda i, j, k: (k, i)),
                  pl.BlockSpec((tk, tn), lambda i, j, k: (k, j))] + ([] if after is None else [UNREAD]),
        out_specs=[out_spec, out_spec] if wire else out_spec,
        out_shape=[out_shape, jax.ShapeDtypeStruct(out_shape.shape, BF16)] if wire else out_shape,
        compiler_params=_params("parallel", "parallel", "arbitrary"),
    )(x, dy, *([] if after is None else [after]))


def rope_tables(T, zero):
    half = HEAD // 2
    inv = ROPE_THETA ** (-jnp.arange(half, dtype=F32) / half)
    ang = (jnp.arange(T).astype(F32) + zero)[:, None] * inv[None, :]
    cos, sin = jnp.cos(ang), jnp.sin(ang)
    return jnp.concatenate([cos, cos], axis=1), jnp.concatenate([-sin, sin], axis=1)


def layer_step(x, mem, target, gains, sinks, b_gate, weights_of, send_grads, zero):
    T = x.shape[0]
    cos, sin_signed = rope_tables(T, zero)
    no_sink = jnp.full((2,), NEG_INF, F32)

    xn1 = rms_scale(x, gains["ffn1_norm_pre"], "ffn1_norm", cos)
    w = dict(weights_of("ffn1_in", xn1))
    xn1, gu1, a1 = ffn_in(x, gains["ffn1_norm_pre"], w["ffn1_w_in"], "ffn1_in", xn=xn1)
    w.update(weights_of("ffn1_out", xn1))
    f1, h1 = mm_norm_res(a1, w["ffn1_w_out"], x, gains["ffn1_norm_post"], 0.5, "ffn1_out")
    w.update(weights_of("mix", f1))
    u, qkv, gt = mix_in(h1, gains["mix_norm_pre"], w["w_in"], w["w_gate"], b_gate, cos, sin_signed, "mix_in")
    outs, lses = [], []
    for gidx, (window, dil) in enumerate(DIL):
        last = gidx == len(DIL) - 1
        o_g, l_g = band_fwd(qkv, no_sink, r=dil, base=A_BASE + 6 * gidx, hkv=2, grp=1, max_dist=window // dil,
                            out_dtype=BF16 if last else F32, name=f"attn_a{gidx}_fwd",
                            merge=(outs, lses) if last else None)
        outs.append(o_g)
        lses.append(l_g)
    o_a, l_a = outs[-1], lses[-1]
    o_b, l_b = band_fwd(qkv, sinks, r=1, base=B_BASE, hkv=2, grp=2, max_dist=HEAD - 1, out_dtype=BF16,
                        name="attn_b_fwd")
    mem_n, mkv = mem_kv(mem, gains["mem_norm"], w["w_mem_kv"], "mem_kv")
    o_m, l_m = mem_fwd(qkv, mkv, "attn_m_fwd")
    merged, mo, h2 = gate_merge_out(gt, o_a, o_b, o_m, w["w_o_a"], w["w_o_b"], w["w_o_m"], w["w_out"], h1,
                                    gains["mix_norm_post"], "gate_merge_out")
    w.update(weights_of("ffn2", mo))
    xn2, gu2, a2 = ffn_in(h2, gains["ffn2_norm_pre"], w["ffn2_w_in"], "ffn2_in")
    f2, dy, sq = mm_norm_res(a2, w["ffn2_w_out"], h2, gains["ffn2_norm_post"], 0.5, "ffn2_out", target=target)

    grads = {}

    def ffn_bwd(tag, dh_out, f, gu, a, xn, h_in, after):
        df, dgu, dh_in, grads[f"{tag}_norm_pre"], grads[f"{tag}_norm_post"] = ffn_tokens_bwd(
            dh_out, f, h_in, gu, gains[f"{tag}_norm_pre"], gains[f"{tag}_norm_post"], w[f"{tag}_w_in"],
            w[f"{tag}_w_out"], 0.5, f"{tag}_tokens_bwd", after)
        sent = send_grads(f"{tag}_in", {f"{tag}_w_in": mm_tn(
            xn, dgu, D_MODEL, FF_T, f"{tag}_w_in_grad", shard_major=True, perm=_ffn_perm, wire=True)})
        sent = send_grads(f"{tag}_out", {f"{tag}_w_out": mm_tn(
            a, df, FF_T, D_MODEL, f"{tag}_w_out_grad", after=sent, wire=True)})
        return dh_in, sent

    dh2, sent = ffn_bwd("ffn2", dy, f2, gu2, a2, xn2, h2, dy)

    mix = {}
    (grads["mix_norm_post"], mix["w_out"], dgt, do_a, do_b, do_m, grads["b_gate"],
     dwa_t, dwb_t, dwm_t) = gate_merge_out_bwd(
        dh2, mo, gains["mix_norm_post"], w["w_out"], merged, gt, o_a, o_b, o_m, w["w_o_a"], w["w_o_b"],
        w["w_o_m"], "gate_merge_out_bwd", sent)
    mix["w_o_a"], mix["w_o_b"], mix["w_o_m"] = dwa_t.T, dwb_t.T, dwm_t.T

    dqkv = lax.empty(qkv.shape, qkv.dtype)
    for gidx, (window, dil) in enumerate(DIL):
        dqkv, = band_bwd(qkv, dqkv, do_a, o_a, l_a, cos, sin_signed, None, r=dil, base=A_BASE + 6 * gidx, hkv=2,
                         grp=1, max_dist=window // dil, name=f"attn_a{gidx}_bwd")
    dqkv, dsink = band_bwd(qkv, dqkv, do_b, o_b, l_b, cos, sin_signed, sinks, r=1, base=B_BASE, hkv=2, grp=2,
                           max_dist=HEAD - 1, name="attn_b_bwd")
    grads["sinks"] = -dsink[:, ::8, 0].reshape(1, 4)
    dqkv, dmk, dmv = mem_bwd(qkv, dqkv, mkv, do_m, o_m, l_m, "attn_m_bwd")
    mix["w_mem_kv"], grads["mem_norm"] = mem_kv_bwd(
        mem, gains["mem_norm"], mem_n, w["w_mem_kv"], jnp.concatenate([dmk, dmv], axis=1), "mem_kv_bwd")

    mix["w_in"] = mm_tn(u, dqkv, D_MODEL, 1280, "w_in_grad")
    mix["w_gate"] = mm_tn(u, dgt, D_MODEL, 1536, "w_gate_grad", shard_major=True, slabs=2, wire=True)
    sent = send_grads("mix", mix)
    dh1, grads["mix_norm_pre"] = mm_nt_norm_bwd(
        [(dqkv, w["w_in"]), (dgt, w["w_gate"])], h1, dh2, gains["mix_norm_pre"], "mix_in_bwd", sent)

    dx, _ = ffn_bwd("ffn1", dh1, f1, gu1, a1, xn1, x, dh1)
    return sq, dx, grads


def _place():
    return lax.axis_index("x"), lax.axis_index("y"), lax.axis_index("c")


def _other_chips(x, y):
    return [(1 - x, y), (x, 1 - y), (1 - x, 1 - y)]


def _hbm(n):
    return [pl.BlockSpec(memory_space=pltpu.HBM)] * n


SEM = pl.BlockSpec(memory_space=pltpu.SEMAPHORE)
SIDE_EFFECT = pltpu.SideEffectType.DATAFLOW_SIDE_EFFECTING


def _chip_copy(src, land, sems, i, j, dst_slot, scatter):
    x, y, c = _place()
    px, py = _other_chips(x, y)[j]
    send_sems, recv_sems = sems
    return pltpu.make_async_remote_copy(
        src_ref=src[i].at[2 * px + py] if scatter else src[i], dst_ref=land[i].at[dst_slot],
        send_sem=send_sems.at[3 * i + j], recv_sem=recv_sems.at[3 * i + j],
        device_id=(px, py, c), device_id_type=MESH)


def chip_copies_start(srcs, lands, groups, scatter, name, after=None):
    n = len(srcs)

    def body(*refs):
        src, land = refs[:n], refs[n:2 * n]
        first_sem = 2 * n + (after is not None)
        sems = refs[first_sem:first_sem + 2 * len(groups)]
        token = refs[-1]
        x, y, _ = _place()
        for g, members in enumerate(groups):
            part = ([src[i] for i in members], [land[i] for i in members])
            for t in range(len(members)):
                for j in range(3):
                    _chip_copy(*part, sems[2 * g:2 * g + 2], t, j, 2 * x + y, scatter).start()
        token[...] = jnp.zeros_like(token)

    sem_shapes = [pltpu.SemaphoreType.DMA((3 * len(m),)) for m in groups for _ in range(2)]
    thru = [pltpu.HBM(a.shape, a.dtype) for a in (*srcs, *lands)]
    res = pl.pallas_call(
        body, name=name,
        out_shape=(*sem_shapes, *thru, jax.ShapeDtypeStruct((8, 128), F32)),
        in_specs=_hbm(2 * n) + ([] if after is None else [UNREAD]),
        out_specs=(*[SEM] * len(sem_shapes), *_hbm(2 * n), pl.BlockSpec(memory_space=pltpu.VMEM)),
        input_output_aliases={i: len(sem_shapes) + i for i in range(2 * n)},
        compiler_params=pltpu.CompilerParams(has_side_effects=SIDE_EFFECT),
    )(*[pltpu.with_memory_space_constraint(a, pltpu.HBM) for a in (*srcs, *lands)],
      *([] if after is None else [after]))
    k = len(sem_shapes)
    sems = [tuple(res[2 * g:2 * g + 2]) for g in range(len(groups))]
    return sems, list(res[k:k + n]), list(res[k + n:k + 2 * n]), res[-1]


def chip_copies_wait(srcs, lands, sems, after, scatter, name):
    n = len(srcs)
    after = list(after) if isinstance(after, (list, tuple)) else [after]

    def body(*refs):
        src, land = refs[:n], refs[n:2 * n]
        pair = refs[2 * n:2 * n + 2]
        x, y, _ = _place()
        for i in range(n):
            for j, (px, py) in enumerate(_other_chips(x, y)):
                copy = _chip_copy(src, land, pair, i, j, 2 * px + py, scatter)
                copy.wait_send()
                copy.wait_recv()

    res = pl.pallas_call(
        body, name=name,
        out_shape=[pltpu.HBM(a.shape, a.dtype) for a in (*srcs, *lands)],
        in_specs=[*_hbm(2 * n), SEM, SEM] + [UNREAD] * len(after),
        out_specs=_hbm(2 * n),
        input_output_aliases={i: i for i in range(2 * n)},
        compiler_params=pltpu.CompilerParams(has_side_effects=SIDE_EFFECT),
    )(*srcs, *lands, *sems, *after)
    return list(res[n:])


def small_all_gather(small, name):
    flips = [(fx, fy, fc) for fx in (0, 1) for fy in (0, 1) for fc in (0, 1)][1:]

    def body(in_ref, out_ref, send_sems, recv_sems, local_sem):
        x, y, c = _place()
        me = 4 * x + 2 * y + c

        def copy(k, slot):
            fx, fy, fc = flips[k]
            return pltpu.make_async_remote_copy(
                src_ref=in_ref, dst_ref=out_ref.at[slot], send_sem=send_sems.at[k], recv_sem=recv_sems.at[k],
                device_id=(x ^ fx, y ^ fy, c ^ fc), device_id_type=MESH)

        local = pltpu.make_async_copy(in_ref, out_ref.at[me], local_sem)
        local.start()
        for k in range(len(flips)):
            copy(k, me).start()
        for k, (fx, fy, fc) in enumerate(flips):
            copy(k, 4 * (x ^ fx) + 2 * (y ^ fy) + (c ^ fc)).wait()
        local.wait()

    return pl.pallas_call(
        body, name=name, in_specs=_hbm(1), out_specs=_hbm(1)[0],
        out_shape=jax.ShapeDtypeStruct((N_DEV,) + small.shape, small.dtype),
        scratch_shapes=[pltpu.SemaphoreType.DMA((len(flips),)), pltpu.SemaphoreType.DMA((len(flips),)),
                        pltpu.SemaphoreType.DMA],
    )(small)


def _sibling_copy(src, land, sems, i):
    x, y, c = _place()
    return pltpu.make_async_remote_copy(
        src_ref=src[i], dst_ref=land[i], send_sem=sems[0].at[i], recv_sem=sems[1].at[i],
        device_id=(x, y, 1 - c), device_id_type=MESH)


def sibling_copies_start(parts, name):
    n = len(parts)
    lands = [lax.empty(p.shape, p.dtype) for p in parts]

    def body(*refs):
        src, land, sems, token = refs[:n], refs[n:2 * n], refs[2 * n:2 * n + 2], refs[-1]
        for i in range(n):
            _sibling_copy(src, land, sems, i).start()
        token[...] = jnp.zeros_like(token)

    res = pl.pallas_call(
        body, name=name,
        out_shape=(pltpu.SemaphoreType.DMA((n,)), pltpu.SemaphoreType.DMA((n,)),
                   *[pltpu.HBM(a.shape, a.dtype) for a in (*parts, *lands)], jax.ShapeDtypeStruct((8, 128), F32)),
        in_specs=_hbm(2 * n),
        out_specs=(SEM, SEM, *_hbm(2 * n), pl.BlockSpec(memory_space=pltpu.VMEM)),
        input_output_aliases={i: 2 + i for i in range(2 * n)},
        compiler_params=pltpu.CompilerParams(has_side_effects=SIDE_EFFECT),
    )(*[pltpu.with_memory_space_constraint(a, pltpu.HBM) for a in (*parts, *lands)])
    return tuple(res[:2]), list(res[2:2 + n]), list(res[2 + n:2 + 2 * n]), res[-1]


def sibling_copies_wait(parts, lands, sems, after, name):
    n = len(parts)

    def body(*refs):
        src, land, sems = refs[:n], refs[n:2 * n], refs[2 * n:2 * n + 2]
        for i in range(n):
            copy = _sibling_copy(src, land, sems, i)
            copy.wait_send()
            copy.wait_recv()

    res = pl.pallas_call(
        body, name=name,
        out_shape=[pltpu.HBM(a.shape, a.dtype) for a in (*parts, *lands)],
        in_specs=[*_hbm(2 * n), SEM, SEM, UNREAD],
        out_specs=_hbm(2 * n),
        input_output_aliases={i: i for i in range(2 * n)},
        compiler_params=pltpu.CompilerParams(has_side_effects=SIDE_EFFECT),
    )(*parts, *lands, *sems, after)
    return list(res[n:])


def _row_tile(rows):
    for t in (256, 176, 128, 64, 32, 16, 8):
        if rows % t == 0:
            return t
    return rows


def chip_partial_sum(me, own_sm, recv, name):
    _, rows, cols = own_sm.shape
    tr = _row_tile(rows)

    def body(me_ref, own_ref, r0, r1, r2, r3, o_ref):
        acc = jnp.zeros((tr, cols), F32)
        for s, r_ref in enumerate((r0, r1, r2, r3)):
            acc = acc + jnp.where(me_ref[0] == s, own_ref[...], r_ref[...].astype(F32))
        o_ref[...] = acc

    def slot(s):
        return pl.BlockSpec((None, tr, cols), lambda i, me_ref, s=s: (s, i, 0))

    return pl.pallas_call(
        body, name=name,
        grid_spec=pltpu.PrefetchScalarGridSpec(
            num_scalar_prefetch=1, grid=(rows // tr,),
            in_specs=[pl.BlockSpec((None, tr, cols), lambda i, me_ref: (me_ref[0], i, 0))] + [slot(s) for s in range(4)],
            out_specs=pl.BlockSpec((tr, cols), lambda i, me_ref: (i, 0))),
        out_shape=jax.ShapeDtypeStruct((rows, cols), F32),
        compiler_params=_params("parallel"),
    )(me, own_sm, recv, recv, recv, recv)


def _adamw(w, g, m, v):
    m = ADAM_B1 * m + (1.0 - ADAM_B1) * g
    v = ADAM_B2 * v + (1.0 - ADAM_B2) * (g * g)
    m_hat = m / (1.0 - ADAM_B1 ** ADAM_STEP)
    v_hat = v / (1.0 - ADAM_B2 ** ADAM_STEP)
    delta = -ADAM_LR * (m_hat / (jnp.sqrt(v_hat) + ADAM_EPS) + ADAM_WD * w)
    return delta, m, v


def adamw_pair(part, sib, w, m, v, name):
    rows, cols = w.shape
    tr = _row_tile(rows)

    def body(p_ref, s_ref, w_ref, m_ref, v_ref, g_ref, d_ref, nm_ref, nv_ref):
        g = p_ref[...] + s_ref[...]
        g_ref[...] = g
        d_ref[...], nm_ref[...], nv_ref[...] = _adamw(w_ref[...], g, m_ref[...], v_ref[...])

    spec = pl.BlockSpec((tr, cols), lambda i: (i, 0))
    return pl.pallas_call(
        body, name=name, grid=(rows // tr,), in_specs=[spec] * 5, out_specs=[spec] * 4,
        out_shape=[jax.ShapeDtypeStruct((rows, cols), F32)] * 4,
        compiler_params=_params("parallel"),
    )(part, sib, w, m, v)


def adamw_small(g_all, w, m, v, name):
    def body(ga_ref, w_ref, m_ref, v_ref, g_ref, d_ref, nm_ref, nv_ref):
        g = ga_ref[0]
        for k in range(1, N_DEV):
            g = g + ga_ref[k]
        g_ref[...] = g
        d_ref[...], nm_ref[...], nv_ref[...] = _adamw(w_ref[...], g, m_ref[...], v_ref[...])

    return pl.pallas_call(
        body, name=name, out_shape=[jax.ShapeDtypeStruct(w.shape, F32)] * 4,
    )(g_all, w, m, v)


WEIGHTS = ("ffn1_norm_pre", "ffn1_w_in", "ffn1_w_out", "ffn1_norm_post", "mix_norm_pre", "w_in", "sinks",
           "mem_norm", "w_mem_kv", "w_gate", "b_gate", "w_o_a", "w_o_b", "w_o_m", "w_out", "mix_norm_post",
           "ffn2_norm_pre", "ffn2_w_in", "ffn2_w_out", "ffn2_norm_post")
GATHER_STAGES = (("ffn1_in", "ffn1_out"), ("mix",), ("ffn2",))
GATHER_GROUPS = {"ffn1_in": ("ffn1_w_in",), "ffn1_out": ("ffn1_w_out",),
                 "mix": ("w_in", "w_gate", "w_mem_kv", "w_o_a", "w_o_b", "w_o_m", "w_out"),
                 "ffn2": ("ffn2_w_in", "ffn2_w_out")}
GROUPS = {"ffn1_in": ("ffn1_w_in",), "ffn1_out": ("ffn1_w_out",),
          "mix": ("w_in", "w_gate", "w_mem_kv", "w_o_a", "w_o_b", "w_o_m", "w_out"),
          "ffn2_in": ("ffn2_w_in",), "ffn2_out": ("ffn2_w_out",)}
COLUMN_SHARDED = ("ffn1_w_in", "ffn2_w_in", "w_in", "w_gate", "w_o_a", "w_o_b", "w_o_m")
KEPT_SHARD_MAJOR = ("ffn1_w_in", "ffn2_w_in", "w_gate")
GAINS = ("ffn1_norm_pre", "ffn1_norm_post", "mix_norm_pre", "mem_norm", "mix_norm_post", "ffn2_norm_pre",
         "ffn2_norm_post")
SMALL_ROWS = 16


def _pack_small(t):
    sinks = jnp.pad(t["sinks"], ((0, 0), (0, D_MODEL - t["sinks"].shape[1])))
    rows = [t[k] for k in GAINS] + [t["b_gate"].reshape(3, D_MODEL), sinks]
    packed = jnp.concatenate(rows, axis=0)
    return jnp.pad(packed, ((0, SMALL_ROWS - packed.shape[0]), (0, 0)))


def _unpack_small(p):
    out = {k: p[i:i + 1] for i, k in enumerate(GAINS)}
    out["b_gate"] = p[7:10].reshape(1, 3 * D_MODEL)
    out["sinks"] = p[10:11, :4]
    return out


def kernel(x, mem, ffn1_norm_pre, ffn1_w_in, ffn1_w_out, ffn1_norm_post, mix_norm_pre, w_in, sinks, mem_norm, w_mem_kv, w_gate, b_gate, w_o_a, w_o_b, w_o_m, w_out, mix_norm_post, ffn2_norm_pre, ffn2_w_in, ffn2_w_out, ffn2_norm_post, loss_target, m_ffn1_norm_pre, m_ffn1_w_in, m_ffn1_w_out, m_ffn1_norm_post, m_mix_norm_pre, m_w_in, m_sinks, m_mem_norm, m_w_mem_kv, m_w_gate, m_b_gate, m_w_o_a, m_w_o_b, m_w_o_m, m_w_out, m_mix_norm_post, m_ffn2_norm_pre, m_ffn2_w_in, m_ffn2_w_out, m_ffn2_norm_post, v_ffn1_norm_pre, v_ffn1_w_in, v_ffn1_w_out, v_ffn1_norm_post, v_mix_norm_pre, v_w_in, v_sinks, v_mem_norm, v_w_mem_kv, v_w_gate, v_b_gate, v_w_o_a, v_w_o_b, v_w_o_m, v_w_out, v_mix_norm_post, v_ffn2_norm_pre, v_ffn2_w_in, v_ffn2_w_out, v_ffn2_norm_post):
    given = dict(locals())
    wt = {k: given[k] for k in WEIGHTS}
    mom = {k: given["m_" + k] for k in WEIGHTS}
    var = {k: given["v_" + k] for k in WEIGHTS}
    chip = (2 * lax.axis_index("x") + lax.axis_index("y")).astype(jnp.int32)
    me = chip.reshape(1)

    def landing_zone(own):
        return lax.dynamic_update_slice_in_dim(lax.empty((N_CHIPS,) + own.shape, own.dtype), own[None], chip, 0)

    started = {}
    tokens = []

    def stage_keys(stage):
        return [k for g in GATHER_STAGES[stage] for k in GATHER_GROUPS[g]]

    def prepare(stage):
        shards = [(wt[k][0] + tokens[0][0, 0] if tokens else wt[k][0]).astype(BF16) for k in stage_keys(stage)]
        return shards, [landing_zone(s) for s in shards]

    def start_gather(stage, after):
        groups, keys = GATHER_STAGES[stage], stage_keys(stage)
        members = [[keys.index(k) for k in GATHER_GROUPS[g]] for g in groups]
        sems, shards, lands, token = chip_copies_start(
            *prepared[stage], members, False, f"weight_gather_start_{stage}", after)
        tokens.append(token)
        for g, idx, pair in zip(groups, members, sems):
            started[g] = ([shards[i] for i in idx], [lands[i] for i in idx], pair)

    prepared = {0: prepare(0)}
    start_gather(0, None)
    prepared.update({stage: prepare(stage) for stage in range(1, len(GATHER_STAGES))})

    def weights_of(group, after):
        if group == GATHER_STAGES[0][0]:
            after = [after] + [a for stage in range(1, len(GATHER_STAGES)) for part in prepared[stage] for a in part]
        got = chip_copies_wait(*started[group], after, False, f"weight_gather_wait_{group}")
        stage = [s + 1 for s, groups in enumerate(GATHER_STAGES[:-1]) if groups[0] == group]
        if stage:
            start_gather(stage[0], got[0])
        full = {}
        for k, g in zip(GATHER_GROUPS[group], got):
            if k in COLUMN_SHARDED:
                if k in ("ffn1_w_in", "ffn2_w_in"):
                    g = jnp.stack([g[0], g[2], g[1], g[3]])
                full[k] = jnp.swapaxes(g, 0, 1).reshape(g.shape[1], N_CHIPS * g.shape[2])
                if k == "w_in":
                    full[k] = to_kernel_heads(full[k])
            else:
                full[k] = g.reshape(N_CHIPS * g.shape[1], g.shape[2])
        return full

    in_flight = {}

    def send_grads(group, grads):
        def shard_major(k, g):
            if k in KEPT_SHARD_MAJOR:
                return g
            if k in COLUMN_SHARDED:
                return jnp.swapaxes(g.reshape(g.shape[0], N_CHIPS, g.shape[1] // N_CHIPS), 0, 1)
            return g.reshape(N_CHIPS, g.shape[0] // N_CHIPS, g.shape[1])

        own, wire = [], []
        for k in GROUPS[group]:
            g, rounded = grads[k] if isinstance(grads[k], (tuple, list)) else (grads[k], None)
            g = shard_major(k, from_kernel_heads(g) if k == "w_in" else g)
            own.append(g)
            wire.append(g.astype(BF16) if rounded is None else shard_major(k, rounded))
        zones = [landing_zone(lax.dynamic_index_in_dim(b, chip, 0, keepdims=False)) for b in wire]
        pair, wire, zones, sent = chip_copies_start(
            wire, zones, [list(range(len(wire)))], True, f"grad_scatter_start_{group}")
        in_flight[group] = (own, wire, zones, pair[0], sent)
        return sent

    gains = {k: wt[k] for k in GAINS}
    sq, dx, grads = layer_step(
        x[0], mem[0], loss_target[0], gains, sinks[0], b_gate, weights_of, send_grads, tokens[0][0, 0])
    loss = lax.psum(0.5 * sq[0, 0] / D_MODEL, ("x", "y", "c"))

    res = {}
    after = in_flight["ffn1_out"][4]
    swaps = []
    for stage in (("ffn2_in", "ffn2_out", "mix", "ffn1_in"), ("ffn1_out",)):
        names, parts = [], []
        for group in stage:
            own, wire, zones, pair, _ = in_flight[group]
            received = chip_copies_wait(wire, zones, pair, after, True, f"grad_scatter_wait_{group}")
            for k, g, r in zip(GROUPS[group], own, received):
                names.append(k)
                parts.append(chip_partial_sum(me, g, r, f"{k}_chip_sum"))
        pair, parts, lands, after = sibling_copies_start(parts, f"sibling_start_{stage[-1]}")
        swaps.append((stage[-1], names, parts, lands, pair))
    small_all = small_all_gather(_pack_small(grads), "small_grad_gather")
    packed = adamw_small(small_all, _pack_small(wt), _pack_small(mom), _pack_small(var), "small_adamw")
    after = packed[0]
    for tag, names, parts, lands, pair in swaps:
        sibs = sibling_copies_wait(parts, lands, pair, after, f"sibling_wait_{tag}")
        for k, p, s in zip(names, parts, sibs):
            res[k] = [t[None] for t in adamw_pair(p, s, wt[k][0], mom[k][0], var[k][0], f"{k}_adamw")]
        after = res[names[-1]][0]
    for idx, p in enumerate(packed):
        for k, t in _unpack_small(p).items():
            res.setdefault(k, [None] * 4)[idx] = t

    return (loss, dx[None], *[res[k][0] for k in WEIGHTS], *[res[k][1] for k in WEIGHTS],
            *[res[k][2] for k in WEIGHTS], *[res[k][3] for k in WEIGHTS])
```

```python
import functools

import jax
import jax.numpy as jnp
from jax import lax
from jax.experimental import pallas as pl
from jax.experimental.pallas import tpu as pltpu

F32 = jnp.float32
BF16 = jnp.bfloat16

D_MODEL = 1024
D_FF = 2816
HEAD = 128
N_CHIPS = 4
N_DEV = 8
EPS = 1e-6
NEG_INF = -1e30
ROPE_THETA = 10000.0
ATT_SCALE = HEAD ** -0.5

ADAM_LR = 0.001
ADAM_B1 = 0.9
ADAM_B2 = 0.999
ADAM_EPS = 1e-08
ADAM_WD = 0.01
ADAM_STEP = 10

VMEM_LIMIT = 52 * 2 ** 20
VMEM_LIMIT_LARGE = 60 * 2 ** 20
MESH = pl.DeviceIdType.MESH

QKV_W = 3840
DIL = ((128, 1), (512, 4), (2048, 16))
B_BASE, MQ, A_BASE = 0, 8, 12
_AQ, _AK, _AV, _BQ, _BK, _BV, _MQ = 0, 6, 12, 18, 22, 24, 26
HEAD_ORDER = tuple(
    [h for j in range(2) for h in (_BQ + 2 * j, _BQ + 2 * j + 1, _BK + j, _BV + j)]
    + [_MQ + i for i in range(4)]
    + [h for g in range(3) for i in range(2) for h in (_AQ + 2 * g + i, _AK + 2 * g + i, _AV + 2 * g + i)])
ROTARY_HEADS = tuple(p for p, h in enumerate(HEAD_ORDER) if h < _AV or _BQ <= h < _BV)


def to_kernel_heads(w):
    return jnp.concatenate([w[..., h * HEAD:(h + 1) * HEAD] for h in HEAD_ORDER], axis=-1)


def from_kernel_heads(w):
    place = {h: p for p, h in enumerate(HEAD_ORDER)}
    return jnp.concatenate([w[..., place[h] * HEAD:(place[h] + 1) * HEAD] for h in range(len(HEAD_ORDER))], axis=-1)

TM = 512
FF_T = D_FF // 2


def _params(*sem):
    return pltpu.CompilerParams(dimension_semantics=sem, vmem_limit_bytes=VMEM_LIMIT)


def _dot(a, b):
    return jnp.dot(a, b, preferred_element_type=F32)


def _dot_nt(a, b):
    return lax.dot_general(a, b, (((1,), (1,)), ((), ())), preferred_element_type=F32)


def _dot_tn(a, b):
    return lax.dot_general(a, b, (((0,), (0,)), ((), ())), preferred_element_type=F32)


def _rstd(x):
    return lax.rsqrt(jnp.mean(x * x, axis=-1, keepdims=True) + EPS)


def _sigmoid(x):
    return 0.5 * jnp.tanh(0.5 * x) + 0.5


def _ffn_perm(k):
    return (k % 2) * 2 + k // 2


UNREAD = pl.BlockSpec(memory_space=pl.ANY)


def _resident(arr):
    return pl.BlockSpec(arr.shape, lambda *_: (0,) * arr.ndim, pipeline_mode=pl.Buffered(1))


def rms_scale(x, g, name, after):
    T, D = x.shape
    tm = 1024

    def body(x_ref, g_ref, _, o_ref):
        v = x_ref[...]
        o_ref[...] = (v * _rstd(v) * g_ref[...]).astype(BF16)

    spec = pl.BlockSpec((tm, D), lambda i: (i, 0))
    return pl.pallas_call(
        body, name=name, grid=(T // tm,), in_specs=[spec, _resident(g), UNREAD], out_specs=spec,
        out_shape=jax.ShapeDtypeStruct((T, D), BF16), compiler_params=_params("parallel"),
    )(x, g, after)


def ffn_in(h, g, w, name, xn=None, half=None, into=None):
    T, D = h.shape
    normed = xn is not None
    steps = T // TM if half is None else T // TM // 2
    first = 0 if half is None else half * steps

    def body(h_ref, g_ref, w_ref, *rest):
        outs = rest[len(into or ()):]
        if normed:
            xn, (gu_ref, a_ref) = h_ref[...], outs
        else:
            xn_ref, gu_ref, a_ref = outs
            x = h_ref[...]
            xn = (x * _rstd(x) * g_ref[...]).astype(BF16)
            xn_ref[...] = xn
        for j in range(2):
            gu = _dot(xn, w_ref[:, j * 2 * FF_T:(j + 1) * 2 * FF_T])
            gu_ref[:, j * 2 * FF_T:(j + 1) * 2 * FF_T] = gu.astype(BF16)
            gate, up = gu[:, :FF_T], gu[:, FF_T:]
            a_ref[:, j * FF_T:(j + 1) * FF_T] = (gate * _sigmoid(gate) * up).astype(BF16)

    def rows(width):
        return pl.BlockSpec((TM, width), lambda i: (i + first, 0))

    n_out = 2 if normed else 3
    res = pl.pallas_call(
        body, name=name,
        grid=(steps,),
        in_specs=[rows(D), _resident(g), _resident(w)] + [UNREAD] * len(into or ()),
        out_specs=[rows(D)] * (not normed) + [rows(2 * D_FF), rows(D_FF)],
        out_shape=[jax.ShapeDtypeStruct((T, D), BF16)] * (not normed)
                  + [jax.ShapeDtypeStruct((T, 2 * D_FF), BF16), jax.ShapeDtypeStruct((T, D_FF), BF16)],
        input_output_aliases={3 + k: n_out - 2 + k for k in range(len(into or ()))},
        compiler_params=_params("parallel"),
    )(xn if normed else h, g, w, *(into or ()))
    return (xn, *res) if normed else tuple(res)


def mm_norm_res(a, w, h_in, g, coef, name, target=None):
    T, K = a.shape
    D = w.shape[1]
    final = target is not None

    def body(*refs):
        if final:
            a_ref, w_ref, h_ref, g_ref, t_ref, f_ref, o_ref, l_ref = refs
        else:
            a_ref, w_ref, h_ref, g_ref, f_ref, o_ref = refs
        f = _dot(a_ref[...], w_ref[...])
        f_ref[...] = f
        y = h_ref[...] + coef * (f * _rstd(f) * g_ref[...])
        if final:
            err = y - t_ref[...]
            o_ref[...] = err * (1.0 / D)

            @pl.when(pl.program_id(0) == 0)
            def _():
                l_ref[...] = jnp.zeros_like(l_ref)

            l_ref[...] += jnp.sum(err * err)
        else:
            o_ref[...] = y

    row = pl.BlockSpec((TM, D), lambda i: (i, 0))
    in_specs = [pl.BlockSpec((TM, K), lambda i: (i, 0)),
                _resident(w),
                row, pl.BlockSpec((1, D), lambda i: (0, 0))]
    out_specs = [row, row]
    out_shape = [jax.ShapeDtypeStruct((T, D), F32), jax.ShapeDtypeStruct((T, D), F32)]
    args = [a, w, h_in, g]
    if final:
        in_specs.append(row)
        args.append(target)
        out_specs.append(pl.BlockSpec((8, 128), lambda i: (0, 0)))
        out_shape.append(jax.ShapeDtypeStruct((8, 128), F32))
    return pl.pallas_call(
        body, name=name, grid=(T // TM,), in_specs=in_specs, out_specs=out_specs, out_shape=out_shape,
        compiler_params=_params("arbitrary"),
    )(*args)


def _rope(x, cos, sin_signed):
    return x * cos + pltpu.roll(x, HEAD // 2, axis=1) * sin_signed


def _unrope(x, cos, sin_signed):
    return x * cos - pltpu.roll(x, HEAD // 2, axis=1) * sin_signed


def mix_in(h, g, w, w_gate, b_gate, cos, sin_signed, name):
    T, D = h.shape
    tn = 768

    def body(h_ref, g_ref, w_ref, wg_ref, b_ref, c_ref, s_ref, u_ref, o_ref, gt_ref):
        x = h_ref[...]
        u = (x * _rstd(x) * g_ref[...]).astype(BF16)
        u_ref[...] = u
        c, s = c_ref[...], s_ref[...]
        for j in range(QKV_W // tn):
            acc = _dot(u, w_ref[:, j * tn:(j + 1) * tn])
            for hd in range(tn // HEAD):
                head = j * (tn // HEAD) + hd
                part = acc[:, hd * HEAD:(hd + 1) * HEAD]
                if head in ROTARY_HEADS:
                    part = _rope(part, c, s)
                o_ref[:, head * HEAD:(head + 1) * HEAD] = part.astype(BF16)
        for j in range(w_gate.shape[1] // tn):
            cols = slice(j * tn, (j + 1) * tn)
            gt_ref[:, cols] = _sigmoid(_dot(u, wg_ref[:, cols]) + b_ref[:, cols]).astype(BF16)

    def rows(width):
        return pl.BlockSpec((TM, width), lambda i: (i, 0))

    return pl.pallas_call(
        body, name=name,
        grid=(T // TM,),
        in_specs=[rows(D), _resident(g), _resident(w), _resident(w_gate), _resident(b_gate), rows(HEAD), rows(HEAD)],
        out_specs=[rows(D), rows(QKV_W), rows(w_gate.shape[1])],
        out_shape=[jax.ShapeDtypeStruct((T, D), BF16), jax.ShapeDtypeStruct((T, QKV_W), BF16),
                   jax.ShapeDtypeStruct((T, w_gate.shape[1]), BF16)],
        compiler_params=_params("parallel"),
    )(h, g, w, w_gate, b_gate, cos, sin_signed)


def gate_merge_out(gt, o_a, o_b, o_m, w_a, w_b, w_m, w_out, h_in, g, name):
    T = gt.shape[0]
    D = D_MODEL

    def body(gt_ref, oa_ref, ob_ref, om_ref, wa_ref, wb_ref, wm_ref, wo_ref, h_ref, g_ref, m_ref, f_ref, o_ref):
        acc = gt_ref[:, :D].astype(F32) * _dot(oa_ref[...], wa_ref[...])
        acc += gt_ref[:, D:2 * D].astype(F32) * _dot(ob_ref[...], wb_ref[...])
        acc += gt_ref[:, 2 * D:].astype(F32) * _dot(om_ref[...], wm_ref[...])
        merged = acc.astype(BF16)
        m_ref[...] = merged
        f = _dot(merged, wo_ref[...])
        f_ref[...] = f
        o_ref[...] = h_ref[...] + f * _rstd(f) * g_ref[...]

    def rows(width):
        return pl.BlockSpec((TM, width), lambda i: (i, 0))

    return pl.pallas_call(
        body, name=name, grid=(T // TM,),
        in_specs=[rows(3 * D), rows(o_a.shape[1]), rows(o_b.shape[1]), rows(o_m.shape[1]),
                  _resident(w_a), _resident(w_b), _resident(w_m), _resident(w_out), rows(D), _resident(g)],
        out_specs=[rows(D), rows(D), rows(D)],
        out_shape=[jax.ShapeDtypeStruct((T, D), BF16), jax.ShapeDtypeStruct((T, D), F32),
                   jax.ShapeDtypeStruct((T, D), F32)],
        compiler_params=_params("parallel"),
    )(gt, o_a, o_b, o_m, w_a, w_b, w_m, w_out, h_in, g)


def _band_rows(start, r):
    return pl.ds(start, HEAD) if r == 1 else pl.ds(start, HEAD, stride=r)


def _band_mask(max_dist, first_has_prev):
    row = lax.broadcasted_iota(jnp.int32, (HEAD, 2 * HEAD), 0)
    col = lax.broadcasted_iota(jnp.int32, (HEAD, 2 * HEAD), 1)
    dist = row + HEAD - col
    band = (dist >= 0) & (dist <= max_dist)
    return band, band & (col >= jnp.where(first_has_prev, 0, HEAD))


def _stack(parts):
    return parts[0] if len(parts) == 1 else jnp.concatenate(parts, axis=0)


def _band_specs(BT, SB, nsub, base, grp):
    stride = grp + 2

    def cur(off, width):
        return pl.BlockSpec((BT, width * HEAD), lambda h, i: (i, (base + h * stride + off) // width))

    def prev(off):
        return pl.BlockSpec((SB, HEAD), lambda h, i: (jnp.maximum(i * nsub - 1, 0), base + h * stride + off))

    return cur(0, grp), cur(grp, 1), prev(grp), cur(grp + 1, 1), prev(grp + 1)


def band_fwd(qkv, sinks, *, r, base, hkv, grp, max_dist, out_dtype, name, merge=None):
    T, W = qkv.shape
    SB = HEAD * r
    BT = min(2048, T)
    nsub, nib = BT // SB, T // BT
    hq = hkv * grp
    heads = [slice(g * HEAD, (g + 1) * HEAD) for g in range(grp)]
    others = [] if merge is None else [*merge[0], *merge[1]]

    def body(sink_ref, q_ref, kc_ref, kp_ref, vc_ref, vp_ref, *rest):
        joint_o, joint_l = rest[len(others):len(others) + 2]
        qf, kf, vf = rest[len(others) + 2:len(others) + 5]
        o_ref, l_ref = rest[len(others) + 5:] if others else (joint_o, joint_l)
        kvh, ib = pl.program_id(0), pl.program_id(1)
        qf[...] = q_ref[...].astype(F32)
        kf[:SB] = kp_ref[...].astype(F32)
        kf[SB:] = kc_ref[...].astype(F32)
        vf[:SB] = vp_ref[...].astype(F32)
        vf[SB:] = vc_ref[...].astype(F32)
        band, band_first = _band_mask(max_dist, ib > 0)
        for c in range(r):
            k_old, v_old = kf[_band_rows(c, r)], vf[_band_rows(c, r)]
            for j in range(nsub):
                mask = band_first if j == 0 else band
                rows = _band_rows(j * SB + c, r)
                k_own, v_own = kf[_band_rows((j + 1) * SB + c, r)], vf[_band_rows((j + 1) * SB + c, r)]
                kcat = jnp.concatenate([k_old, k_own], axis=0).astype(BF16)
                vcat = jnp.concatenate([v_old, v_own], axis=0).astype(BF16)
                k_old, v_old = k_own, v_own
                s_all = _dot_nt(_stack([qf[rows, cols] for cols in heads]).astype(BF16), kcat) * ATT_SCALE
                probs, tots = [], []
                for g, cols in enumerate(heads):
                    s = jnp.where(mask, s_all[cols], NEG_INF)
                    sk = sink_ref[kvh * grp + g]
                    m = jnp.maximum(jnp.max(s, axis=-1, keepdims=True), sk)
                    p = jnp.exp(s - m)
                    tot = jnp.sum(p, axis=-1, keepdims=True) + jnp.exp(sk - m)
                    probs.append(p.astype(BF16))
                    tots.append(tot)
                    l_ref[rows, cols] = jnp.broadcast_to(m + jnp.log(tot), (HEAD, HEAD))
                o_all = _dot(_stack(probs), vcat)
                for g, cols in enumerate(heads):
                    o_ref[rows, cols] = (o_all[cols] / tots[g]).astype(o_ref.dtype)

        if others:
            half = len(others) // 2
            outs = [ref[...] for ref in rest[:half]] + [o_ref[...]]
            logs = [ref[...] for ref in rest[half:len(others)]] + [l_ref[...]]
            top = functools.reduce(jnp.maximum, logs)
            weights = [jnp.exp(lg - top) for lg in logs]
            total = functools.reduce(jnp.add, weights)
            mixed = functools.reduce(jnp.add, [wgt * out for wgt, out in zip(weights, outs)])
            joint_o[...] = (mixed / total).astype(out_dtype)
            joint_l[...] = top + jnp.log(total)

    out_spec = pl.BlockSpec((BT, grp * HEAD), lambda h, i: (i, h))
    own = [pltpu.VMEM((BT, grp * HEAD), F32)] * 2 if others else []
    return pl.pallas_call(
        body, name=name, grid=(hkv, nib),
        in_specs=[pl.BlockSpec(memory_space=pltpu.SMEM), *_band_specs(BT, SB, nsub, base, grp)]
                 + [out_spec] * len(others),
        out_specs=[out_spec, out_spec],
        out_shape=[jax.ShapeDtypeStruct((T, hq * HEAD), out_dtype), jax.ShapeDtypeStruct((T, hq * HEAD), F32)],
        scratch_shapes=[pltpu.VMEM((BT, grp * HEAD), F32), pltpu.VMEM((SB + BT, HEAD), F32),
                        pltpu.VMEM((SB + BT, HEAD), F32)] + own,
        compiler_params=_params("parallel", "arbitrary"),
    )(sinks, qkv, qkv, qkv, qkv, qkv, *others)


def band_bwd(qkv, dqkv, do, o, lse, cos, sin_signed, sinks, *, r, base, hkv, grp, max_dist, name):
    T, W = qkv.shape
    SB = HEAD * r
    BT = min(max(2048, 2 * SB), T)
    nsub, nib = BT // SB, T // BT
    nblk = T // SB
    with_sink = sinks is not None
    heads = [slice(g * HEAD, (g + 1) * HEAD) for g in range(grp)]

    def body(*refs):
        if with_sink:
            sink_ref, refs = refs[0], refs[1:]
        (q_ref, kc_ref, kp_ref, vc_ref, vp_ref, qn_ref, do_ref, don_ref, o_ref, on_ref, l_ref, ln_ref,
         c_ref, s_ref, _) = refs[:15]
        out_ref = refs[15]
        ds_ref = refs[16] if with_sink else None
        qf, dof, of, kf, vf, dqf, dkf, dvf = refs[-8:]
        kvh, ib = pl.program_id(0), pl.program_id(1)
        for buf, cur_ref, nxt_ref in ((qf, q_ref, qn_ref), (dof, do_ref, don_ref), (of, o_ref, on_ref)):
            buf[:BT] = cur_ref[...].astype(F32)
            buf[BT:] = nxt_ref[...].astype(F32)
        kf[:SB] = kp_ref[...].astype(F32)
        kf[SB:] = kc_ref[...].astype(F32)
        vf[:SB] = vp_ref[...].astype(F32)
        vf[SB:] = vc_ref[...].astype(F32)
        band, band_first = _band_mask(max_dist, ib > 0)
        if with_sink:
            @pl.when(ib == 0)
            def _():
                ds_ref[...] = jnp.zeros_like(ds_ref)

        def grads(rows, logzs, keys, vals, mask):
            q = _stack([qf[rows, cols] for cols in heads]).astype(BF16)
            dout = _stack([dof[rows, cols] for cols in heads]).astype(BF16)
            s_all = _dot_nt(q, keys) * ATT_SCALE
            dp_all = _dot_nt(dout, vals)
            probs, dss, deltas = [], [], []
            for g, cols in enumerate(heads):
                delta = jnp.sum(dof[rows, cols] * of[rows, cols], axis=-1, keepdims=True)
                p = jnp.exp(jnp.where(mask, s_all[cols], NEG_INF) - logzs[g][:, :1])
                probs.append(p.astype(BF16))
                dss.append((p * (dp_all[cols] - delta) * ATT_SCALE).astype(BF16))
                deltas.append(delta)
            return q, dout, _stack(probs), _stack(dss), deltas

        row = lax.broadcasted_iota(jnp.int32, (HEAD, HEAD), 0)
        col = lax.broadcasted_iota(jnp.int32, (HEAD, HEAD), 1)
        reach = col >= row + jnp.where(ib < nib - 1, HEAD - max_dist, 2 * HEAD)
        for c in range(r):
            k_old, v_old = kf[_band_rows(c, r)], vf[_band_rows(c, r)]
            dk_own = dv_own = None
            for j in range(nsub):
                rows = _band_rows(j * SB + c, r)
                k_own, v_own = kf[_band_rows((j + 1) * SB + c, r)], vf[_band_rows((j + 1) * SB + c, r)]
                kcat = jnp.concatenate([k_old, k_own], axis=0).astype(BF16)
                vcat = jnp.concatenate([v_old, v_own], axis=0).astype(BF16)
                logzs = [l_ref[rows, cols] for cols in heads]
                q, dout, p, ds, deltas = grads(rows, logzs, kcat, vcat, band_first if j == 0 else band)
                dq = _dot(ds, kcat)
                for g, cols in enumerate(heads):
                    dqf[rows, cols] = dq[cols]
                    if with_sink:
                        p_sink = jnp.exp(sink_ref[kvh * grp + g] - logzs[g][:, :1])
                        ds_ref[g * 8:(g + 1) * 8] += jnp.sum(p_sink * deltas[g])
                dk, dv = _dot_tn(ds, q), _dot_tn(p, dout)
                if j > 0:
                    done = _band_rows((j - 1) * SB + c, r)
                    dkf[done] = dk_own + dk[:HEAD]
                    dvf[done] = dv_own + dv[:HEAD]
                dk_own, dv_own = dk[HEAD:], dv[HEAD:]
                k_old, v_old = k_own, v_own
            logzs = [ln_ref[_band_rows(c, r), cols] for cols in heads]
            q, dout, p, ds, _ = grads(_band_rows(BT + c, r), logzs, k_old.astype(BF16), v_old.astype(BF16), reach)
            done = _band_rows((nsub - 1) * SB + c, r)
            dkf[done] = dk_own + _dot_tn(ds, q)
            dvf[done] = dv_own + _dot_tn(p, dout)

        cs, sn = c_ref[...], s_ref[...]
        for cols in heads:
            out_ref[:, cols] = _unrope(dqf[:, cols], cs, sn).astype(BF16)
        out_ref[:, grp * HEAD:(grp + 1) * HEAD] = _unrope(dkf[...], cs, sn).astype(BF16)
        out_ref[:, (grp + 1) * HEAD:] = dvf[...].astype(BF16)

    def nxt_row(i):
        return jnp.minimum((i + 1) * nsub, nblk - 1)

    stride = grp + 2
    q_next = pl.BlockSpec((SB, grp * HEAD), lambda h, i: (nxt_row(i), (base + h * stride) // grp))
    head_cur = pl.BlockSpec((BT, grp * HEAD), lambda h, i: (i, h))
    head_next = pl.BlockSpec((SB, grp * HEAD), lambda h, i: (nxt_row(i), h))
    table = pl.BlockSpec((BT, HEAD), lambda h, i: (i, 0))

    in_specs = [*_band_specs(BT, SB, nsub, base, grp), q_next,
                head_cur, head_next, head_cur, head_next, head_cur, head_next, table, table, UNREAD]
    args = [qkv, qkv, qkv, qkv, qkv, qkv, do, do, o, o, lse, lse, cos, sin_signed, dqkv]
    out_specs = [pl.BlockSpec((BT, stride * HEAD), lambda h, i: (i, base // stride + h))]
    out_shape = [jax.ShapeDtypeStruct(dqkv.shape, dqkv.dtype)]
    if with_sink:
        in_specs.insert(0, pl.BlockSpec(memory_space=pltpu.SMEM))
        args.insert(0, sinks)
        out_specs.append(pl.BlockSpec((None, grp * 8, HEAD), lambda h, i: (h, 0, 0)))
        out_shape.append(jax.ShapeDtypeStruct((hkv, grp * 8, HEAD), F32))
    wide = pltpu.VMEM((BT + SB, grp * HEAD), F32)
    tall = pltpu.VMEM((SB + BT, HEAD), F32)
    grad = pltpu.VMEM((BT, HEAD), F32)
    return pl.pallas_call(
        body, name=name, grid=(hkv, nib), in_specs=in_specs, out_specs=out_specs, out_shape=out_shape,
        input_output_aliases={len(args) - 1: 0},
        scratch_shapes=[wide, wide, wide, tall, tall, pltpu.VMEM((BT, grp * HEAD), F32), grad, grad],
        compiler_params=pltpu.CompilerParams(dimension_semantics=("parallel", "arbitrary"),
                                             vmem_limit_bytes=VMEM_LIMIT_LARGE),
    )(*args)


M_HEADS = 4


def mem_kv(mem, g, w, name):
    n, D = mem.shape

    def body(m_ref, g_ref, w_ref, mn_ref, kv_ref):
        x = m_ref[...]
        mn = (x * _rstd(x) * g_ref[...]).astype(BF16)
        mn_ref[...] = mn
        kv_ref[...] = _dot(mn, w_ref[...]).astype(BF16)

    return pl.pallas_call(
        body, name=name,
        out_shape=[jax.ShapeDtypeStruct((n, D), BF16), jax.ShapeDtypeStruct((n, w.shape[1]), BF16)],
        compiler_params=pltpu.CompilerParams(vmem_limit_bytes=VMEM_LIMIT),
    )(mem, g, w)


def mem_fwd(qkv, mkv, name):
    T = qkv.shape[0]
    n = mkv.shape[0]
    RB = 1024

    def body(q_ref, kv_ref, o_ref, l_ref):
        for h in range(M_HEADS):
            cols = slice(h * HEAD, (h + 1) * HEAD)
            s = _dot_nt(q_ref[:, cols], kv_ref[:, cols]) * ATT_SCALE
            m = jnp.max(s, axis=-1, keepdims=True)
            p = jnp.exp(s - m)
            den = jnp.sum(p, axis=-1, keepdims=True)
            vals = kv_ref[:, (M_HEADS + h) * HEAD:(M_HEADS + h + 1) * HEAD]
            o_ref[:, cols] = (_dot(p.astype(BF16), vals) / den).astype(BF16)
            l_ref[:, cols] = jnp.broadcast_to(m + jnp.log(den), (RB, HEAD))

    out = pl.BlockSpec((RB, M_HEADS * HEAD), lambda i: (i, 0))
    return pl.pallas_call(
        body, name=name, grid=(T // RB,),
        in_specs=[pl.BlockSpec((RB, M_HEADS * HEAD), lambda i: (i, MQ // M_HEADS)), _resident(mkv)],
        out_specs=[out, out],
        out_shape=[jax.ShapeDtypeStruct((T, M_HEADS * HEAD), BF16), jax.ShapeDtypeStruct((T, M_HEADS * HEAD), F32)],
        compiler_params=_params("parallel"),
    )(qkv, mkv)


def mem_bwd(qkv, dqkv, mkv, do, o, lse, name):
    T = qkv.shape[0]
    n = mkv.shape[0]
    RB = 1024

    def body(q_ref, kv_ref, do_ref, o_ref, l_ref, _, dq_ref, dk_ref, dv_ref):
        @pl.when(pl.program_id(0) == 0)
        def _():
            dk_ref[...] = jnp.zeros_like(dk_ref)
            dv_ref[...] = jnp.zeros_like(dv_ref)

        for h in range(M_HEADS):
            cols = slice(h * HEAD, (h + 1) * HEAD)
            keys, vals = kv_ref[:, cols], kv_ref[:, (M_HEADS + h) * HEAD:(M_HEADS + h + 1) * HEAD]
            q, dout = q_ref[:, cols], do_ref[:, cols]
            delta = jnp.sum(dout.astype(F32) * o_ref[:, cols].astype(F32), axis=-1, keepdims=True)
            p = jnp.exp(_dot_nt(q, keys) * ATT_SCALE - l_ref[:, cols][:, :1])
            ds = (p * (_dot_nt(dout, vals) - delta) * ATT_SCALE).astype(BF16)
            dq_ref[:, cols] = _dot(ds, keys).astype(BF16)
            dk_ref[:, cols] += _dot_tn(ds, q)
            dv_ref[:, cols] += _dot_tn(p.astype(BF16), dout)

    wide = M_HEADS * HEAD
    tok = pl.BlockSpec((RB, wide), lambda i: (i, 0))
    q_cols = pl.BlockSpec((RB, wide), lambda i: (i, MQ // M_HEADS))
    slot = pl.BlockSpec((n, wide), lambda i: (0, 0))
    return pl.pallas_call(
        body, name=name, grid=(T // RB,),
        in_specs=[q_cols, _resident(mkv), tok, tok, tok, UNREAD],
        out_specs=[q_cols, slot, slot],
        out_shape=[jax.ShapeDtypeStruct(dqkv.shape, dqkv.dtype),
                   jax.ShapeDtypeStruct((n, wide), F32), jax.ShapeDtypeStruct((n, wide), F32)],
        input_output_aliases={5: 0},
        compiler_params=_params("arbitrary"),
    )(qkv, mkv, do, o, lse, dqkv)


def mem_kv_bwd(mem, g, mem_n, w, dmkv, name):
    n, D = mem.shape

    def body(m_ref, g_ref, mn_ref, w_ref, d_ref, dw_ref, dg_ref):
        d = d_ref[...].astype(BF16)
        dw_ref[...] = _dot_tn(mn_ref[...], d)
        x = m_ref[...]
        dg_ref[...] = jnp.sum(_dot_nt(d, w_ref[...]) * (x * _rstd(x)), axis=0, keepdims=True)

    return pl.pallas_call(
        body, name=name,
        out_shape=[jax.ShapeDtypeStruct(w.shape, F32), jax.ShapeDtypeStruct((1, D), F32)],
        compiler_params=pltpu.CompilerParams(vmem_limit_bytes=VMEM_LIMIT),
    )(mem, g, mem_n, w, dmkv)


def _rms_bwd(dn, f, g):
    r = _rstd(f)
    fhat = f * r
    dfhat = dn * g
    df = r * (dfhat - fhat * jnp.mean(dfhat * fhat, axis=-1, keepdims=True))
    return df, jnp.sum(dn * fhat, axis=0, keepdims=True)


def ffn_tokens_bwd(dh, f, h_in, gu, g_pre, g_post, w_in, w_out, coef, name, after):
    T, D = dh.shape

    def body(dh_ref, f_ref, h_ref, gu_ref, gpre_ref, gpost_ref, win_ref, wout_ref, _,
             df_ref, dgu_ref, dhin_ref, dgpre_ref, dgpost_ref, dxn_ref):
        i, j = pl.program_id(0), pl.program_id(1)

        @pl.when(j == 0)
        def _():
            @pl.when(i == 0)
            def _():
                dgpre_ref[...] = jnp.zeros_like(dgpre_ref)
                dgpost_ref[...] = jnp.zeros_like(dgpost_ref)

            df, dg_post = _rms_bwd(coef * dh_ref[...], f_ref[...], gpost_ref[...])
            dgpost_ref[...] += dg_post
            df_ref[...] = df.astype(BF16)

        for jj in range(2):
            @pl.when(j == jj)
            def _(jj=jj):
                lo, mid, hi = 2 * jj * FF_T, (2 * jj + 1) * FF_T, (2 * jj + 2) * FF_T
                da = _dot_nt(df_ref[...], wout_ref[jj * FF_T:(jj + 1) * FF_T, :])
                gate = gu_ref[:, :FF_T].astype(F32)
                up = gu_ref[:, FF_T:].astype(F32)
                sig = _sigmoid(gate)
                dgate = (da * up * sig * (1.0 + gate * (1.0 - sig))).astype(BF16)
                dup = (da * gate * sig).astype(BF16)
                dgu_ref[:, :FF_T] = dgate
                dgu_ref[:, FF_T:] = dup
                part = _dot_nt(dgate, win_ref[:, lo:mid]) + _dot_nt(dup, win_ref[:, mid:hi])
                if jj == 0:
                    dxn_ref[...] = part
                else:
                    h = h_ref[...]
                    r = _rstd(h)
                    xhat = h * r
                    dxn = dxn_ref[...] + part
                    dxhat = dxn * gpre_ref[...]
                    dhin_ref[...] = dh_ref[...] + r * (dxhat - xhat * jnp.mean(dxhat * xhat, axis=-1, keepdims=True))
                    dgpre_ref[...] += jnp.sum(dxn * xhat, axis=0, keepdims=True)

    row = pl.BlockSpec((TM, D), lambda i, j: (i, 0))
    wide = pl.BlockSpec((TM, 2 * FF_T), lambda i, j: (i, j))
    vec = pl.BlockSpec((1, D), lambda i, j: (0, 0))
    return pl.pallas_call(
        body, name=name, grid=(T // TM, 2),
        in_specs=[row, row, row, wide, _resident(g_pre), _resident(g_post), _resident(w_in), _resident(w_out),
                  UNREAD],
        out_specs=[row, wide, row, vec, vec],
        out_shape=[jax.ShapeDtypeStruct((T, D), BF16), jax.ShapeDtypeStruct((T, 2 * D_FF), BF16),
                   jax.ShapeDtypeStruct((T, D), F32), jax.ShapeDtypeStruct((1, D), F32),
                   jax.ShapeDtypeStruct((1, D), F32)],
        scratch_shapes=[pltpu.VMEM((TM, D), F32)],
        compiler_params=pltpu.CompilerParams(dimension_semantics=("arbitrary", "arbitrary"),
                                             vmem_limit_bytes=VMEM_LIMIT_LARGE),
    )(dh, f, h_in, gu, g_pre, g_post, w_in, w_out, after)


def mm_nt_norm_bwd(pieces, h_in, dh_out, g, name, after):
    T, D = h_in.shape

    def body(*refs):
        ab = refs[:2 * len(pieces)]
        h_ref, dh_ref, g_ref, _, o_ref, dg_ref = refs[2 * len(pieces):]
        dxn = _dot_nt(ab[0][...], ab[1][...])
        for p in range(1, len(pieces)):
            dxn += _dot_nt(ab[2 * p][...], ab[2 * p + 1][...])
        h = h_ref[...]
        r = _rstd(h)
        xhat = h * r
        dxhat = dxn * g_ref[...]
        o_ref[...] = dh_ref[...] + r * (dxhat - xhat * jnp.mean(dxhat * xhat, axis=-1, keepdims=True))

        @pl.when(pl.program_id(0) == 0)
        def _():
            dg_ref[...] = jnp.zeros_like(dg_ref)

        dg_ref[...] += jnp.sum(dxn * xhat, axis=0, keepdims=True)

    in_specs, args = [], []
    for a, w in pieces:
        in_specs += [pl.BlockSpec((TM, a.shape[1]), lambda i: (i, 0)), _resident(w)]
        args += [a, w]
    row = pl.BlockSpec((TM, D), lambda i: (i, 0))
    return pl.pallas_call(
        body, name=name, grid=(T // TM,),
        in_specs=in_specs + [row, row, _resident(g), UNREAD],
        out_specs=[row, pl.BlockSpec((1, D), lambda i: (0, 0))],
        out_shape=[jax.ShapeDtypeStruct((T, D), F32), jax.ShapeDtypeStruct((1, D), F32)],
        compiler_params=_params("arbitrary"),
    )(*args, h_in, dh_out, g, after)


def gate_merge_out_bwd(dh, f, g, w_out, merged, gt, o_a, o_b, o_m, w_a, w_b, w_m, name, after):
    T = dh.shape[0]
    D = D_MODEL
    branch = ((o_a, w_a), (o_b, w_b), (o_m, w_m))

    def body(dh_ref, f_ref, g_ref, wo_ref, m_ref, gt_ref, oa_ref, ob_ref, om_ref, wa_ref, wb_ref, wm_ref, _,
             dg_ref, dwo_ref, dgt_ref, doa_ref, dob_ref, dom_ref, db_ref, dwa_ref, dwb_ref, dwm_ref):
        @pl.when(pl.program_id(0) == 0)
        def _():
            for acc in (dg_ref, dwo_ref, db_ref, dwa_ref, dwb_ref, dwm_ref):
                acc[...] = jnp.zeros_like(acc)

        df, dg = _rms_bwd(dh_ref[...], f_ref[...], g_ref[...])
        dg_ref[...] += dg
        df = df.astype(BF16)
        dwo_ref[...] += _dot_tn(m_ref[...], df)
        dmf = _dot_nt(df, wo_ref[...])
        for x, (o_ref, w_ref, do_ref, dw_ref) in enumerate(((oa_ref, wa_ref, doa_ref, dwa_ref),
                                                           (ob_ref, wb_ref, dob_ref, dwb_ref),
                                                           (om_ref, wm_ref, dom_ref, dwm_ref))):
            cols = slice(x * D, (x + 1) * D)
            gx = gt_ref[:, cols].astype(F32)
            w = w_ref[...]
            dpre = dmf * _dot(o_ref[...], w) * gx * (1.0 - gx)
            dgt_ref[:, cols] = dpre.astype(BF16)
            db_ref[:, cols] += jnp.sum(dpre, axis=0, keepdims=True)
            dp = (dmf * gx).astype(BF16)
            do_ref[...] = _dot_nt(dp, w).astype(BF16)
            dw_ref[...] += _dot_tn(dp, o_ref[...])

    def rows(width):
        return pl.BlockSpec((TM, width), lambda i: (i, 0))

    def kept(shape):
        return pl.BlockSpec(shape, lambda i: (0,) * len(shape))

    widths = [o.shape[1] for o, _ in branch]
    sums = [(1, D), (D, D), (1, 3 * D)] + [(D, k) for k in widths]
    return pl.pallas_call(
        body, name=name, grid=(T // TM,),
        in_specs=[rows(D), rows(D), _resident(g), _resident(w_out), rows(D), rows(3 * D)]
                 + [rows(k) for k in widths] + [_resident(w) for _, w in branch] + [UNREAD],
        out_specs=[kept(sums[0]), kept(sums[1]), rows(3 * D)] + [rows(k) for k in widths]
                  + [kept(shape) for shape in sums[2:]],
        out_shape=[jax.ShapeDtypeStruct(sums[0], F32), jax.ShapeDtypeStruct(sums[1], F32),
                   jax.ShapeDtypeStruct((T, 3 * D), BF16)] + [jax.ShapeDtypeStruct((T, k), BF16) for k in widths]
                  + [jax.ShapeDtypeStruct(shape, F32) for shape in sums[2:]],
        compiler_params=pltpu.CompilerParams(dimension_semantics=("arbitrary",), vmem_limit_bytes=VMEM_LIMIT_LARGE),
    )(dh, f, g, w_out, merged, gt, o_a, o_b, o_m, w_a, w_b, w_m, after)


def mm_tn(x, dy, tm, tn, name, shard_major=False, perm=None, slabs=1, after=None, wire=False):
    T, M = x.shape
    N = dy.shape[1]
    tk = min(2048, T)
    perm = perm or (lambda j: j)
    w = tn // slabs

    def body(x_ref, dy_ref, *rest):
        o_ref = rest[-2] if wire else rest[-1]

        @pl.when(pl.program_id(2) == 0)
        def _():
            o_ref[...] = jnp.zeros_like(o_ref)

        acc = _dot_tn(x_ref[...], dy_ref[...])
        if shard_major:
            for s in range(slabs):
                o_ref[s] += acc[:, s * w:(s + 1) * w]
        else:
            o_ref[...] += acc
        if wire:
            @pl.when(pl.program_id(2) == T // tk - 1)
            def _():
                rest[-1][...] = o_ref[...].astype(BF16)

    if shard_major:
        out_spec = pl.BlockSpec((slabs, tm, w), lambda i, j, k: (perm(j), i, 0))
        out_shape = jax.ShapeDtypeStruct((N // w, M, w), F32)
    else:
        out_spec = pl.BlockSpec((tm, tn), lambda i, j, k: (i, j))
        out_shape = jax.ShapeDtypeStruct((M, N), F32)
    return pl.pallas_call(
        body, name=name, grid=(M // tm, N // tn, T // tk),
        in_specs=[pl.BlockSpec((tk, tm), lambda i, j, k: (k, i)),
                  pl.BlockSpec((tk, tn), lambda i, j, k: (k, j))] + ([] if after is None else [UNREAD]),
        out_specs=[out_spec, out_spec] if wire else out_spec,
        out_shape=[out_shape, jax.ShapeDtypeStruct(out_shape.shape, BF16)] if wire else out_shape,
        compiler_params=_params("parallel", "parallel", "arbitrary"),
    )(x, dy, *([] if after is None else [after]))


def rope_tables(T, zero):
    half = HEAD // 2
    inv = ROPE_THETA ** (-jnp.arange(half, dtype=F32) / half)
    ang = (jnp.arange(T).astype(F32) + zero)[:, None] * inv[None, :]
    cos, sin = jnp.cos(ang), jnp.sin(ang)
    return jnp.concatenate([cos, cos], axis=1), jnp.concatenate([-sin, sin], axis=1)


def layer_step(x, mem, target, gains, sinks, b_gate, weights_of, send_grads, zero):
    T = x.shape[0]
    cos, sin_signed = rope_tables(T, zero)
    no_sink = jnp.full((2,), NEG_INF, F32)

    xn1 = rms_scale(x, gains["ffn1_norm_pre"], "ffn1_norm", cos)
    w = dict(weights_of("ffn1_in", xn1))
    _, gu1, a1 = ffn_in(x, gains["ffn1_norm_pre"], w["ffn1_w_in"], "ffn1_in_a", xn=xn1, half=0)
    w.update(weights_of("ffn1_out", a1))
    _, gu1, a1 = ffn_in(x, gains["ffn1_norm_pre"], w["ffn1_w_in"], "ffn1_in_b", xn=xn1, half=1, into=(gu1, a1))
    f1, h1 = mm_norm_res(a1, w["ffn1_w_out"], x, gains["ffn1_norm_post"], 0.5, "ffn1_out")
    w.update(weights_of("mix_in", f1))
    u, qkv, gt = mix_in(h1, gains["mix_norm_pre"], w["w_in"], w["w_gate"], b_gate, cos, sin_signed, "mix_in")
    w.update(weights_of("mix_rest", u))
    outs, lses = [], []
    for gidx, (window, dil) in enumerate(DIL):
        last = gidx == len(DIL) - 1
        o_g, l_g = band_fwd(qkv, no_sink, r=dil, base=A_BASE + 6 * gidx, hkv=2, grp=1, max_dist=window // dil,
                            out_dtype=BF16 if last else F32, name=f"attn_a{gidx}_fwd",
                            merge=(outs, lses) if last else None)
        outs.append(o_g)
        lses.append(l_g)
    o_a, l_a = outs[-1], lses[-1]
    o_b, l_b = band_fwd(qkv, sinks, r=1, base=B_BASE, hkv=2, grp=2, max_dist=HEAD - 1, out_dtype=BF16,
                        name="attn_b_fwd")
    mem_n, mkv = mem_kv(mem, gains["mem_norm"], w["w_mem_kv"], "mem_kv")
    o_m, l_m = mem_fwd(qkv, mkv, "attn_m_fwd")
    merged, mo, h2 = gate_merge_out(gt, o_a, o_b, o_m, w["w_o_a"], w["w_o_b"], w["w_o_m"], w["w_out"], h1,
                                    gains["mix_norm_post"], "gate_merge_out")
    w.update(weights_of("ffn2", mo))
    xn2, gu2, a2 = ffn_in(h2, gains["ffn2_norm_pre"], w["ffn2_w_in"], "ffn2_in")
    f2, dy, sq = mm_norm_res(a2, w["ffn2_w_out"], h2, gains["ffn2_norm_post"], 0.5, "ffn2_out", target=target)

    grads = {}

    def ffn_bwd(tag, dh_out, f, gu, a, xn, h_in, after):
        df, dgu, dh_in, grads[f"{tag}_norm_pre"], grads[f"{tag}_norm_post"] = ffn_tokens_bwd(
            dh_out, f, h_in, gu, gains[f"{tag}_norm_pre"], gains[f"{tag}_norm_post"], w[f"{tag}_w_in"],
            w[f"{tag}_w_out"], 0.5, f"{tag}_tokens_bwd", after)
        sent = send_grads(f"{tag}_in", {f"{tag}_w_in": mm_tn(
            xn, dgu, D_MODEL, FF_T, f"{tag}_w_in_grad", shard_major=True, perm=_ffn_perm, wire=True)})
        sent = send_grads(f"{tag}_out", {f"{tag}_w_out": mm_tn(
            a, df, FF_T, D_MODEL, f"{tag}_w_out_grad", after=sent, wire=True)})
        return dh_in, sent

    dh2, sent = ffn_bwd("ffn2", dy, f2, gu2, a2, xn2, h2, dy)

    mix = {}
    (grads["mix_norm_post"], mix["w_out"], dgt, do_a, do_b, do_m, grads["b_gate"],
     dwa_t, dwb_t, dwm_t) = gate_merge_out_bwd(
        dh2, mo, gains["mix_norm_post"], w["w_out"], merged, gt, o_a, o_b, o_m, w["w_o_a"], w["w_o_b"],
        w["w_o_m"], "gate_merge_out_bwd", sent)
    mix["w_o_a"], mix["w_o_b"], mix["w_o_m"] = dwa_t.T, dwb_t.T, dwm_t.T

    dqkv = lax.empty(qkv.shape, qkv.dtype)
    for gidx, (window, dil) in enumerate(DIL):
        dqkv, = band_bwd(qkv, dqkv, do_a, o_a, l_a, cos, sin_signed, None, r=dil, base=A_BASE + 6 * gidx, hkv=2,
                         grp=1, max_dist=window // dil, name=f"attn_a{gidx}_bwd")
    dqkv, dsink = band_bwd(qkv, dqkv, do_b, o_b, l_b, cos, sin_signed, sinks, r=1, base=B_BASE, hkv=2, grp=2,
                           max_dist=HEAD - 1, name="attn_b_bwd")
    grads["sinks"] = -dsink[:, ::8, 0].reshape(1, 4)
    dqkv, dmk, dmv = mem_bwd(qkv, dqkv, mkv, do_m, o_m, l_m, "attn_m_bwd")
    mix["w_mem_kv"], grads["mem_norm"] = mem_kv_bwd(
        mem, gains["mem_norm"], mem_n, w["w_mem_kv"], jnp.concatenate([dmk, dmv], axis=1), "mem_kv_bwd")

    mix["w_in"] = mm_tn(u, dqkv, D_MODEL, 1280, "w_in_grad")
    mix["w_gate"] = mm_tn(u, dgt, D_MODEL, 1536, "w_gate_grad", shard_major=True, slabs=2, wire=True)
    sent = send_grads("mix", mix)
    dh1, grads["mix_norm_pre"] = mm_nt_norm_bwd(
        [(dqkv, w["w_in"]), (dgt, w["w_gate"])], h1, dh2, gains["mix_norm_pre"], "mix_in_bwd", sent)

    dx, _ = ffn_bwd("ffn1", dh1, f1, gu1, a1, xn1, x, dh1)
    return sq, dx, grads


def _place():
    return lax.axis_index("x"), lax.axis_index("y"), lax.axis_index("c")


def _other_chips(x, y):
    return [(1 - x, y), (x, 1 - y), (1 - x, 1 - y)]


def _hbm(n):
    return [pl.BlockSpec(memory_space=pltpu.HBM)] * n


SEM = pl.BlockSpec(memory_space=pltpu.SEMAPHORE)
SIDE_EFFECT = pltpu.SideEffectType.DATAFLOW_SIDE_EFFECTING


def _chip_copy(src, land, sems, i, j, dst_slot, scatter):
    x, y, c = _place()
    px, py = _other_chips(x, y)[j]
    send_sems, recv_sems = sems
    return pltpu.make_async_remote_copy(
        src_ref=src[i].at[2 * px + py] if scatter else src[i], dst_ref=land[i].at[dst_slot],
        send_sem=send_sems.at[3 * i + j], recv_sem=recv_sems.at[3 * i + j],
        device_id=(px, py, c), device_id_type=MESH)


def chip_copies_start(srcs, lands, groups, scatter, name, after=None):
    n = len(srcs)

    def body(*refs):
        src, land = refs[:n], refs[n:2 * n]
        first_sem = 2 * n + (after is not None)
        sems = refs[first_sem:first_sem + 2 * len(groups)]
        token = refs[-1]
        x, y, _ = _place()
        for g, members in enumerate(groups):
            part = ([src[i] for i in members], [land[i] for i in members])
            for t in range(len(members)):
                for j in range(3):
                    _chip_copy(*part, sems[2 * g:2 * g + 2], t, j, 2 * x + y, scatter).start()
        token[...] = jnp.zeros_like(token)

    sem_shapes = [pltpu.SemaphoreType.DMA((3 * len(m),)) for m in groups for _ in range(2)]
    thru = [pltpu.HBM(a.shape, a.dtype) for a in (*srcs, *lands)]
    res = pl.pallas_call(
        body, name=name,
        out_shape=(*sem_shapes, *thru, jax.ShapeDtypeStruct((8, 128), F32)),
        in_specs=_hbm(2 * n) + ([] if after is None else [UNREAD]),
        out_specs=(*[SEM] * len(sem_shapes), *_hbm(2 * n), pl.BlockSpec(memory_space=pltpu.VMEM)),
        input_output_aliases={i: len(sem_shapes) + i for i in range(2 * n)},
        compiler_params=pltpu.CompilerParams(has_side_effects=SIDE_EFFECT),
    )(*[pltpu.with_memory_space_constraint(a, pltpu.HBM) for a in (*srcs, *lands)],
      *([] if after is None else [after]))
    k = len(sem_shapes)
    sems = [tuple(res[2 * g:2 * g + 2]) for g in range(len(groups))]
    return sems, list(res[k:k + n]), list(res[k + n:k + 2 * n]), res[-1]


def chip_copies_wait(srcs, lands, sems, after, scatter, name):
    n = len(srcs)
    after = list(after) if isinstance(after, (list, tuple)) else [after]

    def body(*refs):
        src, land = refs[:n], refs[n:2 * n]
        pair = refs[2 * n:2 * n + 2]
        x, y, _ = _place()
        for i in range(n):
            for j, (px, py) in enumerate(_other_chips(x, y)):
                copy = _chip_copy(src, land, pair, i, j, 2 * px + py, scatter)
                copy.wait_send()
                copy.wait_recv()

    res = pl.pallas_call(
        body, name=name,
        out_shape=[pltpu.HBM(a.shape, a.dtype) for a in (*srcs, *lands)],
        in_specs=[*_hbm(2 * n), SEM, SEM] + [UNREAD] * len(after),
        out_specs=_hbm(2 * n),
        input_output_aliases={i: i for i in range(2 * n)},
        compiler_params=pltpu.CompilerParams(has_side_effects=SIDE_EFFECT),
    )(*srcs, *lands, *sems, *after)
    return list(res[n:])


def small_all_gather(small, name):
    flips = [(fx, fy, fc) for fx in (0, 1) for fy in (0, 1) for fc in (0, 1)][1:]

    def body(in_ref, out_ref, send_sems, recv_sems, local_sem):
        x, y, c = _place()
        me = 4 * x + 2 * y + c

        def copy(k, slot):
            fx, fy, fc = flips[k]
            return pltpu.make_async_remote_copy(
                src_ref=in_ref, dst_ref=out_ref.at[slot], send_sem=send_sems.at[k], recv_sem=recv_sems.at[k],
                device_id=(x ^ fx, y ^ fy, c ^ fc), device_id_type=MESH)

        local = pltpu.make_async_copy(in_ref, out_ref.at[me], local_sem)
        local.start()
        for k in range(len(flips)):
            copy(k, me).start()
        for k, (fx, fy, fc) in enumerate(flips):
            copy(k, 4 * (x ^ fx) + 2 * (y ^ fy) + (c ^ fc)).wait()
        local.wait()

    return pl.pallas_call(
        body, name=name, in_specs=_hbm(1), out_specs=_hbm(1)[0],
        out_shape=jax.ShapeDtypeStruct((N_DEV,) + small.shape, small.dtype),
        scratch_shapes=[pltpu.SemaphoreType.DMA((len(flips),)), pltpu.SemaphoreType.DMA((len(flips),)),
                        pltpu.SemaphoreType.DMA],
    )(small)


def _sibling_copy(src, land, sems, i):
    x, y, c = _place()
    return pltpu.make_async_remote_copy(
        src_ref=src[i], dst_ref=land[i], send_sem=sems[0].at[i], recv_sem=sems[1].at[i],
        device_id=(x, y, 1 - c), device_id_type=MESH)


def sibling_copies_start(parts, name):
    n = len(parts)
    lands = [lax.empty(p.shape, p.dtype) for p in parts]

    def body(*refs):
        src, land, sems, token = refs[:n], refs[n:2 * n], refs[2 * n:2 * n + 2], refs[-1]
        for i in range(n):
            _sibling_copy(src, land, sems, i).start()
        token[...] = jnp.zeros_like(token)

    res = pl.pallas_call(
        body, name=name,
        out_shape=(pltpu.SemaphoreType.DMA((n,)), pltpu.SemaphoreType.DMA((n,)),
                   *[pltpu.HBM(a.shape, a.dtype) for a in (*parts, *lands)], jax.ShapeDtypeStruct((8, 128), F32)),
        in_specs=_hbm(2 * n),
        out_specs=(SEM, SEM, *_hbm(2 * n), pl.BlockSpec(memory_space=pltpu.VMEM)),
        input_output_aliases={i: 2 + i for i in range(2 * n)},
        compiler_params=pltpu.CompilerParams(has_side_effects=SIDE_EFFECT),
    )(*[pltpu.with_memory_space_constraint(a, pltpu.HBM) for a in (*parts, *lands)])
    return tuple(res[:2]), list(res[2:2 + n]), list(res[2 + n:2 + 2 * n]), res[-1]


def sibling_copies_wait(parts, lands, sems, after, name):
    n = len(parts)

    def body(*refs):
        src, land, sems = refs[:n], refs[n:2 * n], refs[2 * n:2 * n + 2]
        for i in range(n):
            copy = _sibling_copy(src, land, sems, i)
            copy.wait_send()
            copy.wait_recv()

    res = pl.pallas_call(
        body, name=name,
        out_shape=[pltpu.HBM(a.shape, a.dtype) for a in (*parts, *lands)],
        in_specs=[*_hbm(2 * n), SEM, SEM, UNREAD],
        out_specs=_hbm(2 * n),
        input_output_aliases={i: i for i in range(2 * n)},
        compiler_params=pltpu.CompilerParams(has_side_effects=SIDE_EFFECT),
    )(*parts, *lands, *sems, after)
    return list(res[n:])


def _row_tile(rows):
    for t in (256, 176, 128, 64, 32, 16, 8):
        if rows % t == 0:
            return t
    return rows


def chip_partial_sum(me, own_sm, recv, name):
    _, rows, cols = own_sm.shape
    tr = _row_tile(rows)

    def body(me_ref, own_ref, r0, r1, r2, r3, o_ref):
        acc = jnp.zeros((tr, cols), F32)
        for s, r_ref in enumerate((r0, r1, r2, r3)):
            acc = acc + jnp.where(me_ref[0] == s, own_ref[...], r_ref[...].astype(F32))
        o_ref[...] = acc

    def slot(s):
        return pl.BlockSpec((None, tr, cols), lambda i, me_ref, s=s: (s, i, 0))

    return pl.pallas_call(
        body, name=name,
        grid_spec=pltpu.PrefetchScalarGridSpec(
            num_scalar_prefetch=1, grid=(rows // tr,),
            in_specs=[pl.BlockSpec((None, tr, cols), lambda i, me_ref: (me_ref[0], i, 0))] + [slot(s) for s in range(4)],
            out_specs=pl.BlockSpec((tr, cols), lambda i, me_ref: (i, 0))),
        out_shape=jax.ShapeDtypeStruct((rows, cols), F32),
        compiler_params=_params("parallel"),
    )(me, own_sm, recv, recv, recv, recv)


def _adamw(w, g, m, v):
    m = ADAM_B1 * m + (1.0 - ADAM_B1) * g
    v = ADAM_B2 * v + (1.0 - ADAM_B2) * (g * g)
    m_hat = m / (1.0 - ADAM_B1 ** ADAM_STEP)
    v_hat = v / (1.0 - ADAM_B2 ** ADAM_STEP)
    delta = -ADAM_LR * (m_hat / (jnp.sqrt(v_hat) + ADAM_EPS) + ADAM_WD * w)
    return delta, m, v


def adamw_pair(part, sib, w, m, v, name):
    rows, cols = w.shape
    tr = _row_tile(rows)

    def body(p_ref, s_ref, w_ref, m_ref, v_ref, g_ref, d_ref, nm_ref, nv_ref):
        g = p_ref[...] + s_ref[...]
        g_ref[...] = g
        d_ref[...], nm_ref[...], nv_ref[...] = _adamw(w_ref[...], g, m_ref[...], v_ref[...])

    spec = pl.BlockSpec((tr, cols), lambda i: (i, 0))
    return pl.pallas_call(
        body, name=name, grid=(rows // tr,), in_specs=[spec] * 5, out_specs=[spec] * 4,
        out_shape=[jax.ShapeDtypeStruct((rows, cols), F32)] * 4,
        compiler_params=_params("parallel"),
    )(part, sib, w, m, v)


def adamw_small(g_all, w, m, v, name):
    def body(ga_ref, w_ref, m_ref, v_ref, g_ref, d_ref, nm_ref, nv_ref):
        g = ga_ref[0]
        for k in range(1, N_DEV):
            g = g + ga_ref[k]
        g_ref[...] = g
        d_ref[...], nm_ref[...], nv_ref[...] = _adamw(w_ref[...], g, m_ref[...], v_ref[...])

    return pl.pallas_call(
        body, name=name, out_shape=[jax.ShapeDtypeStruct(w.shape, F32)] * 4,
    )(g_all, w, m, v)


WEIGHTS = ("ffn1_norm_pre", "ffn1_w_in", "ffn1_w_out", "ffn1_norm_post", "mix_norm_pre", "w_in", "sinks",
           "mem_norm", "w_mem_kv", "w_gate", "b_gate", "w_o_a", "w_o_b", "w_o_m", "w_out", "mix_norm_post",
           "ffn2_norm_pre", "ffn2_w_in", "ffn2_w_out", "ffn2_norm_post")
GATHER_STAGES = (("ffn1_in", "ffn1_out"), ("mix_in",), ("mix_rest", "ffn2"))
GATHER_GROUPS = {"ffn1_in": ("ffn1_w_in",), "ffn1_out": ("ffn1_w_out",),
                 "mix_in": ("w_in", "w_gate"), "mix_rest": ("w_mem_kv", "w_o_a", "w_o_b", "w_o_m", "w_out"),
                 "ffn2": ("ffn2_w_in", "ffn2_w_out")}
GROUPS = {"ffn1_in": ("ffn1_w_in",), "ffn1_out": ("ffn1_w_out",),
          "mix": ("w_in", "w_gate", "w_mem_kv", "w_o_a", "w_o_b", "w_o_m", "w_out"),
          "ffn2_in": ("ffn2_w_in",), "ffn2_out": ("ffn2_w_out",)}
COLUMN_SHARDED = ("ffn1_w_in", "ffn2_w_in", "w_in", "w_gate", "w_o_a", "w_o_b", "w_o_m")
KEPT_SHARD_MAJOR = ("ffn1_w_in", "ffn2_w_in", "w_gate")
GAINS = ("ffn1_norm_pre", "ffn1_norm_post", "mix_norm_pre", "mem_norm", "mix_norm_post", "ffn2_norm_pre",
         "ffn2_norm_post")
SMALL_ROWS = 16


def _pack_small(t):
    sinks = jnp.pad(t["sinks"], ((0, 0), (0, D_MODEL - t["sinks"].shape[1])))
    rows = [t[k] for k in GAINS] + [t["b_gate"].reshape(3, D_MODEL), sinks]
    packed = jnp.concatenate(rows, axis=0)
    return jnp.pad(packed, ((0, SMALL_ROWS - packed.shape[0]), (0, 0)))


def _unpack_small(p):
    out = {k: p[i:i + 1] for i, k in enumerate(GAINS)}
    out["b_gate"] = p[7:10].reshape(1, 3 * D_MODEL)
    out["sinks"] = p[10:11, :4]
    return out


def kernel(x, mem, ffn1_norm_pre, ffn1_w_in, ffn1_w_out, ffn1_norm_post, mix_norm_pre, w_in, sinks, mem_norm, w_mem_kv, w_gate, b_gate, w_o_a, w_o_b, w_o_m, w_out, mix_norm_post, ffn2_norm_pre, ffn2_w_in, ffn2_w_out, ffn2_norm_post, loss_target, m_ffn1_norm_pre, m_ffn1_w_in, m_ffn1_w_out, m_ffn1_norm_post, m_mix_norm_pre, m_w_in, m_sinks, m_mem_norm, m_w_mem_kv, m_w_gate, m_b_gate, m_w_o_a, m_w_o_b, m_w_o_m, m_w_out, m_mix_norm_post, m_ffn2_norm_pre, m_ffn2_w_in, m_ffn2_w_out, m_ffn2_norm_post, v_ffn1_norm_pre, v_ffn1_w_in, v_ffn1_w_out, v_ffn1_norm_post, v_mix_norm_pre, v_w_in, v_sinks, v_mem_norm, v_w_mem_kv, v_w_gate, v_b_gate, v_w_o_a, v_w_o_b, v_w_o_m, v_w_out, v_mix_norm_post, v_ffn2_norm_pre, v_ffn2_w_in, v_ffn2_w_out, v_ffn2_norm_post):
    given = dict(locals())
    wt = {k: given[k] for k in WEIGHTS}
    mom = {k: given["m_" + k] for k in WEIGHTS}
    var = {k: given["v_" + k] for k in WEIGHTS}
    chip = (2 * lax.axis_index("x") + lax.axis_index("y")).astype(jnp.int32)
    me = chip.reshape(1)

    def landing_zone(own):
        return lax.dynamic_update_slice_in_dim(lax.empty((N_CHIPS,) + own.shape, own.dtype), own[None], chip, 0)

    started = {}
    tokens = []

    def stage_keys(stage):
        return [k for g in GATHER_STAGES[stage] for k in GATHER_GROUPS[g]]

    def prepare(stage):
        shards = [(wt[k][0] + tokens[0][0, 0] if tokens else wt[k][0]).astype(BF16) for k in stage_keys(stage)]
        return shards, [landing_zone(s) for s in shards]

    def start_gather(stage, after):
        groups, keys = GATHER_STAGES[stage], stage_keys(stage)
        members = [[keys.index(k) for k in GATHER_GROUPS[g]] for g in groups]
        sems, shards, lands, token = chip_copies_start(
            *prepared[stage], members, False, f"weight_gather_start_{stage}", after)
        tokens.append(token)
        for g, idx, pair in zip(groups, members, sems):
            started[g] = ([shards[i] for i in idx], [lands[i] for i in idx], pair)

    prepared = {0: prepare(0)}
    start_gather(0, None)
    prepared.update({stage: prepare(stage) for stage in range(1, len(GATHER_STAGES))})

    def weights_of(group, after):
        if group == GATHER_STAGES[0][0]:
            after = [after] + [a for stage in range(1, len(GATHER_STAGES)) for part in prepared[stage] for a in part]
        got = chip_copies_wait(*started[group], after, False, f"weight_gather_wait_{group}")
        stage = [s + 1 for s, groups in enumerate(GATHER_STAGES[:-1]) if groups[-1] == group]
        if stage:
            start_gather(stage[0], got[0])
        full = {}
        for k, g in zip(GATHER_GROUPS[group], got):
            if k in COLUMN_SHARDED:
                if k in ("ffn1_w_in", "ffn2_w_in"):
                    g = jnp.stack([g[0], g[2], g[1], g[3]])
                full[k] = jnp.swapaxes(g, 0, 1).reshape(g.shape[1], N_CHIPS * g.shape[2])
                if k == "w_in":
                    full[k] = to_kernel_heads(full[k])
            else:
                full[k] = g.reshape(N_CHIPS * g.shape[1], g.shape[2])
        return full

    in_flight = {}

    def send_grads(group, grads):
        def shard_major(k, g):
            if k in KEPT_SHARD_MAJOR:
                return g
            if k in COLUMN_SHARDED:
                return jnp.swapaxes(g.reshape(g.shape[0], N_CHIPS, g.shape[1] // N_CHIPS), 0, 1)
            return g.reshape(N_CHIPS, g.shape[0] // N_CHIPS, g.shape[1])

        own, wire = [], []
        for k in GROUPS[group]:
            g, rounded = grads[k] if isinstance(grads[k], (tuple, list)) else (grads[k], None)
            g = shard_major(k, from_kernel_heads(g) if k == "w_in" else g)
            own.append(g)
            wire.append(g.astype(BF16) if rounded is None else shard_major(k, rounded))
        zones = [landing_zone(lax.dynamic_index_in_dim(b, chip, 0, keepdims=False)) for b in wire]
        pair, wire, zones, sent = chip_copies_start(
            wire, zones, [list(range(len(wire)))], True, f"grad_scatter_start_{group}")
        in_flight[group] = (own, wire, zones, pair[0], sent)
        return sent

    gains = {k: wt[k] for k in GAINS}
    sq, dx, grads = layer_step(
        x[0], mem[0], loss_target[0], gains, sinks[0], b_gate, weights_of, send_grads, tokens[0][0, 0])
    loss = lax.psum(0.5 * sq[0, 0] / D_MODEL, ("x", "y", "c"))

    res = {}
    after = in_flight["ffn1_out"][4]
    swaps = []
    for stage in (("ffn2_in", "ffn2_out", "mix", "ffn1_in"), ("ffn1_out",)):
        names, parts = [], []
        for group in stage:
            own, wire, zones, pair, _ = in_flight[group]
            received = chip_copies_wait(wire, zones, pair, after, True, f"grad_scatter_wait_{group}")
            for k, g, r in zip(GROUPS[group], own, received):
                names.append(k)
                parts.append(chip_partial_sum(me, g, r, f"{k}_chip_sum"))
        pair, parts, lands, after = sibling_copies_start(parts, f"sibling_start_{stage[-1]}")
        swaps.append((stage[-1], names, parts, lands, pair))
    small_all = small_all_gather(_pack_small(grads), "small_grad_gather")
    packed = adamw_small(small_all, _pack_small(wt), _pack_small(mom), _pack_small(var), "small_adamw")
    after = packed[0]
    for tag, names, parts, lands, pair in swaps:
        sibs = sibling_copies_wait(parts, lands, pair, after, f"sibling_wait_{tag}")
        for k, p, s in zip(names, parts, sibs):
            res[k] = [t[None] for t in adamw_pair(p, s, wt[k][0], mom[k][0], var[k][0], f"{k}_adamw")]
        after = res[names[-1]][0]
    for idx, p in enumerate(packed):
        for k, t in _unpack_small(p).items():
            res.setdefault(k, [None] * 4)[idx] = t

    return (loss, dx[None], *[res[k][0] for k in WEIGHTS], *[res[k][1] for k in WEIGHTS],
            *[res[k][2] for k in WEIGHTS], *[res[k][3] for k in WEIGHTS])
```

```python
import functools

import jax
import jax.numpy as jnp
from jax import lax
from jax.experimental import pallas as pl
from jax.experimental.pallas import tpu as pltpu

F32 = jnp.float32
BF16 = jnp.bfloat16

D_MODEL = 1024
D_FF = 2816
HEAD = 128
N_CHIPS = 4
N_DEV = 8
EPS = 1e-6
NEG_INF = -1e30
ROPE_THETA = 10000.0
ATT_SCALE = HEAD ** -0.5

ADAM_LR = 0.001
ADAM_B1 = 0.9
ADAM_B2 = 0.999
ADAM_EPS = 1e-08
ADAM_WD = 0.01
ADAM_STEP = 10

VMEM_LIMIT = 52 * 2 ** 20
VMEM_LIMIT_LARGE = 60 * 2 ** 20
MESH = pl.DeviceIdType.MESH

QKV_W = 3840
DIL = ((128, 1), (512, 4), (2048, 16))
B_BASE, MQ, A_BASE = 0, 8, 12
_AQ, _AK, _AV, _BQ, _BK, _BV, _MQ = 0, 6, 12, 18, 22, 24, 26
HEAD_ORDER = tuple(
    [h for j in range(2) for h in (_BQ + 2 * j, _BQ + 2 * j + 1, _BK + j, _BV + j)]
    + [_MQ + i for i in range(4)]
    + [h for g in range(3) for i in range(2) for h in (_AQ + 2 * g + i, _AK + 2 * g + i, _AV + 2 * g + i)])
ROTARY_HEADS = tuple(p for p, h in enumerate(HEAD_ORDER) if h < _AV or _BQ <= h < _BV)


def to_kernel_heads(w):
    return jnp.concatenate([w[..., h * HEAD:(h + 1) * HEAD] for h in HEAD_ORDER], axis=-1)


def from_kernel_heads(w):
    place = {h: p for p, h in enumerate(HEAD_ORDER)}
    return jnp.concatenate([w[..., place[h] * HEAD:(place[h] + 1) * HEAD] for h in range(len(HEAD_ORDER))], axis=-1)

TM = 512
FF_T = D_FF // 2


def _params(*sem):
    return pltpu.CompilerParams(dimension_semantics=sem, vmem_limit_bytes=VMEM_LIMIT)


def _dot(a, b):
    return jnp.dot(a, b, preferred_element_type=F32)


def _dot_nt(a, b):
    return lax.dot_general(a, b, (((1,), (1,)), ((), ())), preferred_element_type=F32)


def _dot_tn(a, b):
    return lax.dot_general(a, b, (((0,), (0,)), ((), ())), preferred_element_type=F32)


def _rstd(x):
    return lax.rsqrt(jnp.mean(x * x, axis=-1, keepdims=True) + EPS)


def _sigmoid(x):
    return 0.5 * jnp.tanh(0.5 * x) + 0.5


def _ffn_perm(k):
    return (k % 2) * 2 + k // 2


UNREAD = pl.BlockSpec(memory_space=pl.ANY)


def _resident(arr):
    return pl.BlockSpec(arr.shape, lambda *_: (0,) * arr.ndim, pipeline_mode=pl.Buffered(1))


def rms_scale(x, g, name, after):
    T, D = x.shape
    tm = 1024

    def body(x_ref, g_ref, _, o_ref):
        v = x_ref[...]
        o_ref[...] = (v * _rstd(v) * g_ref[...]).astype(BF16)

    spec = pl.BlockSpec((tm, D), lambda i: (i, 0))
    return pl.pallas_call(
        body, name=name, grid=(T // tm,), in_specs=[spec, _resident(g), UNREAD], out_specs=spec,
        out_shape=jax.ShapeDtypeStruct((T, D), BF16), compiler_params=_params("parallel"),
    )(x, g, after)


def ffn_in(h, g, w, name, xn=None, half=None, into=None):
    T, D = h.shape
    normed = xn is not None
    steps = T // TM if half is None else T // TM // 2
    first = 0 if half is None else half * steps

    def body(h_ref, g_ref, w_ref, *rest):
        outs = rest[len(into or ()):]
        if normed:
            xn, (gu_ref, a_ref) = h_ref[...], outs
        else:
            xn_ref, gu_ref, a_ref = outs
            x = h_ref[...]
            xn = (x * _rstd(x) * g_ref[...]).astype(BF16)
            xn_ref[...] = xn
        for j in range(2):
            gu = _dot(xn, w_ref[:, j * 2 * FF_T:(j + 1) * 2 * FF_T])
            gu_ref[:, j * 2 * FF_T:(j + 1) * 2 * FF_T] = gu.astype(BF16)
            gate, up = gu[:, :FF_T], gu[:, FF_T:]
            a_ref[:, j * FF_T:(j + 1) * FF_T] = (gate * _sigmoid(gate) * up).astype(BF16)

    def rows(width):
        return pl.BlockSpec((TM, width), lambda i: (i + first, 0))

    n_out = 2 if normed else 3
    res = pl.pallas_call(
        body, name=name,
        grid=(steps,),
        in_specs=[rows(D), _resident(g), _resident(w)] + [UNREAD] * len(into or ()),
        out_specs=[rows(D)] * (not normed) + [rows(2 * D_FF), rows(D_FF)],
        out_shape=[jax.ShapeDtypeStruct((T, D), BF16)] * (not normed)
                  + [jax.ShapeDtypeStruct((T, 2 * D_FF), BF16), jax.ShapeDtypeStruct((T, D_FF), BF16)],
        input_output_aliases={3 + k: n_out - 2 + k for k in range(len(into or ()))},
        compiler_params=_params("parallel"),
    )(xn if normed else h, g, w, *(into or ()))
    return (xn, *res) if normed else tuple(res)


def mm_norm_res(a, w, h_in, g, coef, name, target=None):
    T, K = a.shape
    D = w.shape[1]
    final = target is not None

    def body(*refs):
        if final:
            a_ref, w_ref, h_ref, g_ref, t_ref, f_ref, o_ref, l_ref = refs
        else:
            a_ref, w_ref, h_ref, g_ref, f_ref, o_ref = refs
        f = _dot(a_ref[...], w_ref[...])
        f_ref[...] = f
        y = h_ref[...] + coef * (f * _rstd(f) * g_ref[...])
        if final:
            err = y - t_ref[...]
            o_ref[...] = err * (1.0 / D)

            @pl.when(pl.program_id(0) == 0)
            def _():
                l_ref[...] = jnp.zeros_like(l_ref)

            l_ref[...] += jnp.sum(err * err)
        else:
            o_ref[...] = y

    row = pl.BlockSpec((TM, D), lambda i: (i, 0))
    in_specs = [pl.BlockSpec((TM, K), lambda i: (i, 0)),
                _resident(w),
                row, pl.BlockSpec((1, D), lambda i: (0, 0))]
    out_specs = [row, row]
    out_shape = [jax.ShapeDtypeStruct((T, D), F32), jax.ShapeDtypeStruct((T, D), F32)]
    args = [a, w, h_in, g]
    if final:
        in_specs.append(row)
        args.append(target)
        out_specs.append(pl.BlockSpec((8, 128), lambda i: (0, 0)))
        out_shape.append(jax.ShapeDtypeStruct((8, 128), F32))
    return pl.pallas_call(
        body, name=name, grid=(T // TM,), in_specs=in_specs, out_specs=out_specs, out_shape=out_shape,
        compiler_params=_params("arbitrary"),
    )(*args)


def _rope(x, cos, sin_signed):
    return x * cos + pltpu.roll(x, HEAD // 2, axis=1) * sin_signed


def _unrope(x, cos, sin_signed):
    return x * cos - pltpu.roll(x, HEAD // 2, axis=1) * sin_signed


def mix_in(h, g, w, w_gate, b_gate, cos, sin_signed, name):
    T, D = h.shape
    tn = 768

    def body(h_ref, g_ref, w_ref, wg_ref, b_ref, c_ref, s_ref, u_ref, o_ref, gt_ref):
        x = h_ref[...]
        u = (x * _rstd(x) * g_ref[...]).astype(BF16)
        u_ref[...] = u
        c, s = c_ref[...], s_ref[...]
        for j in range(QKV_W // tn):
            acc = _dot(u, w_ref[:, j * tn:(j + 1) * tn])
            for hd in range(tn // HEAD):
                head = j * (tn // HEAD) + hd
                part = acc[:, hd * HEAD:(hd + 1) * HEAD]
                if head in ROTARY_HEADS:
                    part = _rope(part, c, s)
                o_ref[:, head * HEAD:(head + 1) * HEAD] = part.astype(BF16)
        for j in range(w_gate.shape[1] // tn):
            cols = slice(j * tn, (j + 1) * tn)
            gt_ref[:, cols] = _sigmoid(_dot(u, wg_ref[:, cols]) + b_ref[:, cols]).astype(BF16)

    def rows(width):
        return pl.BlockSpec((TM, width), lambda i: (i, 0))

    return pl.pallas_call(
        body, name=name,
        grid=(T // TM,),
        in_specs=[rows(D), _resident(g), _resident(w), _resident(w_gate), _resident(b_gate), rows(HEAD), rows(HEAD)],
        out_specs=[rows(D), rows(QKV_W), rows(w_gate.shape[1])],
        out_shape=[jax.ShapeDtypeStruct((T, D), BF16), jax.ShapeDtypeStruct((T, QKV_W), BF16),
                   jax.ShapeDtypeStruct((T, w_gate.shape[1]), BF16)],
        compiler_params=_params("parallel"),
    )(h, g, w, w_gate, b_gate, cos, sin_signed)


def gate_merge_out(gt, o_a, o_b, o_m, w_a, w_b, w_m, w_out, h_in, g, name):
    T = gt.shape[0]
    D = D_MODEL

    def body(gt_ref, oa_ref, ob_ref, om_ref, wa_ref, wb_ref, wm_ref, wo_ref, h_ref, g_ref, m_ref, f_ref, o_ref):
        acc = gt_ref[:, :D].astype(F32) * _dot(oa_ref[...], wa_ref[...])
        acc += gt_ref[:, D:2 * D].astype(F32) * _dot(ob_ref[...], wb_ref[...])
        acc += gt_ref[:, 2 * D:].astype(F32) * _dot(om_ref[...], wm_ref[...])
        merged = acc.astype(BF16)
        m_ref[...] = merged
        f = _dot(merged, wo_ref[...])
        f_ref[...] = f
        o_ref[...] = h_ref[...] + f * _rstd(f) * g_ref[...]

    def rows(width):
        return pl.BlockSpec((TM, width), lambda i: (i, 0))

    return pl.pallas_call(
        body, name=name, grid=(T // TM,),
        in_specs=[rows(3 * D), rows(o_a.shape[1]), rows(o_b.shape[1]), rows(o_m.shape[1]),
                  _resident(w_a), _resident(w_b), _resident(w_m), _resident(w_out), rows(D), _resident(g)],
        out_specs=[rows(D), rows(D), rows(D)],
        out_shape=[jax.ShapeDtypeStruct((T, D), BF16), jax.ShapeDtypeStruct((T, D), F32),
                   jax.ShapeDtypeStruct((T, D), F32)],
        compiler_params=_params("parallel"),
    )(gt, o_a, o_b, o_m, w_a, w_b, w_m, w_out, h_in, g)


def _band_rows(start, r):
    return pl.ds(start, HEAD) if r == 1 else pl.ds(start, HEAD, stride=r)


def _band_mask(max_dist, first_has_prev):
    row = lax.broadcasted_iota(jnp.int32, (HEAD, 2 * HEAD), 0)
    col = lax.broadcasted_iota(jnp.int32, (HEAD, 2 * HEAD), 1)
    dist = row + HEAD - col
    band = (dist >= 0) & (dist <= max_dist)
    return band, band & (col >= jnp.where(first_has_prev, 0, HEAD))


def _stack(parts):
    return parts[0] if len(parts) == 1 else jnp.concatenate(parts, axis=0)


def _band_specs(BT, SB, nsub, base, grp):
    stride = grp + 2

    def cur(off, width):
        return pl.BlockSpec((BT, width * HEAD), lambda h, i: (i, (base + h * stride + off) // width))

    def prev(off):
        return pl.BlockSpec((SB, HEAD), lambda h, i: (jnp.maximum(i * nsub - 1, 0), base + h * stride + off))

    return cur(0, grp), cur(grp, 1), prev(grp), cur(grp + 1, 1), prev(grp + 1)


def band_fwd(qkv, sinks, *, r, base, hkv, grp, max_dist, out_dtype, name, merge=None):
    T, W = qkv.shape
    SB = HEAD * r
    BT = min(2048, T)
    nsub, nib = BT // SB, T // BT
    hq = hkv * grp
    heads = [slice(g * HEAD, (g + 1) * HEAD) for g in range(grp)]
    others = [] if merge is None else [*merge[0], *merge[1]]

    def body(sink_ref, q_ref, kc_ref, kp_ref, vc_ref, vp_ref, *rest):
        joint_o, joint_l = rest[len(others):len(others) + 2]
        qf, kf, vf = rest[len(others) + 2:len(others) + 5]
        o_ref, l_ref = rest[len(others) + 5:] if others else (joint_o, joint_l)
        kvh, ib = pl.program_id(0), pl.program_id(1)
        qf[...] = q_ref[...].astype(F32)
        kf[:SB] = kp_ref[...].astype(F32)
        kf[SB:] = kc_ref[...].astype(F32)
        vf[:SB] = vp_ref[...].astype(F32)
        vf[SB:] = vc_ref[...].astype(F32)
        band, band_first = _band_mask(max_dist, ib > 0)
        for c in range(r):
            k_old, v_old = kf[_band_rows(c, r)], vf[_band_rows(c, r)]
            for j in range(nsub):
                mask = band_first if j == 0 else band
                rows = _band_rows(j * SB + c, r)
                k_own, v_own = kf[_band_rows((j + 1) * SB + c, r)], vf[_band_rows((j + 1) * SB + c, r)]
                kcat = jnp.concatenate([k_old, k_own], axis=0).astype(BF16)
                vcat = jnp.concatenate([v_old, v_own], axis=0).astype(BF16)
                k_old, v_old = k_own, v_own
                s_all = _dot_nt(_stack([qf[rows, cols] for cols in heads]).astype(BF16), kcat) * ATT_SCALE
                probs, tots = [], []
                for g, cols in enumerate(heads):
                    s = jnp.where(mask, s_all[cols], NEG_INF)
                    sk = sink_ref[kvh * grp + g]
                    m = jnp.maximum(jnp.max(s, axis=-1, keepdims=True), sk)
                    p = jnp.exp(s - m)
                    tot = jnp.sum(p, axis=-1, keepdims=True) + jnp.exp(sk - m)
                    probs.append(p.astype(BF16))
                    tots.append(tot)
                    l_ref[rows, cols] = jnp.broadcast_to(m + jnp.log(tot), (HEAD, HEAD))
                o_all = _dot(_stack(probs), vcat)
                for g, cols in enumerate(heads):
                    o_ref[rows, cols] = (o_all[cols] / tots[g]).astype(o_ref.dtype)

        if others:
            half = len(others) // 2
            outs = [ref[...] for ref in rest[:half]] + [o_ref[...]]
            logs = [ref[...] for ref in rest[half:len(others)]] + [l_ref[...]]
            top = functools.reduce(jnp.maximum, logs)
            weights = [jnp.exp(lg - top) for lg in logs]
            total = functools.reduce(jnp.add, weights)
            mixed = functools.reduce(jnp.add, [wgt * out for wgt, out in zip(weights, outs)])
            joint_o[...] = (mixed / total).astype(out_dtype)
            joint_l[...] = top + jnp.log(total)

    out_spec = pl.BlockSpec((BT, grp * HEAD), lambda h, i: (i, h))
    own = [pltpu.VMEM((BT, grp * HEAD), F32)] * 2 if others else []
    return pl.pallas_call(
        body, name=name, grid=(hkv, nib),
        in_specs=[pl.BlockSpec(memory_space=pltpu.SMEM), *_band_specs(BT, SB, nsub, base, grp)]
                 + [out_spec] * len(others),
        out_specs=[out_spec, out_spec],
        out_shape=[jax.ShapeDtypeStruct((T, hq * HEAD), out_dtype), jax.ShapeDtypeStruct((T, hq * HEAD), F32)],
        scratch_shapes=[pltpu.VMEM((BT, grp * HEAD), F32), pltpu.VMEM((SB + BT, HEAD), F32),
                        pltpu.VMEM((SB + BT, HEAD), F32)] + own,
        compiler_params=_params("parallel", "arbitrary"),
    )(sinks, qkv, qkv, qkv, qkv, qkv, *others)


def band_bwd(qkv, dqkv, do, o, lse, cos, sin_signed, sinks, *, r, base, hkv, grp, max_dist, name):
    T, W = qkv.shape
    SB = HEAD * r
    BT = min(max(2048, 2 * SB), T)
    nsub, nib = BT // SB, T // BT
    nblk = T // SB
    with_sink = sinks is not None
    heads = [slice(g * HEAD, (g + 1) * HEAD) for g in range(grp)]

    def body(*refs):
        if with_sink:
            sink_ref, refs = refs[0], refs[1:]
        (q_ref, kc_ref, kp_ref, vc_ref, vp_ref, qn_ref, do_ref, don_ref, o_ref, on_ref, l_ref, ln_ref,
         c_ref, s_ref, _) = refs[:15]
        out_ref = refs[15]
        ds_ref = refs[16] if with_sink else None
        qf, dof, of, kf, vf, dqf, dkf, dvf = refs[-8:]
        kvh, ib = pl.program_id(0), pl.program_id(1)
        for buf, cur_ref, nxt_ref in ((qf, q_ref, qn_ref), (dof, do_ref, don_ref), (of, o_ref, on_ref)):
            buf[:BT] = cur_ref[...].astype(F32)
            buf[BT:] = nxt_ref[...].astype(F32)
        kf[:SB] = kp_ref[...].astype(F32)
        kf[SB:] = kc_ref[...].astype(F32)
        vf[:SB] = vp_ref[...].astype(F32)
        vf[SB:] = vc_ref[...].astype(F32)
        band, band_first = _band_mask(max_dist, ib > 0)
        if with_sink:
            @pl.when(ib == 0)
            def _():
                ds_ref[...] = jnp.zeros_like(ds_ref)

        def grads(rows, logzs, keys, vals, mask):
            q = _stack([qf[rows, cols] for cols in heads]).astype(BF16)
            dout = _stack([dof[rows, cols] for cols in heads]).astype(BF16)
            s_all = _dot_nt(q, keys) * ATT_SCALE
            dp_all = _dot_nt(dout, vals)
            probs, dss, deltas = [], [], []
            for g, cols in enumerate(heads):
                delta = jnp.sum(dof[rows, cols] * of[rows, cols], axis=-1, keepdims=True)
                p = jnp.exp(jnp.where(mask, s_all[cols], NEG_INF) - logzs[g][:, :1])
                probs.append(p.astype(BF16))
                dss.append((p * (dp_all[cols] - delta) * ATT_SCALE).astype(BF16))
                deltas.append(delta)
            return q, dout, _stack(probs), _stack(dss), deltas

        row = lax.broadcasted_iota(jnp.int32, (HEAD, HEAD), 0)
        col = lax.broadcasted_iota(jnp.int32, (HEAD, HEAD), 1)
        reach = col >= row + jnp.where(ib < nib - 1, HEAD - max_dist, 2 * HEAD)
        for c in range(r):
            k_old, v_old = kf[_band_rows(c, r)], vf[_band_rows(c, r)]
            dk_own = dv_own = None
            for j in range(nsub):
                rows = _band_rows(j * SB + c, r)
                k_own, v_own = kf[_band_rows((j + 1) * SB + c, r)], vf[_band_rows((j + 1) * SB + c, r)]
                kcat = jnp.concatenate([k_old, k_own], axis=0).astype(BF16)
                vcat = jnp.concatenate([v_old, v_own], axis=0).astype(BF16)
                logzs = [l_ref[rows, cols] for cols in heads]
                q, dout, p, ds, deltas = grads(rows, logzs, kcat, vcat, band_first if j == 0 else band)
                dq = _dot(ds, kcat)
                for g, cols in enumerate(heads):
                    dqf[rows, cols] = dq[cols]
                    if with_sink:
                        p_sink = jnp.exp(sink_ref[kvh * grp + g] - logzs[g][:, :1])
                        ds_ref[g * 8:(g + 1) * 8] += jnp.sum(p_sink * deltas[g])
                dk, dv = _dot_tn(ds, q), _dot_tn(p, dout)
                if j > 0:
                    done = _band_rows((j - 1) * SB + c, r)
                    dkf[done] = dk_own + dk[:HEAD]
                    dvf[done] = dv_own + dv[:HEAD]
                dk_own, dv_own = dk[HEAD:], dv[HEAD:]
                k_old, v_old = k_own, v_own
            logzs = [ln_ref[_band_rows(c, r), cols] for cols in heads]
            q, dout, p, ds, _ = grads(_band_rows(BT + c, r), logzs, k_old.astype(BF16), v_old.astype(BF16), reach)
            done = _band_rows((nsub - 1) * SB + c, r)
            dkf[done] = dk_own + _dot_tn(ds, q)
            dvf[done] = dv_own + _dot_tn(p, dout)

        cs, sn = c_ref[...], s_ref[...]
        for cols in heads:
            out_ref[:, cols] = _unrope(dqf[:, cols], cs, sn).astype(BF16)
        out_ref[:, grp * HEAD:(grp + 1) * HEAD] = _unrope(dkf[...], cs, sn).astype(BF16)
        out_ref[:, (grp + 1) * HEAD:] = dvf[...].astype(BF16)

    def nxt_row(i):
        return jnp.minimum((i + 1) * nsub, nblk - 1)

    stride = grp + 2
    q_next = pl.BlockSpec((SB, grp * HEAD), lambda h, i: (nxt_row(i), (base + h * stride) // grp))
    head_cur = pl.BlockSpec((BT, grp * HEAD), lambda h, i: (i, h))
    head_next = pl.BlockSpec((SB, grp * HEAD), lambda h, i: (nxt_row(i), h))
    table = pl.BlockSpec((BT, HEAD), lambda h, i: (i, 0))

    in_specs = [*_band_specs(BT, SB, nsub, base, grp), q_next,
                head_cur, head_next, head_cur, head_next, head_cur, head_next, table, table, UNREAD]
    args = [qkv, qkv, qkv, qkv, qkv, qkv, do, do, o, o, lse, lse, cos, sin_signed, dqkv]
    out_specs = [pl.BlockSpec((BT, stride * HEAD), lambda h, i: (i, base // stride + h))]
    out_shape = [jax.ShapeDtypeStruct(dqkv.shape, dqkv.dtype)]
    if with_sink:
        in_specs.insert(0, pl.BlockSpec(memory_space=pltpu.SMEM))
        args.insert(0, sinks)
        out_specs.append(pl.BlockSpec((None, grp * 8, HEAD), lambda h, i: (h, 0, 0)))
        out_shape.append(jax.ShapeDtypeStruct((hkv, grp * 8, HEAD), F32))
    wide = pltpu.VMEM((BT + SB, grp * HEAD), F32)
    tall = pltpu.VMEM((SB + BT, HEAD), F32)
    grad = pltpu.VMEM((BT, HEAD), F32)
    return pl.pallas_call(
        body, name=name, grid=(hkv, nib), in_specs=in_specs, out_specs=out_specs, out_shape=out_shape,
        input_output_aliases={len(args) - 1: 0},
        scratch_shapes=[wide, wide, wide, tall, tall, pltpu.VMEM((BT, grp * HEAD), F32), grad, grad],
        compiler_params=pltpu.CompilerParams(dimension_semantics=("parallel", "arbitrary"),
                                             vmem_limit_bytes=VMEM_LIMIT_LARGE),
    )(*args)


M_HEADS = 4


def mem_kv(mem, g, w, name):
    n, D = mem.shape

    def body(m_ref, g_ref, w_ref, mn_ref, kv_ref):
        x = m_ref[...]
        mn = (x * _rstd(x) * g_ref[...]).astype(BF16)
        mn_ref[...] = mn
        kv_ref[...] = _dot(mn, w_ref[...]).astype(BF16)

    return pl.pallas_call(
        body, name=name,
        out_shape=[jax.ShapeDtypeStruct((n, D), BF16), jax.ShapeDtypeStruct((n, w.shape[1]), BF16)],
        compiler_params=pltpu.CompilerParams(vmem_limit_bytes=VMEM_LIMIT),
    )(mem, g, w)


def mem_fwd(qkv, mkv, name):
    T = qkv.shape[0]
    n = mkv.shape[0]
    RB = 1024

    def body(q_ref, kv_ref, o_ref, l_ref):
        for h in range(M_HEADS):
            cols = slice(h * HEAD, (h + 1) * HEAD)
            s = _dot_nt(q_ref[:, cols], kv_ref[:, cols]) * ATT_SCALE
            m = jnp.max(s, axis=-1, keepdims=True)
            p = jnp.exp(s - m)
            den = jnp.sum(p, axis=-1, keepdims=True)
            vals = kv_ref[:, (M_HEADS + h) * HEAD:(M_HEADS + h + 1) * HEAD]
            o_ref[:, cols] = (_dot(p.astype(BF16), vals) / den).astype(BF16)
            l_ref[:, cols] = jnp.broadcast_to(m + jnp.log(den), (RB, HEAD))

    out = pl.BlockSpec((RB, M_HEADS * HEAD), lambda i: (i, 0))
    return pl.pallas_call(
        body, name=name, grid=(T // RB,),
        in_specs=[pl.BlockSpec((RB, M_HEADS * HEAD), lambda i: (i, MQ // M_HEADS)), _resident(mkv)],
        out_specs=[out, out],
        out_shape=[jax.ShapeDtypeStruct((T, M_HEADS * HEAD), BF16), jax.ShapeDtypeStruct((T, M_HEADS * HEAD), F32)],
        compiler_params=_params("parallel"),
    )(qkv, mkv)


def mem_bwd(qkv, dqkv, mkv, do, o, lse, name):
    T = qkv.shape[0]
    n = mkv.shape[0]
    RB = 1024

    def body(q_ref, kv_ref, do_ref, o_ref, l_ref, _, dq_ref, dk_ref, dv_ref):
        @pl.when(pl.program_id(0) == 0)
        def _():
            dk_ref[...] = jnp.zeros_like(dk_ref)
            dv_ref[...] = jnp.zeros_like(dv_ref)

        for h in range(M_HEADS):
            cols = slice(h * HEAD, (h + 1) * HEAD)
            keys, vals = kv_ref[:, cols], kv_ref[:, (M_HEADS + h) * HEAD:(M_HEADS + h + 1) * HEAD]
            q, dout = q_ref[:, cols], do_ref[:, cols]
            delta = jnp.sum(dout.astype(F32) * o_ref[:, cols].astype(F32), axis=-1, keepdims=True)
            p = jnp.exp(_dot_nt(q, keys) * ATT_SCALE - l_ref[:, cols][:, :1])
            ds = (p * (_dot_nt(dout, vals) - delta) * ATT_SCALE).astype(BF16)
            dq_ref[:, cols] = _dot(ds, keys).astype(BF16)
            dk_ref[:, cols] += _dot_tn(ds, q)
            dv_ref[:, cols] += _dot_tn(p.astype(BF16), dout)

    wide = M_HEADS * HEAD
    tok = pl.BlockSpec((RB, wide), lambda i: (i, 0))
    q_cols = pl.BlockSpec((RB, wide), lambda i: (i, MQ // M_HEADS))
    slot = pl.BlockSpec((n, wide), lambda i: (0, 0))
    return pl.pallas_call(
        body, name=name, grid=(T // RB,),
        in_specs=[q_cols, _resident(mkv), tok, tok, tok, UNREAD],
        out_specs=[q_cols, slot, slot],
        out_shape=[jax.ShapeDtypeStruct(dqkv.shape, dqkv.dtype),
                   jax.ShapeDtypeStruct((n, wide), F32), jax.ShapeDtypeStruct((n, wide), F32)],
        input_output_aliases={5: 0},
        compiler_params=_params("arbitrary"),
    )(qkv, mkv, do, o, lse, dqkv)


def mem_kv_bwd(mem, g, mem_n, w, dmkv, name):
    n, D = mem.shape

    def body(m_ref, g_ref, mn_ref, w_ref, d_ref, dw_ref, dg_ref):
        d = d_ref[...].astype(BF16)
        dw_ref[...] = _dot_tn(mn_ref[...], d)
        x = m_ref[...]
        dg_ref[...] = jnp.sum(_dot_nt(d, w_ref[...]) * (x * _rstd(x)), axis=0, keepdims=True)

    return pl.pallas_call(
        body, name=name,
        out_shape=[jax.ShapeDtypeStruct(w.shape, F32), jax.ShapeDtypeStruct((1, D), F32)],
        compiler_params=pltpu.CompilerParams(vmem_limit_bytes=VMEM_LIMIT),
    )(mem, g, mem_n, w, dmkv)


def _rms_bwd(dn, f, g):
    r = _rstd(f)
    fhat = f * r
    dfhat = dn * g
    df = r * (dfhat - fhat * jnp.mean(dfhat * fhat, axis=-1, keepdims=True))
    return df, jnp.sum(dn * fhat, axis=0, keepdims=True)


def ffn_tokens_bwd(dh, f, h_in, gu, g_pre, g_post, w_in, w_out, coef, name, after):
    T, D = dh.shape

    def body(dh_ref, f_ref, h_ref, gu_ref, gpre_ref, gpost_ref, win_ref, wout_ref, _,
             df_ref, dgu_ref, dhin_ref, dgpre_ref, dgpost_ref, dxn_ref):
        i, j = pl.program_id(0), pl.program_id(1)

        @pl.when(j == 0)
        def _():
            @pl.when(i == 0)
            def _():
                dgpre_ref[...] = jnp.zeros_like(dgpre_ref)
                dgpost_ref[...] = jnp.zeros_like(dgpost_ref)

            df, dg_post = _rms_bwd(coef * dh_ref[...], f_ref[...], gpost_ref[...])
            dgpost_ref[...] += dg_post
            df_ref[...] = df.astype(BF16)

        for jj in range(2):
            @pl.when(j == jj)
            def _(jj=jj):
                lo, mid, hi = 2 * jj * FF_T, (2 * jj + 1) * FF_T, (2 * jj + 2) * FF_T
                da = _dot_nt(df_ref[...], wout_ref[jj * FF_T:(jj + 1) * FF_T, :])
                gate = gu_ref[:, :FF_T].astype(F32)
                up = gu_ref[:, FF_T:].astype(F32)
                sig = _sigmoid(gate)
                dgate = (da * up * sig * (1.0 + gate * (1.0 - sig))).astype(BF16)
                dup = (da * gate * sig).astype(BF16)
                dgu_ref[:, :FF_T] = dgate
                dgu_ref[:, FF_T:] = dup
                part = _dot_nt(dgate, win_ref[:, lo:mid]) + _dot_nt(dup, win_ref[:, mid:hi])
                if jj == 0:
                    dxn_ref[...] = part
                else:
                    h = h_ref[...]
                    r = _rstd(h)
                    xhat = h * r
                    dxn = dxn_ref[...] + part
                    dxhat = dxn * gpre_ref[...]
                    dhin_ref[...] = dh_ref[...] + r * (dxhat - xhat * jnp.mean(dxhat * xhat, axis=-1, keepdims=True))
                    dgpre_ref[...] += jnp.sum(dxn * xhat, axis=0, keepdims=True)

    row = pl.BlockSpec((TM, D), lambda i, j: (i, 0))
    wide = pl.BlockSpec((TM, 2 * FF_T), lambda i, j: (i, j))
    vec = pl.BlockSpec((1, D), lambda i, j: (0, 0))
    return pl.pallas_call(
        body, name=name, grid=(T // TM, 2),
        in_specs=[row, row, row, wide, _resident(g_pre), _resident(g_post), _resident(w_in), _resident(w_out),
                  UNREAD],
        out_specs=[row, wide, row, vec, vec],
        out_shape=[jax.ShapeDtypeStruct((T, D), BF16), jax.ShapeDtypeStruct((T, 2 * D_FF), BF16),
                   jax.ShapeDtypeStruct((T, D), F32), jax.ShapeDtypeStruct((1, D), F32),
                   jax.ShapeDtypeStruct((1, D), F32)],
        scratch_shapes=[pltpu.VMEM((TM, D), F32)],
        compiler_params=pltpu.CompilerParams(dimension_semantics=("arbitrary", "arbitrary"),
                                             vmem_limit_bytes=VMEM_LIMIT_LARGE),
    )(dh, f, h_in, gu, g_pre, g_post, w_in, w_out, after)


def mm_nt_norm_bwd(pieces, h_in, dh_out, g, name, after):
    T, D = h_in.shape

    def body(*refs):
        ab = refs[:2 * len(pieces)]
        h_ref, dh_ref, g_ref, _, o_ref, dg_ref = refs[2 * len(pieces):]
        dxn = _dot_nt(ab[0][...], ab[1][...])
        for p in range(1, len(pieces)):
            dxn += _dot_nt(ab[2 * p][...], ab[2 * p + 1][...])
        h = h_ref[...]
        r = _rstd(h)
        xhat = h * r
        dxhat = dxn * g_ref[...]
        o_ref[...] = dh_ref[...] + r * (dxhat - xhat * jnp.mean(dxhat * xhat, axis=-1, keepdims=True))

        @pl.when(pl.program_id(0) == 0)
        def _():
            dg_ref[...] = jnp.zeros_like(dg_ref)

        dg_ref[...] += jnp.sum(dxn * xhat, axis=0, keepdims=True)

    in_specs, args = [], []
    for a, w in pieces:
        in_specs += [pl.BlockSpec((TM, a.shape[1]), lambda i: (i, 0)), _resident(w)]
        args += [a, w]
    row = pl.BlockSpec((TM, D), lambda i: (i, 0))
    return pl.pallas_call(
        body, name=name, grid=(T // TM,),
        in_specs=in_specs + [row, row, _resident(g), UNREAD],
        out_specs=[row, pl.BlockSpec((1, D), lambda i: (0, 0))],
        out_shape=[jax.ShapeDtypeStruct((T, D), F32), jax.ShapeDtypeStruct((1, D), F32)],
        compiler_params=_params("arbitrary"),
    )(*args, h_in, dh_out, g, after)


def gate_merge_out_bwd(dh, f, g, w_out, merged, gt, o_a, o_b, o_m, w_a, w_b, w_m, name, after):
    T = dh.shape[0]
    D = D_MODEL
    branch = ((o_a, w_a), (o_b, w_b), (o_m, w_m))

    def body(dh_ref, f_ref, g_ref, wo_ref, m_ref, gt_ref, oa_ref, ob_ref, om_ref, wa_ref, wb_ref, wm_ref, _,
             dg_ref, dwo_ref, dgt_ref, doa_ref, dob_ref, dom_ref, db_ref, dwa_ref, dwb_ref, dwm_ref):
        @pl.when(pl.program_id(0) == 0)
        def _():
            for acc in (dg_ref, dwo_ref, db_ref, dwa_ref, dwb_ref, dwm_ref):
                acc[...] = jnp.zeros_like(acc)

        df, dg = _rms_bwd(dh_ref[...], f_ref[...], g_ref[...])
        dg_ref[...] += dg
        df = df.astype(BF16)
        dwo_ref[...] += _dot_tn(m_ref[...], df)
        dmf = _dot_nt(df, wo_ref[...])
        for x, (o_ref, w_ref, do_ref, dw_ref) in enumerate(((oa_ref, wa_ref, doa_ref, dwa_ref),
                                                           (ob_ref, wb_ref, dob_ref, dwb_ref),
                                                           (om_ref, wm_ref, dom_ref, dwm_ref))):
            cols = slice(x * D, (x + 1) * D)
            gx = gt_ref[:, cols].astype(F32)
            w = w_ref[...]
            dpre = dmf * _dot(o_ref[...], w) * gx * (1.0 - gx)
            dgt_ref[:, cols] = dpre.astype(BF16)
            db_ref[:, cols] += jnp.sum(dpre, axis=0, keepdims=True)
            dp = (dmf * gx).astype(BF16)
            do_ref[...] = _dot_nt(dp, w).astype(BF16)
            dw_ref[...] += _dot_tn(dp, o_ref[...])

    def rows(width):
        return pl.BlockSpec((TM, width), lambda i: (i, 0))

    def kept(shape):
        return pl.BlockSpec(shape, lambda i: (0,) * len(shape))

    widths = [o.shape[1] for o, _ in branch]
    sums = [(1, D), (D, D), (1, 3 * D)] + [(D, k) for k in widths]
    return pl.pallas_call(
        body, name=name, grid=(T // TM,),
        in_specs=[rows(D), rows(D), _resident(g), _resident(w_out), rows(D), rows(3 * D)]
                 + [rows(k) for k in widths] + [_resident(w) for _, w in branch] + [UNREAD],
        out_specs=[kept(sums[0]), kept(sums[1]), rows(3 * D)] + [rows(k) for k in widths]
                  + [kept(shape) for shape in sums[2:]],
        out_shape=[jax.ShapeDtypeStruct(sums[0], F32), jax.ShapeDtypeStruct(sums[1], F32),
                   jax.ShapeDtypeStruct((T, 3 * D), BF16)] + [jax.ShapeDtypeStruct((T, k), BF16) for k in widths]
                  + [jax.ShapeDtypeStruct(shape, F32) for shape in sums[2:]],
        compiler_params=pltpu.CompilerParams(dimension_semantics=("arbitrary",), vmem_limit_bytes=VMEM_LIMIT_LARGE),
    )(dh, f, g, w_out, merged, gt, o_a, o_b, o_m, w_a, w_b, w_m, after)


def mm_tn(x, dy, tm, tn, name, shard_major=False, perm=None, slabs=1, after=None, wire=False):
    T, M = x.shape
    N = dy.shape[1]
    tk = min(2048, T)
    perm = perm or (lambda j: j)
    w = tn // slabs

    def body(x_ref, dy_ref, *rest):
        o_ref = rest[-2] if wire else rest[-1]

        @pl.when(pl.program_id(2) == 0)
        def _():
            o_ref[...] = jnp.zeros_like(o_ref)

        acc = _dot_tn(x_ref[...], dy_ref[...])
        if shard_major:
            for s in range(slabs):
                o_ref[s] += acc[:, s * w:(s + 1) * w]
        else:
            o_ref[...] += acc
        if wire:
            @pl.when(pl.program_id(2) == T // tk - 1)
            def _():
                rest[-1][...] = o_ref[...].astype(BF16)

    if shard_major:
        out_spec = pl.BlockSpec((slabs, tm, w), lambda i, j, k: (perm(j), i, 0))
        out_shape = jax.ShapeDtypeStruct((N // w, M, w), F32)
    else:
        out_spec = pl.BlockSpec((tm, tn), lambda i, j, k: (i, j))
        out_shape = jax.ShapeDtypeStruct((M, N), F32)
    return pl.pallas_call(
        body, name=name, grid=(M // tm, N // tn, T // tk),
        in_specs=[pl.BlockSpec((tk, tm), lambda i, j, k: (k, i)),
                  pl.BlockSpec((tk, tn), lambda i, j, k: (k, j))] + ([] if after is None else [UNREAD]),
        out_specs=[out_spec, out_spec] if wire else out_spec,
        out_shape=[out_shape, jax.ShapeDtypeStruct(out_shape.shape, BF16)] if wire else out_shape,
        compiler_params=_params("parallel", "parallel", "arbitrary"),
    )(x, dy, *([] if after is None else [after]))


def rope_tables(T, zero):
    half = HEAD // 2
    inv = ROPE_THETA ** (-jnp.arange(half, dtype=F32) / half)
    ang = (jnp.arange(T).astype(F32) + zero)[:, None] * inv[None, :]
    cos, sin = jnp.cos(ang), jnp.sin(ang)
    return jnp.concatenate([cos, cos], axis=1), jnp.concatenate([-sin, sin], axis=1)


def layer_step(x, mem, target, gains, sinks, b_gate, weights_of, send_grads, zero):
    T = x.shape[0]
    cos, sin_signed = rope_tables(T, zero)
    no_sink = jnp.full((2,), NEG_INF, F32)

    xn1 = rms_scale(x, gains["ffn1_norm_pre"], "ffn1_norm", cos)
    w = dict(weights_of("ffn1_in", xn1))
    _, gu1, a1 = ffn_in(x, gains["ffn1_norm_pre"], w["ffn1_w_in"], "ffn1_in_a", xn=xn1, half=0)
    w.update(weights_of("ffn1_out", a1))
    _, gu1, a1 = ffn_in(x, gains["ffn1_norm_pre"], w["ffn1_w_in"], "ffn1_in_b", xn=xn1, half=1, into=(gu1, a1))
    f1, h1 = mm_norm_res(a1, w["ffn1_w_out"], x, gains["ffn1_norm_post"], 0.5, "ffn1_out")
    w.update(weights_of("mix_in", f1))
    u, qkv, gt = mix_in(h1, gains["mix_norm_pre"], w["w_in"], w["w_gate"], b_gate, cos, sin_signed, "mix_in")
    w.update(weights_of("mix_rest", u))
    outs, lses = [], []
    for gidx, (window, dil) in enumerate(DIL):
        last = gidx == len(DIL) - 1
        o_g, l_g = band_fwd(qkv, no_sink, r=dil, base=A_BASE + 6 * gidx, hkv=2, grp=1, max_dist=window // dil,
                            out_dtype=BF16 if last else F32, name=f"attn_a{gidx}_fwd",
                            merge=(outs, lses) if last else None)
        outs.append(o_g)
        lses.append(l_g)
    o_a, l_a = outs[-1], lses[-1]
    o_b, l_b = band_fwd(qkv, sinks, r=1, base=B_BASE, hkv=2, grp=2, max_dist=HEAD - 1, out_dtype=BF16,
                        name="attn_b_fwd")
    mem_n, mkv = mem_kv(mem, gains["mem_norm"], w["w_mem_kv"], "mem_kv")
    o_m, l_m = mem_fwd(qkv, mkv, "attn_m_fwd")
    merged, mo, h2 = gate_merge_out(gt, o_a, o_b, o_m, w["w_o_a"], w["w_o_b"], w["w_o_m"], w["w_out"], h1,
                                    gains["mix_norm_post"], "gate_merge_out")
    w.update(weights_of("ffn2", mo))
    xn2, gu2, a2 = ffn_in(h2, gains["ffn2_norm_pre"], w["ffn2_w_in"], "ffn2_in")
    f2, dy, sq = mm_norm_res(a2, w["ffn2_w_out"], h2, gains["ffn2_norm_post"], 0.5, "ffn2_out", target=target)

    grads = {}

    def ffn_bwd(tag, dh_out, f, gu, a, xn, h_in, after):
        df, dgu, dh_in, grads[f"{tag}_norm_pre"], grads[f"{tag}_norm_post"] = ffn_tokens_bwd(
            dh_out, f, h_in, gu, gains[f"{tag}_norm_pre"], gains[f"{tag}_norm_post"], w[f"{tag}_w_in"],
            w[f"{tag}_w_out"], 0.5, f"{tag}_tokens_bwd", after)
        sent = send_grads(f"{tag}_in", {f"{tag}_w_in": mm_tn(
            xn, dgu, D_MODEL, FF_T, f"{tag}_w_in_grad", shard_major=True, perm=_ffn_perm, wire=True)})
        sent = send_grads(f"{tag}_out", {f"{tag}_w_out": mm_tn(
            a, df, FF_T, D_MODEL, f"{tag}_w_out_grad", after=sent, wire=True)})
        return dh_in, sent

    dh2, sent = ffn_bwd("ffn2", dy, f2, gu2, a2, xn2, h2, dy)

    mix = {}
    (grads["mix_norm_post"], mix["w_out"], dgt, do_a, do_b, do_m, grads["b_gate"],
     dwa_t, dwb_t, dwm_t) = gate_merge_out_bwd(
        dh2, mo, gains["mix_norm_post"], w["w_out"], merged, gt, o_a, o_b, o_m, w["w_o_a"], w["w_o_b"],
        w["w_o_m"], "gate_merge_out_bwd", sent)
    mix["w_o_a"], mix["w_o_b"], mix["w_o_m"] = dwa_t.T, dwb_t.T, dwm_t.T

    dqkv = lax.empty(qkv.shape, qkv.dtype)
    for gidx, (window, dil) in enumerate(DIL):
        dqkv, = band_bwd(qkv, dqkv, do_a, o_a, l_a, cos, sin_signed, None, r=dil, base=A_BASE + 6 * gidx, hkv=2,
                         grp=1, max_dist=window // dil, name=f"attn_a{gidx}_bwd")
    dqkv, dsink = band_bwd(qkv, dqkv, do_b, o_b, l_b, cos, sin_signed, sinks, r=1, base=B_BASE, hkv=2, grp=2,
                           max_dist=HEAD - 1, name="attn_b_bwd")
    grads["sinks"] = -dsink[:, ::8, 0].reshape(1, 4)
    dqkv, dmk, dmv = mem_bwd(qkv, dqkv, mkv, do_m, o_m, l_m, "attn_m_bwd")
    mix["w_mem_kv"], grads["mem_norm"] = mem_kv_bwd(
        mem, gains["mem_norm"], mem_n, w["w_mem_kv"], jnp.concatenate([dmk, dmv], axis=1), "mem_kv_bwd")

    mix["w_in"] = mm_tn(u, dqkv, D_MODEL, 1280, "w_in_grad")
    mix["w_gate"] = mm_tn(u, dgt, D_MODEL, 1536, "w_gate_grad", shard_major=True, slabs=2, wire=True)
    sent = send_grads("mix", mix)
    dh1, grads["mix_norm_pre"] = mm_nt_norm_bwd(
        [(dqkv, w["w_in"]), (dgt, w["w_gate"])], h1, dh2, gains["mix_norm_pre"], "mix_in_bwd", sent)

    dx, _ = ffn_bwd("ffn1", dh1, f1, gu1, a1, xn1, x, dh1)
    return sq, dx, grads


def _place():
    return lax.axis_index("x"), lax.axis_index("y"), lax.axis_index("c")


def _other_chips(x, y):
    return [(1 - x, y), (x, 1 - y), (1 - x, 1 - y)]


def _hbm(n):
    return [pl.BlockSpec(memory_space=pltpu.HBM)] * n


SEM = pl.BlockSpec(memory_space=pltpu.SEMAPHORE)
SIDE_EFFECT = pltpu.SideEffectType.DATAFLOW_SIDE_EFFECTING


def _chip_copy(src, land, sems, i, j, dst_slot, scatter):
    x, y, c = _place()
    px, py = _other_chips(x, y)[j]
    send_sems, recv_sems = sems
    return pltpu.make_async_remote_copy(
        src_ref=src[i].at[2 * px + py] if scatter else src[i], dst_ref=land[i].at[dst_slot],
        send_sem=send_sems.at[3 * i + j], recv_sem=recv_sems.at[3 * i + j],
        device_id=(px, py, c), device_id_type=MESH)


def chip_copies_start(srcs, lands, groups, scatter, name, after=None):
    n = len(srcs)

    def body(*refs):
        src, land = refs[:n], refs[n:2 * n]
        first_sem = 2 * n + (after is not None)
        sems = refs[first_sem:first_sem + 2 * len(groups)]
        token = refs[-1]
        x, y, _ = _place()
        for g, members in enumerate(groups):
            part = ([src[i] for i in members], [land[i] for i in members])
            for t in range(len(members)):
                for j in range(3):
                    _chip_copy(*part, sems[2 * g:2 * g + 2], t, j, 2 * x + y, scatter).start()
        token[...] = jnp.zeros_like(token)

    sem_shapes = [pltpu.SemaphoreType.DMA((3 * len(m),)) for m in groups for _ in range(2)]
    thru = [pltpu.HBM(a.shape, a.dtype) for a in (*srcs, *lands)]
    res = pl.pallas_call(
        body, name=name,
        out_shape=(*sem_shapes, *thru, jax.ShapeDtypeStruct((8, 128), F32)),
        in_specs=_hbm(2 * n) + ([] if after is None else [UNREAD]),
        out_specs=(*[SEM] * len(sem_shapes), *_hbm(2 * n), pl.BlockSpec(memory_space=pltpu.VMEM)),
        input_output_aliases={i: len(sem_shapes) + i for i in range(2 * n)},
        compiler_params=pltpu.CompilerParams(has_side_effects=SIDE_EFFECT),
    )(*[pltpu.with_memory_space_constraint(a, pltpu.HBM) for a in (*srcs, *lands)],
      *([] if after is None else [after]))
    k = len(sem_shapes)
    sems = [tuple(res[2 * g:2 * g + 2]) for g in range(len(groups))]
    return sems, list(res[k:k + n]), list(res[k + n:k + 2 * n]), res[-1]


def chip_copies_wait(srcs, lands, sems, after, scatter, name):
    n = len(srcs)
    after = list(after) if isinstance(after, (list, tuple)) else [after]

    def body(*refs):
        src, land = refs[:n], refs[n:2 * n]
        pair = refs[2 * n:2 * n + 2]
        x, y, _ = _place()
        for i in range(n):
            for j, (px, py) in enumerate(_other_chips(x, y)):
                copy = _chip_copy(src, land, pair, i, j, 2 * px + py, scatter)
                copy.wait_send()
                copy.wait_recv()

    res = pl.pallas_call(
        body, name=name,
        out_shape=[pltpu.HBM(a.shape, a.dtype) for a in (*srcs, *lands)],
        in_specs=[*_hbm(2 * n), SEM, SEM] + [UNREAD] * len(after),
        out_specs=_hbm(2 * n),
        input_output_aliases={i: i for i in range(2 * n)},
        compiler_params=pltpu.CompilerParams(has_side_effects=SIDE_EFFECT),
    )(*srcs, *lands, *sems, *after)
    return list(res[n:])


def small_all_gather(small, name):
    flips = [(fx, fy, fc) for fx in (0, 1) for fy in (0, 1) for fc in (0, 1)][1:]

    def body(in_ref, out_ref, send_sems, recv_sems, local_sem):
        x, y, c = _place()
        me = 4 * x + 2 * y + c

        def copy(k, slot):
            fx, fy, fc = flips[k]
            return pltpu.make_async_remote_copy(
                src_ref=in_ref, dst_ref=out_ref.at[slot], send_sem=send_sems.at[k], recv_sem=recv_sems.at[k],
                device_id=(x ^ fx, y ^ fy, c ^ fc), device_id_type=MESH)

        local = pltpu.make_async_copy(in_ref, out_ref.at[me], local_sem)
        local.start()
        for k in range(len(flips)):
            copy(k, me).start()
        for k, (fx, fy, fc) in enumerate(flips):
            copy(k, 4 * (x ^ fx) + 2 * (y ^ fy) + (c ^ fc)).wait()
        local.wait()

    return pl.pallas_call(
        body, name=name, in_specs=_hbm(1), out_specs=_hbm(1)[0],
        out_shape=jax.ShapeDtypeStruct((N_DEV,) + small.shape, small.dtype),
        scratch_shapes=[pltpu.SemaphoreType.DMA((len(flips),)), pltpu.SemaphoreType.DMA((len(flips),)),
                        pltpu.SemaphoreType.DMA],
    )(small)


def _sibling_copy(src, land, sems, i):
    x, y, c = _place()
    return pltpu.make_async_remote_copy(
        src_ref=src[i], dst_ref=land[i], send_sem=sems[0].at[i], recv_sem=sems[1].at[i],
        device_id=(x, y, 1 - c), device_id_type=MESH)


def sibling_copies_start(parts, name):
    n = len(parts)
    lands = [lax.empty(p.shape, p.dtype) for p in parts]

    def body(*refs):
        src, land, sems, token = refs[:n], refs[n:2 * n], refs[2 * n:2 * n + 2], refs[-1]
        for i in range(n):
            _sibling_copy(src, land, sems, i).start()
        token[...] = jnp.zeros_like(token)

    res = pl.pallas_call(
        body, name=name,
        out_shape=(pltpu.SemaphoreType.DMA((n,)), pltpu.SemaphoreType.DMA((n,)),
                   *[pltpu.HBM(a.shape, a.dtype) for a in (*parts, *lands)], jax.ShapeDtypeStruct((8, 128), F32)),
        in_specs=_hbm(2 * n),
        out_specs=(SEM, SEM, *_hbm(2 * n), pl.BlockSpec(memory_space=pltpu.VMEM)),
        input_output_aliases={i: 2 + i for i in range(2 * n)},
        compiler_params=pltpu.CompilerParams(has_side_effects=SIDE_EFFECT),
    )(*[pltpu.with_memory_space_constraint(a, pltpu.HBM) for a in (*parts, *lands)])
    return tuple(res[:2]), list(res[2:2 + n]), list(res[2 + n:2 + 2 * n]), res[-1]


def sibling_copies_wait(parts, lands, sems, after, name):
    n = len(parts)

    def body(*refs):
        src, land, sems = refs[:n], refs[n:2 * n], refs[2 * n:2 * n + 2]
        for i in range(n):
            copy = _sibling_copy(src, land, sems, i)
            copy.wait_send()
            copy.wait_recv()

    res = pl.pallas_call(
        body, name=name,
        out_shape=[pltpu.HBM(a.shape, a.dtype) for a in (*parts, *lands)],
        in_specs=[*_hbm(2 * n), SEM, SEM, UNREAD],
        out_specs=_hbm(2 * n),
        input_output_aliases={i: i for i in range(2 * n)},
        compiler_params=pltpu.CompilerParams(has_side_effects=SIDE_EFFECT),
    )(*parts, *lands, *sems, after)
    return list(res[n:])


def _row_tile(rows):
    for t in (256, 176, 128, 64, 32, 16, 8):
        if rows % t == 0:
            return t
    return rows


def chip_partial_sum(me, own_sm, recv, name):
    _, rows, cols = own_sm.shape
    tr = _row_tile(rows)

    def body(me_ref, own_ref, r1, r2, r3, o_ref):
        o_ref[...] = own_ref[...] + r1[...].astype(F32) + r2[...].astype(F32) + r3[...].astype(F32)

    def slot(d):
        return pl.BlockSpec((None, tr, cols), lambda i, me_ref: ((me_ref[0] + d) % N_CHIPS, i, 0))

    return pl.pallas_call(
        body, name=name,
        grid_spec=pltpu.PrefetchScalarGridSpec(
            num_scalar_prefetch=1, grid=(rows // tr,),
            in_specs=[slot(0), slot(1), slot(2), slot(3)],
            out_specs=pl.BlockSpec((tr, cols), lambda i, me_ref: (i, 0))),
        out_shape=jax.ShapeDtypeStruct((rows, cols), F32),
        compiler_params=_params("parallel"),
    )(me, own_sm, recv, recv, recv)


def _adamw(w, g, m, v):
    m = ADAM_B1 * m + (1.0 - ADAM_B1) * g
    v = ADAM_B2 * v + (1.0 - ADAM_B2) * (g * g)
    m_hat = m / (1.0 - ADAM_B1 ** ADAM_STEP)
    v_hat = v / (1.0 - ADAM_B2 ** ADAM_STEP)
    delta = -ADAM_LR * (m_hat / (jnp.sqrt(v_hat) + ADAM_EPS) + ADAM_WD * w)
    return delta, m, v


def adamw_pair(part, sib, w, m, v, name):
    rows, cols = w.shape
    tr = _row_tile(rows)

    def body(p_ref, s_ref, w_ref, m_ref, v_ref, g_ref, d_ref, nm_ref, nv_ref):
        g = p_ref[...] + s_ref[...]
        g_ref[...] = g
        d_ref[...], nm_ref[...], nv_ref[...] = _adamw(w_ref[...], g, m_ref[...], v_ref[...])

    spec = pl.BlockSpec((tr, cols), lambda i: (i, 0))
    return pl.pallas_call(
        body, name=name, grid=(rows // tr,), in_specs=[spec] * 5, out_specs=[spec] * 4,
        out_shape=[jax.ShapeDtypeStruct((rows, cols), F32)] * 4,
        compiler_params=_params("parallel"),
    )(part, sib, w, m, v)


def adamw_small(g_all, w, m, v, name):
    def body(ga_ref, w_ref, m_ref, v_ref, g_ref, d_ref, nm_ref, nv_ref):
        g = ga_ref[0]
        for k in range(1, N_DEV):
            g = g + ga_ref[k]
        g_ref[...] = g
        d_ref[...], nm_ref[...], nv_ref[...] = _adamw(w_ref[...], g, m_ref[...], v_ref[...])

    return pl.pallas_call(
        body, name=name, out_shape=[jax.ShapeDtypeStruct(w.shape, F32)] * 4,
    )(g_all, w, m, v)


WEIGHTS = ("ffn1_norm_pre", "ffn1_w_in", "ffn1_w_out", "ffn1_norm_post", "mix_norm_pre", "w_in", "sinks",
           "mem_norm", "w_mem_kv", "w_gate", "b_gate", "w_o_a", "w_o_b", "w_o_m", "w_out", "mix_norm_post",
           "ffn2_norm_pre", "ffn2_w_in", "ffn2_w_out", "ffn2_norm_post")
GATHER_STAGES = (("ffn1_in", "ffn1_out"), ("mix_in",), ("mix_rest", "ffn2"))
GATHER_GROUPS = {"ffn1_in": ("ffn1_w_in",), "ffn1_out": ("ffn1_w_out",),
                 "mix_in": ("w_in", "w_gate"), "mix_rest": ("w_mem_kv", "w_o_a", "w_o_b", "w_o_m", "w_out"),
                 "ffn2": ("ffn2_w_in", "ffn2_w_out")}
GROUPS = {"ffn1_in": ("ffn1_w_in",), "ffn1_out": ("ffn1_w_out",),
          "mix": ("w_in", "w_gate", "w_mem_kv", "w_o_a", "w_o_b", "w_o_m", "w_out"),
          "ffn2_in": ("ffn2_w_in",), "ffn2_out": ("ffn2_w_out",)}
COLUMN_SHARDED = ("ffn1_w_in", "ffn2_w_in", "w_in", "w_gate", "w_o_a", "w_o_b", "w_o_m")
KEPT_SHARD_MAJOR = ("ffn1_w_in", "ffn2_w_in", "w_gate")
GAINS = ("ffn1_norm_pre", "ffn1_norm_post", "mix_norm_pre", "mem_norm", "mix_norm_post", "ffn2_norm_pre",
         "ffn2_norm_post")
SMALL_ROWS = 16


def _pack_small(t):
    sinks = jnp.pad(t["sinks"], ((0, 0), (0, D_MODEL - t["sinks"].shape[1])))
    rows = [t[k] for k in GAINS] + [t["b_gate"].reshape(3, D_MODEL), sinks]
    packed = jnp.concatenate(rows, axis=0)
    return jnp.pad(packed, ((0, SMALL_ROWS - packed.shape[0]), (0, 0)))


def _unpack_small(p):
    out = {k: p[i:i + 1] for i, k in enumerate(GAINS)}
    out["b_gate"] = p[7:10].reshape(1, 3 * D_MODEL)
    out["sinks"] = p[10:11, :4]
    return out


def kernel(x, mem, ffn1_norm_pre, ffn1_w_in, ffn1_w_out, ffn1_norm_post, mix_norm_pre, w_in, sinks, mem_norm, w_mem_kv, w_gate, b_gate, w_o_a, w_o_b, w_o_m, w_out, mix_norm_post, ffn2_norm_pre, ffn2_w_in, ffn2_w_out, ffn2_norm_post, loss_target, m_ffn1_norm_pre, m_ffn1_w_in, m_ffn1_w_out, m_ffn1_norm_post, m_mix_norm_pre, m_w_in, m_sinks, m_mem_norm, m_w_mem_kv, m_w_gate, m_b_gate, m_w_o_a, m_w_o_b, m_w_o_m, m_w_out, m_mix_norm_post, m_ffn2_norm_pre, m_ffn2_w_in, m_ffn2_w_out, m_ffn2_norm_post, v_ffn1_norm_pre, v_ffn1_w_in, v_ffn1_w_out, v_ffn1_norm_post, v_mix_norm_pre, v_w_in, v_sinks, v_mem_norm, v_w_mem_kv, v_w_gate, v_b_gate, v_w_o_a, v_w_o_b, v_w_o_m, v_w_out, v_mix_norm_post, v_ffn2_norm_pre, v_ffn2_w_in, v_ffn2_w_out, v_ffn2_norm_post):
    given = dict(locals())
    wt = {k: given[k] for k in WEIGHTS}
    mom = {k: given["m_" + k] for k in WEIGHTS}
    var = {k: given["v_" + k] for k in WEIGHTS}
    chip = (2 * lax.axis_index("x") + lax.axis_index("y")).astype(jnp.int32)
    me = chip.reshape(1)

    def landing_zone(own):
        return lax.dynamic_update_slice_in_dim(lax.empty((N_CHIPS,) + own.shape, own.dtype), own[None], chip, 0)

    started = {}
    tokens = []

    def stage_keys(stage):
        return [k for g in GATHER_STAGES[stage] for k in GATHER_GROUPS[g]]

    def prepare(stage):
        shards = [(wt[k][0] + tokens[0][0, 0] if tokens else wt[k][0]).astype(BF16) for k in stage_keys(stage)]
        return shards, [landing_zone(s) for s in shards]

    def start_gather(stage, after):
        groups, keys = GATHER_STAGES[stage], stage_keys(stage)
        members = [[keys.index(k) for k in GATHER_GROUPS[g]] for g in groups]
        sems, shards, lands, token = chip_copies_start(
            *prepared[stage], members, False, f"weight_gather_start_{stage}", after)
        tokens.append(token)
        for g, idx, pair in zip(groups, members, sems):
            started[g] = ([shards[i] for i in idx], [lands[i] for i in idx], pair)

    prepared = {0: prepare(0)}
    start_gather(0, None)
    prepared.update({stage: prepare(stage) for stage in range(1, len(GATHER_STAGES))})

    def weights_of(group, after):
        if group == GATHER_STAGES[0][0]:
            after = [after] + [a for stage in range(1, len(GATHER_STAGES)) for part in prepared[stage] for a in part]
        got = chip_copies_wait(*started[group], after, False, f"weight_gather_wait_{group}")
        stage = [s + 1 for s, groups in enumerate(GATHER_STAGES[:-1]) if groups[-1] == group]
        if stage:
            start_gather(stage[0], got[0])
        full = {}
        for k, g in zip(GATHER_GROUPS[group], got):
            if k in COLUMN_SHARDED:
                if k in ("ffn1_w_in", "ffn2_w_in"):
                    g = jnp.stack([g[0], g[2], g[1], g[3]])
                full[k] = jnp.swapaxes(g, 0, 1).reshape(g.shape[1], N_CHIPS * g.shape[2])
                if k == "w_in":
                    full[k] = to_kernel_heads(full[k])
            else:
                full[k] = g.reshape(N_CHIPS * g.shape[1], g.shape[2])
        return full

    in_flight = {}

    def send_grads(group, grads):
        def shard_major(k, g):
            if k in KEPT_SHARD_MAJOR:
                return g
            if k in COLUMN_SHARDED:
                return jnp.swapaxes(g.reshape(g.shape[0], N_CHIPS, g.shape[1] // N_CHIPS), 0, 1)
            return g.reshape(N_CHIPS, g.shape[0] // N_CHIPS, g.shape[1])

        own, wire = [], []
        for k in GROUPS[group]:
            g, rounded = grads[k] if isinstance(grads[k], (tuple, list)) else (grads[k], None)
            g = shard_major(k, from_kernel_heads(g) if k == "w_in" else g)
            own.append(g)
            wire.append(g.astype(BF16) if rounded is None else shard_major(k, rounded))
        zones = [lax.empty(b.shape, b.dtype) for b in wire]
        pair, wire, zones, sent = chip_copies_start(
            wire, zones, [list(range(len(wire)))], True, f"grad_scatter_start_{group}")
        in_flight[group] = (own, wire, zones, pair[0], sent)
        return sent

    gains = {k: wt[k] for k in GAINS}
    sq, dx, grads = layer_step(
        x[0], mem[0], loss_target[0], gains, sinks[0], b_gate, weights_of, send_grads, tokens[0][0, 0])
    loss = lax.psum(0.5 * sq[0, 0] / D_MODEL, ("x", "y", "c"))

    res = {}
    after = in_flight["ffn1_out"][4]
    swaps = []
    for stage in (("ffn2_in", "ffn2_out", "mix", "ffn1_in"), ("ffn1_out",)):
        names, parts = [], []
        for group in stage:
            own, wire, zones, pair, _ = in_flight[group]
            received = chip_copies_wait(wire, zones, pair, after, True, f"grad_scatter_wait_{group}")
            for k, g, r in zip(GROUPS[group], own, received):
                names.append(k)
                parts.append(chip_partial_sum(me, g, r, f"{k}_chip_sum"))
        pair, parts, lands, after = sibling_copies_start(parts, f"sibling_start_{stage[-1]}")
        swaps.append((stage[-1], names, parts, lands, pair))
    small_all = small_all_gather(_pack_small(grads), "small_grad_gather")
    packed = adamw_small(small_all, _pack_small(wt), _pack_small(mom), _pack_small(var), "small_adamw")
    after = packed[0]
    for tag, names, parts, lands, pair in swaps:
        sibs = sibling_copies_wait(parts, lands, pair, after, f"sibling_wait_{tag}")
        for k, p, s in zip(names, parts, sibs):
            res[k] = [t[None] for t in adamw_pair(p, s, wt[k][0], mom[k][0], var[k][0], f"{k}_adamw")]
        after = res[names[-1]][0]
    for idx, p in enumerate(packed):
        for k, t in _unpack_small(p).items():
            res.setdefault(k, [None] * 4)[idx] = t

    return (loss, dx[None], *[res[k][0] for k in WEIGHTS], *[res[k][1] for k in WEIGHTS],
            *[res[k][2] for k in WEIGHTS], *[res[k][3] for k in WEIGHTS])
```

```python
import functools

import jax
import jax.numpy as jnp
from jax import lax
from jax.experimental import pallas as pl
from jax.experimental.pallas import tpu as pltpu

F32 = jnp.float32
BF16 = jnp.bfloat16

D_MODEL = 1024
D_FF = 2816
HEAD = 128
N_CHIPS = 4
N_DEV = 8
EPS = 1e-6
NEG_INF = -1e30
ROPE_THETA = 10000.0
ATT_SCALE = HEAD ** -0.5

ADAM_LR = 0.001
ADAM_B1 = 0.9
ADAM_B2 = 0.999
ADAM_EPS = 1e-08
ADAM_WD = 0.01
ADAM_STEP = 10

VMEM_LIMIT = 52 * 2 ** 20
VMEM_LIMIT_LARGE = 60 * 2 ** 20
MESH = pl.DeviceIdType.MESH

QKV_W = 3840
DIL = ((128, 1), (512, 4), (2048, 16))
B_BASE, MQ, A_BASE = 0, 8, 12
_AQ, _AK, _AV, _BQ, _BK, _BV, _MQ = 0, 6, 12, 18, 22, 24, 26
HEAD_ORDER = tuple(
    [h for j in range(2) for h in (_BQ + 2 * j, _BQ + 2 * j + 1, _BK + j, _BV + j)]
    + [_MQ + i for i in range(4)]
    + [h for g in range(3) for i in range(2) for h in (_AQ + 2 * g + i, _AK + 2 * g + i, _AV + 2 * g + i)])
ROTARY_HEADS = tuple(p for p, h in enumerate(HEAD_ORDER) if h < _AV or _BQ <= h < _BV)


def to_kernel_heads(w):
    return jnp.concatenate([w[..., h * HEAD:(h + 1) * HEAD] for h in HEAD_ORDER], axis=-1)


def from_kernel_heads(w):
    place = {h: p for p, h in enumerate(HEAD_ORDER)}
    return jnp.concatenate([w[..., place[h] * HEAD:(place[h] + 1) * HEAD] for h in range(len(HEAD_ORDER))], axis=-1)

TM = 512
FF_T = D_FF // 2


def _params(*sem):
    return pltpu.CompilerParams(dimension_semantics=sem, vmem_limit_bytes=VMEM_LIMIT)


def _dot(a, b):
    return jnp.dot(a, b, preferred_element_type=F32)


def _dot_nt(a, b):
    return lax.dot_general(a, b, (((1,), (1,)), ((), ())), preferred_element_type=F32)


def _dot_tn(a, b):
    return lax.dot_general(a, b, (((0,), (0,)), ((), ())), preferred_element_type=F32)


def _rstd(x):
    return lax.rsqrt(jnp.mean(x * x, axis=-1, keepdims=True) + EPS)


def _sigmoid(x):
    return 0.5 * jnp.tanh(0.5 * x) + 0.5


def _ffn_perm(k):
    return (k % 2) * 2 + k // 2


UNREAD = pl.BlockSpec(memory_space=pl.ANY)


def _resident(arr):
    return pl.BlockSpec(arr.shape, lambda *_: (0,) * arr.ndim, pipeline_mode=pl.Buffered(1))


def rms_scale(x, g, name, after):
    T, D = x.shape
    tm = 1024

    def body(x_ref, g_ref, _, o_ref):
        v = x_ref[...]
        o_ref[...] = (v * _rstd(v) * g_ref[...]).astype(BF16)

    spec = pl.BlockSpec((tm, D), lambda i: (i, 0))
    return pl.pallas_call(
        body, name=name, grid=(T // tm,), in_specs=[spec, _resident(g), UNREAD], out_specs=spec,
        out_shape=jax.ShapeDtypeStruct((T, D), BF16), compiler_params=_params("parallel"),
    )(x, g, after)


def ffn_in(h, g, w, name, xn=None):
    T, D = h.shape
    normed = xn is not None

    def body(h_ref, g_ref, w_ref, *outs):
        if normed:
            xn, (gu_ref, a_ref) = h_ref[...], outs
        else:
            xn_ref, gu_ref, a_ref = outs
            x = h_ref[...]
            xn = (x * _rstd(x) * g_ref[...]).astype(BF16)
            xn_ref[...] = xn
        for j in range(2):
            gu = _dot(xn, w_ref[:, j * 2 * FF_T:(j + 1) * 2 * FF_T])
            gu_ref[:, j * 2 * FF_T:(j + 1) * 2 * FF_T] = gu.astype(BF16)
            gate, up = gu[:, :FF_T], gu[:, FF_T:]
            a_ref[:, j * FF_T:(j + 1) * FF_T] = (gate * _sigmoid(gate) * up).astype(BF16)

    def rows(width):
        return pl.BlockSpec((TM, width), lambda i: (i, 0))

    res = pl.pallas_call(
        body, name=name,
        grid=(T // TM,),
        in_specs=[rows(D), _resident(g), _resident(w)],
        out_specs=[rows(D)] * (not normed) + [rows(2 * D_FF), rows(D_FF)],
        out_shape=[jax.ShapeDtypeStruct((T, D), BF16)] * (not normed)
                  + [jax.ShapeDtypeStruct((T, 2 * D_FF), BF16), jax.ShapeDtypeStruct((T, D_FF), BF16)],
        compiler_params=_params("parallel"),
    )(xn if normed else h, g, w)
    return (xn, *res) if normed else tuple(res)


def mm_norm_res(a, w, h_in, g, coef, name, target=None):
    T, K = a.shape
    D = w.shape[1]
    final = target is not None

    def body(*refs):
        if final:
            a_ref, w_ref, h_ref, g_ref, t_ref, f_ref, o_ref, l_ref = refs
        else:
            a_ref, w_ref, h_ref, g_ref, f_ref, o_ref = refs
        f = _dot(a_ref[...], w_ref[...])
        f_ref[...] = f
        y = h_ref[...] + coef * (f * _rstd(f) * g_ref[...])
        if final:
            err = y - t_ref[...]
            o_ref[...] = err * (1.0 / D)

            @pl.when(pl.program_id(0) == 0)
            def _():
                l_ref[...] = jnp.zeros_like(l_ref)

            l_ref[...] += jnp.sum(err * err)
        else:
            o_ref[...] = y

    row = pl.BlockSpec((TM, D), lambda i: (i, 0))
    in_specs = [pl.BlockSpec((TM, K), lambda i: (i, 0)),
                _resident(w),
                row, pl.BlockSpec((1, D), lambda i: (0, 0))]
    out_specs = [row, row]
    out_shape = [jax.ShapeDtypeStruct((T, D), F32), jax.ShapeDtypeStruct((T, D), F32)]
    args = [a, w, h_in, g]
    if final:
        in_specs.append(row)
        args.append(target)
        out_specs.append(pl.BlockSpec((8, 128), lambda i: (0, 0)))
        out_shape.append(jax.ShapeDtypeStruct((8, 128), F32))
    return pl.pallas_call(
        body, name=name, grid=(T // TM,), in_specs=in_specs, out_specs=out_specs, out_shape=out_shape,
        compiler_params=_params("arbitrary"),
    )(*args)


def _rope(x, cos, sin_signed):
    return x * cos + pltpu.roll(x, HEAD // 2, axis=1) * sin_signed


def _unrope(x, cos, sin_signed):
    return x * cos - pltpu.roll(x, HEAD // 2, axis=1) * sin_signed


def mix_in(h, g, w, w_gate, b_gate, cos, sin_signed, name):
    T, D = h.shape
    tn = 768

    def body(h_ref, g_ref, w_ref, wg_ref, b_ref, c_ref, s_ref, u_ref, o_ref, gt_ref):
        x = h_ref[...]
        u = (x * _rstd(x) * g_ref[...]).astype(BF16)
        u_ref[...] = u
        c, s = c_ref[...], s_ref[...]
        for j in range(QKV_W // tn):
            acc = _dot(u, w_ref[:, j * tn:(j + 1) * tn])
            for hd in range(tn // HEAD):
                head = j * (tn // HEAD) + hd
                part = acc[:, hd * HEAD:(hd + 1) * HEAD]
                if head in ROTARY_HEADS:
                    part = _rope(part, c, s)
                o_ref[:, head * HEAD:(head + 1) * HEAD] = part.astype(BF16)
        for j in range(w_gate.shape[1] // tn):
            cols = slice(j * tn, (j + 1) * tn)
            gt_ref[:, cols] = _sigmoid(_dot(u, wg_ref[:, cols]) + b_ref[:, cols]).astype(BF16)

    def rows(width):
        return pl.BlockSpec((TM, width), lambda i: (i, 0))

    return pl.pallas_call(
        body, name=name,
        grid=(T // TM,),
        in_specs=[rows(D), _resident(g), _resident(w), _resident(w_gate), _resident(b_gate), rows(HEAD), rows(HEAD)],
        out_specs=[rows(D), rows(QKV_W), rows(w_gate.shape[1])],
        out_shape=[jax.ShapeDtypeStruct((T, D), BF16), jax.ShapeDtypeStruct((T, QKV_W), BF16),
                   jax.ShapeDtypeStruct((T, w_gate.shape[1]), BF16)],
        compiler_params=_params("parallel"),
    )(h, g, w, w_gate, b_gate, cos, sin_signed)


def gate_merge_out(gt, o_a, o_b, o_m, w_a, w_b, w_m, w_out, h_in, g, name):
    T = gt.shape[0]
    D = D_MODEL

    def body(gt_ref, oa_ref, ob_ref, om_ref, wa_ref, wb_ref, wm_ref, wo_ref, h_ref, g_ref, m_ref, f_ref, o_ref):
        acc = gt_ref[:, :D].astype(F32) * _dot(oa_ref[...], wa_ref[...])
        acc += gt_ref[:, D:2 * D].astype(F32) * _dot(ob_ref[...], wb_ref[...])
        acc += gt_ref[:, 2 * D:].astype(F32) * _dot(om_ref[...], wm_ref[...])
        merged = acc.astype(BF16)
        m_ref[...] = merged
        f = _dot(merged, wo_ref[...])
        f_ref[...] = f
        o_ref[...] = h_ref[...] + f * _rstd(f) * g_ref[...]

    def rows(width):
        return pl.BlockSpec((TM, width), lambda i: (i, 0))

    return pl.pallas_call(
        body, name=name, grid=(T // TM,),
        in_specs=[rows(3 * D), rows(o_a.shape[1]), rows(o_b.shape[1]), rows(o_m.shape[1]),
                  _resident(w_a), _resident(w_b), _resident(w_m), _resident(w_out), rows(D), _resident(g)],
        out_specs=[rows(D), rows(D), rows(D)],
        out_shape=[jax.ShapeDtypeStruct((T, D), BF16), jax.ShapeDtypeStruct((T, D), F32),
                   jax.ShapeDtypeStruct((T, D), F32)],
        compiler_params=_params("parallel"),
    )(gt, o_a, o_b, o_m, w_a, w_b, w_m, w_out, h_in, g)


def _band_rows(start, r):
    return pl.ds(start, HEAD) if r == 1 else pl.ds(start, HEAD, stride=r)


def _band_mask(max_dist, first_has_prev):
    row = lax.broadcasted_iota(jnp.int32, (HEAD, 2 * HEAD), 0)
    col = lax.broadcasted_iota(jnp.int32, (HEAD, 2 * HEAD), 1)
    dist = row + HEAD - col
    band = (dist >= 0) & (dist <= max_dist)
    return band, band & (col >= jnp.where(first_has_prev, 0, HEAD))


def _stack(parts):
    return parts[0] if len(parts) == 1 else jnp.concatenate(parts, axis=0)


def _band_specs(BT, SB, nsub, base, grp):
    stride = grp + 2

    def cur(off, width):
        return pl.BlockSpec((BT, width * HEAD), lambda h, i: (i, (base + h * stride + off) // width))

    def prev(off):
        return pl.BlockSpec((SB, HEAD), lambda h, i: (jnp.maximum(i * nsub - 1, 0), base + h * stride + off))

    return cur(0, grp), cur(grp, 1), prev(grp), cur(grp + 1, 1), prev(grp + 1)


def band_fwd(qkv, sinks, *, r, base, hkv, grp, max_dist, out_dtype, name, merge=None):
    T, W = qkv.shape
    SB = HEAD * r
    BT = min(2048, T)
    nsub, nib = BT // SB, T // BT
    hq = hkv * grp
    heads = [slice(g * HEAD, (g + 1) * HEAD) for g in range(grp)]
    others = [] if merge is None else [*merge[0], *merge[1]]

    def body(sink_ref, q_ref, kc_ref, kp_ref, vc_ref, vp_ref, *rest):
        joint_o, joint_l = rest[len(others):len(others) + 2]
        qf, kf, vf = rest[len(others) + 2:len(others) + 5]
        o_ref, l_ref = rest[len(others) + 5:] if others else (joint_o, joint_l)
        kvh, ib = pl.program_id(0), pl.program_id(1)
        qf[...] = q_ref[...].astype(F32)
        kf[:SB] = kp_ref[...].astype(F32)
        kf[SB:] = kc_ref[...].astype(F32)
        vf[:SB] = vp_ref[...].astype(F32)
        vf[SB:] = vc_ref[...].astype(F32)
        band, band_first = _band_mask(max_dist, ib > 0)
        for c in range(r):
            k_old, v_old = kf[_band_rows(c, r)], vf[_band_rows(c, r)]
            for j in range(nsub):
                mask = band_first if j == 0 else band
                rows = _band_rows(j * SB + c, r)
                k_own, v_own = kf[_band_rows((j + 1) * SB + c, r)], vf[_band_rows((j + 1) * SB + c, r)]
                kcat = jnp.concatenate([k_old, k_own], axis=0).astype(BF16)
                vcat = jnp.concatenate([v_old, v_own], axis=0).astype(BF16)
                k_old, v_old = k_own, v_own
                s_all = _dot_nt(_stack([qf[rows, cols] for cols in heads]).astype(BF16), kcat) * ATT_SCALE
                probs, tots = [], []
                for g, cols in enumerate(heads):
                    s = jnp.where(mask, s_all[cols], NEG_INF)
                    sk = sink_ref[kvh * grp + g]
                    m = jnp.maximum(jnp.max(s, axis=-1, keepdims=True), sk)
                    p = jnp.exp(s - m)
                    tot = jnp.sum(p, axis=-1, keepdims=True) + jnp.exp(sk - m)
                    probs.append(p.astype(BF16))
                    tots.append(tot)
                    l_ref[rows, cols] = jnp.broadcast_to(m + jnp.log(tot), (HEAD, HEAD))
                o_all = _dot(_stack(probs), vcat)
                for g, cols in enumerate(heads):
                    o_ref[rows, cols] = (o_all[cols] / tots[g]).astype(o_ref.dtype)

        if others:
            half = len(others) // 2
            outs = [ref[...] for ref in rest[:half]] + [o_ref[...]]
            logs = [ref[...] for ref in rest[half:len(others)]] + [l_ref[...]]
            top = functools.reduce(jnp.maximum, logs)
            weights = [jnp.exp(lg - top) for lg in logs]
            total = functools.reduce(jnp.add, weights)
            mixed = functools.reduce(jnp.add, [wgt * out for wgt, out in zip(weights, outs)])
            joint_o[...] = (mixed / total).astype(out_dtype)
            joint_l[...] = top + jnp.log(total)

    out_spec = pl.BlockSpec((BT, grp * HEAD), lambda h, i: (i, h))
    own = [pltpu.VMEM((BT, grp * HEAD), F32)] * 2 if others else []
    return pl.pallas_call(
        body, name=name, grid=(hkv, nib),
        in_specs=[pl.BlockSpec(memory_space=pltpu.SMEM), *_band_specs(BT, SB, nsub, base, grp)]
                 + [out_spec] * len(others),
        out_specs=[out_spec, out_spec],
        out_shape=[jax.ShapeDtypeStruct((T, hq * HEAD), out_dtype), jax.ShapeDtypeStruct((T, hq * HEAD), F32)],
        scratch_shapes=[pltpu.VMEM((BT, grp * HEAD), F32), pltpu.VMEM((SB + BT, HEAD), F32),
                        pltpu.VMEM((SB + BT, HEAD), F32)] + own,
        compiler_params=_params("parallel", "arbitrary"),
    )(sinks, qkv, qkv, qkv, qkv, qkv, *others)


def band_bwd(qkv, dqkv, do, o, lse, cos, sin_signed, sinks, *, r, base, hkv, grp, max_dist, name):
    T, W = qkv.shape
    SB = HEAD * r
    BT = min(max(2048, 2 * SB), T)
    nsub, nib = BT // SB, T // BT
    nblk = T // SB
    with_sink = sinks is not None
    heads = [slice(g * HEAD, (g + 1) * HEAD) for g in range(grp)]

    def body(*refs):
        if with_sink:
            sink_ref, refs = refs[0], refs[1:]
        (q_ref, kc_ref, kp_ref, vc_ref, vp_ref, qn_ref, do_ref, don_ref, o_ref, on_ref, l_ref, ln_ref,
         c_ref, s_ref, _) = refs[:15]
        out_ref = refs[15]
        ds_ref = refs[16] if with_sink else None
        qf, dof, of, kf, vf, dqf, dkf, dvf = refs[-8:]
        kvh, ib = pl.program_id(0), pl.program_id(1)
        for buf, cur_ref, nxt_ref in ((qf, q_ref, qn_ref), (dof, do_ref, don_ref), (of, o_ref, on_ref)):
            buf[:BT] = cur_ref[...].astype(F32)
            buf[BT:] = nxt_ref[...].astype(F32)
        kf[:SB] = kp_ref[...].astype(F32)
        kf[SB:] = kc_ref[...].astype(F32)
        vf[:SB] = vp_ref[...].astype(F32)
        vf[SB:] = vc_ref[...].astype(F32)
        band, band_first = _band_mask(max_dist, ib > 0)
        if with_sink:
            @pl.when(ib == 0)
            def _():
                ds_ref[...] = jnp.zeros_like(ds_ref)

        def grads(rows, logzs, keys, vals, mask):
            q = _stack([qf[rows, cols] for cols in heads]).astype(BF16)
            dout = _stack([dof[rows, cols] for cols in heads]).astype(BF16)
            s_all = _dot_nt(q, keys) * ATT_SCALE
            dp_all = _dot_nt(dout, vals)
            probs, dss, deltas = [], [], []
            for g, cols in enumerate(heads):
                delta = jnp.sum(dof[rows, cols] * of[rows, cols], axis=-1, keepdims=True)
                p = jnp.exp(jnp.where(mask, s_all[cols], NEG_INF) - logzs[g][:, :1])
                probs.append(p.astype(BF16))
                dss.append((p * (dp_all[cols] - delta) * ATT_SCALE).astype(BF16))
                deltas.append(delta)
            return q, dout, _stack(probs), _stack(dss), deltas

        row = lax.broadcasted_iota(jnp.int32, (HEAD, HEAD), 0)
        col = lax.broadcasted_iota(jnp.int32, (HEAD, HEAD), 1)
        reach = col >= row + jnp.where(ib < nib - 1, HEAD - max_dist, 2 * HEAD)
        for c in range(r):
            k_old, v_old = kf[_band_rows(c, r)], vf[_band_rows(c, r)]
            dk_own = dv_own = None
            for j in range(nsub):
                rows = _band_rows(j * SB + c, r)
                k_own, v_own = kf[_band_rows((j + 1) * SB + c, r)], vf[_band_rows((j + 1) * SB + c, r)]
                kcat = jnp.concatenate([k_old, k_own], axis=0).astype(BF16)
                vcat = jnp.concatenate([v_old, v_own], axis=0).astype(BF16)
                logzs = [l_ref[rows, cols] for cols in heads]
                q, dout, p, ds, deltas = grads(rows, logzs, kcat, vcat, band_first if j == 0 else band)
                dq = _dot(ds, kcat)
                for g, cols in enumerate(heads):
                    dqf[rows, cols] = dq[cols]
                    if with_sink:
                        p_sink = jnp.exp(sink_ref[kvh * grp + g] - logzs[g][:, :1])
                        ds_ref[g * 8:(g + 1) * 8] += jnp.sum(p_sink * deltas[g])
                dk, dv = _dot_tn(ds, q), _dot_tn(p, dout)
                if j > 0:
                    done = _band_rows((j - 1) * SB + c, r)
                    dkf[done] = dk_own + dk[:HEAD]
                    dvf[done] = dv_own + dv[:HEAD]
                dk_own, dv_own = dk[HEAD:], dv[HEAD:]
                k_old, v_old = k_own, v_own
            logzs = [ln_ref[_band_rows(c, r), cols] for cols in heads]
            q, dout, p, ds, _ = grads(_band_rows(BT + c, r), logzs, k_old.astype(BF16), v_old.astype(BF16), reach)
            done = _band_rows((nsub - 1) * SB + c, r)
            dkf[done] = dk_own + _dot_tn(ds, q)
            dvf[done] = dv_own + _dot_tn(p, dout)

        cs, sn = c_ref[...], s_ref[...]
        for cols in heads:
            out_ref[:, cols] = _unrope(dqf[:, cols], cs, sn).astype(BF16)
        out_ref[:, grp * HEAD:(grp + 1) * HEAD] = _unrope(dkf[...], cs, sn).astype(BF16)
        out_ref[:, (grp + 1) * HEAD:] = dvf[...].astype(BF16)

    def nxt_row(i):
        return jnp.minimum((i + 1) * nsub, nblk - 1)

    stride = grp + 2
    q_next = pl.BlockSpec((SB, grp * HEAD), lambda h, i: (nxt_row(i), (base + h * stride) // grp))
    head_cur = pl.BlockSpec((BT, grp * HEAD), lambda h, i: (i, h))
    head_next = pl.BlockSpec((SB, grp * HEAD), lambda h, i: (nxt_row(i), h))
    table = pl.BlockSpec((BT, HEAD), lambda h, i: (i, 0))

    in_specs = [*_band_specs(BT, SB, nsub, base, grp), q_next,
                head_cur, head_next, head_cur, head_next, head_cur, head_next, table, table, UNREAD]
    args = [qkv, qkv, qkv, qkv, qkv, qkv, do, do, o, o, lse, lse, cos, sin_signed, dqkv]
    out_specs = [pl.BlockSpec((BT, stride * HEAD), lambda h, i: (i, base // stride + h))]
    out_shape = [jax.ShapeDtypeStruct(dqkv.shape, dqkv.dtype)]
    if with_sink:
        in_specs.insert(0, pl.BlockSpec(memory_space=pltpu.SMEM))
        args.insert(0, sinks)
        out_specs.append(pl.BlockSpec((None, grp * 8, HEAD), lambda h, i: (h, 0, 0)))
        out_shape.append(jax.ShapeDtypeStruct((hkv, grp * 8, HEAD), F32))
    wide = pltpu.VMEM((BT + SB, grp * HEAD), F32)
    tall = pltpu.VMEM((SB + BT, HEAD), F32)
    grad = pltpu.VMEM((BT, HEAD), F32)
    return pl.pallas_call(
        body, name=name, grid=(hkv, nib), in_specs=in_specs, out_specs=out_specs, out_shape=out_shape,
        input_output_aliases={len(args) - 1: 0},
        scratch_shapes=[wide, wide, wide, tall, tall, pltpu.VMEM((BT, grp * HEAD), F32), grad, grad],
        compiler_params=pltpu.CompilerParams(dimension_semantics=("parallel", "arbitrary"),
                                             vmem_limit_bytes=VMEM_LIMIT_LARGE),
    )(*args)


M_HEADS = 4


def mem_kv(mem, g, w, name):
    n, D = mem.shape

    def body(m_ref, g_ref, w_ref, mn_ref, kv_ref):
        x = m_ref[...]
        mn = (x * _rstd(x) * g_ref[...]).astype(BF16)
        mn_ref[...] = mn
        kv_ref[...] = _dot(mn, w_ref[...]).astype(BF16)

    return pl.pallas_call(
        body, name=name,
        out_shape=[jax.ShapeDtypeStruct((n, D), BF16), jax.ShapeDtypeStruct((n, w.shape[1]), BF16)],
        compiler_params=pltpu.CompilerParams(vmem_limit_bytes=VMEM_LIMIT),
    )(mem, g, w)


def mem_fwd(qkv, mkv, name):
    T = qkv.shape[0]
    n = mkv.shape[0]
    RB = 1024

    def body(q_ref, kv_ref, o_ref, l_ref):
        for h in range(M_HEADS):
            cols = slice(h * HEAD, (h + 1) * HEAD)
            s = _dot_nt(q_ref[:, cols], kv_ref[:, cols]) * ATT_SCALE
            m = jnp.max(s, axis=-1, keepdims=True)
            p = jnp.exp(s - m)
            den = jnp.sum(p, axis=-1, keepdims=True)
            vals = kv_ref[:, (M_HEADS + h) * HEAD:(M_HEADS + h + 1) * HEAD]
            o_ref[:, cols] = (_dot(p.astype(BF16), vals) / den).astype(BF16)
            l_ref[:, cols] = jnp.broadcast_to(m + jnp.log(den), (RB, HEAD))

    out = pl.BlockSpec((RB, M_HEADS * HEAD), lambda i: (i, 0))
    return pl.pallas_call(
        body, name=name, grid=(T // RB,),
        in_specs=[pl.BlockSpec((RB, M_HEADS * HEAD), lambda i: (i, MQ // M_HEADS)), _resident(mkv)],
        out_specs=[out, out],
        out_shape=[jax.ShapeDtypeStruct((T, M_HEADS * HEAD), BF16), jax.ShapeDtypeStruct((T, M_HEADS * HEAD), F32)],
        compiler_params=_params("parallel"),
    )(qkv, mkv)


def mem_bwd(qkv, dqkv, mkv, do, o, lse, name):
    T = qkv.shape[0]
    n = mkv.shape[0]
    RB = 1024

    def body(q_ref, kv_ref, do_ref, o_ref, l_ref, _, dq_ref, dk_ref, dv_ref):
        @pl.when(pl.program_id(0) == 0)
        def _():
            dk_ref[...] = jnp.zeros_like(dk_ref)
            dv_ref[...] = jnp.zeros_like(dv_ref)

        for h in range(M_HEADS):
            cols = slice(h * HEAD, (h + 1) * HEAD)
            keys, vals = kv_ref[:, cols], kv_ref[:, (M_HEADS + h) * HEAD:(M_HEADS + h + 1) * HEAD]
            q, dout = q_ref[:, cols], do_ref[:, cols]
            delta = jnp.sum(dout.astype(F32) * o_ref[:, cols].astype(F32), axis=-1, keepdims=True)
            p = jnp.exp(_dot_nt(q, keys) * ATT_SCALE - l_ref[:, cols][:, :1])
            ds = (p * (_dot_nt(dout, vals) - delta) * ATT_SCALE).astype(BF16)
            dq_ref[:, cols] = _dot(ds, keys).astype(BF16)
            dk_ref[:, cols] += _dot_tn(ds, q)
            dv_ref[:, cols] += _dot_tn(p.astype(BF16), dout)

    wide = M_HEADS * HEAD
    tok = pl.BlockSpec((RB, wide), lambda i: (i, 0))
    q_cols = pl.BlockSpec((RB, wide), lambda i: (i, MQ // M_HEADS))
    slot = pl.BlockSpec((n, wide), lambda i: (0, 0))
    return pl.pallas_call(
        body, name=name, grid=(T // RB,),
        in_specs=[q_cols, _resident(mkv), tok, tok, tok, UNREAD],
        out_specs=[q_cols, slot, slot],
        out_shape=[jax.ShapeDtypeStruct(dqkv.shape, dqkv.dtype),
                   jax.ShapeDtypeStruct((n, wide), F32), jax.ShapeDtypeStruct((n, wide), F32)],
        input_output_aliases={5: 0},
        compiler_params=_params("arbitrary"),
    )(qkv, mkv, do, o, lse, dqkv)


def mem_kv_bwd(mem, g, mem_n, w, dmkv, name):
    n, D = mem.shape

    def body(m_ref, g_ref, mn_ref, w_ref, d_ref, dw_ref, dg_ref):
        d = d_ref[...].astype(BF16)
        dw_ref[...] = _dot_tn(mn_ref[...], d)
        x = m_ref[...]
        dg_ref[...] = jnp.sum(_dot_nt(d, w_ref[...]) * (x * _rstd(x)), axis=0, keepdims=True)

    return pl.pallas_call(
        body, name=name,
        out_shape=[jax.ShapeDtypeStruct(w.shape, F32), jax.ShapeDtypeStruct((1, D), F32)],
        compiler_params=pltpu.CompilerParams(vmem_limit_bytes=VMEM_LIMIT),
    )(mem, g, mem_n, w, dmkv)


def _rms_bwd(dn, f, g):
    r = _rstd(f)
    fhat = f * r
    dfhat = dn * g
    df = r * (dfhat - fhat * jnp.mean(dfhat * fhat, axis=-1, keepdims=True))
    return df, jnp.sum(dn * fhat, axis=0, keepdims=True)


def ffn_tokens_bwd(dh, f, h_in, gu, g_pre, g_post, w_in, w_out, coef, name, after):
    T, D = dh.shape

    def body(dh_ref, f_ref, h_ref, gu_ref, gpre_ref, gpost_ref, win_ref, wout_ref, _,
             df_ref, dgu_ref, dhin_ref, dgpre_ref, dgpost_ref, dxn_ref):
        i, j = pl.program_id(0), pl.program_id(1)

        @pl.when(j == 0)
        def _():
            @pl.when(i == 0)
            def _():
                dgpre_ref[...] = jnp.zeros_like(dgpre_ref)
                dgpost_ref[...] = jnp.zeros_like(dgpost_ref)

            df, dg_post = _rms_bwd(coef * dh_ref[...], f_ref[...], gpost_ref[...])
            dgpost_ref[...] += dg_post
            df_ref[...] = df.astype(BF16)

        for jj in range(2):
            @pl.when(j == jj)
            def _(jj=jj):
                lo, mid, hi = 2 * jj * FF_T, (2 * jj + 1) * FF_T, (2 * jj + 2) * FF_T
                da = _dot_nt(df_ref[...], wout_ref[jj * FF_T:(jj + 1) * FF_T, :])
                gate = gu_ref[:, :FF_T].astype(F32)
                up = gu_ref[:, FF_T:].astype(F32)
                sig = _sigmoid(gate)
                dgate = (da * up * sig * (1.0 + gate * (1.0 - sig))).astype(BF16)
                dup = (da * gate * sig).astype(BF16)
                dgu_ref[:, :FF_T] = dgate
                dgu_ref[:, FF_T:] = dup
                part = _dot_nt(dgate, win_ref[:, lo:mid]) + _dot_nt(dup, win_ref[:, mid:hi])
                if jj == 0:
                    dxn_ref[...] = part
                else:
                    h = h_ref[...]
                    r = _rstd(h)
                    xhat = h * r
                    dxn = dxn_ref[...] + part
                    dxhat = dxn * gpre_ref[...]
                    dhin_ref[...] = dh_ref[...] + r * (dxhat - xhat * jnp.mean(dxhat * xhat, axis=-1, keepdims=True))
                    dgpre_ref[...] += jnp.sum(dxn * xhat, axis=0, keepdims=True)

    row = pl.BlockSpec((TM, D), lambda i, j: (i, 0))
    wide = pl.BlockSpec((TM, 2 * FF_T), lambda i, j: (i, j))
    vec = pl.BlockSpec((1, D), lambda i, j: (0, 0))
    return pl.pallas_call(
        body, name=name, grid=(T // TM, 2),
        in_specs=[row, row, row, wide, _resident(g_pre), _resident(g_post), _resident(w_in), _resident(w_out),
                  UNREAD],
        out_specs=[row, wide, row, vec, vec],
        out_shape=[jax.ShapeDtypeStruct((T, D), BF16), jax.ShapeDtypeStruct((T, 2 * D_FF), BF16),
                   jax.ShapeDtypeStruct((T, D), F32), jax.ShapeDtypeStruct((1, D), F32),
                   jax.ShapeDtypeStruct((1, D), F32)],
        scratch_shapes=[pltpu.VMEM((TM, D), F32)],
        compiler_params=pltpu.CompilerParams(dimension_semantics=("arbitrary", "arbitrary"),
                                             vmem_limit_bytes=VMEM_LIMIT_LARGE),
    )(dh, f, h_in, gu, g_pre, g_post, w_in, w_out, after)


def mm_nt_norm_bwd(pieces, h_in, dh_out, g, name, after):
    T, D = h_in.shape

    def body(*refs):
        ab = refs[:2 * len(pieces)]
        h_ref, dh_ref, g_ref, _, o_ref, dg_ref = refs[2 * len(pieces):]
        dxn = _dot_nt(ab[0][...], ab[1][...])
        for p in range(1, len(pieces)):
            dxn += _dot_nt(ab[2 * p][...], ab[2 * p + 1][...])
        h = h_ref[...]
        r = _rstd(h)
        xhat = h * r
        dxhat = dxn * g_ref[...]
        o_ref[...] = dh_ref[...] + r * (dxhat - xhat * jnp.mean(dxhat * xhat, axis=-1, keepdims=True))

        @pl.when(pl.program_id(0) == 0)
        def _():
            dg_ref[...] = jnp.zeros_like(dg_ref)

        dg_ref[...] += jnp.sum(dxn * xhat, axis=0, keepdims=True)

    in_specs, args = [], []
    for a, w in pieces:
        in_specs += [pl.BlockSpec((TM, a.shape[1]), lambda i: (i, 0)), _resident(w)]
        args += [a, w]
    row = pl.BlockSpec((TM, D), lambda i: (i, 0))
    return pl.pallas_call(
        body, name=name, grid=(T // TM,),
        in_specs=in_specs + [row, row, _resident(g), UNREAD],
        out_specs=[row, pl.BlockSpec((1, D), lambda i: (0, 0))],
        out_shape=[jax.ShapeDtypeStruct((T, D), F32), jax.ShapeDtypeStruct((1, D), F32)],
        compiler_params=_params("arbitrary"),
    )(*args, h_in, dh_out, g, after)


def gate_merge_out_bwd(dh, f, g, w_out, merged, gt, o_a, o_b, o_m, w_a, w_b, w_m, name, after):
    T = dh.shape[0]
    D = D_MODEL
    branch = ((o_a, w_a), (o_b, w_b), (o_m, w_m))

    def body(dh_ref, f_ref, g_ref, wo_ref, m_ref, gt_ref, oa_ref, ob_ref, om_ref, wa_ref, wb_ref, wm_ref, _,
             dg_ref, dwo_ref, dgt_ref, doa_ref, dob_ref, dom_ref, db_ref, dwa_ref, dwb_ref, dwm_ref):
        @pl.when(pl.program_id(0) == 0)
        def _():
            for acc in (dg_ref, dwo_ref, db_ref, dwa_ref, dwb_ref, dwm_ref):
                acc[...] = jnp.zeros_like(acc)

        df, dg = _rms_bwd(dh_ref[...], f_ref[...], g_ref[...])
        dg_ref[...] += dg
        df = df.astype(BF16)
        dwo_ref[...] += _dot_tn(m_ref[...], df)
        dmf = _dot_nt(df, wo_ref[...])
        for x, (o_ref, w_ref, do_ref, dw_ref) in enumerate(((oa_ref, wa_ref, doa_ref, dwa_ref),
                                                           (ob_ref, wb_ref, dob_ref, dwb_ref),
                                                           (om_ref, wm_ref, dom_ref, dwm_ref))):
            cols = slice(x * D, (x + 1) * D)
            gx = gt_ref[:, cols].astype(F32)
            w = w_ref[...]
            dpre = dmf * _dot(o_ref[...], w) * gx * (1.0 - gx)
            dgt_ref[:, cols] = dpre.astype(BF16)
            db_ref[:, cols] += jnp.sum(dpre, axis=0, keepdims=True)
            dp = (dmf * gx).astype(BF16)
            do_ref[...] = _dot_nt(dp, w).astype(BF16)
            dw_ref[...] += _dot_tn(dp, o_ref[...])

    def rows(width):
        return pl.BlockSpec((TM, width), lambda i: (i, 0))

    def kept(shape):
        return pl.BlockSpec(shape, lambda i: (0,) * len(shape))

    widths = [o.shape[1] for o, _ in branch]
    sums = [(1, D), (D, D), (1, 3 * D)] + [(D, k) for k in widths]
    return pl.pallas_call(
        body, name=name, grid=(T // TM,),
        in_specs=[rows(D), rows(D), _resident(g), _resident(w_out), rows(D), rows(3 * D)]
                 + [rows(k) for k in widths] + [_resident(w) for _, w in branch] + [UNREAD],
        out_specs=[kept(sums[0]), kept(sums[1]), rows(3 * D)] + [rows(k) for k in widths]
                  + [kept(shape) for shape in sums[2:]],
        out_shape=[jax.ShapeDtypeStruct(sums[0], F32), jax.ShapeDtypeStruct(sums[1], F32),
                   jax.ShapeDtypeStruct((T, 3 * D), BF16)] + [jax.ShapeDtypeStruct((T, k), BF16) for k in widths]
                  + [jax.ShapeDtypeStruct(shape, F32) for shape in sums[2:]],
        compiler_params=pltpu.CompilerParams(dimension_semantics=("arbitrary",), vmem_limit_bytes=VMEM_LIMIT_LARGE),
    )(dh, f, g, w_out, merged, gt, o_a, o_b, o_m, w_a, w_b, w_m, after)


def mm_tn(x, dy, tm, tn, name, shard_major=False, perm=None, slabs=1, after=None, wire=False):
    T, M = x.shape
    N = dy.shape[1]
    tk = min(2048, T)
    perm = perm or (lambda j: j)
    w = tn // slabs

    def body(x_ref, dy_ref, *rest):
        o_ref = rest[-2] if wire else rest[-1]

        @pl.when(pl.program_id(2) == 0)
        def _():
            o_ref[...] = jnp.zeros_like(o_ref)

        acc = _dot_tn(x_ref[...], dy_ref[...])
        if shard_major:
            for s in range(slabs):
                o_ref[s] += acc[:, s * w:(s + 1) * w]
        else:
            o_ref[...] += acc
        if wire:
            @pl.when(pl.program_id(2) == T // tk - 1)
            def _():
                rest[-1][...] = o_ref[...].astype(BF16)

    if shard_major:
        out_spec = pl.BlockSpec((slabs, tm, w), lambda i, j, k: (perm(j), i, 0))
        out_shape = jax.ShapeDtypeStruct((N // w, M, w), F32)
    else:
        out_spec = pl.BlockSpec((tm, tn), lambda i, j, k: (i, j))
        out_shape = jax.ShapeDtypeStruct((M, N), F32)
    return pl.pallas_call(
        body, name=name, grid=(M // tm, N // tn, T // tk),
        in_specs=[pl.BlockSpec((tk, tm), lambda i, j, k: (k, i)),
                  pl.BlockSpec((tk, tn), lambda i, j, k: (k, j))] + ([] if after is None else [UNREAD]),
        out_specs=[out_spec, out_spec] if wire else out_spec,
        out_shape=[out_shape, jax.ShapeDtypeStruct(out_shape.shape, BF16)] if wire else out_shape,
        compiler_params=_params("parallel", "parallel", "arbitrary"),
    )(x, dy, *([] if after is None else [after]))


def rope_tables(T, zero):
    half = HEAD // 2
    inv = ROPE_THETA ** (-jnp.arange(half, dtype=F32) / half)
    ang = (jnp.arange(T).astype(F32) + zero)[:, None] * inv[None, :]
    cos, sin = jnp.cos(ang), jnp.sin(ang)
    return jnp.concatenate([cos, cos], axis=1), jnp.concatenate([-sin, sin], axis=1)


def layer_step(x, mem, target, gains, sinks, b_gate, weights_of, send_grads, zero):
    T = x.shape[0]
    cos, sin_signed = rope_tables(T, zero)
    no_sink = jnp.full((2,), NEG_INF, F32)

    xn1 = rms_scale(x, gains["ffn1_norm_pre"], "ffn1_norm", cos)
    w = dict(weights_of("ffn1_in", xn1))
    _, gu1, a1 = ffn_in(x, gains["ffn1_norm_pre"], w["ffn1_w_in"], "ffn1_in", xn=xn1)
    w.update(weights_of("ffn1_out", a1))
    f1, h1 = mm_norm_res(a1, w["ffn1_w_out"], x, gains["ffn1_norm_post"], 0.5, "ffn1_out")
    w.update(weights_of("mix_in", f1))
    u, qkv, gt = mix_in(h1, gains["mix_norm_pre"], w["w_in"], w["w_gate"], b_gate, cos, sin_signed, "mix_in")
    w.update(weights_of("mix_rest", u))
    outs, lses = [], []
    for gidx, (window, dil) in enumerate(DIL):
        last = gidx == len(DIL) - 1
        o_g, l_g = band_fwd(qkv, no_sink, r=dil, base=A_BASE + 6 * gidx, hkv=2, grp=1, max_dist=window // dil,
                            out_dtype=BF16 if last else F32, name=f"attn_a{gidx}_fwd",
                            merge=(outs, lses) if last else None)
        outs.append(o_g)
        lses.append(l_g)
    o_a, l_a = outs[-1], lses[-1]
    o_b, l_b = band_fwd(qkv, sinks, r=1, base=B_BASE, hkv=2, grp=2, max_dist=HEAD - 1, out_dtype=BF16,
                        name="attn_b_fwd")
    mem_n, mkv = mem_kv(mem, gains["mem_norm"], w["w_mem_kv"], "mem_kv")
    o_m, l_m = mem_fwd(qkv, mkv, "attn_m_fwd")
    merged, mo, h2 = gate_merge_out(gt, o_a, o_b, o_m, w["w_o_a"], w["w_o_b"], w["w_o_m"], w["w_out"], h1,
                                    gains["mix_norm_post"], "gate_merge_out")
    w.update(weights_of("ffn2", mo))
    xn2, gu2, a2 = ffn_in(h2, gains["ffn2_norm_pre"], w["ffn2_w_in"], "ffn2_in")
    f2, dy, sq = mm_norm_res(a2, w["ffn2_w_out"], h2, gains["ffn2_norm_post"], 0.5, "ffn2_out", target=target)

    grads = {}

    def ffn_bwd(tag, dh_out, f, gu, a, xn, h_in, after):
        df, dgu, dh_in, grads[f"{tag}_norm_pre"], grads[f"{tag}_norm_post"] = ffn_tokens_bwd(
            dh_out, f, h_in, gu, gains[f"{tag}_norm_pre"], gains[f"{tag}_norm_post"], w[f"{tag}_w_in"],
            w[f"{tag}_w_out"], 0.5, f"{tag}_tokens_bwd", after)
        sent = send_grads(f"{tag}_in", {f"{tag}_w_in": mm_tn(
            xn, dgu, D_MODEL, FF_T, f"{tag}_w_in_grad", shard_major=True, perm=_ffn_perm, wire=True)})
        sent = send_grads(f"{tag}_out", {f"{tag}_w_out": mm_tn(
            a, df, FF_T, D_MODEL, f"{tag}_w_out_grad", after=sent, wire=True)})
        return dh_in, sent

    dh2, sent = ffn_bwd("ffn2", dy, f2, gu2, a2, xn2, h2, dy)

    mix = {}
    (grads["mix_norm_post"], mix["w_out"], dgt, do_a, do_b, do_m, grads["b_gate"],
     dwa_t, dwb_t, dwm_t) = gate_merge_out_bwd(
        dh2, mo, gains["mix_norm_post"], w["w_out"], merged, gt, o_a, o_b, o_m, w["w_o_a"], w["w_o_b"],
        w["w_o_m"], "gate_merge_out_bwd", sent)
    mix["w_o_a"], mix["w_o_b"], mix["w_o_m"] = dwa_t.T, dwb_t.T, dwm_t.T

    dqkv = lax.empty(qkv.shape, qkv.dtype)
    for gidx, (window, dil) in enumerate(DIL):
        dqkv, = band_bwd(qkv, dqkv, do_a, o_a, l_a, cos, sin_signed, None, r=dil, base=A_BASE + 6 * gidx, hkv=2,
                         grp=1, max_dist=window // dil, name=f"attn_a{gidx}_bwd")
    dqkv, dsink = band_bwd(qkv, dqkv, do_b, o_b, l_b, cos, sin_signed, sinks, r=1, base=B_BASE, hkv=2, grp=2,
                           max_dist=HEAD - 1, name="attn_b_bwd")
    grads["sinks"] = -dsink[:, ::8, 0].reshape(1, 4)
    dqkv, dmk, dmv = mem_bwd(qkv, dqkv, mkv, do_m, o_m, l_m, "attn_m_bwd")
    mix["w_mem_kv"], grads["mem_norm"] = mem_kv_bwd(
        mem, gains["mem_norm"], mem_n, w["w_mem_kv"], jnp.concatenate([dmk, dmv], axis=1), "mem_kv_bwd")

    mix["w_in"] = mm_tn(u, dqkv, D_MODEL, 1280, "w_in_grad")
    mix["w_gate"] = mm_tn(u, dgt, D_MODEL, 1536, "w_gate_grad", shard_major=True, slabs=2, wire=True)
    sent = send_grads("mix", mix)
    dh1, grads["mix_norm_pre"] = mm_nt_norm_bwd(
        [(dqkv, w["w_in"]), (dgt, w["w_gate"])], h1, dh2, gains["mix_norm_pre"], "mix_in_bwd", sent)

    dx, _ = ffn_bwd("ffn1", dh1, f1, gu1, a1, xn1, x, dh1)
    return sq, dx, grads


def _place():
    return lax.axis_index("x"), lax.axis_index("y"), lax.axis_index("c")


def _other_chips(x, y):
    return [(1 - x, y), (x, 1 - y), (1 - x, 1 - y)]


def _hbm(n):
    return [pl.BlockSpec(memory_space=pltpu.HBM)] * n


SEM = pl.BlockSpec(memory_space=pltpu.SEMAPHORE)
SIDE_EFFECT = pltpu.SideEffectType.DATAFLOW_SIDE_EFFECTING


def _chip_copy(src, land, sems, i, j, dst_slot, scatter):
    x, y, c = _place()
    px, py = _other_chips(x, y)[j]
    send_sems, recv_sems = sems
    return pltpu.make_async_remote_copy(
        src_ref=src[i].at[2 * px + py] if scatter else src[i], dst_ref=land[i].at[dst_slot],
        send_sem=send_sems.at[3 * i + j], recv_sem=recv_sems.at[3 * i + j],
        device_id=(px, py, c), device_id_type=MESH)


def chip_copies_start(srcs, lands, groups, scatter, name, after=None):
    n = len(srcs)

    def body(*refs):
        src, land = refs[:n], refs[n:2 * n]
        first_sem = 2 * n + (after is not None)
        sems = refs[first_sem:first_sem + 2 * len(groups)]
        token = refs[-1]
        x, y, _ = _place()
        for g, members in enumerate(groups):
            part = ([src[i] for i in members], [land[i] for i in members])
            for t in range(len(members)):
                for j in range(3):
                    _chip_copy(*part, sems[2 * g:2 * g + 2], t, j, 2 * x + y, scatter).start()
        token[...] = jnp.zeros_like(token)

    sem_shapes = [pltpu.SemaphoreType.DMA((3 * len(m),)) for m in groups for _ in range(2)]
    thru = [pltpu.HBM(a.shape, a.dtype) for a in (*srcs, *lands)]
    res = pl.pallas_call(
        body, name=name,
        out_shape=(*sem_shapes, *thru, jax.ShapeDtypeStruct((8, 128), F32)),
        in_specs=_hbm(2 * n) + ([] if after is None else [UNREAD]),
        out_specs=(*[SEM] * len(sem_shapes), *_hbm(2 * n), pl.BlockSpec(memory_space=pltpu.VMEM)),
        input_output_aliases={i: len(sem_shapes) + i for i in range(2 * n)},
        compiler_params=pltpu.CompilerParams(has_side_effects=SIDE_EFFECT),
    )(*[pltpu.with_memory_space_constraint(a, pltpu.HBM) for a in (*srcs, *lands)],
      *([] if after is None else [after]))
    k = len(sem_shapes)
    sems = [tuple(res[2 * g:2 * g + 2]) for g in range(len(groups))]
    return sems, list(res[k:k + n]), list(res[k + n:k + 2 * n]), res[-1]


def chip_copies_wait(srcs, lands, sems, after, scatter, name):
    n = len(srcs)
    after = list(after) if isinstance(after, (list, tuple)) else [after]

    def body(*refs):
        src, land = refs[:n], refs[n:2 * n]
        pair = refs[2 * n:2 * n + 2]
        x, y, _ = _place()
        for i in range(n):
            for j, (px, py) in enumerate(_other_chips(x, y)):
                copy = _chip_copy(src, land, pair, i, j, 2 * px + py, scatter)
                copy.wait_send()
                copy.wait_recv()

    res = pl.pallas_call(
        body, name=name,
        out_shape=[pltpu.HBM(a.shape, a.dtype) for a in (*srcs, *lands)],
        in_specs=[*_hbm(2 * n), SEM, SEM] + [UNREAD] * len(after),
        out_specs=_hbm(2 * n),
        input_output_aliases={i: i for i in range(2 * n)},
        compiler_params=pltpu.CompilerParams(has_side_effects=SIDE_EFFECT),
    )(*srcs, *lands, *sems, *after)
    return list(res[n:])


def small_all_gather(small, name):
    flips = [(fx, fy, fc) for fx in (0, 1) for fy in (0, 1) for fc in (0, 1)][1:]

    def body(in_ref, out_ref, send_sems, recv_sems, local_sem):
        x, y, c = _place()
        me = 4 * x + 2 * y + c

        def copy(k, slot):
            fx, fy, fc = flips[k]
            return pltpu.make_async_remote_copy(
                src_ref=in_ref, dst_ref=out_ref.at[slot], send_sem=send_sems.at[k], recv_sem=recv_sems.at[k],
                device_id=(x ^ fx, y ^ fy, c ^ fc), device_id_type=MESH)

        local = pltpu.make_async_copy(in_ref, out_ref.at[me], local_sem)
        local.start()
        for k in range(len(flips)):
            copy(k, me).start()
        for k, (fx, fy, fc) in enumerate(flips):
            copy(k, 4 * (x ^ fx) + 2 * (y ^ fy) + (c ^ fc)).wait()
        local.wait()

    return pl.pallas_call(
        body, name=name, in_specs=_hbm(1), out_specs=_hbm(1)[0],
        out_shape=jax.ShapeDtypeStruct((N_DEV,) + small.shape, small.dtype),
        scratch_shapes=[pltpu.SemaphoreType.DMA((len(flips),)), pltpu.SemaphoreType.DMA((len(flips),)),
                        pltpu.SemaphoreType.DMA],
    )(small)


def _sibling_copy(src, land, sems, i):
    x, y, c = _place()
    return pltpu.make_async_remote_copy(
        src_ref=src[i], dst_ref=land[i], send_sem=sems[0].at[i], recv_sem=sems[1].at[i],
        device_id=(x, y, 1 - c), device_id_type=MESH)


def sibling_copies_start(parts, name):
    n = len(parts)
    lands = [lax.empty(p.shape, p.dtype) for p in parts]

    def body(*refs):
        src, land, sems, token = refs[:n], refs[n:2 * n], refs[2 * n:2 * n + 2], refs[-1]
        for i in range(n):
            _sibling_copy(src, land, sems, i).start()
        token[...] = jnp.zeros_like(token)

    res = pl.pallas_call(
        body, name=name,
        out_shape=(pltpu.SemaphoreType.DMA((n,)), pltpu.SemaphoreType.DMA((n,)),
                   *[pltpu.HBM(a.shape, a.dtype) for a in (*parts, *lands)], jax.ShapeDtypeStruct((8, 128), F32)),
        in_specs=_hbm(2 * n),
        out_specs=(SEM, SEM, *_hbm(2 * n), pl.BlockSpec(memory_space=pltpu.VMEM)),
        input_output_aliases={i: 2 + i for i in range(2 * n)},
        compiler_params=pltpu.CompilerParams(has_side_effects=SIDE_EFFECT),
    )(*[pltpu.with_memory_space_constraint(a, pltpu.HBM) for a in (*parts, *lands)])
    return tuple(res[:2]), list(res[2:2 + n]), list(res[2 + n:2 + 2 * n]), res[-1]


def sibling_copies_wait(parts, lands, sems, after, name):
    n = len(parts)

    def body(*refs):
        src, land, sems = refs[:n], refs[n:2 * n], refs[2 * n:2 * n + 2]
        for i in range(n):
            copy = _sibling_copy(src, land, sems, i)
            copy.wait_send()
            copy.wait_recv()

    res = pl.pallas_call(
        body, name=name,
        out_shape=[pltpu.HBM(a.shape, a.dtype) for a in (*parts, *lands)],
        in_specs=[*_hbm(2 * n), SEM, SEM, UNREAD],
        out_specs=_hbm(2 * n),
        input_output_aliases={i: i for i in range(2 * n)},
        compiler_params=pltpu.CompilerParams(has_side_effects=SIDE_EFFECT),
    )(*parts, *lands, *sems, after)
    return list(res[n:])


def _row_tile(rows):
    for t in (256, 176, 128, 64, 32, 16, 8):
        if rows % t == 0:
            return t
    return rows


def chip_partial_sum(me, own_sm, recv, name):
    _, rows, cols = own_sm.shape
    tr = _row_tile(rows)

    def body(me_ref, own_ref, r1, r2, r3, o_ref):
        o_ref[...] = own_ref[...] + r1[...].astype(F32) + r2[...].astype(F32) + r3[...].astype(F32)

    def slot(d):
        return pl.BlockSpec((None, tr, cols), lambda i, me_ref: ((me_ref[0] + d) % N_CHIPS, i, 0))

    return pl.pallas_call(
        body, name=name,
        grid_spec=pltpu.PrefetchScalarGridSpec(
            num_scalar_prefetch=1, grid=(rows // tr,),
            in_specs=[slot(0), slot(1), slot(2), slot(3)],
            out_specs=pl.BlockSpec((tr, cols), lambda i, me_ref: (i, 0))),
        out_shape=jax.ShapeDtypeStruct((rows, cols), F32),
        compiler_params=_params("parallel"),
    )(me, own_sm, recv, recv, recv)


def _adamw(w, g, m, v):
    m = ADAM_B1 * m + (1.0 - ADAM_B1) * g
    v = ADAM_B2 * v + (1.0 - ADAM_B2) * (g * g)
    m_hat = m / (1.0 - ADAM_B1 ** ADAM_STEP)
    v_hat = v / (1.0 - ADAM_B2 ** ADAM_STEP)
    delta = -ADAM_LR * (m_hat / (jnp.sqrt(v_hat) + ADAM_EPS) + ADAM_WD * w)
    return delta, m, v


def adamw_pair(part, sib, w, m, v, name):
    rows, cols = w.shape
    tr = _row_tile(rows)

    def body(p_ref, s_ref, w_ref, m_ref, v_ref, g_ref, d_ref, nm_ref, nv_ref):
        g = p_ref[...] + s_ref[...]
        g_ref[...] = g
        d_ref[...], nm_ref[...], nv_ref[...] = _adamw(w_ref[...], g, m_ref[...], v_ref[...])

    spec = pl.BlockSpec((tr, cols), lambda i: (i, 0))
    return pl.pallas_call(
        body, name=name, grid=(rows // tr,), in_specs=[spec] * 5, out_specs=[spec] * 4,
        out_shape=[jax.ShapeDtypeStruct((rows, cols), F32)] * 4,
        compiler_params=_params("parallel"),
    )(part, sib, w, m, v)


def adamw_small(g_all, w, m, v, name):
    def body(ga_ref, w_ref, m_ref, v_ref, g_ref, d_ref, nm_ref, nv_ref):
        g = ga_ref[0]
        for k in range(1, N_DEV):
            g = g + ga_ref[k]
        g_ref[...] = g
        d_ref[...], nm_ref[...], nv_ref[...] = _adamw(w_ref[...], g, m_ref[...], v_ref[...])

    return pl.pallas_call(
        body, name=name, out_shape=[jax.ShapeDtypeStruct(w.shape, F32)] * 4,
    )(g_all, w, m, v)


WEIGHTS = ("ffn1_norm_pre", "ffn1_w_in", "ffn1_w_out", "ffn1_norm_post", "mix_norm_pre", "w_in", "sinks",
           "mem_norm", "w_mem_kv", "w_gate", "b_gate", "w_o_a", "w_o_b", "w_o_m", "w_out", "mix_norm_post",
           "ffn2_norm_pre", "ffn2_w_in", "ffn2_w_out", "ffn2_norm_post")
GATHER_STAGES = (("ffn1_in", "ffn1_out"), ("mix_in",), ("mix_rest", "ffn2"))
GATHER_GROUPS = {"ffn1_in": ("ffn1_w_in",), "ffn1_out": ("ffn1_w_out",),
                 "mix_in": ("w_in", "w_gate"), "mix_rest": ("w_mem_kv", "w_o_a", "w_o_b", "w_o_m", "w_out"),
                 "ffn2": ("ffn2_w_in", "ffn2_w_out")}
GROUPS = {"ffn1_in": ("ffn1_w_in",), "ffn1_out": ("ffn1_w_out",),
          "mix": ("w_in", "w_gate", "w_mem_kv", "w_o_a", "w_o_b", "w_o_m", "w_out"),
          "ffn2_in": ("ffn2_w_in",), "ffn2_out": ("ffn2_w_out",)}
COLUMN_SHARDED = ("ffn1_w_in", "ffn2_w_in", "w_in", "w_gate", "w_o_a", "w_o_b", "w_o_m")
KEPT_SHARD_MAJOR = ("ffn1_w_in", "ffn2_w_in", "w_gate")
GAINS = ("ffn1_norm_pre", "ffn1_norm_post", "mix_norm_pre", "mem_norm", "mix_norm_post", "ffn2_norm_pre",
         "ffn2_norm_post")
SMALL_ROWS = 16


def _pack_small(t):
    sinks = jnp.pad(t["sinks"], ((0, 0), (0, D_MODEL - t["sinks"].shape[1])))
    rows = [t[k] for k in GAINS] + [t["b_gate"].reshape(3, D_MODEL), sinks]
    packed = jnp.concatenate(rows, axis=0)
    return jnp.pad(packed, ((0, SMALL_ROWS - packed.shape[0]), (0, 0)))


def _unpack_small(p):
    out = {k: p[i:i + 1] for i, k in enumerate(GAINS)}
    out["b_gate"] = p[7:10].reshape(1, 3 * D_MODEL)
    out["sinks"] = p[10:11, :4]
    return out


def kernel(x, mem, ffn1_norm_pre, ffn1_w_in, ffn1_w_out, ffn1_norm_post, mix_norm_pre, w_in, sinks, mem_norm, w_mem_kv, w_gate, b_gate, w_o_a, w_o_b, w_o_m, w_out, mix_norm_post, ffn2_norm_pre, ffn2_w_in, ffn2_w_out, ffn2_norm_post, loss_target, m_ffn1_norm_pre, m_ffn1_w_in, m_ffn1_w_out, m_ffn1_norm_post, m_mix_norm_pre, m_w_in, m_sinks, m_mem_norm, m_w_mem_kv, m_w_gate, m_b_gate, m_w_o_a, m_w_o_b, m_w_o_m, m_w_out, m_mix_norm_post, m_ffn2_norm_pre, m_ffn2_w_in, m_ffn2_w_out, m_ffn2_norm_post, v_ffn1_norm_pre, v_ffn1_w_in, v_ffn1_w_out, v_ffn1_norm_post, v_mix_norm_pre, v_w_in, v_sinks, v_mem_norm, v_w_mem_kv, v_w_gate, v_b_gate, v_w_o_a, v_w_o_b, v_w_o_m, v_w_out, v_mix_norm_post, v_ffn2_norm_pre, v_ffn2_w_in, v_ffn2_w_out, v_ffn2_norm_post):
    given = dict(locals())
    wt = {k: given[k] for k in WEIGHTS}
    mom = {k: given["m_" + k] for k in WEIGHTS}
    var = {k: given["v_" + k] for k in WEIGHTS}
    chip = (2 * lax.axis_index("x") + lax.axis_index("y")).astype(jnp.int32)
    me = chip.reshape(1)

    def landing_zone(own):
        return lax.dynamic_update_slice_in_dim(lax.empty((N_CHIPS,) + own.shape, own.dtype), own[None], chip, 0)

    started = {}
    tokens = []

    def stage_keys(stage):
        return [k for g in GATHER_STAGES[stage] for k in GATHER_GROUPS[g]]

    def prepare(stage):
        shards = [(wt[k][0] + tokens[0][0, 0] if tokens else wt[k][0]).astype(BF16) for k in stage_keys(stage)]
        return shards, [landing_zone(s) for s in shards]

    def start_gather(stage, after):
        groups, keys = GATHER_STAGES[stage], stage_keys(stage)
        members = [[keys.index(k) for k in GATHER_GROUPS[g]] for g in groups]
        sems, shards, lands, token = chip_copies_start(
            *prepared[stage], members, False, f"weight_gather_start_{stage}", after)
        tokens.append(token)
        for g, idx, pair in zip(groups, members, sems):
            started[g] = ([shards[i] for i in idx], [lands[i] for i in idx], pair)

    prepared = {0: prepare(0)}
    start_gather(0, None)
    prepared.update({stage: prepare(stage) for stage in range(1, len(GATHER_STAGES))})

    def weights_of(group, after):
        if group == GATHER_STAGES[0][0]:
            after = [after] + [a for stage in range(1, len(GATHER_STAGES)) for part in prepared[stage] for a in part]
        got = chip_copies_wait(*started[group], after, False, f"weight_gather_wait_{group}")
        stage = [s + 1 for s, groups in enumerate(GATHER_STAGES[:-1]) if groups[0] == group]
        if stage:
            start_gather(stage[0], got[0])
        full = {}
        for k, g in zip(GATHER_GROUPS[group], got):
            if k in COLUMN_SHARDED:
                if k in ("ffn1_w_in", "ffn2_w_in"):
                    g = jnp.stack([g[0], g[2], g[1], g[3]])
                full[k] = jnp.swapaxes(g, 0, 1).reshape(g.shape[1], N_CHIPS * g.shape[2])
                if k == "w_in":
                    full[k] = to_kernel_heads(full[k])
            else:
                full[k] = g.reshape(N_CHIPS * g.shape[1], g.shape[2])
        return full

    in_flight = {}

    def send_grads(group, grads):
        def shard_major(k, g):
            if k in KEPT_SHARD_MAJOR:
                return g
            if k in COLUMN_SHARDED:
                return jnp.swapaxes(g.reshape(g.shape[0], N_CHIPS, g.shape[1] // N_CHIPS), 0, 1)
            return g.reshape(N_CHIPS, g.shape[0] // N_CHIPS, g.shape[1])

        own, wire = [], []
        for k in GROUPS[group]:
            g, rounded = grads[k] if isinstance(grads[k], (tuple, list)) else (grads[k], None)
            g = shard_major(k, from_kernel_heads(g) if k == "w_in" else g)
            own.append(g)
            wire.append(g.astype(BF16) if rounded is None else shard_major(k, rounded))
        zones = [lax.empty(b.shape, b.dtype) for b in wire]
        pair, wire, zones, sent = chip_copies_start(
            wire, zones, [list(range(len(wire)))], True, f"grad_scatter_start_{group}")
        in_flight[group] = (own, wire, zones, pair[0], sent)
        return sent

    gains = {k: wt[k] for k in GAINS}
    sq, dx, grads = layer_step(
        x[0], mem[0], loss_target[0], gains, sinks[0], b_gate, weights_of, send_grads, tokens[0][0, 0])
    loss = lax.psum(0.5 * sq[0, 0] / D_MODEL, ("x", "y", "c"))

    res = {}
    after = in_flight["ffn1_out"][4]
    swaps = []
    for stage in (("ffn2_in", "ffn2_out", "mix", "ffn1_in"), ("ffn1_out",)):
        names, parts = [], []
        for group in stage:
            own, wire, zones, pair, _ = in_flight[group]
            received = chip_copies_wait(wire, zones, pair, after, True, f"grad_scatter_wait_{group}")
            for k, g, r in zip(GROUPS[group], own, received):
                names.append(k)
                parts.append(chip_partial_sum(me, g, r, f"{k}_chip_sum"))
        pair, parts, lands, after = sibling_copies_start(parts, f"sibling_start_{stage[-1]}")
        swaps.append((stage[-1], names, parts, lands, pair))
    small_all = small_all_gather(_pack_small(grads), "small_grad_gather")
    packed = adamw_small(small_all, _pack_small(wt), _pack_small(mom), _pack_small(var), "small_adamw")
    after = packed[0]
    for tag, names, parts, lands, pair in swaps:
        sibs = sibling_copies_wait(parts, lands, pair, after, f"sibling_wait_{tag}")
        for k, p, s in zip(names, parts, sibs):
            res[k] = [t[None] for t in adamw_pair(p, s, wt[k][0], mom[k][0], var[k][0], f"{k}_adamw")]
        after = res[names[-1]][0]
    for idx, p in enumerate(packed):
        for k, t in _unpack_small(p).items():
            res.setdefault(k, [None] * 4)[idx] = t

    return (loss, dx[None], *[res[k][0] for k in WEIGHTS], *[res[k][1] for k in WEIGHTS],
            *[res[k][2] for k in WEIGHTS], *[res[k][3] for k in WEIGHTS])
```

```python
import functools

import jax
import jax.numpy as jnp
from jax import lax
from jax.experimental import pallas as pl
from jax.experimental.pallas import tpu as pltpu

F32 = jnp.float32
BF16 = jnp.bfloat16

D_MODEL = 1024
D_FF = 2816
HEAD = 128
N_CHIPS = 4
N_DEV = 8
EPS = 1e-6
NEG_INF = -1e30
ROPE_THETA = 10000.0
ATT_SCALE = HEAD ** -0.5

ADAM_LR = 0.001
ADAM_B1 = 0.9
ADAM_B2 = 0.999
ADAM_EPS = 1e-08
ADAM_WD = 0.01
ADAM_STEP = 10

VMEM_LIMIT = 52 * 2 ** 20
VMEM_LIMIT_LARGE = 60 * 2 ** 20
MESH = pl.DeviceIdType.MESH

QKV_W = 3840
DIL = ((128, 1), (512, 4), (2048, 16))
B_BASE, MQ, A_BASE = 0, 8, 12
_AQ, _AK, _AV, _BQ, _BK, _BV, _MQ = 0, 6, 12, 18, 22, 24, 26
HEAD_ORDER = tuple(
    [h for j in range(2) for h in (_BQ + 2 * j, _BQ + 2 * j + 1, _BK + j, _BV + j)]
    + [_MQ + i for i in range(4)]
    + [h for g in range(3) for i in range(2) for h in (_AQ + 2 * g + i, _AK + 2 * g + i, _AV + 2 * g + i)])
ROTARY_HEADS = tuple(p for p, h in enumerate(HEAD_ORDER) if h < _AV or _BQ <= h < _BV)


def to_kernel_heads(w):
    return jnp.concatenate([w[..., h * HEAD:(h + 1) * HEAD] for h in HEAD_ORDER], axis=-1)


def from_kernel_heads(w):
    place = {h: p for p, h in enumerate(HEAD_ORDER)}
    return jnp.concatenate([w[..., place[h] * HEAD:(place[h] + 1) * HEAD] for h in range(len(HEAD_ORDER))], axis=-1)

TM = 512
FF_T = D_FF // 2


def _params(*sem):
    return pltpu.CompilerParams(dimension_semantics=sem, vmem_limit_bytes=VMEM_LIMIT)


def _dot(a, b):
    return jnp.dot(a, b, preferred_element_type=F32)


def _dot_nt(a, b):
    return lax.dot_general(a, b, (((1,), (1,)), ((), ())), preferred_element_type=F32)


def _dot_tn(a, b):
    return lax.dot_general(a, b, (((0,), (0,)), ((), ())), preferred_element_type=F32)


def _rstd(x):
    return lax.rsqrt(jnp.mean(x * x, axis=-1, keepdims=True) + EPS)


def _sigmoid(x):
    return 0.5 * jnp.tanh(0.5 * x) + 0.5


def _ffn_perm(k):
    return (k % 2) * 2 + k // 2


UNREAD = pl.BlockSpec(memory_space=pl.ANY)


def _resident(arr):
    return pl.BlockSpec(arr.shape, lambda *_: (0,) * arr.ndim, pipeline_mode=pl.Buffered(1))


def rms_scale(x, g, name, after):
    T, D = x.shape
    tm = 1024

    def body(x_ref, g_ref, _, o_ref):
        v = x_ref[...]
        o_ref[...] = (v * _rstd(v) * g_ref[...]).astype(BF16)

    spec = pl.BlockSpec((tm, D), lambda i: (i, 0))
    return pl.pallas_call(
        body, name=name, grid=(T // tm,), in_specs=[spec, _resident(g), UNREAD], out_specs=spec,
        out_shape=jax.ShapeDtypeStruct((T, D), BF16), compiler_params=_params("parallel"),
    )(x, g, after)


def ffn_in(h, g, w, name, xn=None):
    T, D = h.shape
    normed = xn is not None

    def body(h_ref, g_ref, w_ref, *outs):
        if normed:
            xn, (gu_ref, a_ref) = h_ref[...], outs
        else:
            xn_ref, gu_ref, a_ref = outs
            x = h_ref[...]
            xn = (x * _rstd(x) * g_ref[...]).astype(BF16)
            xn_ref[...] = xn
        for j in range(2):
            gu = _dot(xn, w_ref[:, j * 2 * FF_T:(j + 1) * 2 * FF_T])
            gu_ref[:, j * 2 * FF_T:(j + 1) * 2 * FF_T] = gu.astype(BF16)
            gate, up = gu[:, :FF_T], gu[:, FF_T:]
            a_ref[:, j * FF_T:(j + 1) * FF_T] = (gate * _sigmoid(gate) * up).astype(BF16)

    def rows(width):
        return pl.BlockSpec((TM, width), lambda i: (i, 0))

    res = pl.pallas_call(
        body, name=name,
        grid=(T // TM,),
        in_specs=[rows(D), _resident(g), _resident(w)],
        out_specs=[rows(D)] * (not normed) + [rows(2 * D_FF), rows(D_FF)],
        out_shape=[jax.ShapeDtypeStruct((T, D), BF16)] * (not normed)
                  + [jax.ShapeDtypeStruct((T, 2 * D_FF), BF16), jax.ShapeDtypeStruct((T, D_FF), BF16)],
        compiler_params=_params("parallel"),
    )(xn if normed else h, g, w)
    return (xn, *res) if normed else tuple(res)


def mm_norm_res(a, w, h_in, g, coef, name, target=None):
    T, K = a.shape
    D = w.shape[1]
    final = target is not None

    def body(*refs):
        if final:
            a_ref, w_ref, h_ref, g_ref, t_ref, f_ref, o_ref, l_ref = refs
        else:
            a_ref, w_ref, h_ref, g_ref, f_ref, o_ref = refs
        f = _dot(a_ref[...], w_ref[...])
        f_ref[...] = f
        y = h_ref[...] + coef * (f * _rstd(f) * g_ref[...])
        if final:
            err = y - t_ref[...]
            o_ref[...] = err * (1.0 / D)

            @pl.when(pl.program_id(0) == 0)
            def _():
                l_ref[...] = jnp.zeros_like(l_ref)

            sq = jnp.sum((err * err).reshape(TM // 8, 8, D), axis=0)
            l_ref[...] += functools.reduce(jnp.add, [sq[:, c:c + HEAD] for c in range(0, D, HEAD)])
        else:
            o_ref[...] = y

    row = pl.BlockSpec((TM, D), lambda i: (i, 0))
    in_specs = [pl.BlockSpec((TM, K), lambda i: (i, 0)),
                _resident(w),
                row, pl.BlockSpec((1, D), lambda i: (0, 0))]
    out_specs = [row, row]
    out_shape = [jax.ShapeDtypeStruct((T, D), F32), jax.ShapeDtypeStruct((T, D), F32)]
    args = [a, w, h_in, g]
    if final:
        in_specs.append(row)
        args.append(target)
        out_specs.append(pl.BlockSpec((8, 128), lambda i: (0, 0)))
        out_shape.append(jax.ShapeDtypeStruct((8, 128), F32))
    return pl.pallas_call(
        body, name=name, grid=(T // TM,), in_specs=in_specs, out_specs=out_specs, out_shape=out_shape,
        compiler_params=_params("arbitrary"),
    )(*args)


def _rope(x, cos, sin_signed):
    return x * cos + pltpu.roll(x, HEAD // 2, axis=1) * sin_signed


def _unrope(x, cos, sin_signed):
    return x * cos - pltpu.roll(x, HEAD // 2, axis=1) * sin_signed


def mix_in(h, g, w, w_gate, b_gate, cos, sin_signed, name):
    T, D = h.shape
    tn = 768

    def body(h_ref, g_ref, w_ref, wg_ref, b_ref, c_ref, s_ref, u_ref, o_ref, gt_ref):
        x = h_ref[...]
        u = (x * _rstd(x) * g_ref[...]).astype(BF16)
        u_ref[...] = u
        c, s = c_ref[...], s_ref[...]
        for j in range(QKV_W // tn):
            acc = _dot(u, w_ref[:, j * tn:(j + 1) * tn])
            for hd in range(tn // HEAD):
                head = j * (tn // HEAD) + hd
                part = acc[:, hd * HEAD:(hd + 1) * HEAD]
                if head in ROTARY_HEADS:
                    part = _rope(part, c, s)
                o_ref[:, head * HEAD:(head + 1) * HEAD] = part.astype(BF16)
        for j in range(w_gate.shape[1] // tn):
            cols = slice(j * tn, (j + 1) * tn)
            gt_ref[:, cols] = _sigmoid(_dot(u, wg_ref[:, cols]) + b_ref[:, cols]).astype(BF16)

    def rows(width):
        return pl.BlockSpec((TM, width), lambda i: (i, 0))

    return pl.pallas_call(
        body, name=name,
        grid=(T // TM,),
        in_specs=[rows(D), _resident(g), _resident(w), _resident(w_gate), _resident(b_gate), rows(HEAD), rows(HEAD)],
        out_specs=[rows(D), rows(QKV_W), rows(w_gate.shape[1])],
        out_shape=[jax.ShapeDtypeStruct((T, D), BF16), jax.ShapeDtypeStruct((T, QKV_W), BF16),
                   jax.ShapeDtypeStruct((T, w_gate.shape[1]), BF16)],
        compiler_params=_params("parallel"),
    )(h, g, w, w_gate, b_gate, cos, sin_signed)


def gate_merge_out(gt, o_a, o_b, o_m, w_a, w_b, w_m, w_out, h_in, g, name):
    T = gt.shape[0]
    D = D_MODEL

    def body(gt_ref, oa_ref, ob_ref, om_ref, wa_ref, wb_ref, wm_ref, wo_ref, h_ref, g_ref, m_ref, f_ref, o_ref):
        acc = gt_ref[:, :D].astype(F32) * _dot(oa_ref[...], wa_ref[...])
        acc += gt_ref[:, D:2 * D].astype(F32) * _dot(ob_ref[...], wb_ref[...])
        acc += gt_ref[:, 2 * D:].astype(F32) * _dot(om_ref[...], wm_ref[...])
        merged = acc.astype(BF16)
        m_ref[...] = merged
        f = _dot(merged, wo_ref[...])
        f_ref[...] = f
        o_ref[...] = h_ref[...] + f * _rstd(f) * g_ref[...]

    def rows(width):
        return pl.BlockSpec((TM, width), lambda i: (i, 0))

    return pl.pallas_call(
        body, name=name, grid=(T // TM,),
        in_specs=[rows(3 * D), rows(o_a.shape[1]), rows(o_b.shape[1]), rows(o_m.shape[1]),
                  _resident(w_a), _resident(w_b), _resident(w_m), _resident(w_out), rows(D), _resident(g)],
        out_specs=[rows(D), rows(D), rows(D)],
        out_shape=[jax.ShapeDtypeStruct((T, D), BF16), jax.ShapeDtypeStruct((T, D), F32),
                   jax.ShapeDtypeStruct((T, D), F32)],
        compiler_params=_params("parallel"),
    )(gt, o_a, o_b, o_m, w_a, w_b, w_m, w_out, h_in, g)


def _band_rows(start, r):
    return pl.ds(start, HEAD) if r == 1 else pl.ds(start, HEAD, stride=r)


def _band_mask(max_dist, first_has_prev):
    row = lax.broadcasted_iota(jnp.int32, (HEAD, 2 * HEAD), 0)
    col = lax.broadcasted_iota(jnp.int32, (HEAD, 2 * HEAD), 1)
    dist = row + HEAD - col
    band = (dist >= 0) & (dist <= max_dist)
    return band, band & (col >= jnp.where(first_has_prev, 0, HEAD))


def _stack(parts):
    return parts[0] if len(parts) == 1 else jnp.concatenate(parts, axis=0)


def _band_specs(BT, SB, nsub, base, grp):
    stride = grp + 2

    def cur(off, width):
        return pl.BlockSpec((BT, width * HEAD), lambda h, i: (i, (base + h * stride + off) // width))

    def prev(off):
        return pl.BlockSpec((SB, HEAD), lambda h, i: (jnp.maximum(i * nsub - 1, 0), base + h * stride + off))

    return cur(0, grp), cur(grp, 1), prev(grp), cur(grp + 1, 1), prev(grp + 1)


def band_fwd(qkv, sinks, *, r, base, hkv, grp, max_dist, out_dtype, name, merge=None):
    T, W = qkv.shape
    SB = HEAD * r
    BT = min(2048, T)
    nsub, nib = BT // SB, T // BT
    hq = hkv * grp
    heads = [slice(g * HEAD, (g + 1) * HEAD) for g in range(grp)]
    others = [] if merge is None else [*merge[0], *merge[1]]

    def body(sink_ref, q_ref, kc_ref, kp_ref, vc_ref, vp_ref, *rest):
        joint_o, joint_l = rest[len(others):len(others) + 2]
        qf, kf, vf = rest[len(others) + 2:len(others) + 5]
        o_ref, l_ref = rest[len(others) + 5:] if others else (joint_o, joint_l)
        kvh, ib = pl.program_id(0), pl.program_id(1)
        qf[...] = q_ref[...].astype(F32)
        kf[:SB] = kp_ref[...].astype(F32)
        kf[SB:] = kc_ref[...].astype(F32)
        vf[:SB] = vp_ref[...].astype(F32)
        vf[SB:] = vc_ref[...].astype(F32)
        band, band_first = _band_mask(max_dist, ib > 0)
        for c in range(r):
            k_old, v_old = kf[_band_rows(c, r)], vf[_band_rows(c, r)]
            for j in range(nsub):
                mask = band_first if j == 0 else band
                rows = _band_rows(j * SB + c, r)
                k_own, v_own = kf[_band_rows((j + 1) * SB + c, r)], vf[_band_rows((j + 1) * SB + c, r)]
                kcat = jnp.concatenate([k_old, k_own], axis=0).astype(BF16)
                vcat = jnp.concatenate([v_old, v_own], axis=0).astype(BF16)
                k_old, v_old = k_own, v_own
                s_all = _dot_nt(_stack([qf[rows, cols] for cols in heads]).astype(BF16), kcat) * ATT_SCALE
                probs, tots = [], []
                for g, cols in enumerate(heads):
                    s = jnp.where(mask, s_all[cols], NEG_INF)
                    sk = sink_ref[kvh * grp + g]
                    m = jnp.maximum(jnp.max(s, axis=-1, keepdims=True), sk)
                    p = jnp.exp(s - m)
                    tot = jnp.sum(p, axis=-1, keepdims=True) + jnp.exp(sk - m)
                    probs.append(p.astype(BF16))
                    tots.append(tot)
                    l_ref[rows, cols] = jnp.broadcast_to(m + jnp.log(tot), (HEAD, HEAD))
                o_all = _dot(_stack(probs), vcat)
                for g, cols in enumerate(heads):
                    o_ref[rows, cols] = (o_all[cols] / tots[g]).astype(o_ref.dtype)

        if others:
            half = len(others) // 2
            outs = [ref[...] for ref in rest[:half]] + [o_ref[...]]
            logs = [ref[...] for ref in rest[half:len(others)]] + [l_ref[...]]
            top = functools.reduce(jnp.maximum, logs)
            weights = [jnp.exp(lg - top) for lg in logs]
            total = functools.reduce(jnp.add, weights)
            mixed = functools.reduce(jnp.add, [wgt * out for wgt, out in zip(weights, outs)])
            joint_o[...] = (mixed / total).astype(out_dtype)
            joint_l[...] = top + jnp.log(total)

    out_spec = pl.BlockSpec((BT, grp * HEAD), lambda h, i: (i, h))
    own = [pltpu.VMEM((BT, grp * HEAD), F32)] * 2 if others else []
    return pl.pallas_call(
        body, name=name, grid=(hkv, nib),
        in_specs=[pl.BlockSpec(memory_space=pltpu.SMEM), *_band_specs(BT, SB, nsub, base, grp)]
                 + [out_spec] * len(others),
        out_specs=[out_spec, out_spec],
        out_shape=[jax.ShapeDtypeStruct((T, hq * HEAD), out_dtype), jax.ShapeDtypeStruct((T, hq * HEAD), F32)],
        scratch_shapes=[pltpu.VMEM((BT, grp * HEAD), F32), pltpu.VMEM((SB + BT, HEAD), F32),
                        pltpu.VMEM((SB + BT, HEAD), F32)] + own,
        compiler_params=_params("parallel", "arbitrary"),
    )(sinks, qkv, qkv, qkv, qkv, qkv, *others)


def band_bwd(qkv, dqkv, do, o, lse, cos, sin_signed, sinks, *, r, base, hkv, grp, max_dist, name):
    T, W = qkv.shape
    SB = HEAD * r
    BT = min(max(2048, 2 * SB), T)
    nsub, nib = BT // SB, T // BT
    nblk = T // SB
    with_sink = sinks is not None
    heads = [slice(g * HEAD, (g + 1) * HEAD) for g in range(grp)]

    def body(*refs):
        if with_sink:
            sink_ref, refs = refs[0], refs[1:]
        (q_ref, kc_ref, kp_ref, vc_ref, vp_ref, qn_ref, do_ref, don_ref, o_ref, on_ref, l_ref, ln_ref,
         c_ref, s_ref, _) = refs[:15]
        out_ref = refs[15]
        ds_ref = refs[16] if with_sink else None
        qf, dof, of, kf, vf, dqf, dkf, dvf = refs[-8:]
        kvh, ib = pl.program_id(0), pl.program_id(1)
        for buf, cur_ref, nxt_ref in ((qf, q_ref, qn_ref), (dof, do_ref, don_ref), (of, o_ref, on_ref)):
            buf[:BT] = cur_ref[...].astype(F32)
            buf[BT:] = nxt_ref[...].astype(F32)
        kf[:SB] = kp_ref[...].astype(F32)
        kf[SB:] = kc_ref[...].astype(F32)
        vf[:SB] = vp_ref[...].astype(F32)
        vf[SB:] = vc_ref[...].astype(F32)
        band, band_first = _band_mask(max_dist, ib > 0)
        if with_sink:
            @pl.when(ib == 0)
            def _():
                ds_ref[...] = jnp.zeros_like(ds_ref)

        def grads(rows, logzs, keys, vals, mask):
            q = _stack([qf[rows, cols] for cols in heads]).astype(BF16)
            dout = _stack([dof[rows, cols] for cols in heads]).astype(BF16)
            s_all = _dot_nt(q, keys) * ATT_SCALE
            dp_all = _dot_nt(dout, vals)
            probs, dss, deltas = [], [], []
            for g, cols in enumerate(heads):
                delta = jnp.sum(dof[rows, cols] * of[rows, cols], axis=-1, keepdims=True)
                p = jnp.exp(jnp.where(mask, s_all[cols], NEG_INF) - logzs[g][:, :1])
                probs.append(p.astype(BF16))
                dss.append((p * (dp_all[cols] - delta) * ATT_SCALE).astype(BF16))
                deltas.append(delta)
            return q, dout, _stack(probs), _stack(dss), deltas

        row = lax.broadcasted_iota(jnp.int32, (HEAD, HEAD), 0)
        col = lax.broadcasted_iota(jnp.int32, (HEAD, HEAD), 1)
        reach = col >= row + jnp.where(ib < nib - 1, HEAD - max_dist, 2 * HEAD)
        for c in range(r):
            k_old, v_old = kf[_band_rows(c, r)], vf[_band_rows(c, r)]
            dk_own = dv_own = None
            for j in range(nsub):
                rows = _band_rows(j * SB + c, r)
                k_own, v_own = kf[_band_rows((j + 1) * SB + c, r)], vf[_band_rows((j + 1) * SB + c, r)]
                kcat = jnp.concatenate([k_old, k_own], axis=0).astype(BF16)
                vcat = jnp.concatenate([v_old, v_own], axis=0).astype(BF16)
                logzs = [l_ref[rows, cols] for cols in heads]
                q, dout, p, ds, deltas = grads(rows, logzs, kcat, vcat, band_first if j == 0 else band)
                dq = _dot(ds, kcat)
                for g, cols in enumerate(heads):
                    dqf[rows, cols] = dq[cols]
                    if with_sink:
                        p_sink = jnp.exp(sink_ref[kvh * grp + g] - logzs[g][:, :1])
                        ds_ref[g * 8:(g + 1) * 8] += jnp.sum(p_sink * deltas[g])
                dk, dv = _dot_tn(ds, q), _dot_tn(p, dout)
                if j > 0:
                    done = _band_rows((j - 1) * SB + c, r)
                    dkf[done] = dk_own + dk[:HEAD]
                    dvf[done] = dv_own + dv[:HEAD]
                dk_own, dv_own = dk[HEAD:], dv[HEAD:]
                k_old, v_old = k_own, v_own
            logzs = [ln_ref[_band_rows(c, r), cols] for cols in heads]
            q, dout, p, ds, _ = grads(_band_rows(BT + c, r), logzs, k_old.astype(BF16), v_old.astype(BF16), reach)
            done = _band_rows((nsub - 1) * SB + c, r)
            dkf[done] = dk_own + _dot_tn(ds, q)
            dvf[done] = dv_own + _dot_tn(p, dout)

        cs, sn = c_ref[...], s_ref[...]
        for cols in heads:
            out_ref[:, cols] = _unrope(dqf[:, cols], cs, sn).astype(BF16)
        out_ref[:, grp * HEAD:(grp + 1) * HEAD] = _unrope(dkf[...], cs, sn).astype(BF16)
        out_ref[:, (grp + 1) * HEAD:] = dvf[...].astype(BF16)

    def nxt_row(i):
        return jnp.minimum((i + 1) * nsub, nblk - 1)

    stride = grp + 2
    q_next = pl.BlockSpec((SB, grp * HEAD), lambda h, i: (nxt_row(i), (base + h * stride) // grp))
    head_cur = pl.BlockSpec((BT, grp * HEAD), lambda h, i: (i, h))
    head_next = pl.BlockSpec((SB, grp * HEAD), lambda h, i: (nxt_row(i), h))
    table = pl.BlockSpec((BT, HEAD), lambda h, i: (i, 0))

    in_specs = [*_band_specs(BT, SB, nsub, base, grp), q_next,
                head_cur, head_next, head_cur, head_next, head_cur, head_next, table, table, UNREAD]
    args = [qkv, qkv, qkv, qkv, qkv, qkv, do, do, o, o, lse, lse, cos, sin_signed, dqkv]
    out_specs = [pl.BlockSpec((BT, stride * HEAD), lambda h, i: (i, base // stride + h))]
    out_shape = [jax.ShapeDtypeStruct(dqkv.shape, dqkv.dtype)]
    if with_sink:
        in_specs.insert(0, pl.BlockSpec(memory_space=pltpu.SMEM))
        args.insert(0, sinks)
        out_specs.append(pl.BlockSpec((None, grp * 8, HEAD), lambda h, i: (h, 0, 0)))
        out_shape.append(jax.ShapeDtypeStruct((hkv, grp * 8, HEAD), F32))
    wide = pltpu.VMEM((BT + SB, grp * HEAD), F32)
    tall = pltpu.VMEM((SB + BT, HEAD), F32)
    grad = pltpu.VMEM((BT, HEAD), F32)
    return pl.pallas_call(
        body, name=name, grid=(hkv, nib), in_specs=in_specs, out_specs=out_specs, out_shape=out_shape,
        input_output_aliases={len(args) - 1: 0},
        scratch_shapes=[wide, wide, wide, tall, tall, pltpu.VMEM((BT, grp * HEAD), F32), grad, grad],
        compiler_params=pltpu.CompilerParams(dimension_semantics=("parallel", "arbitrary"),
                                             vmem_limit_bytes=VMEM_LIMIT_LARGE),
    )(*args)


M_HEADS = 4


def mem_kv(mem, g, w, name):
    n, D = mem.shape

    def body(m_ref, g_ref, w_ref, mn_ref, kv_ref):
        x = m_ref[...]
        mn = (x * _rstd(x) * g_ref[...]).astype(BF16)
        mn_ref[...] = mn
        kv_ref[...] = _dot(mn, w_ref[...]).astype(BF16)

    return pl.pallas_call(
        body, name=name,
        out_shape=[jax.ShapeDtypeStruct((n, D), BF16), jax.ShapeDtypeStruct((n, w.shape[1]), BF16)],
        compiler_params=pltpu.CompilerParams(vmem_limit_bytes=VMEM_LIMIT),
    )(mem, g, w)


def mem_fwd(qkv, mkv, name):
    T = qkv.shape[0]
    n = mkv.shape[0]
    RB = 1024

    def body(q_ref, kv_ref, o_ref, l_ref):
        for h in range(M_HEADS):
            cols = slice(h * HEAD, (h + 1) * HEAD)
            s = _dot_nt(q_ref[:, cols], kv_ref[:, cols]) * ATT_SCALE
            m = jnp.max(s, axis=-1, keepdims=True)
            p = jnp.exp(s - m)
            den = jnp.sum(p, axis=-1, keepdims=True)
            vals = kv_ref[:, (M_HEADS + h) * HEAD:(M_HEADS + h + 1) * HEAD]
            o_ref[:, cols] = (_dot(p.astype(BF16), vals) / den).astype(BF16)
            l_ref[:, cols] = jnp.broadcast_to(m + jnp.log(den), (RB, HEAD))

    out = pl.BlockSpec((RB, M_HEADS * HEAD), lambda i: (i, 0))
    return pl.pallas_call(
        body, name=name, grid=(T // RB,),
        in_specs=[pl.BlockSpec((RB, M_HEADS * HEAD), lambda i: (i, MQ // M_HEADS)), _resident(mkv)],
        out_specs=[out, out],
        out_shape=[jax.ShapeDtypeStruct((T, M_HEADS * HEAD), BF16), jax.ShapeDtypeStruct((T, M_HEADS * HEAD), F32)],
        compiler_params=_params("parallel"),
    )(qkv, mkv)


def mem_bwd(qkv, dqkv, mkv, do, o, lse, name):
    T = qkv.shape[0]
    n = mkv.shape[0]
    RB = 1024

    def body(q_ref, kv_ref, do_ref, o_ref, l_ref, _, dq_ref, dk_ref, dv_ref):
        @pl.when(pl.program_id(0) == 0)
        def _():
            dk_ref[...] = jnp.zeros_like(dk_ref)
            dv_ref[...] = jnp.zeros_like(dv_ref)

        for h in range(M_HEADS):
            cols = slice(h * HEAD, (h + 1) * HEAD)
            keys, vals = kv_ref[:, cols], kv_ref[:, (M_HEADS + h) * HEAD:(M_HEADS + h + 1) * HEAD]
            q, dout = q_ref[:, cols], do_ref[:, cols]
            delta = jnp.sum(dout.astype(F32) * o_ref[:, cols].astype(F32), axis=-1, keepdims=True)
            p = jnp.exp(_dot_nt(q, keys) * ATT_SCALE - l_ref[:, cols][:, :1])
            ds = (p * (_dot_nt(dout, vals) - delta) * ATT_SCALE).astype(BF16)
            dq_ref[:, cols] = _dot(ds, keys).astype(BF16)
            dk_ref[:, cols] += _dot_tn(ds, q)
            dv_ref[:, cols] += _dot_tn(p.astype(BF16), dout)

    wide = M_HEADS * HEAD
    tok = pl.BlockSpec((RB, wide), lambda i: (i, 0))
    q_cols = pl.BlockSpec((RB, wide), lambda i: (i, MQ // M_HEADS))
    slot = pl.BlockSpec((n, wide), lambda i: (0, 0))
    return pl.pallas_call(
        body, name=name, grid=(T // RB,),
        in_specs=[q_cols, _resident(mkv), tok, tok, tok, UNREAD],
        out_specs=[q_cols, slot, slot],
        out_shape=[jax.ShapeDtypeStruct(dqkv.shape, dqkv.dtype),
                   jax.ShapeDtypeStruct((n, wide), F32), jax.ShapeDtypeStruct((n, wide), F32)],
        input_output_aliases={5: 0},
        compiler_params=_params("arbitrary"),
    )(qkv, mkv, do, o, lse, dqkv)


def mem_kv_bwd(mem, g, mem_n, w, dmkv, name):
    n, D = mem.shape

    def body(m_ref, g_ref, mn_ref, w_ref, d_ref, dw_ref, dg_ref):
        d = d_ref[...].astype(BF16)
        dw_ref[...] = _dot_tn(mn_ref[...], d)
        x = m_ref[...]
        dg_ref[...] = jnp.sum(_dot_nt(d, w_ref[...]) * (x * _rstd(x)), axis=0, keepdims=True)

    return pl.pallas_call(
        body, name=name,
        out_shape=[jax.ShapeDtypeStruct(w.shape, F32), jax.ShapeDtypeStruct((1, D), F32)],
        compiler_params=pltpu.CompilerParams(vmem_limit_bytes=VMEM_LIMIT),
    )(mem, g, mem_n, w, dmkv)


def _rms_bwd(dn, f, g):
    r = _rstd(f)
    fhat = f * r
    dfhat = dn * g
    df = r * (dfhat - fhat * jnp.mean(dfhat * fhat, axis=-1, keepdims=True))
    return df, jnp.sum(dn * fhat, axis=0, keepdims=True)


def ffn_tokens_bwd(dh, f, h_in, gu, g_pre, g_post, w_in, w_out, coef, name, after):
    T, D = dh.shape

    def body(dh_ref, f_ref, h_ref, gu_ref, gpre_ref, gpost_ref, win_ref, wout_ref, _,
             df_ref, dgu_ref, dhin_ref, dgpre_ref, dgpost_ref, dxn_ref):
        i, j = pl.program_id(0), pl.program_id(1)

        @pl.when(j == 0)
        def _():
            @pl.when(i == 0)
            def _():
                dgpre_ref[...] = jnp.zeros_like(dgpre_ref)
                dgpost_ref[...] = jnp.zeros_like(dgpost_ref)

            df, dg_post = _rms_bwd(coef * dh_ref[...], f_ref[...], gpost_ref[...])
            dgpost_ref[...] += dg_post
            df_ref[...] = df.astype(BF16)

        for jj in range(2):
            @pl.when(j == jj)
            def _(jj=jj):
                lo, mid, hi = 2 * jj * FF_T, (2 * jj + 1) * FF_T, (2 * jj + 2) * FF_T
                da = _dot_nt(df_ref[...], wout_ref[jj * FF_T:(jj + 1) * FF_T, :])
                gate = gu_ref[:, :FF_T].astype(F32)
                up = gu_ref[:, FF_T:].astype(F32)
                sig = _sigmoid(gate)
                dgate = (da * up * sig * (1.0 + gate * (1.0 - sig))).astype(BF16)
                dup = (da * gate * sig).astype(BF16)
                dgu_ref[:, :FF_T] = dgate
                dgu_ref[:, FF_T:] = dup
                part = _dot_nt(dgate, win_ref[:, lo:mid]) + _dot_nt(dup, win_ref[:, mid:hi])
                if jj == 0:
                    dxn_ref[...] = part
                else:
                    h = h_ref[...]
                    r = _rstd(h)
                    xhat = h * r
                    dxn = dxn_ref[...] + part
                    dxhat = dxn * gpre_ref[...]
                    dhin_ref[...] = dh_ref[...] + r * (dxhat - xhat * jnp.mean(dxhat * xhat, axis=-1, keepdims=True))
                    dgpre_ref[...] += jnp.sum(dxn * xhat, axis=0, keepdims=True)

    row = pl.BlockSpec((TM, D), lambda i, j: (i, 0))
    wide = pl.BlockSpec((TM, 2 * FF_T), lambda i, j: (i, j))
    vec = pl.BlockSpec((1, D), lambda i, j: (0, 0))
    return pl.pallas_call(
        body, name=name, grid=(T // TM, 2),
        in_specs=[row, row, row, wide, _resident(g_pre), _resident(g_post), _resident(w_in), _resident(w_out),
                  UNREAD],
        out_specs=[row, wide, row, vec, vec],
        out_shape=[jax.ShapeDtypeStruct((T, D), BF16), jax.ShapeDtypeStruct((T, 2 * D_FF), BF16),
                   jax.ShapeDtypeStruct((T, D), F32), jax.ShapeDtypeStruct((1, D), F32),
                   jax.ShapeDtypeStruct((1, D), F32)],
        scratch_shapes=[pltpu.VMEM((TM, D), F32)],
        compiler_params=pltpu.CompilerParams(dimension_semantics=("arbitrary", "arbitrary"),
                                             vmem_limit_bytes=VMEM_LIMIT_LARGE),
    )(dh, f, h_in, gu, g_pre, g_post, w_in, w_out, after)


def mm_nt_norm_bwd(pieces, h_in, dh_out, g, name, after):
    T, D = h_in.shape

    def body(*refs):
        ab = refs[:2 * len(pieces)]
        h_ref, dh_ref, g_ref, _, o_ref, dg_ref = refs[2 * len(pieces):]
        dxn = _dot_nt(ab[0][...], ab[1][...])
        for p in range(1, len(pieces)):
            dxn += _dot_nt(ab[2 * p][...], ab[2 * p + 1][...])
        h = h_ref[...]
        r = _rstd(h)
        xhat = h * r
        dxhat = dxn * g_ref[...]
        o_ref[...] = dh_ref[...] + r * (dxhat - xhat * jnp.mean(dxhat * xhat, axis=-1, keepdims=True))

        @pl.when(pl.program_id(0) == 0)
        def _():
            dg_ref[...] = jnp.zeros_like(dg_ref)

        dg_ref[...] += jnp.sum(dxn * xhat, axis=0, keepdims=True)

    in_specs, args = [], []
    for a, w in pieces:
        in_specs += [pl.BlockSpec((TM, a.shape[1]), lambda i: (i, 0)), _resident(w)]
        args += [a, w]
    row = pl.BlockSpec((TM, D), lambda i: (i, 0))
    return pl.pallas_call(
        body, name=name, grid=(T // TM,),
        in_specs=in_specs + [row, row, _resident(g), UNREAD],
        out_specs=[row, pl.BlockSpec((1, D), lambda i: (0, 0))],
        out_shape=[jax.ShapeDtypeStruct((T, D), F32), jax.ShapeDtypeStruct((1, D), F32)],
        compiler_params=_params("arbitrary"),
    )(*args, h_in, dh_out, g, after)


def gate_merge_out_bwd(dh, f, g, w_out, merged, gt, o_a, o_b, o_m, w_a, w_b, w_m, name, after):
    T = dh.shape[0]
    D = D_MODEL
    branch = ((o_a, w_a), (o_b, w_b), (o_m, w_m))

    def body(dh_ref, f_ref, g_ref, wo_ref, m_ref, gt_ref, oa_ref, ob_ref, om_ref, wa_ref, wb_ref, wm_ref, _,
             dg_ref, dwo_ref, dgt_ref, doa_ref, dob_ref, dom_ref, db_ref, dwa_ref, dwb_ref, dwm_ref):
        @pl.when(pl.program_id(0) == 0)
        def _():
            for acc in (dg_ref, dwo_ref, db_ref, dwa_ref, dwb_ref, dwm_ref):
                acc[...] = jnp.zeros_like(acc)

        df, dg = _rms_bwd(dh_ref[...], f_ref[...], g_ref[...])
        dg_ref[...] += dg
        df = df.astype(BF16)
        dwo_ref[...] += _dot_tn(m_ref[...], df)
        dmf = _dot_nt(df, wo_ref[...])
        for x, (o_ref, w_ref, do_ref, dw_ref) in enumerate(((oa_ref, wa_ref, doa_ref, dwa_ref),
                                                           (ob_ref, wb_ref, dob_ref, dwb_ref),
                                                           (om_ref, wm_ref, dom_ref, dwm_ref))):
            cols = slice(x * D, (x + 1) * D)
            gx = gt_ref[:, cols].astype(F32)
            w = w_ref[...]
            dpre = dmf * _dot(o_ref[...], w) * gx * (1.0 - gx)
            dgt_ref[:, cols] = dpre.astype(BF16)
            db_ref[:, cols] += jnp.sum(dpre, axis=0, keepdims=True)
            dp = (dmf * gx).astype(BF16)
            do_ref[...] = _dot_nt(dp, w).astype(BF16)
            dw_ref[...] += _dot_tn(dp, o_ref[...])

    def rows(width):
        return pl.BlockSpec((TM, width), lambda i: (i, 0))

    def kept(shape):
        return pl.BlockSpec(shape, lambda i: (0,) * len(shape))

    widths = [o.shape[1] for o, _ in branch]
    sums = [(1, D), (D, D), (1, 3 * D)] + [(D, k) for k in widths]
    return pl.pallas_call(
        body, name=name, grid=(T // TM,),
        in_specs=[rows(D), rows(D), _resident(g), _resident(w_out), rows(D), rows(3 * D)]
                 + [rows(k) for k in widths] + [_resident(w) for _, w in branch] + [UNREAD],
        out_specs=[kept(sums[0]), kept(sums[1]), rows(3 * D)] + [rows(k) for k in widths]
                  + [kept(shape) for shape in sums[2:]],
        out_shape=[jax.ShapeDtypeStruct(sums[0], F32), jax.ShapeDtypeStruct(sums[1], F32),
                   jax.ShapeDtypeStruct((T, 3 * D), BF16)] + [jax.ShapeDtypeStruct((T, k), BF16) for k in widths]
                  + [jax.ShapeDtypeStruct(shape, F32) for shape in sums[2:]],
        compiler_params=pltpu.CompilerParams(dimension_semantics=("arbitrary",), vmem_limit_bytes=VMEM_LIMIT_LARGE),
    )(dh, f, g, w_out, merged, gt, o_a, o_b, o_m, w_a, w_b, w_m, after)


def mm_tn(x, dy, tm, tn, name, shard_major=False, perm=None, slabs=1, after=None, wire=False):
    T, M = x.shape
    N = dy.shape[1]
    tk = min(2048, T)
    perm = perm or (lambda j: j)
    w = tn // slabs

    def body(x_ref, dy_ref, *rest):
        o_ref = rest[-2] if wire else rest[-1]

        @pl.when(pl.program_id(2) == 0)
        def _():
            o_ref[...] = jnp.zeros_like(o_ref)

        acc = _dot_tn(x_ref[...], dy_ref[...])
        if shard_major:
            for s in range(slabs):
                o_ref[s] += acc[:, s * w:(s + 1) * w]
        else:
            o_ref[...] += acc
        if wire:
            @pl.when(pl.program_id(2) == T // tk - 1)
            def _():
                rest[-1][...] = o_ref[...].astype(BF16)

    if shard_major:
        out_spec = pl.BlockSpec((slabs, tm, w), lambda i, j, k: (perm(j), i, 0))
        out_shape = jax.ShapeDtypeStruct((N // w, M, w), F32)
    else:
        out_spec = pl.BlockSpec((tm, tn), lambda i, j, k: (i, j))
        out_shape = jax.ShapeDtypeStruct((M, N), F32)
    return pl.pallas_call(
        body, name=name, grid=(M // tm, N // tn, T // tk),
        in_specs=[pl.BlockSpec((tk, tm), lambda i, j, k: (k, i)),
                  pl.BlockSpec((tk, tn), lambda i, j, k: (k, j))] + ([] if after is None else [UNREAD]),
        out_specs=[out_spec, out_spec] if wire else out_spec,
        out_shape=[out_shape, jax.ShapeDtypeStruct(out_shape.shape, BF16)] if wire else out_shape,
        compiler_params=_params("parallel", "parallel", "arbitrary"),
    )(x, dy, *([] if after is None else [after]))


def rope_tables(T, zero):
    half = HEAD // 2
    inv = ROPE_THETA ** (-jnp.arange(half, dtype=F32) / half)
    ang = (jnp.arange(T).astype(F32) + zero)[:, None] * inv[None, :]
    cos, sin = jnp.cos(ang), jnp.sin(ang)
    return jnp.concatenate([cos, cos], axis=1), jnp.concatenate([-sin, sin], axis=1)


def layer_step(x, mem, target, gains, sinks, b_gate, weights_of, send_grads, zero):
    T = x.shape[0]
    cos, sin_signed = rope_tables(T, zero)
    no_sink = jnp.full((2,), NEG_INF, F32)

    xn1 = rms_scale(x, gains["ffn1_norm_pre"], "ffn1_norm", cos)
    w = dict(weights_of("ffn1_in", xn1))
    _, gu1, a1 = ffn_in(x, gains["ffn1_norm_pre"], w["ffn1_w_in"], "ffn1_in", xn=xn1)
    w.update(weights_of("ffn1_out", a1))
    f1, h1 = mm_norm_res(a1, w["ffn1_w_out"], x, gains["ffn1_norm_post"], 0.5, "ffn1_out")
    w.update(weights_of("mix_in", f1))
    u, qkv, gt = mix_in(h1, gains["mix_norm_pre"], w["w_in"], w["w_gate"], b_gate, cos, sin_signed, "mix_in")
    w.update(weights_of("mix_rest", u))
    outs, lses = [], []
    for gidx, (window, dil) in enumerate(DIL):
        last = gidx == len(DIL) - 1
        o_g, l_g = band_fwd(qkv, no_sink, r=dil, base=A_BASE + 6 * gidx, hkv=2, grp=1, max_dist=window // dil,
                            out_dtype=BF16 if last else F32, name=f"attn_a{gidx}_fwd",
                            merge=(outs, lses) if last else None)
        outs.append(o_g)
        lses.append(l_g)
    o_a, l_a = outs[-1], lses[-1]
    o_b, l_b = band_fwd(qkv, sinks, r=1, base=B_BASE, hkv=2, grp=2, max_dist=HEAD - 1, out_dtype=BF16,
                        name="attn_b_fwd")
    mem_n, mkv = mem_kv(mem, gains["mem_norm"], w["w_mem_kv"], "mem_kv")
    o_m, l_m = mem_fwd(qkv, mkv, "attn_m_fwd")
    merged, mo, h2 = gate_merge_out(gt, o_a, o_b, o_m, w["w_o_a"], w["w_o_b"], w["w_o_m"], w["w_out"], h1,
                                    gains["mix_norm_post"], "gate_merge_out")
    w.update(weights_of("ffn2", mo))
    xn2, gu2, a2 = ffn_in(h2, gains["ffn2_norm_pre"], w["ffn2_w_in"], "ffn2_in")
    f2, dy, sq = mm_norm_res(a2, w["ffn2_w_out"], h2, gains["ffn2_norm_post"], 0.5, "ffn2_out", target=target)

    grads = {}

    def ffn_bwd(tag, dh_out, f, gu, a, xn, h_in, after):
        df, dgu, dh_in, grads[f"{tag}_norm_pre"], grads[f"{tag}_norm_post"] = ffn_tokens_bwd(
            dh_out, f, h_in, gu, gains[f"{tag}_norm_pre"], gains[f"{tag}_norm_post"], w[f"{tag}_w_in"],
            w[f"{tag}_w_out"], 0.5, f"{tag}_tokens_bwd", after)
        sent = send_grads(f"{tag}_in", {f"{tag}_w_in": mm_tn(
            xn, dgu, D_MODEL, FF_T, f"{tag}_w_in_grad", shard_major=True, perm=_ffn_perm, wire=True)})
        sent = send_grads(f"{tag}_out", {f"{tag}_w_out": mm_tn(
            a, df, FF_T, D_MODEL, f"{tag}_w_out_grad", after=sent, wire=True)})
        return dh_in, sent

    dh2, sent = ffn_bwd("ffn2", dy, f2, gu2, a2, xn2, h2, dy)

    mix = {}
    (grads["mix_norm_post"], mix["w_out"], dgt, do_a, do_b, do_m, grads["b_gate"],
     dwa_t, dwb_t, dwm_t) = gate_merge_out_bwd(
        dh2, mo, gains["mix_norm_post"], w["w_out"], merged, gt, o_a, o_b, o_m, w["w_o_a"], w["w_o_b"],
        w["w_o_m"], "gate_merge_out_bwd", sent)
    mix["w_o_a"], mix["w_o_b"], mix["w_o_m"] = dwa_t.T, dwb_t.T, dwm_t.T

    dqkv = lax.empty(qkv.shape, qkv.dtype)
    for gidx, (window, dil) in enumerate(DIL):
        dqkv, = band_bwd(qkv, dqkv, do_a, o_a, l_a, cos, sin_signed, None, r=dil, base=A_BASE + 6 * gidx, hkv=2,
                         grp=1, max_dist=window // dil, name=f"attn_a{gidx}_bwd")
    dqkv, dsink = band_bwd(qkv, dqkv, do_b, o_b, l_b, cos, sin_signed, sinks, r=1, base=B_BASE, hkv=2, grp=2,
                           max_dist=HEAD - 1, name="attn_b_bwd")
    grads["sinks"] = -dsink[:, ::8, 0].reshape(1, 4)
    dqkv, dmk, dmv = mem_bwd(qkv, dqkv, mkv, do_m, o_m, l_m, "attn_m_bwd")
    mix["w_mem_kv"], grads["mem_norm"] = mem_kv_bwd(
        mem, gains["mem_norm"], mem_n, w["w_mem_kv"], jnp.concatenate([dmk, dmv], axis=1), "mem_kv_bwd")

    mix["w_in"] = mm_tn(u, dqkv, D_MODEL, 1280, "w_in_grad")
    mix["w_gate"] = mm_tn(u, dgt, D_MODEL, 1536, "w_gate_grad", shard_major=True, slabs=2, wire=True)
    sent = send_grads("mix", mix)
    dh1, grads["mix_norm_pre"] = mm_nt_norm_bwd(
        [(dqkv, w["w_in"]), (dgt, w["w_gate"])], h1, dh2, gains["mix_norm_pre"], "mix_in_bwd", sent)

    dx, _ = ffn_bwd("ffn1", dh1, f1, gu1, a1, xn1, x, dh1)
    return sq, dx, grads


def _place():
    return lax.axis_index("x"), lax.axis_index("y"), lax.axis_index("c")


def _other_chips(x, y):
    return [(1 - x, y), (x, 1 - y), (1 - x, 1 - y)]


def _hbm(n):
    return [pl.BlockSpec(memory_space=pltpu.HBM)] * n


SEM = pl.BlockSpec(memory_space=pltpu.SEMAPHORE)
SIDE_EFFECT = pltpu.SideEffectType.DATAFLOW_SIDE_EFFECTING


def _chip_copy(src, land, sems, i, j, dst_slot, scatter):
    x, y, c = _place()
    px, py = _other_chips(x, y)[j]
    send_sems, recv_sems = sems
    return pltpu.make_async_remote_copy(
        src_ref=src[i].at[2 * px + py] if scatter else src[i], dst_ref=land[i].at[dst_slot],
        send_sem=send_sems.at[3 * i + j], recv_sem=recv_sems.at[3 * i + j],
        device_id=(px, py, c), device_id_type=MESH)


def chip_copies_start(srcs, lands, groups, scatter, name, after=None):
    n = len(srcs)

    def body(*refs):
        src, land = refs[:n], refs[n:2 * n]
        first_sem = 2 * n + (after is not None)
        sems = refs[first_sem:first_sem + 2 * len(groups)]
        token = refs[-1]
        x, y, _ = _place()
        for g, members in enumerate(groups):
            part = ([src[i] for i in members], [land[i] for i in members])
            for t in range(len(members)):
                for j in range(3):
                    _chip_copy(*part, sems[2 * g:2 * g + 2], t, j, 2 * x + y, scatter).start()
        token[...] = jnp.zeros_like(token)

    sem_shapes = [pltpu.SemaphoreType.DMA((3 * len(m),)) for m in groups for _ in range(2)]
    thru = [pltpu.HBM(a.shape, a.dtype) for a in (*srcs, *lands)]
    res = pl.pallas_call(
        body, name=name,
        out_shape=(*sem_shapes, *thru, jax.ShapeDtypeStruct((8, 128), F32)),
        in_specs=_hbm(2 * n) + ([] if after is None else [UNREAD]),
        out_specs=(*[SEM] * len(sem_shapes), *_hbm(2 * n), pl.BlockSpec(memory_space=pltpu.VMEM)),
        input_output_aliases={i: len(sem_shapes) + i for i in range(2 * n)},
        compiler_params=pltpu.CompilerParams(has_side_effects=SIDE_EFFECT),
    )(*[pltpu.with_memory_space_constraint(a, pltpu.HBM) for a in (*srcs, *lands)],
      *([] if after is None else [after]))
    k = len(sem_shapes)
    sems = [tuple(res[2 * g:2 * g + 2]) for g in range(len(groups))]
    return sems, list(res[k:k + n]), list(res[k + n:k + 2 * n]), res[-1]


def chip_copies_wait(srcs, lands, sems, after, scatter, name):
    n = len(srcs)
    after = list(after) if isinstance(after, (list, tuple)) else [after]

    def body(*refs):
        src, land = refs[:n], refs[n:2 * n]
        pair = refs[2 * n:2 * n + 2]
        x, y, _ = _place()
        for i in range(n):
            for j, (px, py) in enumerate(_other_chips(x, y)):
                copy = _chip_copy(src, land, pair, i, j, 2 * px + py, scatter)
                copy.wait_send()
                copy.wait_recv()

    res = pl.pallas_call(
        body, name=name,
        out_shape=[pltpu.HBM(a.shape, a.dtype) for a in (*srcs, *lands)],
        in_specs=[*_hbm(2 * n), SEM, SEM] + [UNREAD] * len(after),
        out_specs=_hbm(2 * n),
        input_output_aliases={i: i for i in range(2 * n)},
        compiler_params=pltpu.CompilerParams(has_side_effects=SIDE_EFFECT),
    )(*srcs, *lands, *sems, *after)
    return list(res[n:])


def small_all_gather(small, name):
    flips = [(fx, fy, fc) for fx in (0, 1) for fy in (0, 1) for fc in (0, 1)][1:]

    def body(in_ref, out_ref, send_sems, recv_sems, local_sem):
        x, y, c = _place()
        me = 4 * x + 2 * y + c

        def copy(k, slot):
            fx, fy, fc = flips[k]
            return pltpu.make_async_remote_copy(
                src_ref=in_ref, dst_ref=out_ref.at[slot], send_sem=send_sems.at[k], recv_sem=recv_sems.at[k],
                device_id=(x ^ fx, y ^ fy, c ^ fc), device_id_type=MESH)

        local = pltpu.make_async_copy(in_ref, out_ref.at[me], local_sem)
        local.start()
        for k in range(len(flips)):
            copy(k, me).start()
        for k, (fx, fy, fc) in enumerate(flips):
            copy(k, 4 * (x ^ fx) + 2 * (y ^ fy) + (c ^ fc)).wait()
        local.wait()

    return pl.pallas_call(
        body, name=name, in_specs=_hbm(1), out_specs=_hbm(1)[0],
        out_shape=jax.ShapeDtypeStruct((N_DEV,) + small.shape, small.dtype),
        scratch_shapes=[pltpu.SemaphoreType.DMA((len(flips),)), pltpu.SemaphoreType.DMA((len(flips),)),
                        pltpu.SemaphoreType.DMA],
    )(small)


def _sibling_copy(src, land, sems, i):
    x, y, c = _place()
    return pltpu.make_async_remote_copy(
        src_ref=src[i], dst_ref=land[i], send_sem=sems[0].at[i], recv_sem=sems[1].at[i],
        device_id=(x, y, 1 - c), device_id_type=MESH)


def sibling_copies_start(parts, name):
    n = len(parts)
    lands = [lax.empty(p.shape, p.dtype) for p in parts]

    def body(*refs):
        src, land, sems, token = refs[:n], refs[n:2 * n], refs[2 * n:2 * n + 2], refs[-1]
        for i in range(n):
            _sibling_copy(src, land, sems, i).start()
        token[...] = jnp.zeros_like(token)

    res = pl.pallas_call(
        body, name=name,
        out_shape=(pltpu.SemaphoreType.DMA((n,)), pltpu.SemaphoreType.DMA((n,)),
                   *[pltpu.HBM(a.shape, a.dtype) for a in (*parts, *lands)], jax.ShapeDtypeStruct((8, 128), F32)),
        in_specs=_hbm(2 * n),
        out_specs=(SEM, SEM, *_hbm(2 * n), pl.BlockSpec(memory_space=pltpu.VMEM)),
        input_output_aliases={i: 2 + i for i in range(2 * n)},
        compiler_params=pltpu.CompilerParams(has_side_effects=SIDE_EFFECT),
    )(*[pltpu.with_memory_space_constraint(a, pltpu.HBM) for a in (*parts, *lands)])
    return tuple(res[:2]), list(res[2:2 + n]), list(res[2 + n:2 + 2 * n]), res[-1]


def sibling_copies_wait(parts, lands, sems, after, name):
    n = len(parts)

    def body(*refs):
        src, land, sems = refs[:n], refs[n:2 * n], refs[2 * n:2 * n + 2]
        for i in range(n):
            copy = _sibling_copy(src, land, sems, i)
            copy.wait_send()
            copy.wait_recv()

    res = pl.pallas_call(
        body, name=name,
        out_shape=[pltpu.HBM(a.shape, a.dtype) for a in (*parts, *lands)],
        in_specs=[*_hbm(2 * n), SEM, SEM, UNREAD],
        out_specs=_hbm(2 * n),
        input_output_aliases={i: i for i in range(2 * n)},
        compiler_params=pltpu.CompilerParams(has_side_effects=SIDE_EFFECT),
    )(*parts, *lands, *sems, after)
    return list(res[n:])


def _row_tile(rows):
    for t in (256, 176, 128, 64, 32, 16, 8):
        if rows % t == 0:
            return t
    return rows


def chip_partial_sum(me, own_sm, recv, name):
    _, rows, cols = own_sm.shape
    tr = _row_tile(rows)

    def body(me_ref, own_ref, r1, r2, r3, o_ref):
        o_ref[...] = own_ref[...] + r1[...].astype(F32) + r2[...].astype(F32) + r3[...].astype(F32)

    def slot(d):
        return pl.BlockSpec((None, tr, cols), lambda i, me_ref: ((me_ref[0] + d) % N_CHIPS, i, 0))

    return pl.pallas_call(
        body, name=name,
        grid_spec=pltpu.PrefetchScalarGridSpec(
            num_scalar_prefetch=1, grid=(rows // tr,),
            in_specs=[slot(0), slot(1), slot(2), slot(3)],
            out_specs=pl.BlockSpec((tr, cols), lambda i, me_ref: (i, 0))),
        out_shape=jax.ShapeDtypeStruct((rows, cols), F32),
        compiler_params=_params("parallel"),
    )(me, own_sm, recv, recv, recv)


def _adamw(w, g, m, v):
    m = ADAM_B1 * m + (1.0 - ADAM_B1) * g
    v = ADAM_B2 * v + (1.0 - ADAM_B2) * (g * g)
    m_hat = m / (1.0 - ADAM_B1 ** ADAM_STEP)
    v_hat = v / (1.0 - ADAM_B2 ** ADAM_STEP)
    delta = -ADAM_LR * (m_hat / (jnp.sqrt(v_hat) + ADAM_EPS) + ADAM_WD * w)
    return delta, m, v


def adamw_pair(part, sib, w, m, v, name):
    rows, cols = w.shape
    tr = _row_tile(rows)

    def body(p_ref, s_ref, w_ref, m_ref, v_ref, g_ref, d_ref, nm_ref, nv_ref):
        g = p_ref[...] + s_ref[...]
        g_ref[...] = g
        d_ref[...], nm_ref[...], nv_ref[...] = _adamw(w_ref[...], g, m_ref[...], v_ref[...])

    spec = pl.BlockSpec((tr, cols), lambda i: (i, 0))
    return pl.pallas_call(
        body, name=name, grid=(rows // tr,), in_specs=[spec] * 5, out_specs=[spec] * 4,
        out_shape=[jax.ShapeDtypeStruct((rows, cols), F32)] * 4,
        compiler_params=_params("parallel"),
    )(part, sib, w, m, v)


def adamw_small(g_all, w, m, v, name):
    def body(ga_ref, w_ref, m_ref, v_ref, g_ref, d_ref, nm_ref, nv_ref):
        g = ga_ref[0]
        for k in range(1, N_DEV):
            g = g + ga_ref[k]
        g_ref[...] = g
        d_ref[...], nm_ref[...], nv_ref[...] = _adamw(w_ref[...], g, m_ref[...], v_ref[...])

    return pl.pallas_call(
        body, name=name, out_shape=[jax.ShapeDtypeStruct(w.shape, F32)] * 4,
    )(g_all, w, m, v)


WEIGHTS = ("ffn1_norm_pre", "ffn1_w_in", "ffn1_w_out", "ffn1_norm_post", "mix_norm_pre", "w_in", "sinks",
           "mem_norm", "w_mem_kv", "w_gate", "b_gate", "w_o_a", "w_o_b", "w_o_m", "w_out", "mix_norm_post",
           "ffn2_norm_pre", "ffn2_w_in", "ffn2_w_out", "ffn2_norm_post")
GATHER_STAGES = (("ffn1_in", "ffn1_out"), ("mix_in",), ("mix_rest", "ffn2"))
GATHER_GROUPS = {"ffn1_in": ("ffn1_w_in",), "ffn1_out": ("ffn1_w_out",),
                 "mix_in": ("w_in", "w_gate"), "mix_rest": ("w_mem_kv", "w_o_a", "w_o_b", "w_o_m", "w_out"),
                 "ffn2": ("ffn2_w_in", "ffn2_w_out")}
GROUPS = {"ffn1_in": ("ffn1_w_in",), "ffn1_out": ("ffn1_w_out",),
          "mix": ("w_in", "w_gate", "w_mem_kv", "w_o_a", "w_o_b", "w_o_m", "w_out"),
          "ffn2_in": ("ffn2_w_in",), "ffn2_out": ("ffn2_w_out",)}
COLUMN_SHARDED = ("ffn1_w_in", "ffn2_w_in", "w_in", "w_gate", "w_o_a", "w_o_b", "w_o_m")
KEPT_SHARD_MAJOR = ("ffn1_w_in", "ffn2_w_in", "w_gate")
GAINS = ("ffn1_norm_pre", "ffn1_norm_post", "mix_norm_pre", "mem_norm", "mix_norm_post", "ffn2_norm_pre",
         "ffn2_norm_post")
SMALL_ROWS = 16


def _pack_small(t):
    sinks = jnp.pad(t["sinks"], ((0, 0), (0, D_MODEL - t["sinks"].shape[1])))
    rows = [t[k] for k in GAINS] + [t["b_gate"].reshape(3, D_MODEL), sinks]
    packed = jnp.concatenate(rows, axis=0)
    return jnp.pad(packed, ((0, SMALL_ROWS - packed.shape[0]), (0, 0)))


def _unpack_small(p):
    out = {k: p[i:i + 1] for i, k in enumerate(GAINS)}
    out["b_gate"] = p[7:10].reshape(1, 3 * D_MODEL)
    out["sinks"] = p[10:11, :4]
    return out


def kernel(x, mem, ffn1_norm_pre, ffn1_w_in, ffn1_w_out, ffn1_norm_post, mix_norm_pre, w_in, sinks, mem_norm, w_mem_kv, w_gate, b_gate, w_o_a, w_o_b, w_o_m, w_out, mix_norm_post, ffn2_norm_pre, ffn2_w_in, ffn2_w_out, ffn2_norm_post, loss_target, m_ffn1_norm_pre, m_ffn1_w_in, m_ffn1_w_out, m_ffn1_norm_post, m_mix_norm_pre, m_w_in, m_sinks, m_mem_norm, m_w_mem_kv, m_w_gate, m_b_gate, m_w_o_a, m_w_o_b, m_w_o_m, m_w_out, m_mix_norm_post, m_ffn2_norm_pre, m_ffn2_w_in, m_ffn2_w_out, m_ffn2_norm_post, v_ffn1_norm_pre, v_ffn1_w_in, v_ffn1_w_out, v_ffn1_norm_post, v_mix_norm_pre, v_w_in, v_sinks, v_mem_norm, v_w_mem_kv, v_w_gate, v_b_gate, v_w_o_a, v_w_o_b, v_w_o_m, v_w_out, v_mix_norm_post, v_ffn2_norm_pre, v_ffn2_w_in, v_ffn2_w_out, v_ffn2_norm_post):
    given = dict(locals())
    wt = {k: given[k] for k in WEIGHTS}
    mom = {k: given["m_" + k] for k in WEIGHTS}
    var = {k: given["v_" + k] for k in WEIGHTS}
    chip = (2 * lax.axis_index("x") + lax.axis_index("y")).astype(jnp.int32)
    me = chip.reshape(1)

    def landing_zone(own):
        return lax.dynamic_update_slice_in_dim(lax.empty((N_CHIPS,) + own.shape, own.dtype), own[None], chip, 0)

    started = {}
    tokens = []

    def stage_keys(stage):
        return [k for g in GATHER_STAGES[stage] for k in GATHER_GROUPS[g]]

    def prepare(stage):
        shards = [(wt[k][0] + tokens[0][0, 0] if tokens else wt[k][0]).astype(BF16) for k in stage_keys(stage)]
        return shards, [landing_zone(s) for s in shards]

    def start_gather(stage, after):
        groups, keys = GATHER_STAGES[stage], stage_keys(stage)
        members = [[keys.index(k) for k in GATHER_GROUPS[g]] for g in groups]
        sems, shards, lands, token = chip_copies_start(
            *prepared[stage], members, False, f"weight_gather_start_{stage}", after)
        tokens.append(token)
        for g, idx, pair in zip(groups, members, sems):
            started[g] = ([shards[i] for i in idx], [lands[i] for i in idx], pair)

    prepared = {0: prepare(0)}
    start_gather(0, None)
    prepared.update({stage: prepare(stage) for stage in range(1, len(GATHER_STAGES))})

    def weights_of(group, after):
        if group == GATHER_STAGES[0][0]:
            after = [after] + [a for stage in range(1, len(GATHER_STAGES)) for part in prepared[stage] for a in part]
        got = chip_copies_wait(*started[group], after, False, f"weight_gather_wait_{group}")
        stage = [s + 1 for s, groups in enumerate(GATHER_STAGES[:-1]) if groups[0] == group]
        if stage:
            start_gather(stage[0], got[0])
        full = {}
        for k, g in zip(GATHER_GROUPS[group], got):
            if k in COLUMN_SHARDED:
                if k in ("ffn1_w_in", "ffn2_w_in"):
                    g = jnp.stack([g[0], g[2], g[1], g[3]])
                full[k] = jnp.swapaxes(g, 0, 1).reshape(g.shape[1], N_CHIPS * g.shape[2])
                if k == "w_in":
                    full[k] = to_kernel_heads(full[k])
            else:
                full[k] = g.reshape(N_CHIPS * g.shape[1], g.shape[2])
        return full

    in_flight = {}

    def send_grads(group, grads):
        def shard_major(k, g):
            if k in KEPT_SHARD_MAJOR:
                return g
            if k in COLUMN_SHARDED:
                return jnp.swapaxes(g.reshape(g.shape[0], N_CHIPS, g.shape[1] // N_CHIPS), 0, 1)
            return g.reshape(N_CHIPS, g.shape[0] // N_CHIPS, g.shape[1])

        own, wire = [], []
        for k in GROUPS[group]:
            g, rounded = grads[k] if isinstance(grads[k], (tuple, list)) else (grads[k], None)
            g = shard_major(k, from_kernel_heads(g) if k == "w_in" else g)
            own.append(g)
            wire.append(g.astype(BF16) if rounded is None else shard_major(k, rounded))
        zones = [lax.empty(b.shape, b.dtype) for b in wire]
        pair, wire, zones, sent = chip_copies_start(
            wire, zones, [list(range(len(wire)))], True, f"grad_scatter_start_{group}")
        in_flight[group] = (own, wire, zones, pair[0], sent)
        return sent

    gains = {k: wt[k] for k in GAINS}
    sq, dx, grads = layer_step(
        x[0], mem[0], loss_target[0], gains, sinks[0], b_gate, weights_of, send_grads, tokens[0][0, 0])
    loss = lax.psum(0.5 * jnp.sum(sq) / D_MODEL, ("x", "y", "c"))

    res = {}
    after = in_flight["ffn1_out"][4]
    swaps = []
    for stage in (("ffn2_in", "ffn2_out", "mix", "ffn1_in"), ("ffn1_out",)):
        names, parts = [], []
        for group in stage:
            own, wire, zones, pair, _ = in_flight[group]
            received = chip_copies_wait(wire, zones, pair, after, True, f"grad_scatter_wait_{group}")
            for k, g, r in zip(GROUPS[group], own, received):
                names.append(k)
                parts.append(chip_partial_sum(me, g, r, f"{k}_chip_sum"))
        pair, parts, lands, after = sibling_copies_start(parts, f"sibling_start_{stage[-1]}")
        swaps.append((stage[-1], names, parts, lands, pair))
    small_all = small_all_gather(_pack_small(grads), "small_grad_gather")
    packed = adamw_small(small_all, _pack_small(wt), _pack_small(mom), _pack_small(var), "small_adamw")
    after = packed[0]
    for tag, names, parts, lands, pair in swaps:
        sibs = sibling_copies_wait(parts, lands, pair, after, f"sibling_wait_{tag}")
        for k, p, s in zip(names, parts, sibs):
            res[k] = [t[None] for t in adamw_pair(p, s, wt[k][0], mom[k][0], var[k][0], f"{k}_adamw")]
        after = res[names[-1]][0]
    for idx, p in enumerate(packed):
        for k, t in _unpack_small(p).items():
            res.setdefault(k, [None] * 4)[idx] = t

    return (loss, dx[None], *[res[k][0] for k in WEIGHTS], *[res[k][1] for k in WEIGHTS],
            *[res[k][2] for k in WEIGHTS], *[res[k][3] for k in WEIGHTS])
```

```python
import functools

import jax
import jax.numpy as jnp
from jax import lax
from jax.experimental import pallas as pl
from jax.experimental.pallas import tpu as pltpu

F32 = jnp.float32
BF16 = jnp.bfloat16

D_MODEL = 1024
D_FF = 2816
HEAD = 128
N_CHIPS = 4
N_DEV = 8
EPS = 1e-6
NEG_INF = -1e30
ROPE_THETA = 10000.0
ATT_SCALE = HEAD ** -0.5

ADAM_LR = 0.001
ADAM_B1 = 0.9
ADAM_B2 = 0.999
ADAM_EPS = 1e-08
ADAM_WD = 0.01
ADAM_STEP = 10

VMEM_LIMIT = 52 * 2 ** 20
VMEM_LIMIT_LARGE = 60 * 2 ** 20
MESH = pl.DeviceIdType.MESH

QKV_W = 3840
DIL = ((128, 1), (512, 4), (2048, 16))
B_BASE, MQ, A_BASE = 0, 8, 12
_AQ, _AK, _AV, _BQ, _BK, _BV, _MQ = 0, 6, 12, 18, 22, 24, 26
HEAD_ORDER = tuple(
    [h for j in range(2) for h in (_BQ + 2 * j, _BQ + 2 * j + 1, _BK + j, _BV + j)]
    + [_MQ + i for i in range(4)]
    + [h for g in range(3) for i in range(2) for h in (_AQ + 2 * g + i, _AK + 2 * g + i, _AV + 2 * g + i)])
ROTARY_HEADS = tuple(p for p, h in enumerate(HEAD_ORDER) if h < _AV or _BQ <= h < _BV)


def to_kernel_heads(w):
    return jnp.concatenate([w[..., h * HEAD:(h + 1) * HEAD] for h in HEAD_ORDER], axis=-1)


def from_kernel_heads(w):
    place = {h: p for p, h in enumerate(HEAD_ORDER)}
    return jnp.concatenate([w[..., place[h] * HEAD:(place[h] + 1) * HEAD] for h in range(len(HEAD_ORDER))], axis=-1)

TM = 512
FF_T = D_FF // 2


def _params(*sem):
    return pltpu.CompilerParams(dimension_semantics=sem, vmem_limit_bytes=VMEM_LIMIT)


def _dot(a, b):
    return jnp.dot(a, b, preferred_element_type=F32)


def _dot_nt(a, b):
    return lax.dot_general(a, b, (((1,), (1,)), ((), ())), preferred_element_type=F32)


def _dot_tn(a, b):
    return lax.dot_general(a, b, (((0,), (0,)), ((), ())), preferred_element_type=F32)


def _rstd(x):
    return lax.rsqrt(jnp.mean(x * x, axis=-1, keepdims=True) + EPS)


def _sigmoid(x):
    return 0.5 * jnp.tanh(0.5 * x) + 0.5


def _ffn_perm(k):
    return (k % 2) * 2 + k // 2


UNREAD = pl.BlockSpec(memory_space=pl.ANY)


def _resident(arr):
    return pl.BlockSpec(arr.shape, lambda *_: (0,) * arr.ndim, pipeline_mode=pl.Buffered(1))


def rms_scale(x, g, name, after):
    T, D = x.shape
    tm = 1024

    def body(x_ref, g_ref, _, o_ref):
        v = x_ref[...]
        o_ref[...] = (v * _rstd(v) * g_ref[...]).astype(BF16)

    spec = pl.BlockSpec((tm, D), lambda i: (i, 0))
    return pl.pallas_call(
        body, name=name, grid=(T // tm,), in_specs=[spec, _resident(g), UNREAD], out_specs=spec,
        out_shape=jax.ShapeDtypeStruct((T, D), BF16), compiler_params=_params("parallel"),
    )(x, g, after)


def ffn_in(h, g, w, name, xn=None):
    T, D = h.shape
    normed = xn is not None

    def body(h_ref, g_ref, w_ref, *outs):
        if normed:
            xn, (gu_ref, a_ref) = h_ref[...], outs
        else:
            xn_ref, gu_ref, a_ref = outs
            x = h_ref[...]
            xn = (x * _rstd(x) * g_ref[...]).astype(BF16)
            xn_ref[...] = xn
        for j in range(2):
            gu = _dot(xn, w_ref[:, j * 2 * FF_T:(j + 1) * 2 * FF_T])
            gu_ref[:, j * 2 * FF_T:(j + 1) * 2 * FF_T] = gu.astype(BF16)
            gate, up = gu[:, :FF_T], gu[:, FF_T:]
            a_ref[:, j * FF_T:(j + 1) * FF_T] = (gate * _sigmoid(gate) * up).astype(BF16)

    def rows(width):
        return pl.BlockSpec((TM, width), lambda i: (i, 0))

    res = pl.pallas_call(
        body, name=name,
        grid=(T // TM,),
        in_specs=[rows(D), _resident(g), _resident(w)],
        out_specs=[rows(D)] * (not normed) + [rows(2 * D_FF), rows(D_FF)],
        out_shape=[jax.ShapeDtypeStruct((T, D), BF16)] * (not normed)
                  + [jax.ShapeDtypeStruct((T, 2 * D_FF), BF16), jax.ShapeDtypeStruct((T, D_FF), BF16)],
        compiler_params=_params("parallel"),
    )(xn if normed else h, g, w)
    return (xn, *res) if normed else tuple(res)


def mm_norm_res(a, w, h_in, g, coef, name, target=None):
    T, K = a.shape
    D = w.shape[1]
    final = target is not None
    tm = TM if final else min(2 * TM, T)

    def body(*refs):
        if final:
            a_ref, w_ref, h_ref, g_ref, t_ref, f_ref, o_ref, l_ref = refs
        else:
            a_ref, w_ref, h_ref, g_ref, f_ref, o_ref = refs
        f = _dot(a_ref[...], w_ref[...])
        f_ref[...] = f
        y = h_ref[...] + coef * (f * _rstd(f) * g_ref[...])
        if final:
            err = y - t_ref[...]
            o_ref[...] = err * (1.0 / D)

            @pl.when(pl.program_id(0) == 0)
            def _():
                l_ref[...] = jnp.zeros_like(l_ref)

            sq = jnp.sum((err * err).reshape(tm // 8, 8, D), axis=0)
            l_ref[...] += functools.reduce(jnp.add, [sq[:, c:c + HEAD] for c in range(0, D, HEAD)])
        else:
            o_ref[...] = y

    row = pl.BlockSpec((tm, D), lambda i: (i, 0))
    in_specs = [pl.BlockSpec((tm, K), lambda i: (i, 0)),
                _resident(w),
                row, pl.BlockSpec((1, D), lambda i: (0, 0))]
    out_specs = [row, row]
    out_shape = [jax.ShapeDtypeStruct((T, D), F32), jax.ShapeDtypeStruct((T, D), F32)]
    args = [a, w, h_in, g]
    if final:
        in_specs.append(row)
        args.append(target)
        out_specs.append(pl.BlockSpec((8, 128), lambda i: (0, 0)))
        out_shape.append(jax.ShapeDtypeStruct((8, 128), F32))
    return pl.pallas_call(
        body, name=name, grid=(T // tm,), in_specs=in_specs, out_specs=out_specs, out_shape=out_shape,
        compiler_params=_params("arbitrary"),
    )(*args)


def _rope(x, cos, sin_signed):
    return x * cos + pltpu.roll(x, HEAD // 2, axis=1) * sin_signed


def _unrope(x, cos, sin_signed):
    return x * cos - pltpu.roll(x, HEAD // 2, axis=1) * sin_signed


def mix_in(h, g, w, w_gate, b_gate, cos, sin_signed, name):
    T, D = h.shape
    tn = 768

    def body(h_ref, g_ref, w_ref, wg_ref, b_ref, c_ref, s_ref, u_ref, o_ref, gt_ref):
        x = h_ref[...]
        u = (x * _rstd(x) * g_ref[...]).astype(BF16)
        u_ref[...] = u
        c, s = c_ref[...], s_ref[...]
        for j in range(QKV_W // tn):
            acc = _dot(u, w_ref[:, j * tn:(j + 1) * tn])
            for hd in range(tn // HEAD):
                head = j * (tn // HEAD) + hd
                part = acc[:, hd * HEAD:(hd + 1) * HEAD]
                if head in ROTARY_HEADS:
                    part = _rope(part, c, s)
                o_ref[:, head * HEAD:(head + 1) * HEAD] = part.astype(BF16)
        for j in range(w_gate.shape[1] // tn):
            cols = slice(j * tn, (j + 1) * tn)
            gt_ref[:, cols] = _sigmoid(_dot(u, wg_ref[:, cols]) + b_ref[:, cols]).astype(BF16)

    def rows(width):
        return pl.BlockSpec((TM, width), lambda i: (i, 0))

    return pl.pallas_call(
        body, name=name,
        grid=(T // TM,),
        in_specs=[rows(D), _resident(g), _resident(w), _resident(w_gate), _resident(b_gate), rows(HEAD), rows(HEAD)],
        out_specs=[rows(D), rows(QKV_W), rows(w_gate.shape[1])],
        out_shape=[jax.ShapeDtypeStruct((T, D), BF16), jax.ShapeDtypeStruct((T, QKV_W), BF16),
                   jax.ShapeDtypeStruct((T, w_gate.shape[1]), BF16)],
        compiler_params=_params("parallel"),
    )(h, g, w, w_gate, b_gate, cos, sin_signed)


def gate_merge_out(gt, o_a, o_b, o_m, w_a, w_b, w_m, w_out, h_in, g, name):
    T = gt.shape[0]
    D = D_MODEL

    def body(gt_ref, oa_ref, ob_ref, om_ref, wa_ref, wb_ref, wm_ref, wo_ref, h_ref, g_ref, m_ref, f_ref, o_ref):
        acc = gt_ref[:, :D].astype(F32) * _dot(oa_ref[...], wa_ref[...])
        acc += gt_ref[:, D:2 * D].astype(F32) * _dot(ob_ref[...], wb_ref[...])
        acc += gt_ref[:, 2 * D:].astype(F32) * _dot(om_ref[...], wm_ref[...])
        merged = acc.astype(BF16)
        m_ref[...] = merged
        f = _dot(merged, wo_ref[...])
        f_ref[...] = f
        o_ref[...] = h_ref[...] + f * _rstd(f) * g_ref[...]

    def rows(width):
        return pl.BlockSpec((TM, width), lambda i: (i, 0))

    return pl.pallas_call(
        body, name=name, grid=(T // TM,),
        in_specs=[rows(3 * D), rows(o_a.shape[1]), rows(o_b.shape[1]), rows(o_m.shape[1]),
                  _resident(w_a), _resident(w_b), _resident(w_m), _resident(w_out), rows(D), _resident(g)],
        out_specs=[rows(D), rows(D), rows(D)],
        out_shape=[jax.ShapeDtypeStruct((T, D), BF16), jax.ShapeDtypeStruct((T, D), F32),
                   jax.ShapeDtypeStruct((T, D), F32)],
        compiler_params=_params("parallel"),
    )(gt, o_a, o_b, o_m, w_a, w_b, w_m, w_out, h_in, g)


def _band_rows(start, r):
    return pl.ds(start, HEAD) if r == 1 else pl.ds(start, HEAD, stride=r)


def _band_mask(max_dist, first_has_prev):
    row = lax.broadcasted_iota(jnp.int32, (HEAD, 2 * HEAD), 0)
    col = lax.broadcasted_iota(jnp.int32, (HEAD, 2 * HEAD), 1)
    dist = row + HEAD - col
    band = (dist >= 0) & (dist <= max_dist)
    return band, band & (col >= jnp.where(first_has_prev, 0, HEAD))


def _stack(parts):
    return parts[0] if len(parts) == 1 else jnp.concatenate(parts, axis=0)


def _band_specs(BT, SB, nsub, base, grp):
    stride = grp + 2

    def cur(off, width):
        return pl.BlockSpec((BT, width * HEAD), lambda h, i: (i, (base + h * stride + off) // width))

    def prev(off):
        return pl.BlockSpec((SB, HEAD), lambda h, i: (jnp.maximum(i * nsub - 1, 0), base + h * stride + off))

    return cur(0, grp), cur(grp, 1), prev(grp), cur(grp + 1, 1), prev(grp + 1)


def band_fwd(qkv, sinks, *, r, base, hkv, grp, max_dist, out_dtype, name, merge=None):
    T, W = qkv.shape
    SB = HEAD * r
    BT = min(2048, T)
    nsub, nib = BT // SB, T // BT
    hq = hkv * grp
    heads = [slice(g * HEAD, (g + 1) * HEAD) for g in range(grp)]
    others = [] if merge is None else [*merge[0], *merge[1]]

    def body(sink_ref, q_ref, kc_ref, kp_ref, vc_ref, vp_ref, *rest):
        joint_o, joint_l = rest[len(others):len(others) + 2]
        qf, kf, vf = rest[len(others) + 2:len(others) + 5]
        o_ref, l_ref = rest[len(others) + 5:] if others else (joint_o, joint_l)
        kvh, ib = pl.program_id(0), pl.program_id(1)
        qf[...] = q_ref[...].astype(F32)
        kf[:SB] = kp_ref[...].astype(F32)
        kf[SB:] = kc_ref[...].astype(F32)
        vf[:SB] = vp_ref[...].astype(F32)
        vf[SB:] = vc_ref[...].astype(F32)
        band, band_first = _band_mask(max_dist, ib > 0)
        for c in range(r):
            k_old, v_old = kf[_band_rows(c, r)], vf[_band_rows(c, r)]
            for j in range(nsub):
                mask = band_first if j == 0 else band
                rows = _band_rows(j * SB + c, r)
                k_own, v_own = kf[_band_rows((j + 1) * SB + c, r)], vf[_band_rows((j + 1) * SB + c, r)]
                kcat = jnp.concatenate([k_old, k_own], axis=0).astype(BF16)
                vcat = jnp.concatenate([v_old, v_own], axis=0).astype(BF16)
                k_old, v_old = k_own, v_own
                s_all = _dot_nt(_stack([qf[rows, cols] for cols in heads]).astype(BF16), kcat) * ATT_SCALE
                probs, tots = [], []
                for g, cols in enumerate(heads):
                    s = jnp.where(mask, s_all[cols], NEG_INF)
                    sk = sink_ref[kvh * grp + g]
                    m = jnp.maximum(jnp.max(s, axis=-1, keepdims=True), sk)
                    p = jnp.exp(s - m)
                    tot = jnp.sum(p, axis=-1, keepdims=True) + jnp.exp(sk - m)
                    probs.append(p.astype(BF16))
                    tots.append(tot)
                    l_ref[rows, cols] = jnp.broadcast_to(m + jnp.log(tot), (HEAD, HEAD))
                o_all = _dot(_stack(probs), vcat)
                for g, cols in enumerate(heads):
                    o_ref[rows, cols] = (o_all[cols] / tots[g]).astype(o_ref.dtype)

        if others:
            half = len(others) // 2
            outs = [ref[...] for ref in rest[:half]] + [o_ref[...]]
            logs = [ref[...] for ref in rest[half:len(others)]] + [l_ref[...]]
            top = functools.reduce(jnp.maximum, logs)
            weights = [jnp.exp(lg - top) for lg in logs]
            total = functools.reduce(jnp.add, weights)
            mixed = functools.reduce(jnp.add, [wgt * out for wgt, out in zip(weights, outs)])
            joint_o[...] = (mixed / total).astype(out_dtype)
            joint_l[...] = top + jnp.log(total)

    out_spec = pl.BlockSpec((BT, grp * HEAD), lambda h, i: (i, h))
    own = [pltpu.VMEM((BT, grp * HEAD), F32)] * 2 if others else []
    return pl.pallas_call(
        body, name=name, grid=(hkv, nib),
        in_specs=[pl.BlockSpec(memory_space=pltpu.SMEM), *_band_specs(BT, SB, nsub, base, grp)]
                 + [out_spec] * len(others),
        out_specs=[out_spec, out_spec],
        out_shape=[jax.ShapeDtypeStruct((T, hq * HEAD), out_dtype), jax.ShapeDtypeStruct((T, hq * HEAD), F32)],
        scratch_shapes=[pltpu.VMEM((BT, grp * HEAD), F32), pltpu.VMEM((SB + BT, HEAD), F32),
                        pltpu.VMEM((SB + BT, HEAD), F32)] + own,
        compiler_params=_params("parallel", "arbitrary"),
    )(sinks, qkv, qkv, qkv, qkv, qkv, *others)


def band_bwd(qkv, dqkv, do, o, lse, cos, sin_signed, sinks, *, r, base, hkv, grp, max_dist, name):
    T, W = qkv.shape
    SB = HEAD * r
    BT = min(max(2048, 2 * SB), T)
    nsub, nib = BT // SB, T // BT
    nblk = T // SB
    with_sink = sinks is not None
    heads = [slice(g * HEAD, (g + 1) * HEAD) for g in range(grp)]

    def body(*refs):
        if with_sink:
            sink_ref, refs = refs[0], refs[1:]
        (q_ref, kc_ref, kp_ref, vc_ref, vp_ref, qn_ref, do_ref, don_ref, o_ref, on_ref, l_ref, ln_ref,
         c_ref, s_ref, _) = refs[:15]
        out_ref = refs[15]
        ds_ref = refs[16] if with_sink else None
        qf, dof, of, kf, vf, dqf, dkf, dvf = refs[-8:]
        kvh, ib = pl.program_id(0), pl.program_id(1)
        for buf, cur_ref, nxt_ref in ((qf, q_ref, qn_ref), (dof, do_ref, don_ref), (of, o_ref, on_ref)):
            buf[:BT] = cur_ref[...].astype(F32)
            buf[BT:] = nxt_ref[...].astype(F32)
        kf[:SB] = kp_ref[...].astype(F32)
        kf[SB:] = kc_ref[...].astype(F32)
        vf[:SB] = vp_ref[...].astype(F32)
        vf[SB:] = vc_ref[...].astype(F32)
        band, band_first = _band_mask(max_dist, ib > 0)
        if with_sink:
            @pl.when(ib == 0)
            def _():
                ds_ref[...] = jnp.zeros_like(ds_ref)

        def grads(rows, logzs, keys, vals, mask):
            q = _stack([qf[rows, cols] for cols in heads]).astype(BF16)
            dout = _stack([dof[rows, cols] for cols in heads]).astype(BF16)
            s_all = _dot_nt(q, keys) * ATT_SCALE
            dp_all = _dot_nt(dout, vals)
            probs, dss, deltas = [], [], []
            for g, cols in enumerate(heads):
                delta = jnp.sum(dof[rows, cols] * of[rows, cols], axis=-1, keepdims=True)
                p = jnp.exp(jnp.where(mask, s_all[cols], NEG_INF) - logzs[g][:, :1])
                probs.append(p.astype(BF16))
                dss.append((p * (dp_all[cols] - delta) * ATT_SCALE).astype(BF16))
                deltas.append(delta)
            return q, dout, _stack(probs), _stack(dss), deltas

        row = lax.broadcasted_iota(jnp.int32, (HEAD, HEAD), 0)
        col = lax.broadcasted_iota(jnp.int32, (HEAD, HEAD), 1)
        reach = col >= row + jnp.where(ib < nib - 1, HEAD - max_dist, 2 * HEAD)
        for c in range(r):
            k_old, v_old = kf[_band_rows(c, r)], vf[_band_rows(c, r)]
            dk_own = dv_own = None
            for j in range(nsub):
                rows = _band_rows(j * SB + c, r)
                k_own, v_own = kf[_band_rows((j + 1) * SB + c, r)], vf[_band_rows((j + 1) * SB + c, r)]
                kcat = jnp.concatenate([k_old, k_own], axis=0).astype(BF16)
                vcat = jnp.concatenate([v_old, v_own], axis=0).astype(BF16)
                logzs = [l_ref[rows, cols] for cols in heads]
                q, dout, p, ds, deltas = grads(rows, logzs, kcat, vcat, band_first if j == 0 else band)
                dq = _dot(ds, kcat)
                for g, cols in enumerate(heads):
                    dqf[rows, cols] = dq[cols]
                    if with_sink:
                        p_sink = jnp.exp(sink_ref[kvh * grp + g] - logzs[g][:, :1])
                        ds_ref[g * 8:(g + 1) * 8] += jnp.sum(p_sink * deltas[g])
                dk, dv = _dot_tn(ds, q), _dot_tn(p, dout)
                if j > 0:
                    done = _band_rows((j - 1) * SB + c, r)
                    dkf[done] = dk_own + dk[:HEAD]
                    dvf[done] = dv_own + dv[:HEAD]
                dk_own, dv_own = dk[HEAD:], dv[HEAD:]
                k_old, v_old = k_own, v_own
            logzs = [ln_ref[_band_rows(c, r), cols] for cols in heads]
            q, dout, p, ds, _ = grads(_band_rows(BT + c, r), logzs, k_old.astype(BF16), v_old.astype(BF16), reach)
            done = _band_rows((nsub - 1) * SB + c, r)
            dkf[done] = dk_own + _dot_tn(ds, q)
            dvf[done] = dv_own + _dot_tn(p, dout)

        cs, sn = c_ref[...], s_ref[...]
        for cols in heads:
            out_ref[:, cols] = _unrope(dqf[:, cols], cs, sn).astype(BF16)
        out_ref[:, grp * HEAD:(grp + 1) * HEAD] = _unrope(dkf[...], cs, sn).astype(BF16)
        out_ref[:, (grp + 1) * HEAD:] = dvf[...].astype(BF16)

    def nxt_row(i):
        return jnp.minimum((i + 1) * nsub, nblk - 1)

    stride = grp + 2
    q_next = pl.BlockSpec((SB, grp * HEAD), lambda h, i: (nxt_row(i), (base + h * stride) // grp))
    head_cur = pl.BlockSpec((BT, grp * HEAD), lambda h, i: (i, h))
    head_next = pl.BlockSpec((SB, grp * HEAD), lambda h, i: (nxt_row(i), h))
    table = pl.BlockSpec((BT, HEAD), lambda h, i: (i, 0))

    in_specs = [*_band_specs(BT, SB, nsub, base, grp), q_next,
                head_cur, head_next, head_cur, head_next, head_cur, head_next, table, table, UNREAD]
    args = [qkv, qkv, qkv, qkv, qkv, qkv, do, do, o, o, lse, lse, cos, sin_signed, dqkv]
    out_specs = [pl.BlockSpec((BT, stride * HEAD), lambda h, i: (i, base // stride + h))]
    out_shape = [jax.ShapeDtypeStruct(dqkv.shape, dqkv.dtype)]
    if with_sink:
        in_specs.insert(0, pl.BlockSpec(memory_space=pltpu.SMEM))
        args.insert(0, sinks)
        out_specs.append(pl.BlockSpec((None, grp * 8, HEAD), lambda h, i: (h, 0, 0)))
        out_shape.append(jax.ShapeDtypeStruct((hkv, grp * 8, HEAD), F32))
    wide = pltpu.VMEM((BT + SB, grp * HEAD), F32)
    tall = pltpu.VMEM((SB + BT, HEAD), F32)
    grad = pltpu.VMEM((BT, HEAD), F32)
    return pl.pallas_call(
        body, name=name, grid=(hkv, nib), in_specs=in_specs, out_specs=out_specs, out_shape=out_shape,
        input_output_aliases={len(args) - 1: 0},
        scratch_shapes=[wide, wide, wide, tall, tall, pltpu.VMEM((BT, grp * HEAD), F32), grad, grad],
        compiler_params=pltpu.CompilerParams(dimension_semantics=("parallel", "arbitrary"),
                                             vmem_limit_bytes=VMEM_LIMIT_LARGE),
    )(*args)


M_HEADS = 4


def mem_kv(mem, g, w, name):
    n, D = mem.shape

    def body(m_ref, g_ref, w_ref, mn_ref, kv_ref):
        x = m_ref[...]
        mn = (x * _rstd(x) * g_ref[...]).astype(BF16)
        mn_ref[...] = mn
        kv_ref[...] = _dot(mn, w_ref[...]).astype(BF16)

    return pl.pallas_call(
        body, name=name,
        out_shape=[jax.ShapeDtypeStruct((n, D), BF16), jax.ShapeDtypeStruct((n, w.shape[1]), BF16)],
        compiler_params=pltpu.CompilerParams(vmem_limit_bytes=VMEM_LIMIT),
    )(mem, g, w)


def mem_fwd(qkv, mkv, name):
    T = qkv.shape[0]
    n = mkv.shape[0]
    RB = 1024

    def body(q_ref, kv_ref, o_ref, l_ref):
        for h in range(M_HEADS):
            cols = slice(h * HEAD, (h + 1) * HEAD)
            s = _dot_nt(q_ref[:, cols], kv_ref[:, cols]) * ATT_SCALE
            m = jnp.max(s, axis=-1, keepdims=True)
            p = jnp.exp(s - m)
            den = jnp.sum(p, axis=-1, keepdims=True)
            vals = kv_ref[:, (M_HEADS + h) * HEAD:(M_HEADS + h + 1) * HEAD]
            o_ref[:, cols] = (_dot(p.astype(BF16), vals) / den).astype(BF16)
            l_ref[:, cols] = jnp.broadcast_to(m + jnp.log(den), (RB, HEAD))

    out = pl.BlockSpec((RB, M_HEADS * HEAD), lambda i: (i, 0))
    return pl.pallas_call(
        body, name=name, grid=(T // RB,),
        in_specs=[pl.BlockSpec((RB, M_HEADS * HEAD), lambda i: (i, MQ // M_HEADS)), _resident(mkv)],
        out_specs=[out, out],
        out_shape=[jax.ShapeDtypeStruct((T, M_HEADS * HEAD), BF16), jax.ShapeDtypeStruct((T, M_HEADS * HEAD), F32)],
        compiler_params=_params("parallel"),
    )(qkv, mkv)


def mem_bwd(qkv, dqkv, mkv, do, o, lse, name):
    T = qkv.shape[0]
    n = mkv.shape[0]
    RB = 1024

    def body(q_ref, kv_ref, do_ref, o_ref, l_ref, _, dq_ref, dk_ref, dv_ref):
        @pl.when(pl.program_id(0) == 0)
        def _():
            dk_ref[...] = jnp.zeros_like(dk_ref)
            dv_ref[...] = jnp.zeros_like(dv_ref)

        for h in range(M_HEADS):
            cols = slice(h * HEAD, (h + 1) * HEAD)
            keys, vals = kv_ref[:, cols], kv_ref[:, (M_HEADS + h) * HEAD:(M_HEADS + h + 1) * HEAD]
            q, dout = q_ref[:, cols], do_ref[:, cols]
            delta = jnp.sum(dout.astype(F32) * o_ref[:, cols].astype(F32), axis=-1, keepdims=True)
            p = jnp.exp(_dot_nt(q, keys) * ATT_SCALE - l_ref[:, cols][:, :1])
            ds = (p * (_dot_nt(dout, vals) - delta) * ATT_SCALE).astype(BF16)
            dq_ref[:, cols] = _dot(ds, keys).astype(BF16)
            dk_ref[:, cols] += _dot_tn(ds, q)
            dv_ref[:, cols] += _dot_tn(p.astype(BF16), dout)

    wide = M_HEADS * HEAD
    tok = pl.BlockSpec((RB, wide), lambda i: (i, 0))
    q_cols = pl.BlockSpec((RB, wide), lambda i: (i, MQ // M_HEADS))
    slot = pl.BlockSpec((n, wide), lambda i: (0, 0))
    return pl.pallas_call(
        body, name=name, grid=(T // RB,),
        in_specs=[q_cols, _resident(mkv), tok, tok, tok, UNREAD],
        out_specs=[q_cols, slot, slot],
        out_shape=[jax.ShapeDtypeStruct(dqkv.shape, dqkv.dtype),
                   jax.ShapeDtypeStruct((n, wide), F32), jax.ShapeDtypeStruct((n, wide), F32)],
        input_output_aliases={5: 0},
        compiler_params=_params("arbitrary"),
    )(qkv, mkv, do, o, lse, dqkv)


def mem_kv_bwd(mem, g, mem_n, w, dmkv, name):
    n, D = mem.shape

    def body(m_ref, g_ref, mn_ref, w_ref, d_ref, dw_ref, dg_ref):
        d = d_ref[...].astype(BF16)
        dw_ref[...] = _dot_tn(mn_ref[...], d)
        x = m_ref[...]
        dg_ref[...] = jnp.sum(_dot_nt(d, w_ref[...]) * (x * _rstd(x)), axis=0, keepdims=True)

    return pl.pallas_call(
        body, name=name,
        out_shape=[jax.ShapeDtypeStruct(w.shape, F32), jax.ShapeDtypeStruct((1, D), F32)],
        compiler_params=pltpu.CompilerParams(vmem_limit_bytes=VMEM_LIMIT),
    )(mem, g, mem_n, w, dmkv)


def _rms_bwd(dn, f, g):
    r = _rstd(f)
    fhat = f * r
    dfhat = dn * g
    df = r * (dfhat - fhat * jnp.mean(dfhat * fhat, axis=-1, keepdims=True))
    return df, jnp.sum(dn * fhat, axis=0, keepdims=True)


def ffn_tokens_bwd(dh, f, h_in, gu, g_pre, g_post, w_in, w_out, coef, name, after):
    T, D = dh.shape

    def body(dh_ref, f_ref, h_ref, gu_ref, gpre_ref, gpost_ref, win_ref, wout_ref, _,
             df_ref, dgu_ref, dhin_ref, dgpre_ref, dgpost_ref, dxn_ref):
        i, j = pl.program_id(0), pl.program_id(1)

        @pl.when(j == 0)
        def _():
            @pl.when(i == 0)
            def _():
                dgpre_ref[...] = jnp.zeros_like(dgpre_ref)
                dgpost_ref[...] = jnp.zeros_like(dgpost_ref)

            df, dg_post = _rms_bwd(coef * dh_ref[...], f_ref[...], gpost_ref[...])
            dgpost_ref[...] += dg_post
            df_ref[...] = df.astype(BF16)

        for jj in range(2):
            @pl.when(j == jj)
            def _(jj=jj):
                lo, mid, hi = 2 * jj * FF_T, (2 * jj + 1) * FF_T, (2 * jj + 2) * FF_T
                da = _dot_nt(df_ref[...], wout_ref[jj * FF_T:(jj + 1) * FF_T, :])
                gate = gu_ref[:, :FF_T].astype(F32)
                up = gu_ref[:, FF_T:].astype(F32)
                sig = _sigmoid(gate)
                dgate = (da * up * sig * (1.0 + gate * (1.0 - sig))).astype(BF16)
                dup = (da * gate * sig).astype(BF16)
                dgu_ref[:, :FF_T] = dgate
                dgu_ref[:, FF_T:] = dup
                part = _dot_nt(dgate, win_ref[:, lo:mid]) + _dot_nt(dup, win_ref[:, mid:hi])
                if jj == 0:
                    dxn_ref[...] = part
                else:
                    h = h_ref[...]
                    r = _rstd(h)
                    xhat = h * r
                    dxn = dxn_ref[...] + part
                    dxhat = dxn * gpre_ref[...]
                    dhin_ref[...] = dh_ref[...] + r * (dxhat - xhat * jnp.mean(dxhat * xhat, axis=-1, keepdims=True))
                    dgpre_ref[...] += jnp.sum(dxn * xhat, axis=0, keepdims=True)

    row = pl.BlockSpec((TM, D), lambda i, j: (i, 0))
    wide = pl.BlockSpec((TM, 2 * FF_T), lambda i, j: (i, j))
    vec = pl.BlockSpec((1, D), lambda i, j: (0, 0))
    return pl.pallas_call(
        body, name=name, grid=(T // TM, 2),
        in_specs=[row, row, row, wide, _resident(g_pre), _resident(g_post), _resident(w_in), _resident(w_out),
                  UNREAD],
        out_specs=[row, wide, row, vec, vec],
        out_shape=[jax.ShapeDtypeStruct((T, D), BF16), jax.ShapeDtypeStruct((T, 2 * D_FF), BF16),
                   jax.ShapeDtypeStruct((T, D), F32), jax.ShapeDtypeStruct((1, D), F32),
                   jax.ShapeDtypeStruct((1, D), F32)],
        scratch_shapes=[pltpu.VMEM((TM, D), F32)],
        compiler_params=pltpu.CompilerParams(dimension_semantics=("arbitrary", "arbitrary"),
                                             vmem_limit_bytes=VMEM_LIMIT_LARGE),
    )(dh, f, h_in, gu, g_pre, g_post, w_in, w_out, after)


def mm_nt_norm_bwd(pieces, h_in, dh_out, g, name, after):
    T, D = h_in.shape

    def body(*refs):
        ab = refs[:2 * len(pieces)]
        h_ref, dh_ref, g_ref, _, o_ref, dg_ref = refs[2 * len(pieces):]
        dxn = _dot_nt(ab[0][...], ab[1][...])
        for p in range(1, len(pieces)):
            dxn += _dot_nt(ab[2 * p][...], ab[2 * p + 1][...])
        h = h_ref[...]
        r = _rstd(h)
        xhat = h * r
        dxhat = dxn * g_ref[...]
        o_ref[...] = dh_ref[...] + r * (dxhat - xhat * jnp.mean(dxhat * xhat, axis=-1, keepdims=True))

        @pl.when(pl.program_id(0) == 0)
        def _():
            dg_ref[...] = jnp.zeros_like(dg_ref)

        dg_ref[...] += jnp.sum(dxn * xhat, axis=0, keepdims=True)

    in_specs, args = [], []
    for a, w in pieces:
        in_specs += [pl.BlockSpec((TM, a.shape[1]), lambda i: (i, 0)), _resident(w)]
        args += [a, w]
    row = pl.BlockSpec((TM, D), lambda i: (i, 0))
    return pl.pallas_call(
        body, name=name, grid=(T // TM,),
        in_specs=in_specs + [row, row, _resident(g), UNREAD],
        out_specs=[row, pl.BlockSpec((1, D), lambda i: (0, 0))],
        out_shape=[jax.ShapeDtypeStruct((T, D), F32), jax.ShapeDtypeStruct((1, D), F32)],
        compiler_params=_params("arbitrary"),
    )(*args, h_in, dh_out, g, after)


def gate_merge_out_bwd(dh, f, g, w_out, merged, gt, o_a, o_b, o_m, w_a, w_b, w_m, name, after):
    T = dh.shape[0]
    D = D_MODEL
    branch = ((o_a, w_a), (o_b, w_b), (o_m, w_m))

    def body(dh_ref, f_ref, g_ref, wo_ref, m_ref, gt_ref, oa_ref, ob_ref, om_ref, wa_ref, wb_ref, wm_ref, _,
             dg_ref, dwo_ref, dgt_ref, doa_ref, dob_ref, dom_ref, db_ref, dwa_ref, dwb_ref, dwm_ref):
        @pl.when(pl.program_id(0) == 0)
        def _():
            for acc in (dg_ref, dwo_ref, db_ref, dwa_ref, dwb_ref, dwm_ref):
                acc[...] = jnp.zeros_like(acc)

        df, dg = _rms_bwd(dh_ref[...], f_ref[...], g_ref[...])
        dg_ref[...] += dg
        df = df.astype(BF16)
        dwo_ref[...] += _dot_tn(m_ref[...], df)
        dmf = _dot_nt(df, wo_ref[...])
        for x, (o_ref, w_ref, do_ref, dw_ref) in enumerate(((oa_ref, wa_ref, doa_ref, dwa_ref),
                                                           (ob_ref, wb_ref, dob_ref, dwb_ref),
                                                           (om_ref, wm_ref, dom_ref, dwm_ref))):
            cols = slice(x * D, (x + 1) * D)
            gx = gt_ref[:, cols].astype(F32)
            w = w_ref[...]
            dpre = dmf * _dot(o_ref[...], w) * gx * (1.0 - gx)
            dgt_ref[:, cols] = dpre.astype(BF16)
            db_ref[:, cols] += jnp.sum(dpre, axis=0, keepdims=True)
            dp = (dmf * gx).astype(BF16)
            do_ref[...] = _dot_nt(dp, w).astype(BF16)
            dw_ref[...] += _dot_tn(dp, o_ref[...])

    def rows(width):
        return pl.BlockSpec((TM, width), lambda i: (i, 0))

    def kept(shape):
        return pl.BlockSpec(shape, lambda i: (0,) * len(shape))

    widths = [o.shape[1] for o, _ in branch]
    sums = [(1, D), (D, D), (1, 3 * D)] + [(D, k) for k in widths]
    return pl.pallas_call(
        body, name=name, grid=(T // TM,),
        in_specs=[rows(D), rows(D), _resident(g), _resident(w_out), rows(D), rows(3 * D)]
                 + [rows(k) for k in widths] + [_resident(w) for _, w in branch] + [UNREAD],
        out_specs=[kept(sums[0]), kept(sums[1]), rows(3 * D)] + [rows(k) for k in widths]
                  + [kept(shape) for shape in sums[2:]],
        out_shape=[jax.ShapeDtypeStruct(sums[0], F32), jax.ShapeDtypeStruct(sums[1], F32),
                   jax.ShapeDtypeStruct((T, 3 * D), BF16)] + [jax.ShapeDtypeStruct((T, k), BF16) for k in widths]
                  + [jax.ShapeDtypeStruct(shape, F32) for shape in sums[2:]],
        compiler_params=pltpu.CompilerParams(dimension_semantics=("arbitrary",), vmem_limit_bytes=VMEM_LIMIT_LARGE),
    )(dh, f, g, w_out, merged, gt, o_a, o_b, o_m, w_a, w_b, w_m, after)


def mm_tn(x, dy, tm, tn, name, shard_major=False, perm=None, slabs=1, after=None, wire=False):
    T, M = x.shape
    N = dy.shape[1]
    tk = min(2048, T)
    perm = perm or (lambda j: j)
    w = tn // slabs

    def body(x_ref, dy_ref, *rest):
        o_ref = rest[-2] if wire else rest[-1]

        @pl.when(pl.program_id(2) == 0)
        def _():
            o_ref[...] = jnp.zeros_like(o_ref)

        acc = _dot_tn(x_ref[...], dy_ref[...])
        if shard_major:
            for s in range(slabs):
                o_ref[s] += acc[:, s * w:(s + 1) * w]
        else:
            o_ref[...] += acc
        if wire:
            @pl.when(pl.program_id(2) == T // tk - 1)
            def _():
                rest[-1][...] = o_ref[...].astype(BF16)

    if shard_major:
        out_spec = pl.BlockSpec((slabs, tm, w), lambda i, j, k: (perm(j), i, 0))
        out_shape = jax.ShapeDtypeStruct((N // w, M, w), F32)
    else:
        out_spec = pl.BlockSpec((tm, tn), lambda i, j, k: (i, j))
        out_shape = jax.ShapeDtypeStruct((M, N), F32)
    return pl.pallas_call(
        body, name=name, grid=(M // tm, N // tn, T // tk),
        in_specs=[pl.BlockSpec((tk, tm), lambda i, j, k: (k, i)),
                  pl.BlockSpec((tk, tn), lambda i, j, k: (k, j))] + ([] if after is None else [UNREAD]),
        out_specs=[out_spec, out_spec] if wire else out_spec,
        out_shape=[out_shape, jax.ShapeDtypeStruct(out_shape.shape, BF16)] if wire else out_shape,
        compiler_params=_params("parallel", "parallel", "arbitrary"),
    )(x, dy, *([] if after is None else [after]))


def rope_tables(T, zero):
    half = HEAD // 2
    inv = ROPE_THETA ** (-jnp.arange(half, dtype=F32) / half)
    ang = (jnp.arange(T).astype(F32) + zero)[:, None] * inv[None, :]
    cos, sin = jnp.cos(ang), jnp.sin(ang)
    return jnp.concatenate([cos, cos], axis=1), jnp.concatenate([-sin, sin], axis=1)


def layer_step(x, mem, target, gains, sinks, b_gate, weights_of, send_grads, zero):
    T = x.shape[0]
    cos, sin_signed = rope_tables(T, zero)
    no_sink = jnp.full((2,), NEG_INF, F32)

    xn1 = rms_scale(x, gains["ffn1_norm_pre"], "ffn1_norm", cos)
    w = dict(weights_of("ffn1_in", xn1))
    _, gu1, a1 = ffn_in(x, gains["ffn1_norm_pre"], w["ffn1_w_in"], "ffn1_in", xn=xn1)
    w.update(weights_of("ffn1_out", a1))
    f1, h1 = mm_norm_res(a1, w["ffn1_w_out"], x, gains["ffn1_norm_post"], 0.5, "ffn1_out")
    w.update(weights_of("mix_in", f1))
    u, qkv, gt = mix_in(h1, gains["mix_norm_pre"], w["w_in"], w["w_gate"], b_gate, cos, sin_signed, "mix_in")
    w.update(weights_of("mix_rest", u))
    outs, lses = [], []
    for gidx, (window, dil) in enumerate(DIL):
        last = gidx == len(DIL) - 1
        o_g, l_g = band_fwd(qkv, no_sink, r=dil, base=A_BASE + 6 * gidx, hkv=2, grp=1, max_dist=window // dil,
                            out_dtype=BF16 if last else F32, name=f"attn_a{gidx}_fwd",
                            merge=(outs, lses) if last else None)
        outs.append(o_g)
        lses.append(l_g)
    o_a, l_a = outs[-1], lses[-1]
    o_b, l_b = band_fwd(qkv, sinks, r=1, base=B_BASE, hkv=2, grp=2, max_dist=HEAD - 1, out_dtype=BF16,
                        name="attn_b_fwd")
    mem_n, mkv = mem_kv(mem, gains["mem_norm"], w["w_mem_kv"], "mem_kv")
    o_m, l_m = mem_fwd(qkv, mkv, "attn_m_fwd")
    merged, mo, h2 = gate_merge_out(gt, o_a, o_b, o_m, w["w_o_a"], w["w_o_b"], w["w_o_m"], w["w_out"], h1,
                                    gains["mix_norm_post"], "gate_merge_out")
    w.update(weights_of("ffn2", mo))
    xn2, gu2, a2 = ffn_in(h2, gains["ffn2_norm_pre"], w["ffn2_w_in"], "ffn2_in")
    f2, dy, sq = mm_norm_res(a2, w["ffn2_w_out"], h2, gains["ffn2_norm_post"], 0.5, "ffn2_out", target=target)

    grads = {}

    def ffn_bwd(tag, dh_out, f, gu, a, xn, h_in, after):
        df, dgu, dh_in, grads[f"{tag}_norm_pre"], grads[f"{tag}_norm_post"] = ffn_tokens_bwd(
            dh_out, f, h_in, gu, gains[f"{tag}_norm_pre"], gains[f"{tag}_norm_post"], w[f"{tag}_w_in"],
            w[f"{tag}_w_out"], 0.5, f"{tag}_tokens_bwd", after)
        sent = send_grads(f"{tag}_in", {f"{tag}_w_in": mm_tn(
            xn, dgu, D_MODEL, FF_T, f"{tag}_w_in_grad", shard_major=True, perm=_ffn_perm, wire=True)})
        sent = send_grads(f"{tag}_out", {f"{tag}_w_out": mm_tn(
            a, df, FF_T, D_MODEL, f"{tag}_w_out_grad", after=sent, wire=True)})
        return dh_in, sent

    dh2, sent = ffn_bwd("ffn2", dy, f2, gu2, a2, xn2, h2, dy)

    mix = {}
    (grads["mix_norm_post"], mix["w_out"], dgt, do_a, do_b, do_m, grads["b_gate"],
     dwa_t, dwb_t, dwm_t) = gate_merge_out_bwd(
        dh2, mo, gains["mix_norm_post"], w["w_out"], merged, gt, o_a, o_b, o_m, w["w_o_a"], w["w_o_b"],
        w["w_o_m"], "gate_merge_out_bwd", sent)
    mix["w_o_a"], mix["w_o_b"], mix["w_o_m"] = dwa_t.T, dwb_t.T, dwm_t.T

    dqkv = lax.empty(qkv.shape, qkv.dtype)
    for gidx, (window, dil) in enumerate(DIL):
        dqkv, = band_bwd(qkv, dqkv, do_a, o_a, l_a, cos, sin_signed, None, r=dil, base=A_BASE + 6 * gidx, hkv=2,
                         grp=1, max_dist=window // dil, name=f"attn_a{gidx}_bwd")
    dqkv, dsink = band_bwd(qkv, dqkv, do_b, o_b, l_b, cos, sin_signed, sinks, r=1, base=B_BASE, hkv=2, grp=2,
                           max_dist=HEAD - 1, name="attn_b_bwd")
    grads["sinks"] = -dsink[:, ::8, 0].reshape(1, 4)
    dqkv, dmk, dmv = mem_bwd(qkv, dqkv, mkv, do_m, o_m, l_m, "attn_m_bwd")
    mix["w_mem_kv"], grads["mem_norm"] = mem_kv_bwd(
        mem, gains["mem_norm"], mem_n, w["w_mem_kv"], jnp.concatenate([dmk, dmv], axis=1), "mem_kv_bwd")

    mix["w_in"] = mm_tn(u, dqkv, D_MODEL, 1920, "w_in_grad")
    mix["w_gate"] = mm_tn(u, dgt, D_MODEL, 1536, "w_gate_grad", shard_major=True, slabs=2, wire=True)
    sent = send_grads("mix", mix)
    dh1, grads["mix_norm_pre"] = mm_nt_norm_bwd(
        [(dqkv, w["w_in"]), (dgt, w["w_gate"])], h1, dh2, gains["mix_norm_pre"], "mix_in_bwd", sent)

    dx, _ = ffn_bwd("ffn1", dh1, f1, gu1, a1, xn1, x, dh1)
    return sq, dx, grads


def _place():
    return lax.axis_index("x"), lax.axis_index("y"), lax.axis_index("c")


def _other_chips(x, y):
    return [(1 - x, y), (x, 1 - y), (1 - x, 1 - y)]


def _hbm(n):
    return [pl.BlockSpec(memory_space=pltpu.HBM)] * n


SEM = pl.BlockSpec(memory_space=pltpu.SEMAPHORE)
SIDE_EFFECT = pltpu.SideEffectType.DATAFLOW_SIDE_EFFECTING


def _chip_copy(src, land, sems, i, j, dst_slot, scatter):
    x, y, c = _place()
    px, py = _other_chips(x, y)[j]
    send_sems, recv_sems = sems
    return pltpu.make_async_remote_copy(
        src_ref=src[i].at[2 * px + py] if scatter else src[i], dst_ref=land[i].at[dst_slot],
        send_sem=send_sems.at[3 * i + j], recv_sem=recv_sems.at[3 * i + j],
        device_id=(px, py, c), device_id_type=MESH)


def chip_copies_start(srcs, lands, groups, scatter, name, after=None):
    n = len(srcs)

    def body(*refs):
        src, land = refs[:n], refs[n:2 * n]
        first_sem = 2 * n + (after is not None)
        sems = refs[first_sem:first_sem + 2 * len(groups)]
        token = refs[-1]
        x, y, _ = _place()
        for g, members in enumerate(groups):
            part = ([src[i] for i in members], [land[i] for i in members])
            for t in range(len(members)):
                for j in range(3):
                    _chip_copy(*part, sems[2 * g:2 * g + 2], t, j, 2 * x + y, scatter).start()
        token[...] = jnp.zeros_like(token)

    sem_shapes = [pltpu.SemaphoreType.DMA((3 * len(m),)) for m in groups for _ in range(2)]
    thru = [pltpu.HBM(a.shape, a.dtype) for a in (*srcs, *lands)]
    res = pl.pallas_call(
        body, name=name,
        out_shape=(*sem_shapes, *thru, jax.ShapeDtypeStruct((8, 128), F32)),
        in_specs=_hbm(2 * n) + ([] if after is None else [UNREAD]),
        out_specs=(*[SEM] * len(sem_shapes), *_hbm(2 * n), pl.BlockSpec(memory_space=pltpu.VMEM)),
        input_output_aliases={i: len(sem_shapes) + i for i in range(2 * n)},
        compiler_params=pltpu.CompilerParams(has_side_effects=SIDE_EFFECT),
    )(*[pltpu.with_memory_space_constraint(a, pltpu.HBM) for a in (*srcs, *lands)],
      *([] if after is None else [after]))
    k = len(sem_shapes)
    sems = [tuple(res[2 * g:2 * g + 2]) for g in range(len(groups))]
    return sems, list(res[k:k + n]), list(res[k + n:k + 2 * n]), res[-1]


def chip_copies_wait(srcs, lands, sems, after, scatter, name):
    n = len(srcs)
    after = list(after) if isinstance(after, (list, tuple)) else [after]

    def body(*refs):
        src, land = refs[:n], refs[n:2 * n]
        pair = refs[2 * n:2 * n + 2]
        x, y, _ = _place()
        for i in range(n):
            for j, (px, py) in enumerate(_other_chips(x, y)):
                copy = _chip_copy(src, land, pair, i, j, 2 * px + py, scatter)
                copy.wait_send()
                copy.wait_recv()

    res = pl.pallas_call(
        body, name=name,
        out_shape=[pltpu.HBM(a.shape, a.dtype) for a in (*srcs, *lands)],
        in_specs=[*_hbm(2 * n), SEM, SEM] + [UNREAD] * len(after),
        out_specs=_hbm(2 * n),
        input_output_aliases={i: i for i in range(2 * n)},
        compiler_params=pltpu.CompilerParams(has_side_effects=SIDE_EFFECT),
    )(*srcs, *lands, *sems, *after)
    return list(res[n:])


def small_all_gather(small, name):
    flips = [(fx, fy, fc) for fx in (0, 1) for fy in (0, 1) for fc in (0, 1)][1:]

    def body(in_ref, out_ref, send_sems, recv_sems, local_sem):
        x, y, c = _place()
        me = 4 * x + 2 * y + c

        def copy(k, slot):
            fx, fy, fc = flips[k]
            return pltpu.make_async_remote_copy(
                src_ref=in_ref, dst_ref=out_ref.at[slot], send_sem=send_sems.at[k], recv_sem=recv_sems.at[k],
                device_id=(x ^ fx, y ^ fy, c ^ fc), device_id_type=MESH)

        local = pltpu.make_async_copy(in_ref, out_ref.at[me], local_sem)
        local.start()
        for k in range(len(flips)):
            copy(k, me).start()
        for k, (fx, fy, fc) in enumerate(flips):
            copy(k, 4 * (x ^ fx) + 2 * (y ^ fy) + (c ^ fc)).wait()
        local.wait()

    return pl.pallas_call(
        body, name=name, in_specs=_hbm(1), out_specs=_hbm(1)[0],
        out_shape=jax.ShapeDtypeStruct((N_DEV,) + small.shape, small.dtype),
        scratch_shapes=[pltpu.SemaphoreType.DMA((len(flips),)), pltpu.SemaphoreType.DMA((len(flips),)),
                        pltpu.SemaphoreType.DMA],
    )(small)


def _sibling_copy(src, land, sems, i):
    x, y, c = _place()
    return pltpu.make_async_remote_copy(
        src_ref=src[i], dst_ref=land[i], send_sem=sems[0].at[i], recv_sem=sems[1].at[i],
        device_id=(x, y, 1 - c), device_id_type=MESH)


def sibling_copies_start(parts, name):
    n = len(parts)
    lands = [lax.empty(p.shape, p.dtype) for p in parts]

    def body(*refs):
        src, land, sems, token = refs[:n], refs[n:2 * n], refs[2 * n:2 * n + 2], refs[-1]
        for i in range(n):
            _sibling_copy(src, land, sems, i).start()
        token[...] = jnp.zeros_like(token)

    res = pl.pallas_call(
        body, name=name,
        out_shape=(pltpu.SemaphoreType.DMA((n,)), pltpu.SemaphoreType.DMA((n,)),
                   *[pltpu.HBM(a.shape, a.dtype) for a in (*parts, *lands)], jax.ShapeDtypeStruct((8, 128), F32)),
        in_specs=_hbm(2 * n),
        out_specs=(SEM, SEM, *_hbm(2 * n), pl.BlockSpec(memory_space=pltpu.VMEM)),
        input_output_aliases={i: 2 + i for i in range(2 * n)},
        compiler_params=pltpu.CompilerParams(has_side_effects=SIDE_EFFECT),
    )(*[pltpu.with_memory_space_constraint(a, pltpu.HBM) for a in (*parts, *lands)])
    return tuple(res[:2]), list(res[2:2 + n]), list(res[2 + n:2 + 2 * n]), res[-1]


def sibling_copies_wait(parts, lands, sems, after, name):
    n = len(parts)

    def body(*refs):
        src, land, sems = refs[:n], refs[n:2 * n], refs[2 * n:2 * n + 2]
        for i in range(n):
            copy = _sibling_copy(src, land, sems, i)
            copy.wait_send()
            copy.wait_recv()

    res = pl.pallas_call(
        body, name=name,
        out_shape=[pltpu.HBM(a.shape, a.dtype) for a in (*parts, *lands)],
        in_specs=[*_hbm(2 * n), SEM, SEM, UNREAD],
        out_specs=_hbm(2 * n),
        input_output_aliases={i: i for i in range(2 * n)},
        compiler_params=pltpu.CompilerParams(has_side_effects=SIDE_EFFECT),
    )(*parts, *lands, *sems, after)
    return list(res[n:])


def _row_tile(rows):
    for t in (256, 176, 128, 64, 32, 16, 8):
        if rows % t == 0:
            return t
    return rows


def chip_partial_sum(me, own_sm, recv, name):
    _, rows, cols = own_sm.shape
    tr = _row_tile(rows)

    def body(me_ref, own_ref, r1, r2, r3, o_ref):
        o_ref[...] = own_ref[...] + r1[...].astype(F32) + r2[...].astype(F32) + r3[...].astype(F32)

    def slot(d):
        return pl.BlockSpec((None, tr, cols), lambda i, me_ref: ((me_ref[0] + d) % N_CHIPS, i, 0))

    return pl.pallas_call(
        body, name=name,
        grid_spec=pltpu.PrefetchScalarGridSpec(
            num_scalar_prefetch=1, grid=(rows // tr,),
            in_specs=[slot(0), slot(1), slot(2), slot(3)],
            out_specs=pl.BlockSpec((tr, cols), lambda i, me_ref: (i, 0))),
        out_shape=jax.ShapeDtypeStruct((rows, cols), F32),
        compiler_params=_params("parallel"),
    )(me, own_sm, recv, recv, recv)


def _adamw(w, g, m, v):
    m = ADAM_B1 * m + (1.0 - ADAM_B1) * g
    v = ADAM_B2 * v + (1.0 - ADAM_B2) * (g * g)
    m_hat = m / (1.0 - ADAM_B1 ** ADAM_STEP)
    v_hat = v / (1.0 - ADAM_B2 ** ADAM_STEP)
    delta = -ADAM_LR * (m_hat / (jnp.sqrt(v_hat) + ADAM_EPS) + ADAM_WD * w)
    return delta, m, v


def adamw_pair(part, sib, w, m, v, name):
    rows, cols = w.shape
    tr = _row_tile(rows)

    def body(p_ref, s_ref, w_ref, m_ref, v_ref, g_ref, d_ref, nm_ref, nv_ref):
        g = p_ref[...] + s_ref[...]
        g_ref[...] = g
        d_ref[...], nm_ref[...], nv_ref[...] = _adamw(w_ref[...], g, m_ref[...], v_ref[...])

    spec = pl.BlockSpec((tr, cols), lambda i: (i, 0))
    return pl.pallas_call(
        body, name=name, grid=(rows // tr,), in_specs=[spec] * 5, out_specs=[spec] * 4,
        out_shape=[jax.ShapeDtypeStruct((rows, cols), F32)] * 4,
        compiler_params=_params("parallel"),
    )(part, sib, w, m, v)


def adamw_small(g_all, w, m, v, name):
    def body(ga_ref, w_ref, m_ref, v_ref, g_ref, d_ref, nm_ref, nv_ref):
        g = ga_ref[0]
        for k in range(1, N_DEV):
            g = g + ga_ref[k]
        g_ref[...] = g
        d_ref[...], nm_ref[...], nv_ref[...] = _adamw(w_ref[...], g, m_ref[...], v_ref[...])

    return pl.pallas_call(
        body, name=name, out_shape=[jax.ShapeDtypeStruct(w.shape, F32)] * 4,
    )(g_all, w, m, v)


WEIGHTS = ("ffn1_norm_pre", "ffn1_w_in", "ffn1_w_out", "ffn1_norm_post", "mix_norm_pre", "w_in", "sinks",
           "mem_norm", "w_mem_kv", "w_gate", "b_gate", "w_o_a", "w_o_b", "w_o_m", "w_out", "mix_norm_post",
           "ffn2_norm_pre", "ffn2_w_in", "ffn2_w_out", "ffn2_norm_post")
GATHER_STAGES = (("ffn1_in", "ffn1_out"), ("mix_in",), ("mix_rest", "ffn2"))
GATHER_GROUPS = {"ffn1_in": ("ffn1_w_in",), "ffn1_out": ("ffn1_w_out",),
                 "mix_in": ("w_in", "w_gate"), "mix_rest": ("w_mem_kv", "w_o_a", "w_o_b", "w_o_m", "w_out"),
                 "ffn2": ("ffn2_w_in", "ffn2_w_out")}
GROUPS = {"ffn1_in": ("ffn1_w_in",), "ffn1_out": ("ffn1_w_out",),
          "mix": ("w_in", "w_gate", "w_mem_kv", "w_o_a", "w_o_b", "w_o_m", "w_out"),
          "ffn2_in": ("ffn2_w_in",), "ffn2_out": ("ffn2_w_out",)}
COLUMN_SHARDED = ("ffn1_w_in", "ffn2_w_in", "w_in", "w_gate", "w_o_a", "w_o_b", "w_o_m")
KEPT_SHARD_MAJOR = ("ffn1_w_in", "ffn2_w_in", "w_gate")
GAINS = ("ffn1_norm_pre", "ffn1_norm_post", "mix_norm_pre", "mem_norm", "mix_norm_post", "ffn2_norm_pre",
         "ffn2_norm_post")
SMALL_ROWS = 16


def _pack_small(t):
    sinks = jnp.pad(t["sinks"], ((0, 0), (0, D_MODEL - t["sinks"].shape[1])))
    rows = [t[k] for k in GAINS] + [t["b_gate"].reshape(3, D_MODEL), sinks]
    packed = jnp.concatenate(rows, axis=0)
    return jnp.pad(packed, ((0, SMALL_ROWS - packed.shape[0]), (0, 0)))


def _unpack_small(p):
    out = {k: p[i:i + 1] for i, k in enumerate(GAINS)}
    out["b_gate"] = p[7:10].reshape(1, 3 * D_MODEL)
    out["sinks"] = p[10:11, :4]
    return out


def kernel(x, mem, ffn1_norm_pre, ffn1_w_in, ffn1_w_out, ffn1_norm_post, mix_norm_pre, w_in, sinks, mem_norm, w_mem_kv, w_gate, b_gate, w_o_a, w_o_b, w_o_m, w_out, mix_norm_post, ffn2_norm_pre, ffn2_w_in, ffn2_w_out, ffn2_norm_post, loss_target, m_ffn1_norm_pre, m_ffn1_w_in, m_ffn1_w_out, m_ffn1_norm_post, m_mix_norm_pre, m_w_in, m_sinks, m_mem_norm, m_w_mem_kv, m_w_gate, m_b_gate, m_w_o_a, m_w_o_b, m_w_o_m, m_w_out, m_mix_norm_post, m_ffn2_norm_pre, m_ffn2_w_in, m_ffn2_w_out, m_ffn2_norm_post, v_ffn1_norm_pre, v_ffn1_w_in, v_ffn1_w_out, v_ffn1_norm_post, v_mix_norm_pre, v_w_in, v_sinks, v_mem_norm, v_w_mem_kv, v_w_gate, v_b_gate, v_w_o_a, v_w_o_b, v_w_o_m, v_w_out, v_mix_norm_post, v_ffn2_norm_pre, v_ffn2_w_in, v_ffn2_w_out, v_ffn2_norm_post):
    given = dict(locals())
    wt = {k: given[k] for k in WEIGHTS}
    mom = {k: given["m_" + k] for k in WEIGHTS}
    var = {k: given["v_" + k] for k in WEIGHTS}
    chip = (2 * lax.axis_index("x") + lax.axis_index("y")).astype(jnp.int32)
    me = chip.reshape(1)

    def landing_zone(own):
        return lax.dynamic_update_slice_in_dim(lax.empty((N_CHIPS,) + own.shape, own.dtype), own[None], chip, 0)

    started = {}
    tokens = []

    def stage_keys(stage):
        return [k for g in GATHER_STAGES[stage] for k in GATHER_GROUPS[g]]

    def prepare(stage):
        shards = [(wt[k][0] + tokens[0][0, 0] if tokens else wt[k][0]).astype(BF16) for k in stage_keys(stage)]
        return shards, [landing_zone(s) for s in shards]

    def start_gather(stage, after):
        groups, keys = GATHER_STAGES[stage], stage_keys(stage)
        members = [[keys.index(k) for k in GATHER_GROUPS[g]] for g in groups]
        sems, shards, lands, token = chip_copies_start(
            *prepared[stage], members, False, f"weight_gather_start_{stage}", after)
        tokens.append(token)
        for g, idx, pair in zip(groups, members, sems):
            started[g] = ([shards[i] for i in idx], [lands[i] for i in idx], pair)

    prepared = {0: prepare(0)}
    start_gather(0, None)
    prepared.update({stage: prepare(stage) for stage in range(1, len(GATHER_STAGES))})

    def weights_of(group, after):
        if group == GATHER_STAGES[0][0]:
            after = [after] + [a for stage in range(1, len(GATHER_STAGES)) for part in prepared[stage] for a in part]
        got = chip_copies_wait(*started[group], after, False, f"weight_gather_wait_{group}")
        stage = [s + 1 for s, groups in enumerate(GATHER_STAGES[:-1]) if groups[0] == group]
        if stage:
            start_gather(stage[0], got[0])
        full = {}
        for k, g in zip(GATHER_GROUPS[group], got):
            if k in COLUMN_SHARDED:
                if k in ("ffn1_w_in", "ffn2_w_in"):
                    g = jnp.stack([g[0], g[2], g[1], g[3]])
                full[k] = jnp.swapaxes(g, 0, 1).reshape(g.shape[1], N_CHIPS * g.shape[2])
                if k == "w_in":
                    full[k] = to_kernel_heads(full[k])
            else:
                full[k] = g.reshape(N_CHIPS * g.shape[1], g.shape[2])
        return full

    in_flight = {}

    def send_grads(group, grads):
        def shard_major(k, g):
            if k in KEPT_SHARD_MAJOR:
                return g
            if k in COLUMN_SHARDED:
                return jnp.swapaxes(g.reshape(g.shape[0], N_CHIPS, g.shape[1] // N_CHIPS), 0, 1)
            return g.reshape(N_CHIPS, g.shape[0] // N_CHIPS, g.shape[1])

        own, wire = [], []
        for k in GROUPS[group]:
            g, rounded = grads[k] if isinstance(grads[k], (tuple, list)) else (grads[k], None)
            g = shard_major(k, from_kernel_heads(g) if k == "w_in" else g)
            own.append(g)
            wire.append(g.astype(BF16) if rounded is None else shard_major(k, rounded))
        zones = [lax.empty(b.shape, b.dtype) for b in wire]
        pair, wire, zones, sent = chip_copies_start(
            wire, zones, [list(range(len(wire)))], True, f"grad_scatter_start_{group}")
        in_flight[group] = (own, wire, zones, pair[0], sent)
        return sent

    gains = {k: wt[k] for k in GAINS}
    sq, dx, grads = layer_step(
        x[0], mem[0], loss_target[0], gains, sinks[0], b_gate, weights_of, send_grads, tokens[0][0, 0])
    loss = lax.psum(0.5 * jnp.sum(sq) / D_MODEL, ("x", "y", "c"))

    res = {}
    after = in_flight["ffn1_out"][4]
    swaps = []
    for stage in (("ffn2_in", "ffn2_out", "mix", "ffn1_in"), ("ffn1_out",)):
        names, parts = [], []
        for group in stage:
            own, wire, zones, pair, _ = in_flight[group]
            received = chip_copies_wait(wire, zones, pair, after, True, f"grad_scatter_wait_{group}")
            for k, g, r in zip(GROUPS[group], own, received):
                names.append(k)
                parts.append(chip_partial_sum(me, g, r, f"{k}_chip_sum"))
        pair, parts, lands, after = sibling_copies_start(parts, f"sibling_start_{stage[-1]}")
        swaps.append((stage[-1], names, parts, lands, pair))
    small_all = small_all_gather(_pack_small(grads), "small_grad_gather")
    packed = adamw_small(small_all, _pack_small(wt), _pack_small(mom), _pack_small(var), "small_adamw")
    after = packed[0]
    for tag, names, parts, lands, pair in swaps:
        sibs = sibling_copies_wait(parts, lands, pair, after, f"sibling_wait_{tag}")
        for k, p, s in zip(names, parts, sibs):
            res[k] = [t[None] for t in adamw_pair(p, s, wt[k][0], mom[k][0], var[k][0], f"{k}_adamw")]
        after = res[names[-1]][0]
    for idx, p in enumerate(packed):
        for k, t in _unpack_small(p).items():
            res.setdefault(k, [None] * 4)[idx] = t

    return (loss, dx[None], *[res[k][0] for k in WEIGHTS], *[res[k][1] for k in WEIGHTS],
            *[res[k][2] for k in WEIGHTS], *[res[k][3] for k in WEIGHTS])
```

```python
import functools

import jax
import jax.numpy as jnp
from jax import lax
from jax.experimental import pallas as pl
from jax.experimental.pallas import tpu as pltpu

F32 = jnp.float32
BF16 = jnp.bfloat16

D_MODEL = 1024
D_FF = 2816
HEAD = 128
N_CHIPS = 4
N_DEV = 8
EPS = 1e-6
NEG_INF = -1e30
ROPE_THETA = 10000.0
ATT_SCALE = HEAD ** -0.5

ADAM_LR = 0.001
ADAM_B1 = 0.9
ADAM_B2 = 0.999
ADAM_EPS = 1e-08
ADAM_WD = 0.01
ADAM_STEP = 10

VMEM_LIMIT = 52 * 2 ** 20
VMEM_LIMIT_LARGE = 60 * 2 ** 20
MESH = pl.DeviceIdType.MESH

QKV_W = 3840
DIL = ((128, 1), (512, 4), (2048, 16))
B_BASE, MQ, A_BASE = 0, 8, 12
_AQ, _AK, _AV, _BQ, _BK, _BV, _MQ = 0, 6, 12, 18, 22, 24, 26
HEAD_ORDER = tuple(
    [h for j in range(2) for h in (_BQ + 2 * j, _BQ + 2 * j + 1, _BK + j, _BV + j)]
    + [_MQ + i for i in range(4)]
    + [h for g in range(3) for i in range(2) for h in (_AQ + 2 * g + i, _AK + 2 * g + i, _AV + 2 * g + i)])
ROTARY_HEADS = tuple(p for p, h in enumerate(HEAD_ORDER) if h < _AV or _BQ <= h < _BV)


def to_kernel_heads(w):
    return jnp.concatenate([w[..., h * HEAD:(h + 1) * HEAD] for h in HEAD_ORDER], axis=-1)


def from_kernel_heads(w):
    place = {h: p for p, h in enumerate(HEAD_ORDER)}
    return jnp.concatenate([w[..., place[h] * HEAD:(place[h] + 1) * HEAD] for h in range(len(HEAD_ORDER))], axis=-1)

TM = 512
FF_T = D_FF // 2


def _params(*sem):
    return pltpu.CompilerParams(dimension_semantics=sem, vmem_limit_bytes=VMEM_LIMIT)


def _dot(a, b):
    return jnp.dot(a, b, preferred_element_type=F32)


def _dot_nt(a, b):
    return lax.dot_general(a, b, (((1,), (1,)), ((), ())), preferred_element_type=F32)


def _dot_tn(a, b):
    return lax.dot_general(a, b, (((0,), (0,)), ((), ())), preferred_element_type=F32)


def _rstd(x):
    return lax.rsqrt(jnp.mean(x * x, axis=-1, keepdims=True) + EPS)


def _sigmoid(x):
    return 0.5 * jnp.tanh(0.5 * x) + 0.5


def _ffn_perm(k):
    return (k % 2) * 2 + k // 2


UNREAD = pl.BlockSpec(memory_space=pl.ANY)


def _resident(arr):
    return pl.BlockSpec(arr.shape, lambda *_: (0,) * arr.ndim, pipeline_mode=pl.Buffered(1))


def rms_scale(x, g, name, after):
    T, D = x.shape
    tm = 1024

    def body(x_ref, g_ref, _, o_ref):
        v = x_ref[...]
        o_ref[...] = (v * _rstd(v) * g_ref[...]).astype(BF16)

    spec = pl.BlockSpec((tm, D), lambda i: (i, 0))
    return pl.pallas_call(
        body, name=name, grid=(T // tm,), in_specs=[spec, _resident(g), UNREAD], out_specs=spec,
        out_shape=jax.ShapeDtypeStruct((T, D), BF16), compiler_params=_params("parallel"),
    )(x, g, after)


def ffn_in(h, g, w, name, xn=None):
    T, D = h.shape
    normed = xn is not None

    def body(h_ref, g_ref, w_ref, *outs):
        if normed:
            xn, (gu_ref, a_ref) = h_ref[...], outs
        else:
            xn_ref, gu_ref, a_ref = outs
            x = h_ref[...]
            xn = (x * _rstd(x) * g_ref[...]).astype(BF16)
            xn_ref[...] = xn
        for j in range(2):
            gu = _dot(xn, w_ref[:, j * 2 * FF_T:(j + 1) * 2 * FF_T])
            gu_ref[:, j * 2 * FF_T:(j + 1) * 2 * FF_T] = gu.astype(BF16)
            gate, up = gu[:, :FF_T], gu[:, FF_T:]
            a_ref[:, j * FF_T:(j + 1) * FF_T] = (gate * _sigmoid(gate) * up).astype(BF16)

    def rows(width):
        return pl.BlockSpec((TM, width), lambda i: (i, 0))

    res = pl.pallas_call(
        body, name=name,
        grid=(T // TM,),
        in_specs=[rows(D), _resident(g), _resident(w)],
        out_specs=[rows(D)] * (not normed) + [rows(2 * D_FF), rows(D_FF)],
        out_shape=[jax.ShapeDtypeStruct((T, D), BF16)] * (not normed)
                  + [jax.ShapeDtypeStruct((T, 2 * D_FF), BF16), jax.ShapeDtypeStruct((T, D_FF), BF16)],
        compiler_params=_params("parallel"),
    )(xn if normed else h, g, w)
    return (xn, *res) if normed else tuple(res)


def mm_norm_res(a, w, h_in, g, coef, name, target=None):
    T, K = a.shape
    D = w.shape[1]
    final = target is not None
    tm = TM if final else min(2 * TM, T)

    def body(*refs):
        if final:
            a_ref, w_ref, h_ref, g_ref, t_ref, f_ref, o_ref, l_ref = refs
        else:
            a_ref, w_ref, h_ref, g_ref, f_ref, o_ref = refs
        f = _dot(a_ref[...], w_ref[...])
        f_ref[...] = f
        y = h_ref[...] + coef * (f * _rstd(f) * g_ref[...])
        if final:
            err = y - t_ref[...]
            o_ref[...] = err * (1.0 / D)

            @pl.when(pl.program_id(0) == 0)
            def _():
                l_ref[...] = jnp.zeros_like(l_ref)

            sq = jnp.sum((err * err).reshape(tm // 8, 8, D), axis=0)
            l_ref[...] += functools.reduce(jnp.add, [sq[:, c:c + HEAD] for c in range(0, D, HEAD)])
        else:
            o_ref[...] = y

    row = pl.BlockSpec((tm, D), lambda i: (i, 0))
    in_specs = [pl.BlockSpec((tm, K), lambda i: (i, 0)),
                _resident(w),
                row, pl.BlockSpec((1, D), lambda i: (0, 0))]
    out_specs = [row, row]
    out_shape = [jax.ShapeDtypeStruct((T, D), F32), jax.ShapeDtypeStruct((T, D), F32)]
    args = [a, w, h_in, g]
    if final:
        in_specs.append(row)
        args.append(target)
        out_specs.append(pl.BlockSpec((8, 128), lambda i: (0, 0)))
        out_shape.append(jax.ShapeDtypeStruct((8, 128), F32))
    return pl.pallas_call(
        body, name=name, grid=(T // tm,), in_specs=in_specs, out_specs=out_specs, out_shape=out_shape,
        compiler_params=_params("arbitrary"),
    )(*args)


def _rope(x, cos, sin_signed):
    return x * cos + pltpu.roll(x, HEAD // 2, axis=1) * sin_signed


def _unrope(x, cos, sin_signed):
    return x * cos - pltpu.roll(x, HEAD // 2, axis=1) * sin_signed


def mix_in(h, g, w, w_gate, b_gate, cos, sin_signed, name):
    T, D = h.shape
    tn = 768

    def body(h_ref, g_ref, w_ref, wg_ref, b_ref, c_ref, s_ref, u_ref, o_ref, gt_ref):
        x = h_ref[...]
        u = (x * _rstd(x) * g_ref[...]).astype(BF16)
        u_ref[...] = u
        c, s = c_ref[...], s_ref[...]
        for j in range(QKV_W // tn):
            acc = _dot(u, w_ref[:, j * tn:(j + 1) * tn])
            for hd in range(tn // HEAD):
                head = j * (tn // HEAD) + hd
                part = acc[:, hd * HEAD:(hd + 1) * HEAD]
                if head in ROTARY_HEADS:
                    part = _rope(part, c, s)
                o_ref[:, head * HEAD:(head + 1) * HEAD] = part.astype(BF16)
        for j in range(w_gate.shape[1] // tn):
            cols = slice(j * tn, (j + 1) * tn)
            gt_ref[:, cols] = _sigmoid(_dot(u, wg_ref[:, cols]) + b_ref[:, cols]).astype(BF16)

    def rows(width):
        return pl.BlockSpec((TM, width), lambda i: (i, 0))

    return pl.pallas_call(
        body, name=name,
        grid=(T // TM,),
        in_specs=[rows(D), _resident(g), _resident(w), _resident(w_gate), _resident(b_gate), rows(HEAD), rows(HEAD)],
        out_specs=[rows(D), rows(QKV_W), rows(w_gate.shape[1])],
        out_shape=[jax.ShapeDtypeStruct((T, D), BF16), jax.ShapeDtypeStruct((T, QKV_W), BF16),
                   jax.ShapeDtypeStruct((T, w_gate.shape[1]), BF16)],
        compiler_params=_params("parallel"),
    )(h, g, w, w_gate, b_gate, cos, sin_signed)


def gate_merge_out(gt, o_a, o_b, o_m, w_a, w_b, w_m, w_out, h_in, g, name):
    T = gt.shape[0]
    D = D_MODEL

    def body(gt_ref, oa_ref, ob_ref, om_ref, wa_ref, wb_ref, wm_ref, wo_ref, h_ref, g_ref, m_ref, f_ref, o_ref):
        acc = gt_ref[:, :D].astype(F32) * _dot(oa_ref[...], wa_ref[...])
        acc += gt_ref[:, D:2 * D].astype(F32) * _dot(ob_ref[...], wb_ref[...])
        acc += gt_ref[:, 2 * D:].astype(F32) * _dot(om_ref[...], wm_ref[...])
        merged = acc.astype(BF16)
        m_ref[...] = merged
        f = _dot(merged, wo_ref[...])
        f_ref[...] = f
        o_ref[...] = h_ref[...] + f * _rstd(f) * g_ref[...]

    def rows(width):
        return pl.BlockSpec((TM, width), lambda i: (i, 0))

    return pl.pallas_call(
        body, name=name, grid=(T // TM,),
        in_specs=[rows(3 * D), rows(o_a.shape[1]), rows(o_b.shape[1]), rows(o_m.shape[1]),
                  _resident(w_a), _resident(w_b), _resident(w_m), _resident(w_out), rows(D), _resident(g)],
        out_specs=[rows(D), rows(D), rows(D)],
        out_shape=[jax.ShapeDtypeStruct((T, D), BF16), jax.ShapeDtypeStruct((T, D), F32),
                   jax.ShapeDtypeStruct((T, D), F32)],
        compiler_params=_params("parallel"),
    )(gt, o_a, o_b, o_m, w_a, w_b, w_m, w_out, h_in, g)


def _band_rows(start, r):
    return pl.ds(start, HEAD) if r == 1 else pl.ds(start, HEAD, stride=r)


def _band_mask(max_dist, first_has_prev):
    row = lax.broadcasted_iota(jnp.int32, (HEAD, 2 * HEAD), 0)
    col = lax.broadcasted_iota(jnp.int32, (HEAD, 2 * HEAD), 1)
    dist = row + HEAD - col
    band = (dist >= 0) & (dist <= max_dist)
    return band, band & (col >= jnp.where(first_has_prev, 0, HEAD))


def _stack(parts):
    return parts[0] if len(parts) == 1 else jnp.concatenate(parts, axis=0)


def _band_specs(BT, SB, nsub, base, grp):
    stride = grp + 2

    def cur(off, width):
        return pl.BlockSpec((BT, width * HEAD), lambda h, i: (i, (base + h * stride + off) // width))

    def prev(off):
        return pl.BlockSpec((SB, HEAD), lambda h, i: (jnp.maximum(i * nsub - 1, 0), base + h * stride + off))

    return cur(0, grp), cur(grp, 1), prev(grp), cur(grp + 1, 1), prev(grp + 1)


def band_fwd(qkv, sinks, *, r, base, hkv, grp, max_dist, out_dtype, name, merge=None):
    T, W = qkv.shape
    SB = HEAD * r
    BT = min(2048, T)
    nsub, nib = BT // SB, T // BT
    hq = hkv * grp
    heads = [slice(g * HEAD, (g + 1) * HEAD) for g in range(grp)]
    others = [] if merge is None else [*merge[0], *merge[1]]

    def body(sink_ref, q_ref, kc_ref, kp_ref, vc_ref, vp_ref, *rest):
        joint_o, joint_l = rest[len(others):len(others) + 2]
        qf, kf, vf = rest[len(others) + 2:len(others) + 5]
        o_ref, l_ref = rest[len(others) + 5:] if others else (joint_o, joint_l)
        kvh, ib = pl.program_id(0), pl.program_id(1)
        qf[...] = q_ref[...].astype(F32)
        kf[:SB] = kp_ref[...].astype(F32)
        kf[SB:] = kc_ref[...].astype(F32)
        vf[:SB] = vp_ref[...].astype(F32)
        vf[SB:] = vc_ref[...].astype(F32)
        band, band_first = _band_mask(max_dist, ib > 0)
        for c in range(r):
            k_old, v_old = kf[_band_rows(c, r)], vf[_band_rows(c, r)]
            for j in range(nsub):
                mask = band_first if j == 0 else band
                rows = _band_rows(j * SB + c, r)
                k_own, v_own = kf[_band_rows((j + 1) * SB + c, r)], vf[_band_rows((j + 1) * SB + c, r)]
                kcat = jnp.concatenate([k_old, k_own], axis=0).astype(BF16)
                vcat = jnp.concatenate([v_old, v_own], axis=0).astype(BF16)
                k_old, v_old = k_own, v_own
                s_all = _dot_nt(_stack([qf[rows, cols] for cols in heads]).astype(BF16), kcat) * ATT_SCALE
                probs, tots = [], []
                for g, cols in enumerate(heads):
                    s = jnp.where(mask, s_all[cols], NEG_INF)
                    sk = sink_ref[kvh * grp + g]
                    m = jnp.maximum(jnp.max(s, axis=-1, keepdims=True), sk)
                    p = jnp.exp(s - m)
                    tot = jnp.sum(p, axis=-1, keepdims=True) + jnp.exp(sk - m)
                    probs.append(p.astype(BF16))
                    tots.append(tot)
                    l_ref[rows, cols] = jnp.broadcast_to(m + jnp.log(tot), (HEAD, HEAD))
                o_all = _dot(_stack(probs), vcat)
                for g, cols in enumerate(heads):
                    o_ref[rows, cols] = (o_all[cols] / tots[g]).astype(o_ref.dtype)

        if others:
            half = len(others) // 2
            outs = [ref[...] for ref in rest[:half]] + [o_ref[...]]
            logs = [ref[...] for ref in rest[half:len(others)]] + [l_ref[...]]
            top = functools.reduce(jnp.maximum, logs)
            weights = [jnp.exp(lg - top) for lg in logs]
            total = functools.reduce(jnp.add, weights)
            mixed = functools.reduce(jnp.add, [wgt * out for wgt, out in zip(weights, outs)])
            joint_o[...] = (mixed / total).astype(out_dtype)
            joint_l[...] = top + jnp.log(total)

    out_spec = pl.BlockSpec((BT, grp * HEAD), lambda h, i: (i, h))
    own = [pltpu.VMEM((BT, grp * HEAD), F32)] * 2 if others else []
    return pl.pallas_call(
        body, name=name, grid=(hkv, nib),
        in_specs=[pl.BlockSpec(memory_space=pltpu.SMEM), *_band_specs(BT, SB, nsub, base, grp)]
                 + [out_spec] * len(others),
        out_specs=[out_spec, out_spec],
        out_shape=[jax.ShapeDtypeStruct((T, hq * HEAD), out_dtype), jax.ShapeDtypeStruct((T, hq * HEAD), F32)],
        scratch_shapes=[pltpu.VMEM((BT, grp * HEAD), F32), pltpu.VMEM((SB + BT, HEAD), F32),
                        pltpu.VMEM((SB + BT, HEAD), F32)] + own,
        compiler_params=_params("parallel", "arbitrary"),
    )(sinks, qkv, qkv, qkv, qkv, qkv, *others)


def band_bwd(qkv, dqkv, do, o, lse, cos, sin_signed, sinks, *, r, base, hkv, grp, max_dist, name):
    T, W = qkv.shape
    SB = HEAD * r
    BT = min(max(2048, 2 * SB), T)
    nsub, nib = BT // SB, T // BT
    nblk = T // SB
    with_sink = sinks is not None
    heads = [slice(g * HEAD, (g + 1) * HEAD) for g in range(grp)]

    def body(*refs):
        if with_sink:
            sink_ref, refs = refs[0], refs[1:]
        (q_ref, kc_ref, kp_ref, vc_ref, vp_ref, qn_ref, do_ref, don_ref, o_ref, on_ref, l_ref, ln_ref,
         c_ref, s_ref, _) = refs[:15]
        out_ref = refs[15]
        ds_ref = refs[16] if with_sink else None
        qf, dof, of, kf, vf, dqf, dkf, dvf = refs[-8:]
        kvh, ib = pl.program_id(0), pl.program_id(1)
        for buf, cur_ref, nxt_ref in ((qf, q_ref, qn_ref), (dof, do_ref, don_ref), (of, o_ref, on_ref)):
            buf[:BT] = cur_ref[...].astype(F32)
            buf[BT:] = nxt_ref[...].astype(F32)
        kf[:SB] = kp_ref[...].astype(F32)
        kf[SB:] = kc_ref[...].astype(F32)
        vf[:SB] = vp_ref[...].astype(F32)
        vf[SB:] = vc_ref[...].astype(F32)
        band, band_first = _band_mask(max_dist, ib > 0)
        if with_sink:
            @pl.when(ib == 0)
            def _():
                ds_ref[...] = jnp.zeros_like(ds_ref)

        def grads(rows, logzs, keys, vals, mask):
            q = _stack([qf[rows, cols] for cols in heads]).astype(BF16)
            dout = _stack([dof[rows, cols] for cols in heads]).astype(BF16)
            s_all = _dot_nt(q, keys) * ATT_SCALE
            dp_all = _dot_nt(dout, vals)
            probs, dss, deltas = [], [], []
            for g, cols in enumerate(heads):
                delta = jnp.sum(dof[rows, cols] * of[rows, cols], axis=-1, keepdims=True)
                p = jnp.exp(jnp.where(mask, s_all[cols], NEG_INF) - logzs[g][:, :1])
                probs.append(p.astype(BF16))
                dss.append((p * (dp_all[cols] - delta) * ATT_SCALE).astype(BF16))
                deltas.append(delta)
            return q, dout, _stack(probs), _stack(dss), deltas

        row = lax.broadcasted_iota(jnp.int32, (HEAD, HEAD), 0)
        col = lax.broadcasted_iota(jnp.int32, (HEAD, HEAD), 1)
        reach = col >= row + jnp.where(ib < nib - 1, HEAD - max_dist, 2 * HEAD)
        for c in range(r):
            k_old, v_old = kf[_band_rows(c, r)], vf[_band_rows(c, r)]
            dk_own = dv_own = None
            for j in range(nsub):
                rows = _band_rows(j * SB + c, r)
                k_own, v_own = kf[_band_rows((j + 1) * SB + c, r)], vf[_band_rows((j + 1) * SB + c, r)]
                kcat = jnp.concatenate([k_old, k_own], axis=0).astype(BF16)
                vcat = jnp.concatenate([v_old, v_own], axis=0).astype(BF16)
                logzs = [l_ref[rows, cols] for cols in heads]
                q, dout, p, ds, deltas = grads(rows, logzs, kcat, vcat, band_first if j == 0 else band)
                dq = _dot(ds, kcat)
                for g, cols in enumerate(heads):
                    dqf[rows, cols] = dq[cols]
                    if with_sink:
                        p_sink = jnp.exp(sink_ref[kvh * grp + g] - logzs[g][:, :1])
                        ds_ref[g * 8:(g + 1) * 8] += jnp.sum(p_sink * deltas[g])
                dk, dv = _dot_tn(ds, q), _dot_tn(p, dout)
                if j > 0:
                    done = _band_rows((j - 1) * SB + c, r)
                    dkf[done] = dk_own + dk[:HEAD]
                    dvf[done] = dv_own + dv[:HEAD]
                dk_own, dv_own = dk[HEAD:], dv[HEAD:]
                k_old, v_old = k_own, v_own
            logzs = [ln_ref[_band_rows(c, r), cols] for cols in heads]
            q, dout, p, ds, _ = grads(_band_rows(BT + c, r), logzs, k_old.astype(BF16), v_old.astype(BF16), reach)
            done = _band_rows((nsub - 1) * SB + c, r)
            dkf[done] = dk_own + _dot_tn(ds, q)
            dvf[done] = dv_own + _dot_tn(p, dout)

        cs, sn = c_ref[...], s_ref[...]
        for cols in heads:
            out_ref[:, cols] = _unrope(dqf[:, cols], cs, sn).astype(BF16)
        out_ref[:, grp * HEAD:(grp + 1) * HEAD] = _unrope(dkf[...], cs, sn).astype(BF16)
        out_ref[:, (grp + 1) * HEAD:] = dvf[...].astype(BF16)

    def nxt_row(i):
        return jnp.minimum((i + 1) * nsub, nblk - 1)

    stride = grp + 2
    q_next = pl.BlockSpec((SB, grp * HEAD), lambda h, i: (nxt_row(i), (base + h * stride) // grp))
    head_cur = pl.BlockSpec((BT, grp * HEAD), lambda h, i: (i, h))
    head_next = pl.BlockSpec((SB, grp * HEAD), lambda h, i: (nxt_row(i), h))
    table = pl.BlockSpec((BT, HEAD), lambda h, i: (i, 0))

    in_specs = [*_band_specs(BT, SB, nsub, base, grp), q_next,
                head_cur, head_next, head_cur, head_next, head_cur, head_next, table, table, UNREAD]
    args = [qkv, qkv, qkv, qkv, qkv, qkv, do, do, o, o, lse, lse, cos, sin_signed, dqkv]
    out_specs = [pl.BlockSpec((BT, stride * HEAD), lambda h, i: (i, base // stride + h))]
    out_shape = [jax.ShapeDtypeStruct(dqkv.shape, dqkv.dtype)]
    if with_sink:
        in_specs.insert(0, pl.BlockSpec(memory_space=pltpu.SMEM))
        args.insert(0, sinks)
        out_specs.append(pl.BlockSpec((None, grp * 8, HEAD), lambda h, i: (h, 0, 0)))
        out_shape.append(jax.ShapeDtypeStruct((hkv, grp * 8, HEAD), F32))
    wide = pltpu.VMEM((BT + SB, grp * HEAD), F32)
    tall = pltpu.VMEM((SB + BT, HEAD), F32)
    grad = pltpu.VMEM((BT, HEAD), F32)
    return pl.pallas_call(
        body, name=name, grid=(hkv, nib), in_specs=in_specs, out_specs=out_specs, out_shape=out_shape,
        input_output_aliases={len(args) - 1: 0},
        scratch_shapes=[wide, wide, wide, tall, tall, pltpu.VMEM((BT, grp * HEAD), F32), grad, grad],
        compiler_params=pltpu.CompilerParams(dimension_semantics=("parallel", "arbitrary"),
                                             vmem_limit_bytes=VMEM_LIMIT_LARGE),
    )(*args)


M_HEADS = 4


def mem_kv(mem, g, w, name):
    n, D = mem.shape

    def body(m_ref, g_ref, w_ref, mn_ref, kv_ref):
        x = m_ref[...]
        mn = (x * _rstd(x) * g_ref[...]).astype(BF16)
        mn_ref[...] = mn
        kv_ref[...] = _dot(mn, w_ref[...]).astype(BF16)

    return pl.pallas_call(
        body, name=name,
        out_shape=[jax.ShapeDtypeStruct((n, D), BF16), jax.ShapeDtypeStruct((n, w.shape[1]), BF16)],
        compiler_params=pltpu.CompilerParams(vmem_limit_bytes=VMEM_LIMIT),
    )(mem, g, w)


def mem_fwd(qkv, mkv, name):
    T = qkv.shape[0]
    n = mkv.shape[0]
    RB = 1024

    def body(q_ref, kv_ref, o_ref, l_ref):
        for h in range(M_HEADS):
            cols = slice(h * HEAD, (h + 1) * HEAD)
            s = _dot_nt(q_ref[:, cols], kv_ref[:, cols]) * ATT_SCALE
            m = jnp.max(s, axis=-1, keepdims=True)
            p = jnp.exp(s - m)
            den = jnp.sum(p, axis=-1, keepdims=True)
            vals = kv_ref[:, (M_HEADS + h) * HEAD:(M_HEADS + h + 1) * HEAD]
            o_ref[:, cols] = (_dot(p.astype(BF16), vals) / den).astype(BF16)
            l_ref[:, cols] = jnp.broadcast_to(m + jnp.log(den), (RB, HEAD))

    out = pl.BlockSpec((RB, M_HEADS * HEAD), lambda i: (i, 0))
    return pl.pallas_call(
        body, name=name, grid=(T // RB,),
        in_specs=[pl.BlockSpec((RB, M_HEADS * HEAD), lambda i: (i, MQ // M_HEADS)), _resident(mkv)],
        out_specs=[out, out],
        out_shape=[jax.ShapeDtypeStruct((T, M_HEADS * HEAD), BF16), jax.ShapeDtypeStruct((T, M_HEADS * HEAD), F32)],
        compiler_params=_params("parallel"),
    )(qkv, mkv)


def mem_bwd(qkv, dqkv, mkv, do, o, lse, name):
    T = qkv.shape[0]
    n = mkv.shape[0]
    RB = 1024

    def body(q_ref, kv_ref, do_ref, o_ref, l_ref, _, dq_ref, dk_ref, dv_ref):
        @pl.when(pl.program_id(0) == 0)
        def _():
            dk_ref[...] = jnp.zeros_like(dk_ref)
            dv_ref[...] = jnp.zeros_like(dv_ref)

        for h in range(M_HEADS):
            cols = slice(h * HEAD, (h + 1) * HEAD)
            keys, vals = kv_ref[:, cols], kv_ref[:, (M_HEADS + h) * HEAD:(M_HEADS + h + 1) * HEAD]
            q, dout = q_ref[:, cols], do_ref[:, cols]
            delta = jnp.sum(dout.astype(F32) * o_ref[:, cols].astype(F32), axis=-1, keepdims=True)
            p = jnp.exp(_dot_nt(q, keys) * ATT_SCALE - l_ref[:, cols][:, :1])
            ds = (p * (_dot_nt(dout, vals) - delta) * ATT_SCALE).astype(BF16)
            dq_ref[:, cols] = _dot(ds, keys).astype(BF16)
            dk_ref[:, cols] += _dot_tn(ds, q)
            dv_ref[:, cols] += _dot_tn(p.astype(BF16), dout)

    wide = M_HEADS * HEAD
    tok = pl.BlockSpec((RB, wide), lambda i: (i, 0))
    q_cols = pl.BlockSpec((RB, wide), lambda i: (i, MQ // M_HEADS))
    slot = pl.BlockSpec((n, wide), lambda i: (0, 0))
    return pl.pallas_call(
        body, name=name, grid=(T // RB,),
        in_specs=[q_cols, _resident(mkv), tok, tok, tok, UNREAD],
        out_specs=[q_cols, slot, slot],
        out_shape=[jax.ShapeDtypeStruct(dqkv.shape, dqkv.dtype),
                   jax.ShapeDtypeStruct((n, wide), F32), jax.ShapeDtypeStruct((n, wide), F32)],
        input_output_aliases={5: 0},
        compiler_params=_params("arbitrary"),
    )(qkv, mkv, do, o, lse, dqkv)


def mem_kv_bwd(mem, g, mem_n, w, dmkv, name):
    n, D = mem.shape

    def body(m_ref, g_ref, mn_ref, w_ref, d_ref, dw_ref, dg_ref):
        d = d_ref[...].astype(BF16)
        dw_ref[...] = _dot_tn(mn_ref[...], d)
        x = m_ref[...]
        dg_ref[...] = jnp.sum(_dot_nt(d, w_ref[...]) * (x * _rstd(x)), axis=0, keepdims=True)

    return pl.pallas_call(
        body, name=name,
        out_shape=[jax.ShapeDtypeStruct(w.shape, F32), jax.ShapeDtypeStruct((1, D), F32)],
        compiler_params=pltpu.CompilerParams(vmem_limit_bytes=VMEM_LIMIT),
    )(mem, g, mem_n, w, dmkv)


def _rms_bwd(dn, f, g):
    r = _rstd(f)
    fhat = f * r
    dfhat = dn * g
    df = r * (dfhat - fhat * jnp.mean(dfhat * fhat, axis=-1, keepdims=True))
    return df, jnp.sum(dn * fhat, axis=0, keepdims=True)


def ffn_tokens_bwd(dh, f, h_in, gu, g_pre, g_post, w_in, w_out, coef, name, after):
    T, D = dh.shape

    def body(dh_ref, f_ref, h_ref, gu_ref, gpre_ref, gpost_ref, win_ref, wout_ref, _,
             df_ref, dgu_ref, dhin_ref, dgpre_ref, dgpost_ref, dxn_ref):
        i, j = pl.program_id(0), pl.program_id(1)

        @pl.when(j == 0)
        def _():
            @pl.when(i == 0)
            def _():
                dgpre_ref[...] = jnp.zeros_like(dgpre_ref)
                dgpost_ref[...] = jnp.zeros_like(dgpost_ref)

            df, dg_post = _rms_bwd(coef * dh_ref[...], f_ref[...], gpost_ref[...])
            dgpost_ref[...] += dg_post
            df_ref[...] = df.astype(BF16)

        for jj in range(2):
            @pl.when(j == jj)
            def _(jj=jj):
                lo, mid, hi = 2 * jj * FF_T, (2 * jj + 1) * FF_T, (2 * jj + 2) * FF_T
                da = _dot_nt(df_ref[...], wout_ref[jj * FF_T:(jj + 1) * FF_T, :])
                gate = gu_ref[:, :FF_T].astype(F32)
                up = gu_ref[:, FF_T:].astype(F32)
                sig = _sigmoid(gate)
                dgate = (da * up * sig * (1.0 + gate * (1.0 - sig))).astype(BF16)
                dup = (da * gate * sig).astype(BF16)
                dgu_ref[:, :FF_T] = dgate
                dgu_ref[:, FF_T:] = dup
                part = _dot_nt(dgate, win_ref[:, lo:mid]) + _dot_nt(dup, win_ref[:, mid:hi])
                if jj == 0:
                    dxn_ref[...] = part
                else:
                    h = h_ref[...]
                    r = _rstd(h)
                    xhat = h * r
                    dxn = dxn_ref[...] + part
                    dxhat = dxn * gpre_ref[...]
                    dhin_ref[...] = dh_ref[...] + r * (dxhat - xhat * jnp.mean(dxhat * xhat, axis=-1, keepdims=True))
                    dgpre_ref[...] += jnp.sum(dxn * xhat, axis=0, keepdims=True)

    row = pl.BlockSpec((TM, D), lambda i, j: (i, 0))
    wide = pl.BlockSpec((TM, 2 * FF_T), lambda i, j: (i, j))
    vec = pl.BlockSpec((1, D), lambda i, j: (0, 0))
    return pl.pallas_call(
        body, name=name, grid=(T // TM, 2),
        in_specs=[row, row, row, wide, _resident(g_pre), _resident(g_post), _resident(w_in), _resident(w_out),
                  UNREAD],
        out_specs=[row, wide, row, vec, vec],
        out_shape=[jax.ShapeDtypeStruct((T, D), BF16), jax.ShapeDtypeStruct((T, 2 * D_FF), BF16),
                   jax.ShapeDtypeStruct((T, D), F32), jax.ShapeDtypeStruct((1, D), F32),
                   jax.ShapeDtypeStruct((1, D), F32)],
        scratch_shapes=[pltpu.VMEM((TM, D), F32)],
        compiler_params=pltpu.CompilerParams(dimension_semantics=("arbitrary", "arbitrary"),
                                             vmem_limit_bytes=VMEM_LIMIT_LARGE),
    )(dh, f, h_in, gu, g_pre, g_post, w_in, w_out, after)


def mm_nt_norm_bwd(pieces, h_in, dh_out, g, name, after):
    T, D = h_in.shape

    def body(*refs):
        ab = refs[:2 * len(pieces)]
        h_ref, dh_ref, g_ref, _, o_ref, dg_ref = refs[2 * len(pieces):]
        dxn = _dot_nt(ab[0][...], ab[1][...])
        for p in range(1, len(pieces)):
            dxn += _dot_nt(ab[2 * p][...], ab[2 * p + 1][...])
        h = h_ref[...]
        r = _rstd(h)
        xhat = h * r
        dxhat = dxn * g_ref[...]
        o_ref[...] = dh_ref[...] + r * (dxhat - xhat * jnp.mean(dxhat * xhat, axis=-1, keepdims=True))

        @pl.when(pl.program_id(0) == 0)
        def _():
            dg_ref[...] = jnp.zeros_like(dg_ref)

        dg_ref[...] += jnp.sum(dxn * xhat, axis=0, keepdims=True)

    in_specs, args = [], []
    for a, w in pieces:
        in_specs += [pl.BlockSpec((TM, a.shape[1]), lambda i: (i, 0)), _resident(w)]
        args += [a, w]
    row = pl.BlockSpec((TM, D), lambda i: (i, 0))
    return pl.pallas_call(
        body, name=name, grid=(T // TM,),
        in_specs=in_specs + [row, row, _resident(g), UNREAD],
        out_specs=[row, pl.BlockSpec((1, D), lambda i: (0, 0))],
        out_shape=[jax.ShapeDtypeStruct((T, D), F32), jax.ShapeDtypeStruct((1, D), F32)],
        compiler_params=_params("arbitrary"),
    )(*args, h_in, dh_out, g, after)


def gate_merge_out_bwd(dh, f, g, w_out, merged, gt, o_a, o_b, o_m, w_a, w_b, w_m, name, after):
    T = dh.shape[0]
    D = D_MODEL
    branch = ((o_a, w_a), (o_b, w_b), (o_m, w_m))

    def body(dh_ref, f_ref, g_ref, wo_ref, m_ref, gt_ref, oa_ref, ob_ref, om_ref, wa_ref, wb_ref, wm_ref, _,
             dg_ref, dwo_ref, dgt_ref, doa_ref, dob_ref, dom_ref, db_ref, dwa_ref, dwb_ref, dwm_ref):
        @pl.when(pl.program_id(0) == 0)
        def _():
            for acc in (dg_ref, dwo_ref, db_ref, dwa_ref, dwb_ref, dwm_ref):
                acc[...] = jnp.zeros_like(acc)

        df, dg = _rms_bwd(dh_ref[...], f_ref[...], g_ref[...])
        dg_ref[...] += dg
        df = df.astype(BF16)
        dwo_ref[...] += _dot_tn(m_ref[...], df)
        dmf = _dot_nt(df, wo_ref[...])
        for x, (o_ref, w_ref, do_ref, dw_ref) in enumerate(((oa_ref, wa_ref, doa_ref, dwa_ref),
                                                           (ob_ref, wb_ref, dob_ref, dwb_ref),
                                                           (om_ref, wm_ref, dom_ref, dwm_ref))):
            cols = slice(x * D, (x + 1) * D)
            gx = gt_ref[:, cols].astype(F32)
            w = w_ref[...]
            dpre = dmf * _dot(o_ref[...], w) * gx * (1.0 - gx)
            dgt_ref[:, cols] = dpre.astype(BF16)
            db_ref[:, cols] += jnp.sum(dpre, axis=0, keepdims=True)
            dp = (dmf * gx).astype(BF16)
            do_ref[...] = _dot_nt(dp, w).astype(BF16)
            dw_ref[...] += _dot_tn(dp, o_ref[...])

    def rows(width):
        return pl.BlockSpec((TM, width), lambda i: (i, 0))

    def kept(shape):
        return pl.BlockSpec(shape, lambda i: (0,) * len(shape))

    widths = [o.shape[1] for o, _ in branch]
    sums = [(1, D), (D, D), (1, 3 * D)] + [(D, k) for k in widths]
    return pl.pallas_call(
        body, name=name, grid=(T // TM,),
        in_specs=[rows(D), rows(D), _resident(g), _resident(w_out), rows(D), rows(3 * D)]
                 + [rows(k) for k in widths] + [_resident(w) for _, w in branch] + [UNREAD],
        out_specs=[kept(sums[0]), kept(sums[1]), rows(3 * D)] + [rows(k) for k in widths]
                  + [kept(shape) for shape in sums[2:]],
        out_shape=[jax.ShapeDtypeStruct(sums[0], F32), jax.ShapeDtypeStruct(sums[1], F32),
                   jax.ShapeDtypeStruct((T, 3 * D), BF16)] + [jax.ShapeDtypeStruct((T, k), BF16) for k in widths]
                  + [jax.ShapeDtypeStruct(shape, F32) for shape in sums[2:]],
        compiler_params=pltpu.CompilerParams(dimension_semantics=("arbitrary",), vmem_limit_bytes=VMEM_LIMIT_LARGE),
    )(dh, f, g, w_out, merged, gt, o_a, o_b, o_m, w_a, w_b, w_m, after)


def mm_tn(x, dy, tm, tn, name, shard_major=False, perm=None, slabs=1, after=None, wire=False):
    T, M = x.shape
    N = dy.shape[1]
    tk = min(2048, T)
    perm = perm or (lambda j: j)
    w = tn // slabs

    def body(x_ref, dy_ref, *rest):
        o_ref = rest[-2] if wire else rest[-1]

        @pl.when(pl.program_id(2) == 0)
        def _():
            o_ref[...] = jnp.zeros_like(o_ref)

        acc = _dot_tn(x_ref[...], dy_ref[...])
        if shard_major:
            for s in range(slabs):
                o_ref[s] += acc[:, s * w:(s + 1) * w]
        else:
            o_ref[...] += acc
        if wire:
            @pl.when(pl.program_id(2) == T // tk - 1)
            def _():
                rest[-1][...] = o_ref[...].astype(BF16)

    if shard_major:
        out_spec = pl.BlockSpec((slabs, tm, w), lambda i, j, k: (perm(j), i, 0))
        out_shape = jax.ShapeDtypeStruct((N // w, M, w), F32)
    else:
        out_spec = pl.BlockSpec((tm, tn), lambda i, j, k: (i, j))
        out_shape = jax.ShapeDtypeStruct((M, N), F32)
    return pl.pallas_call(
        body, name=name, grid=(M // tm, N // tn, T // tk),
        in_specs=[pl.BlockSpec((tk, tm), lambda i, j, k: (k, i)),
                  pl.BlockSpec((tk, tn), lambda i, j, k: (k, j))] + ([] if after is None else [UNREAD]),
        out_specs=[out_spec, out_spec] if wire else out_spec,
        out_shape=[out_shape, jax.ShapeDtypeStruct(out_shape.shape, BF16)] if wire else out_shape,
        compiler_params=_params("parallel", "parallel", "arbitrary"),
    )(x, dy, *([] if after is None else [after]))


def rope_tables(T, zero):
    half = HEAD // 2
    inv = ROPE_THETA ** (-jnp.arange(half, dtype=F32) / half)
    ang = (jnp.arange(T).astype(F32) + zero)[:, None] * inv[None, :]
    cos, sin = jnp.cos(ang), jnp.sin(ang)
    return jnp.concatenate([cos, cos], axis=1), jnp.concatenate([-sin, sin], axis=1)


def layer_step(x, mem, target, gains, sinks, b_gate, weights_of, send_grads, zero):
    T = x.shape[0]
    cos, sin_signed = rope_tables(T, zero)
    no_sink = jnp.full((2,), NEG_INF, F32)

    xn1 = rms_scale(x, gains["ffn1_norm_pre"], "ffn1_norm", cos)
    w = dict(weights_of("ffn1_in", xn1))
    _, gu1, a1 = ffn_in(x, gains["ffn1_norm_pre"], w["ffn1_w_in"], "ffn1_in", xn=xn1)
    w.update(weights_of("ffn1_out", a1))
    f1, h1 = mm_norm_res(a1, w["ffn1_w_out"], x, gains["ffn1_norm_post"], 0.5, "ffn1_out")
    w.update(weights_of("mix_in", f1))
    u, qkv, gt = mix_in(h1, gains["mix_norm_pre"], w["w_in"], w["w_gate"], b_gate, cos, sin_signed, "mix_in")
    w.update(weights_of("mix_rest", u))
    outs, lses = [], []
    for gidx, (window, dil) in enumerate(DIL):
        last = gidx == len(DIL) - 1
        o_g, l_g = band_fwd(qkv, no_sink, r=dil, base=A_BASE + 6 * gidx, hkv=2, grp=1, max_dist=window // dil,
                            out_dtype=BF16 if last else F32, name=f"attn_a{gidx}_fwd",
                            merge=(outs, lses) if last else None)
        outs.append(o_g)
        lses.append(l_g)
    o_a, l_a = outs[-1], lses[-1]
    o_b, l_b = band_fwd(qkv, sinks, r=1, base=B_BASE, hkv=2, grp=2, max_dist=HEAD - 1, out_dtype=BF16,
                        name="attn_b_fwd")
    mem_n, mkv = mem_kv(mem, gains["mem_norm"], w["w_mem_kv"], "mem_kv")
    o_m, l_m = mem_fwd(qkv, mkv, "attn_m_fwd")
    merged, mo, h2 = gate_merge_out(gt, o_a, o_b, o_m, w["w_o_a"], w["w_o_b"], w["w_o_m"], w["w_out"], h1,
                                    gains["mix_norm_post"], "gate_merge_out")
    w.update(weights_of("ffn2", mo))
    xn2, gu2, a2 = ffn_in(h2, gains["ffn2_norm_pre"], w["ffn2_w_in"], "ffn2_in")
    f2, dy, sq = mm_norm_res(a2, w["ffn2_w_out"], h2, gains["ffn2_norm_post"], 0.5, "ffn2_out", target=target)

    grads = {}

    def ffn_bwd(tag, dh_out, f, gu, a, xn, h_in, after):
        df, dgu, dh_in, grads[f"{tag}_norm_pre"], grads[f"{tag}_norm_post"] = ffn_tokens_bwd(
            dh_out, f, h_in, gu, gains[f"{tag}_norm_pre"], gains[f"{tag}_norm_post"], w[f"{tag}_w_in"],
            w[f"{tag}_w_out"], 0.5, f"{tag}_tokens_bwd", after)
        sent = send_grads(f"{tag}_in", {f"{tag}_w_in": mm_tn(
            xn, dgu, D_MODEL, FF_T, f"{tag}_w_in_grad", shard_major=True, perm=_ffn_perm, wire=True)})
        sent = send_grads(f"{tag}_out", {f"{tag}_w_out": mm_tn(
            a, df, FF_T, D_MODEL, f"{tag}_w_out_grad", after=sent, wire=True)})
        return dh_in, sent

    dh2, sent = ffn_bwd("ffn2", dy, f2, gu2, a2, xn2, h2, dy)

    mix = {}
    (grads["mix_norm_post"], mix["w_out"], dgt, do_a, do_b, do_m, grads["b_gate"],
     dwa_t, dwb_t, dwm_t) = gate_merge_out_bwd(
        dh2, mo, gains["mix_norm_post"], w["w_out"], merged, gt, o_a, o_b, o_m, w["w_o_a"], w["w_o_b"],
        w["w_o_m"], "gate_merge_out_bwd", sent)
    mix["w_o_a"], mix["w_o_b"], mix["w_o_m"] = dwa_t.T, dwb_t.T, dwm_t.T

    dqkv = lax.empty(qkv.shape, qkv.dtype)
    for gidx, (window, dil) in enumerate(DIL):
        dqkv, = band_bwd(qkv, dqkv, do_a, o_a, l_a, cos, sin_signed, None, r=dil, base=A_BASE + 6 * gidx, hkv=2,
                         grp=1, max_dist=window // dil, name=f"attn_a{gidx}_bwd")
    dqkv, dsink = band_bwd(qkv, dqkv, do_b, o_b, l_b, cos, sin_signed, sinks, r=1, base=B_BASE, hkv=2, grp=2,
                           max_dist=HEAD - 1, name="attn_b_bwd")
    grads["sinks"] = -dsink[:, ::8, 0].reshape(1, 4)
    dqkv, dmk, dmv = mem_bwd(qkv, dqkv, mkv, do_m, o_m, l_m, "attn_m_bwd")
    mix["w_mem_kv"], grads["mem_norm"] = mem_kv_bwd(
        mem, gains["mem_norm"], mem_n, w["w_mem_kv"], jnp.concatenate([dmk, dmv], axis=1), "mem_kv_bwd")

    mix["w_in"] = mm_tn(u, dqkv, D_MODEL, 1280, "w_in_grad")
    mix["w_gate"] = mm_tn(u, dgt, D_MODEL, 1536, "w_gate_grad", shard_major=True, slabs=2, wire=True)
    sent = send_grads("mix", mix)
    dh1, grads["mix_norm_pre"] = mm_nt_norm_bwd(
        [(dqkv, w["w_in"]), (dgt, w["w_gate"])], h1, dh2, gains["mix_norm_pre"], "mix_in_bwd", sent)

    dx, _ = ffn_bwd("ffn1", dh1, f1, gu1, a1, xn1, x, dh1)
    return sq, dx, grads


def _place():
    return lax.axis_index("x"), lax.axis_index("y"), lax.axis_index("c")


def _other_chips(x, y):
    return [(1 - x, y), (x, 1 - y), (1 - x, 1 - y)]


def _hbm(n):
    return [pl.BlockSpec(memory_space=pltpu.HBM)] * n


SEM = pl.BlockSpec(memory_space=pltpu.SEMAPHORE)
SIDE_EFFECT = pltpu.SideEffectType.DATAFLOW_SIDE_EFFECTING


def _chip_copy(src, land, sems, i, j, dst_slot, scatter):
    x, y, c = _place()
    px, py = _other_chips(x, y)[j]
    send_sems, recv_sems = sems
    return pltpu.make_async_remote_copy(
        src_ref=src[i].at[2 * px + py] if scatter else src[i], dst_ref=land[i].at[dst_slot],
        send_sem=send_sems.at[3 * i + j], recv_sem=recv_sems.at[3 * i + j],
        device_id=(px, py, c), device_id_type=MESH)


def chip_copies_start(srcs, lands, groups, scatter, name, after=None):
    n = len(srcs)

    def body(*refs):
        src, land = refs[:n], refs[n:2 * n]
        first_sem = 2 * n + (after is not None)
        sems = refs[first_sem:first_sem + 2 * len(groups)]
        token = refs[-1]
        x, y, _ = _place()
        for g, members in enumerate(groups):
            part = ([src[i] for i in members], [land[i] for i in members])
            for t in range(len(members)):
                for j in range(3):
                    _chip_copy(*part, sems[2 * g:2 * g + 2], t, j, 2 * x + y, scatter).start()
        token[...] = jnp.zeros_like(token)

    sem_shapes = [pltpu.SemaphoreType.DMA((3 * len(m),)) for m in groups for _ in range(2)]
    thru = [pltpu.HBM(a.shape, a.dtype) for a in (*srcs, *lands)]
    res = pl.pallas_call(
        body, name=name,
        out_shape=(*sem_shapes, *thru, jax.ShapeDtypeStruct((8, 128), F32)),
        in_specs=_hbm(2 * n) + ([] if after is None else [UNREAD]),
        out_specs=(*[SEM] * len(sem_shapes), *_hbm(2 * n), pl.BlockSpec(memory_space=pltpu.VMEM)),
        input_output_aliases={i: len(sem_shapes) + i for i in range(2 * n)},
        compiler_params=pltpu.CompilerParams(has_side_effects=SIDE_EFFECT),
    )(*[pltpu.with_memory_space_constraint(a, pltpu.HBM) for a in (*srcs, *lands)],
      *([] if after is None else [after]))
    k = len(sem_shapes)
    sems = [tuple(res[2 * g:2 * g + 2]) for g in range(len(groups))]
    return sems, list(res[k:k + n]), list(res[k + n:k + 2 * n]), res[-1]


def chip_copies_wait(srcs, lands, sems, after, scatter, name):
    n = len(srcs)
    after = list(after) if isinstance(after, (list, tuple)) else [after]

    def body(*refs):
        src, land = refs[:n], refs[n:2 * n]
        pair = refs[2 * n:2 * n + 2]
        x, y, _ = _place()
        for i in range(n):
            for j, (px, py) in enumerate(_other_chips(x, y)):
                copy = _chip_copy(src, land, pair, i, j, 2 * px + py, scatter)
                copy.wait_send()
                copy.wait_recv()

    res = pl.pallas_call(
        body, name=name,
        out_shape=[pltpu.HBM(a.shape, a.dtype) for a in (*srcs, *lands)],
        in_specs=[*_hbm(2 * n), SEM, SEM] + [UNREAD] * len(after),
        out_specs=_hbm(2 * n),
        input_output_aliases={i: i for i in range(2 * n)},
        compiler_params=pltpu.CompilerParams(has_side_effects=SIDE_EFFECT),
    )(*srcs, *lands, *sems, *after)
    return list(res[n:])


def small_all_gather(small, name):
    flips = [(fx, fy, fc) for fx in (0, 1) for fy in (0, 1) for fc in (0, 1)][1:]

    def body(in_ref, out_ref, send_sems, recv_sems, local_sem):
        x, y, c = _place()
        me = 4 * x + 2 * y + c

        def copy(k, slot):
            fx, fy, fc = flips[k]
            return pltpu.make_async_remote_copy(
                src_ref=in_ref, dst_ref=out_ref.at[slot], send_sem=send_sems.at[k], recv_sem=recv_sems.at[k],
                device_id=(x ^ fx, y ^ fy, c ^ fc), device_id_type=MESH)

        local = pltpu.make_async_copy(in_ref, out_ref.at[me], local_sem)
        local.start()
        for k in range(len(flips)):
            copy(k, me).start()
        for k, (fx, fy, fc) in enumerate(flips):
            copy(k, 4 * (x ^ fx) + 2 * (y ^ fy) + (c ^ fc)).wait()
        local.wait()

    return pl.pallas_call(
        body, name=name, in_specs=_hbm(1), out_specs=_hbm(1)[0],
        out_shape=jax.ShapeDtypeStruct((N_DEV,) + small.shape, small.dtype),
        scratch_shapes=[pltpu.SemaphoreType.DMA((len(flips),)), pltpu.SemaphoreType.DMA((len(flips),)),
                        pltpu.SemaphoreType.DMA],
    )(small)


def _sibling_copy(src, land, sems, i):
    x, y, c = _place()
    return pltpu.make_async_remote_copy(
        src_ref=src[i], dst_ref=land[i], send_sem=sems[0].at[i], recv_sem=sems[1].at[i],
        device_id=(x, y, 1 - c), device_id_type=MESH)


def sibling_copies_start(parts, name):
    n = len(parts)
    lands = [lax.empty(p.shape, p.dtype) for p in parts]

    def body(*refs):
        src, land, sems, token = refs[:n], refs[n:2 * n], refs[2 * n:2 * n + 2], refs[-1]
        for i in range(n):
            _sibling_copy(src, land, sems, i).start()
        token[...] = jnp.zeros_like(token)

    res = pl.pallas_call(
        body, name=name,
        out_shape=(pltpu.SemaphoreType.DMA((n,)), pltpu.SemaphoreType.DMA((n,)),
                   *[pltpu.HBM(a.shape, a.dtype) for a in (*parts, *lands)], jax.ShapeDtypeStruct((8, 128), F32)),
        in_specs=_hbm(2 * n),
        out_specs=(SEM, SEM, *_hbm(2 * n), pl.BlockSpec(memory_space=pltpu.VMEM)),
        input_output_aliases={i: 2 + i for i in range(2 * n)},
        compiler_params=pltpu.CompilerParams(has_side_effects=SIDE_EFFECT),
    )(*[pltpu.with_memory_space_constraint(a, pltpu.HBM) for a in (*parts, *lands)])
    return tuple(res[:2]), list(res[2:2 + n]), list(res[2 + n:2 + 2 * n]), res[-1]


def sibling_copies_wait(parts, lands, sems, after, name):
    n = len(parts)

    def body(*refs):
        src, land, sems = refs[:n], refs[n:2 * n], refs[2 * n:2 * n + 2]
        for i in range(n):
            copy = _sibling_copy(src, land, sems, i)
            copy.wait_send()
            copy.wait_recv()

    res = pl.pallas_call(
        body, name=name,
        out_shape=[pltpu.HBM(a.shape, a.dtype) for a in (*parts, *lands)],
        in_specs=[*_hbm(2 * n), SEM, SEM, UNREAD],
        out_specs=_hbm(2 * n),
        input_output_aliases={i: i for i in range(2 * n)},
        compiler_params=pltpu.CompilerParams(has_side_effects=SIDE_EFFECT),
    )(*parts, *lands, *sems, after)
    return list(res[n:])


def _row_tile(rows):
    for t in (256, 176, 128, 64, 32, 16, 8):
        if rows % t == 0:
            return t
    return rows


def chip_partial_sum(me, own_sm, recv, name):
    _, rows, cols = own_sm.shape
    tr = _row_tile(rows)

    def body(me_ref, own_ref, r1, r2, r3, o_ref):
        o_ref[...] = own_ref[...] + r1[...].astype(F32) + r2[...].astype(F32) + r3[...].astype(F32)

    def slot(d):
        return pl.BlockSpec((None, tr, cols), lambda i, me_ref: ((me_ref[0] + d) % N_CHIPS, i, 0))

    return pl.pallas_call(
        body, name=name,
        grid_spec=pltpu.PrefetchScalarGridSpec(
            num_scalar_prefetch=1, grid=(rows // tr,),
            in_specs=[slot(0), slot(1), slot(2), slot(3)],
            out_specs=pl.BlockSpec((tr, cols), lambda i, me_ref: (i, 0))),
        out_shape=jax.ShapeDtypeStruct((rows, cols), F32),
        compiler_params=_params("parallel"),
    )(me, own_sm, recv, recv, recv)


def _adamw(w, g, m, v):
    m = ADAM_B1 * m + (1.0 - ADAM_B1) * g
    v = ADAM_B2 * v + (1.0 - ADAM_B2) * (g * g)
    m_hat = m / (1.0 - ADAM_B1 ** ADAM_STEP)
    v_hat = v / (1.0 - ADAM_B2 ** ADAM_STEP)
    delta = -ADAM_LR * (m_hat / (jnp.sqrt(v_hat) + ADAM_EPS) + ADAM_WD * w)
    return delta, m, v


def adamw_pair(part, sib, w, m, v, name):
    rows, cols = w.shape
    tr = _row_tile(rows)

    def body(p_ref, s_ref, w_ref, m_ref, v_ref, g_ref, d_ref, nm_ref, nv_ref):
        g = p_ref[...] + s_ref[...]
        g_ref[...] = g
        d_ref[...], nm_ref[...], nv_ref[...] = _adamw(w_ref[...], g, m_ref[...], v_ref[...])

    spec = pl.BlockSpec((tr, cols), lambda i: (i, 0))
    return pl.pallas_call(
        body, name=name, grid=(rows // tr,), in_specs=[spec] * 5, out_specs=[spec] * 4,
        out_shape=[jax.ShapeDtypeStruct((rows, cols), F32)] * 4,
        compiler_params=_params("parallel"),
    )(part, sib, w, m, v)


def adamw_small(g_all, w, m, v, name):
    def body(ga_ref, w_ref, m_ref, v_ref, g_ref, d_ref, nm_ref, nv_ref):
        g = ga_ref[0]
        for k in range(1, N_DEV):
            g = g + ga_ref[k]
        g_ref[...] = g
        d_ref[...], nm_ref[...], nv_ref[...] = _adamw(w_ref[...], g, m_ref[...], v_ref[...])

    return pl.pallas_call(
        body, name=name, out_shape=[jax.ShapeDtypeStruct(w.shape, F32)] * 4,
    )(g_all, w, m, v)


WEIGHTS = ("ffn1_norm_pre", "ffn1_w_in", "ffn1_w_out", "ffn1_norm_post", "mix_norm_pre", "w_in", "sinks",
           "mem_norm", "w_mem_kv", "w_gate", "b_gate", "w_o_a", "w_o_b", "w_o_m", "w_out", "mix_norm_post",
           "ffn2_norm_pre", "ffn2_w_in", "ffn2_w_out", "ffn2_norm_post")
GATHER_STAGES = (("ffn1_in", "ffn1_out"), ("mix_in",), ("mix_rest", "ffn2"))
GATHER_GROUPS = {"ffn1_in": ("ffn1_w_in",), "ffn1_out": ("ffn1_w_out",),
                 "mix_in": ("w_in", "w_gate"), "mix_rest": ("w_mem_kv", "w_o_a", "w_o_b", "w_o_m", "w_out"),
                 "ffn2": ("ffn2_w_in", "ffn2_w_out")}
GROUPS = {"ffn1_in": ("ffn1_w_in",), "ffn1_out": ("ffn1_w_out",),
          "mix": ("w_in", "w_gate", "w_mem_kv", "w_o_a", "w_o_b", "w_o_m", "w_out"),
          "ffn2_in": ("ffn2_w_in",), "ffn2_out": ("ffn2_w_out",)}
COLUMN_SHARDED = ("ffn1_w_in", "ffn2_w_in", "w_in", "w_gate", "w_o_a", "w_o_b", "w_o_m")
KEPT_SHARD_MAJOR = ("ffn1_w_in", "ffn2_w_in", "w_gate")
GAINS = ("ffn1_norm_pre", "ffn1_norm_post", "mix_norm_pre", "mem_norm", "mix_norm_post", "ffn2_norm_pre",
         "ffn2_norm_post")
SMALL_ROWS = 16


def _pack_small(t):
    sinks = jnp.pad(t["sinks"], ((0, 0), (0, D_MODEL - t["sinks"].shape[1])))
    rows = [t[k] for k in GAINS] + [t["b_gate"].reshape(3, D_MODEL), sinks]
    packed = jnp.concatenate(rows, axis=0)
    return jnp.pad(packed, ((0, SMALL_ROWS - packed.shape[0]), (0, 0)))


def _unpack_small(p):
    out = {k: p[i:i + 1] for i, k in enumerate(GAINS)}
    out["b_gate"] = p[7:10].reshape(1, 3 * D_MODEL)
    out["sinks"] = p[10:11, :4]
    return out


def kernel(x, mem, ffn1_norm_pre, ffn1_w_in, ffn1_w_out, ffn1_norm_post, mix_norm_pre, w_in, sinks, mem_norm, w_mem_kv, w_gate, b_gate, w_o_a, w_o_b, w_o_m, w_out, mix_norm_post, ffn2_norm_pre, ffn2_w_in, ffn2_w_out, ffn2_norm_post, loss_target, m_ffn1_norm_pre, m_ffn1_w_in, m_ffn1_w_out, m_ffn1_norm_post, m_mix_norm_pre, m_w_in, m_sinks, m_mem_norm, m_w_mem_kv, m_w_gate, m_b_gate, m_w_o_a, m_w_o_b, m_w_o_m, m_w_out, m_mix_norm_post, m_ffn2_norm_pre, m_ffn2_w_in, m_ffn2_w_out, m_ffn2_norm_post, v_ffn1_norm_pre, v_ffn1_w_in, v_ffn1_w_out, v_ffn1_norm_post, v_mix_norm_pre, v_w_in, v_sinks, v_mem_norm, v_w_mem_kv, v_w_gate, v_b_gate, v_w_o_a, v_w_o_b, v_w_o_m, v_w_out, v_mix_norm_post, v_ffn2_norm_pre, v_ffn2_w_in, v_ffn2_w_out, v_ffn2_norm_post):
    given = dict(locals())
    wt = {k: given[k] for k in WEIGHTS}
    mom = {k: given["m_" + k] for k in WEIGHTS}
    var = {k: given["v_" + k] for k in WEIGHTS}
    chip = (2 * lax.axis_index("x") + lax.axis_index("y")).astype(jnp.int32)
    me = chip.reshape(1)

    def landing_zone(own):
        return lax.dynamic_update_slice_in_dim(lax.empty((N_CHIPS,) + own.shape, own.dtype), own[None], chip, 0)

    started = {}
    tokens = []

    def stage_keys(stage):
        return [k for g in GATHER_STAGES[stage] for k in GATHER_GROUPS[g]]

    def prepare(stage):
        shards = [(wt[k][0] + tokens[0][0, 0] if tokens else wt[k][0]).astype(BF16) for k in stage_keys(stage)]
        return shards, [landing_zone(s) for s in shards]

    def start_gather(stage, after):
        groups, keys = GATHER_STAGES[stage], stage_keys(stage)
        members = [[keys.index(k) for k in GATHER_GROUPS[g]] for g in groups]
        sems, shards, lands, token = chip_copies_start(
            *prepared[stage], members, False, f"weight_gather_start_{stage}", after)
        tokens.append(token)
        for g, idx, pair in zip(groups, members, sems):
            started[g] = ([shards[i] for i in idx], [lands[i] for i in idx], pair)

    prepared = {0: prepare(0)}
    start_gather(0, None)
    prepared.update({stage: prepare(stage) for stage in range(1, len(GATHER_STAGES))})

    def weights_of(group, after):
        if group == GATHER_STAGES[0][0]:
            after = [after] + [a for stage in range(1, len(GATHER_STAGES)) for part in prepared[stage] for a in part]
        got = chip_copies_wait(*started[group], after, False, f"weight_gather_wait_{group}")
        stage = [s + 1 for s, groups in enumerate(GATHER_STAGES[:-1]) if groups[0] == group]
        if stage:
            start_gather(stage[0], got[0])
        full = {}
        for k, g in zip(GATHER_GROUPS[group], got):
            if k in COLUMN_SHARDED:
                if k in ("ffn1_w_in", "ffn2_w_in"):
                    g = jnp.stack([g[0], g[2], g[1], g[3]])
                full[k] = jnp.swapaxes(g, 0, 1).reshape(g.shape[1], N_CHIPS * g.shape[2])
                if k == "w_in":
                    full[k] = to_kernel_heads(full[k])
            else:
                full[k] = g.reshape(N_CHIPS * g.shape[1], g.shape[2])
        return full

    in_flight = {}

    def send_grads(group, grads):
        def shard_major(k, g):
            if k in KEPT_SHARD_MAJOR:
                return g
            if k in COLUMN_SHARDED:
                return jnp.swapaxes(g.reshape(g.shape[0], N_CHIPS, g.shape[1] // N_CHIPS), 0, 1)
            return g.reshape(N_CHIPS, g.shape[0] // N_CHIPS, g.shape[1])

        own, wire = [], []
        for k in GROUPS[group]:
            g, rounded = grads[k] if isinstance(grads[k], (tuple, list)) else (grads[k], None)
            g = shard_major(k, from_kernel_heads(g) if k == "w_in" else g)
            own.append(g)
            wire.append(g.astype(BF16) if rounded is None else shard_major(k, rounded))
        zones = [lax.empty(b.shape, b.dtype) for b in wire]
        pair, wire, zones, sent = chip_copies_start(
            wire, zones, [list(range(len(wire)))], True, f"grad_scatter_start_{group}")
        in_flight[group] = (own, wire, zones, pair[0], sent)
        return sent

    gains = {k: wt[k] for k in GAINS}
    sq, dx, grads = layer_step(
        x[0], mem[0], loss_target[0], gains, sinks[0], b_gate, weights_of, send_grads, tokens[0][0, 0])
    loss = lax.psum(0.5 * jnp.sum(sq) / D_MODEL, ("x", "y", "c"))

    res = {}
    after = in_flight["ffn1_out"][4]
    swaps = []
    for stage in (("ffn2_in", "ffn2_out", "mix", "ffn1_in"), ("ffn1_out",)):
        names, parts = [], []
        for group in stage:
            own, wire, zones, pair, _ = in_flight[group]
            received = chip_copies_wait(wire, zones, pair, after, True, f"grad_scatter_wait_{group}")
            for k, g, r in zip(GROUPS[group], own, received):
                names.append(k)
                parts.append(chip_partial_sum(me, g, r, f"{k}_chip_sum"))
        pair, parts, lands, after = sibling_copies_start(parts, f"sibling_start_{stage[-1]}")
        swaps.append((stage[-1], names, parts, lands, pair))
    small_all = small_all_gather(_pack_small(grads), "small_grad_gather")
    packed = adamw_small(small_all, _pack_small(wt), _pack_small(mom), _pack_small(var), "small_adamw")
    after = packed[0]
    for tag, names, parts, lands, pair in swaps:
        sibs = sibling_copies_wait(parts, lands, pair, after, f"sibling_wait_{tag}")
        for k, p, s in zip(names, parts, sibs):
            res[k] = [t[None] for t in adamw_pair(p, s, wt[k][0], mom[k][0], var[k][0], f"{k}_adamw")]
        after = res[names[-1]][0]
    for idx, p in enumerate(packed):
        for k, t in _unpack_small(p).items():
            res.setdefault(k, [None] * 4)[idx] = t

    return (loss, dx[None], *[res[k][0] for k in WEIGHTS], *[res[k][1] for k in WEIGHTS],
            *[res[k][2] for k in WEIGHTS], *[res[k][3] for k in WEIGHTS])
```

```python
import functools

import jax
import jax.numpy as jnp
from jax import lax
from jax.experimental import pallas as pl
from jax.experimental.pallas import tpu as pltpu

F32 = jnp.float32
BF16 = jnp.bfloat16

D_MODEL = 1024
D_FF = 2816
HEAD = 128
N_CHIPS = 4
N_DEV = 8
EPS = 1e-6
NEG_INF = -1e30
ROPE_THETA = 10000.0
ATT_SCALE = HEAD ** -0.5

ADAM_LR = 0.001
ADAM_B1 = 0.9
ADAM_B2 = 0.999
ADAM_EPS = 1e-08
ADAM_WD = 0.01
ADAM_STEP = 10

VMEM_LIMIT = 52 * 2 ** 20
VMEM_LIMIT_LARGE = 60 * 2 ** 20
MESH = pl.DeviceIdType.MESH

QKV_W = 3840
DIL = ((128, 1), (512, 4), (2048, 16))
B_BASE, MQ, A_BASE = 0, 8, 12
_AQ, _AK, _AV, _BQ, _BK, _BV, _MQ = 0, 6, 12, 18, 22, 24, 26
HEAD_ORDER = tuple(
    [h for j in range(2) for h in (_BQ + 2 * j, _BQ + 2 * j + 1, _BK + j, _BV + j)]
    + [_MQ + i for i in range(4)]
    + [h for g in range(3) for i in range(2) for h in (_AQ + 2 * g + i, _AK + 2 * g + i, _AV + 2 * g + i)])
ROTARY_HEADS = tuple(p for p, h in enumerate(HEAD_ORDER) if h < _AV or _BQ <= h < _BV)


def to_kernel_heads(w):
    return jnp.concatenate([w[..., h * HEAD:(h + 1) * HEAD] for h in HEAD_ORDER], axis=-1)


def from_kernel_heads(w):
    place = {h: p for p, h in enumerate(HEAD_ORDER)}
    return jnp.concatenate([w[..., place[h] * HEAD:(place[h] + 1) * HEAD] for h in range(len(HEAD_ORDER))], axis=-1)

TM = 512
FF_T = D_FF // 2


def _params(*sem):
    return pltpu.CompilerParams(dimension_semantics=sem, vmem_limit_bytes=VMEM_LIMIT)


def _dot(a, b):
    return jnp.dot(a, b, preferred_element_type=F32)


def _dot_nt(a, b):
    return lax.dot_general(a, b, (((1,), (1,)), ((), ())), preferred_element_type=F32)


def _dot_tn(a, b):
    return lax.dot_general(a, b, (((0,), (0,)), ((), ())), preferred_element_type=F32)


def _rstd(x):
    return lax.rsqrt(jnp.mean(x * x, axis=-1, keepdims=True) + EPS)


def _sigmoid(x):
    return 0.5 * jnp.tanh(0.5 * x) + 0.5


def _ffn_perm(k):
    return (k % 2) * 2 + k // 2


UNREAD = pl.BlockSpec(memory_space=pl.ANY)


def _resident(arr):
    return pl.BlockSpec(arr.shape, lambda *_: (0,) * arr.ndim, pipeline_mode=pl.Buffered(1))


def rms_scale(x, g, name, after):
    T, D = x.shape
    tm = 1024

    def body(x_ref, g_ref, _, o_ref):
        v = x_ref[...]
        o_ref[...] = (v * _rstd(v) * g_ref[...]).astype(BF16)

    spec = pl.BlockSpec((tm, D), lambda i: (i, 0))
    return pl.pallas_call(
        body, name=name, grid=(T // tm,), in_specs=[spec, _resident(g), UNREAD], out_specs=spec,
        out_shape=jax.ShapeDtypeStruct((T, D), BF16), compiler_params=_params("parallel"),
    )(x, g, after)


def ffn_in(h, g, w, name, xn=None):
    T, D = h.shape
    normed = xn is not None

    def body(h_ref, g_ref, w_ref, *outs):
        if normed:
            xn, (gu_ref, a_ref) = h_ref[...], outs
        else:
            xn_ref, gu_ref, a_ref = outs
            x = h_ref[...]
            xn = (x * _rstd(x) * g_ref[...]).astype(BF16)
            xn_ref[...] = xn
        for j in range(2):
            gu = _dot(xn, w_ref[:, j * 2 * FF_T:(j + 1) * 2 * FF_T])
            gu_ref[:, j * 2 * FF_T:(j + 1) * 2 * FF_T] = gu.astype(BF16)
            gate, up = gu[:, :FF_T], gu[:, FF_T:]
            a_ref[:, j * FF_T:(j + 1) * FF_T] = (gate * _sigmoid(gate) * up).astype(BF16)

    def rows(width):
        return pl.BlockSpec((TM, width), lambda i: (i, 0))

    res = pl.pallas_call(
        body, name=name,
        grid=(T // TM,),
        in_specs=[rows(D), _resident(g), _resident(w)],
        out_specs=[rows(D)] * (not normed) + [rows(2 * D_FF), rows(D_FF)],
        out_shape=[jax.ShapeDtypeStruct((T, D), BF16)] * (not normed)
                  + [jax.ShapeDtypeStruct((T, 2 * D_FF), BF16), jax.ShapeDtypeStruct((T, D_FF), BF16)],
        compiler_params=_params("parallel"),
    )(xn if normed else h, g, w)
    return (xn, *res) if normed else tuple(res)


def mm_norm_res(a, w, h_in, g, coef, name, target=None):
    T, K = a.shape
    D = w.shape[1]
    final = target is not None
    tm = min(2 * TM, T)

    def body(*refs):
        if final:
            a_ref, w_ref, h_ref, g_ref, t_ref, f_ref, o_ref, l_ref = refs
        else:
            a_ref, w_ref, h_ref, g_ref, f_ref, o_ref = refs
        f = _dot(a_ref[...], w_ref[...])
        f_ref[...] = f
        y = h_ref[...] + coef * (f * _rstd(f) * g_ref[...])
        if final:
            err = y - t_ref[...]
            o_ref[...] = err * (1.0 / D)

            @pl.when(pl.program_id(0) == 0)
            def _():
                l_ref[...] = jnp.zeros_like(l_ref)

            sq = jnp.sum((err * err).reshape(tm // 8, 8, D), axis=0)
            l_ref[...] += functools.reduce(jnp.add, [sq[:, c:c + HEAD] for c in range(0, D, HEAD)])
        else:
            o_ref[...] = y

    row = pl.BlockSpec((tm, D), lambda i: (i, 0))
    in_specs = [pl.BlockSpec((tm, K), lambda i: (i, 0)),
                _resident(w),
                row, pl.BlockSpec((1, D), lambda i: (0, 0))]
    out_specs = [row, row]
    out_shape = [jax.ShapeDtypeStruct((T, D), F32), jax.ShapeDtypeStruct((T, D), F32)]
    args = [a, w, h_in, g]
    if final:
        in_specs.append(row)
        args.append(target)
        out_specs.append(pl.BlockSpec((8, 128), lambda i: (0, 0)))
        out_shape.append(jax.ShapeDtypeStruct((8, 128), F32))
    return pl.pallas_call(
        body, name=name, grid=(T // tm,), in_specs=in_specs, out_specs=out_specs, out_shape=out_shape,
        compiler_params=pltpu.CompilerParams(dimension_semantics=("arbitrary",), vmem_limit_bytes=VMEM_LIMIT_LARGE),
    )(*args)


def _rope(x, cos, sin_signed):
    return x * cos + pltpu.roll(x, HEAD // 2, axis=1) * sin_signed


def _unrope(x, cos, sin_signed):
    return x * cos - pltpu.roll(x, HEAD // 2, axis=1) * sin_signed


def mix_in(h, g, w, w_gate, b_gate, cos, sin_signed, name):
    T, D = h.shape
    tn = 768

    def body(h_ref, g_ref, w_ref, wg_ref, b_ref, c_ref, s_ref, u_ref, o_ref, gt_ref):
        x = h_ref[...]
        u = (x * _rstd(x) * g_ref[...]).astype(BF16)
        u_ref[...] = u
        c, s = c_ref[...], s_ref[...]
        for j in range(QKV_W // tn):
            acc = _dot(u, w_ref[:, j * tn:(j + 1) * tn])
            for hd in range(tn // HEAD):
                head = j * (tn // HEAD) + hd
                part = acc[:, hd * HEAD:(hd + 1) * HEAD]
                if head in ROTARY_HEADS:
                    part = _rope(part, c, s)
                o_ref[:, head * HEAD:(head + 1) * HEAD] = part.astype(BF16)
        for j in range(w_gate.shape[1] // tn):
            cols = slice(j * tn, (j + 1) * tn)
            gt_ref[:, cols] = _sigmoid(_dot(u, wg_ref[:, cols]) + b_ref[:, cols]).astype(BF16)

    def rows(width):
        return pl.BlockSpec((TM, width), lambda i: (i, 0))

    return pl.pallas_call(
        body, name=name,
        grid=(T // TM,),
        in_specs=[rows(D), _resident(g), _resident(w), _resident(w_gate), _resident(b_gate), rows(HEAD), rows(HEAD)],
        out_specs=[rows(D), rows(QKV_W), rows(w_gate.shape[1])],
        out_shape=[jax.ShapeDtypeStruct((T, D), BF16), jax.ShapeDtypeStruct((T, QKV_W), BF16),
                   jax.ShapeDtypeStruct((T, w_gate.shape[1]), BF16)],
        compiler_params=_params("parallel"),
    )(h, g, w, w_gate, b_gate, cos, sin_signed)


def gate_merge_out(gt, o_a, o_b, o_m, w_a, w_b, w_m, w_out, h_in, g, name):
    T = gt.shape[0]
    D = D_MODEL

    def body(gt_ref, oa_ref, ob_ref, om_ref, wa_ref, wb_ref, wm_ref, wo_ref, h_ref, g_ref, m_ref, f_ref, o_ref):
        acc = gt_ref[:, :D].astype(F32) * _dot(oa_ref[...], wa_ref[...])
        acc += gt_ref[:, D:2 * D].astype(F32) * _dot(ob_ref[...], wb_ref[...])
        acc += gt_ref[:, 2 * D:].astype(F32) * _dot(om_ref[...], wm_ref[...])
        merged = acc.astype(BF16)
        m_ref[...] = merged
        f = _dot(merged, wo_ref[...])
        f_ref[...] = f
        o_ref[...] = h_ref[...] + f * _rstd(f) * g_ref[...]

    def rows(width):
        return pl.BlockSpec((TM, width), lambda i: (i, 0))

    return pl.pallas_call(
        body, name=name, grid=(T // TM,),
        in_specs=[rows(3 * D), rows(o_a.shape[1]), rows(o_b.shape[1]), rows(o_m.shape[1]),
                  _resident(w_a), _resident(w_b), _resident(w_m), _resident(w_out), rows(D), _resident(g)],
        out_specs=[rows(D), rows(D), rows(D)],
        out_shape=[jax.ShapeDtypeStruct((T, D), BF16), jax.ShapeDtypeStruct((T, D), F32),
                   jax.ShapeDtypeStruct((T, D), F32)],
        compiler_params=_params("parallel"),
    )(gt, o_a, o_b, o_m, w_a, w_b, w_m, w_out, h_in, g)


def _band_rows(start, r):
    return pl.ds(start, HEAD) if r == 1 else pl.ds(start, HEAD, stride=r)


def _band_mask(max_dist, first_has_prev):
    row = lax.broadcasted_iota(jnp.int32, (HEAD, 2 * HEAD), 0)
    col = lax.broadcasted_iota(jnp.int32, (HEAD, 2 * HEAD), 1)
    dist = row + HEAD - col
    band = (dist >= 0) & (dist <= max_dist)
    return band, band & (col >= jnp.where(first_has_prev, 0, HEAD))


def _stack(parts):
    return parts[0] if len(parts) == 1 else jnp.concatenate(parts, axis=0)


def _band_specs(BT, SB, nsub, base, grp):
    stride = grp + 2

    def cur(off, width):
        return pl.BlockSpec((BT, width * HEAD), lambda h, i: (i, (base + h * stride + off) // width))

    def prev(off):
        return pl.BlockSpec((SB, HEAD), lambda h, i: (jnp.maximum(i * nsub - 1, 0), base + h * stride + off))

    return cur(0, grp), cur(grp, 1), prev(grp), cur(grp + 1, 1), prev(grp + 1)


def band_fwd(qkv, sinks, *, r, base, hkv, grp, max_dist, out_dtype, name, merge=None):
    T, W = qkv.shape
    SB = HEAD * r
    BT = min(2048, T)
    nsub, nib = BT // SB, T // BT
    hq = hkv * grp
    heads = [slice(g * HEAD, (g + 1) * HEAD) for g in range(grp)]
    others = [] if merge is None else [*merge[0], *merge[1]]

    def body(sink_ref, q_ref, kc_ref, kp_ref, vc_ref, vp_ref, *rest):
        joint_o, joint_l = rest[len(others):len(others) + 2]
        qf, kf, vf = rest[len(others) + 2:len(others) + 5]
        o_ref, l_ref = rest[len(others) + 5:] if others else (joint_o, joint_l)
        kvh, ib = pl.program_id(0), pl.program_id(1)
        qf[...] = q_ref[...].astype(F32)
        kf[:SB] = kp_ref[...].astype(F32)
        kf[SB:] = kc_ref[...].astype(F32)
        vf[:SB] = vp_ref[...].astype(F32)
        vf[SB:] = vc_ref[...].astype(F32)
        band, band_first = _band_mask(max_dist, ib > 0)
        for c in range(r):
            k_old, v_old = kf[_band_rows(c, r)], vf[_band_rows(c, r)]
            for j in range(nsub):
                mask = band_first if j == 0 else band
                rows = _band_rows(j * SB + c, r)
                k_own, v_own = kf[_band_rows((j + 1) * SB + c, r)], vf[_band_rows((j + 1) * SB + c, r)]
                kcat = jnp.concatenate([k_old, k_own], axis=0).astype(BF16)
                vcat = jnp.concatenate([v_old, v_own], axis=0).astype(BF16)
                k_old, v_old = k_own, v_own
                s_all = _dot_nt(_stack([qf[rows, cols] for cols in heads]).astype(BF16), kcat) * ATT_SCALE
                probs, tots = [], []
                for g, cols in enumerate(heads):
                    s = jnp.where(mask, s_all[cols], NEG_INF)
                    sk = sink_ref[kvh * grp + g]
                    m = jnp.maximum(jnp.max(s, axis=-1, keepdims=True), sk)
                    p = jnp.exp(s - m)
                    tot = jnp.sum(p, axis=-1, keepdims=True) + jnp.exp(sk - m)
                    probs.append(p.astype(BF16))
                    tots.append(tot)
                    l_ref[rows, cols] = jnp.broadcast_to(m + jnp.log(tot), (HEAD, HEAD))
                o_all = _dot(_stack(probs), vcat)
                for g, cols in enumerate(heads):
                    o_ref[rows, cols] = (o_all[cols] / tots[g]).astype(o_ref.dtype)

        if others:
            half = len(others) // 2
            outs = [ref[...] for ref in rest[:half]] + [o_ref[...]]
            logs = [ref[...] for ref in rest[half:len(others)]] + [l_ref[...]]
            top = functools.reduce(jnp.maximum, logs)
            weights = [jnp.exp(lg - top) for lg in logs]
            total = functools.reduce(jnp.add, weights)
            mixed = functools.reduce(jnp.add, [wgt * out for wgt, out in zip(weights, outs)])
            joint_o[...] = (mixed / total).astype(out_dtype)
            joint_l[...] = top + jnp.log(total)

    out_spec = pl.BlockSpec((BT, grp * HEAD), lambda h, i: (i, h))
    own = [pltpu.VMEM((BT, grp * HEAD), F32)] * 2 if others else []
    return pl.pallas_call(
        body, name=name, grid=(hkv, nib),
        in_specs=[pl.BlockSpec(memory_space=pltpu.SMEM), *_band_specs(BT, SB, nsub, base, grp)]
                 + [out_spec] * len(others),
        out_specs=[out_spec, out_spec],
        out_shape=[jax.ShapeDtypeStruct((T, hq * HEAD), out_dtype), jax.ShapeDtypeStruct((T, hq * HEAD), F32)],
        scratch_shapes=[pltpu.VMEM((BT, grp * HEAD), F32), pltpu.VMEM((SB + BT, HEAD), F32),
                        pltpu.VMEM((SB + BT, HEAD), F32)] + own,
        compiler_params=_params("parallel", "arbitrary"),
    )(sinks, qkv, qkv, qkv, qkv, qkv, *others)


def band_bwd(qkv, dqkv, do, o, lse, cos, sin_signed, sinks, *, r, base, hkv, grp, max_dist, name):
    T, W = qkv.shape
    SB = HEAD * r
    BT = min(max(2048, 2 * SB), T)
    nsub, nib = BT // SB, T // BT
    nblk = T // SB
    with_sink = sinks is not None
    heads = [slice(g * HEAD, (g + 1) * HEAD) for g in range(grp)]

    def body(*refs):
        if with_sink:
            sink_ref, refs = refs[0], refs[1:]
        (q_ref, kc_ref, kp_ref, vc_ref, vp_ref, qn_ref, do_ref, don_ref, o_ref, on_ref, l_ref, ln_ref,
         c_ref, s_ref, _) = refs[:15]
        out_ref = refs[15]
        ds_ref = refs[16] if with_sink else None
        qf, dof, of, kf, vf, dqf, dkf, dvf = refs[-8:]
        kvh, ib = pl.program_id(0), pl.program_id(1)
        for buf, cur_ref, nxt_ref in ((qf, q_ref, qn_ref), (dof, do_ref, don_ref), (of, o_ref, on_ref)):
            buf[:BT] = cur_ref[...].astype(F32)
            buf[BT:] = nxt_ref[...].astype(F32)
        kf[:SB] = kp_ref[...].astype(F32)
        kf[SB:] = kc_ref[...].astype(F32)
        vf[:SB] = vp_ref[...].astype(F32)
        vf[SB:] = vc_ref[...].astype(F32)
        band, band_first = _band_mask(max_dist, ib > 0)
        if with_sink:
            @pl.when(ib == 0)
            def _():
                ds_ref[...] = jnp.zeros_like(ds_ref)

        def grads(rows, logzs, keys, vals, mask):
            q = _stack([qf[rows, cols] for cols in heads]).astype(BF16)
            dout = _stack([dof[rows, cols] for cols in heads]).astype(BF16)
            s_all = _dot_nt(q, keys) * ATT_SCALE
            dp_all = _dot_nt(dout, vals)
            probs, dss, deltas = [], [], []
            for g, cols in enumerate(heads):
                delta = jnp.sum(dof[rows, cols] * of[rows, cols], axis=-1, keepdims=True)
                p = jnp.exp(jnp.where(mask, s_all[cols], NEG_INF) - logzs[g][:, :1])
                probs.append(p.astype(BF16))
                dss.append((p * (dp_all[cols] - delta) * ATT_SCALE).astype(BF16))
                deltas.append(delta)
            return q, dout, _stack(probs), _stack(dss), deltas

        row = lax.broadcasted_iota(jnp.int32, (HEAD, HEAD), 0)
        col = lax.broadcasted_iota(jnp.int32, (HEAD, HEAD), 1)
        reach = col >= row + jnp.where(ib < nib - 1, HEAD - max_dist, 2 * HEAD)
        for c in range(r):
            k_old, v_old = kf[_band_rows(c, r)], vf[_band_rows(c, r)]
            dk_own = dv_own = None
            for j in range(nsub):
                rows = _band_rows(j * SB + c, r)
                k_own, v_own = kf[_band_rows((j + 1) * SB + c, r)], vf[_band_rows((j + 1) * SB + c, r)]
                kcat = jnp.concatenate([k_old, k_own], axis=0).astype(BF16)
                vcat = jnp.concatenate([v_old, v_own], axis=0).astype(BF16)
                logzs = [l_ref[rows, cols] for cols in heads]
                q, dout, p, ds, deltas = grads(rows, logzs, kcat, vcat, band_first if j == 0 else band)
                dq = _dot(ds, kcat)
                for g, cols in enumerate(heads):
                    dqf[rows, cols] = dq[cols]
                    if with_sink:
                        p_sink = jnp.exp(sink_ref[kvh * grp + g] - logzs[g][:, :1])
                        ds_ref[g * 8:(g + 1) * 8] += jnp.sum(p_sink * deltas[g])
                dk, dv = _dot_tn(ds, q), _dot_tn(p, dout)
                if j > 0:
                    done = _band_rows((j - 1) * SB + c, r)
                    dkf[done] = dk_own + dk[:HEAD]
                    dvf[done] = dv_own + dv[:HEAD]
                dk_own, dv_own = dk[HEAD:], dv[HEAD:]
                k_old, v_old = k_own, v_own
            logzs = [ln_ref[_band_rows(c, r), cols] for cols in heads]
            q, dout, p, ds, _ = grads(_band_rows(BT + c, r), logzs, k_old.astype(BF16), v_old.astype(BF16), reach)
            done = _band_rows((nsub - 1) * SB + c, r)
            dkf[done] = dk_own + _dot_tn(ds, q)
            dvf[done] = dv_own + _dot_tn(p, dout)

        cs, sn = c_ref[...], s_ref[...]
        for cols in heads:
            out_ref[:, cols] = _unrope(dqf[:, cols], cs, sn).astype(BF16)
        out_ref[:, grp * HEAD:(grp + 1) * HEAD] = _unrope(dkf[...], cs, sn).astype(BF16)
        out_ref[:, (grp + 1) * HEAD:] = dvf[...].astype(BF16)

    def nxt_row(i):
        return jnp.minimum((i + 1) * nsub, nblk - 1)

    stride = grp + 2
    q_next = pl.BlockSpec((SB, grp * HEAD), lambda h, i: (nxt_row(i), (base + h * stride) // grp))
    head_cur = pl.BlockSpec((BT, grp * HEAD), lambda h, i: (i, h))
    head_next = pl.BlockSpec((SB, grp * HEAD), lambda h, i: (nxt_row(i), h))
    table = pl.BlockSpec((BT, HEAD), lambda h, i: (i, 0))

    in_specs = [*_band_specs(BT, SB, nsub, base, grp), q_next,
                head_cur, head_next, head_cur, head_next, head_cur, head_next, table, table, UNREAD]
    args = [qkv, qkv, qkv, qkv, qkv, qkv, do, do, o, o, lse, lse, cos, sin_signed, dqkv]
    out_specs = [pl.BlockSpec((BT, stride * HEAD), lambda h, i: (i, base // stride + h))]
    out_shape = [jax.ShapeDtypeStruct(dqkv.shape, dqkv.dtype)]
    if with_sink:
        in_specs.insert(0, pl.BlockSpec(memory_space=pltpu.SMEM))
        args.insert(0, sinks)
        out_specs.append(pl.BlockSpec((None, grp * 8, HEAD), lambda h, i: (h, 0, 0)))
        out_shape.append(jax.ShapeDtypeStruct((hkv, grp * 8, HEAD), F32))
    wide = pltpu.VMEM((BT + SB, grp * HEAD), F32)
    tall = pltpu.VMEM((SB + BT, HEAD), F32)
    grad = pltpu.VMEM((BT, HEAD), F32)
    return pl.pallas_call(
        body, name=name, grid=(hkv, nib), in_specs=in_specs, out_specs=out_specs, out_shape=out_shape,
        input_output_aliases={len(args) - 1: 0},
        scratch_shapes=[wide, wide, wide, tall, tall, pltpu.VMEM((BT, grp * HEAD), F32), grad, grad],
        compiler_params=pltpu.CompilerParams(dimension_semantics=("parallel", "arbitrary"),
                                             vmem_limit_bytes=VMEM_LIMIT_LARGE),
    )(*args)


M_HEADS = 4


def mem_kv(mem, g, w, name):
    n, D = mem.shape

    def body(m_ref, g_ref, w_ref, mn_ref, kv_ref):
        x = m_ref[...]
        mn = (x * _rstd(x) * g_ref[...]).astype(BF16)
        mn_ref[...] = mn
        kv_ref[...] = _dot(mn, w_ref[...]).astype(BF16)

    return pl.pallas_call(
        body, name=name,
        out_shape=[jax.ShapeDtypeStruct((n, D), BF16), jax.ShapeDtypeStruct((n, w.shape[1]), BF16)],
        compiler_params=pltpu.CompilerParams(vmem_limit_bytes=VMEM_LIMIT),
    )(mem, g, w)


def mem_fwd(qkv, mkv, name):
    T = qkv.shape[0]
    n = mkv.shape[0]
    RB = 1024

    def body(q_ref, kv_ref, o_ref, l_ref):
        for h in range(M_HEADS):
            cols = slice(h * HEAD, (h + 1) * HEAD)
            s = _dot_nt(q_ref[:, cols], kv_ref[:, cols]) * ATT_SCALE
            m = jnp.max(s, axis=-1, keepdims=True)
            p = jnp.exp(s - m)
            den = jnp.sum(p, axis=-1, keepdims=True)
            vals = kv_ref[:, (M_HEADS + h) * HEAD:(M_HEADS + h + 1) * HEAD]
            o_ref[:, cols] = (_dot(p.astype(BF16), vals) / den).astype(BF16)
            l_ref[:, cols] = jnp.broadcast_to(m + jnp.log(den), (RB, HEAD))

    out = pl.BlockSpec((RB, M_HEADS * HEAD), lambda i: (i, 0))
    return pl.pallas_call(
        body, name=name, grid=(T // RB,),
        in_specs=[pl.BlockSpec((RB, M_HEADS * HEAD), lambda i: (i, MQ // M_HEADS)), _resident(mkv)],
        out_specs=[out, out],
        out_shape=[jax.ShapeDtypeStruct((T, M_HEADS * HEAD), BF16), jax.ShapeDtypeStruct((T, M_HEADS * HEAD), F32)],
        compiler_params=_params("parallel"),
    )(qkv, mkv)


def mem_bwd(qkv, dqkv, mkv, do, o, lse, name):
    T = qkv.shape[0]
    n = mkv.shape[0]
    RB = 1024

    def body(q_ref, kv_ref, do_ref, o_ref, l_ref, _, dq_ref, dk_ref, dv_ref):
        @pl.when(pl.program_id(0) == 0)
        def _():
            dk_ref[...] = jnp.zeros_like(dk_ref)
            dv_ref[...] = jnp.zeros_like(dv_ref)

        for h in range(M_HEADS):
            cols = slice(h * HEAD, (h + 1) * HEAD)
            keys, vals = kv_ref[:, cols], kv_ref[:, (M_HEADS + h) * HEAD:(M_HEADS + h + 1) * HEAD]
            q, dout = q_ref[:, cols], do_ref[:, cols]
            delta = jnp.sum(dout.astype(F32) * o_ref[:, cols].astype(F32), axis=-1, keepdims=True)
            p = jnp.exp(_dot_nt(q, keys) * ATT_SCALE - l_ref[:, cols][:, :1])
            ds = (p * (_dot_nt(dout, vals) - delta) * ATT_SCALE).astype(BF16)
            dq_ref[:, cols] = _dot(ds, keys).astype(BF16)
            dk_ref[:, cols] += _dot_tn(ds, q)
            dv_ref[:, cols] += _dot_tn(p.astype(BF16), dout)

    wide = M_HEADS * HEAD
    tok = pl.BlockSpec((RB, wide), lambda i: (i, 0))
    q_cols = pl.BlockSpec((RB, wide), lambda i: (i, MQ // M_HEADS))
    slot = pl.BlockSpec((n, wide), lambda i: (0, 0))
    return pl.pallas_call(
        body, name=name, grid=(T // RB,),
        in_specs=[q_cols, _resident(mkv), tok, tok, tok, UNREAD],
        out_specs=[q_cols, slot, slot],
        out_shape=[jax.ShapeDtypeStruct(dqkv.shape, dqkv.dtype),
                   jax.ShapeDtypeStruct((n, wide), F32), jax.ShapeDtypeStruct((n, wide), F32)],
        input_output_aliases={5: 0},
        compiler_params=_params("arbitrary"),
    )(qkv, mkv, do, o, lse, dqkv)


def mem_kv_bwd(mem, g, mem_n, w, dmkv, name):
    n, D = mem.shape

    def body(m_ref, g_ref, mn_ref, w_ref, d_ref, dw_ref, dg_ref):
        d = d_ref[...].astype(BF16)
        dw_ref[...] = _dot_tn(mn_ref[...], d)
        x = m_ref[...]
        dg_ref[...] = jnp.sum(_dot_nt(d, w_ref[...]) * (x * _rstd(x)), axis=0, keepdims=True)

    return pl.pallas_call(
        body, name=name,
        out_shape=[jax.ShapeDtypeStruct(w.shape, F32), jax.ShapeDtypeStruct((1, D), F32)],
        compiler_params=pltpu.CompilerParams(vmem_limit_bytes=VMEM_LIMIT),
    )(mem, g, mem_n, w, dmkv)


def _rms_bwd(dn, f, g):
    r = _rstd(f)
    fhat = f * r
    dfhat = dn * g
    df = r * (dfhat - fhat * jnp.mean(dfhat * fhat, axis=-1, keepdims=True))
    return df, jnp.sum(dn * fhat, axis=0, keepdims=True)


def ffn_tokens_bwd(dh, f, h_in, gu, g_pre, g_post, w_in, w_out, coef, name, after):
    T, D = dh.shape

    def body(dh_ref, f_ref, h_ref, gu_ref, gpre_ref, gpost_ref, win_ref, wout_ref, _,
             df_ref, dgu_ref, dhin_ref, dgpre_ref, dgpost_ref, dxn_ref):
        i, j = pl.program_id(0), pl.program_id(1)

        @pl.when(j == 0)
        def _():
            @pl.when(i == 0)
            def _():
                dgpre_ref[...] = jnp.zeros_like(dgpre_ref)
                dgpost_ref[...] = jnp.zeros_like(dgpost_ref)

            df, dg_post = _rms_bwd(coef * dh_ref[...], f_ref[...], gpost_ref[...])
            dgpost_ref[...] += dg_post
            df_ref[...] = df.astype(BF16)

        for jj in range(2):
            @pl.when(j == jj)
            def _(jj=jj):
                lo, mid, hi = 2 * jj * FF_T, (2 * jj + 1) * FF_T, (2 * jj + 2) * FF_T
                da = _dot_nt(df_ref[...], wout_ref[jj * FF_T:(jj + 1) * FF_T, :])
                gate = gu_ref[:, :FF_T].astype(F32)
                up = gu_ref[:, FF_T:].astype(F32)
                sig = _sigmoid(gate)
                dgate = (da * up * sig * (1.0 + gate * (1.0 - sig))).astype(BF16)
                dup = (da * gate * sig).astype(BF16)
                dgu_ref[:, :FF_T] = dgate
                dgu_ref[:, FF_T:] = dup
                part = _dot_nt(dgate, win_ref[:, lo:mid]) + _dot_nt(dup, win_ref[:, mid:hi])
                if jj == 0:
                    dxn_ref[...] = part
                else:
                    h = h_ref[...]
                    r = _rstd(h)
                    xhat = h * r
                    dxn = dxn_ref[...] + part
                    dxhat = dxn * gpre_ref[...]
                    dhin_ref[...] = dh_ref[...] + r * (dxhat - xhat * jnp.mean(dxhat * xhat, axis=-1, keepdims=True))
                    dgpre_ref[...] += jnp.sum(dxn * xhat, axis=0, keepdims=True)

    row = pl.BlockSpec((TM, D), lambda i, j: (i, 0))
    wide = pl.BlockSpec((TM, 2 * FF_T), lambda i, j: (i, j))
    vec = pl.BlockSpec((1, D), lambda i, j: (0, 0))
    return pl.pallas_call(
        body, name=name, grid=(T // TM, 2),
        in_specs=[row, row, row, wide, _resident(g_pre), _resident(g_post), _resident(w_in), _resident(w_out),
                  UNREAD],
        out_specs=[row, wide, row, vec, vec],
        out_shape=[jax.ShapeDtypeStruct((T, D), BF16), jax.ShapeDtypeStruct((T, 2 * D_FF), BF16),
                   jax.ShapeDtypeStruct((T, D), F32), jax.ShapeDtypeStruct((1, D), F32),
                   jax.ShapeDtypeStruct((1, D), F32)],
        scratch_shapes=[pltpu.VMEM((TM, D), F32)],
        compiler_params=pltpu.CompilerParams(dimension_semantics=("arbitrary", "arbitrary"),
                                             vmem_limit_bytes=VMEM_LIMIT_LARGE),
    )(dh, f, h_in, gu, g_pre, g_post, w_in, w_out, after)


def mm_nt_norm_bwd(pieces, h_in, dh_out, g, name, after):
    T, D = h_in.shape

    def body(*refs):
        ab = refs[:2 * len(pieces)]
        h_ref, dh_ref, g_ref, _, o_ref, dg_ref = refs[2 * len(pieces):]
        dxn = _dot_nt(ab[0][...], ab[1][...])
        for p in range(1, len(pieces)):
            dxn += _dot_nt(ab[2 * p][...], ab[2 * p + 1][...])
        h = h_ref[...]
        r = _rstd(h)
        xhat = h * r
        dxhat = dxn * g_ref[...]
        o_ref[...] = dh_ref[...] + r * (dxhat - xhat * jnp.mean(dxhat * xhat, axis=-1, keepdims=True))

        @pl.when(pl.program_id(0) == 0)
        def _():
            dg_ref[...] = jnp.zeros_like(dg_ref)

        dg_ref[...] += jnp.sum(dxn * xhat, axis=0, keepdims=True)

    in_specs, args = [], []
    for a, w in pieces:
        in_specs += [pl.BlockSpec((TM, a.shape[1]), lambda i: (i, 0)), _resident(w)]
        args += [a, w]
    row = pl.BlockSpec((TM, D), lambda i: (i, 0))
    return pl.pallas_call(
        body, name=name, grid=(T // TM,),
        in_specs=in_specs + [row, row, _resident(g), UNREAD],
        out_specs=[row, pl.BlockSpec((1, D), lambda i: (0, 0))],
        out_shape=[jax.ShapeDtypeStruct((T, D), F32), jax.ShapeDtypeStruct((1, D), F32)],
        compiler_params=_params("arbitrary"),
    )(*args, h_in, dh_out, g, after)


def gate_merge_out_bwd(dh, f, g, w_out, merged, gt, o_a, o_b, o_m, w_a, w_b, w_m, name, after):
    T = dh.shape[0]
    D = D_MODEL
    branch = ((o_a, w_a), (o_b, w_b), (o_m, w_m))

    def body(dh_ref, f_ref, g_ref, wo_ref, m_ref, gt_ref, oa_ref, ob_ref, om_ref, wa_ref, wb_ref, wm_ref, _,
             dg_ref, dwo_ref, dgt_ref, doa_ref, dob_ref, dom_ref, db_ref, dwa_ref, dwb_ref, dwm_ref):
        @pl.when(pl.program_id(0) == 0)
        def _():
            for acc in (dg_ref, dwo_ref, db_ref, dwa_ref, dwb_ref, dwm_ref):
                acc[...] = jnp.zeros_like(acc)

        df, dg = _rms_bwd(dh_ref[...], f_ref[...], g_ref[...])
        dg_ref[...] += dg
        df = df.astype(BF16)
        dwo_ref[...] += _dot_tn(m_ref[...], df)
        dmf = _dot_nt(df, wo_ref[...])
        for x, (o_ref, w_ref, do_ref, dw_ref) in enumerate(((oa_ref, wa_ref, doa_ref, dwa_ref),
                                                           (ob_ref, wb_ref, dob_ref, dwb_ref),
                                                           (om_ref, wm_ref, dom_ref, dwm_ref))):
            cols = slice(x * D, (x + 1) * D)
            gx = gt_ref[:, cols].astype(F32)
            w = w_ref[...]
            dpre = dmf * _dot(o_ref[...], w) * gx * (1.0 - gx)
            dgt_ref[:, cols] = dpre.astype(BF16)
            db_ref[:, cols] += jnp.sum(dpre, axis=0, keepdims=True)
            dp = (dmf * gx).astype(BF16)
            do_ref[...] = _dot_nt(dp, w).astype(BF16)
            dw_ref[...] += _dot_tn(dp, o_ref[...])

    def rows(width):
        return pl.BlockSpec((TM, width), lambda i: (i, 0))

    def kept(shape):
        return pl.BlockSpec(shape, lambda i: (0,) * len(shape))

    widths = [o.shape[1] for o, _ in branch]
    sums = [(1, D), (D, D), (1, 3 * D)] + [(D, k) for k in widths]
    return pl.pallas_call(
        body, name=name, grid=(T // TM,),
        in_specs=[rows(D), rows(D), _resident(g), _resident(w_out), rows(D), rows(3 * D)]
                 + [rows(k) for k in widths] + [_resident(w) for _, w in branch] + [UNREAD],
        out_specs=[kept(sums[0]), kept(sums[1]), rows(3 * D)] + [rows(k) for k in widths]
                  + [kept(shape) for shape in sums[2:]],
        out_shape=[jax.ShapeDtypeStruct(sums[0], F32), jax.ShapeDtypeStruct(sums[1], F32),
                   jax.ShapeDtypeStruct((T, 3 * D), BF16)] + [jax.ShapeDtypeStruct((T, k), BF16) for k in widths]
                  + [jax.ShapeDtypeStruct(shape, F32) for shape in sums[2:]],
        compiler_params=pltpu.CompilerParams(dimension_semantics=("arbitrary",), vmem_limit_bytes=VMEM_LIMIT_LARGE),
    )(dh, f, g, w_out, merged, gt, o_a, o_b, o_m, w_a, w_b, w_m, after)


def mm_tn(x, dy, tm, tn, name, shard_major=False, perm=None, slabs=1, after=None, wire=False):
    T, M = x.shape
    N = dy.shape[1]
    tk = min(2048, T)
    perm = perm or (lambda j: j)
    w = tn // slabs

    def body(x_ref, dy_ref, *rest):
        o_ref = rest[-2] if wire else rest[-1]

        @pl.when(pl.program_id(2) == 0)
        def _():
            o_ref[...] = jnp.zeros_like(o_ref)

        acc = _dot_tn(x_ref[...], dy_ref[...])
        if shard_major:
            for s in range(slabs):
                o_ref[s] += acc[:, s * w:(s + 1) * w]
        else:
            o_ref[...] += acc
        if wire:
            @pl.when(pl.program_id(2) == T // tk - 1)
            def _():
                rest[-1][...] = o_ref[...].astype(BF16)

    if shard_major:
        out_spec = pl.BlockSpec((slabs, tm, w), lambda i, j, k: (perm(j), i, 0))
        out_shape = jax.ShapeDtypeStruct((N // w, M, w), F32)
    else:
        out_spec = pl.BlockSpec((tm, tn), lambda i, j, k: (i, j))
        out_shape = jax.ShapeDtypeStruct((M, N), F32)
    return pl.pallas_call(
        body, name=name, grid=(M // tm, N // tn, T // tk),
        in_specs=[pl.BlockSpec((tk, tm), lambda i, j, k: (k, i)),
                  pl.BlockSpec((tk, tn), lambda i, j, k: (k, j))] + ([] if after is None else [UNREAD]),
        out_specs=[out_spec, out_spec] if wire else out_spec,
        out_shape=[out_shape, jax.ShapeDtypeStruct(out_shape.shape, BF16)] if wire else out_shape,
        compiler_params=_params("parallel", "parallel", "arbitrary"),
    )(x, dy, *([] if after is None else [after]))


def rope_tables(T, zero):
    half = HEAD // 2
    inv = ROPE_THETA ** (-jnp.arange(half, dtype=F32) / half)
    ang = (jnp.arange(T).astype(F32) + zero)[:, None] * inv[None, :]
    cos, sin = jnp.cos(ang), jnp.sin(ang)
    return jnp.concatenate([cos, cos], axis=1), jnp.concatenate([-sin, sin], axis=1)


def layer_step(x, mem, target, gains, sinks, b_gate, weights_of, send_grads, zero):
    T = x.shape[0]
    cos, sin_signed = rope_tables(T, zero)
    no_sink = jnp.full((2,), NEG_INF, F32)

    xn1 = rms_scale(x, gains["ffn1_norm_pre"], "ffn1_norm", cos)
    w = dict(weights_of("ffn1_in", xn1))
    _, gu1, a1 = ffn_in(x, gains["ffn1_norm_pre"], w["ffn1_w_in"], "ffn1_in", xn=xn1)
    w.update(weights_of("ffn1_out", a1))
    f1, h1 = mm_norm_res(a1, w["ffn1_w_out"], x, gains["ffn1_norm_post"], 0.5, "ffn1_out")
    w.update(weights_of("mix_in", f1))
    u, qkv, gt = mix_in(h1, gains["mix_norm_pre"], w["w_in"], w["w_gate"], b_gate, cos, sin_signed, "mix_in")
    w.update(weights_of("mix_rest", u))
    outs, lses = [], []
    for gidx, (window, dil) in enumerate(DIL):
        last = gidx == len(DIL) - 1
        o_g, l_g = band_fwd(qkv, no_sink, r=dil, base=A_BASE + 6 * gidx, hkv=2, grp=1, max_dist=window // dil,
                            out_dtype=BF16 if last else F32, name=f"attn_a{gidx}_fwd",
                            merge=(outs, lses) if last else None)
        outs.append(o_g)
        lses.append(l_g)
    o_a, l_a = outs[-1], lses[-1]
    o_b, l_b = band_fwd(qkv, sinks, r=1, base=B_BASE, hkv=2, grp=2, max_dist=HEAD - 1, out_dtype=BF16,
                        name="attn_b_fwd")
    mem_n, mkv = mem_kv(mem, gains["mem_norm"], w["w_mem_kv"], "mem_kv")
    o_m, l_m = mem_fwd(qkv, mkv, "attn_m_fwd")
    merged, mo, h2 = gate_merge_out(gt, o_a, o_b, o_m, w["w_o_a"], w["w_o_b"], w["w_o_m"], w["w_out"], h1,
                                    gains["mix_norm_post"], "gate_merge_out")
    w.update(weights_of("ffn2", mo))
    xn2, gu2, a2 = ffn_in(h2, gains["ffn2_norm_pre"], w["ffn2_w_in"], "ffn2_in")
    f2, dy, sq = mm_norm_res(a2, w["ffn2_w_out"], h2, gains["ffn2_norm_post"], 0.5, "ffn2_out", target=target)

    grads = {}

    def ffn_bwd(tag, dh_out, f, gu, a, xn, h_in, after):
        df, dgu, dh_in, grads[f"{tag}_norm_pre"], grads[f"{tag}_norm_post"] = ffn_tokens_bwd(
            dh_out, f, h_in, gu, gains[f"{tag}_norm_pre"], gains[f"{tag}_norm_post"], w[f"{tag}_w_in"],
            w[f"{tag}_w_out"], 0.5, f"{tag}_tokens_bwd", after)
        sent = send_grads(f"{tag}_in", {f"{tag}_w_in": mm_tn(
            xn, dgu, D_MODEL, FF_T, f"{tag}_w_in_grad", shard_major=True, perm=_ffn_perm, wire=True)})
        sent = send_grads(f"{tag}_out", {f"{tag}_w_out": mm_tn(
            a, df, FF_T, D_MODEL, f"{tag}_w_out_grad", after=sent, wire=True)})
        return dh_in, sent

    dh2, sent = ffn_bwd("ffn2", dy, f2, gu2, a2, xn2, h2, dy)

    mix = {}
    (grads["mix_norm_post"], mix["w_out"], dgt, do_a, do_b, do_m, grads["b_gate"],
     dwa_t, dwb_t, dwm_t) = gate_merge_out_bwd(
        dh2, mo, gains["mix_norm_post"], w["w_out"], merged, gt, o_a, o_b, o_m, w["w_o_a"], w["w_o_b"],
        w["w_o_m"], "gate_merge_out_bwd", sent)
    mix["w_o_a"], mix["w_o_b"], mix["w_o_m"] = dwa_t.T, dwb_t.T, dwm_t.T

    dqkv = lax.empty(qkv.shape, qkv.dtype)
    for gidx, (window, dil) in enumerate(DIL):
        dqkv, = band_bwd(qkv, dqkv, do_a, o_a, l_a, cos, sin_signed, None, r=dil, base=A_BASE + 6 * gidx, hkv=2,
                         grp=1, max_dist=window // dil, name=f"attn_a{gidx}_bwd")
    dqkv, dsink = band_bwd(qkv, dqkv, do_b, o_b, l_b, cos, sin_signed, sinks, r=1, base=B_BASE, hkv=2, grp=2,
                           max_dist=HEAD - 1, name="attn_b_bwd")
    grads["sinks"] = -dsink[:, ::8, 0].reshape(1, 4)
    dqkv, dmk, dmv = mem_bwd(qkv, dqkv, mkv, do_m, o_m, l_m, "attn_m_bwd")
    mix["w_mem_kv"], grads["mem_norm"] = mem_kv_bwd(
        mem, gains["mem_norm"], mem_n, w["w_mem_kv"], jnp.concatenate([dmk, dmv], axis=1), "mem_kv_bwd")

    mix["w_in"] = mm_tn(u, dqkv, D_MODEL, 1280, "w_in_grad")
    mix["w_gate"] = mm_tn(u, dgt, D_MODEL, 1536, "w_gate_grad", shard_major=True, slabs=2, wire=True)
    sent = send_grads("mix", mix)
    dh1, grads["mix_norm_pre"] = mm_nt_norm_bwd(
        [(dqkv, w["w_in"]), (dgt, w["w_gate"])], h1, dh2, gains["mix_norm_pre"], "mix_in_bwd", sent)

    dx, _ = ffn_bwd("ffn1", dh1, f1, gu1, a1, xn1, x, dh1)
    return sq, dx, grads


def _place():
    return lax.axis_index("x"), lax.axis_index("y"), lax.axis_index("c")


def _other_chips(x, y):
    return [(1 - x, y), (x, 1 - y), (1 - x, 1 - y)]


def _hbm(n):
    return [pl.BlockSpec(memory_space=pltpu.HBM)] * n


SEM = pl.BlockSpec(memory_space=pltpu.SEMAPHORE)
SIDE_EFFECT = pltpu.SideEffectType.DATAFLOW_SIDE_EFFECTING


def _chip_copy(src, land, sems, i, j, dst_slot, scatter):
    x, y, c = _place()
    px, py = _other_chips(x, y)[j]
    send_sems, recv_sems = sems
    return pltpu.make_async_remote_copy(
        src_ref=src[i].at[2 * px + py] if scatter else src[i], dst_ref=land[i].at[dst_slot],
        send_sem=send_sems.at[3 * i + j], recv_sem=recv_sems.at[3 * i + j],
        device_id=(px, py, c), device_id_type=MESH)


def chip_copies_start(srcs, lands, groups, scatter, name, after=None):
    n = len(srcs)

    def body(*refs):
        src, land = refs[:n], refs[n:2 * n]
        first_sem = 2 * n + (after is not None)
        sems = refs[first_sem:first_sem + 2 * len(groups)]
        token = refs[-1]
        x, y, _ = _place()
        for g, members in enumerate(groups):
            part = ([src[i] for i in members], [land[i] for i in members])
            for t in range(len(members)):
                for j in range(3):
                    _chip_copy(*part, sems[2 * g:2 * g + 2], t, j, 2 * x + y, scatter).start()
        token[...] = jnp.zeros_like(token)

    sem_shapes = [pltpu.SemaphoreType.DMA((3 * len(m),)) for m in groups for _ in range(2)]
    thru = [pltpu.HBM(a.shape, a.dtype) for a in (*srcs, *lands)]
    res = pl.pallas_call(
        body, name=name,
        out_shape=(*sem_shapes, *thru, jax.ShapeDtypeStruct((8, 128), F32)),
        in_specs=_hbm(2 * n) + ([] if after is None else [UNREAD]),
        out_specs=(*[SEM] * len(sem_shapes), *_hbm(2 * n), pl.BlockSpec(memory_space=pltpu.VMEM)),
        input_output_aliases={i: len(sem_shapes) + i for i in range(2 * n)},
        compiler_params=pltpu.CompilerParams(has_side_effects=SIDE_EFFECT),
    )(*[pltpu.with_memory_space_constraint(a, pltpu.HBM) for a in (*srcs, *lands)],
      *([] if after is None else [after]))
    k = len(sem_shapes)
    sems = [tuple(res[2 * g:2 * g + 2]) for g in range(len(groups))]
    return sems, list(res[k:k + n]), list(res[k + n:k + 2 * n]), res[-1]


def chip_copies_wait(srcs, lands, sems, after, scatter, name):
    n = len(srcs)
    after = list(after) if isinstance(after, (list, tuple)) else [after]

    def body(*refs):
        src, land = refs[:n], refs[n:2 * n]
        pair = refs[2 * n:2 * n + 2]
        x, y, _ = _place()
        for i in range(n):
            for j, (px, py) in enumerate(_other_chips(x, y)):
                copy = _chip_copy(src, land, pair, i, j, 2 * px + py, scatter)
                copy.wait_send()
                copy.wait_recv()

    res = pl.pallas_call(
        body, name=name,
        out_shape=[pltpu.HBM(a.shape, a.dtype) for a in (*srcs, *lands)],
        in_specs=[*_hbm(2 * n), SEM, SEM] + [UNREAD] * len(after),
        out_specs=_hbm(2 * n),
        input_output_aliases={i: i for i in range(2 * n)},
        compiler_params=pltpu.CompilerParams(has_side_effects=SIDE_EFFECT),
    )(*srcs, *lands, *sems, *after)
    return list(res[n:])


def small_all_gather(small, name):
    flips = [(fx, fy, fc) for fx in (0, 1) for fy in (0, 1) for fc in (0, 1)][1:]

    def body(in_ref, out_ref, send_sems, recv_sems, local_sem):
        x, y, c = _place()
        me = 4 * x + 2 * y + c

        def copy(k, slot):
            fx, fy, fc = flips[k]
            return pltpu.make_async_remote_copy(
                src_ref=in_ref, dst_ref=out_ref.at[slot], send_sem=send_sems.at[k], recv_sem=recv_sems.at[k],
                device_id=(x ^ fx, y ^ fy, c ^ fc), device_id_type=MESH)

        local = pltpu.make_async_copy(in_ref, out_ref.at[me], local_sem)
        local.start()
        for k in range(len(flips)):
            copy(k, me).start()
        for k, (fx, fy, fc) in enumerate(flips):
            copy(k, 4 * (x ^ fx) + 2 * (y ^ fy) + (c ^ fc)).wait()
        local.wait()

    return pl.pallas_call(
        body, name=name, in_specs=_hbm(1), out_specs=_hbm(1)[0],
        out_shape=jax.ShapeDtypeStruct((N_DEV,) + small.shape, small.dtype),
        scratch_shapes=[pltpu.SemaphoreType.DMA((len(flips),)), pltpu.SemaphoreType.DMA((len(flips),)),
                        pltpu.SemaphoreType.DMA],
    )(small)


def _sibling_copy(src, land, sems, i):
    x, y, c = _place()
    return pltpu.make_async_remote_copy(
        src_ref=src[i], dst_ref=land[i], send_sem=sems[0].at[i], recv_sem=sems[1].at[i],
        device_id=(x, y, 1 - c), device_id_type=MESH)


def sibling_copies_start(parts, name):
    n = len(parts)
    lands = [lax.empty(p.shape, p.dtype) for p in parts]

    def body(*refs):
        src, land, sems, token = refs[:n], refs[n:2 * n], refs[2 * n:2 * n + 2], refs[-1]
        for i in range(n):
            _sibling_copy(src, land, sems, i).start()
        token[...] = jnp.zeros_like(token)

    res = pl.pallas_call(
        body, name=name,
        out_shape=(pltpu.SemaphoreType.DMA((n,)), pltpu.SemaphoreType.DMA((n,)),
                   *[pltpu.HBM(a.shape, a.dtype) for a in (*parts, *lands)], jax.ShapeDtypeStruct((8, 128), F32)),
        in_specs=_hbm(2 * n),
        out_specs=(SEM, SEM, *_hbm(2 * n), pl.BlockSpec(memory_space=pltpu.VMEM)),
        input_output_aliases={i: 2 + i for i in range(2 * n)},
        compiler_params=pltpu.CompilerParams(has_side_effects=SIDE_EFFECT),
    )(*[pltpu.with_memory_space_constraint(a, pltpu.HBM) for a in (*parts, *lands)])
    return tuple(res[:2]), list(res[2:2 + n]), list(res[2 + n:2 + 2 * n]), res[-1]


def sibling_copies_wait(parts, lands, sems, after, name):
    n = len(parts)

    def body(*refs):
        src, land, sems = refs[:n], refs[n:2 * n], refs[2 * n:2 * n + 2]
        for i in range(n):
            copy = _sibling_copy(src, land, sems, i)
            copy.wait_send()
            copy.wait_recv()

    res = pl.pallas_call(
        body, name=name,
        out_shape=[pltpu.HBM(a.shape, a.dtype) for a in (*parts, *lands)],
        in_specs=[*_hbm(2 * n), SEM, SEM, UNREAD],
        out_specs=_hbm(2 * n),
        input_output_aliases={i: i for i in range(2 * n)},
        compiler_params=pltpu.CompilerParams(has_side_effects=SIDE_EFFECT),
    )(*parts, *lands, *sems, after)
    return list(res[n:])


def _row_tile(rows):
    for t in (256, 176, 128, 64, 32, 16, 8):
        if rows % t == 0:
            return t
    return rows


def chip_partial_sum(me, own_sm, recv, name):
    _, rows, cols = own_sm.shape
    tr = _row_tile(rows)

    def body(me_ref, own_ref, r1, r2, r3, o_ref):
        o_ref[...] = own_ref[...] + r1[...].astype(F32) + r2[...].astype(F32) + r3[...].astype(F32)

    def slot(d):
        return pl.BlockSpec((None, tr, cols), lambda i, me_ref: ((me_ref[0] + d) % N_CHIPS, i, 0))

    return pl.pallas_call(
        body, name=name,
        grid_spec=pltpu.PrefetchScalarGridSpec(
            num_scalar_prefetch=1, grid=(rows // tr,),
            in_specs=[slot(0), slot(1), slot(2), slot(3)],
            out_specs=pl.BlockSpec((tr, cols), lambda i, me_ref: (i, 0))),
        out_shape=jax.ShapeDtypeStruct((rows, cols), F32),
        compiler_params=_params("parallel"),
    )(me, own_sm, recv, recv, recv)


def _adamw(w, g, m, v):
    m = ADAM_B1 * m + (1.0 - ADAM_B1) * g
    v = ADAM_B2 * v + (1.0 - ADAM_B2) * (g * g)
    m_hat = m / (1.0 - ADAM_B1 ** ADAM_STEP)
    v_hat = v / (1.0 - ADAM_B2 ** ADAM_STEP)
    delta = -ADAM_LR * (m_hat / (jnp.sqrt(v_hat) + ADAM_EPS) + ADAM_WD * w)
    return delta, m, v


def adamw_pair(part, sib, w, m, v, name):
    rows, cols = w.shape
    tr = _row_tile(rows)

    def body(p_ref, s_ref, w_ref, m_ref, v_ref, g_ref, d_ref, nm_ref, nv_ref):
        g = p_ref[...] + s_ref[...]
        g_ref[...] = g
        d_ref[...], nm_ref[...], nv_ref[...] = _adamw(w_ref[...], g, m_ref[...], v_ref[...])

    spec = pl.BlockSpec((tr, cols), lambda i: (i, 0))
    return pl.pallas_call(
        body, name=name, grid=(rows // tr,), in_specs=[spec] * 5, out_specs=[spec] * 4,
        out_shape=[jax.ShapeDtypeStruct((rows, cols), F32)] * 4,
        compiler_params=_params("parallel"),
    )(part, sib, w, m, v)


def adamw_small(g_all, w, m, v, name):
    def body(ga_ref, w_ref, m_ref, v_ref, g_ref, d_ref, nm_ref, nv_ref):
        g = ga_ref[0]
        for k in range(1, N_DEV):
            g = g + ga_ref[k]
        g_ref[...] = g
        d_ref[...], nm_ref[...], nv_ref[...] = _adamw(w_ref[...], g, m_ref[...], v_ref[...])

    return pl.pallas_call(
        body, name=name, out_shape=[jax.ShapeDtypeStruct(w.shape, F32)] * 4,
    )(g_all, w, m, v)


WEIGHTS = ("ffn1_norm_pre", "ffn1_w_in", "ffn1_w_out", "ffn1_norm_post", "mix_norm_pre", "w_in", "sinks",
           "mem_norm", "w_mem_kv", "w_gate", "b_gate", "w_o_a", "w_o_b", "w_o_m", "w_out", "mix_norm_post",
           "ffn2_norm_pre", "ffn2_w_in", "ffn2_w_out", "ffn2_norm_post")
GATHER_STAGES = (("ffn1_in", "ffn1_out"), ("mix_in",), ("mix_rest", "ffn2"))
GATHER_GROUPS = {"ffn1_in": ("ffn1_w_in",), "ffn1_out": ("ffn1_w_out",),
                 "mix_in": ("w_in", "w_gate"), "mix_rest": ("w_mem_kv", "w_o_a", "w_o_b", "w_o_m", "w_out"),
                 "ffn2": ("ffn2_w_in", "ffn2_w_out")}
GROUPS = {"ffn1_in": ("ffn1_w_in",), "ffn1_out": ("ffn1_w_out",),
          "mix": ("w_in", "w_gate", "w_mem_kv", "w_o_a", "w_o_b", "w_o_m", "w_out"),
          "ffn2_in": ("ffn2_w_in",), "ffn2_out": ("ffn2_w_out",)}
COLUMN_SHARDED = ("ffn1_w_in", "ffn2_w_in", "w_in", "w_gate", "w_o_a", "w_o_b", "w_o_m")
KEPT_SHARD_MAJOR = ("ffn1_w_in", "ffn2_w_in", "w_gate")
GAINS = ("ffn1_norm_pre", "ffn1_norm_post", "mix_norm_pre", "mem_norm", "mix_norm_post", "ffn2_norm_pre",
         "ffn2_norm_post")
SMALL_ROWS = 16


def _pack_small(t):
    sinks = jnp.pad(t["sinks"], ((0, 0), (0, D_MODEL - t["sinks"].shape[1])))
    rows = [t[k] for k in GAINS] + [t["b_gate"].reshape(3, D_MODEL), sinks]
    packed = jnp.concatenate(rows, axis=0)
    return jnp.pad(packed, ((0, SMALL_ROWS - packed.shape[0]), (0, 0)))


def _unpack_small(p):
    out = {k: p[i:i + 1] for i, k in enumerate(GAINS)}
    out["b_gate"] = p[7:10].reshape(1, 3 * D_MODEL)
    out["sinks"] = p[10:11, :4]
    return out


def kernel(x, mem, ffn1_norm_pre, ffn1_w_in, ffn1_w_out, ffn1_norm_post, mix_norm_pre, w_in, sinks, mem_norm, w_mem_kv, w_gate, b_gate, w_o_a, w_o_b, w_o_m, w_out, mix_norm_post, ffn2_norm_pre, ffn2_w_in, ffn2_w_out, ffn2_norm_post, loss_target, m_ffn1_norm_pre, m_ffn1_w_in, m_ffn1_w_out, m_ffn1_norm_post, m_mix_norm_pre, m_w_in, m_sinks, m_mem_norm, m_w_mem_kv, m_w_gate, m_b_gate, m_w_o_a, m_w_o_b, m_w_o_m, m_w_out, m_mix_norm_post, m_ffn2_norm_pre, m_ffn2_w_in, m_ffn2_w_out, m_ffn2_norm_post, v_ffn1_norm_pre, v_ffn1_w_in, v_ffn1_w_out, v_ffn1_norm_post, v_mix_norm_pre, v_w_in, v_sinks, v_mem_norm, v_w_mem_kv, v_w_gate, v_b_gate, v_w_o_a, v_w_o_b, v_w_o_m, v_w_out, v_mix_norm_post, v_ffn2_norm_pre, v_ffn2_w_in, v_ffn2_w_out, v_ffn2_norm_post):
    given = dict(locals())
    wt = {k: given[k] for k in WEIGHTS}
    mom = {k: given["m_" + k] for k in WEIGHTS}
    var = {k: given["v_" + k] for k in WEIGHTS}
    chip = (2 * lax.axis_index("x") + lax.axis_index("y")).astype(jnp.int32)
    me = chip.reshape(1)

    def landing_zone(own):
        return lax.dynamic_update_slice_in_dim(lax.empty((N_CHIPS,) + own.shape, own.dtype), own[None], chip, 0)

    started = {}
    tokens = []

    def stage_keys(stage):
        return [k for g in GATHER_STAGES[stage] for k in GATHER_GROUPS[g]]

    def prepare(stage):
        shards = [(wt[k][0] + tokens[0][0, 0] if tokens else wt[k][0]).astype(BF16) for k in stage_keys(stage)]
        return shards, [landing_zone(s) for s in shards]

    def start_gather(stage, after):
        groups, keys = GATHER_STAGES[stage], stage_keys(stage)
        members = [[keys.index(k) for k in GATHER_GROUPS[g]] for g in groups]
        sems, shards, lands, token = chip_copies_start(
            *prepared[stage], members, False, f"weight_gather_start_{stage}", after)
        tokens.append(token)
        for g, idx, pair in zip(groups, members, sems):
            started[g] = ([shards[i] for i in idx], [lands[i] for i in idx], pair)

    prepared = {0: prepare(0)}
    start_gather(0, None)
    prepared.update({stage: prepare(stage) for stage in range(1, len(GATHER_STAGES))})

    def weights_of(group, after):
        if group == GATHER_STAGES[0][0]:
            after = [after] + [a for stage in range(1, len(GATHER_STAGES)) for part in prepared[stage] for a in part]
        got = chip_copies_wait(*started[group], after, False, f"weight_gather_wait_{group}")
        stage = [s + 1 for s, groups in enumerate(GATHER_STAGES[:-1]) if groups[0] == group]
        if stage:
            start_gather(stage[0], got[0])
        full = {}
        for k, g in zip(GATHER_GROUPS[group], got):
            if k in COLUMN_SHARDED:
                if k in ("ffn1_w_in", "ffn2_w_in"):
                    g = jnp.stack([g[0], g[2], g[1], g[3]])
                full[k] = jnp.swapaxes(g, 0, 1).reshape(g.shape[1], N_CHIPS * g.shape[2])
                if k == "w_in":
                    full[k] = to_kernel_heads(full[k])
            else:
                full[k] = g.reshape(N_CHIPS * g.shape[1], g.shape[2])
        return full

    in_flight = {}

    def send_grads(group, grads):
        def shard_major(k, g):
            if k in KEPT_SHARD_MAJOR:
                return g
            if k in COLUMN_SHARDED:
                return jnp.swapaxes(g.reshape(g.shape[0], N_CHIPS, g.shape[1] // N_CHIPS), 0, 1)
            return g.reshape(N_CHIPS, g.shape[0] // N_CHIPS, g.shape[1])

        own, wire = [], []
        for k in GROUPS[group]:
            g, rounded = grads[k] if isinstance(grads[k], (tuple, list)) else (grads[k], None)
            g = shard_major(k, from_kernel_heads(g) if k == "w_in" else g)
            own.append(g)
            wire.append(g.astype(BF16) if rounded is None else shard_major(k, rounded))
        zones = [lax.empty(b.shape, b.dtype) for b in wire]
        pair, wire, zones, sent = chip_copies_start(
            wire, zones, [list(range(len(wire)))], True, f"grad_scatter_start_{group}")
        in_flight[group] = (own, wire, zones, pair[0], sent)
        return sent

    gains = {k: wt[k] for k in GAINS}
    sq, dx, grads = layer_step(
        x[0], mem[0], loss_target[0], gains, sinks[0], b_gate, weights_of, send_grads, tokens[0][0, 0])
    loss = lax.psum(0.5 * jnp.sum(sq) / D_MODEL, ("x", "y", "c"))

    res = {}
    after = in_flight["ffn1_out"][4]
    swaps = []
    for stage in (("ffn2_in", "ffn2_out", "mix", "ffn1_in"), ("ffn1_out",)):
        names, parts = [], []
        for group in stage:
            own, wire, zones, pair, _ = in_flight[group]
            received = chip_copies_wait(wire, zones, pair, after, True, f"grad_scatter_wait_{group}")
            for k, g, r in zip(GROUPS[group], own, received):
                names.append(k)
                parts.append(chip_partial_sum(me, g, r, f"{k}_chip_sum"))
        pair, parts, lands, after = sibling_copies_start(parts, f"sibling_start_{stage[-1]}")
        swaps.append((stage[-1], names, parts, lands, pair))
    small_all = small_all_gather(_pack_small(grads), "small_grad_gather")
    packed = adamw_small(small_all, _pack_small(wt), _pack_small(mom), _pack_small(var), "small_adamw")
    after = packed[0]
    for tag, names, parts, lands, pair in swaps:
        sibs = sibling_copies_wait(parts, lands, pair, after, f"sibling_wait_{tag}")
        for k, p, s in zip(names, parts, sibs):
            res[k] = [t[None] for t in adamw_pair(p, s, wt[k][0], mom[k][0], var[k][0], f"{k}_adamw")]
        after = res[names[-1]][0]
    for idx, p in enumerate(packed):
        for k, t in _unpack_small(p).items():
            res.setdefault(k, [None] * 4)[idx] = t

    return (loss, dx[None], *[res[k][0] for k in WEIGHTS], *[res[k][1] for k in WEIGHTS],
            *[res[k][2] for k in WEIGHTS], *[res[k][3] for k in WEIGHTS])
```

```python
import functools

import jax
import jax.numpy as jnp
from jax import lax
from jax.experimental import pallas as pl
from jax.experimental.pallas import tpu as pltpu

F32 = jnp.float32
BF16 = jnp.bfloat16

D_MODEL = 1024
D_FF = 2816
HEAD = 128
N_CHIPS = 4
N_DEV = 8
EPS = 1e-6
NEG_INF = -1e30
ROPE_THETA = 10000.0
ATT_SCALE = HEAD ** -0.5

ADAM_LR = 0.001
ADAM_B1 = 0.9
ADAM_B2 = 0.999
ADAM_EPS = 1e-08
ADAM_WD = 0.01
ADAM_STEP = 10

VMEM_LIMIT = 52 * 2 ** 20
VMEM_LIMIT_LARGE = 60 * 2 ** 20
MESH = pl.DeviceIdType.MESH

QKV_W = 3840
DIL = ((128, 1), (512, 4), (2048, 16))
B_BASE, MQ, A_BASE = 0, 8, 12
_AQ, _AK, _AV, _BQ, _BK, _BV, _MQ = 0, 6, 12, 18, 22, 24, 26
HEAD_ORDER = tuple(
    [h for j in range(2) for h in (_BQ + 2 * j, _BQ + 2 * j + 1, _BK + j, _BV + j)]
    + [_MQ + i for i in range(4)]
    + [h for g in range(3) for i in range(2) for h in (_AQ + 2 * g + i, _AK + 2 * g + i, _AV + 2 * g + i)])
ROTARY_HEADS = tuple(p for p, h in enumerate(HEAD_ORDER) if h < _AV or _BQ <= h < _BV)


def to_kernel_heads(w):
    return jnp.concatenate([w[..., h * HEAD:(h + 1) * HEAD] for h in HEAD_ORDER], axis=-1)


def from_kernel_heads(w):
    place = {h: p for p, h in enumerate(HEAD_ORDER)}
    return jnp.concatenate([w[..., place[h] * HEAD:(place[h] + 1) * HEAD] for h in range(len(HEAD_ORDER))], axis=-1)

TM = 512
FF_T = D_FF // 2


def _params(*sem):
    return pltpu.CompilerParams(dimension_semantics=sem, vmem_limit_bytes=VMEM_LIMIT)


def _dot(a, b):
    return jnp.dot(a, b, preferred_element_type=F32)


def _dot_nt(a, b):
    return lax.dot_general(a, b, (((1,), (1,)), ((), ())), preferred_element_type=F32)


def _dot_tn(a, b):
    return lax.dot_general(a, b, (((0,), (0,)), ((), ())), preferred_element_type=F32)


def _rstd(x):
    return lax.rsqrt(jnp.mean(x * x, axis=-1, keepdims=True) + EPS)


def _sigmoid(x):
    return 0.5 * jnp.tanh(0.5 * x) + 0.5


def _ffn_perm(k):
    return (k % 2) * 2 + k // 2


UNREAD = pl.BlockSpec(memory_space=pl.ANY)


def _resident(arr):
    return pl.BlockSpec(arr.shape, lambda *_: (0,) * arr.ndim, pipeline_mode=pl.Buffered(1))


def rms_scale(x, g, name, after):
    T, D = x.shape
    tm = 1024

    def body(x_ref, g_ref, _, o_ref):
        v = x_ref[...]
        o_ref[...] = (v * _rstd(v) * g_ref[...]).astype(BF16)

    spec = pl.BlockSpec((tm, D), lambda i: (i, 0))
    return pl.pallas_call(
        body, name=name, grid=(T // tm,), in_specs=[spec, _resident(g), UNREAD], out_specs=spec,
        out_shape=jax.ShapeDtypeStruct((T, D), BF16), compiler_params=_params("parallel"),
    )(x, g, after)


def ffn_in(h, g, w, name, xn=None):
    T, D = h.shape
    normed = xn is not None

    def body(h_ref, g_ref, w_ref, *outs):
        if normed:
            xn, (gu_ref, a_ref) = h_ref[...], outs
        else:
            xn_ref, gu_ref, a_ref = outs
            x = h_ref[...]
            xn = (x * _rstd(x) * g_ref[...]).astype(BF16)
            xn_ref[...] = xn
        for j in range(2):
            gu = _dot(xn, w_ref[:, j * 2 * FF_T:(j + 1) * 2 * FF_T])
            gu_ref[:, j * 2 * FF_T:(j + 1) * 2 * FF_T] = gu.astype(BF16)
            gate, up = gu[:, :FF_T], gu[:, FF_T:]
            a_ref[:, j * FF_T:(j + 1) * FF_T] = (gate * _sigmoid(gate) * up).astype(BF16)

    def rows(width):
        return pl.BlockSpec((TM, width), lambda i: (i, 0))

    res = pl.pallas_call(
        body, name=name,
        grid=(T // TM,),
        in_specs=[rows(D), _resident(g), _resident(w)],
        out_specs=[rows(D)] * (not normed) + [rows(2 * D_FF), rows(D_FF)],
        out_shape=[jax.ShapeDtypeStruct((T, D), BF16)] * (not normed)
                  + [jax.ShapeDtypeStruct((T, 2 * D_FF), BF16), jax.ShapeDtypeStruct((T, D_FF), BF16)],
        compiler_params=_params("parallel"),
    )(xn if normed else h, g, w)
    return (xn, *res) if normed else tuple(res)


def mm_norm_res(a, w, h_in, g, coef, name, target=None):
    T, K = a.shape
    D = w.shape[1]
    final = target is not None
    tm = min(2 * TM, T)

    def body(*refs):
        if final:
            a_ref, w_ref, h_ref, g_ref, t_ref, f_ref, o_ref, l_ref = refs
        else:
            a_ref, w_ref, h_ref, g_ref, f_ref, o_ref = refs
        f = _dot(a_ref[...], w_ref[...])
        f_ref[...] = f
        y = h_ref[...] + coef * (f * _rstd(f) * g_ref[...])
        if final:
            err = y - t_ref[...]
            o_ref[...] = err * (1.0 / D)

            @pl.when(pl.program_id(0) == 0)
            def _():
                l_ref[...] = jnp.zeros_like(l_ref)

            sq = jnp.sum((err * err).reshape(tm // 8, 8, D), axis=0)
            l_ref[...] += functools.reduce(jnp.add, [sq[:, c:c + HEAD] for c in range(0, D, HEAD)])
        else:
            o_ref[...] = y

    row = pl.BlockSpec((tm, D), lambda i: (i, 0))
    in_specs = [pl.BlockSpec((tm, K), lambda i: (i, 0)),
                _resident(w),
                row, pl.BlockSpec((1, D), lambda i: (0, 0))]
    out_specs = [row, row]
    out_shape = [jax.ShapeDtypeStruct((T, D), F32), jax.ShapeDtypeStruct((T, D), F32)]
    args = [a, w, h_in, g]
    if final:
        in_specs.append(row)
        args.append(target)
        out_specs.append(pl.BlockSpec((8, 128), lambda i: (0, 0)))
        out_shape.append(jax.ShapeDtypeStruct((8, 128), F32))
    return pl.pallas_call(
        body, name=name, grid=(T // tm,), in_specs=in_specs, out_specs=out_specs, out_shape=out_shape,
        compiler_params=pltpu.CompilerParams(dimension_semantics=("arbitrary",), vmem_limit_bytes=VMEM_LIMIT_LARGE),
    )(*args)


def _rope(x, cos, sin_signed):
    return x * cos + pltpu.roll(x, HEAD // 2, axis=1) * sin_signed


def _unrope(x, cos, sin_signed):
    return x * cos - pltpu.roll(x, HEAD // 2, axis=1) * sin_signed


def mix_in(h, g, w, w_gate, b_gate, cos, sin_signed, name):
    T, D = h.shape
    tn = 768

    def body(h_ref, g_ref, w_ref, wg_ref, b_ref, c_ref, s_ref, u_ref, o_ref, gt_ref):
        x = h_ref[...]
        u = (x * _rstd(x) * g_ref[...]).astype(BF16)
        u_ref[...] = u
        c, s = c_ref[...], s_ref[...]
        for j in range(QKV_W // tn):
            acc = _dot(u, w_ref[:, j * tn:(j + 1) * tn])
            for hd in range(tn // HEAD):
                head = j * (tn // HEAD) + hd
                part = acc[:, hd * HEAD:(hd + 1) * HEAD]
                if head in ROTARY_HEADS:
                    part = _rope(part, c, s)
                o_ref[:, head * HEAD:(head + 1) * HEAD] = part.astype(BF16)
        for j in range(w_gate.shape[1] // tn):
            cols = slice(j * tn, (j + 1) * tn)
            gt_ref[:, cols] = _sigmoid(_dot(u, wg_ref[:, cols]) + b_ref[:, cols]).astype(BF16)

    def rows(width):
        return pl.BlockSpec((TM, width), lambda i: (i, 0))

    return pl.pallas_call(
        body, name=name,
        grid=(T // TM,),
        in_specs=[rows(D), _resident(g), _resident(w), _resident(w_gate), _resident(b_gate), rows(HEAD), rows(HEAD)],
        out_specs=[rows(D), rows(QKV_W), rows(w_gate.shape[1])],
        out_shape=[jax.ShapeDtypeStruct((T, D), BF16), jax.ShapeDtypeStruct((T, QKV_W), BF16),
                   jax.ShapeDtypeStruct((T, w_gate.shape[1]), BF16)],
        compiler_params=_params("parallel"),
    )(h, g, w, w_gate, b_gate, cos, sin_signed)


def gate_merge_out(gt, o_a, o_b, o_m, w_a, w_b, w_m, w_out, h_in, g, name):
    T = gt.shape[0]
    D = D_MODEL

    def body(gt_ref, oa_ref, ob_ref, om_ref, wa_ref, wb_ref, wm_ref, wo_ref, h_ref, g_ref, m_ref, f_ref, o_ref):
        acc = gt_ref[:, :D].astype(F32) * _dot(oa_ref[...], wa_ref[...])
        acc += gt_ref[:, D:2 * D].astype(F32) * _dot(ob_ref[...], wb_ref[...])
        acc += gt_ref[:, 2 * D:].astype(F32) * _dot(om_ref[...], wm_ref[...])
        merged = acc.astype(BF16)
        m_ref[...] = merged
        f = _dot(merged, wo_ref[...])
        f_ref[...] = f
        o_ref[...] = h_ref[...] + f * _rstd(f) * g_ref[...]

    tm = min(2 * TM, T)

    def rows(width):
        return pl.BlockSpec((tm, width), lambda i: (i, 0))

    return pl.pallas_call(
        body, name=name, grid=(T // tm,),
        in_specs=[rows(3 * D), rows(o_a.shape[1]), rows(o_b.shape[1]), rows(o_m.shape[1]),
                  _resident(w_a), _resident(w_b), _resident(w_m), _resident(w_out), rows(D), _resident(g)],
        out_specs=[rows(D), rows(D), rows(D)],
        out_shape=[jax.ShapeDtypeStruct((T, D), BF16), jax.ShapeDtypeStruct((T, D), F32),
                   jax.ShapeDtypeStruct((T, D), F32)],
        compiler_params=pltpu.CompilerParams(dimension_semantics=("parallel",), vmem_limit_bytes=VMEM_LIMIT_LARGE),
    )(gt, o_a, o_b, o_m, w_a, w_b, w_m, w_out, h_in, g)


def _band_rows(start, r):
    return pl.ds(start, HEAD) if r == 1 else pl.ds(start, HEAD, stride=r)


def _band_mask(max_dist, first_has_prev):
    row = lax.broadcasted_iota(jnp.int32, (HEAD, 2 * HEAD), 0)
    col = lax.broadcasted_iota(jnp.int32, (HEAD, 2 * HEAD), 1)
    dist = row + HEAD - col
    band = (dist >= 0) & (dist <= max_dist)
    return band, band & (col >= jnp.where(first_has_prev, 0, HEAD))


def _stack(parts):
    return parts[0] if len(parts) == 1 else jnp.concatenate(parts, axis=0)


def _band_specs(BT, SB, nsub, base, grp):
    stride = grp + 2

    def cur(off, width):
        return pl.BlockSpec((BT, width * HEAD), lambda h, i: (i, (base + h * stride + off) // width))

    def prev(off):
        return pl.BlockSpec((SB, HEAD), lambda h, i: (jnp.maximum(i * nsub - 1, 0), base + h * stride + off))

    return cur(0, grp), cur(grp, 1), prev(grp), cur(grp + 1, 1), prev(grp + 1)


def band_fwd(qkv, sinks, *, r, base, hkv, grp, max_dist, out_dtype, name, merge=None):
    T, W = qkv.shape
    SB = HEAD * r
    BT = min(2048, T)
    nsub, nib = BT // SB, T // BT
    hq = hkv * grp
    heads = [slice(g * HEAD, (g + 1) * HEAD) for g in range(grp)]
    others = [] if merge is None else [*merge[0], *merge[1]]

    def body(sink_ref, q_ref, kc_ref, kp_ref, vc_ref, vp_ref, *rest):
        joint_o, joint_l = rest[len(others):len(others) + 2]
        qf, kf, vf = rest[len(others) + 2:len(others) + 5]
        o_ref, l_ref = rest[len(others) + 5:] if others else (joint_o, joint_l)
        kvh, ib = pl.program_id(0), pl.program_id(1)
        qf[...] = q_ref[...].astype(F32)
        kf[:SB] = kp_ref[...].astype(F32)
        kf[SB:] = kc_ref[...].astype(F32)
        vf[:SB] = vp_ref[...].astype(F32)
        vf[SB:] = vc_ref[...].astype(F32)
        band, band_first = _band_mask(max_dist, ib > 0)
        for c in range(r):
            k_old, v_old = kf[_band_rows(c, r)], vf[_band_rows(c, r)]
            for j in range(nsub):
                mask = band_first if j == 0 else band
                rows = _band_rows(j * SB + c, r)
                k_own, v_own = kf[_band_rows((j + 1) * SB + c, r)], vf[_band_rows((j + 1) * SB + c, r)]
                kcat = jnp.concatenate([k_old, k_own], axis=0).astype(BF16)
                vcat = jnp.concatenate([v_old, v_own], axis=0).astype(BF16)
                k_old, v_old = k_own, v_own
                s_all = _dot_nt(_stack([qf[rows, cols] for cols in heads]).astype(BF16), kcat) * ATT_SCALE
                probs, tots = [], []
                for g, cols in enumerate(heads):
                    s = jnp.where(mask, s_all[cols], NEG_INF)
                    sk = sink_ref[kvh * grp + g]
                    m = jnp.maximum(jnp.max(s, axis=-1, keepdims=True), sk)
                    p = jnp.exp(s - m)
                    tot = jnp.sum(p, axis=-1, keepdims=True) + jnp.exp(sk - m)
                    probs.append(p.astype(BF16))
                    tots.append(tot)
                    l_ref[rows, cols] = jnp.broadcast_to(m + jnp.log(tot), (HEAD, HEAD))
                o_all = _dot(_stack(probs), vcat)
                for g, cols in enumerate(heads):
                    o_ref[rows, cols] = (o_all[cols] / tots[g]).astype(o_ref.dtype)

        if others:
            half = len(others) // 2
            outs = [ref[...] for ref in rest[:half]] + [o_ref[...]]
            logs = [ref[...] for ref in rest[half:len(others)]] + [l_ref[...]]
            top = functools.reduce(jnp.maximum, logs)
            weights = [jnp.exp(lg - top) for lg in logs]
            total = functools.reduce(jnp.add, weights)
            mixed = functools.reduce(jnp.add, [wgt * out for wgt, out in zip(weights, outs)])
            joint_o[...] = (mixed / total).astype(out_dtype)
            joint_l[...] = top + jnp.log(total)

    out_spec = pl.BlockSpec((BT, grp * HEAD), lambda h, i: (i, h))
    own = [pltpu.VMEM((BT, grp * HEAD), F32)] * 2 if others else []
    return pl.pallas_call(
        body, name=name, grid=(hkv, nib),
        in_specs=[pl.BlockSpec(memory_space=pltpu.SMEM), *_band_specs(BT, SB, nsub, base, grp)]
                 + [out_spec] * len(others),
        out_specs=[out_spec, out_spec],
        out_shape=[jax.ShapeDtypeStruct((T, hq * HEAD), out_dtype), jax.ShapeDtypeStruct((T, hq * HEAD), F32)],
        scratch_shapes=[pltpu.VMEM((BT, grp * HEAD), F32), pltpu.VMEM((SB + BT, HEAD), F32),
                        pltpu.VMEM((SB + BT, HEAD), F32)] + own,
        compiler_params=_params("parallel", "arbitrary"),
    )(sinks, qkv, qkv, qkv, qkv, qkv, *others)


def band_bwd(qkv, dqkv, do, o, lse, cos, sin_signed, sinks, *, r, base, hkv, grp, max_dist, name):
    T, W = qkv.shape
    SB = HEAD * r
    BT = min(max(2048, 2 * SB), T)
    nsub, nib = BT // SB, T // BT
    nblk = T // SB
    with_sink = sinks is not None
    heads = [slice(g * HEAD, (g + 1) * HEAD) for g in range(grp)]

    def body(*refs):
        if with_sink:
            sink_ref, refs = refs[0], refs[1:]
        (q_ref, kc_ref, kp_ref, vc_ref, vp_ref, qn_ref, do_ref, don_ref, o_ref, on_ref, l_ref, ln_ref,
         c_ref, s_ref, _) = refs[:15]
        out_ref = refs[15]
        ds_ref = refs[16] if with_sink else None
        qf, dof, of, kf, vf, dqf, dkf, dvf = refs[-8:]
        kvh, ib = pl.program_id(0), pl.program_id(1)
        for buf, cur_ref, nxt_ref in ((qf, q_ref, qn_ref), (dof, do_ref, don_ref), (of, o_ref, on_ref)):
            buf[:BT] = cur_ref[...].astype(F32)
            buf[BT:] = nxt_ref[...].astype(F32)
        kf[:SB] = kp_ref[...].astype(F32)
        kf[SB:] = kc_ref[...].astype(F32)
        vf[:SB] = vp_ref[...].astype(F32)
        vf[SB:] = vc_ref[...].astype(F32)
        band, band_first = _band_mask(max_dist, ib > 0)
        if with_sink:
            @pl.when(ib == 0)
            def _():
                ds_ref[...] = jnp.zeros_like(ds_ref)

        def grads(rows, logzs, keys, vals, mask):
            q = _stack([qf[rows, cols] for cols in heads]).astype(BF16)
            dout = _stack([dof[rows, cols] for cols in heads]).astype(BF16)
            s_all = _dot_nt(q, keys) * ATT_SCALE
            dp_all = _dot_nt(dout, vals)
            probs, dss, deltas = [], [], []
            for g, cols in enumerate(heads):
                delta = jnp.sum(dof[rows, cols] * of[rows, cols], axis=-1, keepdims=True)
                p = jnp.exp(jnp.where(mask, s_all[cols], NEG_INF) - logzs[g][:, :1])
                probs.append(p.astype(BF16))
                dss.append((p * (dp_all[cols] - delta) * ATT_SCALE).astype(BF16))
                deltas.append(delta)
            return q, dout, _stack(probs), _stack(dss), deltas

        row = lax.broadcasted_iota(jnp.int32, (HEAD, HEAD), 0)
        col = lax.broadcasted_iota(jnp.int32, (HEAD, HEAD), 1)
        reach = col >= row + jnp.where(ib < nib - 1, HEAD - max_dist, 2 * HEAD)
        for c in range(r):
            k_old, v_old = kf[_band_rows(c, r)], vf[_band_rows(c, r)]
            dk_own = dv_own = None
            for j in range(nsub):
                rows = _band_rows(j * SB + c, r)
                k_own, v_own = kf[_band_rows((j + 1) * SB + c, r)], vf[_band_rows((j + 1) * SB + c, r)]
                kcat = jnp.concatenate([k_old, k_own], axis=0).astype(BF16)
                vcat = jnp.concatenate([v_old, v_own], axis=0).astype(BF16)
                logzs = [l_ref[rows, cols] for cols in heads]
                q, dout, p, ds, deltas = grads(rows, logzs, kcat, vcat, band_first if j == 0 else band)
                dq = _dot(ds, kcat)
                for g, cols in enumerate(heads):
                    dqf[rows, cols] = dq[cols]
                    if with_sink:
                        p_sink = jnp.exp(sink_ref[kvh * grp + g] - logzs[g][:, :1])
                        ds_ref[g * 8:(g + 1) * 8] += jnp.sum(p_sink * deltas[g])
                dk, dv = _dot_tn(ds, q), _dot_tn(p, dout)
                if j > 0:
                    done = _band_rows((j - 1) * SB + c, r)
                    dkf[done] = dk_own + dk[:HEAD]
                    dvf[done] = dv_own + dv[:HEAD]
                dk_own, dv_own = dk[HEAD:], dv[HEAD:]
                k_old, v_old = k_own, v_own
            logzs = [ln_ref[_band_rows(c, r), cols] for cols in heads]
            q, dout, p, ds, _ = grads(_band_rows(BT + c, r), logzs, k_old.astype(BF16), v_old.astype(BF16), reach)
            done = _band_rows((nsub - 1) * SB + c, r)
            dkf[done] = dk_own + _dot_tn(ds, q)
            dvf[done] = dv_own + _dot_tn(p, dout)

        cs, sn = c_ref[...], s_ref[...]
        for cols in heads:
            out_ref[:, cols] = _unrope(dqf[:, cols], cs, sn).astype(BF16)
        out_ref[:, grp * HEAD:(grp + 1) * HEAD] = _unrope(dkf[...], cs, sn).astype(BF16)
        out_ref[:, (grp + 1) * HEAD:] = dvf[...].astype(BF16)

    def nxt_row(i):
        return jnp.minimum((i + 1) * nsub, nblk - 1)

    stride = grp + 2
    q_next = pl.BlockSpec((SB, grp * HEAD), lambda h, i: (nxt_row(i), (base + h * stride) // grp))
    head_cur = pl.BlockSpec((BT, grp * HEAD), lambda h, i: (i, h))
    head_next = pl.BlockSpec((SB, grp * HEAD), lambda h, i: (nxt_row(i), h))
    table = pl.BlockSpec((BT, HEAD), lambda h, i: (i, 0))

    in_specs = [*_band_specs(BT, SB, nsub, base, grp), q_next,
                head_cur, head_next, head_cur, head_next, head_cur, head_next, table, table, UNREAD]
    args = [qkv, qkv, qkv, qkv, qkv, qkv, do, do, o, o, lse, lse, cos, sin_signed, dqkv]
    out_specs = [pl.BlockSpec((BT, stride * HEAD), lambda h, i: (i, base // stride + h))]
    out_shape = [jax.ShapeDtypeStruct(dqkv.shape, dqkv.dtype)]
    if with_sink:
        in_specs.insert(0, pl.BlockSpec(memory_space=pltpu.SMEM))
        args.insert(0, sinks)
        out_specs.append(pl.BlockSpec((None, grp * 8, HEAD), lambda h, i: (h, 0, 0)))
        out_shape.append(jax.ShapeDtypeStruct((hkv, grp * 8, HEAD), F32))
    wide = pltpu.VMEM((BT + SB, grp * HEAD), F32)
    tall = pltpu.VMEM((SB + BT, HEAD), F32)
    grad = pltpu.VMEM((BT, HEAD), F32)
    return pl.pallas_call(
        body, name=name, grid=(hkv, nib), in_specs=in_specs, out_specs=out_specs, out_shape=out_shape,
        input_output_aliases={len(args) - 1: 0},
        scratch_shapes=[wide, wide, wide, tall, tall, pltpu.VMEM((BT, grp * HEAD), F32), grad, grad],
        compiler_params=pltpu.CompilerParams(dimension_semantics=("parallel", "arbitrary"),
                                             vmem_limit_bytes=VMEM_LIMIT_LARGE),
    )(*args)


M_HEADS = 4


def mem_kv(mem, g, w, name):
    n, D = mem.shape

    def body(m_ref, g_ref, w_ref, mn_ref, kv_ref):
        x = m_ref[...]
        mn = (x * _rstd(x) * g_ref[...]).astype(BF16)
        mn_ref[...] = mn
        kv_ref[...] = _dot(mn, w_ref[...]).astype(BF16)

    return pl.pallas_call(
        body, name=name,
        out_shape=[jax.ShapeDtypeStruct((n, D), BF16), jax.ShapeDtypeStruct((n, w.shape[1]), BF16)],
        compiler_params=pltpu.CompilerParams(vmem_limit_bytes=VMEM_LIMIT),
    )(mem, g, w)


def mem_fwd(qkv, mkv, name):
    T = qkv.shape[0]
    n = mkv.shape[0]
    RB = 1024

    def body(q_ref, kv_ref, o_ref, l_ref):
        for h in range(M_HEADS):
            cols = slice(h * HEAD, (h + 1) * HEAD)
            s = _dot_nt(q_ref[:, cols], kv_ref[:, cols]) * ATT_SCALE
            m = jnp.max(s, axis=-1, keepdims=True)
            p = jnp.exp(s - m)
            den = jnp.sum(p, axis=-1, keepdims=True)
            vals = kv_ref[:, (M_HEADS + h) * HEAD:(M_HEADS + h + 1) * HEAD]
            o_ref[:, cols] = (_dot(p.astype(BF16), vals) / den).astype(BF16)
            l_ref[:, cols] = jnp.broadcast_to(m + jnp.log(den), (RB, HEAD))

    out = pl.BlockSpec((RB, M_HEADS * HEAD), lambda i: (i, 0))
    return pl.pallas_call(
        body, name=name, grid=(T // RB,),
        in_specs=[pl.BlockSpec((RB, M_HEADS * HEAD), lambda i: (i, MQ // M_HEADS)), _resident(mkv)],
        out_specs=[out, out],
        out_shape=[jax.ShapeDtypeStruct((T, M_HEADS * HEAD), BF16), jax.ShapeDtypeStruct((T, M_HEADS * HEAD), F32)],
        compiler_params=_params("parallel"),
    )(qkv, mkv)


def mem_bwd(qkv, dqkv, mkv, do, o, lse, name):
    T = qkv.shape[0]
    n = mkv.shape[0]
    RB = 1024

    def body(q_ref, kv_ref, do_ref, o_ref, l_ref, _, dq_ref, dk_ref, dv_ref):
        @pl.when(pl.program_id(0) == 0)
        def _():
            dk_ref[...] = jnp.zeros_like(dk_ref)
            dv_ref[...] = jnp.zeros_like(dv_ref)

        for h in range(M_HEADS):
            cols = slice(h * HEAD, (h + 1) * HEAD)
            keys, vals = kv_ref[:, cols], kv_ref[:, (M_HEADS + h) * HEAD:(M_HEADS + h + 1) * HEAD]
            q, dout = q_ref[:, cols], do_ref[:, cols]
            delta = jnp.sum(dout.astype(F32) * o_ref[:, cols].astype(F32), axis=-1, keepdims=True)
            p = jnp.exp(_dot_nt(q, keys) * ATT_SCALE - l_ref[:, cols][:, :1])
            ds = (p * (_dot_nt(dout, vals) - delta) * ATT_SCALE).astype(BF16)
            dq_ref[:, cols] = _dot(ds, keys).astype(BF16)
            dk_ref[:, cols] += _dot_tn(ds, q)
            dv_ref[:, cols] += _dot_tn(p.astype(BF16), dout)

    wide = M_HEADS * HEAD
    tok = pl.BlockSpec((RB, wide), lambda i: (i, 0))
    q_cols = pl.BlockSpec((RB, wide), lambda i: (i, MQ // M_HEADS))
    slot = pl.BlockSpec((n, wide), lambda i: (0, 0))
    return pl.pallas_call(
        body, name=name, grid=(T // RB,),
        in_specs=[q_cols, _resident(mkv), tok, tok, tok, UNREAD],
        out_specs=[q_cols, slot, slot],
        out_shape=[jax.ShapeDtypeStruct(dqkv.shape, dqkv.dtype),
                   jax.ShapeDtypeStruct((n, wide), F32), jax.ShapeDtypeStruct((n, wide), F32)],
        input_output_aliases={5: 0},
        compiler_params=_params("arbitrary"),
    )(qkv, mkv, do, o, lse, dqkv)


def mem_kv_bwd(mem, g, mem_n, w, dmkv, name):
    n, D = mem.shape

    def body(m_ref, g_ref, mn_ref, w_ref, d_ref, dw_ref, dg_ref):
        d = d_ref[...].astype(BF16)
        dw_ref[...] = _dot_tn(mn_ref[...], d)
        x = m_ref[...]
        dg_ref[...] = jnp.sum(_dot_nt(d, w_ref[...]) * (x * _rstd(x)), axis=0, keepdims=True)

    return pl.pallas_call(
        body, name=name,
        out_shape=[jax.ShapeDtypeStruct(w.shape, F32), jax.ShapeDtypeStruct((1, D), F32)],
        compiler_params=pltpu.CompilerParams(vmem_limit_bytes=VMEM_LIMIT),
    )(mem, g, mem_n, w, dmkv)


def _rms_bwd(dn, f, g):
    r = _rstd(f)
    fhat = f * r
    dfhat = dn * g
    df = r * (dfhat - fhat * jnp.mean(dfhat * fhat, axis=-1, keepdims=True))
    return df, jnp.sum(dn * fhat, axis=0, keepdims=True)


def ffn_tokens_bwd(dh, f, h_in, gu, g_pre, g_post, w_in, w_out, coef, name, after):
    T, D = dh.shape

    def body(dh_ref, f_ref, h_ref, gu_ref, gpre_ref, gpost_ref, win_ref, wout_ref, _,
             df_ref, dgu_ref, dhin_ref, dgpre_ref, dgpost_ref, dxn_ref):
        i, j = pl.program_id(0), pl.program_id(1)

        @pl.when(j == 0)
        def _():
            @pl.when(i == 0)
            def _():
                dgpre_ref[...] = jnp.zeros_like(dgpre_ref)
                dgpost_ref[...] = jnp.zeros_like(dgpost_ref)

            df, dg_post = _rms_bwd(coef * dh_ref[...], f_ref[...], gpost_ref[...])
            dgpost_ref[...] += dg_post
            df_ref[...] = df.astype(BF16)

        for jj in range(2):
            @pl.when(j == jj)
            def _(jj=jj):
                lo, mid, hi = 2 * jj * FF_T, (2 * jj + 1) * FF_T, (2 * jj + 2) * FF_T
                da = _dot_nt(df_ref[...], wout_ref[jj * FF_T:(jj + 1) * FF_T, :])
                gate = gu_ref[:, :FF_T].astype(F32)
                up = gu_ref[:, FF_T:].astype(F32)
                sig = _sigmoid(gate)
                dgate = (da * up * sig * (1.0 + gate * (1.0 - sig))).astype(BF16)
                dup = (da * gate * sig).astype(BF16)
                dgu_ref[:, :FF_T] = dgate
                dgu_ref[:, FF_T:] = dup
                part = _dot_nt(dgate, win_ref[:, lo:mid]) + _dot_nt(dup, win_ref[:, mid:hi])
                if jj == 0:
                    dxn_ref[...] = part
                else:
                    h = h_ref[...]
                    r = _rstd(h)
                    xhat = h * r
                    dxn = dxn_ref[...] + part
                    dxhat = dxn * gpre_ref[...]
                    dhin_ref[...] = dh_ref[...] + r * (dxhat - xhat * jnp.mean(dxhat * xhat, axis=-1, keepdims=True))
                    dgpre_ref[...] += jnp.sum(dxn * xhat, axis=0, keepdims=True)

    row = pl.BlockSpec((TM, D), lambda i, j: (i, 0))
    wide = pl.BlockSpec((TM, 2 * FF_T), lambda i, j: (i, j))
    vec = pl.BlockSpec((1, D), lambda i, j: (0, 0))
    return pl.pallas_call(
        body, name=name, grid=(T // TM, 2),
        in_specs=[row, row, row, wide, _resident(g_pre), _resident(g_post), _resident(w_in), _resident(w_out),
                  UNREAD],
        out_specs=[row, wide, row, vec, vec],
        out_shape=[jax.ShapeDtypeStruct((T, D), BF16), jax.ShapeDtypeStruct((T, 2 * D_FF), BF16),
                   jax.ShapeDtypeStruct((T, D), F32), jax.ShapeDtypeStruct((1, D), F32),
                   jax.ShapeDtypeStruct((1, D), F32)],
        scratch_shapes=[pltpu.VMEM((TM, D), F32)],
        compiler_params=pltpu.CompilerParams(dimension_semantics=("arbitrary", "arbitrary"),
                                             vmem_limit_bytes=VMEM_LIMIT_LARGE),
    )(dh, f, h_in, gu, g_pre, g_post, w_in, w_out, after)


def mm_nt_norm_bwd(pieces, h_in, dh_out, g, name, after):
    T, D = h_in.shape

    def body(*refs):
        ab = refs[:2 * len(pieces)]
        h_ref, dh_ref, g_ref, _, o_ref, dg_ref = refs[2 * len(pieces):]
        dxn = _dot_nt(ab[0][...], ab[1][...])
        for p in range(1, len(pieces)):
            dxn += _dot_nt(ab[2 * p][...], ab[2 * p + 1][...])
        h = h_ref[...]
        r = _rstd(h)
        xhat = h * r
        dxhat = dxn * g_ref[...]
        o_ref[...] = dh_ref[...] + r * (dxhat - xhat * jnp.mean(dxhat * xhat, axis=-1, keepdims=True))

        @pl.when(pl.program_id(0) == 0)
        def _():
            dg_ref[...] = jnp.zeros_like(dg_ref)

        dg_ref[...] += jnp.sum(dxn * xhat, axis=0, keepdims=True)

    in_specs, args = [], []
    for a, w in pieces:
        in_specs += [pl.BlockSpec((TM, a.shape[1]), lambda i: (i, 0)), _resident(w)]
        args += [a, w]
    row = pl.BlockSpec((TM, D), lambda i: (i, 0))
    return pl.pallas_call(
        body, name=name, grid=(T // TM,),
        in_specs=in_specs + [row, row, _resident(g), UNREAD],
        out_specs=[row, pl.BlockSpec((1, D), lambda i: (0, 0))],
        out_shape=[jax.ShapeDtypeStruct((T, D), F32), jax.ShapeDtypeStruct((1, D), F32)],
        compiler_params=_params("arbitrary"),
    )(*args, h_in, dh_out, g, after)


def gate_merge_out_bwd(dh, f, g, w_out, merged, gt, o_a, o_b, o_m, w_a, w_b, w_m, name, after):
    T = dh.shape[0]
    D = D_MODEL
    branch = ((o_a, w_a), (o_b, w_b), (o_m, w_m))

    def body(dh_ref, f_ref, g_ref, wo_ref, m_ref, gt_ref, oa_ref, ob_ref, om_ref, wa_ref, wb_ref, wm_ref, _,
             dg_ref, dwo_ref, dgt_ref, doa_ref, dob_ref, dom_ref, db_ref, dwa_ref, dwb_ref, dwm_ref):
        @pl.when(pl.program_id(0) == 0)
        def _():
            for acc in (dg_ref, dwo_ref, db_ref, dwa_ref, dwb_ref, dwm_ref):
                acc[...] = jnp.zeros_like(acc)

        df, dg = _rms_bwd(dh_ref[...], f_ref[...], g_ref[...])
        dg_ref[...] += dg
        df = df.astype(BF16)
        dwo_ref[...] += _dot_tn(m_ref[...], df)
        dmf = _dot_nt(df, wo_ref[...])
        for x, (o_ref, w_ref, do_ref, dw_ref) in enumerate(((oa_ref, wa_ref, doa_ref, dwa_ref),
                                                           (ob_ref, wb_ref, dob_ref, dwb_ref),
                                                           (om_ref, wm_ref, dom_ref, dwm_ref))):
            cols = slice(x * D, (x + 1) * D)
            gx = gt_ref[:, cols].astype(F32)
            w = w_ref[...]
            dpre = dmf * _dot(o_ref[...], w) * gx * (1.0 - gx)
            dgt_ref[:, cols] = dpre.astype(BF16)
            db_ref[:, cols] += jnp.sum(dpre, axis=0, keepdims=True)
            dp = (dmf * gx).astype(BF16)
            do_ref[...] = _dot_nt(dp, w).astype(BF16)
            dw_ref[...] += _dot_tn(dp, o_ref[...])

    def rows(width):
        return pl.BlockSpec((TM, width), lambda i: (i, 0))

    def kept(shape):
        return pl.BlockSpec(shape, lambda i: (0,) * len(shape))

    widths = [o.shape[1] for o, _ in branch]
    sums = [(1, D), (D, D), (1, 3 * D)] + [(D, k) for k in widths]
    return pl.pallas_call(
        body, name=name, grid=(T // TM,),
        in_specs=[rows(D), rows(D), _resident(g), _resident(w_out), rows(D), rows(3 * D)]
                 + [rows(k) for k in widths] + [_resident(w) for _, w in branch] + [UNREAD],
        out_specs=[kept(sums[0]), kept(sums[1]), rows(3 * D)] + [rows(k) for k in widths]
                  + [kept(shape) for shape in sums[2:]],
        out_shape=[jax.ShapeDtypeStruct(sums[0], F32), jax.ShapeDtypeStruct(sums[1], F32),
                   jax.ShapeDtypeStruct((T, 3 * D), BF16)] + [jax.ShapeDtypeStruct((T, k), BF16) for k in widths]
                  + [jax.ShapeDtypeStruct(shape, F32) for shape in sums[2:]],
        compiler_params=pltpu.CompilerParams(dimension_semantics=("arbitrary",), vmem_limit_bytes=VMEM_LIMIT_LARGE),
    )(dh, f, g, w_out, merged, gt, o_a, o_b, o_m, w_a, w_b, w_m, after)


def mm_tn(x, dy, tm, tn, name, shard_major=False, perm=None, slabs=1, after=None, wire=False):
    T, M = x.shape
    N = dy.shape[1]
    tk = min(2048, T)
    perm = perm or (lambda j: j)
    w = tn // slabs

    def body(x_ref, dy_ref, *rest):
        o_ref = rest[-2] if wire else rest[-1]

        @pl.when(pl.program_id(2) == 0)
        def _():
            o_ref[...] = jnp.zeros_like(o_ref)

        acc = _dot_tn(x_ref[...], dy_ref[...])
        if shard_major:
            for s in range(slabs):
                o_ref[s] += acc[:, s * w:(s + 1) * w]
        else:
            o_ref[...] += acc
        if wire:
            @pl.when(pl.program_id(2) == T // tk - 1)
            def _():
                rest[-1][...] = o_ref[...].astype(BF16)

    if shard_major:
        out_spec = pl.BlockSpec((slabs, tm, w), lambda i, j, k: (perm(j), i, 0))
        out_shape = jax.ShapeDtypeStruct((N // w, M, w), F32)
    else:
        out_spec = pl.BlockSpec((tm, tn), lambda i, j, k: (i, j))
        out_shape = jax.ShapeDtypeStruct((M, N), F32)
    return pl.pallas_call(
        body, name=name, grid=(M // tm, N // tn, T // tk),
        in_specs=[pl.BlockSpec((tk, tm), lambda i, j, k: (k, i)),
                  pl.BlockSpec((tk, tn), lambda i, j, k: (k, j))] + ([] if after is None else [UNREAD]),
        out_specs=[out_spec, out_spec] if wire else out_spec,
        out_shape=[out_shape, jax.ShapeDtypeStruct(out_shape.shape, BF16)] if wire else out_shape,
        compiler_params=_params("parallel", "parallel", "arbitrary"),
    )(x, dy, *([] if after is None else [after]))


def rope_tables(T, zero):
    half = HEAD // 2
    inv = ROPE_THETA ** (-jnp.arange(half, dtype=F32) / half)
    ang = (jnp.arange(T).astype(F32) + zero)[:, None] * inv[None, :]
    cos, sin = jnp.cos(ang), jnp.sin(ang)
    return jnp.concatenate([cos, cos], axis=1), jnp.concatenate([-sin, sin], axis=1)


def layer_step(x, mem, target, gains, sinks, b_gate, weights_of, send_grads, zero):
    T = x.shape[0]
    cos, sin_signed = rope_tables(T, zero)
    no_sink = jnp.full((2,), NEG_INF, F32)

    xn1 = rms_scale(x, gains["ffn1_norm_pre"], "ffn1_norm", cos)
    w = dict(weights_of("ffn1_in", xn1))
    _, gu1, a1 = ffn_in(x, gains["ffn1_norm_pre"], w["ffn1_w_in"], "ffn1_in", xn=xn1)
    w.update(weights_of("ffn1_out", a1))
    f1, h1 = mm_norm_res(a1, w["ffn1_w_out"], x, gains["ffn1_norm_post"], 0.5, "ffn1_out")
    w.update(weights_of("mix_in", f1))
    u, qkv, gt = mix_in(h1, gains["mix_norm_pre"], w["w_in"], w["w_gate"], b_gate, cos, sin_signed, "mix_in")
    w.update(weights_of("mix_rest", u))
    outs, lses = [], []
    for gidx, (window, dil) in enumerate(DIL):
        last = gidx == len(DIL) - 1
        o_g, l_g = band_fwd(qkv, no_sink, r=dil, base=A_BASE + 6 * gidx, hkv=2, grp=1, max_dist=window // dil,
                            out_dtype=BF16 if last else F32, name=f"attn_a{gidx}_fwd",
                            merge=(outs, lses) if last else None)
        outs.append(o_g)
        lses.append(l_g)
    o_a, l_a = outs[-1], lses[-1]
    o_b, l_b = band_fwd(qkv, sinks, r=1, base=B_BASE, hkv=2, grp=2, max_dist=HEAD - 1, out_dtype=BF16,
                        name="attn_b_fwd")
    mem_n, mkv = mem_kv(mem, gains["mem_norm"], w["w_mem_kv"], "mem_kv")
    o_m, l_m = mem_fwd(qkv, mkv, "attn_m_fwd")
    merged, mo, h2 = gate_merge_out(gt, o_a, o_b, o_m, w["w_o_a"], w["w_o_b"], w["w_o_m"], w["w_out"], h1,
                                    gains["mix_norm_post"], "gate_merge_out")
    w.update(weights_of("ffn2", mo))
    xn2, gu2, a2 = ffn_in(h2, gains["ffn2_norm_pre"], w["ffn2_w_in"], "ffn2_in")
    f2, dy, sq = mm_norm_res(a2, w["ffn2_w_out"], h2, gains["ffn2_norm_post"], 0.5, "ffn2_out", target=target)

    grads = {}

    def ffn_bwd(tag, dh_out, f, gu, a, xn, h_in, after):
        df, dgu, dh_in, grads[f"{tag}_norm_pre"], grads[f"{tag}_norm_post"] = ffn_tokens_bwd(
            dh_out, f, h_in, gu, gains[f"{tag}_norm_pre"], gains[f"{tag}_norm_post"], w[f"{tag}_w_in"],
            w[f"{tag}_w_out"], 0.5, f"{tag}_tokens_bwd", after)
        sent = send_grads(f"{tag}_in", {f"{tag}_w_in": mm_tn(
            xn, dgu, D_MODEL, FF_T, f"{tag}_w_in_grad", shard_major=True, perm=_ffn_perm, wire=True)})
        sent = send_grads(f"{tag}_out", {f"{tag}_w_out": mm_tn(
            a, df, FF_T, D_MODEL, f"{tag}_w_out_grad", after=sent, wire=True)})
        return dh_in, sent

    dh2, sent = ffn_bwd("ffn2", dy, f2, gu2, a2, xn2, h2, dy)

    mix = {}
    (grads["mix_norm_post"], mix["w_out"], dgt, do_a, do_b, do_m, grads["b_gate"],
     dwa_t, dwb_t, dwm_t) = gate_merge_out_bwd(
        dh2, mo, gains["mix_norm_post"], w["w_out"], merged, gt, o_a, o_b, o_m, w["w_o_a"], w["w_o_b"],
        w["w_o_m"], "gate_merge_out_bwd", sent)
    mix["w_o_a"], mix["w_o_b"], mix["w_o_m"] = dwa_t.T, dwb_t.T, dwm_t.T

    dqkv = lax.empty(qkv.shape, qkv.dtype)
    for gidx, (window, dil) in enumerate(DIL):
        dqkv, = band_bwd(qkv, dqkv, do_a, o_a, l_a, cos, sin_signed, None, r=dil, base=A_BASE + 6 * gidx, hkv=2,
                         grp=1, max_dist=window // dil, name=f"attn_a{gidx}_bwd")
    dqkv, dsink = band_bwd(qkv, dqkv, do_b, o_b, l_b, cos, sin_signed, sinks, r=1, base=B_BASE, hkv=2, grp=2,
                           max_dist=HEAD - 1, name="attn_b_bwd")
    grads["sinks"] = -dsink[:, ::8, 0].reshape(1, 4)
    dqkv, dmk, dmv = mem_bwd(qkv, dqkv, mkv, do_m, o_m, l_m, "attn_m_bwd")
    mix["w_mem_kv"], grads["mem_norm"] = mem_kv_bwd(
        mem, gains["mem_norm"], mem_n, w["w_mem_kv"], jnp.concatenate([dmk, dmv], axis=1), "mem_kv_bwd")

    mix["w_in"] = mm_tn(u, dqkv, D_MODEL, 1280, "w_in_grad")
    mix["w_gate"] = mm_tn(u, dgt, D_MODEL, 1536, "w_gate_grad", shard_major=True, slabs=2, wire=True)
    sent = send_grads("mix", mix)
    dh1, grads["mix_norm_pre"] = mm_nt_norm_bwd(
        [(dqkv, w["w_in"]), (dgt, w["w_gate"])], h1, dh2, gains["mix_norm_pre"], "mix_in_bwd", sent)

    dx, _ = ffn_bwd("ffn1", dh1, f1, gu1, a1, xn1, x, dh1)
    return sq, dx, grads


def _place():
    return lax.axis_index("x"), lax.axis_index("y"), lax.axis_index("c")


def _other_chips(x, y):
    return [(1 - x, y), (x, 1 - y), (1 - x, 1 - y)]


def _hbm(n):
    return [pl.BlockSpec(memory_space=pltpu.HBM)] * n


SEM = pl.BlockSpec(memory_space=pltpu.SEMAPHORE)
SIDE_EFFECT = pltpu.SideEffectType.DATAFLOW_SIDE_EFFECTING


def _chip_copy(src, land, sems, i, j, dst_slot, scatter):
    x, y, c = _place()
    px, py = _other_chips(x, y)[j]
    send_sems, recv_sems = sems
    return pltpu.make_async_remote_copy(
        src_ref=src[i].at[2 * px + py] if scatter else src[i], dst_ref=land[i].at[dst_slot],
        send_sem=send_sems.at[3 * i + j], recv_sem=recv_sems.at[3 * i + j],
        device_id=(px, py, c), device_id_type=MESH)


def chip_copies_start(srcs, lands, groups, scatter, name, after=None):
    n = len(srcs)

    def body(*refs):
        src, land = refs[:n], refs[n:2 * n]
        first_sem = 2 * n + (after is not None)
        sems = refs[first_sem:first_sem + 2 * len(groups)]
        token = refs[-1]
        x, y, _ = _place()
        for g, members in enumerate(groups):
            part = ([src[i] for i in members], [land[i] for i in members])
            for t in range(len(members)):
                for j in range(3):
                    _chip_copy(*part, sems[2 * g:2 * g + 2], t, j, 2 * x + y, scatter).start()
        token[...] = jnp.zeros_like(token)

    sem_shapes = [pltpu.SemaphoreType.DMA((3 * len(m),)) for m in groups for _ in range(2)]
    thru = [pltpu.HBM(a.shape, a.dtype) for a in (*srcs, *lands)]
    res = pl.pallas_call(
        body, name=name,
        out_shape=(*sem_shapes, *thru, jax.ShapeDtypeStruct((8, 128), F32)),
        in_specs=_hbm(2 * n) + ([] if after is None else [UNREAD]),
        out_specs=(*[SEM] * len(sem_shapes), *_hbm(2 * n), pl.BlockSpec(memory_space=pltpu.VMEM)),
        input_output_aliases={i: len(sem_shapes) + i for i in range(2 * n)},
        compiler_params=pltpu.CompilerParams(has_side_effects=SIDE_EFFECT),
    )(*[pltpu.with_memory_space_constraint(a, pltpu.HBM) for a in (*srcs, *lands)],
      *([] if after is None else [after]))
    k = len(sem_shapes)
    sems = [tuple(res[2 * g:2 * g + 2]) for g in range(len(groups))]
    return sems, list(res[k:k + n]), list(res[k + n:k + 2 * n]), res[-1]


def chip_copies_wait(srcs, lands, sems, after, scatter, name):
    n = len(srcs)
    after = list(after) if isinstance(after, (list, tuple)) else [after]

    def body(*refs):
        src, land = refs[:n], refs[n:2 * n]
        pair = refs[2 * n:2 * n + 2]
        x, y, _ = _place()
        for i in range(n):
            for j, (px, py) in enumerate(_other_chips(x, y)):
                copy = _chip_copy(src, land, pair, i, j, 2 * px + py, scatter)
                copy.wait_send()
                copy.wait_recv()

    res = pl.pallas_call(
        body, name=name,
        out_shape=[pltpu.HBM(a.shape, a.dtype) for a in (*srcs, *lands)],
        in_specs=[*_hbm(2 * n), SEM, SEM] + [UNREAD] * len(after),
        out_specs=_hbm(2 * n),
        input_output_aliases={i: i for i in range(2 * n)},
        compiler_params=pltpu.CompilerParams(has_side_effects=SIDE_EFFECT),
    )(*srcs, *lands, *sems, *after)
    return list(res[n:])


def small_all_gather(small, name):
    flips = [(fx, fy, fc) for fx in (0, 1) for fy in (0, 1) for fc in (0, 1)][1:]

    def body(in_ref, out_ref, send_sems, recv_sems, local_sem):
        x, y, c = _place()
        me = 4 * x + 2 * y + c

        def copy(k, slot):
            fx, fy, fc = flips[k]
            return pltpu.make_async_remote_copy(
                src_ref=in_ref, dst_ref=out_ref.at[slot], send_sem=send_sems.at[k], recv_sem=recv_sems.at[k],
                device_id=(x ^ fx, y ^ fy, c ^ fc), device_id_type=MESH)

        local = pltpu.make_async_copy(in_ref, out_ref.at[me], local_sem)
        local.start()
        for k in range(len(flips)):
            copy(k, me).start()
        for k, (fx, fy, fc) in enumerate(flips):
            copy(k, 4 * (x ^ fx) + 2 * (y ^ fy) + (c ^ fc)).wait()
        local.wait()

    return pl.pallas_call(
        body, name=name, in_specs=_hbm(1), out_specs=_hbm(1)[0],
        out_shape=jax.ShapeDtypeStruct((N_DEV,) + small.shape, small.dtype),
        scratch_shapes=[pltpu.SemaphoreType.DMA((len(flips),)), pltpu.SemaphoreType.DMA((len(flips),)),
                        pltpu.SemaphoreType.DMA],
    )(small)


def _sibling_copy(src, land, sems, i):
    x, y, c = _place()
    return pltpu.make_async_remote_copy(
        src_ref=src[i], dst_ref=land[i], send_sem=sems[0].at[i], recv_sem=sems[1].at[i],
        device_id=(x, y, 1 - c), device_id_type=MESH)


def sibling_copies_start(parts, name):
    n = len(parts)
    lands = [lax.empty(p.shape, p.dtype) for p in parts]

    def body(*refs):
        src, land, sems, token = refs[:n], refs[n:2 * n], refs[2 * n:2 * n + 2], refs[-1]
        for i in range(n):
            _sibling_copy(src, land, sems, i).start()
        token[...] = jnp.zeros_like(token)

    res = pl.pallas_call(
        body, name=name,
        out_shape=(pltpu.SemaphoreType.DMA((n,)), pltpu.SemaphoreType.DMA((n,)),
                   *[pltpu.HBM(a.shape, a.dtype) for a in (*parts, *lands)], jax.ShapeDtypeStruct((8, 128), F32)),
        in_specs=_hbm(2 * n),
        out_specs=(SEM, SEM, *_hbm(2 * n), pl.BlockSpec(memory_space=pltpu.VMEM)),
        input_output_aliases={i: 2 + i for i in range(2 * n)},
        compiler_params=pltpu.CompilerParams(has_side_effects=SIDE_EFFECT),
    )(*[pltpu.with_memory_space_constraint(a, pltpu.HBM) for a in (*parts, *lands)])
    return tuple(res[:2]), list(res[2:2 + n]), list(res[2 + n:2 + 2 * n]), res[-1]


def sibling_copies_wait(parts, lands, sems, after, name):
    n = len(parts)

    def body(*refs):
        src, land, sems = refs[:n], refs[n:2 * n], refs[2 * n:2 * n + 2]
        for i in range(n):
            copy = _sibling_copy(src, land, sems, i)
            copy.wait_send()
            copy.wait_recv()

    res = pl.pallas_call(
        body, name=name,
        out_shape=[pltpu.HBM(a.shape, a.dtype) for a in (*parts, *lands)],
        in_specs=[*_hbm(2 * n), SEM, SEM, UNREAD],
        out_specs=_hbm(2 * n),
        input_output_aliases={i: i for i in range(2 * n)},
        compiler_params=pltpu.CompilerParams(has_side_effects=SIDE_EFFECT),
    )(*parts, *lands, *sems, after)
    return list(res[n:])


def _row_tile(rows):
    for t in (256, 176, 128, 64, 32, 16, 8):
        if rows % t == 0:
            return t
    return rows


def chip_partial_sum(me, own_sm, recv, name):
    _, rows, cols = own_sm.shape
    tr = _row_tile(rows)

    def body(me_ref, own_ref, r1, r2, r3, o_ref):
        o_ref[...] = own_ref[...] + r1[...].astype(F32) + r2[...].astype(F32) + r3[...].astype(F32)

    def slot(d):
        return pl.BlockSpec((None, tr, cols), lambda i, me_ref: ((me_ref[0] + d) % N_CHIPS, i, 0))

    return pl.pallas_call(
        body, name=name,
        grid_spec=pltpu.PrefetchScalarGridSpec(
            num_scalar_prefetch=1, grid=(rows // tr,),
            in_specs=[slot(0), slot(1), slot(2), slot(3)],
            out_specs=pl.BlockSpec((tr, cols), lambda i, me_ref: (i, 0))),
        out_shape=jax.ShapeDtypeStruct((rows, cols), F32),
        compiler_params=_params("parallel"),
    )(me, own_sm, recv, recv, recv)


def _adamw(w, g, m, v):
    m = ADAM_B1 * m + (1.0 - ADAM_B1) * g
    v = ADAM_B2 * v + (1.0 - ADAM_B2) * (g * g)
    m_hat = m / (1.0 - ADAM_B1 ** ADAM_STEP)
    v_hat = v / (1.0 - ADAM_B2 ** ADAM_STEP)
    delta = -ADAM_LR * (m_hat / (jnp.sqrt(v_hat) + ADAM_EPS) + ADAM_WD * w)
    return delta, m, v


def adamw_pair(part, sib, w, m, v, name):
    rows, cols = w.shape
    tr = _row_tile(rows)

    def body(p_ref, s_ref, w_ref, m_ref, v_ref, g_ref, d_ref, nm_ref, nv_ref):
        g = p_ref[...] + s_ref[...]
        g_ref[...] = g
        d_ref[...], nm_ref[...], nv_ref[...] = _adamw(w_ref[...], g, m_ref[...], v_ref[...])

    spec = pl.BlockSpec((tr, cols), lambda i: (i, 0))
    return pl.pallas_call(
        body, name=name, grid=(rows // tr,), in_specs=[spec] * 5, out_specs=[spec] * 4,
        out_shape=[jax.ShapeDtypeStruct((rows, cols), F32)] * 4,
        compiler_params=_params("parallel"),
    )(part, sib, w, m, v)


def adamw_small(g_all, w, m, v, name):
    def body(ga_ref, w_ref, m_ref, v_ref, g_ref, d_ref, nm_ref, nv_ref):
        g = ga_ref[0]
        for k in range(1, N_DEV):
            g = g + ga_ref[k]
        g_ref[...] = g
        d_ref[...], nm_ref[...], nv_ref[...] = _adamw(w_ref[...], g, m_ref[...], v_ref[...])

    return pl.pallas_call(
        body, name=name, out_shape=[jax.ShapeDtypeStruct(w.shape, F32)] * 4,
    )(g_all, w, m, v)


WEIGHTS = ("ffn1_norm_pre", "ffn1_w_in", "ffn1_w_out", "ffn1_norm_post", "mix_norm_pre", "w_in", "sinks",
           "mem_norm", "w_mem_kv", "w_gate", "b_gate", "w_o_a", "w_o_b", "w_o_m", "w_out", "mix_norm_post",
           "ffn2_norm_pre", "ffn2_w_in", "ffn2_w_out", "ffn2_norm_post")
GATHER_STAGES = (("ffn1_in", "ffn1_out"), ("mix_in",), ("mix_rest", "ffn2"))
GATHER_GROUPS = {"ffn1_in": ("ffn1_w_in",), "ffn1_out": ("ffn1_w_out",),
                 "mix_in": ("w_in", "w_gate"), "mix_rest": ("w_mem_kv", "w_o_a", "w_o_b", "w_o_m", "w_out"),
                 "ffn2": ("ffn2_w_in", "ffn2_w_out")}
GROUPS = {"ffn1_in": ("ffn1_w_in",), "ffn1_out": ("ffn1_w_out",),
          "mix": ("w_in", "w_gate", "w_mem_kv", "w_o_a", "w_o_b", "w_o_m", "w_out"),
          "ffn2_in": ("ffn2_w_in",), "ffn2_out": ("ffn2_w_out",)}
COLUMN_SHARDED = ("ffn1_w_in", "ffn2_w_in", "w_in", "w_gate", "w_o_a", "w_o_b", "w_o_m")
KEPT_SHARD_MAJOR = ("ffn1_w_in", "ffn2_w_in", "w_gate")
GAINS = ("ffn1_norm_pre", "ffn1_norm_post", "mix_norm_pre", "mem_norm", "mix_norm_post", "ffn2_norm_pre",
         "ffn2_norm_post")
SMALL_ROWS = 16


def _pack_small(t):
    sinks = jnp.pad(t["sinks"], ((0, 0), (0, D_MODEL - t["sinks"].shape[1])))
    rows = [t[k] for k in GAINS] + [t["b_gate"].reshape(3, D_MODEL), sinks]
    packed = jnp.concatenate(rows, axis=0)
    return jnp.pad(packed, ((0, SMALL_ROWS - packed.shape[0]), (0, 0)))


def _unpack_small(p):
    out = {k: p[i:i + 1] for i, k in enumerate(GAINS)}
    out["b_gate"] = p[7:10].reshape(1, 3 * D_MODEL)
    out["sinks"] = p[10:11, :4]
    return out


def kernel(x, mem, ffn1_norm_pre, ffn1_w_in, ffn1_w_out, ffn1_norm_post, mix_norm_pre, w_in, sinks, mem_norm, w_mem_kv, w_gate, b_gate, w_o_a, w_o_b, w_o_m, w_out, mix_norm_post, ffn2_norm_pre, ffn2_w_in, ffn2_w_out, ffn2_norm_post, loss_target, m_ffn1_norm_pre, m_ffn1_w_in, m_ffn1_w_out, m_ffn1_norm_post, m_mix_norm_pre, m_w_in, m_sinks, m_mem_norm, m_w_mem_kv, m_w_gate, m_b_gate, m_w_o_a, m_w_o_b, m_w_o_m, m_w_out, m_mix_norm_post, m_ffn2_norm_pre, m_ffn2_w_in, m_ffn2_w_out, m_ffn2_norm_post, v_ffn1_norm_pre, v_ffn1_w_in, v_ffn1_w_out, v_ffn1_norm_post, v_mix_norm_pre, v_w_in, v_sinks, v_mem_norm, v_w_mem_kv, v_w_gate, v_b_gate, v_w_o_a, v_w_o_b, v_w_o_m, v_w_out, v_mix_norm_post, v_ffn2_norm_pre, v_ffn2_w_in, v_ffn2_w_out, v_ffn2_norm_post):
    given = dict(locals())
    wt = {k: given[k] for k in WEIGHTS}
    mom = {k: given["m_" + k] for k in WEIGHTS}
    var = {k: given["v_" + k] for k in WEIGHTS}
    chip = (2 * lax.axis_index("x") + lax.axis_index("y")).astype(jnp.int32)
    me = chip.reshape(1)

    def landing_zone(own):
        return lax.dynamic_update_slice_in_dim(lax.empty((N_CHIPS,) + own.shape, own.dtype), own[None], chip, 0)

    started = {}
    tokens = []

    def stage_keys(stage):
        return [k for g in GATHER_STAGES[stage] for k in GATHER_GROUPS[g]]

    def prepare(stage):
        shards = [(wt[k][0] + tokens[0][0, 0] if tokens else wt[k][0]).astype(BF16) for k in stage_keys(stage)]
        return shards, [landing_zone(s) for s in shards]

    def start_gather(stage, after):
        groups, keys = GATHER_STAGES[stage], stage_keys(stage)
        members = [[keys.index(k) for k in GATHER_GROUPS[g]] for g in groups]
        sems, shards, lands, token = chip_copies_start(
            *prepared[stage], members, False, f"weight_gather_start_{stage}", after)
        tokens.append(token)
        for g, idx, pair in zip(groups, members, sems):
            started[g] = ([shards[i] for i in idx], [lands[i] for i in idx], pair)

    prepared = {0: prepare(0)}
    start_gather(0, None)
    prepared.update({stage: prepare(stage) for stage in range(1, len(GATHER_STAGES))})

    def weights_of(group, after):
        if group == GATHER_STAGES[0][0]:
            after = [after] + [a for stage in range(1, len(GATHER_STAGES)) for part in prepared[stage] for a in part]
        got = chip_copies_wait(*started[group], after, False, f"weight_gather_wait_{group}")
        stage = [s + 1 for s, groups in enumerate(GATHER_STAGES[:-1]) if groups[0] == group]
        if stage:
            start_gather(stage[0], got[0])
        full = {}
        for k, g in zip(GATHER_GROUPS[group], got):
            if k in COLUMN_SHARDED:
                if k in ("ffn1_w_in", "ffn2_w_in"):
                    g = jnp.stack([g[0], g[2], g[1], g[3]])
                full[k] = jnp.swapaxes(g, 0, 1).reshape(g.shape[1], N_CHIPS * g.shape[2])
                if k == "w_in":
                    full[k] = to_kernel_heads(full[k])
            else:
                full[k] = g.reshape(N_CHIPS * g.shape[1], g.shape[2])
        return full

    in_flight = {}

    def send_grads(group, grads):
        def shard_major(k, g):
            if k in KEPT_SHARD_MAJOR:
                return g
            if k in COLUMN_SHARDED:
                return jnp.swapaxes(g.reshape(g.shape[0], N_CHIPS, g.shape[1] // N_CHIPS), 0, 1)
            return g.reshape(N_CHIPS, g.shape[0] // N_CHIPS, g.shape[1])

        own, wire = [], []
        for k in GROUPS[group]:
            g, rounded = grads[k] if isinstance(grads[k], (tuple, list)) else (grads[k], None)
            g = shard_major(k, from_kernel_heads(g) if k == "w_in" else g)
            own.append(g)
            wire.append(g.astype(BF16) if rounded is None else shard_major(k, rounded))
        zones = [lax.empty(b.shape, b.dtype) for b in wire]
        pair, wire, zones, sent = chip_copies_start(
            wire, zones, [list(range(len(wire)))], True, f"grad_scatter_start_{group}")
        in_flight[group] = (own, wire, zones, pair[0], sent)
        return sent

    gains = {k: wt[k] for k in GAINS}
    sq, dx, grads = layer_step(
        x[0], mem[0], loss_target[0], gains, sinks[0], b_gate, weights_of, send_grads, tokens[0][0, 0])
    loss = lax.psum(0.5 * jnp.sum(sq) / D_MODEL, ("x", "y", "c"))

    res = {}
    after = in_flight["ffn1_out"][4]
    swaps = []
    for stage in (("ffn2_in", "ffn2_out", "mix", "ffn1_in"), ("ffn1_out",)):
        names, parts = [], []
        for group in stage:
            own, wire, zones, pair, _ = in_flight[group]
            received = chip_copies_wait(wire, zones, pair, after, True, f"grad_scatter_wait_{group}")
            for k, g, r in zip(GROUPS[group], own, received):
                names.append(k)
                parts.append(chip_partial_sum(me, g, r, f"{k}_chip_sum"))
        pair, parts, lands, after = sibling_copies_start(parts, f"sibling_start_{stage[-1]}")
        swaps.append((stage[-1], names, parts, lands, pair))
    small_all = small_all_gather(_pack_small(grads), "small_grad_gather")
    packed = adamw_small(small_all, _pack_small(wt), _pack_small(mom), _pack_small(var), "small_adamw")
    after = packed[0]
    for tag, names, parts, lands, pair in swaps:
        sibs = sibling_copies_wait(parts, lands, pair, after, f"sibling_wait_{tag}")
        for k, p, s in zip(names, parts, sibs):
            res[k] = [t[None] for t in adamw_pair(p, s, wt[k][0], mom[k][0], var[k][0], f"{k}_adamw")]
        after = res[names[-1]][0]
    for idx, p in enumerate(packed):
        for k, t in _unpack_small(p).items():
            res.setdefault(k, [None] * 4)[idx] = t

    return (loss, dx[None], *[res[k][0] for k in WEIGHTS], *[res[k][1] for k in WEIGHTS],
            *[res[k][2] for k in WEIGHTS], *[res[k][3] for k in WEIGHTS])
```

```python
import functools

import jax
import jax.numpy as jnp
from jax import lax
from jax.experimental import pallas as pl
from jax.experimental.pallas import tpu as pltpu

F32 = jnp.float32
BF16 = jnp.bfloat16

D_MODEL = 1024
D_FF = 2816
HEAD = 128
N_CHIPS = 4
N_DEV = 8
EPS = 1e-6
NEG_INF = -1e30
ROPE_THETA = 10000.0
ATT_SCALE = HEAD ** -0.5

ADAM_LR = 0.001
ADAM_B1 = 0.9
ADAM_B2 = 0.999
ADAM_EPS = 1e-08
ADAM_WD = 0.01
ADAM_STEP = 10

VMEM_LIMIT = 52 * 2 ** 20
VMEM_LIMIT_LARGE = 60 * 2 ** 20
MESH = pl.DeviceIdType.MESH

QKV_W = 3840
DIL = ((128, 1), (512, 4), (2048, 16))
B_BASE, MQ, A_BASE = 0, 8, 12
_AQ, _AK, _AV, _BQ, _BK, _BV, _MQ = 0, 6, 12, 18, 22, 24, 26
HEAD_ORDER = tuple(
    [h for j in range(2) for h in (_BQ + 2 * j, _BQ + 2 * j + 1, _BK + j, _BV + j)]
    + [_MQ + i for i in range(4)]
    + [h for g in range(3) for i in range(2) for h in (_AQ + 2 * g + i, _AK + 2 * g + i, _AV + 2 * g + i)])
ROTARY_HEADS = tuple(p for p, h in enumerate(HEAD_ORDER) if h < _AV or _BQ <= h < _BV)


def to_kernel_heads(w):
    return jnp.concatenate([w[..., h * HEAD:(h + 1) * HEAD] for h in HEAD_ORDER], axis=-1)


def from_kernel_heads(w):
    place = {h: p for p, h in enumerate(HEAD_ORDER)}
    return jnp.concatenate([w[..., place[h] * HEAD:(place[h] + 1) * HEAD] for h in range(len(HEAD_ORDER))], axis=-1)

TM = 512
FF_T = D_FF // 2


def _params(*sem):
    return pltpu.CompilerParams(dimension_semantics=sem, vmem_limit_bytes=VMEM_LIMIT)


def _dot(a, b):
    return jnp.dot(a, b, preferred_element_type=F32)


def _dot_nt(a, b):
    return lax.dot_general(a, b, (((1,), (1,)), ((), ())), preferred_element_type=F32)


def _dot_tn(a, b):
    return lax.dot_general(a, b, (((0,), (0,)), ((), ())), preferred_element_type=F32)


def _rstd(x):
    return lax.rsqrt(jnp.mean(x * x, axis=-1, keepdims=True) + EPS)


def _sigmoid(x):
    return 0.5 * jnp.tanh(0.5 * x) + 0.5


def _ffn_perm(k):
    return (k % 2) * 2 + k // 2


UNREAD = pl.BlockSpec(memory_space=pl.ANY)


def _resident(arr):
    return pl.BlockSpec(arr.shape, lambda *_: (0,) * arr.ndim, pipeline_mode=pl.Buffered(1))


def rms_scale(x, g, name, after):
    T, D = x.shape
    tm = 1024

    def body(x_ref, g_ref, _, o_ref):
        v = x_ref[...]
        o_ref[...] = (v * _rstd(v) * g_ref[...]).astype(BF16)

    spec = pl.BlockSpec((tm, D), lambda i: (i, 0))
    return pl.pallas_call(
        body, name=name, grid=(T // tm,), in_specs=[spec, _resident(g), UNREAD], out_specs=spec,
        out_shape=jax.ShapeDtypeStruct((T, D), BF16), compiler_params=_params("parallel"),
    )(x, g, after)


def ffn_in(h, g, w, name, xn=None):
    T, D = h.shape
    normed = xn is not None

    def body(h_ref, g_ref, w_ref, *outs):
        if normed:
            xn, (gu_ref, a_ref) = h_ref[...], outs
        else:
            xn_ref, gu_ref, a_ref = outs
            x = h_ref[...]
            xn = (x * _rstd(x) * g_ref[...]).astype(BF16)
            xn_ref[...] = xn
        for j in range(2):
            gu = _dot(xn, w_ref[:, j * 2 * FF_T:(j + 1) * 2 * FF_T])
            gu_ref[:, j * 2 * FF_T:(j + 1) * 2 * FF_T] = gu.astype(BF16)
            gate, up = gu[:, :FF_T], gu[:, FF_T:]
            a_ref[:, j * FF_T:(j + 1) * FF_T] = (gate * _sigmoid(gate) * up).astype(BF16)

    def rows(width):
        return pl.BlockSpec((TM, width), lambda i: (i, 0))

    res = pl.pallas_call(
        body, name=name,
        grid=(T // TM,),
        in_specs=[rows(D), _resident(g), _resident(w)],
        out_specs=[rows(D)] * (not normed) + [rows(2 * D_FF), rows(D_FF)],
        out_shape=[jax.ShapeDtypeStruct((T, D), BF16)] * (not normed)
                  + [jax.ShapeDtypeStruct((T, 2 * D_FF), BF16), jax.ShapeDtypeStruct((T, D_FF), BF16)],
        compiler_params=_params("parallel"),
    )(xn if normed else h, g, w)
    return (xn, *res) if normed else tuple(res)


def mm_norm_res(a, w, h_in, g, coef, name, target=None):
    T, K = a.shape
    D = w.shape[1]
    final = target is not None
    tm = min(2 * TM, T)

    def body(*refs):
        if final:
            a_ref, w_ref, h_ref, g_ref, t_ref, f_ref, o_ref, l_ref = refs
        else:
            a_ref, w_ref, h_ref, g_ref, f_ref, o_ref = refs
        f = _dot(a_ref[...], w_ref[...])
        f_ref[...] = f
        y = h_ref[...] + coef * (f * _rstd(f) * g_ref[...])
        if final:
            err = y - t_ref[...]
            o_ref[...] = err * (1.0 / D)

            @pl.when(pl.program_id(0) == 0)
            def _():
                l_ref[...] = jnp.zeros_like(l_ref)

            sq = jnp.sum((err * err).reshape(tm // 8, 8, D), axis=0)
            l_ref[...] += functools.reduce(jnp.add, [sq[:, c:c + HEAD] for c in range(0, D, HEAD)])
        else:
            o_ref[...] = y

    row = pl.BlockSpec((tm, D), lambda i: (i, 0))
    in_specs = [pl.BlockSpec((tm, K), lambda i: (i, 0)),
                _resident(w),
                row, pl.BlockSpec((1, D), lambda i: (0, 0))]
    out_specs = [row, row]
    out_shape = [jax.ShapeDtypeStruct((T, D), F32), jax.ShapeDtypeStruct((T, D), F32)]
    args = [a, w, h_in, g]
    if final:
        in_specs.append(row)
        args.append(target)
        out_specs.append(pl.BlockSpec((8, 128), lambda i: (0, 0)))
        out_shape.append(jax.ShapeDtypeStruct((8, 128), F32))
    return pl.pallas_call(
        body, name=name, grid=(T // tm,), in_specs=in_specs, out_specs=out_specs, out_shape=out_shape,
        compiler_params=pltpu.CompilerParams(dimension_semantics=("arbitrary",), vmem_limit_bytes=VMEM_LIMIT_LARGE),
    )(*args)


def _rope(x, cos, sin_signed):
    return x * cos + pltpu.roll(x, HEAD // 2, axis=1) * sin_signed


def _unrope(x, cos, sin_signed):
    return x * cos - pltpu.roll(x, HEAD // 2, axis=1) * sin_signed


def mix_in(h, g, w, w_gate, b_gate, cos, sin_signed, name):
    T, D = h.shape
    tn = 768

    def body(h_ref, g_ref, w_ref, wg_ref, b_ref, c_ref, s_ref, u_ref, o_ref, gt_ref):
        x = h_ref[...]
        u = (x * _rstd(x) * g_ref[...]).astype(BF16)
        u_ref[...] = u
        c, s = c_ref[...], s_ref[...]
        for j in range(QKV_W // tn):
            acc = _dot(u, w_ref[:, j * tn:(j + 1) * tn])
            for hd in range(tn // HEAD):
                head = j * (tn // HEAD) + hd
                part = acc[:, hd * HEAD:(hd + 1) * HEAD]
                if head in ROTARY_HEADS:
                    part = _rope(part, c, s)
                o_ref[:, head * HEAD:(head + 1) * HEAD] = part.astype(BF16)
        for j in range(w_gate.shape[1] // tn):
            cols = slice(j * tn, (j + 1) * tn)
            gt_ref[:, cols] = _sigmoid(_dot(u, wg_ref[:, cols]) + b_ref[:, cols]).astype(BF16)

    def rows(width):
        return pl.BlockSpec((TM, width), lambda i: (i, 0))

    return pl.pallas_call(
        body, name=name,
        grid=(T // TM,),
        in_specs=[rows(D), _resident(g), _resident(w), _resident(w_gate), _resident(b_gate), rows(HEAD), rows(HEAD)],
        out_specs=[rows(D), rows(QKV_W), rows(w_gate.shape[1])],
        out_shape=[jax.ShapeDtypeStruct((T, D), BF16), jax.ShapeDtypeStruct((T, QKV_W), BF16),
                   jax.ShapeDtypeStruct((T, w_gate.shape[1]), BF16)],
        compiler_params=_params("parallel"),
    )(h, g, w, w_gate, b_gate, cos, sin_signed)


def gate_merge_out(gt, o_a, o_b, o_m, w_a, w_b, w_m, w_out, h_in, g, name):
    T = gt.shape[0]
    D = D_MODEL

    def body(gt_ref, oa_ref, ob_ref, om_ref, wa_ref, wb_ref, wm_ref, wo_ref, h_ref, g_ref, m_ref, f_ref, o_ref):
        acc = gt_ref[:, :D].astype(F32) * _dot(oa_ref[...], wa_ref[...])
        acc += gt_ref[:, D:2 * D].astype(F32) * _dot(ob_ref[...], wb_ref[...])
        acc += gt_ref[:, 2 * D:].astype(F32) * _dot(om_ref[...], wm_ref[...])
        merged = acc.astype(BF16)
        m_ref[...] = merged
        f = _dot(merged, wo_ref[...])
        f_ref[...] = f
        o_ref[...] = h_ref[...] + f * _rstd(f) * g_ref[...]

    tm = min(2 * TM, T)

    def rows(width):
        return pl.BlockSpec((tm, width), lambda i: (i, 0))

    return pl.pallas_call(
        body, name=name, grid=(T // tm,),
        in_specs=[rows(3 * D), rows(o_a.shape[1]), rows(o_b.shape[1]), rows(o_m.shape[1]),
                  _resident(w_a), _resident(w_b), _resident(w_m), _resident(w_out), rows(D), _resident(g)],
        out_specs=[rows(D), rows(D), rows(D)],
        out_shape=[jax.ShapeDtypeStruct((T, D), BF16), jax.ShapeDtypeStruct((T, D), F32),
                   jax.ShapeDtypeStruct((T, D), F32)],
        compiler_params=pltpu.CompilerParams(dimension_semantics=("parallel",), vmem_limit_bytes=VMEM_LIMIT_LARGE),
    )(gt, o_a, o_b, o_m, w_a, w_b, w_m, w_out, h_in, g)


def _band_rows(start, r):
    return pl.ds(start, HEAD) if r == 1 else pl.ds(start, HEAD, stride=r)


def _band_mask(max_dist, first_has_prev):
    row = lax.broadcasted_iota(jnp.int32, (HEAD, 2 * HEAD), 0)
    col = lax.broadcasted_iota(jnp.int32, (HEAD, 2 * HEAD), 1)
    dist = row + HEAD - col
    band = (dist >= 0) & (dist <= max_dist)
    return band, band & (col >= jnp.where(first_has_prev, 0, HEAD))


def _stack(parts):
    return parts[0] if len(parts) == 1 else jnp.concatenate(parts, axis=0)


def _band_specs(BT, SB, nsub, base, grp):
    stride = grp + 2

    def cur(off, width):
        return pl.BlockSpec((BT, width * HEAD), lambda h, i: (i, (base + h * stride + off) // width))

    def prev(off):
        return pl.BlockSpec((SB, HEAD), lambda h, i: (jnp.maximum(i * nsub - 1, 0), base + h * stride + off))

    return cur(0, grp), cur(grp, 1), prev(grp), cur(grp + 1, 1), prev(grp + 1)


def band_fwd(qkv, sinks, *, r, base, hkv, grp, max_dist, out_dtype, name, merge=None):
    T, W = qkv.shape
    SB = HEAD * r
    BT = min(2048, T)
    nsub, nib = BT // SB, T // BT
    hq = hkv * grp
    heads = [slice(g * HEAD, (g + 1) * HEAD) for g in range(grp)]
    others = [] if merge is None else [*merge[0], *merge[1]]

    def body(sink_ref, q_ref, kc_ref, kp_ref, vc_ref, vp_ref, *rest):
        joint_o, joint_l = rest[len(others):len(others) + 2]
        qf, kf, vf = rest[len(others) + 2:len(others) + 5]
        o_ref, l_ref = rest[len(others) + 5:] if others else (joint_o, joint_l)
        kvh, ib = pl.program_id(0), pl.program_id(1)
        qf[...] = q_ref[...].astype(F32)
        kf[:SB] = kp_ref[...].astype(F32)
        kf[SB:] = kc_ref[...].astype(F32)
        vf[:SB] = vp_ref[...].astype(F32)
        vf[SB:] = vc_ref[...].astype(F32)
        band, band_first = _band_mask(max_dist, ib > 0)
        for c in range(r):
            k_old, v_old = kf[_band_rows(c, r)], vf[_band_rows(c, r)]
            for j in range(nsub):
                mask = band_first if j == 0 else band
                rows = _band_rows(j * SB + c, r)
                k_own, v_own = kf[_band_rows((j + 1) * SB + c, r)], vf[_band_rows((j + 1) * SB + c, r)]
                kcat = jnp.concatenate([k_old, k_own], axis=0).astype(BF16)
                vcat = jnp.concatenate([v_old, v_own], axis=0).astype(BF16)
                k_old, v_old = k_own, v_own
                s_all = _dot_nt(_stack([qf[rows, cols] for cols in heads]).astype(BF16), kcat) * ATT_SCALE
                probs, tots = [], []
                for g, cols in enumerate(heads):
                    s = jnp.where(mask, s_all[cols], NEG_INF)
                    sk = sink_ref[kvh * grp + g]
                    m = jnp.maximum(jnp.max(s, axis=-1, keepdims=True), sk)
                    p = jnp.exp(s - m)
                    tot = jnp.sum(p, axis=-1, keepdims=True) + jnp.exp(sk - m)
                    probs.append(p.astype(BF16))
                    tots.append(tot)
                    l_ref[rows, cols] = jnp.broadcast_to(m + jnp.log(tot), (HEAD, HEAD))
                o_all = _dot(_stack(probs), vcat)
                for g, cols in enumerate(heads):
                    o_ref[rows, cols] = (o_all[cols] * (1.0 / tots[g])).astype(o_ref.dtype)

        if others:
            half = len(others) // 2
            outs = [ref[...] for ref in rest[:half]] + [o_ref[...]]
            logs = [ref[...] for ref in rest[half:len(others)]] + [l_ref[...]]
            top = functools.reduce(jnp.maximum, logs)
            weights = [jnp.exp(lg - top) for lg in logs]
            total = functools.reduce(jnp.add, weights)
            mixed = functools.reduce(jnp.add, [wgt * out for wgt, out in zip(weights, outs)])
            joint_o[...] = (mixed / total).astype(out_dtype)
            joint_l[...] = top + jnp.log(total)

    out_spec = pl.BlockSpec((BT, grp * HEAD), lambda h, i: (i, h))
    own = [pltpu.VMEM((BT, grp * HEAD), F32)] * 2 if others else []
    return pl.pallas_call(
        body, name=name, grid=(hkv, nib),
        in_specs=[pl.BlockSpec(memory_space=pltpu.SMEM), *_band_specs(BT, SB, nsub, base, grp)]
                 + [out_spec] * len(others),
        out_specs=[out_spec, out_spec],
        out_shape=[jax.ShapeDtypeStruct((T, hq * HEAD), out_dtype), jax.ShapeDtypeStruct((T, hq * HEAD), F32)],
        scratch_shapes=[pltpu.VMEM((BT, grp * HEAD), F32), pltpu.VMEM((SB + BT, HEAD), F32),
                        pltpu.VMEM((SB + BT, HEAD), F32)] + own,
        compiler_params=_params("parallel", "arbitrary"),
    )(sinks, qkv, qkv, qkv, qkv, qkv, *others)


def band_bwd(qkv, dqkv, do, o, lse, cos, sin_signed, sinks, *, r, base, hkv, grp, max_dist, name):
    T, W = qkv.shape
    SB = HEAD * r
    BT = min(max(2048, 2 * SB), T)
    nsub, nib = BT // SB, T // BT
    nblk = T // SB
    with_sink = sinks is not None
    heads = [slice(g * HEAD, (g + 1) * HEAD) for g in range(grp)]

    def body(*refs):
        if with_sink:
            sink_ref, refs = refs[0], refs[1:]
        (q_ref, kc_ref, kp_ref, vc_ref, vp_ref, qn_ref, do_ref, don_ref, o_ref, on_ref, l_ref, ln_ref,
         c_ref, s_ref, _) = refs[:15]
        out_ref = refs[15]
        ds_ref = refs[16] if with_sink else None
        qf, dof, of, kf, vf, dqf, dkf, dvf = refs[-8:]
        kvh, ib = pl.program_id(0), pl.program_id(1)
        for buf, cur_ref, nxt_ref in ((qf, q_ref, qn_ref), (dof, do_ref, don_ref), (of, o_ref, on_ref)):
            buf[:BT] = cur_ref[...].astype(F32)
            buf[BT:] = nxt_ref[...].astype(F32)
        kf[:SB] = kp_ref[...].astype(F32)
        kf[SB:] = kc_ref[...].astype(F32)
        vf[:SB] = vp_ref[...].astype(F32)
        vf[SB:] = vc_ref[...].astype(F32)
        band, band_first = _band_mask(max_dist, ib > 0)
        if with_sink:
            @pl.when(ib == 0)
            def _():
                ds_ref[...] = jnp.zeros_like(ds_ref)

        def grads(rows, logzs, keys, vals, mask):
            q = _stack([qf[rows, cols] for cols in heads]).astype(BF16)
            dout = _stack([dof[rows, cols] for cols in heads]).astype(BF16)
            s_all = _dot_nt(q, keys) * ATT_SCALE
            dp_all = _dot_nt(dout, vals)
            probs, dss, deltas = [], [], []
            for g, cols in enumerate(heads):
                delta = jnp.sum(dof[rows, cols] * of[rows, cols], axis=-1, keepdims=True)
                p = jnp.exp(jnp.where(mask, s_all[cols], NEG_INF) - logzs[g][:, :1])
                probs.append(p.astype(BF16))
                dss.append((p * (dp_all[cols] - delta) * ATT_SCALE).astype(BF16))
                deltas.append(delta)
            return q, dout, _stack(probs), _stack(dss), deltas

        row = lax.broadcasted_iota(jnp.int32, (HEAD, HEAD), 0)
        col = lax.broadcasted_iota(jnp.int32, (HEAD, HEAD), 1)
        reach = col >= row + jnp.where(ib < nib - 1, HEAD - max_dist, 2 * HEAD)
        for c in range(r):
            k_old, v_old = kf[_band_rows(c, r)], vf[_band_rows(c, r)]
            dk_own = dv_own = None
            for j in range(nsub):
                rows = _band_rows(j * SB + c, r)
                k_own, v_own = kf[_band_rows((j + 1) * SB + c, r)], vf[_band_rows((j + 1) * SB + c, r)]
                kcat = jnp.concatenate([k_old, k_own], axis=0).astype(BF16)
                vcat = jnp.concatenate([v_old, v_own], axis=0).astype(BF16)
                logzs = [l_ref[rows, cols] for cols in heads]
                q, dout, p, ds, deltas = grads(rows, logzs, kcat, vcat, band_first if j == 0 else band)
                dq = _dot(ds, kcat)
                for g, cols in enumerate(heads):
                    dqf[rows, cols] = dq[cols]
                    if with_sink:
                        p_sink = jnp.exp(sink_ref[kvh * grp + g] - logzs[g][:, :1])
                        ds_ref[g * 8:(g + 1) * 8] += jnp.sum(p_sink * deltas[g])
                dk, dv = _dot_tn(ds, q), _dot_tn(p, dout)
                if j > 0:
                    done = _band_rows((j - 1) * SB + c, r)
                    dkf[done] = dk_own + dk[:HEAD]
                    dvf[done] = dv_own + dv[:HEAD]
                dk_own, dv_own = dk[HEAD:], dv[HEAD:]
                k_old, v_old = k_own, v_own
            logzs = [ln_ref[_band_rows(c, r), cols] for cols in heads]
            q, dout, p, ds, _ = grads(_band_rows(BT + c, r), logzs, k_old.astype(BF16), v_old.astype(BF16), reach)
            done = _band_rows((nsub - 1) * SB + c, r)
            dkf[done] = dk_own + _dot_tn(ds, q)
            dvf[done] = dv_own + _dot_tn(p, dout)

        cs, sn = c_ref[...], s_ref[...]
        for cols in heads:
            out_ref[:, cols] = _unrope(dqf[:, cols], cs, sn).astype(BF16)
        out_ref[:, grp * HEAD:(grp + 1) * HEAD] = _unrope(dkf[...], cs, sn).astype(BF16)
        out_ref[:, (grp + 1) * HEAD:] = dvf[...].astype(BF16)

    def nxt_row(i):
        return jnp.minimum((i + 1) * nsub, nblk - 1)

    stride = grp + 2
    q_next = pl.BlockSpec((SB, grp * HEAD), lambda h, i: (nxt_row(i), (base + h * stride) // grp))
    head_cur = pl.BlockSpec((BT, grp * HEAD), lambda h, i: (i, h))
    head_next = pl.BlockSpec((SB, grp * HEAD), lambda h, i: (nxt_row(i), h))
    table = pl.BlockSpec((BT, HEAD), lambda h, i: (i, 0))

    in_specs = [*_band_specs(BT, SB, nsub, base, grp), q_next,
                head_cur, head_next, head_cur, head_next, head_cur, head_next, table, table, UNREAD]
    args = [qkv, qkv, qkv, qkv, qkv, qkv, do, do, o, o, lse, lse, cos, sin_signed, dqkv]
    out_specs = [pl.BlockSpec((BT, stride * HEAD), lambda h, i: (i, base // stride + h))]
    out_shape = [jax.ShapeDtypeStruct(dqkv.shape, dqkv.dtype)]
    if with_sink:
        in_specs.insert(0, pl.BlockSpec(memory_space=pltpu.SMEM))
        args.insert(0, sinks)
        out_specs.append(pl.BlockSpec((None, grp * 8, HEAD), lambda h, i: (h, 0, 0)))
        out_shape.append(jax.ShapeDtypeStruct((hkv, grp * 8, HEAD), F32))
    wide = pltpu.VMEM((BT + SB, grp * HEAD), F32)
    tall = pltpu.VMEM((SB + BT, HEAD), F32)
    grad = pltpu.VMEM((BT, HEAD), F32)
    return pl.pallas_call(
        body, name=name, grid=(hkv, nib), in_specs=in_specs, out_specs=out_specs, out_shape=out_shape,
        input_output_aliases={len(args) - 1: 0},
        scratch_shapes=[wide, wide, wide, tall, tall, pltpu.VMEM((BT, grp * HEAD), F32), grad, grad],
        compiler_params=pltpu.CompilerParams(dimension_semantics=("parallel", "arbitrary"),
                                             vmem_limit_bytes=VMEM_LIMIT_LARGE),
    )(*args)


M_HEADS = 4


def mem_kv(mem, g, w, name):
    n, D = mem.shape

    def body(m_ref, g_ref, w_ref, mn_ref, kv_ref):
        x = m_ref[...]
        mn = (x * _rstd(x) * g_ref[...]).astype(BF16)
        mn_ref[...] = mn
        kv_ref[...] = _dot(mn, w_ref[...]).astype(BF16)

    return pl.pallas_call(
        body, name=name,
        out_shape=[jax.ShapeDtypeStruct((n, D), BF16), jax.ShapeDtypeStruct((n, w.shape[1]), BF16)],
        compiler_params=pltpu.CompilerParams(vmem_limit_bytes=VMEM_LIMIT),
    )(mem, g, w)


def mem_fwd(qkv, mkv, name):
    T = qkv.shape[0]
    n = mkv.shape[0]
    RB = 1024

    def body(q_ref, kv_ref, o_ref, l_ref):
        for h in range(M_HEADS):
            cols = slice(h * HEAD, (h + 1) * HEAD)
            s = _dot_nt(q_ref[:, cols], kv_ref[:, cols]) * ATT_SCALE
            m = jnp.max(s, axis=-1, keepdims=True)
            p = jnp.exp(s - m)
            den = jnp.sum(p, axis=-1, keepdims=True)
            vals = kv_ref[:, (M_HEADS + h) * HEAD:(M_HEADS + h + 1) * HEAD]
            o_ref[:, cols] = (_dot(p.astype(BF16), vals) * (1.0 / den)).astype(BF16)
            l_ref[:, cols] = jnp.broadcast_to(m + jnp.log(den), (RB, HEAD))

    out = pl.BlockSpec((RB, M_HEADS * HEAD), lambda i: (i, 0))
    return pl.pallas_call(
        body, name=name, grid=(T // RB,),
        in_specs=[pl.BlockSpec((RB, M_HEADS * HEAD), lambda i: (i, MQ // M_HEADS)), _resident(mkv)],
        out_specs=[out, out],
        out_shape=[jax.ShapeDtypeStruct((T, M_HEADS * HEAD), BF16), jax.ShapeDtypeStruct((T, M_HEADS * HEAD), F32)],
        compiler_params=_params("parallel"),
    )(qkv, mkv)


def mem_bwd(qkv, dqkv, mkv, do, o, lse, name):
    T = qkv.shape[0]
    n = mkv.shape[0]
    RB = 1024

    def body(q_ref, kv_ref, do_ref, o_ref, l_ref, _, dq_ref, dk_ref, dv_ref):
        @pl.when(pl.program_id(0) == 0)
        def _():
            dk_ref[...] = jnp.zeros_like(dk_ref)
            dv_ref[...] = jnp.zeros_like(dv_ref)

        for h in range(M_HEADS):
            cols = slice(h * HEAD, (h + 1) * HEAD)
            keys, vals = kv_ref[:, cols], kv_ref[:, (M_HEADS + h) * HEAD:(M_HEADS + h + 1) * HEAD]
            q, dout = q_ref[:, cols], do_ref[:, cols]
            delta = jnp.sum(dout.astype(F32) * o_ref[:, cols].astype(F32), axis=-1, keepdims=True)
            p = jnp.exp(_dot_nt(q, keys) * ATT_SCALE - l_ref[:, cols][:, :1])
            ds = (p * (_dot_nt(dout, vals) - delta) * ATT_SCALE).astype(BF16)
            dq_ref[:, cols] = _dot(ds, keys).astype(BF16)
            dk_ref[:, cols] += _dot_tn(ds, q)
            dv_ref[:, cols] += _dot_tn(p.astype(BF16), dout)

    wide = M_HEADS * HEAD
    tok = pl.BlockSpec((RB, wide), lambda i: (i, 0))
    q_cols = pl.BlockSpec((RB, wide), lambda i: (i, MQ // M_HEADS))
    slot = pl.BlockSpec((n, wide), lambda i: (0, 0))
    return pl.pallas_call(
        body, name=name, grid=(T // RB,),
        in_specs=[q_cols, _resident(mkv), tok, tok, tok, UNREAD],
        out_specs=[q_cols, slot, slot],
        out_shape=[jax.ShapeDtypeStruct(dqkv.shape, dqkv.dtype),
                   jax.ShapeDtypeStruct((n, wide), F32), jax.ShapeDtypeStruct((n, wide), F32)],
        input_output_aliases={5: 0},
        compiler_params=_params("arbitrary"),
    )(qkv, mkv, do, o, lse, dqkv)


def mem_kv_bwd(mem, g, mem_n, w, dmkv, name):
    n, D = mem.shape

    def body(m_ref, g_ref, mn_ref, w_ref, d_ref, dw_ref, dg_ref):
        d = d_ref[...].astype(BF16)
        dw_ref[...] = _dot_tn(mn_ref[...], d)
        x = m_ref[...]
        dg_ref[...] = jnp.sum(_dot_nt(d, w_ref[...]) * (x * _rstd(x)), axis=0, keepdims=True)

    return pl.pallas_call(
        body, name=name,
        out_shape=[jax.ShapeDtypeStruct(w.shape, F32), jax.ShapeDtypeStruct((1, D), F32)],
        compiler_params=pltpu.CompilerParams(vmem_limit_bytes=VMEM_LIMIT),
    )(mem, g, mem_n, w, dmkv)


def _rms_bwd(dn, f, g):
    r = _rstd(f)
    fhat = f * r
    dfhat = dn * g
    df = r * (dfhat - fhat * jnp.mean(dfhat * fhat, axis=-1, keepdims=True))
    return df, jnp.sum(dn * fhat, axis=0, keepdims=True)


def ffn_tokens_bwd(dh, f, h_in, gu, g_pre, g_post, w_in, w_out, coef, name, after):
    T, D = dh.shape

    def body(dh_ref, f_ref, h_ref, gu_ref, gpre_ref, gpost_ref, win_ref, wout_ref, _,
             df_ref, dgu_ref, dhin_ref, dgpre_ref, dgpost_ref, dxn_ref):
        i, j = pl.program_id(0), pl.program_id(1)

        @pl.when(j == 0)
        def _():
            @pl.when(i == 0)
            def _():
                dgpre_ref[...] = jnp.zeros_like(dgpre_ref)
                dgpost_ref[...] = jnp.zeros_like(dgpost_ref)

            df, dg_post = _rms_bwd(coef * dh_ref[...], f_ref[...], gpost_ref[...])
            dgpost_ref[...] += dg_post
            df_ref[...] = df.astype(BF16)

        for jj in range(2):
            @pl.when(j == jj)
            def _(jj=jj):
                lo, mid, hi = 2 * jj * FF_T, (2 * jj + 1) * FF_T, (2 * jj + 2) * FF_T
                da = _dot_nt(df_ref[...], wout_ref[jj * FF_T:(jj + 1) * FF_T, :])
                gate = gu_ref[:, :FF_T].astype(F32)
                up = gu_ref[:, FF_T:].astype(F32)
                sig = _sigmoid(gate)
                dgate = (da * up * sig * (1.0 + gate * (1.0 - sig))).astype(BF16)
                dup = (da * gate * sig).astype(BF16)
                dgu_ref[:, :FF_T] = dgate
                dgu_ref[:, FF_T:] = dup
                part = _dot_nt(dgate, win_ref[:, lo:mid]) + _dot_nt(dup, win_ref[:, mid:hi])
                if jj == 0:
                    dxn_ref[...] = part
                else:
                    h = h_ref[...]
                    r = _rstd(h)
                    xhat = h * r
                    dxn = dxn_ref[...] + part
                    dxhat = dxn * gpre_ref[...]
                    dhin_ref[...] = dh_ref[...] + r * (dxhat - xhat * jnp.mean(dxhat * xhat, axis=-1, keepdims=True))
                    dgpre_ref[...] += jnp.sum(dxn * xhat, axis=0, keepdims=True)

    row = pl.BlockSpec((TM, D), lambda i, j: (i, 0))
    wide = pl.BlockSpec((TM, 2 * FF_T), lambda i, j: (i, j))
    vec = pl.BlockSpec((1, D), lambda i, j: (0, 0))
    return pl.pallas_call(
        body, name=name, grid=(T // TM, 2),
        in_specs=[row, row, row, wide, _resident(g_pre), _resident(g_post), _resident(w_in), _resident(w_out),
                  UNREAD],
        out_specs=[row, wide, row, vec, vec],
        out_shape=[jax.ShapeDtypeStruct((T, D), BF16), jax.ShapeDtypeStruct((T, 2 * D_FF), BF16),
                   jax.ShapeDtypeStruct((T, D), F32), jax.ShapeDtypeStruct((1, D), F32),
                   jax.ShapeDtypeStruct((1, D), F32)],
        scratch_shapes=[pltpu.VMEM((TM, D), F32)],
        compiler_params=pltpu.CompilerParams(dimension_semantics=("arbitrary", "arbitrary"),
                                             vmem_limit_bytes=VMEM_LIMIT_LARGE),
    )(dh, f, h_in, gu, g_pre, g_post, w_in, w_out, after)


def mm_nt_norm_bwd(pieces, h_in, dh_out, g, name, after):
    T, D = h_in.shape

    def body(*refs):
        ab = refs[:2 * len(pieces)]
        h_ref, dh_ref, g_ref, _, o_ref, dg_ref = refs[2 * len(pieces):]
        dxn = _dot_nt(ab[0][...], ab[1][...])
        for p in range(1, len(pieces)):
            dxn += _dot_nt(ab[2 * p][...], ab[2 * p + 1][...])
        h = h_ref[...]
        r = _rstd(h)
        xhat = h * r
        dxhat = dxn * g_ref[...]
        o_ref[...] = dh_ref[...] + r * (dxhat - xhat * jnp.mean(dxhat * xhat, axis=-1, keepdims=True))

        @pl.when(pl.program_id(0) == 0)
        def _():
            dg_ref[...] = jnp.zeros_like(dg_ref)

        dg_ref[...] += jnp.sum(dxn * xhat, axis=0, keepdims=True)

    in_specs, args = [], []
    for a, w in pieces:
        in_specs += [pl.BlockSpec((TM, a.shape[1]), lambda i: (i, 0)), _resident(w)]
        args += [a, w]
    row = pl.BlockSpec((TM, D), lambda i: (i, 0))
    return pl.pallas_call(
        body, name=name, grid=(T // TM,),
        in_specs=in_specs + [row, row, _resident(g), UNREAD],
        out_specs=[row, pl.BlockSpec((1, D), lambda i: (0, 0))],
        out_shape=[jax.ShapeDtypeStruct((T, D), F32), jax.ShapeDtypeStruct((1, D), F32)],
        compiler_params=_params("arbitrary"),
    )(*args, h_in, dh_out, g, after)


def gate_merge_out_bwd(dh, f, g, w_out, merged, gt, o_a, o_b, o_m, w_a, w_b, w_m, name, after):
    T = dh.shape[0]
    D = D_MODEL
    branch = ((o_a, w_a), (o_b, w_b), (o_m, w_m))

    def body(dh_ref, f_ref, g_ref, wo_ref, m_ref, gt_ref, oa_ref, ob_ref, om_ref, wa_ref, wb_ref, wm_ref, _,
             dg_ref, dwo_ref, dgt_ref, doa_ref, dob_ref, dom_ref, db_ref, dwa_ref, dwb_ref, dwm_ref):
        @pl.when(pl.program_id(0) == 0)
        def _():
            for acc in (dg_ref, dwo_ref, db_ref, dwa_ref, dwb_ref, dwm_ref):
                acc[...] = jnp.zeros_like(acc)

        df, dg = _rms_bwd(dh_ref[...], f_ref[...], g_ref[...])
        dg_ref[...] += dg
        df = df.astype(BF16)
        dwo_ref[...] += _dot_tn(m_ref[...], df)
        dmf = _dot_nt(df, wo_ref[...])
        for x, (o_ref, w_ref, do_ref, dw_ref) in enumerate(((oa_ref, wa_ref, doa_ref, dwa_ref),
                                                           (ob_ref, wb_ref, dob_ref, dwb_ref),
                                                           (om_ref, wm_ref, dom_ref, dwm_ref))):
            cols = slice(x * D, (x + 1) * D)
            gx = gt_ref[:, cols].astype(F32)
            w = w_ref[...]
            dpre = dmf * _dot(o_ref[...], w) * gx * (1.0 - gx)
            dgt_ref[:, cols] = dpre.astype(BF16)
            db_ref[:, cols] += jnp.sum(dpre, axis=0, keepdims=True)
            dp = (dmf * gx).astype(BF16)
            do_ref[...] = _dot_nt(dp, w).astype(BF16)
            dw_ref[...] += _dot_tn(dp, o_ref[...])

    def rows(width):
        return pl.BlockSpec((TM, width), lambda i: (i, 0))

    def kept(shape):
        return pl.BlockSpec(shape, lambda i: (0,) * len(shape))

    widths = [o.shape[1] for o, _ in branch]
    sums = [(1, D), (D, D), (1, 3 * D)] + [(D, k) for k in widths]
    return pl.pallas_call(
        body, name=name, grid=(T // TM,),
        in_specs=[rows(D), rows(D), _resident(g), _resident(w_out), rows(D), rows(3 * D)]
                 + [rows(k) for k in widths] + [_resident(w) for _, w in branch] + [UNREAD],
        out_specs=[kept(sums[0]), kept(sums[1]), rows(3 * D)] + [rows(k) for k in widths]
                  + [kept(shape) for shape in sums[2:]],
        out_shape=[jax.ShapeDtypeStruct(sums[0], F32), jax.ShapeDtypeStruct(sums[1], F32),
                   jax.ShapeDtypeStruct((T, 3 * D), BF16)] + [jax.ShapeDtypeStruct((T, k), BF16) for k in widths]
                  + [jax.ShapeDtypeStruct(shape, F32) for shape in sums[2:]],
        compiler_params=pltpu.CompilerParams(dimension_semantics=("arbitrary",), vmem_limit_bytes=VMEM_LIMIT_LARGE),
    )(dh, f, g, w_out, merged, gt, o_a, o_b, o_m, w_a, w_b, w_m, after)


def mm_tn(x, dy, tm, tn, name, shard_major=False, perm=None, slabs=1, after=None, wire=False):
    T, M = x.shape
    N = dy.shape[1]
    tk = min(2048, T)
    perm = perm or (lambda j: j)
    w = tn // slabs

    def body(x_ref, dy_ref, *rest):
        o_ref = rest[-2] if wire else rest[-1]

        @pl.when(pl.program_id(2) == 0)
        def _():
            o_ref[...] = jnp.zeros_like(o_ref)

        acc = _dot_tn(x_ref[...], dy_ref[...])
        if shard_major:
            for s in range(slabs):
                o_ref[s] += acc[:, s * w:(s + 1) * w]
        else:
            o_ref[...] += acc
        if wire:
            @pl.when(pl.program_id(2) == T // tk - 1)
            def _():
                rest[-1][...] = o_ref[...].astype(BF16)

    if shard_major:
        out_spec = pl.BlockSpec((slabs, tm, w), lambda i, j, k: (perm(j), i, 0))
        out_shape = jax.ShapeDtypeStruct((N // w, M, w), F32)
    else:
        out_spec = pl.BlockSpec((tm, tn), lambda i, j, k: (i, j))
        out_shape = jax.ShapeDtypeStruct((M, N), F32)
    return pl.pallas_call(
        body, name=name, grid=(M // tm, N // tn, T // tk),
        in_specs=[pl.BlockSpec((tk, tm), lambda i, j, k: (k, i)),
                  pl.BlockSpec((tk, tn), lambda i, j, k: (k, j))] + ([] if after is None else [UNREAD]),
        out_specs=[out_spec, out_spec] if wire else out_spec,
        out_shape=[out_shape, jax.ShapeDtypeStruct(out_shape.shape, BF16)] if wire else out_shape,
        compiler_params=_params("parallel", "parallel", "arbitrary"),
    )(x, dy, *([] if after is None else [after]))


def rope_tables(T, zero):
    half = HEAD // 2
    inv = ROPE_THETA ** (-jnp.arange(half, dtype=F32) / half)
    ang = (jnp.arange(T).astype(F32) + zero)[:, None] * inv[None, :]
    cos, sin = jnp.cos(ang), jnp.sin(ang)
    return jnp.concatenate([cos, cos], axis=1), jnp.concatenate([-sin, sin], axis=1)


def layer_step(x, mem, target, gains, sinks, b_gate, weights_of, send_grads, zero):
    T = x.shape[0]
    cos, sin_signed = rope_tables(T, zero)
    no_sink = jnp.full((2,), NEG_INF, F32)

    xn1 = rms_scale(x, gains["ffn1_norm_pre"], "ffn1_norm", cos)
    w = dict(weights_of("ffn1_in", xn1))
    _, gu1, a1 = ffn_in(x, gains["ffn1_norm_pre"], w["ffn1_w_in"], "ffn1_in", xn=xn1)
    w.update(weights_of("ffn1_out", a1))
    f1, h1 = mm_norm_res(a1, w["ffn1_w_out"], x, gains["ffn1_norm_post"], 0.5, "ffn1_out")
    w.update(weights_of("mix_in", f1))
    u, qkv, gt = mix_in(h1, gains["mix_norm_pre"], w["w_in"], w["w_gate"], b_gate, cos, sin_signed, "mix_in")
    w.update(weights_of("mix_rest", u))
    outs, lses = [], []
    for gidx, (window, dil) in enumerate(DIL):
        last = gidx == len(DIL) - 1
        o_g, l_g = band_fwd(qkv, no_sink, r=dil, base=A_BASE + 6 * gidx, hkv=2, grp=1, max_dist=window // dil,
                            out_dtype=BF16 if last else F32, name=f"attn_a{gidx}_fwd",
                            merge=(outs, lses) if last else None)
        outs.append(o_g)
        lses.append(l_g)
    o_a, l_a = outs[-1], lses[-1]
    o_b, l_b = band_fwd(qkv, sinks, r=1, base=B_BASE, hkv=2, grp=2, max_dist=HEAD - 1, out_dtype=BF16,
                        name="attn_b_fwd")
    mem_n, mkv = mem_kv(mem, gains["mem_norm"], w["w_mem_kv"], "mem_kv")
    o_m, l_m = mem_fwd(qkv, mkv, "attn_m_fwd")
    merged, mo, h2 = gate_merge_out(gt, o_a, o_b, o_m, w["w_o_a"], w["w_o_b"], w["w_o_m"], w["w_out"], h1,
                                    gains["mix_norm_post"], "gate_merge_out")
    w.update(weights_of("ffn2", mo))
    xn2, gu2, a2 = ffn_in(h2, gains["ffn2_norm_pre"], w["ffn2_w_in"], "ffn2_in")
    f2, dy, sq = mm_norm_res(a2, w["ffn2_w_out"], h2, gains["ffn2_norm_post"], 0.5, "ffn2_out", target=target)

    grads = {}

    def ffn_bwd(tag, dh_out, f, gu, a, xn, h_in, after):
        df, dgu, dh_in, grads[f"{tag}_norm_pre"], grads[f"{tag}_norm_post"] = ffn_tokens_bwd(
            dh_out, f, h_in, gu, gains[f"{tag}_norm_pre"], gains[f"{tag}_norm_post"], w[f"{tag}_w_in"],
            w[f"{tag}_w_out"], 0.5, f"{tag}_tokens_bwd", after)
        sent = send_grads(f"{tag}_in", {f"{tag}_w_in": mm_tn(
            xn, dgu, D_MODEL, FF_T, f"{tag}_w_in_grad", shard_major=True, perm=_ffn_perm, wire=True)})
        sent = send_grads(f"{tag}_out", {f"{tag}_w_out": mm_tn(
            a, df, FF_T, D_MODEL, f"{tag}_w_out_grad", after=sent, wire=True)})
        return dh_in, sent

    dh2, sent = ffn_bwd("ffn2", dy, f2, gu2, a2, xn2, h2, dy)

    mix = {}
    (grads["mix_norm_post"], mix["w_out"], dgt, do_a, do_b, do_m, grads["b_gate"],
     dwa_t, dwb_t, dwm_t) = gate_merge_out_bwd(
        dh2, mo, gains["mix_norm_post"], w["w_out"], merged, gt, o_a, o_b, o_m, w["w_o_a"], w["w_o_b"],
        w["w_o_m"], "gate_merge_out_bwd", sent)
    mix["w_o_a"], mix["w_o_b"], mix["w_o_m"] = dwa_t.T, dwb_t.T, dwm_t.T

    dqkv = lax.empty(qkv.shape, qkv.dtype)
    for gidx, (window, dil) in enumerate(DIL):
        dqkv, = band_bwd(qkv, dqkv, do_a, o_a, l_a, cos, sin_signed, None, r=dil, base=A_BASE + 6 * gidx, hkv=2,
                         grp=1, max_dist=window // dil, name=f"attn_a{gidx}_bwd")
    dqkv, dsink = band_bwd(qkv, dqkv, do_b, o_b, l_b, cos, sin_signed, sinks, r=1, base=B_BASE, hkv=2, grp=2,
                           max_dist=HEAD - 1, name="attn_b_bwd")
    grads["sinks"] = -dsink[:, ::8, 0].reshape(1, 4)
    dqkv, dmk, dmv = mem_bwd(qkv, dqkv, mkv, do_m, o_m, l_m, "attn_m_bwd")
    mix["w_mem_kv"], grads["mem_norm"] = mem_kv_bwd(
        mem, gains["mem_norm"], mem_n, w["w_mem_kv"], jnp.concatenate([dmk, dmv], axis=1), "mem_kv_bwd")

    mix["w_in"] = mm_tn(u, dqkv, D_MODEL, 1280, "w_in_grad")
    mix["w_gate"] = mm_tn(u, dgt, D_MODEL, 1536, "w_gate_grad", shard_major=True, slabs=2, wire=True)
    sent = send_grads("mix", mix)
    dh1, grads["mix_norm_pre"] = mm_nt_norm_bwd(
        [(dqkv, w["w_in"]), (dgt, w["w_gate"])], h1, dh2, gains["mix_norm_pre"], "mix_in_bwd", sent)

    dx, _ = ffn_bwd("ffn1", dh1, f1, gu1, a1, xn1, x, dh1)
    return sq, dx, grads


def _place():
    return lax.axis_index("x"), lax.axis_index("y"), lax.axis_index("c")


def _other_chips(x, y):
    return [(1 - x, y), (x, 1 - y), (1 - x, 1 - y)]


def _hbm(n):
    return [pl.BlockSpec(memory_space=pltpu.HBM)] * n


SEM = pl.BlockSpec(memory_space=pltpu.SEMAPHORE)
SIDE_EFFECT = pltpu.SideEffectType.DATAFLOW_SIDE_EFFECTING


def _chip_copy(src, land, sems, i, j, dst_slot, scatter):
    x, y, c = _place()
    px, py = _other_chips(x, y)[j]
    send_sems, recv_sems = sems
    return pltpu.make_async_remote_copy(
        src_ref=src[i].at[2 * px + py] if scatter else src[i], dst_ref=land[i].at[dst_slot],
        send_sem=send_sems.at[3 * i + j], recv_sem=recv_sems.at[3 * i + j],
        device_id=(px, py, c), device_id_type=MESH)


def chip_copies_start(srcs, lands, groups, scatter, name, after=None):
    n = len(srcs)

    def body(*refs):
        src, land = refs[:n], refs[n:2 * n]
        first_sem = 2 * n + (after is not None)
        sems = refs[first_sem:first_sem + 2 * len(groups)]
        token = refs[-1]
        x, y, _ = _place()
        for g, members in enumerate(groups):
            part = ([src[i] for i in members], [land[i] for i in members])
            for t in range(len(members)):
                for j in range(3):
                    _chip_copy(*part, sems[2 * g:2 * g + 2], t, j, 2 * x + y, scatter).start()
        token[...] = jnp.zeros_like(token)

    sem_shapes = [pltpu.SemaphoreType.DMA((3 * len(m),)) for m in groups for _ in range(2)]
    thru = [pltpu.HBM(a.shape, a.dtype) for a in (*srcs, *lands)]
    res = pl.pallas_call(
        body, name=name,
        out_shape=(*sem_shapes, *thru, jax.ShapeDtypeStruct((8, 128), F32)),
        in_specs=_hbm(2 * n) + ([] if after is None else [UNREAD]),
        out_specs=(*[SEM] * len(sem_shapes), *_hbm(2 * n), pl.BlockSpec(memory_space=pltpu.VMEM)),
        input_output_aliases={i: len(sem_shapes) + i for i in range(2 * n)},
        compiler_params=pltpu.CompilerParams(has_side_effects=SIDE_EFFECT),
    )(*[pltpu.with_memory_space_constraint(a, pltpu.HBM) for a in (*srcs, *lands)],
      *([] if after is None else [after]))
    k = len(sem_shapes)
    sems = [tuple(res[2 * g:2 * g + 2]) for g in range(len(groups))]
    return sems, list(res[k:k + n]), list(res[k + n:k + 2 * n]), res[-1]


def chip_copies_wait(srcs, lands, sems, after, scatter, name):
    n = len(srcs)
    after = list(after) if isinstance(after, (list, tuple)) else [after]

    def body(*refs):
        src, land = refs[:n], refs[n:2 * n]
        pair = refs[2 * n:2 * n + 2]
        x, y, _ = _place()
        for i in range(n):
            for j, (px, py) in enumerate(_other_chips(x, y)):
                copy = _chip_copy(src, land, pair, i, j, 2 * px + py, scatter)
                copy.wait_send()
                copy.wait_recv()

    res = pl.pallas_call(
        body, name=name,
        out_shape=[pltpu.HBM(a.shape, a.dtype) for a in (*srcs, *lands)],
        in_specs=[*_hbm(2 * n), SEM, SEM] + [UNREAD] * len(after),
        out_specs=_hbm(2 * n),
        input_output_aliases={i: i for i in range(2 * n)},
        compiler_params=pltpu.CompilerParams(has_side_effects=SIDE_EFFECT),
    )(*srcs, *lands, *sems, *after)
    return list(res[n:])


def small_all_gather(small, name):
    flips = [(fx, fy, fc) for fx in (0, 1) for fy in (0, 1) for fc in (0, 1)][1:]

    def body(in_ref, out_ref, send_sems, recv_sems, local_sem):
        x, y, c = _place()
        me = 4 * x + 2 * y + c

        def copy(k, slot):
            fx, fy, fc = flips[k]
            return pltpu.make_async_remote_copy(
                src_ref=in_ref, dst_ref=out_ref.at[slot], send_sem=send_sems.at[k], recv_sem=recv_sems.at[k],
                device_id=(x ^ fx, y ^ fy, c ^ fc), device_id_type=MESH)

        local = pltpu.make_async_copy(in_ref, out_ref.at[me], local_sem)
        local.start()
        for k in range(len(flips)):
            copy(k, me).start()
        for k, (fx, fy, fc) in enumerate(flips):
            copy(k, 4 * (x ^ fx) + 2 * (y ^ fy) + (c ^ fc)).wait()
        local.wait()

    return pl.pallas_call(
        body, name=name, in_specs=_hbm(1), out_specs=_hbm(1)[0],
        out_shape=jax.ShapeDtypeStruct((N_DEV,) + small.shape, small.dtype),
        scratch_shapes=[pltpu.SemaphoreType.DMA((len(flips),)), pltpu.SemaphoreType.DMA((len(flips),)),
                        pltpu.SemaphoreType.DMA],
    )(small)


def _sibling_copy(src, land, sems, i):
    x, y, c = _place()
    return pltpu.make_async_remote_copy(
        src_ref=src[i], dst_ref=land[i], send_sem=sems[0].at[i], recv_sem=sems[1].at[i],
        device_id=(x, y, 1 - c), device_id_type=MESH)


def sibling_copies_start(parts, name):
    n = len(parts)
    lands = [lax.empty(p.shape, p.dtype) for p in parts]

    def body(*refs):
        src, land, sems, token = refs[:n], refs[n:2 * n], refs[2 * n:2 * n + 2], refs[-1]
        for i in range(n):
            _sibling_copy(src, land, sems, i).start()
        token[...] = jnp.zeros_like(token)

    res = pl.pallas_call(
        body, name=name,
        out_shape=(pltpu.SemaphoreType.DMA((n,)), pltpu.SemaphoreType.DMA((n,)),
                   *[pltpu.HBM(a.shape, a.dtype) for a in (*parts, *lands)], jax.ShapeDtypeStruct((8, 128), F32)),
        in_specs=_hbm(2 * n),
        out_specs=(SEM, SEM, *_hbm(2 * n), pl.BlockSpec(memory_space=pltpu.VMEM)),
        input_output_aliases={i: 2 + i for i in range(2 * n)},
        compiler_params=pltpu.CompilerParams(has_side_effects=SIDE_EFFECT),
    )(*[pltpu.with_memory_space_constraint(a, pltpu.HBM) for a in (*parts, *lands)])
    return tuple(res[:2]), list(res[2:2 + n]), list(res[2 + n:2 + 2 * n]), res[-1]


def sibling_copies_wait(parts, lands, sems, after, name):
    n = len(parts)

    def body(*refs):
        src, land, sems = refs[:n], refs[n:2 * n], refs[2 * n:2 * n + 2]
        for i in range(n):
            copy = _sibling_copy(src, land, sems, i)
            copy.wait_send()
            copy.wait_recv()

    res = pl.pallas_call(
        body, name=name,
        out_shape=[pltpu.HBM(a.shape, a.dtype) for a in (*parts, *lands)],
        in_specs=[*_hbm(2 * n), SEM, SEM, UNREAD],
        out_specs=_hbm(2 * n),
        input_output_aliases={i: i for i in range(2 * n)},
        compiler_params=pltpu.CompilerParams(has_side_effects=SIDE_EFFECT),
    )(*parts, *lands, *sems, after)
    return list(res[n:])


def _row_tile(rows):
    for t in (256, 176, 128, 64, 32, 16, 8):
        if rows % t == 0:
            return t
    return rows


def chip_partial_sum(me, own_sm, recv, name):
    _, rows, cols = own_sm.shape
    tr = _row_tile(rows)

    def body(me_ref, own_ref, r1, r2, r3, o_ref):
        o_ref[...] = own_ref[...] + r1[...].astype(F32) + r2[...].astype(F32) + r3[...].astype(F32)

    def slot(d):
        return pl.BlockSpec((None, tr, cols), lambda i, me_ref: ((me_ref[0] + d) % N_CHIPS, i, 0))

    return pl.pallas_call(
        body, name=name,
        grid_spec=pltpu.PrefetchScalarGridSpec(
            num_scalar_prefetch=1, grid=(rows // tr,),
            in_specs=[slot(0), slot(1), slot(2), slot(3)],
            out_specs=pl.BlockSpec((tr, cols), lambda i, me_ref: (i, 0))),
        out_shape=jax.ShapeDtypeStruct((rows, cols), F32),
        compiler_params=_params("parallel"),
    )(me, own_sm, recv, recv, recv)


def _adamw(w, g, m, v):
    m = ADAM_B1 * m + (1.0 - ADAM_B1) * g
    v = ADAM_B2 * v + (1.0 - ADAM_B2) * (g * g)
    m_hat = m / (1.0 - ADAM_B1 ** ADAM_STEP)
    v_hat = v / (1.0 - ADAM_B2 ** ADAM_STEP)
    delta = -ADAM_LR * (m_hat / (jnp.sqrt(v_hat) + ADAM_EPS) + ADAM_WD * w)
    return delta, m, v


def adamw_pair(part, sib, w, m, v, name):
    rows, cols = w.shape
    tr = _row_tile(rows)

    def body(p_ref, s_ref, w_ref, m_ref, v_ref, g_ref, d_ref, nm_ref, nv_ref):
        g = p_ref[...] + s_ref[...]
        g_ref[...] = g
        d_ref[...], nm_ref[...], nv_ref[...] = _adamw(w_ref[...], g, m_ref[...], v_ref[...])

    spec = pl.BlockSpec((tr, cols), lambda i: (i, 0))
    return pl.pallas_call(
        body, name=name, grid=(rows // tr,), in_specs=[spec] * 5, out_specs=[spec] * 4,
        out_shape=[jax.ShapeDtypeStruct((rows, cols), F32)] * 4,
        compiler_params=_params("parallel"),
    )(part, sib, w, m, v)


def adamw_small(g_all, w, m, v, name):
    def body(ga_ref, w_ref, m_ref, v_ref, g_ref, d_ref, nm_ref, nv_ref):
        g = ga_ref[0]
        for k in range(1, N_DEV):
            g = g + ga_ref[k]
        g_ref[...] = g
        d_ref[...], nm_ref[...], nv_ref[...] = _adamw(w_ref[...], g, m_ref[...], v_ref[...])

    return pl.pallas_call(
        body, name=name, out_shape=[jax.ShapeDtypeStruct(w.shape, F32)] * 4,
    )(g_all, w, m, v)


WEIGHTS = ("ffn1_norm_pre", "ffn1_w_in", "ffn1_w_out", "ffn1_norm_post", "mix_norm_pre", "w_in", "sinks",
           "mem_norm", "w_mem_kv", "w_gate", "b_gate", "w_o_a", "w_o_b", "w_o_m", "w_out", "mix_norm_post",
           "ffn2_norm_pre", "ffn2_w_in", "ffn2_w_out", "ffn2_norm_post")
GATHER_STAGES = (("ffn1_in", "ffn1_out"), ("mix_in",), ("mix_rest", "ffn2"))
GATHER_GROUPS = {"ffn1_in": ("ffn1_w_in",), "ffn1_out": ("ffn1_w_out",),
                 "mix_in": ("w_in", "w_gate"), "mix_rest": ("w_mem_kv", "w_o_a", "w_o_b", "w_o_m", "w_out"),
                 "ffn2": ("ffn2_w_in", "ffn2_w_out")}
GROUPS = {"ffn1_in": ("ffn1_w_in",), "ffn1_out": ("ffn1_w_out",),
          "mix": ("w_in", "w_gate", "w_mem_kv", "w_o_a", "w_o_b", "w_o_m", "w_out"),
          "ffn2_in": ("ffn2_w_in",), "ffn2_out": ("ffn2_w_out",)}
COLUMN_SHARDED = ("ffn1_w_in", "ffn2_w_in", "w_in", "w_gate", "w_o_a", "w_o_b", "w_o_m")
KEPT_SHARD_MAJOR = ("ffn1_w_in", "ffn2_w_in", "w_gate")
GAINS = ("ffn1_norm_pre", "ffn1_norm_post", "mix_norm_pre", "mem_norm", "mix_norm_post", "ffn2_norm_pre",
         "ffn2_norm_post")
SMALL_ROWS = 16


def _pack_small(t):
    sinks = jnp.pad(t["sinks"], ((0, 0), (0, D_MODEL - t["sinks"].shape[1])))
    rows = [t[k] for k in GAINS] + [t["b_gate"].reshape(3, D_MODEL), sinks]
    packed = jnp.concatenate(rows, axis=0)
    return jnp.pad(packed, ((0, SMALL_ROWS - packed.shape[0]), (0, 0)))


def _unpack_small(p):
    out = {k: p[i:i + 1] for i, k in enumerate(GAINS)}
    out["b_gate"] = p[7:10].reshape(1, 3 * D_MODEL)
    out["sinks"] = p[10:11, :4]
    return out


def kernel(x, mem, ffn1_norm_pre, ffn1_w_in, ffn1_w_out, ffn1_norm_post, mix_norm_pre, w_in, sinks, mem_norm, w_mem_kv, w_gate, b_gate, w_o_a, w_o_b, w_o_m, w_out, mix_norm_post, ffn2_norm_pre, ffn2_w_in, ffn2_w_out, ffn2_norm_post, loss_target, m_ffn1_norm_pre, m_ffn1_w_in, m_ffn1_w_out, m_ffn1_norm_post, m_mix_norm_pre, m_w_in, m_sinks, m_mem_norm, m_w_mem_kv, m_w_gate, m_b_gate, m_w_o_a, m_w_o_b, m_w_o_m, m_w_out, m_mix_norm_post, m_ffn2_norm_pre, m_ffn2_w_in, m_ffn2_w_out, m_ffn2_norm_post, v_ffn1_norm_pre, v_ffn1_w_in, v_ffn1_w_out, v_ffn1_norm_post, v_mix_norm_pre, v_w_in, v_sinks, v_mem_norm, v_w_mem_kv, v_w_gate, v_b_gate, v_w_o_a, v_w_o_b, v_w_o_m, v_w_out, v_mix_norm_post, v_ffn2_norm_pre, v_ffn2_w_in, v_ffn2_w_out, v_ffn2_norm_post):
    given = dict(locals())
    wt = {k: given[k] for k in WEIGHTS}
    mom = {k: given["m_" + k] for k in WEIGHTS}
    var = {k: given["v_" + k] for k in WEIGHTS}
    chip = (2 * lax.axis_index("x") + lax.axis_index("y")).astype(jnp.int32)
    me = chip.reshape(1)

    def landing_zone(own):
        return lax.dynamic_update_slice_in_dim(lax.empty((N_CHIPS,) + own.shape, own.dtype), own[None], chip, 0)

    started = {}
    tokens = []

    def stage_keys(stage):
        return [k for g in GATHER_STAGES[stage] for k in GATHER_GROUPS[g]]

    def prepare(stage):
        shards = [(wt[k][0] + tokens[0][0, 0] if tokens else wt[k][0]).astype(BF16) for k in stage_keys(stage)]
        return shards, [landing_zone(s) for s in shards]

    def start_gather(stage, after):
        groups, keys = GATHER_STAGES[stage], stage_keys(stage)
        members = [[keys.index(k) for k in GATHER_GROUPS[g]] for g in groups]
        sems, shards, lands, token = chip_copies_start(
            *prepared[stage], members, False, f"weight_gather_start_{stage}", after)
        tokens.append(token)
        for g, idx, pair in zip(groups, members, sems):
            started[g] = ([shards[i] for i in idx], [lands[i] for i in idx], pair)

    prepared = {0: prepare(0)}
    start_gather(0, None)
    prepared.update({stage: prepare(stage) for stage in range(1, len(GATHER_STAGES))})

    def weights_of(group, after):
        if group == GATHER_STAGES[0][0]:
            after = [after] + [a for stage in range(1, len(GATHER_STAGES)) for part in prepared[stage] for a in part]
        got = chip_copies_wait(*started[group], after, False, f"weight_gather_wait_{group}")
        stage = [s + 1 for s, groups in enumerate(GATHER_STAGES[:-1]) if groups[0] == group]
        if stage:
            start_gather(stage[0], got[0])
        full = {}
        for k, g in zip(GATHER_GROUPS[group], got):
            if k in COLUMN_SHARDED:
                if k in ("ffn1_w_in", "ffn2_w_in"):
                    g = jnp.stack([g[0], g[2], g[1], g[3]])
                full[k] = jnp.swapaxes(g, 0, 1).reshape(g.shape[1], N_CHIPS * g.shape[2])
                if k == "w_in":
                    full[k] = to_kernel_heads(full[k])
            else:
                full[k] = g.reshape(N_CHIPS * g.shape[1], g.shape[2])
        return full

    in_flight = {}

    def send_grads(group, grads):
        def shard_major(k, g):
            if k in KEPT_SHARD_MAJOR:
                return g
            if k in COLUMN_SHARDED:
                return jnp.swapaxes(g.reshape(g.shape[0], N_CHIPS, g.shape[1] // N_CHIPS), 0, 1)
            return g.reshape(N_CHIPS, g.shape[0] // N_CHIPS, g.shape[1])

        own, wire = [], []
        for k in GROUPS[group]:
            g, rounded = grads[k] if isinstance(grads[k], (tuple, list)) else (grads[k], None)
            g = shard_major(k, from_kernel_heads(g) if k == "w_in" else g)
            own.append(g)
            wire.append(g.astype(BF16) if rounded is None else shard_major(k, rounded))
        zones = [lax.empty(b.shape, b.dtype) for b in wire]
        pair, wire, zones, sent = chip_copies_start(
            wire, zones, [list(range(len(wire)))], True, f"grad_scatter_start_{group}")
        in_flight[group] = (own, wire, zones, pair[0], sent)
        return sent

    gains = {k: wt[k] for k in GAINS}
    sq, dx, grads = layer_step(
        x[0], mem[0], loss_target[0], gains, sinks[0], b_gate, weights_of, send_grads, tokens[0][0, 0])
    loss = lax.psum(0.5 * jnp.sum(sq) / D_MODEL, ("x", "y", "c"))

    res = {}
    after = in_flight["ffn1_out"][4]
    swaps = []
    for stage in (("ffn2_in", "ffn2_out", "mix", "ffn1_in"), ("ffn1_out",)):
        names, parts = [], []
        for group in stage:
            own, wire, zones, pair, _ = in_flight[group]
            received = chip_copies_wait(wire, zones, pair, after, True, f"grad_scatter_wait_{group}")
            for k, g, r in zip(GROUPS[group], own, received):
                names.append(k)
                parts.append(chip_partial_sum(me, g, r, f"{k}_chip_sum"))
        pair, parts, lands, after = sibling_copies_start(parts, f"sibling_start_{stage[-1]}")
        swaps.append((stage[-1], names, parts, lands, pair))
    small_all = small_all_gather(_pack_small(grads), "small_grad_gather")
    packed = adamw_small(small_all, _pack_small(wt), _pack_small(mom), _pack_small(var), "small_adamw")
    after = packed[0]
    for tag, names, parts, lands, pair in swaps:
        sibs = sibling_copies_wait(parts, lands, pair, after, f"sibling_wait_{tag}")
        for k, p, s in zip(names, parts, sibs):
            res[k] = [t[None] for t in adamw_pair(p, s, wt[k][0], mom[k][0], var[k][0], f"{k}_adamw")]
        after = res[names[-1]][0]
    for idx, p in enumerate(packed):
        for k, t in _unpack_small(p).items():
            res.setdefault(k, [None] * 4)[idx] = t

    return (loss, dx[None], *[res[k][0] for k in WEIGHTS], *[res[k][1] for k in WEIGHTS],
            *[res[k][2] for k in WEIGHTS], *[res[k][3] for k in WEIGHTS])
```
